```python
import functools
import jax, jax.numpy as jnp
from jax import lax
import numpy as np

D_MODEL = 2048
BATCH = 8
SEQ = 2048
DEPTH = 2

MIX_WIDTH = D_MODEL
HEAD_DIM = 128
Q_BLOCK = 128
SB_WIDTH = MIX_WIDTH // 2
SB_HEADS = SB_WIDTH // HEAD_DIM
SC_WIDTH = MIX_WIDTH - SB_WIDTH
SC_GROUPS = SC_WIDTH // HEAD_DIM
CONV_WIDTH = 3
CHUNK = 128
SG_WIDTH = MIX_WIDTH // 2
SG_GROUP_DIM = 128
SG_GROUPS = SG_WIDTH // SG_GROUP_DIM
FOX_WIDTH = MIX_WIDTH - SG_WIDTH
FOX_HEADS = FOX_WIDTH // HEAD_DIM
IN_AB = 3 * SB_WIDTH + 3 * SC_WIDTH
IN_CD = 2 * SG_WIDTH + 3 * FOX_WIDTH + FOX_HEADS
D_FF = 5632
EPS = 1e-6

kernel_name = "hybrid_stickbreak_shortconv_chunkgmlp_fox_block"


def rmsnorm(x, g):
    xf = x.astype(jnp.float32)
    y = xf * lax.rsqrt(jnp.mean(xf * xf, axis=-1, keepdims=True) + EPS)
    return (y * g.astype(jnp.float32)).astype(x.dtype)


def layernorm(x, g):
    xf = x.astype(jnp.float32)
    mu = jnp.mean(xf, axis=-1, keepdims=True)
    xc = xf - mu
    y = xc * lax.rsqrt(jnp.mean(xc * xc, axis=-1, keepdims=True) + EPS)
    return (y * g.astype(jnp.float32)).astype(x.dtype)


def causal_dwconv(x, w):
    K = w.shape[0]
    S = x.shape[1]
    xp = jnp.pad(x, ((0, 0), (K - 1, 0), (0, 0)))
    y = xp[:, 0:S] * w[0]
    for j in range(1, K):
        y = y + xp[:, j:j + S] * w[j]
    return y


def split_heads(t, n_heads):
    return t.reshape(t.shape[0], t.shape[1], n_heads, -1)


def stick_breaking_attention(q, k, v):
    S = q.shape[1]
    scale = HEAD_DIM ** -0.5
    outs = []
    for i in range(S // Q_BLOCK):
        q0 = i * Q_BLOCK
        kend = q0 + Q_BLOCK
        qb = q[:, q0:kend].astype(jnp.float32)
        kb = k[:, :kend].astype(jnp.float32)
        vb = v[:, :kend].astype(jnp.float32)
        z = jnp.einsum('bqhd,bkhd->bhqk', qb, kb) * scale
        t_idx = q0 + jnp.arange(Q_BLOCK)[:, None]
        s_idx = jnp.arange(kend)[None, :]
        mask = s_idx < t_idx
        log_1mb = jnp.where(mask, jax.nn.log_sigmoid(-z), 0.0)
        later = lax.cumsum(log_1mb, axis=3, reverse=True) - log_1mb
        a = jnp.where(mask, jnp.exp(jax.nn.log_sigmoid(z) + later), 0.0)
        outs.append(jnp.einsum('bhqk,bkhd->bqhd', a, vb))
    return jnp.concatenate(outs, axis=1).astype(q.dtype)


def forgetting_attention(q, k, v, log_f):
    S = q.shape[1]
    scale = HEAD_DIM ** -0.5
    c = jnp.cumsum(log_f, axis=1).transpose(0, 2, 1)
    outs = []
    for i in range(S // Q_BLOCK):
        q0 = i * Q_BLOCK
        kend = q0 + Q_BLOCK
        qb = q[:, q0:kend].astype(jnp.float32)
        kb = k[:, :kend].astype(jnp.float32)
        vb = v[:, :kend].astype(jnp.float32)
        logits = jnp.einsum('bqhd,bkhd->bhqk', qb, kb) * scale
        logits = logits + c[:, :, q0:kend, None] - c[:, :, None, :kend]
        t_idx = q0 + jnp.arange(Q_BLOCK)[:, None]
        s_idx = jnp.arange(kend)[None, :]
        p = jax.nn.softmax(jnp.where(s_idx <= t_idx, logits, -jnp.inf), axis=-1)
        outs.append(jnp.einsum('bhqk,bkhd->bqhd', p, vb))
    return jnp.concatenate(outs, axis=1).astype(q.dtype)


def chunked_spatial_gate(u, v, w_s, b_s, g):
    B, S, W = v.shape
    v = layernorm(v, g)
    vc = v.reshape(B, S // CHUNK, CHUNK, SG_GROUPS, SG_GROUP_DIM)
    w = w_s * jnp.tril(jnp.ones((CHUNK, CHUNK), w_s.dtype))
    mixed = jnp.einsum('gts,bnsgc->bntgc', w, vc) + b_s.T[None, None, :, :, None]
    return u * mixed.reshape(B, S, W)


def mixer_ab(h, w_in, sc_conv_w, w_out):
    B, S, _ = h.shape
    p = h @ w_in
    q, k, v, gate_b, gate_c, hin = jnp.split(
        p, [SB_WIDTH, 2 * SB_WIDTH, 3 * SB_WIDTH,
            3 * SB_WIDTH + SC_WIDTH, 3 * SB_WIDTH + 2 * SC_WIDTH], axis=-1)
    a_out = stick_breaking_attention(split_heads(q, SB_HEADS), split_heads(k, SB_HEADS),
                                     split_heads(v, SB_HEADS)).reshape(B, S, SB_WIDTH)
    b_out = gate_b * causal_dwconv(gate_c * hin, sc_conv_w)
    return jnp.concatenate([a_out, b_out], axis=-1) @ w_out


def mixer_cd(h, w_in, fox_b_f, sg_w, sg_b, sg_norm_g, w_out):
    B, S, _ = h.shape
    p = h @ w_in
    u, v, q, k, vv, f = jnp.split(
        p, [SG_WIDTH, 2 * SG_WIDTH, 2 * SG_WIDTH + FOX_WIDTH,
            2 * SG_WIDTH + 2 * FOX_WIDTH, 2 * SG_WIDTH + 3 * FOX_WIDTH], axis=-1)
    c_out = chunked_spatial_gate(jax.nn.gelu(u), jax.nn.gelu(v), sg_w, sg_b, sg_norm_g)
    log_f = jax.nn.log_sigmoid(f.astype(jnp.float32) + fox_b_f.astype(jnp.float32))
    d_out = forgetting_attention(split_heads(q, FOX_HEADS), split_heads(k, FOX_HEADS),
                                 split_heads(vv, FOX_HEADS), log_f).reshape(B, S, FOX_WIDTH)
    return jnp.concatenate([c_out, d_out], axis=-1) @ w_out


def conv_ffn(h, w_up, conv_w, w_down):
    a = causal_dwconv(h @ w_up, conv_w)
    gate, up = jnp.split(a, 2, axis=-1)
    return (jax.nn.silu(gate) * up) @ w_down


def _fwd_setup_inputs(seed: int = 0) -> dict:
    key = jax.random.key(seed)
    ks = iter(jax.random.split(key, 32))
    f32 = jnp.float32

    def w(shape, fan_in):
        return jax.random.normal(next(ks), shape, f32) * (fan_in ** -0.5)

    def gain(n):
        return 1.0 + 0.02 * jax.random.normal(next(ks), (n,), f32)

    inp = {}
    inp["x"] = jax.random.normal(next(ks), (BATCH, SEQ, D_MODEL), f32)
    inp["l0_mix_norm_g"] = gain(D_MODEL)
    inp["l0_w_in"] = w((D_MODEL, IN_AB), D_MODEL)
    inp["l0_sc_conv_w"] = w((CONV_WIDTH, SC_WIDTH), CONV_WIDTH)
    inp["l0_w_out"] = w((MIX_WIDTH, D_MODEL), MIX_WIDTH)
    inp["l0_ffn_norm_g"] = gain(D_MODEL)
    inp["l0_ffn_up"] = w((D_MODEL, 2 * D_FF), D_MODEL)
    inp["l0_ffn_conv_w"] = w((CONV_WIDTH, 2 * D_FF), CONV_WIDTH)
    inp["l0_ffn_down"] = w((D_FF, D_MODEL), D_FF)
    inp["l1_mix_norm_g"] = gain(D_MODEL)
    inp["l1_w_in"] = w((D_MODEL, IN_CD), D_MODEL)
    inp["l1_fox_b_f"] = 1.0 + 0.5 * jax.random.normal(next(ks), (FOX_HEADS,), f32)
    inp["l1_sg_w"] = w((SG_GROUPS, CHUNK, CHUNK), CHUNK)
    inp["l1_sg_b"] = 1.0 + 0.1 * jax.random.normal(next(ks), (SG_GROUPS, CHUNK), f32)
    inp["l1_sg_norm_g"] = gain(SG_WIDTH)
    inp["l1_w_out"] = w((MIX_WIDTH, D_MODEL), MIX_WIDTH)
    inp["l1_ffn_norm_g"] = gain(D_MODEL)
    inp["l1_ffn_up"] = w((D_MODEL, 2 * D_FF), D_MODEL)
    inp["l1_ffn_conv_w"] = w((CONV_WIDTH, 2 * D_FF), CONV_WIDTH)
    inp["l1_ffn_down"] = w((D_FF, D_MODEL), D_FF)
    inp["final_norm_g"] = gain(D_MODEL)
    return inp


def _fwd_reference(x, l0_mix_norm_g, l0_w_in, l0_sc_conv_w, l0_w_out, l0_ffn_norm_g,
              l0_ffn_up, l0_ffn_conv_w, l0_ffn_down,
              l1_mix_norm_g, l1_w_in, l1_fox_b_f, l1_sg_w, l1_sg_b, l1_sg_norm_g,
              l1_w_out, l1_ffn_norm_g, l1_ffn_up, l1_ffn_conv_w, l1_ffn_down,
              final_norm_g):
    layers = [
        (l0_mix_norm_g,
         functools.partial(mixer_ab, w_in=l0_w_in, sc_conv_w=l0_sc_conv_w, w_out=l0_w_out),
         l0_ffn_norm_g, l0_ffn_up, l0_ffn_conv_w, l0_ffn_down),
        (l1_mix_norm_g,
         functools.partial(mixer_cd, w_in=l1_w_in, fox_b_f=l1_fox_b_f, sg_w=l1_sg_w,
                           sg_b=l1_sg_b, sg_norm_g=l1_sg_norm_g, w_out=l1_w_out),
         l1_ffn_norm_g, l1_ffn_up, l1_ffn_conv_w, l1_ffn_down),
    ]
    for i in range(DEPTH):
        mix_g, mixer, ffn_g, w_up, conv_w, w_down = layers[i]
        x = x + mixer(rmsnorm(x, mix_g))
        x = x + conv_ffn(rmsnorm(x, ffn_g), w_up, conv_w, w_down)
    return rmsnorm(x, final_norm_g)


import jax as _jax
import jax.numpy as _jnp

TWIN_FORMAT = 'train_step'
FWD_PARAMS = ['x', 'l0_mix_norm_g', 'l0_w_in', 'l0_sc_conv_w', 'l0_w_out', 'l0_ffn_norm_g', 'l0_ffn_up', 'l0_ffn_conv_w', 'l0_ffn_down', 'l1_mix_norm_g', 'l1_w_in', 'l1_fox_b_f', 'l1_sg_w', 'l1_sg_b', 'l1_sg_norm_g', 'l1_w_out', 'l1_ffn_norm_g', 'l1_ffn_up', 'l1_ffn_conv_w', 'l1_ffn_down', 'final_norm_g']
TWIN_WEIGHTS = ['l0_mix_norm_g', 'l0_w_in', 'l0_sc_conv_w', 'l0_w_out', 'l0_ffn_norm_g', 'l0_ffn_up', 'l0_ffn_conv_w', 'l0_ffn_down', 'l1_mix_norm_g', 'l1_w_in', 'l1_fox_b_f', 'l1_sg_w', 'l1_sg_b', 'l1_sg_norm_g', 'l1_w_out', 'l1_ffn_norm_g', 'l1_ffn_up', 'l1_ffn_conv_w', 'l1_ffn_down', 'final_norm_g']
TWIN_DIFF_INPUT = 'x'
TWIN_INPUTS = ['x', 'l0_mix_norm_g', 'l0_w_in', 'l0_sc_conv_w', 'l0_w_out', 'l0_ffn_norm_g', 'l0_ffn_up', 'l0_ffn_conv_w', 'l0_ffn_down', 'l1_mix_norm_g', 'l1_w_in', 'l1_fox_b_f', 'l1_sg_w', 'l1_sg_b', 'l1_sg_norm_g', 'l1_w_out', 'l1_ffn_norm_g', 'l1_ffn_up', 'l1_ffn_conv_w', 'l1_ffn_down', 'final_norm_g', 'loss_target', 'm_l0_mix_norm_g', 'm_l0_w_in', 'm_l0_sc_conv_w', 'm_l0_w_out', 'm_l0_ffn_norm_g', 'm_l0_ffn_up', 'm_l0_ffn_conv_w', 'm_l0_ffn_down', 'm_l1_mix_norm_g', 'm_l1_w_in', 'm_l1_fox_b_f', 'm_l1_sg_w', 'm_l1_sg_b', 'm_l1_sg_norm_g', 'm_l1_w_out', 'm_l1_ffn_norm_g', 'm_l1_ffn_up', 'm_l1_ffn_conv_w', 'm_l1_ffn_down', 'm_final_norm_g', 'v_l0_mix_norm_g', 'v_l0_w_in', 'v_l0_sc_conv_w', 'v_l0_w_out', 'v_l0_ffn_norm_g', 'v_l0_ffn_up', 'v_l0_ffn_conv_w', 'v_l0_ffn_down', 'v_l1_mix_norm_g', 'v_l1_w_in', 'v_l1_fox_b_f', 'v_l1_sg_w', 'v_l1_sg_b', 'v_l1_sg_norm_g', 'v_l1_w_out', 'v_l1_ffn_norm_g', 'v_l1_ffn_up', 'v_l1_ffn_conv_w', 'v_l1_ffn_down', 'v_final_norm_g']
TWIN_OUTPUTS = ['loss', 'grad_x', 'grad_l0_mix_norm_g', 'grad_l0_w_in', 'grad_l0_sc_conv_w', 'grad_l0_w_out', 'grad_l0_ffn_norm_g', 'grad_l0_ffn_up', 'grad_l0_ffn_conv_w', 'grad_l0_ffn_down', 'grad_l1_mix_norm_g', 'grad_l1_w_in', 'grad_l1_fox_b_f', 'grad_l1_sg_w', 'grad_l1_sg_b', 'grad_l1_sg_norm_g', 'grad_l1_w_out', 'grad_l1_ffn_norm_g', 'grad_l1_ffn_up', 'grad_l1_ffn_conv_w', 'grad_l1_ffn_down', 'grad_final_norm_g', 'delta_l0_mix_norm_g', 'delta_l0_w_in', 'delta_l0_sc_conv_w', 'delta_l0_w_out', 'delta_l0_ffn_norm_g', 'delta_l0_ffn_up', 'delta_l0_ffn_conv_w', 'delta_l0_ffn_down', 'delta_l1_mix_norm_g', 'delta_l1_w_in', 'delta_l1_fox_b_f', 'delta_l1_sg_w', 'delta_l1_sg_b', 'delta_l1_sg_norm_g', 'delta_l1_w_out', 'delta_l1_ffn_norm_g', 'delta_l1_ffn_up', 'delta_l1_ffn_conv_w', 'delta_l1_ffn_down', 'delta_final_norm_g', 'new_m_l0_mix_norm_g', 'new_m_l0_w_in', 'new_m_l0_sc_conv_w', 'new_m_l0_w_out', 'new_m_l0_ffn_norm_g', 'new_m_l0_ffn_up', 'new_m_l0_ffn_conv_w', 'new_m_l0_ffn_down', 'new_m_l1_mix_norm_g', 'new_m_l1_w_in', 'new_m_l1_fox_b_f', 'new_m_l1_sg_w', 'new_m_l1_sg_b', 'new_m_l1_sg_norm_g', 'new_m_l1_w_out', 'new_m_l1_ffn_norm_g', 'new_m_l1_ffn_up', 'new_m_l1_ffn_conv_w', 'new_m_l1_ffn_down', 'new_m_final_norm_g', 'new_v_l0_mix_norm_g', 'new_v_l0_w_in', 'new_v_l0_sc_conv_w', 'new_v_l0_w_out', 'new_v_l0_ffn_norm_g', 'new_v_l0_ffn_up', 'new_v_l0_ffn_conv_w', 'new_v_l0_ffn_down', 'new_v_l1_mix_norm_g', 'new_v_l1_w_in', 'new_v_l1_fox_b_f', 'new_v_l1_sg_w', 'new_v_l1_sg_b', 'new_v_l1_sg_norm_g', 'new_v_l1_w_out', 'new_v_l1_ffn_norm_g', 'new_v_l1_ffn_up', 'new_v_l1_ffn_conv_w', 'new_v_l1_ffn_down', 'new_v_final_norm_g']
TWIN_LEAF_KINDS = {'loss': 'loss', 'grad_x': 'grad_x', 'grad_l0_mix_norm_g': 'grad_w', 'grad_l0_w_in': 'grad_w', 'grad_l0_sc_conv_w': 'grad_w', 'grad_l0_w_out': 'grad_w', 'grad_l0_ffn_norm_g': 'grad_w', 'grad_l0_ffn_up': 'grad_w', 'grad_l0_ffn_conv_w': 'grad_w', 'grad_l0_ffn_down': 'grad_w', 'grad_l1_mix_norm_g': 'grad_w', 'grad_l1_w_in': 'grad_w', 'grad_l1_fox_b_f': 'grad_w', 'grad_l1_sg_w': 'grad_w', 'grad_l1_sg_b': 'grad_w', 'grad_l1_sg_norm_g': 'grad_w', 'grad_l1_w_out': 'grad_w', 'grad_l1_ffn_norm_g': 'grad_w', 'grad_l1_ffn_up': 'grad_w', 'grad_l1_ffn_conv_w': 'grad_w', 'grad_l1_ffn_down': 'grad_w', 'grad_final_norm_g': 'grad_w', 'delta_l0_mix_norm_g': 'delta_w', 'delta_l0_w_in': 'delta_w', 'delta_l0_sc_conv_w': 'delta_w', 'delta_l0_w_out': 'delta_w', 'delta_l0_ffn_norm_g': 'delta_w', 'delta_l0_ffn_up': 'delta_w', 'delta_l0_ffn_conv_w': 'delta_w', 'delta_l0_ffn_down': 'delta_w', 'delta_l1_mix_norm_g': 'delta_w', 'delta_l1_w_in': 'delta_w', 'delta_l1_fox_b_f': 'delta_w', 'delta_l1_sg_w': 'delta_w', 'delta_l1_sg_b': 'delta_w', 'delta_l1_sg_norm_g': 'delta_w', 'delta_l1_w_out': 'delta_w', 'delta_l1_ffn_norm_g': 'delta_w', 'delta_l1_ffn_up': 'delta_w', 'delta_l1_ffn_conv_w': 'delta_w', 'delta_l1_ffn_down': 'delta_w', 'delta_final_norm_g': 'delta_w', 'new_m_l0_mix_norm_g': 'new_m', 'new_m_l0_w_in': 'new_m', 'new_m_l0_sc_conv_w': 'new_m', 'new_m_l0_w_out': 'new_m', 'new_m_l0_ffn_norm_g': 'new_m', 'new_m_l0_ffn_up': 'new_m', 'new_m_l0_ffn_conv_w': 'new_m', 'new_m_l0_ffn_down': 'new_m', 'new_m_l1_mix_norm_g': 'new_m', 'new_m_l1_w_in': 'new_m', 'new_m_l1_fox_b_f': 'new_m', 'new_m_l1_sg_w': 'new_m', 'new_m_l1_sg_b': 'new_m', 'new_m_l1_sg_norm_g': 'new_m', 'new_m_l1_w_out': 'new_m', 'new_m_l1_ffn_norm_g': 'new_m', 'new_m_l1_ffn_up': 'new_m', 'new_m_l1_ffn_conv_w': 'new_m', 'new_m_l1_ffn_down': 'new_m', 'new_m_final_norm_g': 'new_m', 'new_v_l0_mix_norm_g': 'new_v', 'new_v_l0_w_in': 'new_v', 'new_v_l0_sc_conv_w': 'new_v', 'new_v_l0_w_out': 'new_v', 'new_v_l0_ffn_norm_g': 'new_v', 'new_v_l0_ffn_up': 'new_v', 'new_v_l0_ffn_conv_w': 'new_v', 'new_v_l0_ffn_down': 'new_v', 'new_v_l1_mix_norm_g': 'new_v', 'new_v_l1_w_in': 'new_v', 'new_v_l1_fox_b_f': 'new_v', 'new_v_l1_sg_w': 'new_v', 'new_v_l1_sg_b': 'new_v', 'new_v_l1_sg_norm_g': 'new_v', 'new_v_l1_w_out': 'new_v', 'new_v_l1_ffn_norm_g': 'new_v', 'new_v_l1_ffn_up': 'new_v', 'new_v_l1_ffn_conv_w': 'new_v', 'new_v_l1_ffn_down': 'new_v', 'new_v_final_norm_g': 'new_v'}


def _forward(args):
    return _fwd_reference(*[args[k] for k in FWD_PARAMS])


def _output_shape():
    out = _jax.eval_shape(lambda: _forward(_fwd_setup_inputs(0)))
    return out.shape, out.dtype

N_MICROBATCH = 1
ADAM_LR = 0.001
ADAM_B1 = 0.9
ADAM_B2 = 0.999
ADAM_EPS = 1e-08
ADAM_WD = 0.01
ADAM_STEP = 10
PER_EXAMPLE_BATCH_AXIS = {'x': 0, 'loss_target': 0}
SHARED_INPUTS = []
_WEIGHT_DTYPES = {'l0_mix_norm_g': _jnp.float32, 'l0_w_in': _jnp.float32, 'l0_sc_conv_w': _jnp.float32, 'l0_w_out': _jnp.float32, 'l0_ffn_norm_g': _jnp.float32, 'l0_ffn_up': _jnp.float32, 'l0_ffn_conv_w': _jnp.float32, 'l0_ffn_down': _jnp.float32, 'l1_mix_norm_g': _jnp.float32, 'l1_w_in': _jnp.float32, 'l1_fox_b_f': _jnp.float32, 'l1_sg_w': _jnp.float32, 'l1_sg_b': _jnp.float32, 'l1_sg_norm_g': _jnp.float32, 'l1_w_out': _jnp.float32, 'l1_ffn_norm_g': _jnp.float32, 'l1_ffn_up': _jnp.float32, 'l1_ffn_conv_w': _jnp.float32, 'l1_ffn_down': _jnp.float32, 'final_norm_g': _jnp.float32}
MOMENT_SCALE = {'l0_mix_norm_g': 8.062487e-02, 'l0_w_in': 4.518731e-02, 'l0_sc_conv_w': 5.961702e-02, 'l0_w_out': 4.921199e-02, 'l0_ffn_norm_g': 4.457060e-02, 'l0_ffn_up': 1.827617e-02, 'l0_ffn_conv_w': 1.866237e-02, 'l0_ffn_down': 2.980565e-02, 'l1_mix_norm_g': 3.776460e-02, 'l1_w_in': 2.317689e-02, 'l1_fox_b_f': 1.192407e-01, 'l1_sg_w': 1.936589e-02, 'l1_sg_b': 2.702228e-02, 'l1_sg_norm_g': 2.002751e-02, 'l1_w_out': 2.892064e-02, 'l1_ffn_norm_g': 3.229916e-02, 'l1_ffn_up': 1.354248e-02, 'l1_ffn_conv_w': 1.380064e-02, 'l1_ffn_down': 2.209723e-02, 'final_norm_g': 8.011163e+00}


def _to_microbatches(a, axis):
    t = _jnp.moveaxis(a, axis, 0)
    t = t.reshape((N_MICROBATCH, t.shape[0] // N_MICROBATCH) + t.shape[1:])
    return _jnp.moveaxis(t, 1, axis + 1)


def setup_inputs(seed: int = 0) -> dict:
    inp = _fwd_setup_inputs(seed)
    key = _jax.random.fold_in(_jax.random.key(seed), 7919)
    shape, _ = _output_shape()
    out = dict(inp)
    out["loss_target"] = _jax.random.normal(_jax.random.fold_in(key, 0), shape, _jnp.float32)
    for i, name in enumerate(TWIN_WEIGHTS):
        w = inp[name].astype(_jnp.float32)
        if MOMENT_SCALE is None:
            s = _jnp.sqrt(_jnp.mean(_jnp.square(w)) + 1e-30)
        else:
            s = MOMENT_SCALE[name]
        km, kv = _jax.random.split(_jax.random.fold_in(key, i + 1))
        out[name] = w
        out["m_" + name] = s * _jax.random.normal(km, w.shape, _jnp.float32)
        out["v_" + name] = (s * s) * _jax.random.uniform(kv, w.shape, _jnp.float32, 0.5, 1.5)
    if N_MICROBATCH > 1:
        for name, axis in PER_EXAMPLE_BATCH_AXIS.items():
            out[name] = _to_microbatches(out[name], axis)
    return {'x': out['x'], 'l0_mix_norm_g': out['l0_mix_norm_g'], 'l0_w_in': out['l0_w_in'], 'l0_sc_conv_w': out['l0_sc_conv_w'], 'l0_w_out': out['l0_w_out'], 'l0_ffn_norm_g': out['l0_ffn_norm_g'], 'l0_ffn_up': out['l0_ffn_up'], 'l0_ffn_conv_w': out['l0_ffn_conv_w'], 'l0_ffn_down': out['l0_ffn_down'], 'l1_mix_norm_g': out['l1_mix_norm_g'], 'l1_w_in': out['l1_w_in'], 'l1_fox_b_f': out['l1_fox_b_f'], 'l1_sg_w': out['l1_sg_w'], 'l1_sg_b': out['l1_sg_b'], 'l1_sg_norm_g': out['l1_sg_norm_g'], 'l1_w_out': out['l1_w_out'], 'l1_ffn_norm_g': out['l1_ffn_norm_g'], 'l1_ffn_up': out['l1_ffn_up'], 'l1_ffn_conv_w': out['l1_ffn_conv_w'], 'l1_ffn_down': out['l1_ffn_down'], 'final_norm_g': out['final_norm_g'], 'loss_target': out['loss_target'], 'm_l0_mix_norm_g': out['m_l0_mix_norm_g'], 'm_l0_w_in': out['m_l0_w_in'], 'm_l0_sc_conv_w': out['m_l0_sc_conv_w'], 'm_l0_w_out': out['m_l0_w_out'], 'm_l0_ffn_norm_g': out['m_l0_ffn_norm_g'], 'm_l0_ffn_up': out['m_l0_ffn_up'], 'm_l0_ffn_conv_w': out['m_l0_ffn_conv_w'], 'm_l0_ffn_down': out['m_l0_ffn_down'], 'm_l1_mix_norm_g': out['m_l1_mix_norm_g'], 'm_l1_w_in': out['m_l1_w_in'], 'm_l1_fox_b_f': out['m_l1_fox_b_f'], 'm_l1_sg_w': out['m_l1_sg_w'], 'm_l1_sg_b': out['m_l1_sg_b'], 'm_l1_sg_norm_g': out['m_l1_sg_norm_g'], 'm_l1_w_out': out['m_l1_w_out'], 'm_l1_ffn_norm_g': out['m_l1_ffn_norm_g'], 'm_l1_ffn_up': out['m_l1_ffn_up'], 'm_l1_ffn_conv_w': out['m_l1_ffn_conv_w'], 'm_l1_ffn_down': out['m_l1_ffn_down'], 'm_final_norm_g': out['m_final_norm_g'], 'v_l0_mix_norm_g': out['v_l0_mix_norm_g'], 'v_l0_w_in': out['v_l0_w_in'], 'v_l0_sc_conv_w': out['v_l0_sc_conv_w'], 'v_l0_w_out': out['v_l0_w_out'], 'v_l0_ffn_norm_g': out['v_l0_ffn_norm_g'], 'v_l0_ffn_up': out['v_l0_ffn_up'], 'v_l0_ffn_conv_w': out['v_l0_ffn_conv_w'], 'v_l0_ffn_down': out['v_l0_ffn_down'], 'v_l1_mix_norm_g': out['v_l1_mix_norm_g'], 'v_l1_w_in': out['v_l1_w_in'], 'v_l1_fox_b_f': out['v_l1_fox_b_f'], 'v_l1_sg_w': out['v_l1_sg_w'], 'v_l1_sg_b': out['v_l1_sg_b'], 'v_l1_sg_norm_g': out['v_l1_sg_norm_g'], 'v_l1_w_out': out['v_l1_w_out'], 'v_l1_ffn_norm_g': out['v_l1_ffn_norm_g'], 'v_l1_ffn_up': out['v_l1_ffn_up'], 'v_l1_ffn_conv_w': out['v_l1_ffn_conv_w'], 'v_l1_ffn_down': out['v_l1_ffn_down'], 'v_final_norm_g': out['v_final_norm_g']}


def _loss(weights, diff, rest, loss_target):
    with _jax.named_scope("forward"):
        args = {**rest, TWIN_DIFF_INPUT: diff, **{k: w.astype(_WEIGHT_DTYPES[k]) for k, w in weights.items()}}
        y = _forward(args)
    with _jax.named_scope("loss_head"):
        err = _jnp.square(y.astype(_jnp.float32) - loss_target)
        return 0.5 * _jnp.sum(_jnp.mean(err, axis=-1)) if err.ndim else 0.5 * err


def _adamw(w, g, m, v):
    m = ADAM_B1 * m + (1.0 - ADAM_B1) * g
    v = ADAM_B2 * v + (1.0 - ADAM_B2) * _jnp.square(g)
    m_hat = m / (1.0 - ADAM_B1 ** ADAM_STEP)
    v_hat = v / (1.0 - ADAM_B2 ** ADAM_STEP)
    delta = -ADAM_LR * (m_hat / (_jnp.sqrt(v_hat) + ADAM_EPS) + ADAM_WD * w)
    return delta, m, v


def reference(x, l0_mix_norm_g, l0_w_in, l0_sc_conv_w, l0_w_out, l0_ffn_norm_g, l0_ffn_up, l0_ffn_conv_w, l0_ffn_down, l1_mix_norm_g, l1_w_in, l1_fox_b_f, l1_sg_w, l1_sg_b, l1_sg_norm_g, l1_w_out, l1_ffn_norm_g, l1_ffn_up, l1_ffn_conv_w, l1_ffn_down, final_norm_g, loss_target, m_l0_mix_norm_g, m_l0_w_in, m_l0_sc_conv_w, m_l0_w_out, m_l0_ffn_norm_g, m_l0_ffn_up, m_l0_ffn_conv_w, m_l0_ffn_down, m_l1_mix_norm_g, m_l1_w_in, m_l1_fox_b_f, m_l1_sg_w, m_l1_sg_b, m_l1_sg_norm_g, m_l1_w_out, m_l1_ffn_norm_g, m_l1_ffn_up, m_l1_ffn_conv_w, m_l1_ffn_down, m_final_norm_g, v_l0_mix_norm_g, v_l0_w_in, v_l0_sc_conv_w, v_l0_w_out, v_l0_ffn_norm_g, v_l0_ffn_up, v_l0_ffn_conv_w, v_l0_ffn_down, v_l1_mix_norm_g, v_l1_w_in, v_l1_fox_b_f, v_l1_sg_w, v_l1_sg_b, v_l1_sg_norm_g, v_l1_w_out, v_l1_ffn_norm_g, v_l1_ffn_up, v_l1_ffn_conv_w, v_l1_ffn_down, v_final_norm_g):
    given = dict(x=x, l0_mix_norm_g=l0_mix_norm_g, l0_w_in=l0_w_in, l0_sc_conv_w=l0_sc_conv_w, l0_w_out=l0_w_out, l0_ffn_norm_g=l0_ffn_norm_g, l0_ffn_up=l0_ffn_up, l0_ffn_conv_w=l0_ffn_conv_w, l0_ffn_down=l0_ffn_down, l1_mix_norm_g=l1_mix_norm_g, l1_w_in=l1_w_in, l1_fox_b_f=l1_fox_b_f, l1_sg_w=l1_sg_w, l1_sg_b=l1_sg_b, l1_sg_norm_g=l1_sg_norm_g, l1_w_out=l1_w_out, l1_ffn_norm_g=l1_ffn_norm_g, l1_ffn_up=l1_ffn_up, l1_ffn_conv_w=l1_ffn_conv_w, l1_ffn_down=l1_ffn_down, final_norm_g=final_norm_g, loss_target=loss_target, m_l0_mix_norm_g=m_l0_mix_norm_g, m_l0_w_in=m_l0_w_in, m_l0_sc_conv_w=m_l0_sc_conv_w, m_l0_w_out=m_l0_w_out, m_l0_ffn_norm_g=m_l0_ffn_norm_g, m_l0_ffn_up=m_l0_ffn_up, m_l0_ffn_conv_w=m_l0_ffn_conv_w, m_l0_ffn_down=m_l0_ffn_down, m_l1_mix_norm_g=m_l1_mix_norm_g, m_l1_w_in=m_l1_w_in, m_l1_fox_b_f=m_l1_fox_b_f, m_l1_sg_w=m_l1_sg_w, m_l1_sg_b=m_l1_sg_b, m_l1_sg_norm_g=m_l1_sg_norm_g, m_l1_w_out=m_l1_w_out, m_l1_ffn_norm_g=m_l1_ffn_norm_g, m_l1_ffn_up=m_l1_ffn_up, m_l1_ffn_conv_w=m_l1_ffn_conv_w, m_l1_ffn_down=m_l1_ffn_down, m_final_norm_g=m_final_norm_g, v_l0_mix_norm_g=v_l0_mix_norm_g, v_l0_w_in=v_l0_w_in, v_l0_sc_conv_w=v_l0_sc_conv_w, v_l0_w_out=v_l0_w_out, v_l0_ffn_norm_g=v_l0_ffn_norm_g, v_l0_ffn_up=v_l0_ffn_up, v_l0_ffn_conv_w=v_l0_ffn_conv_w, v_l0_ffn_down=v_l0_ffn_down, v_l1_mix_norm_g=v_l1_mix_norm_g, v_l1_w_in=v_l1_w_in, v_l1_fox_b_f=v_l1_fox_b_f, v_l1_sg_w=v_l1_sg_w, v_l1_sg_b=v_l1_sg_b, v_l1_sg_norm_g=v_l1_sg_norm_g, v_l1_w_out=v_l1_w_out, v_l1_ffn_norm_g=v_l1_ffn_norm_g, v_l1_ffn_up=v_l1_ffn_up, v_l1_ffn_conv_w=v_l1_ffn_conv_w, v_l1_ffn_down=v_l1_ffn_down, v_final_norm_g=v_final_norm_g)
    weights = {n: given[n] for n in TWIN_WEIGHTS}
    shared = {n: given[n] for n in SHARED_INPUTS}
    per_example = {n: given[n] for n in ['x']}
    grad_fn = _jax.value_and_grad(_loss, argnums=(0, 1))

    def one_microbatch(ex, loss_target):
        ex = dict(ex)
        diff = ex.pop(TWIN_DIFF_INPUT)
        return grad_fn(weights, diff, {**shared, **ex}, loss_target)

    if N_MICROBATCH == 1:
        loss, (grad_w, grad_x) = one_microbatch(per_example, given["loss_target"])
    else:
        def body(carry, xs):
            loss_sum, grad_sum = carry
            l_k, (gw_k, gx_k) = one_microbatch(xs[0], xs[1])
            with _jax.named_scope("update"):
                return (loss_sum + l_k, _jax.tree.map(_jnp.add, grad_sum, gw_k)), gx_k

        init = (_jnp.zeros((), _jnp.float32), _jax.tree.map(_jnp.zeros_like, weights))
        (loss, grad_w), grad_x = _jax.lax.scan(body, init, (per_example, given["loss_target"]))
    with _jax.named_scope("update"):
        delta_w, new_m, new_v = {}, {}, {}
        for n in TWIN_WEIGHTS:
            delta_w[n], new_m[n], new_v[n] = _adamw(weights[n], grad_w[n], given["m_" + n], given["v_" + n])
    return (loss, grad_x, *[grad_w[n] for n in TWIN_WEIGHTS], *[delta_w[n] for n in TWIN_WEIGHTS],
            *[new_m[n] for n in TWIN_WEIGHTS], *[new_v[n] for n in TWIN_WEIGHTS])
```

```python
import functools

import jax
import jax.numpy as jnp
from jax import lax
from jax.experimental import pallas as pl
from jax.experimental.pallas import tpu as pltpu

F32 = jnp.float32
BF16 = jnp.bfloat16

D_MODEL = 2048
HEAD = 128
N_HEADS = 8
HALF = N_HEADS * HEAD
D_FF = 5632
EPS = 1e-6
ATT_SCALE = HEAD ** -0.5
ATT_BLOCK = 256
NEG = -1e30

ADAM_LR = 0.001
ADAM_B1 = 0.9
ADAM_B2 = 0.999
ADAM_EPS = 1e-08
ADAM_WD = 0.01
ADAM_STEP = 10

VMEM_LIMIT_BYTES = 48 * 1024 * 1024
MESH = pl.DeviceIdType.MESH
HBM_SPEC = pl.BlockSpec(memory_space=pltpu.HBM)
VMEM_SPEC = pl.BlockSpec(memory_space=pltpu.VMEM)


def _pcall(body, **kw):
    return pl.pallas_call(body, **kw)


def _params(*semantics):
    return pltpu.CompilerParams(dimension_semantics=semantics, vmem_limit_bytes=VMEM_LIMIT_BYTES)


def _pick(n, cap):
    best = None
    for t in range(128, min(n, cap) + 1, 128):
        if n % t == 0:
            best = t
    return n if best is None else best


def _dot(a, b, dims):
    return lax.dot_general(a, b, (dims, ((), ())), preferred_element_type=F32)


def _dot_nn(a, b):
    return _dot(a, b, ((1,), (0,)))


def _dot_nt(a, b):
    return _dot(a, b, ((1,), (1,)))


def _dot_tn(a, b):
    return _dot(a, b, ((0,), (0,)))


def _split3(x):
    hi = x.astype(BF16)
    r1 = x - hi.astype(F32)
    mid = r1.astype(BF16)
    lo = (r1 - mid.astype(F32)).astype(BF16)
    return hi, mid, lo


def _softplus(z):
    return jnp.maximum(z, 0.0) + jnp.log1p(jnp.exp(-jnp.abs(z)))


def _log_sigmoid(z):
    return jnp.minimum(z, 0.0) - jnp.log1p(jnp.exp(-jnp.abs(z)))


_GELU_K = 0.7978845608028654


def _gelu(x):
    return 0.5 * x * (1.0 + jnp.tanh(_GELU_K * (x + 0.044715 * x * x * x)))


def _gelu_grad(x):
    t = jnp.tanh(_GELU_K * (x + 0.044715 * x * x * x))
    return 0.5 * (1.0 + t) + 0.5 * x * (1.0 - t * t) * _GELU_K * (1.0 + 3.0 * 0.044715 * x * x)


def _shift_down(x, k):
    r = lax.broadcasted_iota(jnp.int32, x.shape, 0)
    return jnp.where(r >= k, pltpu.roll(x, k, axis=0), 0.0)


def _shift_up(x, k):
    n = x.shape[0]
    r = lax.broadcasted_iota(jnp.int32, x.shape, 0)
    return jnp.where(r < n - k, pltpu.roll(x, n - k, axis=0), 0.0)


def _conv3(s, w):
    return w[0:1, :] * _shift_down(s, 2) + w[1:2, :] * _shift_down(s, 1) + w[2:3, :] * s


def _conv3_transpose(d, w):
    return w[2:3, :] * d + w[1:2, :] * _shift_up(d, 1) + w[0:1, :] * _shift_up(d, 2)


def _conv3_wgrad(d, s, dw_ref):
    dw_ref[0:1, :] = jnp.sum(d * _shift_down(s, 2), axis=0, keepdims=True)
    dw_ref[1:2, :] = jnp.sum(d * _shift_down(s, 1), axis=0, keepdims=True)
    dw_ref[2:3, :] = jnp.sum(d * s, axis=0, keepdims=True)


def _matmul(a, b, mode, out_dtype, name, res=None):
    if mode == "nn":
        (m, k), (k2, n) = a.shape, b.shape
    elif mode == "nt":
        (m, k), (n, k2) = a.shape, b.shape
    else:
        (k, m), (k2, n) = a.shape, b.shape
    assert k == k2, (a.shape, b.shape, mode)
    tm, tn, tk = _pick(m, 512), _pick(n, 512), _pick(k, 2048)
    nk = k // tk
    if mode == "tn":
        a_spec = pl.BlockSpec((tk, tm), lambda i, j, kk: (kk, i))
    else:
        a_spec = pl.BlockSpec((tm, tk), lambda i, j, kk: (i, kk))
    if mode == "nt":
        b_spec = pl.BlockSpec((tn, tk), lambda i, j, kk: (j, kk))
    else:
        b_spec = pl.BlockSpec((tk, tn), lambda i, j, kk: (kk, j))
    o_spec = pl.BlockSpec((tm, tn), lambda i, j, kk: (i, j))
    dims = {"nn": ((1,), (0,)), "nt": ((1,), (1,)), "tn": ((0,), (0,))}[mode]
    has_res = res is not None

    def body(*refs):
        a_ref, b_ref = refs[0], refs[1]
        r_ref = refs[2] if has_res else None
        o_ref = refs[3] if has_res else refs[2]
        part = _dot(a_ref[...].astype(BF16), b_ref[...].astype(BF16), dims)

        def finish(total):
            if has_res:
                total = total + r_ref[...]
            o_ref[...] = total.astype(out_dtype)

        if nk == 1:
            finish(part)
        else:
            acc_ref = refs[-1]
            kk = pl.program_id(2)

            @pl.when(kk == 0)
            def _():
                acc_ref[...] = part

            @pl.when(kk > 0)
            def _():
                acc_ref[...] += part

            @pl.when(kk == nk - 1)
            def _():
                finish(acc_ref[...])

    in_specs = [a_spec, b_spec] + ([o_spec] if has_res else [])
    args = (a, b) + ((res,) if has_res else ())
    return _pcall(
        body, name=name, grid=(m // tm, n // tn, nk),
        in_specs=in_specs, out_specs=o_spec,
        out_shape=jax.ShapeDtypeStruct((m, n), out_dtype),
        scratch_shapes=[pltpu.VMEM((tm, tn), F32)] if nk > 1 else [],
        compiler_params=_params("parallel", "parallel", "arbitrary"),
    )(*args)


ROW_TILE = 256


def _rmsnorm_fwd(x, g, name):
    s, d = x.shape

    def body(x_ref, g_ref, o_ref):
        xf = x_ref[...]
        r = lax.rsqrt(jnp.mean(xf * xf, axis=-1, keepdims=True) + EPS)
        o_ref[...] = (xf * r * g_ref[...]).astype(BF16)

    row = pl.BlockSpec((ROW_TILE, d), lambda i: (i, 0))
    vec = pl.BlockSpec((1, d), lambda i: (0, 0))
    return _pcall(body, name=name, grid=(s // ROW_TILE,), in_specs=[row, vec], out_specs=row,
                  out_shape=jax.ShapeDtypeStruct((s, d), BF16), compiler_params=_params("parallel"))(x, g)


def _rmsnorm_bwd(x, g, dh, dres, name):
    s, d = x.shape

    def body(x_ref, g_ref, dh_ref, dres_ref, dx_ref, dg_ref):
        xf = x_ref[...]
        r = lax.rsqrt(jnp.mean(xf * xf, axis=-1, keepdims=True) + EPS)
        xhat = xf * r
        dh_v = dh_ref[...]
        dxh = dh_v * g_ref[...]
        proj = jnp.mean(dxh * xhat, axis=-1, keepdims=True)
        dx_ref[...] = dres_ref[...] + r * (dxh - xhat * proj)
        part = jnp.sum(dh_v * xhat, axis=0, keepdims=True)

        @pl.when(pl.program_id(0) == 0)
        def _():
            dg_ref[...] = part

        @pl.when(pl.program_id(0) > 0)
        def _():
            dg_ref[...] += part

    row = pl.BlockSpec((ROW_TILE, d), lambda i: (i, 0))
    vec = pl.BlockSpec((1, d), lambda i: (0, 0))
    return _pcall(body, name=name, grid=(s // ROW_TILE,), in_specs=[row, vec, row, row], out_specs=[row, vec],
                  out_shape=[jax.ShapeDtypeStruct((s, d), F32), jax.ShapeDtypeStruct((1, d), F32)],
                  compiler_params=_params("arbitrary"))(x, g, dh, dres)


def _loss_head(x, g, target, name):
    s, d = x.shape

    def body(x_ref, g_ref, t_ref, dx_ref, dg_ref, loss_ref):
        xf = x_ref[...]
        r = lax.rsqrt(jnp.mean(xf * xf, axis=-1, keepdims=True) + EPS)
        xhat = xf * r
        gv = g_ref[...]
        err = xhat * gv - t_ref[...]
        dy = err * (1.0 / d)
        dxh = dy * gv
        proj = jnp.mean(dxh * xhat, axis=-1, keepdims=True)
        dx_ref[...] = r * (dxh - xhat * proj)
        dg_part = jnp.sum(dy * xhat, axis=0, keepdims=True)
        row_loss = jnp.sum(err * err, axis=-1, keepdims=True) * (0.5 / d)
        loss_part = jnp.broadcast_to(jnp.sum(row_loss, axis=0, keepdims=True), (1, 128))

        @pl.when(pl.program_id(0) == 0)
        def _():
            dg_ref[...] = dg_part
            loss_ref[...] = loss_part

        @pl.when(pl.program_id(0) > 0)
        def _():
            dg_ref[...] += dg_part
            loss_ref[...] += loss_part

    row = pl.BlockSpec((ROW_TILE, d), lambda i: (i, 0))
    vec = pl.BlockSpec((1, d), lambda i: (0, 0))
    one = pl.BlockSpec((1, 128), lambda i: (0, 0))
    return _pcall(body, name=name, grid=(s // ROW_TILE,), in_specs=[row, vec, row], out_specs=[row, vec, one],
                  out_shape=[jax.ShapeDtypeStruct((s, d), F32), jax.ShapeDtypeStruct((1, d), F32),
                             jax.ShapeDtypeStruct((1, 128), F32)],
                  compiler_params=_params("arbitrary"))(x, g, target)


def _head_specs(s, col0):
    t = ATT_BLOCK
    qspec = pl.BlockSpec((t, HEAD), lambda h, i: (i, col0[0] + h))
    kspec = pl.BlockSpec((s, HEAD), lambda h, i: (0, col0[1] + h))
    vspec = pl.BlockSpec((s, HEAD), lambda h, i: (0, col0[2] + h))
    return qspec, kspec, vspec


def _order_matrix(t, later):
    r, c = lax.broadcasted_iota(jnp.int32, (t, t), 0), lax.broadcasted_iota(jnp.int32, (t, t), 1)
    return (r > c if later else r < c).astype(BF16)


def _exact_dot(x, m):
    hi, mid, lo = _split3(x)
    return _dot_nn(hi, m) + _dot_nn(mid, m) + _dot_nn(lo, m)


def _sb_block(q, kblk, row, ks, carry_l, u):
    t = ATT_BLOCK
    z = _dot_nt(q, kblk) * ATT_SCALE
    col = ks + lax.broadcasted_iota(jnp.int32, (t, t), 1)
    mask = col < row
    sp = _softplus(z)
    l = jnp.where(mask, -sp, 0.0)
    later = _exact_dot(l, u) + carry_l
    a = jnp.where(mask, jnp.exp(z - sp + later), 0.0)
    return z, mask, l, a


def _sb_carry_spec(s):
    t = ATT_BLOCK
    return pl.BlockSpec((None, None, s // t, t, 1), lambda h, i: (h, i, 0, 0, 0))


def _sb_fwd(p, name):
    s = p.shape[0]
    t = ATT_BLOCK
    nb = s // t

    def body(q_ref, k_ref, v_ref, o_ref, cl_ref):
        i = pl.program_id(1)
        q = q_ref[...].astype(BF16)
        row = i * t + lax.broadcasted_iota(jnp.int32, (t, t), 0)
        u = _order_matrix(t, True)
        cl_ref[...] = jnp.zeros_like(cl_ref)

        def step(n, carry):
            acc, carry_l = carry
            kb = i - n
            ks = pl.multiple_of(kb * t, t)
            kblk = k_ref[pl.ds(ks, t), :].astype(BF16)
            vblk = v_ref[pl.ds(ks, t), :].astype(BF16)
            cl_ref[kb] = carry_l
            _, _, l, a = _sb_block(q, kblk, row, ks, carry_l, u)
            acc = acc + _dot_nn(a.astype(BF16), vblk)
            return acc, carry_l + jnp.sum(l, axis=1, keepdims=True)

        acc, _ = lax.fori_loop(0, i + 1, step, (jnp.zeros((t, HEAD), F32), jnp.zeros((t, 1), F32)))
        o_ref[...] = acc

    qspec, kspec, vspec = _head_specs(s, (0, N_HEADS, 2 * N_HEADS))
    ospec = pl.BlockSpec((t, HEAD), lambda h, i: (i, h))
    return _pcall(body, name=name, grid=(N_HEADS, nb), in_specs=[qspec, kspec, vspec],
                  out_specs=[ospec, _sb_carry_spec(s)],
                  out_shape=[jax.ShapeDtypeStruct((s, HALF), F32), jax.ShapeDtypeStruct((N_HEADS, nb, nb, t, 1), F32)],
                  compiler_params=_params("parallel", "parallel"))(p, p, p)


def _sb_bwd(p, d_ab, carries, name):
    s = p.shape[0]
    t = ATT_BLOCK

    def body(q_ref, k_ref, v_ref, do_ref, cl_ref, dq_ref, dk_ref, dv_ref):
        i = pl.program_id(1)

        @pl.when(i == 0)
        def _():
            dk_ref[...] = jnp.zeros_like(dk_ref)
            dv_ref[...] = jnp.zeros_like(dv_ref)

        q = q_ref[...].astype(BF16)
        do = do_ref[...].astype(BF16)
        row = i * t + lax.broadcasted_iota(jnp.int32, (t, t), 0)
        u = _order_matrix(t, True)
        lower = _order_matrix(t, False)

        def step(kb, carry):
            dq, carry_g = carry
            ks = pl.multiple_of(kb * t, t)
            kblk = k_ref[pl.ds(ks, t), :].astype(BF16)
            vblk = v_ref[pl.ds(ks, t), :].astype(BF16)
            z, mask, _, a = _sb_block(q, kblk, row, ks, cl_ref[kb], u)
            g = a * _dot_nt(do, vblk)
            earlier_g = _exact_dot(g, lower) + carry_g
            sig = jax.nn.sigmoid(z)
            dz = jnp.where(mask, g * (1.0 - sig) - sig * earlier_g, 0.0).astype(BF16)
            dv_ref[pl.ds(ks, t), :] += _dot_tn(a.astype(BF16), do)
            dk_ref[pl.ds(ks, t), :] += _dot_tn(dz, q) * ATT_SCALE
            return dq + _dot_nn(dz, kblk) * ATT_SCALE, carry_g + jnp.sum(g, axis=1, keepdims=True)

        dq, _ = lax.fori_loop(0, i + 1, step, (jnp.zeros((t, HEAD), F32), jnp.zeros((t, 1), F32)))
        dq_ref[...] = dq

    qspec, kspec, vspec = _head_specs(s, (0, N_HEADS, 2 * N_HEADS))
    blk = pl.BlockSpec((t, HEAD), lambda h, i: (i, h))
    whole = pl.BlockSpec((s, HEAD), lambda h, i: (0, h))
    shape = jax.ShapeDtypeStruct((s, HALF), F32)
    return _pcall(body, name=name, grid=(N_HEADS, s // t), in_specs=[qspec, kspec, vspec, blk, _sb_carry_spec(s)],
                  out_specs=[blk, whole, whole], out_shape=[shape, shape, shape],
                  compiler_params=_params("parallel", "arbitrary"))(p, p, p, d_ab, carries)


COL_TILE = 256


def _sc_fwd(p, w, name):
    s = p.shape[0]
    nb = HALF // COL_TILE

    def body(gb_ref, gc_ref, h_ref, w_ref, o_ref):
        conv = _conv3(gc_ref[...] * h_ref[...], w_ref[...])
        o_ref[...] = (gb_ref[...] * conv).astype(BF16)

    def col(k):
        return pl.BlockSpec((s, COL_TILE), lambda j: (0, k * nb + j))

    wspec = pl.BlockSpec((3, COL_TILE), lambda j: (0, j))
    return _pcall(body, name=name, grid=(nb,), in_specs=[col(3), col(4), col(5), wspec], out_specs=col(0),
                  out_shape=jax.ShapeDtypeStruct((s, HALF), BF16), compiler_params=_params("parallel"))(p, p, p, w)


def _sc_bwd(p, w, d_ab, name):
    s = p.shape[0]
    nb = HALF // COL_TILE

    def body(gb_ref, gc_ref, h_ref, w_ref, d_ref, dgb_ref, dgc_ref, dh_ref, dw_ref):
        gc, hin, wv, d = gc_ref[...], h_ref[...], w_ref[...], d_ref[...]
        sig = gc * hin
        dgb_ref[...] = d * _conv3(sig, wv)
        dconv = d * gb_ref[...]
        _conv3_wgrad(dconv, sig, dw_ref)
        dsig = _conv3_transpose(dconv, wv)
        dgc_ref[...] = dsig * hin
        dh_ref[...] = dsig * gc

    def col(k):
        return pl.BlockSpec((s, COL_TILE), lambda j: (0, k * nb + j))

    wspec = pl.BlockSpec((3, COL_TILE), lambda j: (0, j))
    act = jax.ShapeDtypeStruct((s, HALF), F32)
    return _pcall(body, name=name, grid=(nb,), in_specs=[col(3), col(4), col(5), wspec, col(1)],
                  out_specs=[col(0), col(0), col(0), wspec],
                  out_shape=[act, act, act, jax.ShapeDtypeStruct((3, HALF), F32)],
                  compiler_params=_params("parallel"))(p, p, p, w, d_ab)


def _ffn_act_fwd(u, w, name):
    s = u.shape[0]
    nb = D_FF // COL_TILE

    def body(ug_ref, uu_ref, wg_ref, wu_ref, o_ref):
        gate = _conv3(ug_ref[...], wg_ref[...])
        up = _conv3(uu_ref[...], wu_ref[...])
        o_ref[...] = (gate * jax.nn.sigmoid(gate) * up).astype(BF16)

    def col(k):
        return pl.BlockSpec((s, COL_TILE), lambda j: (0, k * nb + j))

    def wcol(k):
        return pl.BlockSpec((3, COL_TILE), lambda j: (0, k * nb + j))

    return _pcall(body, name=name, grid=(nb,), in_specs=[col(0), col(1), wcol(0), wcol(1)], out_specs=col(0),
                  out_shape=jax.ShapeDtypeStruct((s, D_FF), BF16),
                  compiler_params=_params("parallel"))(u, u, w, w)


def _ffn_act_bwd(u, w, d_f, name):
    s = u.shape[0]
    nb = D_FF // COL_TILE

    def body(ug_ref, uu_ref, wg_ref, wu_ref, d_ref, dug_ref, duu_ref, dwg_ref, dwu_ref):
        ug, uu, wg, wu, d = ug_ref[...], uu_ref[...], wg_ref[...], wu_ref[...], d_ref[...]
        gate = _conv3(ug, wg)
        up = _conv3(uu, wu)
        sig = jax.nn.sigmoid(gate)
        d_gate = d * up * sig * (1.0 + gate * (1.0 - sig))
        d_up = d * gate * sig
        _conv3_wgrad(d_gate, ug, dwg_ref)
        _conv3_wgrad(d_up, uu, dwu_ref)
        dug_ref[...] = _conv3_transpose(d_gate, wg).astype(BF16)
        duu_ref[...] = _conv3_transpose(d_up, wu).astype(BF16)

    def col(k):
        return pl.BlockSpec((s, COL_TILE), lambda j: (0, k * nb + j))

    def wcol(k):
        return pl.BlockSpec((3, COL_TILE), lambda j: (0, k * nb + j))

    act = jax.ShapeDtypeStruct((s, D_FF), BF16)
    wsh = jax.ShapeDtypeStruct((3, D_FF), F32)
    return _pcall(body, name=name, grid=(nb,), in_specs=[col(0), col(1), wcol(0), wcol(1), col(0)],
                  out_specs=[col(0), col(0), wcol(0), wcol(0)], out_shape=[act, act, wsh, wsh],
                  compiler_params=_params("parallel"))(u, u, w, w, d_f)


def _sg_common(u, v, g, w_ref, bias, mixed_ref):
    rows = u.shape[0]
    gu = _gelu(u)
    gv = _gelu(v)
    xc = gv - jnp.mean(gv, axis=-1, keepdims=True)
    rstd = lax.rsqrt(jnp.mean(xc * xc, axis=-1, keepdims=True) + EPS)
    xhat = xc * rstd
    vn = xhat * g
    tril = lax.broadcasted_iota(jnp.int32, (HEAD, HEAD), 0) >= lax.broadcasted_iota(jnp.int32, (HEAD, HEAD), 1)
    wts = [jnp.where(tril, w_ref[grp], 0.0).astype(BF16) for grp in range(N_HEADS)]
    for n in range(rows // HEAD):
        for grp in range(N_HEADS):
            blk = vn[n * HEAD:(n + 1) * HEAD, grp * HEAD:(grp + 1) * HEAD].astype(BF16)
            mixed_ref[n * HEAD:(n + 1) * HEAD, grp * HEAD:(grp + 1) * HEAD] = _dot_nn(wts[grp], blk)
    mixed = mixed_ref[...] + jnp.concatenate([bias] * (rows // HEAD), axis=0)
    return gu, xhat, rstd, vn, mixed, wts, tril


def _sg_fwd(p, sg_w, bias, g, name):
    s = p.shape[0]

    def body(u_ref, v_ref, w_ref, b_ref, g_ref, o_ref, mixed_ref):
        gu, _, _, _, mixed, _, _ = _sg_common(u_ref[...], v_ref[...], g_ref[...], w_ref, b_ref[...], mixed_ref)
        o_ref[...] = (gu * mixed).astype(BF16)

    def half(k):
        return pl.BlockSpec((ROW_TILE, HALF), lambda i: (i, k))

    wspec = pl.BlockSpec((N_HEADS, HEAD, HEAD), lambda i: (0, 0, 0))
    bspec = pl.BlockSpec((HEAD, HALF), lambda i: (0, 0))
    gspec = pl.BlockSpec((1, HALF), lambda i: (0, 0))
    return _pcall(body, name=name, grid=(s // ROW_TILE,), in_specs=[half(0), half(1), wspec, bspec, gspec],
                  out_specs=half(0), out_shape=jax.ShapeDtypeStruct((s, HALF), BF16),
                  scratch_shapes=[pltpu.VMEM((ROW_TILE, HALF), F32)],
                  compiler_params=_params("parallel"))(p, p, sg_w, bias, g)


def _sg_bwd(p, sg_w, bias, g, d_cd, name):
    s = p.shape[0]
    nsteps = s // ROW_TILE

    def body(u_ref, v_ref, w_ref, b_ref, g_ref, d_ref, du_ref, dv_ref, dw_ref, db_ref, dg_ref,
             mixed_ref, dvn_ref, dbias_ref):
        i = pl.program_id(0)
        u, v, gain, d = u_ref[...], v_ref[...], g_ref[...], d_ref[...]
        gu, xhat, rstd, vn, mixed, wts, tril = _sg_common(u, v, gain, w_ref, b_ref[...], mixed_ref)

        @pl.when(i == 0)
        def _():
            dw_ref[...] = jnp.zeros_like(dw_ref)
            dg_ref[...] = jnp.zeros_like(dg_ref)
            dbias_ref[...] = jnp.zeros_like(dbias_ref)

        du_ref[...] = d * mixed * _gelu_grad(u)
        dm = d * gu
        for n in range(ROW_TILE // HEAD):
            rs = slice(n * HEAD, (n + 1) * HEAD)
            dbias_ref[...] += dm[rs, :]
            for grp in range(N_HEADS):
                cs = slice(grp * HEAD, (grp + 1) * HEAD)
                dm_blk = dm[rs, cs].astype(BF16)
                dw_ref[grp] += jnp.where(tril, _dot_nt(dm_blk, vn[rs, cs].astype(BF16)), 0.0)
                dvn_ref[rs, cs] = _dot_tn(wts[grp], dm_blk)
        dvn = dvn_ref[...]
        dg_ref[...] += jnp.sum(dvn * xhat, axis=0, keepdims=True)
        dxh = dvn * gain
        d_gv = rstd * (dxh - jnp.mean(dxh, axis=-1, keepdims=True) - xhat * jnp.mean(dxh * xhat, axis=-1, keepdims=True))
        dv_ref[...] = d_gv * _gelu_grad(v)

        @pl.when(i == nsteps - 1)
        def _():
            lane = lax.broadcasted_iota(jnp.int32, (HEAD, HEAD), 1)
            out = jnp.zeros((HEAD, HEAD), F32)
            for grp in range(N_HEADS):
                tot = jnp.sum(dbias_ref[:, grp * HEAD:(grp + 1) * HEAD], axis=1, keepdims=True)
                out = out + jnp.where(lane == grp, tot, 0.0)
            db_ref[...] = out

    def half(k):
        return pl.BlockSpec((ROW_TILE, HALF), lambda i: (i, k))

    wspec = pl.BlockSpec((N_HEADS, HEAD, HEAD), lambda i: (0, 0, 0))
    bspec = pl.BlockSpec((HEAD, HALF), lambda i: (0, 0))
    gspec = pl.BlockSpec((1, HALF), lambda i: (0, 0))
    dbspec = pl.BlockSpec((HEAD, HEAD), lambda i: (0, 0))
    act = jax.ShapeDtypeStruct((s, HALF), F32)
    return _pcall(body, name=name, grid=(nsteps,), in_specs=[half(0), half(1), wspec, bspec, gspec, half(0)],
                  out_specs=[half(0), half(0), wspec, dbspec, gspec],
                  out_shape=[act, act, jax.ShapeDtypeStruct((N_HEADS, HEAD, HEAD), F32),
                             jax.ShapeDtypeStruct((HEAD, HEAD), F32), jax.ShapeDtypeStruct((1, HALF), F32)],
                  scratch_shapes=[pltpu.VMEM((ROW_TILE, HALF), F32), pltpu.VMEM((ROW_TILE, HALF), F32),
                                  pltpu.VMEM((HEAD, HALF), F32)],
                  compiler_params=_params("arbitrary"))(p, p, sg_w, bias, g, d_cd)


def _fox_prep(f, b, name):
    s = f.shape[0]
    t = ATT_BLOCK

    def body(f_ref, b_ref, c_ref):
        tri = (lax.broadcasted_iota(jnp.int32, (t, t), 0) >= lax.broadcasted_iota(jnp.int32, (t, t), 1)).astype(BF16)
        carry = jnp.zeros((1, 128), F32)
        for n in range(s // t):
            lf = _log_sigmoid(f_ref[n * t:(n + 1) * t, :] + b_ref[...])
            hi, mid, lo = _split3(lf)
            c_ref[n * t:(n + 1) * t, :] = _dot_nn(tri, hi) + _dot_nn(tri, mid) + _dot_nn(tri, lo) + carry
            carry = carry + jnp.sum(lf, axis=0, keepdims=True)

    return _pcall(body, name=name, in_specs=[VMEM_SPEC, VMEM_SPEC], out_specs=VMEM_SPEC,
                  out_shape=jax.ShapeDtypeStruct((s, 128), F32))(f, b)


def _fox_post(drow, dcol, f, b, name):
    s = f.shape[0]
    t = ATT_BLOCK

    def body(drow_ref, dcol_ref, f_ref, b_ref, df_ref, db_ref):
        tri = (lax.broadcasted_iota(jnp.int32, (t, t), 1) >= lax.broadcasted_iota(jnp.int32, (t, t), 0)).astype(BF16)
        carry = jnp.zeros((1, 128), F32)
        db = jnp.zeros((1, 128), F32)
        for n in reversed(range(s // t)):
            rs = slice(n * t, (n + 1) * t)
            dc = drow_ref[rs, :] - dcol_ref[rs, :]
            hi, mid, lo = _split3(dc)
            dlogf = _dot_nn(tri, hi) + _dot_nn(tri, mid) + _dot_nn(tri, lo) + carry
            carry = carry + jnp.sum(dc, axis=0, keepdims=True)
            df = dlogf * jax.nn.sigmoid(-(f_ref[rs, :] + b_ref[...]))
            df_ref[rs, :] = df
            db = db + jnp.sum(df, axis=0, keepdims=True)
        db_ref[...] = db

    return _pcall(body, name=name, in_specs=[VMEM_SPEC] * 4, out_specs=[VMEM_SPEC, VMEM_SPEC],
                  out_shape=[jax.ShapeDtypeStruct((s, 128), F32), jax.ShapeDtypeStruct((1, 128), F32)])(drow, dcol, f, b)


def _fox_specs(s):
    t = ATT_BLOCK
    ccol = pl.BlockSpec((None, t, 1), lambda h, i: (h, i, 0))
    crow = pl.BlockSpec((None, s // t, 1, t), lambda h, i: (h, 0, 0, 0))
    return ccol, crow


def _fox_fwd(p, c_col, c_row, name):
    s = p.shape[0]
    t = ATT_BLOCK

    def body(q_ref, k_ref, v_ref, cc_ref, cr_ref, o_ref, lse_ref):
        i = pl.program_id(1)
        q = q_ref[...].astype(BF16)
        ct = cc_ref[...]
        row = i * t + lax.broadcasted_iota(jnp.int32, (t, t), 0)

        def step(n, carry):
            acc, m, l = carry
            ks = pl.multiple_of(n * t, t)
            kblk = k_ref[pl.ds(ks, t), :].astype(BF16)
            vblk = v_ref[pl.ds(ks, t), :].astype(BF16)
            logit = _dot_nt(q, kblk) * ATT_SCALE + ct - cr_ref[n]
            col = ks + lax.broadcasted_iota(jnp.int32, (t, t), 1)
            logit = jnp.where(col <= row, logit, NEG)
            m_new = jnp.maximum(m, jnp.max(logit, axis=1, keepdims=True))
            alpha = jnp.exp(m - m_new)
            pr = jnp.exp(logit - m_new)
            l = alpha * l + jnp.sum(pr, axis=1, keepdims=True)
            acc = alpha * acc + _dot_nn(pr.astype(BF16), vblk)
            return acc, m_new, l

        init = (jnp.zeros((t, HEAD), F32), jnp.full((t, 1), NEG, F32), jnp.zeros((t, 1), F32))
        acc, m, l = lax.fori_loop(0, i + 1, step, init)
        o_ref[...] = acc / l
        lse_ref[...] = m + jnp.log(l)

    qspec, kspec, vspec = _head_specs(s, (2 * N_HEADS, 3 * N_HEADS, 4 * N_HEADS))
    ccol, crow = _fox_specs(s)
    ospec = pl.BlockSpec((t, HEAD), lambda h, i: (i, h))
    return _pcall(body, name=name, grid=(N_HEADS, s // t), in_specs=[qspec, kspec, vspec, ccol, crow],
                  out_specs=[ospec, ccol],
                  out_shape=[jax.ShapeDtypeStruct((s, HALF), F32), jax.ShapeDtypeStruct((N_HEADS, s, 1), F32)],
                  compiler_params=_params("parallel", "parallel"))(p, p, p, c_col, c_row)


def _fox_bwd(p, c_col, c_row, lse, d_cd, d_out, name):
    s = p.shape[0]
    t = ATT_BLOCK

    def body(q_ref, k_ref, v_ref, cc_ref, cr_ref, lse_ref, do_ref, o_ref, dq_ref, dk_ref, dv_ref, dcol_ref, drow_ref):
        i = pl.program_id(1)

        @pl.when(i == 0)
        def _():
            dk_ref[...] = jnp.zeros_like(dk_ref)
            dv_ref[...] = jnp.zeros_like(dv_ref)
            dcol_ref[...] = jnp.zeros_like(dcol_ref)

        q = q_ref[...].astype(BF16)
        do_f = do_ref[...]
        do = do_f.astype(BF16)
        delta = jnp.sum(do_f * o_ref[...], axis=1, keepdims=True)
        ct = cc_ref[...]
        lse_v = lse_ref[...]
        row = i * t + lax.broadcasted_iota(jnp.int32, (t, t), 0)
        ones = jnp.ones((t, HEAD), BF16)

        def step(n, carry):
            dq, drow = carry
            ks = pl.multiple_of(n * t, t)
            kblk = k_ref[pl.ds(ks, t), :].astype(BF16)
            vblk = v_ref[pl.ds(ks, t), :].astype(BF16)
            logit = _dot_nt(q, kblk) * ATT_SCALE + ct - cr_ref[n]
            col = ks + lax.broadcasted_iota(jnp.int32, (t, t), 1)
            pr = jnp.where(col <= row, jnp.exp(logit - lse_v), 0.0)
            ds = pr * (_dot_nt(do, vblk) - delta)
            dsb = ds.astype(BF16)
            dv_ref[pl.ds(ks, t), :] += _dot_tn(pr.astype(BF16), do)
            dk_ref[pl.ds(ks, t), :] += _dot_tn(dsb, q) * ATT_SCALE
            dcol_ref[pl.ds(ks, t), :] += _dot_tn(dsb, ones)
            return dq + _dot_nn(dsb, kblk) * ATT_SCALE, drow + jnp.sum(dsb.astype(F32), axis=1, keepdims=True)

        dq, drow = lax.fori_loop(0, i + 1, step, (jnp.zeros((t, HEAD), F32), jnp.zeros((t, 1), F32)))
        dq_ref[...] = dq
        drow_ref[...] = drow

    qspec, kspec, vspec = _head_specs(s, (2 * N_HEADS, 3 * N_HEADS, 4 * N_HEADS))
    ccol, crow = _fox_specs(s)
    dospec = pl.BlockSpec((t, HEAD), lambda h, i: (i, N_HEADS + h))
    blk = pl.BlockSpec((t, HEAD), lambda h, i: (i, h))
    whole = pl.BlockSpec((s, HEAD), lambda h, i: (0, h))
    shape = jax.ShapeDtypeStruct((s, HALF), F32)
    return _pcall(body, name=name, grid=(N_HEADS, s // t),
                  in_specs=[qspec, kspec, vspec, ccol, crow, ccol, dospec, blk],
                  out_specs=[blk, whole, whole, whole, ccol],
                  out_shape=[shape, shape, shape, shape, jax.ShapeDtypeStruct((N_HEADS, s, 1), F32)],
                  compiler_params=_params("parallel", "arbitrary"))(p, p, p, c_col, c_row, lse, d_cd, d_out)


def _row_tile(rows, cap):
    for t in (256, 128, 64, 32, 16, 8):
        if t <= cap and rows % t == 0:
            return t
    return rows


def _adamw(w, g, m, v, name):
    rows, cols = w.shape
    tr = _row_tile(rows, 128)
    c1 = 1.0 / (1.0 - ADAM_B1 ** ADAM_STEP)
    c2 = 1.0 / (1.0 - ADAM_B2 ** ADAM_STEP)

    def body(w_ref, g_ref, m_ref, v_ref, d_ref, nm_ref, nv_ref):
        gv = g_ref[...]
        nm = ADAM_B1 * m_ref[...] + (1.0 - ADAM_B1) * gv
        nv = ADAM_B2 * v_ref[...] + (1.0 - ADAM_B2) * (gv * gv)
        nm_ref[...] = nm
        nv_ref[...] = nv
        d_ref[...] = -ADAM_LR * ((nm * c1) / (jnp.sqrt(nv * c2) + ADAM_EPS) + ADAM_WD * w_ref[...])

    spec = pl.BlockSpec((tr, cols), lambda i: (i, 0))
    shape = jax.ShapeDtypeStruct((rows, cols), F32)
    return _pcall(body, name=name, grid=(rows // tr,), in_specs=[spec] * 4, out_specs=[spec] * 3,
                  out_shape=[shape] * 3, compiler_params=_params("parallel"))(w, g, m, v)


def _sum_partials(parts, name):
    n, rows, cols = parts.shape
    tr = _row_tile(rows, 64)

    def body(p_ref, o_ref):
        acc = p_ref[0].astype(F32)
        for k in range(1, n):
            acc = acc + p_ref[k].astype(F32)
        o_ref[...] = acc

    return _pcall(body, name=name, grid=(rows // tr,),
                  in_specs=[pl.BlockSpec((n, tr, cols), lambda i: (0, i, 0))],
                  out_specs=pl.BlockSpec((tr, cols), lambda i: (i, 0)),
                  out_shape=jax.ShapeDtypeStruct((rows, cols), F32), compiler_params=_params("parallel"))(parts)


N_DEV = 8
RELATIONS = [(r >> 2 & 1, r >> 1 & 1, r & 1) for r in range(1, N_DEV)]


def _position():
    return lax.axis_index("x"), lax.axis_index("y"), lax.axis_index("c")


def _related(pos, rel):
    return tuple(1 - p if f else p for p, f in zip(pos, rel))


def _index(pos):
    return 4 * pos[0] + 2 * pos[1] + pos[2]


def _window(ref, kind, pos):
    px, py, pc = pos
    j = 2 * px + py
    if kind == "col":
        r, c = ref.shape
        return ref.at[pl.ds(pc * (r // 2), r // 2), pl.ds(pl.multiple_of(j * (c // 4), 128), c // 4)]
    if kind == "row":
        rj = ref.shape[0] // 4
        return ref.at[pl.ds(j * rj + pc * (rj // 2), rj // 2), :]
    r = ref.shape[1]
    return ref.at[j, pl.ds(pc * (r // 2), r // 2), :]


def _half_rows(ref, pc):
    r = ref.shape[0]
    return ref.at[pl.ds(pc * (r // 2), r // 2), :]


def _whole_shape(shard_shape, kind):
    r, c = shard_shape
    return {"col": (r, 4 * c), "row": (4 * r, c), "maj": (4, r, c)}[kind]


def _gather_weights(shards, kinds, small):
    n = len(shards)

    def body(*refs):
        shard_refs, small_ref = refs[:n], refs[n]
        whole_refs, small_out = refs[n + 1:2 * n + 1], refs[2 * n + 1]
        send_sems, recv_sems, local_sems, small_send, small_recv, small_local = refs[2 * n + 2:]
        x, y, c = _position()
        me, sibling = (x, y, c), (x, y, 1 - c)
        chips = [(1 - x, y), (x, 1 - y), (1 - x, 1 - y)]

        def copy(w, k, block, to, src=None):
            dst = _window(whole_refs[w], kinds[w], block)
            return pltpu.make_async_remote_copy(
                src_ref=dst if src is None else src, dst_ref=dst,
                send_sem=send_sems.at[w, k], recv_sem=recv_sems.at[w, k], device_id=to, device_id_type=MESH)

        def small_copy(k, chip):
            return pltpu.make_async_remote_copy(
                src_ref=small_ref, dst_ref=small_out.at[2 * x + y],
                send_sem=small_send.at[k], recv_sem=small_recv.at[k], device_id=(*chip, c), device_id_type=MESH)

        started, local = [], []
        small_mine = pltpu.make_async_copy(small_ref, small_out.at[2 * x + y], small_local)
        small_mine.start()
        local.append(small_mine)
        for k, chip in enumerate(chips):
            cp = small_copy(k, chip)
            cp.start()
            started.append(cp)
        for w in range(n):
            mine = _half_rows(shard_refs[w], c)
            lc = pltpu.make_async_copy(mine, _window(whole_refs[w], kinds[w], me), local_sems.at[w])
            lc.start()
            local.append(lc)
            first = [copy(w, 0, me, sibling, src=mine)]
            first += [copy(w, 1 + j, me, (*chip, c), src=mine) for j, chip in enumerate(chips)]
            for cp in first:
                cp.start()
            started += first
        for j, chip in enumerate(chips):
            for w in range(n):
                copy(w, 1 + j, (*chip, c), me).wait_recv()
                fwd = copy(w, 4 + j, (*chip, c), sibling)
                fwd.start()
                started.append(fwd)
        for w in range(n):
            copy(w, 0, sibling, me).wait_recv()
            for j, chip in enumerate(chips):
                copy(w, 4 + j, (*chip, 1 - c), me).wait_recv()
        for k, chip in enumerate(chips):
            pltpu.make_async_remote_copy(
                src_ref=small_ref, dst_ref=small_out.at[2 * chip[0] + chip[1]],
                send_sem=small_send.at[k], recv_sem=small_recv.at[k], device_id=me, device_id_type=MESH).wait_recv()
        for cp in started:
            cp.wait_send()
        for lc in local:
            lc.wait()

    out_shape = [jax.ShapeDtypeStruct(_whole_shape(sh.shape, kd), sh.dtype) for sh, kd in zip(shards, kinds)]
    out_shape.append(jax.ShapeDtypeStruct((4,) + small.shape, small.dtype))
    outs = _pcall(body, name="gather_weights", in_specs=[HBM_SPEC] * (n + 1), out_specs=[HBM_SPEC] * (n + 1),
                  out_shape=out_shape,
                  scratch_shapes=[pltpu.SemaphoreType.DMA((n, 7)), pltpu.SemaphoreType.DMA((n, 7)),
                                  pltpu.SemaphoreType.DMA((n,)), pltpu.SemaphoreType.DMA((3,)),
                                  pltpu.SemaphoreType.DMA((3,)), pltpu.SemaphoreType.DMA(())])(*shards, small)
    return outs[:n], outs[n]


def _scatter_grads(grads, kinds):
    n = len(grads)

    def half_shape(g, kind):
        if kind == "col":
            return (g.shape[0] // 2, g.shape[1] // 4)
        if kind == "row":
            return (g.shape[0] // 8, g.shape[1])
        return (g.shape[1] // 2, g.shape[2])

    def body(*refs):
        g_refs, out_refs = refs[:n], refs[n:2 * n]
        send_sems, recv_sems, local_sems = refs[2 * n:]
        me = _position()
        started = []
        for w in range(n):
            lc = pltpu.make_async_copy(_window(g_refs[w], kinds[w], me), out_refs[w].at[_index(me)], local_sems.at[w])
            lc.start()
            started.append(lc)
        sends = []
        for k, rel in enumerate(RELATIONS):
            peer = _related(me, rel)
            for w in range(n):
                cp = pltpu.make_async_remote_copy(
                    src_ref=_window(g_refs[w], kinds[w], peer), dst_ref=out_refs[w].at[_index(me)],
                    send_sem=send_sems.at[w, k], recv_sem=recv_sems.at[w, k], device_id=peer, device_id_type=MESH)
                cp.start()
                sends.append(cp)
        for k, rel in enumerate(RELATIONS):
            peer = _related(me, rel)
            for w in range(n):
                pltpu.make_async_remote_copy(
                    src_ref=_window(g_refs[w], kinds[w], me), dst_ref=out_refs[w].at[_index(peer)],
                    send_sem=send_sems.at[w, k], recv_sem=recv_sems.at[w, k], device_id=peer,
                    device_id_type=MESH).wait_recv()
        for cp in sends:
            cp.wait_send()
        for lc in started:
            lc.wait()

    out_shape = [jax.ShapeDtypeStruct((N_DEV,) + half_shape(g, kd), g.dtype) for g, kd in zip(grads, kinds)]
    return _pcall(body, name="scatter_grads", in_specs=[HBM_SPEC] * n, out_specs=[HBM_SPEC] * n, out_shape=out_shape,
                  scratch_shapes=[pltpu.SemaphoreType.DMA((n, 7)), pltpu.SemaphoreType.DMA((n, 7)),
                                  pltpu.SemaphoreType.DMA((n,))])(*grads)


def _swap_halves(halves):
    n = len(halves)

    def body(*refs):
        h_refs, out_refs = refs[:n], refs[n:2 * n]
        send_sems, recv_sems, local_sems = refs[2 * n:]
        x, y, c = _position()
        sibling = (x, y, 1 - c)
        cps, lcs = [], []
        for w in range(n):
            lc = pltpu.make_async_copy(h_refs[w], _half_rows(out_refs[w], c), local_sems.at[w])
            lc.start()
            lcs.append(lc)
            cp = pltpu.make_async_remote_copy(
                src_ref=h_refs[w], dst_ref=_half_rows(out_refs[w], c),
                send_sem=send_sems.at[w], recv_sem=recv_sems.at[w], device_id=sibling, device_id_type=MESH)
            cp.start()
            cps.append(cp)
        for w in range(n):
            pltpu.make_async_remote_copy(
                src_ref=h_refs[w], dst_ref=_half_rows(out_refs[w], 1 - c),
                send_sem=send_sems.at[w], recv_sem=recv_sems.at[w], device_id=sibling, device_id_type=MESH).wait_recv()
        for cp in cps:
            cp.wait_send()
        for lc in lcs:
            lc.wait()

    out_shape = [jax.ShapeDtypeStruct((2 * h.shape[0], h.shape[1]), h.dtype) for h in halves]
    return _pcall(body, name="swap_halves", in_specs=[HBM_SPEC] * n, out_specs=[HBM_SPEC] * n, out_shape=out_shape,
                  scratch_shapes=[pltpu.SemaphoreType.DMA((n,)), pltpu.SemaphoreType.DMA((n,)),
                                  pltpu.SemaphoreType.DMA((n,))])(*halves)


def _allreduce_small(v):
    rows = v.shape[0]

    def body(v_ref, o_ref, recv_ref, send_sems, recv_sems):
        me = _position()
        recv_ref[_index(me)] = v_ref[...]
        sends = []
        for k, rel in enumerate(RELATIONS):
            peer = _related(me, rel)
            cp = pltpu.make_async_remote_copy(
                src_ref=v_ref, dst_ref=recv_ref.at[_index(me)],
                send_sem=send_sems.at[k], recv_sem=recv_sems.at[k], device_id=peer, device_id_type=MESH)
            cp.start()
            sends.append(cp)
        for k, rel in enumerate(RELATIONS):
            peer = _related(me, rel)
            pltpu.make_async_remote_copy(
                src_ref=v_ref, dst_ref=recv_ref.at[_index(peer)],
                send_sem=send_sems.at[k], recv_sem=recv_sems.at[k], device_id=peer, device_id_type=MESH).wait_recv()
        for cp in sends:
            cp.wait_send()
        acc = recv_ref[0]
        for k in range(1, N_DEV):
            acc = acc + recv_ref[k]
        o_ref[...] = acc

    return _pcall(body, name="allreduce_small", in_specs=[VMEM_SPEC], out_specs=VMEM_SPEC,
                  out_shape=jax.ShapeDtypeStruct((rows, 128), F32),
                  scratch_shapes=[pltpu.VMEM((N_DEV, rows, 128), F32), pltpu.SemaphoreType.DMA((7,)),
                                  pltpu.SemaphoreType.DMA((7,))],
                  compiler_params=pltpu.CompilerParams(vmem_limit_bytes=VMEM_LIMIT_BYTES))(v)


def _pack(arrays):
    flat = []
    for a in arrays:
        a = a.reshape(-1)
        flat.append(jnp.pad(a, (0, -a.shape[0] % 128)))
    flat = jnp.concatenate(flat)
    flat = jnp.pad(flat, (0, -flat.shape[0] % 1024))
    return flat.reshape(-1, 128)


def _unpack(packed, shapes):
    flat = packed.reshape(-1)
    out, at = [], 0
    for shp in shapes:
        size = 1
        for d in shp:
            size *= d
        out.append(flat[at:at + size].reshape(shp))
        at += size + (-size % 128)
    return out


WEIGHTS = ['l0_mix_norm_g', 'l0_w_in', 'l0_sc_conv_w', 'l0_w_out', 'l0_ffn_norm_g', 'l0_ffn_up', 'l0_ffn_conv_w',
           'l0_ffn_down', 'l1_mix_norm_g', 'l1_w_in', 'l1_fox_b_f', 'l1_sg_w', 'l1_sg_b', 'l1_sg_norm_g', 'l1_w_out',
           'l1_ffn_norm_g', 'l1_ffn_up', 'l1_ffn_conv_w', 'l1_ffn_down', 'final_norm_g']
BIG = {'l0_w_in': 'col', 'l0_w_out': 'row', 'l0_ffn_up': 'col', 'l0_ffn_down': 'row',
       'l1_w_in': 'maj', 'l1_w_out': 'row', 'l1_ffn_up': 'col', 'l1_ffn_down': 'row'}
CONV = ['l0_sc_conv_w', 'l0_ffn_conv_w', 'l1_ffn_conv_w']
SMALL = [n for n in WEIGHTS if n not in BIG]
IN_CD = 5 * HALF + N_HEADS


def _ffn_forward(x, g, w_up, conv_w, w_down, tag):
    h = _rmsnorm_fwd(x, g, tag + "_norm")
    u = _matmul(h, w_up, "nn", F32, tag + "_up")
    f = _ffn_act_fwd(u, conv_w, tag + "_act")
    return _matmul(f, w_down, "nn", F32, tag + "_down", res=x), (h, u, f)


def _ffn_backward(x, g, w_up, conv_w, w_down, saved, d_out, tag):
    h, u, f = saved
    dw_down = _matmul(f, d_out, "tn", BF16, tag + "_dwdown")
    d_f = _matmul(d_out, w_down, "nt", F32, tag + "_df")
    du_gate, du_up, dcw_gate, dcw_up = _ffn_act_bwd(u, conv_w, d_f, tag + "_dact")
    du = jnp.concatenate([du_gate, du_up], axis=1)
    dw_up = _matmul(h, du, "tn", BF16, tag + "_dwup")
    dh = _matmul(du, w_up, "nt", F32, tag + "_dh")
    dx, dg = _rmsnorm_bwd(x, g, dh, d_out, tag + "_dnorm")
    return dx, dg, dw_up, jnp.concatenate([dcw_gate, dcw_up], axis=1), dw_down


def kernel(x, l0_mix_norm_g, l0_w_in, l0_sc_conv_w, l0_w_out, l0_ffn_norm_g, l0_ffn_up, l0_ffn_conv_w, l0_ffn_down, l1_mix_norm_g, l1_w_in, l1_fox_b_f, l1_sg_w, l1_sg_b, l1_sg_norm_g, l1_w_out, l1_ffn_norm_g, l1_ffn_up, l1_ffn_conv_w, l1_ffn_down, final_norm_g, loss_target, m_l0_mix_norm_g, m_l0_w_in, m_l0_sc_conv_w, m_l0_w_out, m_l0_ffn_norm_g, m_l0_ffn_up, m_l0_ffn_conv_w, m_l0_ffn_down, m_l1_mix_norm_g, m_l1_w_in, m_l1_fox_b_f, m_l1_sg_w, m_l1_sg_b, m_l1_sg_norm_g, m_l1_w_out, m_l1_ffn_norm_g, m_l1_ffn_up, m_l1_ffn_conv_w, m_l1_ffn_down, m_final_norm_g, v_l0_mix_norm_g, v_l0_w_in, v_l0_sc_conv_w, v_l0_w_out, v_l0_ffn_norm_g, v_l0_ffn_up, v_l0_ffn_conv_w, v_l0_ffn_down, v_l1_mix_norm_g, v_l1_w_in, v_l1_fox_b_f, v_l1_sg_w, v_l1_sg_b, v_l1_sg_norm_g, v_l1_w_out, v_l1_ffn_norm_g, v_l1_ffn_up, v_l1_ffn_conv_w, v_l1_ffn_down, v_final_norm_g):
    given = (l0_mix_norm_g, l0_w_in, l0_sc_conv_w, l0_w_out, l0_ffn_norm_g, l0_ffn_up, l0_ffn_conv_w, l0_ffn_down, l1_mix_norm_g, l1_w_in, l1_fox_b_f, l1_sg_w, l1_sg_b, l1_sg_norm_g, l1_w_out, l1_ffn_norm_g, l1_ffn_up, l1_ffn_conv_w, l1_ffn_down, final_norm_g)
    given_m = (m_l0_mix_norm_g, m_l0_w_in, m_l0_sc_conv_w, m_l0_w_out, m_l0_ffn_norm_g, m_l0_ffn_up, m_l0_ffn_conv_w, m_l0_ffn_down, m_l1_mix_norm_g, m_l1_w_in, m_l1_fox_b_f, m_l1_sg_w, m_l1_sg_b, m_l1_sg_norm_g, m_l1_w_out, m_l1_ffn_norm_g, m_l1_ffn_up, m_l1_ffn_conv_w, m_l1_ffn_down, m_final_norm_g)
    given_v = (v_l0_mix_norm_g, v_l0_w_in, v_l0_sc_conv_w, v_l0_w_out, v_l0_ffn_norm_g, v_l0_ffn_up, v_l0_ffn_conv_w, v_l0_ffn_down, v_l1_mix_norm_g, v_l1_w_in, v_l1_fox_b_f, v_l1_sg_w, v_l1_sg_b, v_l1_sg_norm_g, v_l1_w_out, v_l1_ffn_norm_g, v_l1_ffn_up, v_l1_ffn_conv_w, v_l1_ffn_down, v_final_norm_g)
    wt = dict(zip(WEIGHTS, given))
    mom = dict(zip(WEIGHTS, given_m))
    var = dict(zip(WEIGHTS, given_v))
    s = x.shape[1]
    t = ATT_BLOCK
    x0, target = x[0], loss_target[0]
    chip = 2 * lax.axis_index("x") + lax.axis_index("y")

    big_names = list(BIG)
    kinds = [BIG[n] for n in big_names]
    conv_widths = [wt[n].shape[1] for n in CONV]
    conv_shard = jnp.concatenate([wt[n] for n in CONV], axis=1)
    wholes, conv_all = _gather_weights([wt[n].astype(BF16) for n in big_names], kinds, conv_shard)
    full = dict(zip(big_names, wholes))
    conv_full, at = {}, 0
    for n, cw in zip(CONV, conv_widths):
        conv_full[n] = jnp.transpose(conv_all[:, :, at:at + cw], (1, 0, 2)).reshape(3, 4 * cw)
        at += cw
    w_in1 = jnp.transpose(full['l1_w_in'], (1, 0, 2)).reshape(D_MODEL, IN_CD)
    w_in1_main = w_in1[:, :5 * HALF]
    w_in1_f = jnp.pad(w_in1[:, 5 * HALF:], ((0, 0), (0, 128 - N_HEADS)))

    def vec(name):
        return wt[name].reshape(1, -1)

    h0 = _rmsnorm_fwd(x0, vec('l0_mix_norm_g'), "l0_mix_norm")
    p0 = _matmul(h0, full['l0_w_in'], "nn", F32, "l0_in")
    a_out, sb_carries = _sb_fwd(p0, "l0_sb")
    b_out = _sc_fwd(p0, conv_full['l0_sc_conv_w'], "l0_sc")
    ab0 = jnp.concatenate([a_out.astype(BF16), b_out], axis=1)
    x1 = _matmul(ab0, full['l0_w_out'], "nn", F32, "l0_out", res=x0)
    x2, ffn0_saved = _ffn_forward(x1, vec('l0_ffn_norm_g'), full['l0_ffn_up'], conv_full['l0_ffn_conv_w'],
                                  full['l0_ffn_down'], "l0_ffn")
    h2 = _rmsnorm_fwd(x2, vec('l1_mix_norm_g'), "l1_mix_norm")
    p1 = _matmul(h2, w_in1_main, "nn", F32, "l1_in")
    f_logit = _matmul(h2, w_in1_f, "nn", F32, "l1_in_f")
    b_f = jnp.pad(wt['l1_fox_b_f'], (0, 128 - N_HEADS)).reshape(1, 128)
    c_heads = _fox_prep(f_logit, b_f, "l1_fox_prep")[:, :N_HEADS].T
    c_col = c_heads[:, :, None]
    c_row = c_heads.reshape(N_HEADS, s // t, 1, t)
    sg_bias = jnp.repeat(wt['l1_sg_b'].T, HEAD, axis=1)
    sg_gain = vec('l1_sg_norm_g')
    c_out = _sg_fwd(p1, wt['l1_sg_w'], sg_bias, sg_gain, "l1_sg")
    d_out, lse = _fox_fwd(p1, c_col, c_row, "l1_fox")
    cd1 = jnp.concatenate([c_out, d_out.astype(BF16)], axis=1)
    x3 = _matmul(cd1, full['l1_w_out'], "nn", F32, "l1_out", res=x2)
    x4, ffn1_saved = _ffn_forward(x3, vec('l1_ffn_norm_g'), full['l1_ffn_up'], conv_full['l1_ffn_conv_w'],
                                  full['l1_ffn_down'], "l1_ffn")
    dx4, dg_final, loss_part = _loss_head(x4, vec('final_norm_g'), target, "loss_head")
    loss = lax.psum(loss_part[0, 0], ("x", "y", "c"))

    grads = {'final_norm_g': dg_final}
    dx3, grads['l1_ffn_norm_g'], grads['l1_ffn_up'], grads['l1_ffn_conv_w'], grads['l1_ffn_down'] = _ffn_backward(
        x3, vec('l1_ffn_norm_g'), full['l1_ffn_up'], conv_full['l1_ffn_conv_w'], full['l1_ffn_down'], ffn1_saved, dx4,
        "l1_ffn")
    grads['l1_w_out'] = _matmul(cd1, dx3, "tn", BF16, "l1_dwout")
    d_cd = _matmul(dx3, full['l1_w_out'], "nt", F32, "l1_dcd")
    du, dv, grads['l1_sg_w'], db_sg, grads['l1_sg_norm_g'] = _sg_bwd(p1, wt['l1_sg_w'], sg_bias, sg_gain, d_cd, "l1_dsg")
    grads['l1_sg_b'] = db_sg[:, :N_HEADS].T
    dq, dk, dvv, dcol, drow = _fox_bwd(p1, c_col, c_row, lse, d_cd, d_out, "l1_dfox")
    pad8 = ((0, 0), (0, 128 - N_HEADS))
    d_f_logit, d_b_f = _fox_post(jnp.pad(drow[:, :, 0].T, pad8), jnp.pad(dcol[:, ::HEAD], pad8), f_logit, b_f,
                                 "l1_fox_post")
    grads['l1_fox_b_f'] = d_b_f[0, :N_HEADS]
    dp1 = jnp.concatenate([a.astype(BF16) for a in (du, dv, dq, dk, dvv)], axis=1)
    dw_main = _matmul(h2, dp1, "tn", BF16, "l1_dwin")
    dw_f = _matmul(h2, d_f_logit, "tn", BF16, "l1_dwin_f")
    dw_in1 = jnp.concatenate([dw_main, dw_f[:, :N_HEADS]], axis=1)
    grads['l1_w_in'] = jnp.transpose(dw_in1.reshape(D_MODEL, 4, IN_CD // 4), (1, 0, 2))
    dh2 = _matmul(dp1, w_in1_main, "nt", F32, "l1_dh")
    dh2 = _matmul(d_f_logit, w_in1_f, "nt", F32, "l1_dh_f", res=dh2)
    dx2, grads['l1_mix_norm_g'] = _rmsnorm_bwd(x2, vec('l1_mix_norm_g'), dh2, dx3, "l1_dmix_norm")
    dx1, grads['l0_ffn_norm_g'], grads['l0_ffn_up'], grads['l0_ffn_conv_w'], grads['l0_ffn_down'] = _ffn_backward(
        x1, vec('l0_ffn_norm_g'), full['l0_ffn_up'], conv_full['l0_ffn_conv_w'], full['l0_ffn_down'], ffn0_saved, dx2,
        "l0_ffn")
    grads['l0_w_out'] = _matmul(ab0, dx1, "tn", BF16, "l0_dwout")
    d_ab = _matmul(dx1, full['l0_w_out'], "nt", F32, "l0_dab")
    dq0, dk0, dv0 = _sb_bwd(p0, d_ab, sb_carries, "l0_dsb")
    dgb, dgc, dhin, grads['l0_sc_conv_w'] = _sc_bwd(p0, conv_full['l0_sc_conv_w'], d_ab, "l0_dsc")
    dp0 = jnp.concatenate([a.astype(BF16) for a in (dq0, dk0, dv0, dgb, dgc, dhin)], axis=1)
    grads['l0_w_in'] = _matmul(h0, dp0, "tn", BF16, "l0_dwin")
    dh0 = _matmul(dp0, full['l0_w_in'], "nt", F32, "l0_dh")
    dx0, grads['l0_mix_norm_g'] = _rmsnorm_bwd(x0, vec('l0_mix_norm_g'), dh0, dx1, "l0_dmix_norm")

    partials = _scatter_grads([grads[n] for n in big_names], kinds)
    halves = [_sum_partials(pt, "sum_" + n) for n, pt in zip(big_names, partials)]
    shard_grads = dict(zip(big_names, _swap_halves(halves)))
    small_shapes = [conv_full[n].shape if n in CONV else wt[n].shape for n in SMALL]
    small_sum = _unpack(_allreduce_small(_pack([grads[n] for n in SMALL])), small_shapes)
    for n, g in zip(SMALL, small_sum):
        shard_grads[n] = lax.dynamic_slice_in_dim(g, chip * wt[n].shape[1], wt[n].shape[1], axis=1) if n in CONV else g

    delta, new_m, new_v = {}, {}, {}
    for n in big_names:
        delta[n], new_m[n], new_v[n] = _adamw(wt[n], shard_grads[n], mom[n], var[n], "adamw_" + n)
    shapes = [wt[n].shape for n in SMALL]
    packed = _adamw(_pack([wt[n] for n in SMALL]), _pack([shard_grads[n] for n in SMALL]),
                    _pack([mom[n] for n in SMALL]), _pack([var[n] for n in SMALL]), "adamw_small")
    for out, pk in zip((delta, new_m, new_v), packed):
        out.update(zip(SMALL, _unpack(pk, shapes)))

    return (loss, dx0[None], *[shard_grads[n] for n in WEIGHTS], *[delta[n] for n in WEIGHTS],
            *[new_m[n] for n in WEIGHTS], *[new_v[n] for n in WEIGHTS])
```

```python
import functools

import jax
import jax.numpy as jnp
from jax import lax
from jax.experimental import pallas as pl
from jax.experimental.pallas import tpu as pltpu

F32 = jnp.float32
BF16 = jnp.bfloat16

D_MODEL = 2048
HEAD = 128
N_HEADS = 8
HALF = N_HEADS * HEAD
D_FF = 5632
EPS = 1e-6
ATT_SCALE = HEAD ** -0.5
ATT_BLOCK = 256
NEG = -1e30

ADAM_LR = 0.001
ADAM_B1 = 0.9
ADAM_B2 = 0.999
ADAM_EPS = 1e-08
ADAM_WD = 0.01
ADAM_STEP = 10

VMEM_LIMIT_BYTES = 48 * 1024 * 1024
MESH = pl.DeviceIdType.MESH
HBM_SPEC = pl.BlockSpec(memory_space=pltpu.HBM)
VMEM_SPEC = pl.BlockSpec(memory_space=pltpu.VMEM)


def _pcall(body, **kw):
    return pl.pallas_call(body, **kw)


def _params(*semantics):
    return pltpu.CompilerParams(dimension_semantics=semantics, vmem_limit_bytes=VMEM_LIMIT_BYTES)


def _pick(n, cap):
    best = None
    for t in range(128, min(n, cap) + 1, 128):
        if n % t == 0:
            best = t
    return n if best is None else best


def _dot(a, b, dims):
    return lax.dot_general(a, b, (dims, ((), ())), preferred_element_type=F32)


def _dot_nn(a, b):
    return _dot(a, b, ((1,), (0,)))


def _dot_nt(a, b):
    return _dot(a, b, ((1,), (1,)))


def _dot_tn(a, b):
    return _dot(a, b, ((0,), (0,)))


def _split3(x):
    hi = x.astype(BF16)
    r1 = x - hi.astype(F32)
    mid = r1.astype(BF16)
    lo = (r1 - mid.astype(F32)).astype(BF16)
    return hi, mid, lo


def _softplus(z):
    return jnp.maximum(z, 0.0) + jnp.log1p(jnp.exp(-jnp.abs(z)))


def _log_sigmoid(z):
    return jnp.minimum(z, 0.0) - jnp.log1p(jnp.exp(-jnp.abs(z)))


_GELU_K = 0.7978845608028654


def _gelu(x):
    return 0.5 * x * (1.0 + jnp.tanh(_GELU_K * (x + 0.044715 * x * x * x)))


def _gelu_grad(x):
    t = jnp.tanh(_GELU_K * (x + 0.044715 * x * x * x))
    return 0.5 * (1.0 + t) + 0.5 * x * (1.0 - t * t) * _GELU_K * (1.0 + 3.0 * 0.044715 * x * x)


def _shift_down(x, k):
    r = lax.broadcasted_iota(jnp.int32, x.shape, 0)
    return jnp.where(r >= k, pltpu.roll(x, k, axis=0), 0.0)


def _shift_up(x, k):
    n = x.shape[0]
    r = lax.broadcasted_iota(jnp.int32, x.shape, 0)
    return jnp.where(r < n - k, pltpu.roll(x, n - k, axis=0), 0.0)


def _conv3(s, w):
    return w[0:1, :] * _shift_down(s, 2) + w[1:2, :] * _shift_down(s, 1) + w[2:3, :] * s


def _conv3_transpose(d, w):
    return w[2:3, :] * d + w[1:2, :] * _shift_up(d, 1) + w[0:1, :] * _shift_up(d, 2)


def _conv3_wgrad(d, s, dw_ref):
    dw_ref[0:1, :] = jnp.sum(d * _shift_down(s, 2), axis=0, keepdims=True)
    dw_ref[1:2, :] = jnp.sum(d * _shift_down(s, 1), axis=0, keepdims=True)
    dw_ref[2:3, :] = jnp.sum(d * s, axis=0, keepdims=True)


def _matmul(a, b, mode, out_dtype, name, res=None):
    if mode == "nn":
        (m, k), (k2, n) = a.shape, b.shape
    elif mode == "nt":
        (m, k), (n, k2) = a.shape, b.shape
    else:
        (k, m), (k2, n) = a.shape, b.shape
    assert k == k2, (a.shape, b.shape, mode)
    tm, tn, tk = _pick(m, 512), _pick(n, 512), _pick(k, 2048)
    nk = k // tk
    if mode == "tn":
        a_spec = pl.BlockSpec((tk, tm), lambda i, j, kk: (kk, i))
    else:
        a_spec = pl.BlockSpec((tm, tk), lambda i, j, kk: (i, kk))
    if mode == "nt":
        b_spec = pl.BlockSpec((tn, tk), lambda i, j, kk: (j, kk))
    else:
        b_spec = pl.BlockSpec((tk, tn), lambda i, j, kk: (kk, j))
    o_spec = pl.BlockSpec((tm, tn), lambda i, j, kk: (i, j))
    dims = {"nn": ((1,), (0,)), "nt": ((1,), (1,)), "tn": ((0,), (0,))}[mode]
    has_res = res is not None

    def body(*refs):
        a_ref, b_ref = refs[0], refs[1]
        r_ref = refs[2] if has_res else None
        o_ref = refs[3] if has_res else refs[2]
        part = _dot(a_ref[...].astype(BF16), b_ref[...].astype(BF16), dims)

        def finish(total):
            if has_res:
                total = total + r_ref[...]
            o_ref[...] = total.astype(out_dtype)

        if nk == 1:
            finish(part)
        else:
            acc_ref = refs[-1]
            kk = pl.program_id(2)

            @pl.when(kk == 0)
            def _():
                acc_ref[...] = part

            @pl.when(kk > 0)
            def _():
                acc_ref[...] += part

            @pl.when(kk == nk - 1)
            def _():
                finish(acc_ref[...])

    in_specs = [a_spec, b_spec] + ([o_spec] if has_res else [])
    args = (a, b) + ((res,) if has_res else ())
    return _pcall(
        body, name=name, grid=(m // tm, n // tn, nk),
        in_specs=in_specs, out_specs=o_spec,
        out_shape=jax.ShapeDtypeStruct((m, n), out_dtype),
        scratch_shapes=[pltpu.VMEM((tm, tn), F32)] if nk > 1 else [],
        compiler_params=_params("parallel", "parallel", "arbitrary"),
    )(*args)


ROW_TILE = 256


def _rmsnorm_fwd(x, g, name):
    s, d = x.shape

    def body(x_ref, g_ref, o_ref):
        xf = x_ref[...]
        r = lax.rsqrt(jnp.mean(xf * xf, axis=-1, keepdims=True) + EPS)
        o_ref[...] = (xf * r * g_ref[...]).astype(BF16)

    row = pl.BlockSpec((ROW_TILE, d), lambda i: (i, 0))
    vec = pl.BlockSpec((1, d), lambda i: (0, 0))
    return _pcall(body, name=name, grid=(s // ROW_TILE,), in_specs=[row, vec], out_specs=row,
                  out_shape=jax.ShapeDtypeStruct((s, d), BF16), compiler_params=_params("parallel"))(x, g)


def _rmsnorm_bwd(x, g, dh, dres, name):
    s, d = x.shape

    def body(x_ref, g_ref, dh_ref, dres_ref, dx_ref, dg_ref):
        xf = x_ref[...]
        r = lax.rsqrt(jnp.mean(xf * xf, axis=-1, keepdims=True) + EPS)
        xhat = xf * r
        dh_v = dh_ref[...]
        dxh = dh_v * g_ref[...]
        proj = jnp.mean(dxh * xhat, axis=-1, keepdims=True)
        dx_ref[...] = dres_ref[...] + r * (dxh - xhat * proj)
        part = jnp.sum(dh_v * xhat, axis=0, keepdims=True)

        @pl.when(pl.program_id(0) == 0)
        def _():
            dg_ref[...] = part

        @pl.when(pl.program_id(0) > 0)
        def _():
            dg_ref[...] += part

    row = pl.BlockSpec((ROW_TILE, d), lambda i: (i, 0))
    vec = pl.BlockSpec((1, d), lambda i: (0, 0))
    return _pcall(body, name=name, grid=(s // ROW_TILE,), in_specs=[row, vec, row, row], out_specs=[row, vec],
                  out_shape=[jax.ShapeDtypeStruct((s, d), F32), jax.ShapeDtypeStruct((1, d), F32)],
                  compiler_params=_params("arbitrary"))(x, g, dh, dres)


def _loss_head(x, g, target, name):
    s, d = x.shape

    def body(x_ref, g_ref, t_ref, dx_ref, dg_ref, loss_ref):
        xf = x_ref[...]
        r = lax.rsqrt(jnp.mean(xf * xf, axis=-1, keepdims=True) + EPS)
        xhat = xf * r
        gv = g_ref[...]
        err = xhat * gv - t_ref[...]
        dy = err * (1.0 / d)
        dxh = dy * gv
        proj = jnp.mean(dxh * xhat, axis=-1, keepdims=True)
        dx_ref[...] = r * (dxh - xhat * proj)
        dg_part = jnp.sum(dy * xhat, axis=0, keepdims=True)
        row_loss = jnp.sum(err * err, axis=-1, keepdims=True) * (0.5 / d)
        loss_part = jnp.broadcast_to(jnp.sum(row_loss, axis=0, keepdims=True), (1, 128))

        @pl.when(pl.program_id(0) == 0)
        def _():
            dg_ref[...] = dg_part
            loss_ref[...] = loss_part

        @pl.when(pl.program_id(0) > 0)
        def _():
            dg_ref[...] += dg_part
            loss_ref[...] += loss_part

    row = pl.BlockSpec((ROW_TILE, d), lambda i: (i, 0))
    vec = pl.BlockSpec((1, d), lambda i: (0, 0))
    one = pl.BlockSpec((1, 128), lambda i: (0, 0))
    return _pcall(body, name=name, grid=(s // ROW_TILE,), in_specs=[row, vec, row], out_specs=[row, vec, one],
                  out_shape=[jax.ShapeDtypeStruct((s, d), F32), jax.ShapeDtypeStruct((1, d), F32),
                             jax.ShapeDtypeStruct((1, 128), F32)],
                  compiler_params=_params("arbitrary"))(x, g, target)


def _head_specs(s, col0):
    t = ATT_BLOCK
    qspec = pl.BlockSpec((t, HEAD), lambda h, i: (i, col0[0] + h))
    kspec = pl.BlockSpec((s, HEAD), lambda h, i: (0, col0[1] + h))
    vspec = pl.BlockSpec((s, HEAD), lambda h, i: (0, col0[2] + h))
    return qspec, kspec, vspec


def _order_matrix(t, later):
    r, c = lax.broadcasted_iota(jnp.int32, (t, t), 0), lax.broadcasted_iota(jnp.int32, (t, t), 1)
    return (r > c if later else r < c).astype(BF16)


def _exact_dot(x, m):
    hi, mid, lo = _split3(x)
    return _dot_nn(hi, m) + _dot_nn(mid, m) + _dot_nn(lo, m)


def _sb_block(q, kblk, row, ks, carry_l, u):
    t = ATT_BLOCK
    z = _dot_nt(q, kblk) * ATT_SCALE
    col = ks + lax.broadcasted_iota(jnp.int32, (t, t), 1)
    mask = col < row
    sp = _softplus(z)
    l = jnp.where(mask, -sp, 0.0)
    later = _exact_dot(l, u) + carry_l
    a = jnp.where(mask, jnp.exp(z - sp + later), 0.0)
    return z, mask, l, a


def _sb_carry_spec(s):
    t = ATT_BLOCK
    return pl.BlockSpec((None, None, s // t, t, 1), lambda h, i: (h, i, 0, 0, 0))


def _sb_fwd(p, name):
    s = p.shape[0]
    t = ATT_BLOCK
    nb = s // t

    def body(q_ref, k_ref, v_ref, o_ref, cl_ref):
        i = pl.program_id(1)
        q = q_ref[...].astype(BF16)
        row = i * t + lax.broadcasted_iota(jnp.int32, (t, t), 0)
        u = _order_matrix(t, True)
        cl_ref[...] = jnp.zeros_like(cl_ref)

        def step(n, carry):
            acc, carry_l = carry
            kb = i - n
            ks = pl.multiple_of(kb * t, t)
            kblk = k_ref[pl.ds(ks, t), :].astype(BF16)
            vblk = v_ref[pl.ds(ks, t), :].astype(BF16)
            cl_ref[kb] = carry_l
            _, _, l, a = _sb_block(q, kblk, row, ks, carry_l, u)
            acc = acc + _dot_nn(a.astype(BF16), vblk)
            return acc, carry_l + jnp.sum(l, axis=1, keepdims=True)

        acc, _ = lax.fori_loop(0, i + 1, step, (jnp.zeros((t, HEAD), F32), jnp.zeros((t, 1), F32)))
        o_ref[...] = acc

    qspec, kspec, vspec = _head_specs(s, (0, N_HEADS, 2 * N_HEADS))
    ospec = pl.BlockSpec((t, HEAD), lambda h, i: (i, h))
    return _pcall(body, name=name, grid=(N_HEADS, nb), in_specs=[qspec, kspec, vspec],
                  out_specs=[ospec, _sb_carry_spec(s)],
                  out_shape=[jax.ShapeDtypeStruct((s, HALF), F32), jax.ShapeDtypeStruct((N_HEADS, nb, nb, t, 1), F32)],
                  compiler_params=_params("parallel", "parallel"))(p, p, p)


def _sb_bwd(p, d_ab, carries, name):
    s = p.shape[0]
    t = ATT_BLOCK

    def body(q_ref, k_ref, v_ref, do_ref, cl_ref, dq_ref, dk_ref, dv_ref):
        i = pl.program_id(1)

        @pl.when(i == 0)
        def _():
            dk_ref[...] = jnp.zeros_like(dk_ref)
            dv_ref[...] = jnp.zeros_like(dv_ref)

        q = q_ref[...].astype(BF16)
        do = do_ref[...].astype(BF16)
        row = i * t + lax.broadcasted_iota(jnp.int32, (t, t), 0)
        u = _order_matrix(t, True)
        lower = _order_matrix(t, False)

        def step(kb, carry):
            dq, carry_g = carry
            ks = pl.multiple_of(kb * t, t)
            kblk = k_ref[pl.ds(ks, t), :].astype(BF16)
            vblk = v_ref[pl.ds(ks, t), :].astype(BF16)
            z, mask, _, a = _sb_block(q, kblk, row, ks, cl_ref[kb], u)
            g = a * _dot_nt(do, vblk)
            earlier_g = _exact_dot(g, lower) + carry_g
            sig = jax.nn.sigmoid(z)
            dz = jnp.where(mask, g * (1.0 - sig) - sig * earlier_g, 0.0).astype(BF16)
            dv_ref[pl.ds(ks, t), :] += _dot_tn(a.astype(BF16), do)
            dk_ref[pl.ds(ks, t), :] += _dot_tn(dz, q) * ATT_SCALE
            return dq + _dot_nn(dz, kblk) * ATT_SCALE, carry_g + jnp.sum(g, axis=1, keepdims=True)

        dq, _ = lax.fori_loop(0, i + 1, step, (jnp.zeros((t, HEAD), F32), jnp.zeros((t, 1), F32)))
        dq_ref[...] = dq

    qspec, kspec, vspec = _head_specs(s, (0, N_HEADS, 2 * N_HEADS))
    blk = pl.BlockSpec((t, HEAD), lambda h, i: (i, h))
    whole = pl.BlockSpec((s, HEAD), lambda h, i: (0, h))
    shape = jax.ShapeDtypeStruct((s, HALF), F32)
    return _pcall(body, name=name, grid=(N_HEADS, s // t), in_specs=[qspec, kspec, vspec, blk, _sb_carry_spec(s)],
                  out_specs=[blk, whole, whole], out_shape=[shape, shape, shape],
                  compiler_params=_params("parallel", "arbitrary"))(p, p, p, d_ab, carries)


COL_TILE = 256


def _sc_fwd(p, w, name):
    s = p.shape[0]
    nb = HALF // COL_TILE

    def body(gb_ref, gc_ref, h_ref, w_ref, o_ref):
        conv = _conv3(gc_ref[...] * h_ref[...], w_ref[...])
        o_ref[...] = (gb_ref[...] * conv).astype(BF16)

    def col(k):
        return pl.BlockSpec((s, COL_TILE), lambda j: (0, k * nb + j))

    wspec = pl.BlockSpec((3, COL_TILE), lambda j: (0, j))
    return _pcall(body, name=name, grid=(nb,), in_specs=[col(3), col(4), col(5), wspec], out_specs=col(0),
                  out_shape=jax.ShapeDtypeStruct((s, HALF), BF16), compiler_params=_params("parallel"))(p, p, p, w)


def _sc_bwd(p, w, d_ab, name):
    s = p.shape[0]
    nb = HALF // COL_TILE

    def body(gb_ref, gc_ref, h_ref, w_ref, d_ref, dgb_ref, dgc_ref, dh_ref, dw_ref):
        gc, hin, wv, d = gc_ref[...], h_ref[...], w_ref[...], d_ref[...]
        sig = gc * hin
        dgb_ref[...] = d * _conv3(sig, wv)
        dconv = d * gb_ref[...]
        _conv3_wgrad(dconv, sig, dw_ref)
        dsig = _conv3_transpose(dconv, wv)
        dgc_ref[...] = dsig * hin
        dh_ref[...] = dsig * gc

    def col(k):
        return pl.BlockSpec((s, COL_TILE), lambda j: (0, k * nb + j))

    wspec = pl.BlockSpec((3, COL_TILE), lambda j: (0, j))
    act = jax.ShapeDtypeStruct((s, HALF), F32)
    return _pcall(body, name=name, grid=(nb,), in_specs=[col(3), col(4), col(5), wspec, col(1)],
                  out_specs=[col(0), col(0), col(0), wspec],
                  out_shape=[act, act, act, jax.ShapeDtypeStruct((3, HALF), F32)],
                  compiler_params=_params("parallel"))(p, p, p, w, d_ab)


def _ffn_act_fwd(u, w, name):
    s = u.shape[0]
    nb = D_FF // COL_TILE

    def body(ug_ref, uu_ref, wg_ref, wu_ref, o_ref):
        gate = _conv3(ug_ref[...], wg_ref[...])
        up = _conv3(uu_ref[...], wu_ref[...])
        o_ref[...] = (gate * jax.nn.sigmoid(gate) * up).astype(BF16)

    def col(k):
        return pl.BlockSpec((s, COL_TILE), lambda j: (0, k * nb + j))

    def wcol(k):
        return pl.BlockSpec((3, COL_TILE), lambda j: (0, k * nb + j))

    return _pcall(body, name=name, grid=(nb,), in_specs=[col(0), col(1), wcol(0), wcol(1)], out_specs=col(0),
                  out_shape=jax.ShapeDtypeStruct((s, D_FF), BF16),
                  compiler_params=_params("parallel"))(u, u, w, w)


def _ffn_act_bwd(u, w, d_f, name):
    s = u.shape[0]
    nb = D_FF // COL_TILE

    def body(ug_ref, uu_ref, wg_ref, wu_ref, d_ref, dug_ref, duu_ref, dwg_ref, dwu_ref):
        ug, uu, wg, wu, d = ug_ref[...], uu_ref[...], wg_ref[...], wu_ref[...], d_ref[...]
        gate = _conv3(ug, wg)
        up = _conv3(uu, wu)
        sig = jax.nn.sigmoid(gate)
        d_gate = d * up * sig * (1.0 + gate * (1.0 - sig))
        d_up = d * gate * sig
        _conv3_wgrad(d_gate, ug, dwg_ref)
        _conv3_wgrad(d_up, uu, dwu_ref)
        dug_ref[...] = _conv3_transpose(d_gate, wg).astype(BF16)
        duu_ref[...] = _conv3_transpose(d_up, wu).astype(BF16)

    def col(k):
        return pl.BlockSpec((s, COL_TILE), lambda j: (0, k * nb + j))

    def wcol(k):
        return pl.BlockSpec((3, COL_TILE), lambda j: (0, k * nb + j))

    act = jax.ShapeDtypeStruct((s, D_FF), BF16)
    wsh = jax.ShapeDtypeStruct((3, D_FF), F32)
    return _pcall(body, name=name, grid=(nb,), in_specs=[col(0), col(1), wcol(0), wcol(1), col(0)],
                  out_specs=[col(0), col(0), wcol(0), wcol(0)], out_shape=[act, act, wsh, wsh],
                  compiler_params=_params("parallel"))(u, u, w, w, d_f)


def _sg_common(u, v, g, w_ref, bias, mixed_ref):
    rows = u.shape[0]
    gu = _gelu(u)
    gv = _gelu(v)
    xc = gv - jnp.mean(gv, axis=-1, keepdims=True)
    rstd = lax.rsqrt(jnp.mean(xc * xc, axis=-1, keepdims=True) + EPS)
    xhat = xc * rstd
    vn = xhat * g
    tril = lax.broadcasted_iota(jnp.int32, (HEAD, HEAD), 0) >= lax.broadcasted_iota(jnp.int32, (HEAD, HEAD), 1)
    wts = [jnp.where(tril, w_ref[grp], 0.0).astype(BF16) for grp in range(N_HEADS)]
    for n in range(rows // HEAD):
        for grp in range(N_HEADS):
            blk = vn[n * HEAD:(n + 1) * HEAD, grp * HEAD:(grp + 1) * HEAD].astype(BF16)
            mixed_ref[n * HEAD:(n + 1) * HEAD, grp * HEAD:(grp + 1) * HEAD] = _dot_nn(wts[grp], blk)
    mixed = mixed_ref[...] + jnp.concatenate([bias] * (rows // HEAD), axis=0)
    return gu, xhat, rstd, vn, mixed, wts, tril


def _sg_fwd(p, sg_w, bias, g, name):
    s = p.shape[0]

    def body(u_ref, v_ref, w_ref, b_ref, g_ref, o_ref, mixed_ref):
        gu, _, _, _, mixed, _, _ = _sg_common(u_ref[...], v_ref[...], g_ref[...], w_ref, b_ref[...], mixed_ref)
        o_ref[...] = (gu * mixed).astype(BF16)

    def half(k):
        return pl.BlockSpec((ROW_TILE, HALF), lambda i: (i, k))

    wspec = pl.BlockSpec((N_HEADS, HEAD, HEAD), lambda i: (0, 0, 0))
    bspec = pl.BlockSpec((HEAD, HALF), lambda i: (0, 0))
    gspec = pl.BlockSpec((1, HALF), lambda i: (0, 0))
    return _pcall(body, name=name, grid=(s // ROW_TILE,), in_specs=[half(0), half(1), wspec, bspec, gspec],
                  out_specs=half(0), out_shape=jax.ShapeDtypeStruct((s, HALF), BF16),
                  scratch_shapes=[pltpu.VMEM((ROW_TILE, HALF), F32)],
                  compiler_params=_params("parallel"))(p, p, sg_w, bias, g)


def _sg_bwd(p, sg_w, bias, g, d_cd, name):
    s = p.shape[0]
    nsteps = s // ROW_TILE

    def body(u_ref, v_ref, w_ref, b_ref, g_ref, d_ref, du_ref, dv_ref, dw_ref, db_ref, dg_ref,
             mixed_ref, dvn_ref, dbias_ref):
        i = pl.program_id(0)
        u, v, gain, d = u_ref[...], v_ref[...], g_ref[...], d_ref[...]
        gu, xhat, rstd, vn, mixed, wts, tril = _sg_common(u, v, gain, w_ref, b_ref[...], mixed_ref)

        @pl.when(i == 0)
        def _():
            dw_ref[...] = jnp.zeros_like(dw_ref)
            dg_ref[...] = jnp.zeros_like(dg_ref)
            dbias_ref[...] = jnp.zeros_like(dbias_ref)

        du_ref[...] = d * mixed * _gelu_grad(u)
        dm = d * gu
        for n in range(ROW_TILE // HEAD):
            rs = slice(n * HEAD, (n + 1) * HEAD)
            dbias_ref[...] += dm[rs, :]
            for grp in range(N_HEADS):
                cs = slice(grp * HEAD, (grp + 1) * HEAD)
                dm_blk = dm[rs, cs].astype(BF16)
                dw_ref[grp] += jnp.where(tril, _dot_nt(dm_blk, vn[rs, cs].astype(BF16)), 0.0)
                dvn_ref[rs, cs] = _dot_tn(wts[grp], dm_blk)
        dvn = dvn_ref[...]
        dg_ref[...] += jnp.sum(dvn * xhat, axis=0, keepdims=True)
        dxh = dvn * gain
        d_gv = rstd * (dxh - jnp.mean(dxh, axis=-1, keepdims=True) - xhat * jnp.mean(dxh * xhat, axis=-1, keepdims=True))
        dv_ref[...] = d_gv * _gelu_grad(v)

        @pl.when(i == nsteps - 1)
        def _():
            lane = lax.broadcasted_iota(jnp.int32, (HEAD, HEAD), 1)
            out = jnp.zeros((HEAD, HEAD), F32)
            for grp in range(N_HEADS):
                tot = jnp.sum(dbias_ref[:, grp * HEAD:(grp + 1) * HEAD], axis=1, keepdims=True)
                out = out + jnp.where(lane == grp, tot, 0.0)
            db_ref[...] = out

    def half(k):
        return pl.BlockSpec((ROW_TILE, HALF), lambda i: (i, k))

    wspec = pl.BlockSpec((N_HEADS, HEAD, HEAD), lambda i: (0, 0, 0))
    bspec = pl.BlockSpec((HEAD, HALF), lambda i: (0, 0))
    gspec = pl.BlockSpec((1, HALF), lambda i: (0, 0))
    dbspec = pl.BlockSpec((HEAD, HEAD), lambda i: (0, 0))
    act = jax.ShapeDtypeStruct((s, HALF), F32)
    return _pcall(body, name=name, grid=(nsteps,), in_specs=[half(0), half(1), wspec, bspec, gspec, half(0)],
                  out_specs=[half(0), half(0), wspec, dbspec, gspec],
                  out_shape=[act, act, jax.ShapeDtypeStruct((N_HEADS, HEAD, HEAD), F32),
                             jax.ShapeDtypeStruct((HEAD, HEAD), F32), jax.ShapeDtypeStruct((1, HALF), F32)],
                  scratch_shapes=[pltpu.VMEM((ROW_TILE, HALF), F32), pltpu.VMEM((ROW_TILE, HALF), F32),
                                  pltpu.VMEM((HEAD, HALF), F32)],
                  compiler_params=_params("arbitrary"))(p, p, sg_w, bias, g, d_cd)


def _fox_prep(f, b, name):
    s = f.shape[0]
    t = ATT_BLOCK

    def body(f_ref, b_ref, c_ref):
        tri = (lax.broadcasted_iota(jnp.int32, (t, t), 0) >= lax.broadcasted_iota(jnp.int32, (t, t), 1)).astype(BF16)
        carry = jnp.zeros((1, 128), F32)
        for n in range(s // t):
            lf = _log_sigmoid(f_ref[n * t:(n + 1) * t, :] + b_ref[...])
            hi, mid, lo = _split3(lf)
            c_ref[n * t:(n + 1) * t, :] = _dot_nn(tri, hi) + _dot_nn(tri, mid) + _dot_nn(tri, lo) + carry
            carry = carry + jnp.sum(lf, axis=0, keepdims=True)

    return _pcall(body, name=name, in_specs=[VMEM_SPEC, VMEM_SPEC], out_specs=VMEM_SPEC,
                  out_shape=jax.ShapeDtypeStruct((s, 128), F32))(f, b)


def _fox_post(drow, dcol, f, b, name):
    s = f.shape[0]
    t = ATT_BLOCK

    def body(drow_ref, dcol_ref, f_ref, b_ref, df_ref, db_ref):
        tri = (lax.broadcasted_iota(jnp.int32, (t, t), 1) >= lax.broadcasted_iota(jnp.int32, (t, t), 0)).astype(BF16)
        carry = jnp.zeros((1, 128), F32)
        db = jnp.zeros((1, 128), F32)
        for n in reversed(range(s // t)):
            rs = slice(n * t, (n + 1) * t)
            dc = drow_ref[rs, :] - dcol_ref[rs, :]
            hi, mid, lo = _split3(dc)
            dlogf = _dot_nn(tri, hi) + _dot_nn(tri, mid) + _dot_nn(tri, lo) + carry
            carry = carry + jnp.sum(dc, axis=0, keepdims=True)
            df = dlogf * jax.nn.sigmoid(-(f_ref[rs, :] + b_ref[...]))
            df_ref[rs, :] = df
            db = db + jnp.sum(df, axis=0, keepdims=True)
        db_ref[...] = db

    return _pcall(body, name=name, in_specs=[VMEM_SPEC] * 4, out_specs=[VMEM_SPEC, VMEM_SPEC],
                  out_shape=[jax.ShapeDtypeStruct((s, 128), F32), jax.ShapeDtypeStruct((1, 128), F32)])(drow, dcol, f, b)


def _fox_specs(s):
    t = ATT_BLOCK
    ccol = pl.BlockSpec((None, t, 1), lambda h, i: (h, i, 0))
    crow = pl.BlockSpec((None, s // t, 1, t), lambda h, i: (h, 0, 0, 0))
    return ccol, crow


def _fox_fwd(p, c_col, c_row, name):
    s = p.shape[0]
    t = ATT_BLOCK

    def body(q_ref, k_ref, v_ref, cc_ref, cr_ref, o_ref, lse_ref):
        i = pl.program_id(1)
        q = q_ref[...].astype(BF16)
        ct = cc_ref[...]
        row = i * t + lax.broadcasted_iota(jnp.int32, (t, t), 0)

        def step(n, carry):
            acc, m, l = carry
            ks = pl.multiple_of(n * t, t)
            kblk = k_ref[pl.ds(ks, t), :].astype(BF16)
            vblk = v_ref[pl.ds(ks, t), :].astype(BF16)
            logit = _dot_nt(q, kblk) * ATT_SCALE + ct - cr_ref[n]
            col = ks + lax.broadcasted_iota(jnp.int32, (t, t), 1)
            logit = jnp.where(col <= row, logit, NEG)
            m_new = jnp.maximum(m, jnp.max(logit, axis=1, keepdims=True))
            alpha = jnp.exp(m - m_new)
            pr = jnp.exp(logit - m_new)
            l = alpha * l + jnp.sum(pr, axis=1, keepdims=True)
            acc = alpha * acc + _dot_nn(pr.astype(BF16), vblk)
            return acc, m_new, l

        init = (jnp.zeros((t, HEAD), F32), jnp.full((t, 1), NEG, F32), jnp.zeros((t, 1), F32))
        acc, m, l = lax.fori_loop(0, i + 1, step, init)
        o_ref[...] = acc / l
        lse_ref[...] = m + jnp.log(l)

    qspec, kspec, vspec = _head_specs(s, (2 * N_HEADS, 3 * N_HEADS, 4 * N_HEADS))
    ccol, crow = _fox_specs(s)
    ospec = pl.BlockSpec((t, HEAD), lambda h, i: (i, h))
    return _pcall(body, name=name, grid=(N_HEADS, s // t), in_specs=[qspec, kspec, vspec, ccol, crow],
                  out_specs=[ospec, ccol],
                  out_shape=[jax.ShapeDtypeStruct((s, HALF), F32), jax.ShapeDtypeStruct((N_HEADS, s, 1), F32)],
                  compiler_params=_params("parallel", "parallel"))(p, p, p, c_col, c_row)


def _fox_bwd(p, c_col, c_row, lse, d_cd, d_out, name):
    s = p.shape[0]
    t = ATT_BLOCK

    def body(q_ref, k_ref, v_ref, cc_ref, cr_ref, lse_ref, do_ref, o_ref, dq_ref, dk_ref, dv_ref, dcol_ref, drow_ref):
        i = pl.program_id(1)

        @pl.when(i == 0)
        def _():
            dk_ref[...] = jnp.zeros_like(dk_ref)
            dv_ref[...] = jnp.zeros_like(dv_ref)
            dcol_ref[...] = jnp.zeros_like(dcol_ref)

        q = q_ref[...].astype(BF16)
        do_f = do_ref[...]
        do = do_f.astype(BF16)
        delta = jnp.sum(do_f * o_ref[...], axis=1, keepdims=True)
        ct = cc_ref[...]
        lse_v = lse_ref[...]
        row = i * t + lax.broadcasted_iota(jnp.int32, (t, t), 0)
        ones = jnp.ones((t, HEAD), BF16)

        def step(n, carry):
            dq, drow = carry
            ks = pl.multiple_of(n * t, t)
            kblk = k_ref[pl.ds(ks, t), :].astype(BF16)
            vblk = v_ref[pl.ds(ks, t), :].astype(BF16)
            logit = _dot_nt(q, kblk) * ATT_SCALE + ct - cr_ref[n]
            col = ks + lax.broadcasted_iota(jnp.int32, (t, t), 1)
            pr = jnp.where(col <= row, jnp.exp(logit - lse_v), 0.0)
            ds = pr * (_dot_nt(do, vblk) - delta)
            dsb = ds.astype(BF16)
            dv_ref[pl.ds(ks, t), :] += _dot_tn(pr.astype(BF16), do)
            dk_ref[pl.ds(ks, t), :] += _dot_tn(dsb, q) * ATT_SCALE
            dcol_ref[pl.ds(ks, t), :] += _dot_tn(dsb, ones)
            return dq + _dot_nn(dsb, kblk) * ATT_SCALE, drow + jnp.sum(dsb.astype(F32), axis=1, keepdims=True)

        dq, drow = lax.fori_loop(0, i + 1, step, (jnp.zeros((t, HEAD), F32), jnp.zeros((t, 1), F32)))
        dq_ref[...] = dq
        drow_ref[...] = drow

    qspec, kspec, vspec = _head_specs(s, (2 * N_HEADS, 3 * N_HEADS, 4 * N_HEADS))
    ccol, crow = _fox_specs(s)
    dospec = pl.BlockSpec((t, HEAD), lambda h, i: (i, N_HEADS + h))
    blk = pl.BlockSpec((t, HEAD), lambda h, i: (i, h))
    whole = pl.BlockSpec((s, HEAD), lambda h, i: (0, h))
    shape = jax.ShapeDtypeStruct((s, HALF), F32)
    return _pcall(body, name=name, grid=(N_HEADS, s // t),
                  in_specs=[qspec, kspec, vspec, ccol, crow, ccol, dospec, blk],
                  out_specs=[blk, whole, whole, whole, ccol],
                  out_shape=[shape, shape, shape, shape, jax.ShapeDtypeStruct((N_HEADS, s, 1), F32)],
                  compiler_params=_params("parallel", "arbitrary"))(p, p, p, c_col, c_row, lse, d_cd, d_out)


def _row_tile(rows, cap):
    for t in (256, 128, 64, 32, 16, 8):
        if t <= cap and rows % t == 0:
            return t
    return rows


def _adamw(w, g, m, v, name):
    rows, cols = w.shape
    tr = _row_tile(rows, 128)
    c1 = 1.0 / (1.0 - ADAM_B1 ** ADAM_STEP)
    c2 = 1.0 / (1.0 - ADAM_B2 ** ADAM_STEP)

    def body(w_ref, g_ref, m_ref, v_ref, d_ref, nm_ref, nv_ref):
        gv = g_ref[...]
        nm = ADAM_B1 * m_ref[...] + (1.0 - ADAM_B1) * gv
        nv = ADAM_B2 * v_ref[...] + (1.0 - ADAM_B2) * (gv * gv)
        nm_ref[...] = nm
        nv_ref[...] = nv
        d_ref[...] = -ADAM_LR * ((nm * c1) / (jnp.sqrt(nv * c2) + ADAM_EPS) + ADAM_WD * w_ref[...])

    spec = pl.BlockSpec((tr, cols), lambda i: (i, 0))
    shape = jax.ShapeDtypeStruct((rows, cols), F32)
    return _pcall(body, name=name, grid=(rows // tr,), in_specs=[spec] * 4, out_specs=[spec] * 3,
                  out_shape=[shape] * 3, compiler_params=_params("parallel"))(w, g, m, v)


def _half_shape(whole_shape, kind):
    if kind == "col":
        return (whole_shape[0] // 2, whole_shape[1] // 4)
    if kind == "row":
        return (whole_shape[0] // 8, whole_shape[1])
    return (whole_shape[1] // 2, whole_shape[2])


def _own_half_spec(whole_shape, kind, tr):
    hr, hc = _half_shape(whole_shape, kind)
    nb = hr // tr
    if kind == "col":
        return pl.BlockSpec((tr, hc), lambda i, pos: (pos[1] * nb + i, pos[0]))
    if kind == "row":
        return pl.BlockSpec((tr, hc), lambda i, pos: ((2 * pos[0] + pos[1]) * nb + i, 0))
    return pl.BlockSpec((None, tr, hc), lambda i, pos: (pos[0], pos[1] * nb + i, 0))


def _sum_partials(pos, grad, landed, kind, name):
    hr, hc = _half_shape(grad.shape, kind)
    tr = _row_tile(hr, 64)

    def body(pos_ref, g_ref, p_ref, o_ref):
        acc = g_ref[...].astype(F32)
        for k in range(N_DEV - 1):
            acc = acc + p_ref[k].astype(F32)
        o_ref[...] = acc

    grid_spec = pltpu.PrefetchScalarGridSpec(
        num_scalar_prefetch=1, grid=(hr // tr,),
        in_specs=[_own_half_spec(grad.shape, kind, tr), pl.BlockSpec((N_DEV - 1, tr, hc), lambda i, pos: (0, i, 0))],
        out_specs=pl.BlockSpec((tr, hc), lambda i, pos: (i, 0)))
    return _pcall(body, name=name, grid_spec=grid_spec, out_shape=jax.ShapeDtypeStruct((hr, hc), F32),
                  compiler_params=_params("parallel"))(pos, grad, landed)


def _adamw_shard(pos, w, g_mine, g_sibling, m, v, name):
    hr, hc = g_mine.shape
    tr = _row_tile(hr, 128)
    nb = hr // tr
    c1 = 1.0 / (1.0 - ADAM_B1 ** ADAM_STEP)
    c2 = 1.0 / (1.0 - ADAM_B2 ** ADAM_STEP)

    def body(pos_ref, w_ref, gm_ref, gs_ref, m_ref, v_ref, g_ref, d_ref, nm_ref, nv_ref):
        mine = (pl.program_id(0) // nb) == pos_ref[1]
        gv = jnp.where(mine, gm_ref[...], gs_ref[...])
        nm = ADAM_B1 * m_ref[...] + (1.0 - ADAM_B1) * gv
        nv = ADAM_B2 * v_ref[...] + (1.0 - ADAM_B2) * (gv * gv)
        g_ref[...] = gv
        nm_ref[...] = nm
        nv_ref[...] = nv
        d_ref[...] = -ADAM_LR * ((nm * c1) / (jnp.sqrt(nv * c2) + ADAM_EPS) + ADAM_WD * w_ref[...])

    full = pl.BlockSpec((tr, hc), lambda i, pos: (i, 0))
    mine_spec = pl.BlockSpec((tr, hc), lambda i, pos: (jnp.clip(i - pos[1] * nb, 0, nb - 1), 0))
    sib_spec = pl.BlockSpec((tr, hc), lambda i, pos: (jnp.clip(i - (1 - pos[1]) * nb, 0, nb - 1), 0))
    grid_spec = pltpu.PrefetchScalarGridSpec(
        num_scalar_prefetch=1, grid=(2 * nb,), in_specs=[full, mine_spec, sib_spec, full, full], out_specs=[full] * 4)
    shape = jax.ShapeDtypeStruct((2 * hr, hc), F32)
    return _pcall(body, name=name, grid_spec=grid_spec, out_shape=[shape] * 4,
                  compiler_params=_params("parallel"))(pos, w, g_mine, g_sibling, m, v)


def _place_shard(pos, shard, kind, name):
    rows, cols = shard.shape
    tr = _row_tile(rows, 256)
    nb = rows // tr
    if kind == "col":
        out_spec = pl.BlockSpec((tr, cols), lambda i, pos: (i, pos[0]))
    elif kind == "row":
        out_spec = pl.BlockSpec((tr, cols), lambda i, pos: (pos[0] * nb + i, 0))
    else:
        out_spec = pl.BlockSpec((None, tr, cols), lambda i, pos: (pos[0], i, 0))

    def body(pos_ref, s_ref, o_ref):
        o_ref[...] = s_ref[...].astype(BF16)

    grid_spec = pltpu.PrefetchScalarGridSpec(
        num_scalar_prefetch=1, grid=(nb,), in_specs=[pl.BlockSpec((tr, cols), lambda i, pos: (i, 0))],
        out_specs=out_spec)
    return _pcall(body, name=name, grid_spec=grid_spec,
                  out_shape=jax.ShapeDtypeStruct(_whole_shape(shard.shape, kind), BF16),
                  compiler_params=_params("parallel"))(pos, shard)


N_DEV = 8
RELATIONS = [(r >> 2 & 1, r >> 1 & 1, r & 1) for r in range(1, N_DEV)]


def _position():
    return lax.axis_index("x"), lax.axis_index("y"), lax.axis_index("c")


def _related(pos, rel):
    return tuple(1 - p if f else p for p, f in zip(pos, rel))


def _index(pos):
    return 4 * pos[0] + 2 * pos[1] + pos[2]


def _window(ref, kind, pos):
    px, py, pc = pos
    j = 2 * px + py
    if kind == "col":
        r, c = ref.shape
        return ref.at[pl.ds(pc * (r // 2), r // 2), pl.ds(pl.multiple_of(j * (c // 4), 128), c // 4)]
    if kind == "row":
        rj = ref.shape[0] // 4
        return ref.at[pl.ds(j * rj + pc * (rj // 2), rj // 2), :]
    r = ref.shape[1]
    return ref.at[j, pl.ds(pc * (r // 2), r // 2), :]


def _half_rows(ref, pc):
    r = ref.shape[0]
    return ref.at[pl.ds(pc * (r // 2), r // 2), :]


def _whole_shape(shard_shape, kind):
    r, c = shard_shape
    return {"col": (r, 4 * c), "row": (4 * r, c), "maj": (4, r, c)}[kind]


SEM_SPEC = pl.BlockSpec(memory_space=pltpu.SEMAPHORE)
ANY_SPEC = pl.BlockSpec(memory_space=pl.ANY)
DATAFLOW = pltpu.SideEffectType.DATAFLOW_SIDE_EFFECTING
TOKEN = jax.ShapeDtypeStruct((8, 128), F32)


def _hbm(a):
    return pltpu.with_memory_space_constraint(a, pltpu.HBM)


def _tie(a, *tokens):
    return lax.optimization_barrier((a,) + tokens)[0]


def _chips(x, y):
    return [(1 - x, y), (x, 1 - y), (1 - x, 1 - y)]


def _split_start(body, name, buffers, n_sems):
    n = len(buffers)

    def wrapped(*refs):
        body(refs[:n], refs[n], refs[n + 1])
        refs[-1][...] = jnp.zeros_like(refs[-1])

    outs = _pcall(
        wrapped, name=name, in_specs=[HBM_SPEC] * n,
        out_specs=[SEM_SPEC, SEM_SPEC] + [HBM_SPEC] * n + [VMEM_SPEC],
        out_shape=[pltpu.SemaphoreType.DMA(n_sems), pltpu.SemaphoreType.DMA(n_sems)]
        + [pltpu.HBM(b.shape, b.dtype) for b in buffers] + [TOKEN],
        input_output_aliases={i: 2 + i for i in range(n)},
        compiler_params=pltpu.CompilerParams(has_side_effects=DATAFLOW))(*[_hbm(b) for b in buffers])
    return outs[0], outs[1], list(outs[2:2 + n]), outs[2 + n]


def _split_wait(body, name, buffers, send_sems, recv_sems, after):
    n = len(buffers)

    def wrapped(*refs):
        body(refs[:n], refs[n], refs[n + 1])

    outs = _pcall(
        wrapped, name=name, in_specs=[HBM_SPEC] * n + [SEM_SPEC, SEM_SPEC, ANY_SPEC],
        out_specs=[HBM_SPEC] * n, out_shape=[pltpu.HBM(b.shape, b.dtype) for b in buffers],
        input_output_aliases={i: i for i in range(n)},
        compiler_params=pltpu.CompilerParams(has_side_effects=DATAFLOW))(*buffers, send_sems, recv_sems, after)
    return list(outs)


def _gather_start(wholes, kinds, name):
    def body(w_refs, send_sems, recv_sems):
        x, y, c = _position()
        for w, ref in enumerate(w_refs):
            mine = _window(ref, kinds[w], (x, y, c))
            for k, chip in enumerate(_chips(x, y)):
                pltpu.make_async_remote_copy(src_ref=mine, dst_ref=mine, send_sem=send_sems.at[3 * w + k],
                                             recv_sem=recv_sems.at[3 * w + k], device_id=(*chip, c),
                                             device_id_type=MESH).start()

    return _split_start(body, name, wholes, (3 * len(wholes),))


def _gather_forward(wholes, kinds, send1, recv1, after, name):
    n = len(wholes)

    def wrapped(*refs):
        w_refs, s1, r1, s2, r2 = refs[:n], refs[n], refs[n + 1], refs[n + 3], refs[n + 4]
        x, y, c = _position()
        for k, chip in enumerate(_chips(x, y)):
            for w, ref in enumerate(w_refs):
                theirs = _window(ref, kinds[w], (*chip, c))
                pltpu.make_async_remote_copy(src_ref=theirs, dst_ref=theirs, send_sem=s1.at[3 * w + k],
                                             recv_sem=r1.at[3 * w + k], device_id=(*chip, c),
                                             device_id_type=MESH).wait_recv()
                pltpu.make_async_remote_copy(src_ref=theirs, dst_ref=theirs, send_sem=s2.at[3 * w + k],
                                             recv_sem=r2.at[3 * w + k], device_id=(x, y, 1 - c),
                                             device_id_type=MESH).start()
        for w, ref in enumerate(w_refs):
            mine = _window(ref, kinds[w], (x, y, c))
            for k, chip in enumerate(_chips(x, y)):
                pltpu.make_async_remote_copy(src_ref=mine, dst_ref=mine, send_sem=s1.at[3 * w + k],
                                             recv_sem=r1.at[3 * w + k], device_id=(*chip, c),
                                             device_id_type=MESH).wait_send()
        refs[-1][...] = jnp.zeros_like(refs[-1])

    outs = _pcall(
        wrapped, name=name, in_specs=[HBM_SPEC] * n + [SEM_SPEC, SEM_SPEC, ANY_SPEC],
        out_specs=[SEM_SPEC, SEM_SPEC] + [HBM_SPEC] * n + [VMEM_SPEC],
        out_shape=[pltpu.SemaphoreType.DMA((3 * n,)), pltpu.SemaphoreType.DMA((3 * n,))]
        + [pltpu.HBM(b.shape, b.dtype) for b in wholes] + [TOKEN],
        input_output_aliases={i: 2 + i for i in range(n)},
        compiler_params=pltpu.CompilerParams(has_side_effects=DATAFLOW))(*wholes, send1, recv1, after)
    return outs[0], outs[1], list(outs[2:2 + n]), outs[2 + n]


def _gather_finish(wholes, kinds, send2, recv2, after, name):
    def body(w_refs, s2, r2):
        x, y, c = _position()
        for k, chip in enumerate(_chips(x, y)):
            for w, ref in enumerate(w_refs):
                sent = _window(ref, kinds[w], (*chip, c))
                got = _window(ref, kinds[w], (*chip, 1 - c))
                pltpu.make_async_remote_copy(src_ref=sent, dst_ref=got, send_sem=s2.at[3 * w + k],
                                             recv_sem=r2.at[3 * w + k], device_id=(x, y, 1 - c),
                                             device_id_type=MESH).wait()

    return _split_wait(body, name, wholes, send2, recv2, after)


def _gather_small(small):
    def body(s_ref, o_ref, send_sems, recv_sems, local_sem):
        x, y, c = _position()
        mine = pltpu.make_async_copy(s_ref, o_ref.at[2 * x + y], local_sem)
        mine.start()
        sends = []
        for k, chip in enumerate(_chips(x, y)):
            cp = pltpu.make_async_remote_copy(src_ref=s_ref, dst_ref=o_ref.at[2 * x + y], send_sem=send_sems.at[k],
                                              recv_sem=recv_sems.at[k], device_id=(*chip, c), device_id_type=MESH)
            cp.start()
            sends.append(cp)
        for k, chip in enumerate(_chips(x, y)):
            pltpu.make_async_remote_copy(src_ref=s_ref, dst_ref=o_ref.at[2 * chip[0] + chip[1]], send_sem=send_sems.at[k],
                                         recv_sem=recv_sems.at[k], device_id=(*chip, c), device_id_type=MESH).wait_recv()
        for cp in sends:
            cp.wait_send()
        mine.wait()

    return _pcall(body, name="gather_small", in_specs=[HBM_SPEC], out_specs=HBM_SPEC,
                  out_shape=jax.ShapeDtypeStruct((4,) + small.shape, small.dtype),
                  scratch_shapes=[pltpu.SemaphoreType.DMA((3,)), pltpu.SemaphoreType.DMA((3,)),
                                  pltpu.SemaphoreType.DMA(())])(small)


def _scatter_copies(g_refs, land_refs, kinds, send_sems, recv_sems):
    me = _position()
    copies = []
    for k, rel in enumerate(RELATIONS):
        peer = _related(me, rel)
        for w, (g_ref, land_ref) in enumerate(zip(g_refs, land_refs)):
            copies.append(pltpu.make_async_remote_copy(
                src_ref=_window(g_ref, kinds[w], peer), dst_ref=land_ref.at[k],
                send_sem=send_sems.at[7 * w + k], recv_sem=recv_sems.at[7 * w + k], device_id=peer,
                device_id_type=MESH))
    return copies


def _scatter_start(grads, kinds, name):
    n = len(grads)
    lands = [lax.empty((N_DEV - 1,) + _half_shape(g.shape, kd), g.dtype) for g, kd in zip(grads, kinds)]

    def body(refs, send_sems, recv_sems):
        for cp in _scatter_copies(refs[:n], refs[n:], kinds, send_sems, recv_sems):
            cp.start()

    send, recv, thru, token = _split_start(body, name, list(grads) + lands, ((N_DEV - 1) * n,))
    return send, recv, thru[:n], thru[n:], token


def _scatter_wait(grads, lands, kinds, send, recv, after, name):
    n = len(grads)

    def body(refs, send_sems, recv_sems):
        for cp in _scatter_copies(refs[:n], refs[n:], kinds, send_sems, recv_sems):
            cp.wait()

    out = _split_wait(body, name, list(grads) + list(lands), send, recv, after)
    return out[:n], out[n:]


def _swap_start(halves, name):
    n = len(halves)
    lands = [lax.empty(h.shape, h.dtype) for h in halves]

    def body(refs, send_sems, recv_sems):
        x, y, c = _position()
        for w in range(n):
            pltpu.make_async_remote_copy(src_ref=refs[w], dst_ref=refs[n + w], send_sem=send_sems.at[w],
                                         recv_sem=recv_sems.at[w], device_id=(x, y, 1 - c), device_id_type=MESH).start()

    send, recv, thru, token = _split_start(body, name, list(halves) + lands, (n,))
    return send, recv, thru[:n], thru[n:], token


def _swap_wait(halves, lands, send, recv, after, name):
    n = len(halves)

    def body(refs, send_sems, recv_sems):
        x, y, c = _position()
        for w in range(n):
            pltpu.make_async_remote_copy(src_ref=refs[w], dst_ref=refs[n + w], send_sem=send_sems.at[w],
                                         recv_sem=recv_sems.at[w], device_id=(x, y, 1 - c), device_id_type=MESH).wait()

    out = _split_wait(body, name, list(halves) + list(lands), send, recv, after)
    return out[:n], out[n:]


def _allreduce_small(v):
    rows = v.shape[0]

    def body(v_ref, o_ref, recv_ref, send_sems, recv_sems):
        me = _position()
        recv_ref[_index(me)] = v_ref[...]
        sends = []
        for k, rel in enumerate(RELATIONS):
            peer = _related(me, rel)
            cp = pltpu.make_async_remote_copy(
                src_ref=v_ref, dst_ref=recv_ref.at[_index(me)],
                send_sem=send_sems.at[k], recv_sem=recv_sems.at[k], device_id=peer, device_id_type=MESH)
            cp.start()
            sends.append(cp)
        for k, rel in enumerate(RELATIONS):
            peer = _related(me, rel)
            pltpu.make_async_remote_copy(
                src_ref=v_ref, dst_ref=recv_ref.at[_index(peer)],
                send_sem=send_sems.at[k], recv_sem=recv_sems.at[k], device_id=peer, device_id_type=MESH).wait_recv()
        for cp in sends:
            cp.wait_send()
        acc = recv_ref[0]
        for k in range(1, N_DEV):
            acc = acc + recv_ref[k]
        o_ref[...] = acc

    return _pcall(body, name="allreduce_small", in_specs=[VMEM_SPEC], out_specs=VMEM_SPEC,
                  out_shape=jax.ShapeDtypeStruct((rows, 128), F32),
                  scratch_shapes=[pltpu.VMEM((N_DEV, rows, 128), F32), pltpu.SemaphoreType.DMA((7,)),
                                  pltpu.SemaphoreType.DMA((7,))],
                  compiler_params=pltpu.CompilerParams(vmem_limit_bytes=VMEM_LIMIT_BYTES))(v)


def _pack(arrays):
    flat = []
    for a in arrays:
        a = a.reshape(-1)
        flat.append(jnp.pad(a, (0, -a.shape[0] % 128)))
    flat = jnp.concatenate(flat)
    flat = jnp.pad(flat, (0, -flat.shape[0] % 1024))
    return flat.reshape(-1, 128)


def _unpack(packed, shapes):
    flat = packed.reshape(-1)
    out, at = [], 0
    for shp in shapes:
        size = 1
        for d in shp:
            size *= d
        out.append(flat[at:at + size].reshape(shp))
        at += size + (-size % 128)
    return out


WEIGHTS = ['l0_mix_norm_g', 'l0_w_in', 'l0_sc_conv_w', 'l0_w_out', 'l0_ffn_norm_g', 'l0_ffn_up', 'l0_ffn_conv_w',
           'l0_ffn_down', 'l1_mix_norm_g', 'l1_w_in', 'l1_fox_b_f', 'l1_sg_w', 'l1_sg_b', 'l1_sg_norm_g', 'l1_w_out',
           'l1_ffn_norm_g', 'l1_ffn_up', 'l1_ffn_conv_w', 'l1_ffn_down', 'final_norm_g']
BIG = {'l0_w_in': 'col', 'l0_w_out': 'row', 'l0_ffn_up': 'col', 'l0_ffn_down': 'row',
       'l1_w_in': 'maj', 'l1_w_out': 'row', 'l1_ffn_up': 'col', 'l1_ffn_down': 'row'}
GATHER_GROUPS = [['l0_w_in'], ['l0_w_out'], ['l0_ffn_up', 'l0_ffn_down'], ['l1_w_in', 'l1_w_out'],
                 ['l1_ffn_up', 'l1_ffn_down']]
CONV = ['l0_sc_conv_w', 'l0_ffn_conv_w', 'l1_ffn_conv_w']
SMALL = [n for n in WEIGHTS if n not in BIG]
IN_CD = 5 * HALF + N_HEADS


def _ffn_forward(x, g, weights_after, conv_w, tag):
    h = _rmsnorm_fwd(x, g, tag + "_norm")
    w_up, w_down = weights_after(h)
    u = _matmul(h, w_up, "nn", F32, tag + "_up")
    f = _ffn_act_fwd(u, conv_w, tag + "_act")
    return _matmul(f, w_down, "nn", F32, tag + "_down", res=x), (h, u, f)


def _ffn_backward(x, g, w_up, conv_w, w_down, saved, d_out, send_grads, tag):
    h, u, f = saved
    dw_down = _matmul(f, d_out, "tn", BF16, tag + "_dwdown")
    d_f = _matmul(d_out, w_down, "nt", F32, tag + "_df")
    du_gate, du_up, dcw_gate, dcw_up = _ffn_act_bwd(u, conv_w, d_f, tag + "_dact")
    du = jnp.concatenate([du_gate, du_up], axis=1)
    dw_up = _matmul(h, du, "tn", BF16, tag + "_dwup")
    du = _tie(du, send_grads(dw_up, dw_down))
    dh = _matmul(du, w_up, "nt", F32, tag + "_dh")
    dx, dg = _rmsnorm_bwd(x, g, dh, d_out, tag + "_dnorm")
    return dx, dg, jnp.concatenate([dcw_gate, dcw_up], axis=1)


def kernel(x, l0_mix_norm_g, l0_w_in, l0_sc_conv_w, l0_w_out, l0_ffn_norm_g, l0_ffn_up, l0_ffn_conv_w, l0_ffn_down, l1_mix_norm_g, l1_w_in, l1_fox_b_f, l1_sg_w, l1_sg_b, l1_sg_norm_g, l1_w_out, l1_ffn_norm_g, l1_ffn_up, l1_ffn_conv_w, l1_ffn_down, final_norm_g, loss_target, m_l0_mix_norm_g, m_l0_w_in, m_l0_sc_conv_w, m_l0_w_out, m_l0_ffn_norm_g, m_l0_ffn_up, m_l0_ffn_conv_w, m_l0_ffn_down, m_l1_mix_norm_g, m_l1_w_in, m_l1_fox_b_f, m_l1_sg_w, m_l1_sg_b, m_l1_sg_norm_g, m_l1_w_out, m_l1_ffn_norm_g, m_l1_ffn_up, m_l1_ffn_conv_w, m_l1_ffn_down, m_final_norm_g, v_l0_mix_norm_g, v_l0_w_in, v_l0_sc_conv_w, v_l0_w_out, v_l0_ffn_norm_g, v_l0_ffn_up, v_l0_ffn_conv_w, v_l0_ffn_down, v_l1_mix_norm_g, v_l1_w_in, v_l1_fox_b_f, v_l1_sg_w, v_l1_sg_b, v_l1_sg_norm_g, v_l1_w_out, v_l1_ffn_norm_g, v_l1_ffn_up, v_l1_ffn_conv_w, v_l1_ffn_down, v_final_norm_g):
    given = (l0_mix_norm_g, l0_w_in, l0_sc_conv_w, l0_w_out, l0_ffn_norm_g, l0_ffn_up, l0_ffn_conv_w, l0_ffn_down, l1_mix_norm_g, l1_w_in, l1_fox_b_f, l1_sg_w, l1_sg_b, l1_sg_norm_g, l1_w_out, l1_ffn_norm_g, l1_ffn_up, l1_ffn_conv_w, l1_ffn_down, final_norm_g)
    given_m = (m_l0_mix_norm_g, m_l0_w_in, m_l0_sc_conv_w, m_l0_w_out, m_l0_ffn_norm_g, m_l0_ffn_up, m_l0_ffn_conv_w, m_l0_ffn_down, m_l1_mix_norm_g, m_l1_w_in, m_l1_fox_b_f, m_l1_sg_w, m_l1_sg_b, m_l1_sg_norm_g, m_l1_w_out, m_l1_ffn_norm_g, m_l1_ffn_up, m_l1_ffn_conv_w, m_l1_ffn_down, m_final_norm_g)
    given_v = (v_l0_mix_norm_g, v_l0_w_in, v_l0_sc_conv_w, v_l0_w_out, v_l0_ffn_norm_g, v_l0_ffn_up, v_l0_ffn_conv_w, v_l0_ffn_down, v_l1_mix_norm_g, v_l1_w_in, v_l1_fox_b_f, v_l1_sg_w, v_l1_sg_b, v_l1_sg_norm_g, v_l1_w_out, v_l1_ffn_norm_g, v_l1_ffn_up, v_l1_ffn_conv_w, v_l1_ffn_down, v_final_norm_g)
    wt = dict(zip(WEIGHTS, given))
    mom = dict(zip(WEIGHTS, given_m))
    var = dict(zip(WEIGHTS, given_v))
    s = x.shape[1]
    t = ATT_BLOCK
    x0, target = x[0], loss_target[0]
    chip = 2 * lax.axis_index("x") + lax.axis_index("y")

    pos = jnp.stack([chip, lax.axis_index("c")]).astype(jnp.int32)
    big_names = list(BIG)

    conv_widths = [wt[n].shape[1] for n in CONV]
    conv_all = _gather_small(jnp.concatenate([wt[n] for n in CONV], axis=1))
    conv_full, at = {}, 0
    for n, cw in zip(CONV, conv_widths):
        conv_full[n] = jnp.transpose(conv_all[:, :, at:at + cw], (1, 0, 2)).reshape(3, 4 * cw)
        at += cw
    gathers, token = [], conv_all
    for gi, names in enumerate(GATHER_GROUPS):
        placed = [_place_shard(pos, wt[n], BIG[n], "place_" + n) for n in names]
        placed[0] = _tie(placed[0], token)
        send, recv, thru, token = _gather_start(placed, [BIG[n] for n in names], "gather_start_%d" % gi)
        gathers.append((send, recv, thru))
    full = {}

    def forward_gather(gi, after):
        send, recv, thru = gathers[gi]
        kinds = [BIG[n] for n in GATHER_GROUPS[gi]]
        gathers[gi] = _gather_forward(thru, kinds, send, recv, after, "gather_forward_%d" % gi)

    def finish_gather(gi, after):
        send, recv, thru, tok = gathers[gi]
        names = GATHER_GROUPS[gi]
        wholes = _gather_finish(thru, [BIG[n] for n in names], send, recv, tok if after is None else after,
                                "gather_finish_%d" % gi)
        full.update(zip(names, wholes))

    def vec(name):
        return wt[name].reshape(1, -1)

    h0 = _rmsnorm_fwd(_tie(x0, token), vec('l0_mix_norm_g'), "l0_mix_norm")
    forward_gather(0, h0)
    finish_gather(0, None)
    p0 = _matmul(h0, full['l0_w_in'], "nn", F32, "l0_in")
    forward_gather(1, p0)
    a_out, sb_carries = _sb_fwd(p0, "l0_sb")
    finish_gather(1, a_out)
    b_out = _sc_fwd(p0, conv_full['l0_sc_conv_w'], "l0_sc")
    forward_gather(2, b_out)
    ab0 = jnp.concatenate([a_out.astype(BF16), b_out], axis=1)
    x1 = _matmul(ab0, full['l0_w_out'], "nn", F32, "l0_out", res=x0)

    def ffn0_weights(h):
        finish_gather(2, h)
        return full['l0_ffn_up'], full['l0_ffn_down']

    x2, ffn0_saved = _ffn_forward(x1, vec('l0_ffn_norm_g'), ffn0_weights, conv_full['l0_ffn_conv_w'], "l0_ffn")
    forward_gather(3, ffn0_saved[2])
    h2 = _rmsnorm_fwd(x2, vec('l1_mix_norm_g'), "l1_mix_norm")
    finish_gather(3, h2)
    w_in1 = jnp.transpose(full['l1_w_in'], (1, 0, 2)).reshape(D_MODEL, IN_CD)
    w_in1_main = w_in1[:, :5 * HALF]
    w_in1_f = jnp.pad(w_in1[:, 5 * HALF:], ((0, 0), (0, 128 - N_HEADS)))
    p1 = _matmul(h2, w_in1_main, "nn", F32, "l1_in")
    f_logit = _matmul(h2, w_in1_f, "nn", F32, "l1_in_f")
    b_f = jnp.pad(wt['l1_fox_b_f'], (0, 128 - N_HEADS)).reshape(1, 128)
    c_heads = _fox_prep(f_logit, b_f, "l1_fox_prep")[:, :N_HEADS].T
    c_col = c_heads[:, :, None]
    c_row = c_heads.reshape(N_HEADS, s // t, 1, t)
    sg_bias = jnp.repeat(wt['l1_sg_b'].T, HEAD, axis=1)
    sg_gain = vec('l1_sg_norm_g')
    c_out = _sg_fwd(p1, wt['l1_sg_w'], sg_bias, sg_gain, "l1_sg")
    forward_gather(4, c_out)
    d_out, lse = _fox_fwd(p1, c_col, c_row, "l1_fox")
    cd1 = jnp.concatenate([c_out, d_out.astype(BF16)], axis=1)
    x3 = _matmul(cd1, full['l1_w_out'], "nn", F32, "l1_out", res=x2)

    def ffn1_weights(h):
        finish_gather(4, h)
        return full['l1_ffn_up'], full['l1_ffn_down']

    x4, ffn1_saved = _ffn_forward(x3, vec('l1_ffn_norm_g'), ffn1_weights, conv_full['l1_ffn_conv_w'], "l1_ffn")
    dx4, dg_final, loss_part = _loss_head(x4, vec('final_norm_g'), target, "loss_head")
    loss = lax.psum(loss_part[0, 0], ("x", "y", "c"))

    grads = {'final_norm_g': dg_final}
    scatters = []

    def send_grads(names):
        def start(*group):
            send, recv, thru, lands, tok = _scatter_start(list(group), [BIG[n] for n in names],
                                                          "scatter_start_%d" % len(scatters))
            scatters.append((names, send, recv, thru, lands))
            return tok
        return start

    dx3, grads['l1_ffn_norm_g'], grads['l1_ffn_conv_w'] = _ffn_backward(
        x3, vec('l1_ffn_norm_g'), full['l1_ffn_up'], conv_full['l1_ffn_conv_w'], full['l1_ffn_down'], ffn1_saved, dx4,
        send_grads(['l1_ffn_up', 'l1_ffn_down']), "l1_ffn")
    dw_out1 = _matmul(cd1, dx3, "tn", BF16, "l1_dwout")
    d_cd = _matmul(dx3, full['l1_w_out'], "nt", F32, "l1_dcd")
    du, dv, grads['l1_sg_w'], db_sg, grads['l1_sg_norm_g'] = _sg_bwd(p1, wt['l1_sg_w'], sg_bias, sg_gain, d_cd, "l1_dsg")
    grads['l1_sg_b'] = db_sg[:, :N_HEADS].T
    dq, dk, dvv, dcol, drow = _fox_bwd(p1, c_col, c_row, lse, d_cd, d_out, "l1_dfox")
    pad8 = ((0, 0), (0, 128 - N_HEADS))
    d_f_logit, d_b_f = _fox_post(jnp.pad(drow[:, :, 0].T, pad8), jnp.pad(dcol[:, ::HEAD], pad8), f_logit, b_f,
                                 "l1_fox_post")
    grads['l1_fox_b_f'] = d_b_f[0, :N_HEADS]
    dp1 = jnp.concatenate([a.astype(BF16) for a in (du, dv, dq, dk, dvv)], axis=1)
    dw_main = _matmul(h2, dp1, "tn", BF16, "l1_dwin")
    dw_f = _matmul(h2, d_f_logit, "tn", BF16, "l1_dwin_f")
    dw_in1 = jnp.concatenate([dw_main, dw_f[:, :N_HEADS]], axis=1)
    dw_in1 = jnp.transpose(dw_in1.reshape(D_MODEL, 4, IN_CD // 4), (1, 0, 2))
    dp1 = _tie(dp1, send_grads(['l1_w_out', 'l1_w_in'])(dw_out1, dw_in1))
    dh2 = _matmul(dp1, w_in1_main, "nt", F32, "l1_dh")
    dh2 = _matmul(d_f_logit, w_in1_f, "nt", F32, "l1_dh_f", res=dh2)
    dx2, grads['l1_mix_norm_g'] = _rmsnorm_bwd(x2, vec('l1_mix_norm_g'), dh2, dx3, "l1_dmix_norm")
    dx1, grads['l0_ffn_norm_g'], grads['l0_ffn_conv_w'] = _ffn_backward(
        x1, vec('l0_ffn_norm_g'), full['l0_ffn_up'], conv_full['l0_ffn_conv_w'], full['l0_ffn_down'], ffn0_saved, dx2,
        send_grads(['l0_ffn_up', 'l0_ffn_down']), "l0_ffn")
    dw_out0 = _matmul(ab0, dx1, "tn", BF16, "l0_dwout")
    d_ab = _matmul(dx1, full['l0_w_out'], "nt", F32, "l0_dab")
    dq0, dk0, dv0 = _sb_bwd(p0, d_ab, sb_carries, "l0_dsb")
    dgb, dgc, dhin, grads['l0_sc_conv_w'] = _sc_bwd(p0, conv_full['l0_sc_conv_w'], d_ab, "l0_dsc")
    dp0 = jnp.concatenate([a.astype(BF16) for a in (dq0, dk0, dv0, dgb, dgc, dhin)], axis=1)
    dw_in0 = _matmul(h0, dp0, "tn", BF16, "l0_dwin")
    dp0 = _tie(dp0, send_grads(['l0_w_out', 'l0_w_in'])(dw_out0, dw_in0))
    dh0 = _matmul(dp0, full['l0_w_in'], "nt", F32, "l0_dh")
    dx0, grads['l0_mix_norm_g'] = _rmsnorm_bwd(x0, vec('l0_mix_norm_g'), dh0, dx1, "l0_dmix_norm")

    shard_grads = {}
    small_shapes = [conv_full[n].shape if n in CONV else wt[n].shape for n in SMALL]
    small_all = _allreduce_small(_pack([grads[n] for n in SMALL]))
    for n, g in zip(SMALL, _unpack(small_all, small_shapes)):
        shard_grads[n] = lax.dynamic_slice_in_dim(g, chip * wt[n].shape[1], wt[n].shape[1], axis=1) if n in CONV else g

    delta, new_m, new_v = {}, {}, {}
    after, swaps = small_all, []
    for gi, (names, send, recv, thru, lands) in enumerate(scatters):
        kinds = [BIG[n] for n in names]
        g_thru, landed = _scatter_wait(thru, lands, kinds, send, recv, after, "scatter_wait_%d" % gi)
        halves = [_sum_partials(pos, g, ld, kd, "sum_" + n) for n, g, ld, kd in zip(names, g_thru, landed, kinds)]
        s_send, s_recv, h_thru, s_lands, after = _swap_start(halves, "swap_start_%d" % gi)
        swaps.append((names, s_send, s_recv, h_thru, s_lands))
    for gi, (names, s_send, s_recv, h_thru, s_lands) in enumerate(swaps):
        mine, theirs = _swap_wait(h_thru, s_lands, s_send, s_recv, after, "swap_wait_%d" % gi)
        for n, gm, gs in zip(names, mine, theirs):
            shard_grads[n], delta[n], new_m[n], new_v[n] = _adamw_shard(pos, wt[n], gm, gs, mom[n], var[n], "adamw_" + n)
            after = delta[n]
    shapes = [wt[n].shape for n in SMALL]
    packed = _adamw(_pack([wt[n] for n in SMALL]), _pack([shard_grads[n] for n in SMALL]),
                    _pack([mom[n] for n in SMALL]), _pack([var[n] for n in SMALL]), "adamw_small")
    for out, pk in zip((delta, new_m, new_v), packed):
        out.update(zip(SMALL, _unpack(pk, shapes)))

    return (loss, dx0[None], *[shard_grads[n] for n in WEIGHTS], *[delta[n] for n in WEIGHTS],
            *[new_m[n] for n in WEIGHTS], *[new_v[n] for n in WEIGHTS])
```

```python
import functools

import jax
import jax.numpy as jnp
from jax import lax
from jax.experimental import pallas as pl
from jax.experimental.pallas import tpu as pltpu

F32 = jnp.float32
BF16 = jnp.bfloat16

D_MODEL = 2048
HEAD = 128
N_HEADS = 8
HALF = N_HEADS * HEAD
D_FF = 5632
EPS = 1e-6
ATT_SCALE = HEAD ** -0.5
ATT_BLOCK = 256
NEG = -1e30

ADAM_LR = 0.001
ADAM_B1 = 0.9
ADAM_B2 = 0.999
ADAM_EPS = 1e-08
ADAM_WD = 0.01
ADAM_STEP = 10

VMEM_LIMIT_BYTES = 48 * 1024 * 1024
MESH = pl.DeviceIdType.MESH
HBM_SPEC = pl.BlockSpec(memory_space=pltpu.HBM)
VMEM_SPEC = pl.BlockSpec(memory_space=pltpu.VMEM)


def _pcall(body, after=(), **kw):
    if not after:
        return pl.pallas_call(body, **kw)
    n_in, n_after, inner = len(kw["in_specs"]), len(after), body
    kw["in_specs"] = list(kw["in_specs"]) + [pl.BlockSpec(memory_space=pl.ANY)] * n_after

    def body(*refs):
        inner(*refs[:n_in], *refs[n_in + n_after:])

    call = pl.pallas_call(body, **kw)
    return lambda *args: call(*args, *after)


def _params(*semantics):
    return pltpu.CompilerParams(dimension_semantics=semantics, vmem_limit_bytes=VMEM_LIMIT_BYTES)


def _pick(n, cap):
    best = None
    for t in range(128, min(n, cap) + 1, 128):
        if n % t == 0:
            best = t
    return n if best is None else best


def _dot(a, b, dims):
    return lax.dot_general(a, b, (dims, ((), ())), preferred_element_type=F32)


def _dot_nn(a, b):
    return _dot(a, b, ((1,), (0,)))


def _dot_nt(a, b):
    return _dot(a, b, ((1,), (1,)))


def _dot_tn(a, b):
    return _dot(a, b, ((0,), (0,)))


def _split3(x):
    hi = x.astype(BF16)
    r1 = x - hi.astype(F32)
    mid = r1.astype(BF16)
    lo = (r1 - mid.astype(F32)).astype(BF16)
    return hi, mid, lo


def _softplus(z):
    return jnp.maximum(z, 0.0) + jnp.log1p(jnp.exp(-jnp.abs(z)))


def _log_sigmoid(z):
    return jnp.minimum(z, 0.0) - jnp.log1p(jnp.exp(-jnp.abs(z)))


_GELU_K = 0.7978845608028654


def _gelu(x):
    return 0.5 * x * (1.0 + jnp.tanh(_GELU_K * (x + 0.044715 * x * x * x)))


def _gelu_grad(x):
    t = jnp.tanh(_GELU_K * (x + 0.044715 * x * x * x))
    return 0.5 * (1.0 + t) + 0.5 * x * (1.0 - t * t) * _GELU_K * (1.0 + 3.0 * 0.044715 * x * x)


def _shift_down(x, k):
    r = lax.broadcasted_iota(jnp.int32, x.shape, 0)
    return jnp.where(r >= k, pltpu.roll(x, k, axis=0), 0.0)


def _shift_up(x, k):
    n = x.shape[0]
    r = lax.broadcasted_iota(jnp.int32, x.shape, 0)
    return jnp.where(r < n - k, pltpu.roll(x, n - k, axis=0), 0.0)


def _conv3(s, w):
    return w[0:1, :] * _shift_down(s, 2) + w[1:2, :] * _shift_down(s, 1) + w[2:3, :] * s


def _conv3_transpose(d, w):
    return w[2:3, :] * d + w[1:2, :] * _shift_up(d, 1) + w[0:1, :] * _shift_up(d, 2)


def _conv3_wgrad(d, s, dw_ref):
    dw_ref[0:1, :] = jnp.sum(d * _shift_down(s, 2), axis=0, keepdims=True)
    dw_ref[1:2, :] = jnp.sum(d * _shift_down(s, 1), axis=0, keepdims=True)
    dw_ref[2:3, :] = jnp.sum(d * s, axis=0, keepdims=True)


def _matmul(a, b, mode, out_dtype, name, res=None, after=()):
    if mode == "nn":
        (m, k), (k2, n) = a.shape, b.shape
    elif mode == "nt":
        (m, k), (n, k2) = a.shape, b.shape
    else:
        (k, m), (k2, n) = a.shape, b.shape
    assert k == k2, (a.shape, b.shape, mode)
    tm, tn, tk = _pick(m, 512), _pick(n, 512), _pick(k, 2048)
    nk = k // tk
    if mode == "tn":
        a_spec = pl.BlockSpec((tk, tm), lambda i, j, kk: (kk, i))
    else:
        a_spec = pl.BlockSpec((tm, tk), lambda i, j, kk: (i, kk))
    if mode == "nt":
        b_spec = pl.BlockSpec((tn, tk), lambda i, j, kk: (j, kk))
    else:
        b_spec = pl.BlockSpec((tk, tn), lambda i, j, kk: (kk, j))
    o_spec = pl.BlockSpec((tm, tn), lambda i, j, kk: (i, j))
    dims = {"nn": ((1,), (0,)), "nt": ((1,), (1,)), "tn": ((0,), (0,))}[mode]
    has_res = res is not None

    def body(*refs):
        a_ref, b_ref = refs[0], refs[1]
        r_ref = refs[2] if has_res else None
        o_ref = refs[3] if has_res else refs[2]
        part = _dot(a_ref[...].astype(BF16), b_ref[...].astype(BF16), dims)

        def finish(total):
            if has_res:
                total = total + r_ref[...]
            o_ref[...] = total.astype(out_dtype)

        if nk == 1:
            finish(part)
        else:
            acc_ref = refs[-1]
            kk = pl.program_id(2)

            @pl.when(kk == 0)
            def _():
                acc_ref[...] = part

            @pl.when(kk > 0)
            def _():
                acc_ref[...] += part

            @pl.when(kk == nk - 1)
            def _():
                finish(acc_ref[...])

    in_specs = [a_spec, b_spec] + ([o_spec] if has_res else [])
    args = (a, b) + ((res,) if has_res else ())
    return _pcall(
        body, after=after, name=name, grid=(m // tm, n // tn, nk),
        in_specs=in_specs, out_specs=o_spec,
        out_shape=jax.ShapeDtypeStruct((m, n), out_dtype),
        scratch_shapes=[pltpu.VMEM((tm, tn), F32)] if nk > 1 else [],
        compiler_params=_params("parallel", "parallel", "arbitrary"),
    )(*args)


ROW_TILE = 256


def _rmsnorm_fwd(x, g, name, after=()):
    s, d = x.shape

    def body(x_ref, g_ref, o_ref):
        xf = x_ref[...]
        r = lax.rsqrt(jnp.mean(xf * xf, axis=-1, keepdims=True) + EPS)
        o_ref[...] = (xf * r * g_ref[...]).astype(BF16)

    row = pl.BlockSpec((ROW_TILE, d), lambda i: (i, 0))
    vec = pl.BlockSpec((1, d), lambda i: (0, 0))
    return _pcall(body, after=after, name=name, grid=(s // ROW_TILE,), in_specs=[row, vec], out_specs=row,
                  out_shape=jax.ShapeDtypeStruct((s, d), BF16), compiler_params=_params("parallel"))(x, g)


def _rmsnorm_bwd(x, g, dh, dres, name):
    s, d = x.shape

    def body(x_ref, g_ref, dh_ref, dres_ref, dx_ref, dg_ref):
        xf = x_ref[...]
        r = lax.rsqrt(jnp.mean(xf * xf, axis=-1, keepdims=True) + EPS)
        xhat = xf * r
        dh_v = dh_ref[...]
        dxh = dh_v * g_ref[...]
        proj = jnp.mean(dxh * xhat, axis=-1, keepdims=True)
        dx_ref[...] = dres_ref[...] + r * (dxh - xhat * proj)
        part = jnp.sum(dh_v * xhat, axis=0, keepdims=True)

        @pl.when(pl.program_id(0) == 0)
        def _():
            dg_ref[...] = part

        @pl.when(pl.program_id(0) > 0)
        def _():
            dg_ref[...] += part

    row = pl.BlockSpec((ROW_TILE, d), lambda i: (i, 0))
    vec = pl.BlockSpec((1, d), lambda i: (0, 0))
    return _pcall(body, name=name, grid=(s // ROW_TILE,), in_specs=[row, vec, row, row], out_specs=[row, vec],
                  out_shape=[jax.ShapeDtypeStruct((s, d), F32), jax.ShapeDtypeStruct((1, d), F32)],
                  compiler_params=_params("arbitrary"))(x, g, dh, dres)


def _loss_head(x, g, target, name):
    s, d = x.shape

    def body(x_ref, g_ref, t_ref, dx_ref, dg_ref, loss_ref):
        xf = x_ref[...]
        r = lax.rsqrt(jnp.mean(xf * xf, axis=-1, keepdims=True) + EPS)
        xhat = xf * r
        gv = g_ref[...]
        err = xhat * gv - t_ref[...]
        dy = err * (1.0 / d)
        dxh = dy * gv
        proj = jnp.mean(dxh * xhat, axis=-1, keepdims=True)
        dx_ref[...] = r * (dxh - xhat * proj)
        dg_part = jnp.sum(dy * xhat, axis=0, keepdims=True)
        row_loss = jnp.sum(err * err, axis=-1, keepdims=True) * (0.5 / d)
        loss_part = jnp.broadcast_to(jnp.sum(row_loss, axis=0, keepdims=True), (1, 128))

        @pl.when(pl.program_id(0) == 0)
        def _():
            dg_ref[...] = dg_part
            loss_ref[...] = loss_part

        @pl.when(pl.program_id(0) > 0)
        def _():
            dg_ref[...] += dg_part
            loss_ref[...] += loss_part

    row = pl.BlockSpec((ROW_TILE, d), lambda i: (i, 0))
    vec = pl.BlockSpec((1, d), lambda i: (0, 0))
    one = pl.BlockSpec((1, 128), lambda i: (0, 0))
    return _pcall(body, name=name, grid=(s // ROW_TILE,), in_specs=[row, vec, row], out_specs=[row, vec, one],
                  out_shape=[jax.ShapeDtypeStruct((s, d), F32), jax.ShapeDtypeStruct((1, d), F32),
                             jax.ShapeDtypeStruct((1, 128), F32)],
                  compiler_params=_params("arbitrary"))(x, g, target)


def _head_specs(s, col0):
    t = ATT_BLOCK
    qspec = pl.BlockSpec((t, HEAD), lambda h, i: (i, col0[0] + h))
    kspec = pl.BlockSpec((s, HEAD), lambda h, i: (0, col0[1] + h))
    vspec = pl.BlockSpec((s, HEAD), lambda h, i: (0, col0[2] + h))
    return qspec, kspec, vspec


def _order_matrix(t, later):
    r, c = lax.broadcasted_iota(jnp.int32, (t, t), 0), lax.broadcasted_iota(jnp.int32, (t, t), 1)
    return (r > c if later else r < c).astype(BF16)


def _exact_dot(x, m):
    hi, mid, lo = _split3(x)
    return _dot_nn(hi, m) + _dot_nn(mid, m) + _dot_nn(lo, m)


def _sb_block(q, kblk, row, ks, carry_l, u):
    t = ATT_BLOCK
    z = _dot_nt(q, kblk) * ATT_SCALE
    col = ks + lax.broadcasted_iota(jnp.int32, (t, t), 1)
    mask = col < row
    sp = _softplus(z)
    l = jnp.where(mask, -sp, 0.0)
    later = _exact_dot(l, u) + carry_l
    a = jnp.where(mask, jnp.exp(z - sp + later), 0.0)
    return z, mask, l, a


def _sb_carry_spec(s):
    t = ATT_BLOCK
    return pl.BlockSpec((None, None, s // t, t, 1), lambda h, i: (h, i, 0, 0, 0))


def _sb_fwd(p, name, after=()):
    s = p.shape[0]
    t = ATT_BLOCK
    nb = s // t

    def body(q_ref, k_ref, v_ref, o_ref, cl_ref):
        i = pl.program_id(1)
        q = q_ref[...].astype(BF16)
        row = i * t + lax.broadcasted_iota(jnp.int32, (t, t), 0)
        u = _order_matrix(t, True)
        cl_ref[...] = jnp.zeros_like(cl_ref)

        def step(n, carry):
            acc, carry_l = carry
            kb = i - n
            ks = pl.multiple_of(kb * t, t)
            kblk = k_ref[pl.ds(ks, t), :].astype(BF16)
            vblk = v_ref[pl.ds(ks, t), :].astype(BF16)
            cl_ref[kb] = carry_l
            _, _, l, a = _sb_block(q, kblk, row, ks, carry_l, u)
            acc = acc + _dot_nn(a.astype(BF16), vblk)
            return acc, carry_l + jnp.sum(l, axis=1, keepdims=True)

        acc, _ = lax.fori_loop(0, i + 1, step, (jnp.zeros((t, HEAD), F32), jnp.zeros((t, 1), F32)))
        o_ref[...] = acc

    qspec, kspec, vspec = _head_specs(s, (0, N_HEADS, 2 * N_HEADS))
    ospec = pl.BlockSpec((t, HEAD), lambda h, i: (i, h))
    return _pcall(body, after=after, name=name, grid=(N_HEADS, nb), in_specs=[qspec, kspec, vspec],
                  out_specs=[ospec, _sb_carry_spec(s)],
                  out_shape=[jax.ShapeDtypeStruct((s, HALF), F32), jax.ShapeDtypeStruct((N_HEADS, nb, nb, t, 1), F32)],
                  compiler_params=_params("parallel", "parallel"))(p, p, p)


def _sb_bwd(p, d_ab, carries, name):
    s = p.shape[0]
    t = ATT_BLOCK

    def body(q_ref, k_ref, v_ref, do_ref, cl_ref, dq_ref, dk_ref, dv_ref):
        i = pl.program_id(1)

        @pl.when(i == 0)
        def _():
            dk_ref[...] = jnp.zeros_like(dk_ref)
            dv_ref[...] = jnp.zeros_like(dv_ref)

        q = q_ref[...].astype(BF16)
        do = do_ref[...].astype(BF16)
        row = i * t + lax.broadcasted_iota(jnp.int32, (t, t), 0)
        u = _order_matrix(t, True)
        lower = _order_matrix(t, False)

        def step(kb, carry):
            dq, carry_g = carry
            ks = pl.multiple_of(kb * t, t)
            kblk = k_ref[pl.ds(ks, t), :].astype(BF16)
            vblk = v_ref[pl.ds(ks, t), :].astype(BF16)
            z, mask, _, a = _sb_block(q, kblk, row, ks, cl_ref[kb], u)
            g = a * _dot_nt(do, vblk)
            earlier_g = _exact_dot(g, lower) + carry_g
            sig = jax.nn.sigmoid(z)
            dz = jnp.where(mask, g * (1.0 - sig) - sig * earlier_g, 0.0).astype(BF16)
            dv_ref[pl.ds(ks, t), :] += _dot_tn(a.astype(BF16), do)
            dk_ref[pl.ds(ks, t), :] += _dot_tn(dz, q) * ATT_SCALE
            return dq + _dot_nn(dz, kblk) * ATT_SCALE, carry_g + jnp.sum(g, axis=1, keepdims=True)

        dq, _ = lax.fori_loop(0, i + 1, step, (jnp.zeros((t, HEAD), F32), jnp.zeros((t, 1), F32)))
        dq_ref[...] = dq

    qspec, kspec, vspec = _head_specs(s, (0, N_HEADS, 2 * N_HEADS))
    blk = pl.BlockSpec((t, HEAD), lambda h, i: (i, h))
    whole = pl.BlockSpec((s, HEAD), lambda h, i: (0, h))
    shape = jax.ShapeDtypeStruct((s, HALF), F32)
    return _pcall(body, name=name, grid=(N_HEADS, s // t), in_specs=[qspec, kspec, vspec, blk, _sb_carry_spec(s)],
                  out_specs=[blk, whole, whole], out_shape=[shape, shape, shape],
                  compiler_params=_params("parallel", "arbitrary"))(p, p, p, d_ab, carries)


COL_TILE = 256


def _sc_fwd(p, w, name):
    s = p.shape[0]
    nb = HALF // COL_TILE

    def body(gb_ref, gc_ref, h_ref, w_ref, o_ref):
        conv = _conv3(gc_ref[...] * h_ref[...], w_ref[...])
        o_ref[...] = (gb_ref[...] * conv).astype(BF16)

    def col(k):
        return pl.BlockSpec((s, COL_TILE), lambda j: (0, k * nb + j))

    wspec = pl.BlockSpec((3, COL_TILE), lambda j: (0, j))
    return _pcall(body, name=name, grid=(nb,), in_specs=[col(3), col(4), col(5), wspec], out_specs=col(0),
                  out_shape=jax.ShapeDtypeStruct((s, HALF), BF16), compiler_params=_params("parallel"))(p, p, p, w)


def _sc_bwd(p, w, d_ab, name):
    s = p.shape[0]
    nb = HALF // COL_TILE

    def body(gb_ref, gc_ref, h_ref, w_ref, d_ref, dgb_ref, dgc_ref, dh_ref, dw_ref):
        gc, hin, wv, d = gc_ref[...], h_ref[...], w_ref[...], d_ref[...]
        sig = gc * hin
        dgb_ref[...] = d * _conv3(sig, wv)
        dconv = d * gb_ref[...]
        _conv3_wgrad(dconv, sig, dw_ref)
        dsig = _conv3_transpose(dconv, wv)
        dgc_ref[...] = dsig * hin
        dh_ref[...] = dsig * gc

    def col(k):
        return pl.BlockSpec((s, COL_TILE), lambda j: (0, k * nb + j))

    wspec = pl.BlockSpec((3, COL_TILE), lambda j: (0, j))
    act = jax.ShapeDtypeStruct((s, HALF), F32)
    return _pcall(body, name=name, grid=(nb,), in_specs=[col(3), col(4), col(5), wspec, col(1)],
                  out_specs=[col(0), col(0), col(0), wspec],
                  out_shape=[act, act, act, jax.ShapeDtypeStruct((3, HALF), F32)],
                  compiler_params=_params("parallel"))(p, p, p, w, d_ab)


def _ffn_act_fwd(u, w, name):
    s = u.shape[0]
    nb = D_FF // COL_TILE

    def body(ug_ref, uu_ref, wg_ref, wu_ref, o_ref):
        gate = _conv3(ug_ref[...], wg_ref[...])
        up = _conv3(uu_ref[...], wu_ref[...])
        o_ref[...] = (gate * jax.nn.sigmoid(gate) * up).astype(BF16)

    def col(k):
        return pl.BlockSpec((s, COL_TILE), lambda j: (0, k * nb + j))

    def wcol(k):
        return pl.BlockSpec((3, COL_TILE), lambda j: (0, k * nb + j))

    return _pcall(body, name=name, grid=(nb,), in_specs=[col(0), col(1), wcol(0), wcol(1)], out_specs=col(0),
                  out_shape=jax.ShapeDtypeStruct((s, D_FF), BF16),
                  compiler_params=_params("parallel"))(u, u, w, w)


def _ffn_act_bwd(u, w, d_f, name):
    s = u.shape[0]
    nb = D_FF // COL_TILE

    def body(ug_ref, uu_ref, wg_ref, wu_ref, d_ref, dug_ref, duu_ref, dwg_ref, dwu_ref):
        ug, uu, wg, wu, d = ug_ref[...], uu_ref[...], wg_ref[...], wu_ref[...], d_ref[...]
        gate = _conv3(ug, wg)
        up = _conv3(uu, wu)
        sig = jax.nn.sigmoid(gate)
        d_gate = d * up * sig * (1.0 + gate * (1.0 - sig))
        d_up = d * gate * sig
        _conv3_wgrad(d_gate, ug, dwg_ref)
        _conv3_wgrad(d_up, uu, dwu_ref)
        dug_ref[...] = _conv3_transpose(d_gate, wg).astype(BF16)
        duu_ref[...] = _conv3_transpose(d_up, wu).astype(BF16)

    def col(k):
        return pl.BlockSpec((s, COL_TILE), lambda j: (0, k * nb + j))

    def wcol(k):
        return pl.BlockSpec((3, COL_TILE), lambda j: (0, k * nb + j))

    act = jax.ShapeDtypeStruct((s, D_FF), BF16)
    wsh = jax.ShapeDtypeStruct((3, D_FF), F32)
    return _pcall(body, name=name, grid=(nb,), in_specs=[col(0), col(1), wcol(0), wcol(1), col(0)],
                  out_specs=[col(0), col(0), wcol(0), wcol(0)], out_shape=[act, act, wsh, wsh],
                  compiler_params=_params("parallel"))(u, u, w, w, d_f)


def _sg_common(u, v, g, w_ref, bias, mixed_ref):
    rows = u.shape[0]
    gu = _gelu(u)
    gv = _gelu(v)
    xc = gv - jnp.mean(gv, axis=-1, keepdims=True)
    rstd = lax.rsqrt(jnp.mean(xc * xc, axis=-1, keepdims=True) + EPS)
    xhat = xc * rstd
    vn = xhat * g
    tril = lax.broadcasted_iota(jnp.int32, (HEAD, HEAD), 0) >= lax.broadcasted_iota(jnp.int32, (HEAD, HEAD), 1)
    wts = [jnp.where(tril, w_ref[grp], 0.0).astype(BF16) for grp in range(N_HEADS)]
    for n in range(rows // HEAD):
        for grp in range(N_HEADS):
            blk = vn[n * HEAD:(n + 1) * HEAD, grp * HEAD:(grp + 1) * HEAD].astype(BF16)
            mixed_ref[n * HEAD:(n + 1) * HEAD, grp * HEAD:(grp + 1) * HEAD] = _dot_nn(wts[grp], blk)
    mixed = mixed_ref[...] + jnp.concatenate([bias] * (rows // HEAD), axis=0)
    return gu, xhat, rstd, vn, mixed, wts, tril


def _sg_fwd(p, sg_w, bias, g, name):
    s = p.shape[0]

    def body(u_ref, v_ref, w_ref, b_ref, g_ref, o_ref, mixed_ref):
        gu, _, _, _, mixed, _, _ = _sg_common(u_ref[...], v_ref[...], g_ref[...], w_ref, b_ref[...], mixed_ref)
        o_ref[...] = (gu * mixed).astype(BF16)

    def half(k):
        return pl.BlockSpec((ROW_TILE, HALF), lambda i: (i, k))

    wspec = pl.BlockSpec((N_HEADS, HEAD, HEAD), lambda i: (0, 0, 0))
    bspec = pl.BlockSpec((HEAD, HALF), lambda i: (0, 0))
    gspec = pl.BlockSpec((1, HALF), lambda i: (0, 0))
    return _pcall(body, name=name, grid=(s // ROW_TILE,), in_specs=[half(0), half(1), wspec, bspec, gspec],
                  out_specs=half(0), out_shape=jax.ShapeDtypeStruct((s, HALF), BF16),
                  scratch_shapes=[pltpu.VMEM((ROW_TILE, HALF), F32)],
                  compiler_params=_params("parallel"))(p, p, sg_w, bias, g)


def _sg_bwd(p, sg_w, bias, g, d_cd, name):
    s = p.shape[0]
    nsteps = s // ROW_TILE

    def body(u_ref, v_ref, w_ref, b_ref, g_ref, d_ref, du_ref, dv_ref, dw_ref, db_ref, dg_ref,
             mixed_ref, dvn_ref, dbias_ref):
        i = pl.program_id(0)
        u, v, gain, d = u_ref[...], v_ref[...], g_ref[...], d_ref[...]
        gu, xhat, rstd, vn, mixed, wts, tril = _sg_common(u, v, gain, w_ref, b_ref[...], mixed_ref)

        @pl.when(i == 0)
        def _():
            dw_ref[...] = jnp.zeros_like(dw_ref)
            dg_ref[...] = jnp.zeros_like(dg_ref)
            dbias_ref[...] = jnp.zeros_like(dbias_ref)

        du_ref[...] = d * mixed * _gelu_grad(u)
        dm = d * gu
        for n in range(ROW_TILE // HEAD):
            rs = slice(n * HEAD, (n + 1) * HEAD)
            dbias_ref[...] += dm[rs, :]
            for grp in range(N_HEADS):
                cs = slice(grp * HEAD, (grp + 1) * HEAD)
                dm_blk = dm[rs, cs].astype(BF16)
                dw_ref[grp] += jnp.where(tril, _dot_nt(dm_blk, vn[rs, cs].astype(BF16)), 0.0)
                dvn_ref[rs, cs] = _dot_tn(wts[grp], dm_blk)
        dvn = dvn_ref[...]
        dg_ref[...] += jnp.sum(dvn * xhat, axis=0, keepdims=True)
        dxh = dvn * gain
        d_gv = rstd * (dxh - jnp.mean(dxh, axis=-1, keepdims=True) - xhat * jnp.mean(dxh * xhat, axis=-1, keepdims=True))
        dv_ref[...] = d_gv * _gelu_grad(v)

        @pl.when(i == nsteps - 1)
        def _():
            lane = lax.broadcasted_iota(jnp.int32, (HEAD, HEAD), 1)
            out = jnp.zeros((HEAD, HEAD), F32)
            for grp in range(N_HEADS):
                tot = jnp.sum(dbias_ref[:, grp * HEAD:(grp + 1) * HEAD], axis=1, keepdims=True)
                out = out + jnp.where(lane == grp, tot, 0.0)
            db_ref[...] = out

    def half(k):
        return pl.BlockSpec((ROW_TILE, HALF), lambda i: (i, k))

    wspec = pl.BlockSpec((N_HEADS, HEAD, HEAD), lambda i: (0, 0, 0))
    bspec = pl.BlockSpec((HEAD, HALF), lambda i: (0, 0))
    gspec = pl.BlockSpec((1, HALF), lambda i: (0, 0))
    dbspec = pl.BlockSpec((HEAD, HEAD), lambda i: (0, 0))
    act = jax.ShapeDtypeStruct((s, HALF), F32)
    return _pcall(body, name=name, grid=(nsteps,), in_specs=[half(0), half(1), wspec, bspec, gspec, half(0)],
                  out_specs=[half(0), half(0), wspec, dbspec, gspec],
                  out_shape=[act, act, jax.ShapeDtypeStruct((N_HEADS, HEAD, HEAD), F32),
                             jax.ShapeDtypeStruct((HEAD, HEAD), F32), jax.ShapeDtypeStruct((1, HALF), F32)],
                  scratch_shapes=[pltpu.VMEM((ROW_TILE, HALF), F32), pltpu.VMEM((ROW_TILE, HALF), F32),
                                  pltpu.VMEM((HEAD, HALF), F32)],
                  compiler_params=_params("arbitrary"))(p, p, sg_w, bias, g, d_cd)


def _fox_prep(f, b, name):
    s = f.shape[0]
    t = ATT_BLOCK

    def body(f_ref, b_ref, c_ref):
        tri = (lax.broadcasted_iota(jnp.int32, (t, t), 0) >= lax.broadcasted_iota(jnp.int32, (t, t), 1)).astype(BF16)
        carry = jnp.zeros((1, 128), F32)
        for n in range(s // t):
            lf = _log_sigmoid(f_ref[n * t:(n + 1) * t, :] + b_ref[...])
            hi, mid, lo = _split3(lf)
            c_ref[n * t:(n + 1) * t, :] = _dot_nn(tri, hi) + _dot_nn(tri, mid) + _dot_nn(tri, lo) + carry
            carry = carry + jnp.sum(lf, axis=0, keepdims=True)

    return _pcall(body, name=name, in_specs=[VMEM_SPEC, VMEM_SPEC], out_specs=VMEM_SPEC,
                  out_shape=jax.ShapeDtypeStruct((s, 128), F32))(f, b)


def _fox_post(drow, dcol, f, b, name):
    s = f.shape[0]
    t = ATT_BLOCK

    def body(drow_ref, dcol_ref, f_ref, b_ref, df_ref, db_ref):
        tri = (lax.broadcasted_iota(jnp.int32, (t, t), 1) >= lax.broadcasted_iota(jnp.int32, (t, t), 0)).astype(BF16)
        carry = jnp.zeros((1, 128), F32)
        db = jnp.zeros((1, 128), F32)
        for n in reversed(range(s // t)):
            rs = slice(n * t, (n + 1) * t)
            dc = drow_ref[rs, :] - dcol_ref[rs, :]
            hi, mid, lo = _split3(dc)
            dlogf = _dot_nn(tri, hi) + _dot_nn(tri, mid) + _dot_nn(tri, lo) + carry
            carry = carry + jnp.sum(dc, axis=0, keepdims=True)
            df = dlogf * jax.nn.sigmoid(-(f_ref[rs, :] + b_ref[...]))
            df_ref[rs, :] = df
            db = db + jnp.sum(df, axis=0, keepdims=True)
        db_ref[...] = db

    return _pcall(body, name=name, in_specs=[VMEM_SPEC] * 4, out_specs=[VMEM_SPEC, VMEM_SPEC],
                  out_shape=[jax.ShapeDtypeStruct((s, 128), F32), jax.ShapeDtypeStruct((1, 128), F32)])(drow, dcol, f, b)


def _fox_specs(s):
    t = ATT_BLOCK
    ccol = pl.BlockSpec((None, t, 1), lambda h, i: (h, i, 0))
    crow = pl.BlockSpec((None, s // t, 1, t), lambda h, i: (h, 0, 0, 0))
    return ccol, crow


def _fox_fwd(p, c_col, c_row, name, after=()):
    s = p.shape[0]
    t = ATT_BLOCK

    def body(q_ref, k_ref, v_ref, cc_ref, cr_ref, o_ref, lse_ref):
        i = pl.program_id(1)
        q = q_ref[...].astype(BF16)
        ct = cc_ref[...]
        row = i * t + lax.broadcasted_iota(jnp.int32, (t, t), 0)

        def step(n, carry):
            acc, m, l = carry
            ks = pl.multiple_of(n * t, t)
            kblk = k_ref[pl.ds(ks, t), :].astype(BF16)
            vblk = v_ref[pl.ds(ks, t), :].astype(BF16)
            logit = _dot_nt(q, kblk) * ATT_SCALE + ct - cr_ref[n]
            col = ks + lax.broadcasted_iota(jnp.int32, (t, t), 1)
            logit = jnp.where(col <= row, logit, NEG)
            m_new = jnp.maximum(m, jnp.max(logit, axis=1, keepdims=True))
            alpha = jnp.exp(m - m_new)
            pr = jnp.exp(logit - m_new)
            l = alpha * l + jnp.sum(pr, axis=1, keepdims=True)
            acc = alpha * acc + _dot_nn(pr.astype(BF16), vblk)
            return acc, m_new, l

        init = (jnp.zeros((t, HEAD), F32), jnp.full((t, 1), NEG, F32), jnp.zeros((t, 1), F32))
        acc, m, l = lax.fori_loop(0, i + 1, step, init)
        o_ref[...] = acc / l
        lse_ref[...] = m + jnp.log(l)

    qspec, kspec, vspec = _head_specs(s, (2 * N_HEADS, 3 * N_HEADS, 4 * N_HEADS))
    ccol, crow = _fox_specs(s)
    ospec = pl.BlockSpec((t, HEAD), lambda h, i: (i, h))
    return _pcall(body, after=after, name=name, grid=(N_HEADS, s // t), in_specs=[qspec, kspec, vspec, ccol, crow],
                  out_specs=[ospec, ccol],
                  out_shape=[jax.ShapeDtypeStruct((s, HALF), F32), jax.ShapeDtypeStruct((N_HEADS, s, 1), F32)],
                  compiler_params=_params("parallel", "parallel"))(p, p, p, c_col, c_row)


def _fox_bwd(p, c_col, c_row, lse, d_cd, d_out, name):
    s = p.shape[0]
    t = ATT_BLOCK

    def body(q_ref, k_ref, v_ref, cc_ref, cr_ref, lse_ref, do_ref, o_ref, dq_ref, dk_ref, dv_ref, dcol_ref, drow_ref):
        i = pl.program_id(1)

        @pl.when(i == 0)
        def _():
            dk_ref[...] = jnp.zeros_like(dk_ref)
            dv_ref[...] = jnp.zeros_like(dv_ref)
            dcol_ref[...] = jnp.zeros_like(dcol_ref)

        q = q_ref[...].astype(BF16)
        do_f = do_ref[...]
        do = do_f.astype(BF16)
        delta = jnp.sum(do_f * o_ref[...], axis=1, keepdims=True)
        ct = cc_ref[...]
        lse_v = lse_ref[...]
        row = i * t + lax.broadcasted_iota(jnp.int32, (t, t), 0)
        ones = jnp.ones((t, HEAD), BF16)

        def step(n, carry):
            dq, drow = carry
            ks = pl.multiple_of(n * t, t)
            kblk = k_ref[pl.ds(ks, t), :].astype(BF16)
            vblk = v_ref[pl.ds(ks, t), :].astype(BF16)
            logit = _dot_nt(q, kblk) * ATT_SCALE + ct - cr_ref[n]
            col = ks + lax.broadcasted_iota(jnp.int32, (t, t), 1)
            pr = jnp.where(col <= row, jnp.exp(logit - lse_v), 0.0)
            ds = pr * (_dot_nt(do, vblk) - delta)
            dsb = ds.astype(BF16)
            dv_ref[pl.ds(ks, t), :] += _dot_tn(pr.astype(BF16), do)
            dk_ref[pl.ds(ks, t), :] += _dot_tn(dsb, q) * ATT_SCALE
            dcol_ref[pl.ds(ks, t), :] += _dot_tn(dsb, ones)
            return dq + _dot_nn(dsb, kblk) * ATT_SCALE, drow + jnp.sum(dsb.astype(F32), axis=1, keepdims=True)

        dq, drow = lax.fori_loop(0, i + 1, step, (jnp.zeros((t, HEAD), F32), jnp.zeros((t, 1), F32)))
        dq_ref[...] = dq
        drow_ref[...] = drow

    qspec, kspec, vspec = _head_specs(s, (2 * N_HEADS, 3 * N_HEADS, 4 * N_HEADS))
    ccol, crow = _fox_specs(s)
    dospec = pl.BlockSpec((t, HEAD), lambda h, i: (i, N_HEADS + h))
    blk = pl.BlockSpec((t, HEAD), lambda h, i: (i, h))
    whole = pl.BlockSpec((s, HEAD), lambda h, i: (0, h))
    shape = jax.ShapeDtypeStruct((s, HALF), F32)
    return _pcall(body, name=name, grid=(N_HEADS, s // t),
                  in_specs=[qspec, kspec, vspec, ccol, crow, ccol, dospec, blk],
                  out_specs=[blk, whole, whole, whole, ccol],
                  out_shape=[shape, shape, shape, shape, jax.ShapeDtypeStruct((N_HEADS, s, 1), F32)],
                  compiler_params=_params("parallel", "arbitrary"))(p, p, p, c_col, c_row, lse, d_cd, d_out)


def _row_tile(rows, cap):
    for t in (256, 128, 64, 32, 16, 8):
        if t <= cap and rows % t == 0:
            return t
    return rows


def _adamw(w, g, m, v, name):
    rows, cols = w.shape
    tr = _row_tile(rows, 128)
    c1 = 1.0 / (1.0 - ADAM_B1 ** ADAM_STEP)
    c2 = 1.0 / (1.0 - ADAM_B2 ** ADAM_STEP)

    def body(w_ref, g_ref, m_ref, v_ref, d_ref, nm_ref, nv_ref):
        gv = g_ref[...]
        nm = ADAM_B1 * m_ref[...] + (1.0 - ADAM_B1) * gv
        nv = ADAM_B2 * v_ref[...] + (1.0 - ADAM_B2) * (gv * gv)
        nm_ref[...] = nm
        nv_ref[...] = nv
        d_ref[...] = -ADAM_LR * ((nm * c1) / (jnp.sqrt(nv * c2) + ADAM_EPS) + ADAM_WD * w_ref[...])

    spec = pl.BlockSpec((tr, cols), lambda i: (i, 0))
    shape = jax.ShapeDtypeStruct((rows, cols), F32)
    return _pcall(body, name=name, grid=(rows // tr,), in_specs=[spec] * 4, out_specs=[spec] * 3,
                  out_shape=[shape] * 3, compiler_params=_params("parallel"))(w, g, m, v)


def _half_shape(whole_shape, kind):
    if kind == "col":
        return (whole_shape[0] // 2, whole_shape[1] // 4)
    if kind == "row":
        return (whole_shape[0] // 8, whole_shape[1])
    return (whole_shape[1] // 2, whole_shape[2])


def _own_half_spec(whole_shape, kind, tr):
    hr, hc = _half_shape(whole_shape, kind)
    nb = hr // tr
    if kind == "col":
        return pl.BlockSpec((tr, hc), lambda i, pos: (pos[1] * nb + i, pos[0]))
    if kind == "row":
        return pl.BlockSpec((tr, hc), lambda i, pos: ((2 * pos[0] + pos[1]) * nb + i, 0))
    return pl.BlockSpec((None, tr, hc), lambda i, pos: (pos[0], pos[1] * nb + i, 0))


def _sum_partials(pos, grad, landed, kind, name):
    hr, hc = _half_shape(grad.shape, kind)
    tr = _row_tile(hr, 64)

    def body(pos_ref, g_ref, p_ref, o_ref):
        acc = g_ref[...].astype(F32)
        for k in range(N_DEV - 1):
            acc = acc + p_ref[k].astype(F32)
        o_ref[...] = acc

    grid_spec = pltpu.PrefetchScalarGridSpec(
        num_scalar_prefetch=1, grid=(hr // tr,),
        in_specs=[_own_half_spec(grad.shape, kind, tr), pl.BlockSpec((N_DEV - 1, tr, hc), lambda i, pos: (0, i, 0))],
        out_specs=pl.BlockSpec((tr, hc), lambda i, pos: (i, 0)))
    return _pcall(body, name=name, grid_spec=grid_spec, out_shape=jax.ShapeDtypeStruct((hr, hc), F32),
                  compiler_params=_params("parallel"))(pos, grad, landed)


def _adamw_shard(pos, w, g_mine, g_sibling, m, v, name):
    hr, hc = g_mine.shape
    tr = _row_tile(hr, 128)
    nb = hr // tr
    c1 = 1.0 / (1.0 - ADAM_B1 ** ADAM_STEP)
    c2 = 1.0 / (1.0 - ADAM_B2 ** ADAM_STEP)

    def body(pos_ref, w_ref, gm_ref, gs_ref, m_ref, v_ref, g_ref, d_ref, nm_ref, nv_ref):
        mine = (pl.program_id(0) // nb) == pos_ref[1]
        gv = jnp.where(mine, gm_ref[...], gs_ref[...])
        nm = ADAM_B1 * m_ref[...] + (1.0 - ADAM_B1) * gv
        nv = ADAM_B2 * v_ref[...] + (1.0 - ADAM_B2) * (gv * gv)
        g_ref[...] = gv
        nm_ref[...] = nm
        nv_ref[...] = nv
        d_ref[...] = -ADAM_LR * ((nm * c1) / (jnp.sqrt(nv * c2) + ADAM_EPS) + ADAM_WD * w_ref[...])

    full = pl.BlockSpec((tr, hc), lambda i, pos: (i, 0))
    mine_spec = pl.BlockSpec((tr, hc), lambda i, pos: (jnp.clip(i - pos[1] * nb, 0, nb - 1), 0))
    sib_spec = pl.BlockSpec((tr, hc), lambda i, pos: (jnp.clip(i - (1 - pos[1]) * nb, 0, nb - 1), 0))
    grid_spec = pltpu.PrefetchScalarGridSpec(
        num_scalar_prefetch=1, grid=(2 * nb,), in_specs=[full, mine_spec, sib_spec, full, full], out_specs=[full] * 4)
    shape = jax.ShapeDtypeStruct((2 * hr, hc), F32)
    return _pcall(body, name=name, grid_spec=grid_spec, out_shape=[shape] * 4,
                  compiler_params=_params("parallel"))(pos, w, g_mine, g_sibling, m, v)


def _place_shard(pos, shard, kind, name, after=()):
    rows, cols = shard.shape
    tr = _row_tile(rows, 256)
    nb = rows // tr
    if kind == "col":
        out_spec = pl.BlockSpec((tr, cols), lambda i, pos: (i, pos[0]))
    elif kind == "row":
        out_spec = pl.BlockSpec((tr, cols), lambda i, pos: (pos[0] * nb + i, 0))
    else:
        out_spec = pl.BlockSpec((None, tr, cols), lambda i, pos: (pos[0], i, 0))

    def body(pos_ref, s_ref, *rest):
        rest[-1][...] = s_ref[...].astype(BF16)

    grid_spec = pltpu.PrefetchScalarGridSpec(
        num_scalar_prefetch=1, grid=(nb,),
        in_specs=[pl.BlockSpec((tr, cols), lambda i, pos: (i, 0))] + [pl.BlockSpec(memory_space=pl.ANY)] * len(after),
        out_specs=out_spec)
    return _pcall(body, name=name, grid_spec=grid_spec,
                  out_shape=jax.ShapeDtypeStruct(_whole_shape(shard.shape, kind), BF16),
                  compiler_params=_params("parallel"))(pos, shard, *after)


N_DEV = 8
RELATIONS = [(r >> 2 & 1, r >> 1 & 1, r & 1) for r in range(1, N_DEV)]


def _position():
    return lax.axis_index("x"), lax.axis_index("y"), lax.axis_index("c")


def _related(pos, rel):
    return tuple(1 - p if f else p for p, f in zip(pos, rel))


def _index(pos):
    return 4 * pos[0] + 2 * pos[1] + pos[2]


def _window(ref, kind, pos):
    px, py, pc = pos
    j = 2 * px + py
    if kind == "col":
        r, c = ref.shape
        return ref.at[pl.ds(pc * (r // 2), r // 2), pl.ds(pl.multiple_of(j * (c // 4), 128), c // 4)]
    if kind == "row":
        rj = ref.shape[0] // 4
        return ref.at[pl.ds(j * rj + pc * (rj // 2), rj // 2), :]
    r = ref.shape[1]
    return ref.at[j, pl.ds(pc * (r // 2), r // 2), :]


def _whole_shape(shard_shape, kind):
    r, c = shard_shape
    return {"col": (r, 4 * c), "row": (4 * r, c), "maj": (4, r, c)}[kind]


SEM_SPEC = pl.BlockSpec(memory_space=pltpu.SEMAPHORE)
ANY_SPEC = pl.BlockSpec(memory_space=pl.ANY)
DATAFLOW = pltpu.SideEffectType.DATAFLOW_SIDE_EFFECTING
TOKEN = jax.ShapeDtypeStruct((8, 128), F32)


def _hbm(a):
    return pltpu.with_memory_space_constraint(a, pltpu.HBM)


def _chips(x, y):
    return [(1 - x, y), (x, 1 - y), (1 - x, 1 - y)]


def _split_start(body, name, buffers, n_sems, after=()):
    n = len(buffers)

    def wrapped(*refs):
        body(refs[:n], refs[n], refs[n + 1])
        refs[-1][...] = jnp.zeros_like(refs[-1])

    outs = _pcall(
        wrapped, after=after, name=name, in_specs=[HBM_SPEC] * n,
        out_specs=[SEM_SPEC, SEM_SPEC] + [HBM_SPEC] * n + [VMEM_SPEC],
        out_shape=[pltpu.SemaphoreType.DMA(n_sems), pltpu.SemaphoreType.DMA(n_sems)]
        + [pltpu.HBM(b.shape, b.dtype) for b in buffers] + [TOKEN],
        input_output_aliases={i: 2 + i for i in range(n)},
        compiler_params=pltpu.CompilerParams(has_side_effects=DATAFLOW))(*[_hbm(b) for b in buffers])
    return outs[0], outs[1], list(outs[2:2 + n]), outs[2 + n]


def _split_wait(body, name, buffers, send_sems, recv_sems, after):
    n = len(buffers)

    def wrapped(*refs):
        body(refs[:n], refs[n], refs[n + 1])

    outs = _pcall(
        wrapped, name=name, in_specs=[HBM_SPEC] * n + [SEM_SPEC, SEM_SPEC, ANY_SPEC],
        out_specs=[HBM_SPEC] * n, out_shape=[pltpu.HBM(b.shape, b.dtype) for b in buffers],
        input_output_aliases={i: i for i in range(n)},
        compiler_params=pltpu.CompilerParams(has_side_effects=DATAFLOW))(*buffers, send_sems, recv_sems, after)
    return list(outs)


def _gather_start(wholes, kinds, name, after=()):
    def body(w_refs, send_sems, recv_sems):
        x, y, c = _position()
        for w, ref in enumerate(w_refs):
            mine = _window(ref, kinds[w], (x, y, c))
            for k, chip in enumerate(_chips(x, y)):
                pltpu.make_async_remote_copy(src_ref=mine, dst_ref=mine, send_sem=send_sems.at[3 * w + k],
                                             recv_sem=recv_sems.at[3 * w + k], device_id=(*chip, c),
                                             device_id_type=MESH).start()

    return _split_start(body, name, wholes, (3 * len(wholes),), after)


def _gather_forward(wholes, kinds, send1, recv1, after, name):
    n = len(wholes)

    def wrapped(*refs):
        w_refs, s1, r1, s2, r2 = refs[:n], refs[n], refs[n + 1], refs[n + 3], refs[n + 4]
        x, y, c = _position()
        for k, chip in enumerate(_chips(x, y)):
            for w, ref in enumerate(w_refs):
                theirs = _window(ref, kinds[w], (*chip, c))
                pltpu.make_async_remote_copy(src_ref=theirs, dst_ref=theirs, send_sem=s1.at[3 * w + k],
                                             recv_sem=r1.at[3 * w + k], device_id=(*chip, c),
                                             device_id_type=MESH).wait_recv()
                pltpu.make_async_remote_copy(src_ref=theirs, dst_ref=theirs, send_sem=s2.at[3 * w + k],
                                             recv_sem=r2.at[3 * w + k], device_id=(x, y, 1 - c),
                                             device_id_type=MESH).start()
        for w, ref in enumerate(w_refs):
            mine = _window(ref, kinds[w], (x, y, c))
            for k, chip in enumerate(_chips(x, y)):
                pltpu.make_async_remote_copy(src_ref=mine, dst_ref=mine, send_sem=s1.at[3 * w + k],
                                             recv_sem=r1.at[3 * w + k], device_id=(*chip, c),
                                             device_id_type=MESH).wait_send()
        refs[-1][...] = jnp.zeros_like(refs[-1])

    outs = _pcall(
        wrapped, name=name, in_specs=[HBM_SPEC] * n + [SEM_SPEC, SEM_SPEC, ANY_SPEC],
        out_specs=[SEM_SPEC, SEM_SPEC] + [HBM_SPEC] * n + [VMEM_SPEC],
        out_shape=[pltpu.SemaphoreType.DMA((3 * n,)), pltpu.SemaphoreType.DMA((3 * n,))]
        + [pltpu.HBM(b.shape, b.dtype) for b in wholes] + [TOKEN],
        input_output_aliases={i: 2 + i for i in range(n)},
        compiler_params=pltpu.CompilerParams(has_side_effects=DATAFLOW))(*wholes, send1, recv1, after)
    return outs[0], outs[1], list(outs[2:2 + n]), outs[2 + n]


def _gather_finish(wholes, kinds, send2, recv2, after, name):
    def body(w_refs, s2, r2):
        x, y, c = _position()
        for k, chip in enumerate(_chips(x, y)):
            for w, ref in enumerate(w_refs):
                sent = _window(ref, kinds[w], (*chip, c))
                got = _window(ref, kinds[w], (*chip, 1 - c))
                pltpu.make_async_remote_copy(src_ref=sent, dst_ref=got, send_sem=s2.at[3 * w + k],
                                             recv_sem=r2.at[3 * w + k], device_id=(x, y, 1 - c),
                                             device_id_type=MESH).wait()

    return _split_wait(body, name, wholes, send2, recv2, after)


def _gather_small(small, after=()):
    def body(s_ref, o_ref, send_sems, recv_sems, local_sem):
        x, y, c = _position()
        mine = pltpu.make_async_copy(s_ref, o_ref.at[2 * x + y], local_sem)
        mine.start()
        sends = []
        for k, chip in enumerate(_chips(x, y)):
            cp = pltpu.make_async_remote_copy(src_ref=s_ref, dst_ref=o_ref.at[2 * x + y], send_sem=send_sems.at[k],
                                              recv_sem=recv_sems.at[k], device_id=(*chip, c), device_id_type=MESH)
            cp.start()
            sends.append(cp)
        for k, chip in enumerate(_chips(x, y)):
            pltpu.make_async_remote_copy(src_ref=s_ref, dst_ref=o_ref.at[2 * chip[0] + chip[1]], send_sem=send_sems.at[k],
                                         recv_sem=recv_sems.at[k], device_id=(*chip, c), device_id_type=MESH).wait_recv()
        for cp in sends:
            cp.wait_send()
        mine.wait()

    return _pcall(body, after=after, name="gather_small", in_specs=[HBM_SPEC], out_specs=HBM_SPEC,
                  out_shape=jax.ShapeDtypeStruct((4,) + small.shape, small.dtype),
                  scratch_shapes=[pltpu.SemaphoreType.DMA((3,)), pltpu.SemaphoreType.DMA((3,)),
                                  pltpu.SemaphoreType.DMA(())])(small)


def _scatter_copies(g_refs, land_refs, kinds, send_sems, recv_sems):
    me = _position()
    copies = []
    for k, rel in enumerate(RELATIONS):
        peer = _related(me, rel)
        for w, (g_ref, land_ref) in enumerate(zip(g_refs, land_refs)):
            copies.append(pltpu.make_async_remote_copy(
                src_ref=_window(g_ref, kinds[w], peer), dst_ref=land_ref.at[k],
                send_sem=send_sems.at[7 * w + k], recv_sem=recv_sems.at[7 * w + k], device_id=peer,
                device_id_type=MESH))
    return copies


def _scatter_start(grads, kinds, name):
    n = len(grads)
    lands = [lax.empty((N_DEV - 1,) + _half_shape(g.shape, kd), g.dtype) for g, kd in zip(grads, kinds)]

    def body(refs, send_sems, recv_sems):
        for cp in _scatter_copies(refs[:n], refs[n:], kinds, send_sems, recv_sems):
            cp.start()

    send, recv, thru, token = _split_start(body, name, list(grads) + lands, ((N_DEV - 1) * n,))
    return send, recv, thru[:n], thru[n:], token


def _scatter_wait(grads, lands, kinds, send, recv, after, name):
    n = len(grads)

    def body(refs, send_sems, recv_sems):
        for cp in _scatter_copies(refs[:n], refs[n:], kinds, send_sems, recv_sems):
            cp.wait()

    out = _split_wait(body, name, list(grads) + list(lands), send, recv, after)
    return out[:n], out[n:]


def _swap_start(halves, name):
    n = len(halves)
    lands = [lax.empty(h.shape, h.dtype) for h in halves]

    def body(refs, send_sems, recv_sems):
        x, y, c = _position()
        for w in range(n):
            pltpu.make_async_remote_copy(src_ref=refs[w], dst_ref=refs[n + w], send_sem=send_sems.at[w],
                                         recv_sem=recv_sems.at[w], device_id=(x, y, 1 - c), device_id_type=MESH).start()

    send, recv, thru, token = _split_start(body, name, list(halves) + lands, (n,))
    return send, recv, thru[:n], thru[n:], token


def _swap_wait(halves, lands, send, recv, after, name):
    n = len(halves)

    def body(refs, send_sems, recv_sems):
        x, y, c = _position()
        for w in range(n):
            pltpu.make_async_remote_copy(src_ref=refs[w], dst_ref=refs[n + w], send_sem=send_sems.at[w],
                                         recv_sem=recv_sems.at[w], device_id=(x, y, 1 - c), device_id_type=MESH).wait()

    out = _split_wait(body, name, list(halves) + list(lands), send, recv, after)
    return out[:n], out[n:]


def _allreduce_small(v, after=()):
    rows = v.shape[0]

    def body(v_ref, o_ref, recv_ref, send_sems, recv_sems):
        me = _position()
        recv_ref[_index(me)] = v_ref[...]
        sends = []
        for k, rel in enumerate(RELATIONS):
            peer = _related(me, rel)
            cp = pltpu.make_async_remote_copy(
                src_ref=v_ref, dst_ref=recv_ref.at[_index(me)],
                send_sem=send_sems.at[k], recv_sem=recv_sems.at[k], device_id=peer, device_id_type=MESH)
            cp.start()
            sends.append(cp)
        for k, rel in enumerate(RELATIONS):
            peer = _related(me, rel)
            pltpu.make_async_remote_copy(
                src_ref=v_ref, dst_ref=recv_ref.at[_index(peer)],
                send_sem=send_sems.at[k], recv_sem=recv_sems.at[k], device_id=peer, device_id_type=MESH).wait_recv()
        for cp in sends:
            cp.wait_send()
        acc = recv_ref[0]
        for k in range(1, N_DEV):
            acc = acc + recv_ref[k]
        o_ref[...] = acc

    return _pcall(body, after=after, name="allreduce_small", in_specs=[VMEM_SPEC], out_specs=VMEM_SPEC,
                  out_shape=jax.ShapeDtypeStruct((rows, 128), F32),
                  scratch_shapes=[pltpu.VMEM((N_DEV, rows, 128), F32), pltpu.SemaphoreType.DMA((7,)),
                                  pltpu.SemaphoreType.DMA((7,))],
                  compiler_params=pltpu.CompilerParams(vmem_limit_bytes=VMEM_LIMIT_BYTES))(v)


def _pack(arrays):
    flat = []
    for a in arrays:
        a = a.reshape(-1)
        flat.append(jnp.pad(a, (0, -a.shape[0] % 128)))
    flat = jnp.concatenate(flat)
    flat = jnp.pad(flat, (0, -flat.shape[0] % 1024))
    return flat.reshape(-1, 128)


def _unpack(packed, shapes):
    flat = packed.reshape(-1)
    out, at = [], 0
    for shp in shapes:
        size = 1
        for d in shp:
            size *= d
        out.append(flat[at:at + size].reshape(shp))
        at += size + (-size % 128)
    return out


WEIGHTS = ['l0_mix_norm_g', 'l0_w_in', 'l0_sc_conv_w', 'l0_w_out', 'l0_ffn_norm_g', 'l0_ffn_up', 'l0_ffn_conv_w',
           'l0_ffn_down', 'l1_mix_norm_g', 'l1_w_in', 'l1_fox_b_f', 'l1_sg_w', 'l1_sg_b', 'l1_sg_norm_g', 'l1_w_out',
           'l1_ffn_norm_g', 'l1_ffn_up', 'l1_ffn_conv_w', 'l1_ffn_down', 'final_norm_g']
BIG = {'l0_w_in': 'col', 'l0_w_out': 'row', 'l0_ffn_up': 'col', 'l0_ffn_down': 'row',
       'l1_w_in': 'maj', 'l1_w_out': 'row', 'l1_ffn_up': 'col', 'l1_ffn_down': 'row'}
GATHER_GROUPS = [['l0_w_in'], ['l0_w_out'], ['l0_ffn_up', 'l0_ffn_down'], ['l1_w_in', 'l1_w_out'],
                 ['l1_ffn_up', 'l1_ffn_down']]
CONV = ['l0_sc_conv_w', 'l0_ffn_conv_w', 'l1_ffn_conv_w']
SMALL = [n for n in WEIGHTS if n not in BIG]
IN_CD = 5 * HALF + N_HEADS


def _ffn_forward(x, g, weights_after, conv_w, tag, behind_down=None):
    h = _rmsnorm_fwd(x, g, tag + "_norm")
    w_up, w_down = weights_after(h)
    u = _matmul(h, w_up, "nn", F32, tag + "_up")
    f = _ffn_act_fwd(u, conv_w, tag + "_act")
    after = [behind_down(f)] if behind_down else ()
    return _matmul(f, w_down, "nn", F32, tag + "_down", res=x, after=after), (h, u, f)


def _ffn_backward(x, g, w_up, conv_w, w_down, saved, d_out, send_grads, tag):
    h, u, f = saved
    dw_down = _matmul(f, d_out, "tn", BF16, tag + "_dwdown")
    d_f = _matmul(d_out, w_down, "nt", F32, tag + "_df")
    du_gate, du_up, dcw_gate, dcw_up = _ffn_act_bwd(u, conv_w, d_f, tag + "_dact")
    du = jnp.concatenate([du_gate, du_up], axis=1)
    dw_up = _matmul(h, du, "tn", BF16, tag + "_dwup")
    dh = _matmul(du, w_up, "nt", F32, tag + "_dh", after=[send_grads(dw_up, dw_down)])
    dx, dg = _rmsnorm_bwd(x, g, dh, d_out, tag + "_dnorm")
    return dx, dg, jnp.concatenate([dcw_gate, dcw_up], axis=1)


def kernel(x, l0_mix_norm_g, l0_w_in, l0_sc_conv_w, l0_w_out, l0_ffn_norm_g, l0_ffn_up, l0_ffn_conv_w, l0_ffn_down, l1_mix_norm_g, l1_w_in, l1_fox_b_f, l1_sg_w, l1_sg_b, l1_sg_norm_g, l1_w_out, l1_ffn_norm_g, l1_ffn_up, l1_ffn_conv_w, l1_ffn_down, final_norm_g, loss_target, m_l0_mix_norm_g, m_l0_w_in, m_l0_sc_conv_w, m_l0_w_out, m_l0_ffn_norm_g, m_l0_ffn_up, m_l0_ffn_conv_w, m_l0_ffn_down, m_l1_mix_norm_g, m_l1_w_in, m_l1_fox_b_f, m_l1_sg_w, m_l1_sg_b, m_l1_sg_norm_g, m_l1_w_out, m_l1_ffn_norm_g, m_l1_ffn_up, m_l1_ffn_conv_w, m_l1_ffn_down, m_final_norm_g, v_l0_mix_norm_g, v_l0_w_in, v_l0_sc_conv_w, v_l0_w_out, v_l0_ffn_norm_g, v_l0_ffn_up, v_l0_ffn_conv_w, v_l0_ffn_down, v_l1_mix_norm_g, v_l1_w_in, v_l1_fox_b_f, v_l1_sg_w, v_l1_sg_b, v_l1_sg_norm_g, v_l1_w_out, v_l1_ffn_norm_g, v_l1_ffn_up, v_l1_ffn_conv_w, v_l1_ffn_down, v_final_norm_g):
    given = (l0_mix_norm_g, l0_w_in, l0_sc_conv_w, l0_w_out, l0_ffn_norm_g, l0_ffn_up, l0_ffn_conv_w, l0_ffn_down, l1_mix_norm_g, l1_w_in, l1_fox_b_f, l1_sg_w, l1_sg_b, l1_sg_norm_g, l1_w_out, l1_ffn_norm_g, l1_ffn_up, l1_ffn_conv_w, l1_ffn_down, final_norm_g)
    given_m = (m_l0_mix_norm_g, m_l0_w_in, m_l0_sc_conv_w, m_l0_w_out, m_l0_ffn_norm_g, m_l0_ffn_up, m_l0_ffn_conv_w, m_l0_ffn_down, m_l1_mix_norm_g, m_l1_w_in, m_l1_fox_b_f, m_l1_sg_w, m_l1_sg_b, m_l1_sg_norm_g, m_l1_w_out, m_l1_ffn_norm_g, m_l1_ffn_up, m_l1_ffn_conv_w, m_l1_ffn_down, m_final_norm_g)
    given_v = (v_l0_mix_norm_g, v_l0_w_in, v_l0_sc_conv_w, v_l0_w_out, v_l0_ffn_norm_g, v_l0_ffn_up, v_l0_ffn_conv_w, v_l0_ffn_down, v_l1_mix_norm_g, v_l1_w_in, v_l1_fox_b_f, v_l1_sg_w, v_l1_sg_b, v_l1_sg_norm_g, v_l1_w_out, v_l1_ffn_norm_g, v_l1_ffn_up, v_l1_ffn_conv_w, v_l1_ffn_down, v_final_norm_g)
    wt = dict(zip(WEIGHTS, given))
    mom = dict(zip(WEIGHTS, given_m))
    var = dict(zip(WEIGHTS, given_v))
    s = x.shape[1]
    t = ATT_BLOCK
    x0, target = x[0], loss_target[0]
    chip = 2 * lax.axis_index("x") + lax.axis_index("y")

    pos = jnp.stack([chip, lax.axis_index("c")]).astype(jnp.int32)
    big_names = list(BIG)

    conv_widths = [wt[n].shape[1] for n in CONV]
    conv_all = _gather_small(jnp.concatenate([wt[n] for n in CONV], axis=1))
    conv_full, at = {}, 0
    for n, cw in zip(CONV, conv_widths):
        conv_full[n] = jnp.transpose(conv_all[:, :, at:at + cw], (1, 0, 2)).reshape(3, 4 * cw)
        at += cw
    gathers, token = [], conv_all
    for gi, names in enumerate(GATHER_GROUPS):
        placed = [_place_shard(pos, wt[n], BIG[n], "place_" + n, [token]) for n in names]
        send, recv, thru, token = _gather_start(placed, [BIG[n] for n in names], "gather_start_%d" % gi, [token])
        gathers.append((send, recv, thru))
    full = {}

    def forward_gather(gi, after):
        send, recv, thru = gathers[gi]
        kinds = [BIG[n] for n in GATHER_GROUPS[gi]]
        gathers[gi] = _gather_forward(thru, kinds, send, recv, after, "gather_forward_%d" % gi)
        return gathers[gi][3]

    def finish_gather(gi, after):
        send, recv, thru, tok = gathers[gi]
        names = GATHER_GROUPS[gi]
        wholes = _gather_finish(thru, [BIG[n] for n in names], send, recv, tok if after is None else after,
                                "gather_finish_%d" % gi)
        full.update(zip(names, wholes))

    def vec(name):
        return wt[name].reshape(1, -1)

    h0 = _rmsnorm_fwd(x0, vec('l0_mix_norm_g'), "l0_mix_norm", after=[token])
    forward_gather(0, h0)
    finish_gather(0, None)
    p0 = _matmul(h0, full['l0_w_in'], "nn", F32, "l0_in")
    a_out, sb_carries = _sb_fwd(p0, "l0_sb", after=[forward_gather(1, p0)])
    finish_gather(1, a_out)
    b_out = _sc_fwd(p0, conv_full['l0_sc_conv_w'], "l0_sc")
    ab0 = jnp.concatenate([a_out.astype(BF16), b_out], axis=1)
    x1 = _matmul(ab0, full['l0_w_out'], "nn", F32, "l0_out", res=x0, after=[forward_gather(2, b_out)])

    def ffn0_weights(h):
        finish_gather(2, h)
        return full['l0_ffn_up'], full['l0_ffn_down']

    x2, ffn0_saved = _ffn_forward(x1, vec('l0_ffn_norm_g'), ffn0_weights, conv_full['l0_ffn_conv_w'], "l0_ffn",
                                  behind_down=lambda f: forward_gather(3, f))
    h2 = _rmsnorm_fwd(x2, vec('l1_mix_norm_g'), "l1_mix_norm")
    finish_gather(3, h2)
    w_in1 = jnp.transpose(full['l1_w_in'], (1, 0, 2)).reshape(D_MODEL, IN_CD)
    w_in1_main = w_in1[:, :5 * HALF]
    w_in1_f = jnp.pad(w_in1[:, 5 * HALF:], ((0, 0), (0, 128 - N_HEADS)))
    p1 = _matmul(h2, w_in1_main, "nn", F32, "l1_in")
    f_logit = _matmul(h2, w_in1_f, "nn", F32, "l1_in_f")
    b_f = jnp.pad(wt['l1_fox_b_f'], (0, 128 - N_HEADS)).reshape(1, 128)
    c_heads = _fox_prep(f_logit, b_f, "l1_fox_prep")[:, :N_HEADS].T
    c_col = c_heads[:, :, None]
    c_row = c_heads.reshape(N_HEADS, s // t, 1, t)
    sg_bias = jnp.repeat(wt['l1_sg_b'].T, HEAD, axis=1)
    sg_gain = vec('l1_sg_norm_g')
    c_out = _sg_fwd(p1, wt['l1_sg_w'], sg_bias, sg_gain, "l1_sg")
    d_out, lse = _fox_fwd(p1, c_col, c_row, "l1_fox", after=[forward_gather(4, c_out)])
    cd1 = jnp.concatenate([c_out, d_out.astype(BF16)], axis=1)
    x3 = _matmul(cd1, full['l1_w_out'], "nn", F32, "l1_out", res=x2)

    def ffn1_weights(h):
        finish_gather(4, h)
        return full['l1_ffn_up'], full['l1_ffn_down']

    x4, ffn1_saved = _ffn_forward(x3, vec('l1_ffn_norm_g'), ffn1_weights, conv_full['l1_ffn_conv_w'], "l1_ffn")
    dx4, dg_final, loss_part = _loss_head(x4, vec('final_norm_g'), target, "loss_head")

    grads = {'final_norm_g': dg_final}
    scatters = []

    def send_grads(names):
        def start(*group):
            send, recv, thru, lands, tok = _scatter_start(list(group), [BIG[n] for n in names],
                                                          "scatter_start_%d" % len(scatters))
            scatters.append((names, send, recv, thru, lands))
            return tok
        return start

    dx3, grads['l1_ffn_norm_g'], grads['l1_ffn_conv_w'] = _ffn_backward(
        x3, vec('l1_ffn_norm_g'), full['l1_ffn_up'], conv_full['l1_ffn_conv_w'], full['l1_ffn_down'], ffn1_saved, dx4,
        send_grads(['l1_ffn_up', 'l1_ffn_down']), "l1_ffn")
    dw_out1 = _matmul(cd1, dx3, "tn", BF16, "l1_dwout")
    d_cd = _matmul(dx3, full['l1_w_out'], "nt", F32, "l1_dcd")
    du, dv, grads['l1_sg_w'], db_sg, grads['l1_sg_norm_g'] = _sg_bwd(p1, wt['l1_sg_w'], sg_bias, sg_gain, d_cd, "l1_dsg")
    grads['l1_sg_b'] = db_sg[:, :N_HEADS].T
    dq, dk, dvv, dcol, drow = _fox_bwd(p1, c_col, c_row, lse, d_cd, d_out, "l1_dfox")
    pad8 = ((0, 0), (0, 128 - N_HEADS))
    d_f_logit, d_b_f = _fox_post(jnp.pad(drow[:, :, 0].T, pad8), jnp.pad(dcol[:, ::HEAD], pad8), f_logit, b_f,
                                 "l1_fox_post")
    grads['l1_fox_b_f'] = d_b_f[0, :N_HEADS]
    dp1 = jnp.concatenate([a.astype(BF16) for a in (du, dv, dq, dk, dvv)], axis=1)
    dw_main = _matmul(h2, dp1, "tn", BF16, "l1_dwin")
    dw_f = _matmul(h2, d_f_logit, "tn", BF16, "l1_dwin_f")
    dw_in1 = jnp.concatenate([dw_main, dw_f[:, :N_HEADS]], axis=1)
    dw_in1 = jnp.transpose(dw_in1.reshape(D_MODEL, 4, IN_CD // 4), (1, 0, 2))
    dh2 = _matmul(dp1, w_in1_main, "nt", F32, "l1_dh", after=[send_grads(['l1_w_out', 'l1_w_in'])(dw_out1, dw_in1)])
    dh2 = _matmul(d_f_logit, w_in1_f, "nt", F32, "l1_dh_f", res=dh2)
    dx2, grads['l1_mix_norm_g'] = _rmsnorm_bwd(x2, vec('l1_mix_norm_g'), dh2, dx3, "l1_dmix_norm")
    dx1, grads['l0_ffn_norm_g'], grads['l0_ffn_conv_w'] = _ffn_backward(
        x1, vec('l0_ffn_norm_g'), full['l0_ffn_up'], conv_full['l0_ffn_conv_w'], full['l0_ffn_down'], ffn0_saved, dx2,
        send_grads(['l0_ffn_up', 'l0_ffn_down']), "l0_ffn")
    dw_out0 = _matmul(ab0, dx1, "tn", BF16, "l0_dwout")
    d_ab = _matmul(dx1, full['l0_w_out'], "nt", F32, "l0_dab")
    dq0, dk0, dv0 = _sb_bwd(p0, d_ab, sb_carries, "l0_dsb")
    dgb, dgc, dhin, grads['l0_sc_conv_w'] = _sc_bwd(p0, conv_full['l0_sc_conv_w'], d_ab, "l0_dsc")
    dp0 = jnp.concatenate([a.astype(BF16) for a in (dq0, dk0, dv0, dgb, dgc, dhin)], axis=1)
    dw_in0 = _matmul(h0, dp0, "tn", BF16, "l0_dwin")
    dh0 = _matmul(dp0, full['l0_w_in'], "nt", F32, "l0_dh", after=[send_grads(['l0_w_out', 'l0_w_in'])(dw_out0, dw_in0)])
    dx0, grads['l0_mix_norm_g'] = _rmsnorm_bwd(x0, vec('l0_mix_norm_g'), dh0, dx1, "l0_dmix_norm")

    shard_grads, delta, new_m, new_v, swaps = {}, {}, {}, {}, {}

    def reduce_group(gi, after):
        names, send, recv, thru, lands = scatters[gi]
        kinds = [BIG[n] for n in names]
        g_thru, landed = _scatter_wait(thru, lands, kinds, send, recv, after, "scatter_wait_%d" % gi)
        halves = [_sum_partials(pos, g, ld, kd, "sum_" + n) for n, g, ld, kd in zip(names, g_thru, landed, kinds)]
        s_send, s_recv, h_thru, s_lands, tok = _swap_start(halves, "swap_start_%d" % gi)
        swaps[gi] = (names, s_send, s_recv, h_thru, s_lands)
        return tok

    def update_group(gi, after):
        names, s_send, s_recv, h_thru, s_lands = swaps[gi]
        mine, theirs = _swap_wait(h_thru, s_lands, s_send, s_recv, after, "swap_wait_%d" % gi)
        for n, gm, gs in zip(names, mine, theirs):
            shard_grads[n], delta[n], new_m[n], new_v[n] = _adamw_shard(pos, wt[n], gm, gs, mom[n], var[n], "adamw_" + n)
        return delta[names[-1]]

    after = reduce_group(1, reduce_group(0, dx0))
    after = update_group(1, update_group(0, after))
    after = reduce_group(3, reduce_group(2, after))
    small_shapes = [conv_full[n].shape if n in CONV else wt[n].shape for n in SMALL] + [loss_part.shape]
    small_all = _allreduce_small(_pack([grads[n] for n in SMALL] + [loss_part]), [after])
    small_sums = _unpack(small_all, small_shapes)
    loss = small_sums[-1][0, 0]
    for n, g in zip(SMALL, small_sums):
        shard_grads[n] = lax.dynamic_slice_in_dim(g, chip * wt[n].shape[1], wt[n].shape[1], axis=1) if n in CONV else g
    update_group(3, update_group(2, small_all))
    shapes = [wt[n].shape for n in SMALL]
    packed = _adamw(_pack([wt[n] for n in SMALL]), _pack([shard_grads[n] for n in SMALL]),
                    _pack([mom[n] for n in SMALL]), _pack([var[n] for n in SMALL]), "adamw_small")
    for out, pk in zip((delta, new_m, new_v), packed):
        out.update(zip(SMALL, _unpack(pk, shapes)))

    return (loss, dx0[None], *[shard_grads[n] for n in WEIGHTS], *[delta[n] for n in WEIGHTS],
            *[new_m[n] for n in WEIGHTS], *[new_v[n] for n in WEIGHTS])
```

```python
import functools

import jax
import jax.numpy as jnp
from jax import lax
from jax.experimental import pallas as pl
from jax.experimental.pallas import tpu as pltpu

F32 = jnp.float32
BF16 = jnp.bfloat16

D_MODEL = 2048
HEAD = 128
N_HEADS = 8
HALF = N_HEADS * HEAD
D_FF = 5632
EPS = 1e-6
ATT_SCALE = HEAD ** -0.5
ATT_BLOCK = 256
NEG = -1e30

ADAM_LR = 0.001
ADAM_B1 = 0.9
ADAM_B2 = 0.999
ADAM_EPS = 1e-08
ADAM_WD = 0.01
ADAM_STEP = 10

VMEM_LIMIT_BYTES = 48 * 1024 * 1024
MESH = pl.DeviceIdType.MESH
HBM_SPEC = pl.BlockSpec(memory_space=pltpu.HBM)
VMEM_SPEC = pl.BlockSpec(memory_space=pltpu.VMEM)


def _pcall(body, after=(), **kw):
    if not after:
        return pl.pallas_call(body, **kw)
    n_in, n_after, inner = len(kw["in_specs"]), len(after), body
    kw["in_specs"] = list(kw["in_specs"]) + [pl.BlockSpec(memory_space=pl.ANY)] * n_after

    def body(*refs):
        inner(*refs[:n_in], *refs[n_in + n_after:])

    call = pl.pallas_call(body, **kw)
    return lambda *args: call(*args, *after)


def _params(*semantics):
    return pltpu.CompilerParams(dimension_semantics=semantics, vmem_limit_bytes=VMEM_LIMIT_BYTES)


def _pick(n, cap):
    best = None
    for t in range(128, min(n, cap) + 1, 128):
        if n % t == 0:
            best = t
    return n if best is None else best


def _dot(a, b, dims):
    return lax.dot_general(a, b, (dims, ((), ())), preferred_element_type=F32)


def _dot_nn(a, b):
    return _dot(a, b, ((1,), (0,)))


def _dot_nt(a, b):
    return _dot(a, b, ((1,), (1,)))


def _dot_tn(a, b):
    return _dot(a, b, ((0,), (0,)))


def _split3(x):
    hi = x.astype(BF16)
    r1 = x - hi.astype(F32)
    mid = r1.astype(BF16)
    lo = (r1 - mid.astype(F32)).astype(BF16)
    return hi, mid, lo


def _softplus(z):
    return jnp.maximum(z, 0.0) + jnp.log1p(jnp.exp(-jnp.abs(z)))


def _log_sigmoid(z):
    return jnp.minimum(z, 0.0) - jnp.log1p(jnp.exp(-jnp.abs(z)))


_GELU_K = 0.7978845608028654


def _gelu(x):
    return 0.5 * x * (1.0 + jnp.tanh(_GELU_K * (x + 0.044715 * x * x * x)))


def _gelu_grad(x):
    t = jnp.tanh(_GELU_K * (x + 0.044715 * x * x * x))
    return 0.5 * (1.0 + t) + 0.5 * x * (1.0 - t * t) * _GELU_K * (1.0 + 3.0 * 0.044715 * x * x)


SUBLANES = 8


def _shift_down(x, k):
    rolled = pltpu.roll(x, k, axis=0)
    head = rolled[:SUBLANES]
    head = jnp.where(lax.broadcasted_iota(jnp.int32, head.shape, 0) >= k, head, 0.0)
    return jnp.concatenate([head, rolled[SUBLANES:]], axis=0)


def _shift_up(x, k):
    n = x.shape[0]
    rolled = pltpu.roll(x, n - k, axis=0)
    tail = rolled[n - SUBLANES:]
    tail = jnp.where(lax.broadcasted_iota(jnp.int32, tail.shape, 0) < SUBLANES - k, tail, 0.0)
    return jnp.concatenate([rolled[:n - SUBLANES], tail], axis=0)


def _conv3(s, w, shifted=None):
    s1, s2 = shifted if shifted else (_shift_down(s, 1), _shift_down(s, 2))
    return w[0:1, :] * s2 + w[1:2, :] * s1 + w[2:3, :] * s


def _conv3_transpose(d, w):
    return w[2:3, :] * d + w[1:2, :] * _shift_up(d, 1) + w[0:1, :] * _shift_up(d, 2)


def _conv3_wgrad(d, s, shifted, dw_ref):
    s1, s2 = shifted
    dw_ref[0:1, :] = jnp.sum(d * s2, axis=0, keepdims=True)
    dw_ref[1:2, :] = jnp.sum(d * s1, axis=0, keepdims=True)
    dw_ref[2:3, :] = jnp.sum(d * s, axis=0, keepdims=True)


MM_TILE_M, MM_TILE_N, MM_TILE_K = 1024, 512, 2816


def _matmul(a, b, mode, out_dtype, name, res=None, after=()):
    if mode == "nn":
        (m, k), (k2, n) = a.shape, b.shape
    elif mode == "nt":
        (m, k), (n, k2) = a.shape, b.shape
    else:
        (k, m), (k2, n) = a.shape, b.shape
    assert k == k2, (a.shape, b.shape, mode)
    tm, tn, tk = _pick(m, MM_TILE_M), _pick(n, MM_TILE_N), _pick(k, MM_TILE_K)
    nk = k // tk
    if mode == "tn":
        a_spec = pl.BlockSpec((tk, tm), lambda i, j, kk: (kk, i))
    else:
        a_spec = pl.BlockSpec((tm, tk), lambda i, j, kk: (i, kk))
    if mode == "nt":
        b_spec = pl.BlockSpec((tn, tk), lambda i, j, kk: (j, kk))
    else:
        b_spec = pl.BlockSpec((tk, tn), lambda i, j, kk: (kk, j))
    o_spec = pl.BlockSpec((tm, tn), lambda i, j, kk: (i, j))
    dims = {"nn": ((1,), (0,)), "nt": ((1,), (1,)), "tn": ((0,), (0,))}[mode]
    has_res = res is not None

    def body(*refs):
        a_ref, b_ref = refs[0], refs[1]
        r_ref = refs[2] if has_res else None
        o_ref = refs[3] if has_res else refs[2]
        part = _dot(a_ref[...].astype(BF16), b_ref[...].astype(BF16), dims)

        def finish(total):
            if has_res:
                total = total + r_ref[...]
            o_ref[...] = total.astype(out_dtype)

        if nk == 1:
            finish(part)
        else:
            acc_ref = refs[-1]
            kk = pl.program_id(2)

            @pl.when(kk == 0)
            def _():
                acc_ref[...] = part

            @pl.when(kk > 0)
            def _():
                acc_ref[...] += part

            @pl.when(kk == nk - 1)
            def _():
                finish(acc_ref[...])

    in_specs = [a_spec, b_spec] + ([o_spec] if has_res else [])
    args = (a, b) + ((res,) if has_res else ())
    return _pcall(
        body, after=after, name=name, grid=(m // tm, n // tn, nk),
        in_specs=in_specs, out_specs=o_spec,
        out_shape=jax.ShapeDtypeStruct((m, n), out_dtype),
        scratch_shapes=[pltpu.VMEM((tm, tn), F32)] if nk > 1 else [],
        compiler_params=_params("parallel", "parallel", "arbitrary"),
    )(*args)


ROW_TILE = 256


def _rmsnorm_fwd(x, g, name, after=()):
    s, d = x.shape

    def body(x_ref, g_ref, o_ref):
        xf = x_ref[...]
        r = lax.rsqrt(jnp.mean(xf * xf, axis=-1, keepdims=True) + EPS)
        o_ref[...] = (xf * r * g_ref[...]).astype(BF16)

    row = pl.BlockSpec((ROW_TILE, d), lambda i: (i, 0))
    vec = pl.BlockSpec((1, d), lambda i: (0, 0))
    return _pcall(body, after=after, name=name, grid=(s // ROW_TILE,), in_specs=[row, vec], out_specs=row,
                  out_shape=jax.ShapeDtypeStruct((s, d), BF16), compiler_params=_params("parallel"))(x, g)


def _rmsnorm_bwd(x, g, dh, dres, name):
    s, d = x.shape

    def body(x_ref, g_ref, dh_ref, dres_ref, dx_ref, dg_ref):
        xf = x_ref[...]
        r = lax.rsqrt(jnp.mean(xf * xf, axis=-1, keepdims=True) + EPS)
        xhat = xf * r
        dh_v = dh_ref[...]
        dxh = dh_v * g_ref[...]
        proj = jnp.mean(dxh * xhat, axis=-1, keepdims=True)
        dx_ref[...] = dres_ref[...] + r * (dxh - xhat * proj)
        part = jnp.sum(dh_v * xhat, axis=0, keepdims=True)

        @pl.when(pl.program_id(0) == 0)
        def _():
            dg_ref[...] = part

        @pl.when(pl.program_id(0) > 0)
        def _():
            dg_ref[...] += part

    row = pl.BlockSpec((ROW_TILE, d), lambda i: (i, 0))
    vec = pl.BlockSpec((1, d), lambda i: (0, 0))
    return _pcall(body, name=name, grid=(s // ROW_TILE,), in_specs=[row, vec, row, row], out_specs=[row, vec],
                  out_shape=[jax.ShapeDtypeStruct((s, d), F32), jax.ShapeDtypeStruct((1, d), F32)],
                  compiler_params=_params("arbitrary"))(x, g, dh, dres)


def _loss_head(x, g, target, name):
    s, d = x.shape

    def body(x_ref, g_ref, t_ref, dx_ref, dg_ref, loss_ref):
        xf = x_ref[...]
        r = lax.rsqrt(jnp.mean(xf * xf, axis=-1, keepdims=True) + EPS)
        xhat = xf * r
        gv = g_ref[...]
        err = xhat * gv - t_ref[...]
        dy = err * (1.0 / d)
        dxh = dy * gv
        proj = jnp.mean(dxh * xhat, axis=-1, keepdims=True)
        dx_ref[...] = r * (dxh - xhat * proj)
        dg_part = jnp.sum(dy * xhat, axis=0, keepdims=True)
        row_loss = jnp.sum(err * err, axis=-1, keepdims=True) * (0.5 / d)
        loss_part = jnp.broadcast_to(jnp.sum(row_loss, axis=0, keepdims=True), (1, 128))

        @pl.when(pl.program_id(0) == 0)
        def _():
            dg_ref[...] = dg_part
            loss_ref[...] = loss_part

        @pl.when(pl.program_id(0) > 0)
        def _():
            dg_ref[...] += dg_part
            loss_ref[...] += loss_part

    row = pl.BlockSpec((ROW_TILE, d), lambda i: (i, 0))
    vec = pl.BlockSpec((1, d), lambda i: (0, 0))
    one = pl.BlockSpec((1, 128), lambda i: (0, 0))
    return _pcall(body, name=name, grid=(s // ROW_TILE,), in_specs=[row, vec, row], out_specs=[row, vec, one],
                  out_shape=[jax.ShapeDtypeStruct((s, d), F32), jax.ShapeDtypeStruct((1, d), F32),
                             jax.ShapeDtypeStruct((1, 128), F32)],
                  compiler_params=_params("arbitrary"))(x, g, target)


HEADS_PER_STEP = 2
GROUP_W = HEADS_PER_STEP * HEAD
N_GROUPS = N_HEADS // HEADS_PER_STEP


def _head_cols(h):
    return slice(h * HEAD, (h + 1) * HEAD)


def _head_specs(s, col0):
    t = ATT_BLOCK
    g0 = [c // HEADS_PER_STEP for c in col0]
    qspec = pl.BlockSpec((t, GROUP_W), lambda g, i: (i, g0[0] + g))
    kspec = pl.BlockSpec((s, GROUP_W), lambda g, i: (0, g0[1] + g))
    vspec = pl.BlockSpec((s, GROUP_W), lambda g, i: (0, g0[2] + g))
    return qspec, kspec, vspec


def _order_matrix(t, later):
    r, c = lax.broadcasted_iota(jnp.int32, (t, t), 0), lax.broadcasted_iota(jnp.int32, (t, t), 1)
    return (r > c if later else r < c).astype(BF16)


def _exact_dot(x, m):
    hi = x.astype(BF16)
    lo = (x - hi.astype(F32)).astype(BF16)
    return _dot_nn(hi, m) + _dot_nn(lo, m)


def _sb_block(q, kblk, carry_l, u, diagonal):
    t = ATT_BLOCK
    z = _dot_nt(q, kblk) * ATT_SCALE
    sp = jnp.maximum(z, 0.0) + jnp.log(1.0 + jnp.exp(-jnp.abs(z)))
    if not diagonal:
        l = -sp
        return z, None, l, jnp.exp(z + l + _exact_dot(l, u) + carry_l)
    mask = lax.broadcasted_iota(jnp.int32, (t, t), 1) < lax.broadcasted_iota(jnp.int32, (t, t), 0)
    l = jnp.where(mask, -sp, 0.0)
    a = jnp.where(mask, jnp.exp(z - sp + _exact_dot(l, u) + carry_l), 0.0)
    return z, mask, l, a


def _sb_carry_spec(s):
    t = ATT_BLOCK
    return pl.BlockSpec((HEADS_PER_STEP, None, s // t, t, 1), lambda g, i: (g, i, 0, 0, 0))


def _sb_fwd(p, name, after=()):
    s = p.shape[0]
    t = ATT_BLOCK
    nb = s // t

    def body(q_ref, k_ref, v_ref, o_ref, cl_ref):
        i = pl.program_id(1)
        heads = range(HEADS_PER_STEP)
        q = [q_ref[:, _head_cols(h)].astype(BF16) for h in heads]
        u = _order_matrix(t, True)
        cl_ref[...] = jnp.zeros_like(cl_ref)

        def tile(kb, carry, diagonal):
            ks = pl.multiple_of(kb * t, t)
            out = []
            for h in heads:
                acc, carry_l = carry[h]
                kblk = k_ref[pl.ds(ks, t), _head_cols(h)].astype(BF16)
                vblk = v_ref[pl.ds(ks, t), _head_cols(h)].astype(BF16)
                cl_ref[h, kb] = carry_l
                _, _, l, a = _sb_block(q[h], kblk, carry_l, u, diagonal)
                out.append((acc + _dot_nn(a.astype(BF16), vblk), carry_l + jnp.sum(l, axis=1, keepdims=True)))
            return tuple(out)

        carry = tile(i, tuple((jnp.zeros((t, HEAD), F32), jnp.zeros((t, 1), F32)) for _ in heads), True)
        carry = lax.fori_loop(0, i, lambda n, c: tile(i - 1 - n, c, False), carry)
        for h in heads:
            o_ref[:, _head_cols(h)] = carry[h][0]

    qspec, kspec, vspec = _head_specs(s, (0, N_HEADS, 2 * N_HEADS))
    ospec = pl.BlockSpec((t, GROUP_W), lambda g, i: (i, g))
    return _pcall(body, after=after, name=name, grid=(N_GROUPS, nb), in_specs=[qspec, kspec, vspec],
                  out_specs=[ospec, _sb_carry_spec(s)],
                  out_shape=[jax.ShapeDtypeStruct((s, HALF), F32), jax.ShapeDtypeStruct((N_HEADS, nb, nb, t, 1), F32)],
                  compiler_params=_params("parallel", "parallel"))(p, p, p)


def _sb_bwd(p, d_ab, carries, name):
    s = p.shape[0]
    t = ATT_BLOCK

    def body(q_ref, k_ref, v_ref, do_ref, cl_ref, dq_ref, dk_ref, dv_ref):
        i = pl.program_id(1)

        @pl.when(i == 0)
        def _():
            dk_ref[...] = jnp.zeros_like(dk_ref)
            dv_ref[...] = jnp.zeros_like(dv_ref)

        heads = range(HEADS_PER_STEP)
        q = [q_ref[:, _head_cols(h)].astype(BF16) for h in heads]
        do = [do_ref[:, _head_cols(h)].astype(BF16) for h in heads]
        u = _order_matrix(t, True)
        lower = _order_matrix(t, False)

        def tile(kb, carry, diagonal):
            ks = pl.multiple_of(kb * t, t)
            out = []
            for h in heads:
                dq, carry_g = carry[h]
                kblk = k_ref[pl.ds(ks, t), _head_cols(h)].astype(BF16)
                vblk = v_ref[pl.ds(ks, t), _head_cols(h)].astype(BF16)
                z, mask, _, a = _sb_block(q[h], kblk, cl_ref[h, kb], u, diagonal)
                g = a * _dot_nt(do[h], vblk)
                earlier_g = _exact_dot(g, lower) + carry_g
                sig = jax.nn.sigmoid(z)
                dz = g * (1.0 - sig) - sig * earlier_g
                if diagonal:
                    dz = jnp.where(mask, dz, 0.0)
                dz = dz.astype(BF16)
                dv_ref[pl.ds(ks, t), _head_cols(h)] += _dot_tn(a.astype(BF16), do[h])
                dk_ref[pl.ds(ks, t), _head_cols(h)] += _dot_tn(dz, q[h]) * ATT_SCALE
                out.append((dq + _dot_nn(dz, kblk) * ATT_SCALE, carry_g + jnp.sum(g, axis=1, keepdims=True)))
            return tuple(out)

        init = tuple((jnp.zeros((t, HEAD), F32), jnp.zeros((t, 1), F32)) for _ in heads)
        carry = tile(i, lax.fori_loop(0, i, lambda kb, c: tile(kb, c, False), init), True)
        for h in heads:
            dq_ref[:, _head_cols(h)] = carry[h][0]

    qspec, kspec, vspec = _head_specs(s, (0, N_HEADS, 2 * N_HEADS))
    blk = pl.BlockSpec((t, GROUP_W), lambda g, i: (i, g))
    whole = pl.BlockSpec((s, GROUP_W), lambda g, i: (0, g))
    shape = jax.ShapeDtypeStruct((s, HALF), F32)
    return _pcall(body, name=name, grid=(N_GROUPS, s // t), in_specs=[qspec, kspec, vspec, blk, _sb_carry_spec(s)],
                  out_specs=[blk, whole, whole], out_shape=[shape, shape, shape],
                  compiler_params=_params("parallel", "arbitrary"))(p, p, p, d_ab, carries)


COL_TILE = 256


def _sc_fwd(p, w, name):
    s = p.shape[0]
    nb = HALF // COL_TILE

    def body(gb_ref, gc_ref, h_ref, w_ref, o_ref):
        conv = _conv3(gc_ref[...] * h_ref[...], w_ref[...])
        o_ref[...] = (gb_ref[...] * conv).astype(BF16)

    def col(k):
        return pl.BlockSpec((s, COL_TILE), lambda j: (0, k * nb + j))

    wspec = pl.BlockSpec((3, COL_TILE), lambda j: (0, j))
    return _pcall(body, name=name, grid=(nb,), in_specs=[col(3), col(4), col(5), wspec], out_specs=col(0),
                  out_shape=jax.ShapeDtypeStruct((s, HALF), BF16), compiler_params=_params("parallel"))(p, p, p, w)


def _sc_bwd(p, w, d_ab, name):
    s = p.shape[0]
    nb = HALF // COL_TILE

    def body(gb_ref, gc_ref, h_ref, w_ref, d_ref, dgb_ref, dgc_ref, dh_ref, dw_ref):
        gc, hin, wv, d = gc_ref[...], h_ref[...], w_ref[...], d_ref[...]
        sig = gc * hin
        shifted = (_shift_down(sig, 1), _shift_down(sig, 2))
        dgb_ref[...] = d * _conv3(sig, wv, shifted)
        dconv = d * gb_ref[...]
        _conv3_wgrad(dconv, sig, shifted, dw_ref)
        dsig = _conv3_transpose(dconv, wv)
        dgc_ref[...] = dsig * hin
        dh_ref[...] = dsig * gc

    def col(k):
        return pl.BlockSpec((s, COL_TILE), lambda j: (0, k * nb + j))

    wspec = pl.BlockSpec((3, COL_TILE), lambda j: (0, j))
    act = jax.ShapeDtypeStruct((s, HALF), F32)
    return _pcall(body, name=name, grid=(nb,), in_specs=[col(3), col(4), col(5), wspec, col(1)],
                  out_specs=[col(0), col(0), col(0), wspec],
                  out_shape=[act, act, act, jax.ShapeDtypeStruct((3, HALF), F32)],
                  compiler_params=_params("parallel"))(p, p, p, w, d_ab)


def _ffn_act_fwd(u, w, name):
    s = u.shape[0]
    nb = D_FF // COL_TILE

    def body(ug_ref, uu_ref, wg_ref, wu_ref, o_ref):
        gate = _conv3(ug_ref[...], wg_ref[...])
        up = _conv3(uu_ref[...], wu_ref[...])
        o_ref[...] = (gate * jax.nn.sigmoid(gate) * up).astype(BF16)

    def col(k):
        return pl.BlockSpec((s, COL_TILE), lambda j: (0, k * nb + j))

    def wcol(k):
        return pl.BlockSpec((3, COL_TILE), lambda j: (0, k * nb + j))

    return _pcall(body, name=name, grid=(nb,), in_specs=[col(0), col(1), wcol(0), wcol(1)], out_specs=col(0),
                  out_shape=jax.ShapeDtypeStruct((s, D_FF), BF16),
                  compiler_params=_params("parallel"))(u, u, w, w)


def _ffn_act_bwd(u, w, d_f, name):
    s = u.shape[0]
    nb = D_FF // COL_TILE

    def body(ug_ref, uu_ref, wg_ref, wu_ref, d_ref, dug_ref, duu_ref, dwg_ref, dwu_ref):
        ug, uu, wg, wu, d = ug_ref[...], uu_ref[...], wg_ref[...], wu_ref[...], d_ref[...]
        ug_shifted = (_shift_down(ug, 1), _shift_down(ug, 2))
        uu_shifted = (_shift_down(uu, 1), _shift_down(uu, 2))
        gate = _conv3(ug, wg, ug_shifted)
        up = _conv3(uu, wu, uu_shifted)
        sig = jax.nn.sigmoid(gate)
        d_up = d * gate * sig
        d_gate = d * up * sig * (1.0 + gate * (1.0 - sig))
        _conv3_wgrad(d_gate, ug, ug_shifted, dwg_ref)
        _conv3_wgrad(d_up, uu, uu_shifted, dwu_ref)
        dug_ref[...] = _conv3_transpose(d_gate, wg).astype(BF16)
        duu_ref[...] = _conv3_transpose(d_up, wu).astype(BF16)

    def col(k):
        return pl.BlockSpec((s, COL_TILE), lambda j: (0, k * nb + j))

    def wcol(k):
        return pl.BlockSpec((3, COL_TILE), lambda j: (0, k * nb + j))

    act = jax.ShapeDtypeStruct((s, D_FF), BF16)
    wsh = jax.ShapeDtypeStruct((3, D_FF), F32)
    return _pcall(body, name=name, grid=(nb,), in_specs=[col(0), col(1), wcol(0), wcol(1), col(0)],
                  out_specs=[col(0), col(0), wcol(0), wcol(0)], out_shape=[act, act, wsh, wsh],
                  compiler_params=_params("parallel"))(u, u, w, w, d_f)


def _sg_common(u, v, g, w_ref, bias, mixed_ref):
    rows = u.shape[0]
    gu = _gelu(u)
    gv = _gelu(v)
    xc = gv - jnp.mean(gv, axis=-1, keepdims=True)
    rstd = lax.rsqrt(jnp.mean(xc * xc, axis=-1, keepdims=True) + EPS)
    xhat = xc * rstd
    vn = xhat * g
    tril = lax.broadcasted_iota(jnp.int32, (HEAD, HEAD), 0) >= lax.broadcasted_iota(jnp.int32, (HEAD, HEAD), 1)
    wts = [jnp.where(tril, w_ref[grp], 0.0).astype(BF16) for grp in range(N_HEADS)]
    for n in range(rows // HEAD):
        for grp in range(N_HEADS):
            blk = vn[n * HEAD:(n + 1) * HEAD, grp * HEAD:(grp + 1) * HEAD].astype(BF16)
            mixed_ref[n * HEAD:(n + 1) * HEAD, grp * HEAD:(grp + 1) * HEAD] = _dot_nn(wts[grp], blk)
    mixed = mixed_ref[...] + jnp.concatenate([bias] * (rows // HEAD), axis=0)
    return gu, xhat, rstd, vn, mixed, wts, tril


def _sg_fwd(p, sg_w, bias, g, name):
    s = p.shape[0]

    def body(u_ref, v_ref, w_ref, b_ref, g_ref, o_ref, mixed_ref):
        gu, _, _, _, mixed, _, _ = _sg_common(u_ref[...], v_ref[...], g_ref[...], w_ref, b_ref[...], mixed_ref)
        o_ref[...] = (gu * mixed).astype(BF16)

    def half(k):
        return pl.BlockSpec((ROW_TILE, HALF), lambda i: (i, k))

    wspec = pl.BlockSpec((N_HEADS, HEAD, HEAD), lambda i: (0, 0, 0))
    bspec = pl.BlockSpec((HEAD, HALF), lambda i: (0, 0))
    gspec = pl.BlockSpec((1, HALF), lambda i: (0, 0))
    return _pcall(body, name=name, grid=(s // ROW_TILE,), in_specs=[half(0), half(1), wspec, bspec, gspec],
                  out_specs=half(0), out_shape=jax.ShapeDtypeStruct((s, HALF), BF16),
                  scratch_shapes=[pltpu.VMEM((ROW_TILE, HALF), F32)],
                  compiler_params=_params("parallel"))(p, p, sg_w, bias, g)


def _sg_bwd(p, sg_w, bias, g, d_cd, name):
    s = p.shape[0]
    nsteps = s // ROW_TILE

    def body(u_ref, v_ref, w_ref, b_ref, g_ref, d_ref, du_ref, dv_ref, dw_ref, db_ref, dg_ref,
             mixed_ref, dvn_ref, dbias_ref):
        i = pl.program_id(0)
        u, v, gain, d = u_ref[...], v_ref[...], g_ref[...], d_ref[...]
        gu, xhat, rstd, vn, mixed, wts, tril = _sg_common(u, v, gain, w_ref, b_ref[...], mixed_ref)

        @pl.when(i == 0)
        def _():
            dw_ref[...] = jnp.zeros_like(dw_ref)
            dg_ref[...] = jnp.zeros_like(dg_ref)
            dbias_ref[...] = jnp.zeros_like(dbias_ref)

        du_ref[...] = d * mixed * _gelu_grad(u)
        dm = d * gu
        for n in range(ROW_TILE // HEAD):
            rs = slice(n * HEAD, (n + 1) * HEAD)
            dbias_ref[...] += dm[rs, :]
            for grp in range(N_HEADS):
                cs = slice(grp * HEAD, (grp + 1) * HEAD)
                dm_blk = dm[rs, cs].astype(BF16)
                dw_ref[grp] += jnp.where(tril, _dot_nt(dm_blk, vn[rs, cs].astype(BF16)), 0.0)
                dvn_ref[rs, cs] = _dot_tn(wts[grp], dm_blk)
        dvn = dvn_ref[...]
        dg_ref[...] += jnp.sum(dvn * xhat, axis=0, keepdims=True)
        dxh = dvn * gain
        d_gv = rstd * (dxh - jnp.mean(dxh, axis=-1, keepdims=True) - xhat * jnp.mean(dxh * xhat, axis=-1, keepdims=True))
        dv_ref[...] = d_gv * _gelu_grad(v)

        @pl.when(i == nsteps - 1)
        def _():
            lane = lax.broadcasted_iota(jnp.int32, (HEAD, HEAD), 1)
            out = jnp.zeros((HEAD, HEAD), F32)
            for grp in range(N_HEADS):
                tot = jnp.sum(dbias_ref[:, grp * HEAD:(grp + 1) * HEAD], axis=1, keepdims=True)
                out = out + jnp.where(lane == grp, tot, 0.0)
            db_ref[...] = out

    def half(k):
        return pl.BlockSpec((ROW_TILE, HALF), lambda i: (i, k))

    wspec = pl.BlockSpec((N_HEADS, HEAD, HEAD), lambda i: (0, 0, 0))
    bspec = pl.BlockSpec((HEAD, HALF), lambda i: (0, 0))
    gspec = pl.BlockSpec((1, HALF), lambda i: (0, 0))
    dbspec = pl.BlockSpec((HEAD, HEAD), lambda i: (0, 0))
    act = jax.ShapeDtypeStruct((s, HALF), F32)
    return _pcall(body, name=name, grid=(nsteps,), in_specs=[half(0), half(1), wspec, bspec, gspec, half(0)],
                  out_specs=[half(0), half(0), wspec, dbspec, gspec],
                  out_shape=[act, act, jax.ShapeDtypeStruct((N_HEADS, HEAD, HEAD), F32),
                             jax.ShapeDtypeStruct((HEAD, HEAD), F32), jax.ShapeDtypeStruct((1, HALF), F32)],
                  scratch_shapes=[pltpu.VMEM((ROW_TILE, HALF), F32), pltpu.VMEM((ROW_TILE, HALF), F32),
                                  pltpu.VMEM((HEAD, HALF), F32)],
                  compiler_params=_params("arbitrary"))(p, p, sg_w, bias, g, d_cd)


def _fox_prep(f, b, name):
    s = f.shape[0]
    t = ATT_BLOCK

    def body(f_ref, b_ref, c_ref):
        tri = (lax.broadcasted_iota(jnp.int32, (t, t), 0) >= lax.broadcasted_iota(jnp.int32, (t, t), 1)).astype(BF16)
        carry = jnp.zeros((1, 128), F32)
        for n in range(s // t):
            lf = _log_sigmoid(f_ref[n * t:(n + 1) * t, :] + b_ref[...])
            hi, mid, lo = _split3(lf)
            c_ref[n * t:(n + 1) * t, :] = _dot_nn(tri, hi) + _dot_nn(tri, mid) + _dot_nn(tri, lo) + carry
            carry = carry + jnp.sum(lf, axis=0, keepdims=True)

    return _pcall(body, name=name, in_specs=[VMEM_SPEC, VMEM_SPEC], out_specs=VMEM_SPEC,
                  out_shape=jax.ShapeDtypeStruct((s, 128), F32))(f, b)


def _fox_post(drow, dcol, f, b, name):
    s = f.shape[0]
    t = ATT_BLOCK

    def body(drow_ref, dcol_ref, f_ref, b_ref, df_ref, db_ref):
        tri = (lax.broadcasted_iota(jnp.int32, (t, t), 1) >= lax.broadcasted_iota(jnp.int32, (t, t), 0)).astype(BF16)
        carry = jnp.zeros((1, 128), F32)
        db = jnp.zeros((1, 128), F32)
        for n in reversed(range(s // t)):
            rs = slice(n * t, (n + 1) * t)
            dc = drow_ref[rs, :] - dcol_ref[rs, :]
            hi, mid, lo = _split3(dc)
            dlogf = _dot_nn(tri, hi) + _dot_nn(tri, mid) + _dot_nn(tri, lo) + carry
            carry = carry + jnp.sum(dc, axis=0, keepdims=True)
            df = dlogf * jax.nn.sigmoid(-(f_ref[rs, :] + b_ref[...]))
            df_ref[rs, :] = df
            db = db + jnp.sum(df, axis=0, keepdims=True)
        db_ref[...] = db

    return _pcall(body, name=name, in_specs=[VMEM_SPEC] * 4, out_specs=[VMEM_SPEC, VMEM_SPEC],
                  out_shape=[jax.ShapeDtypeStruct((s, 128), F32), jax.ShapeDtypeStruct((1, 128), F32)])(drow, dcol, f, b)


def _fox_specs(s):
    t = ATT_BLOCK
    ccol = pl.BlockSpec((HEADS_PER_STEP, t, 1), lambda g, i: (g, i, 0))
    crow = pl.BlockSpec((HEADS_PER_STEP, s // t, 1, t), lambda g, i: (g, 0, 0, 0))
    return ccol, crow


def _fox_fwd(p, c_col, c_row, name, after=()):
    s = p.shape[0]
    t = ATT_BLOCK

    def body(q_ref, k_ref, v_ref, cc_ref, cr_ref, o_ref, lse_ref):
        i = pl.program_id(1)
        heads = range(HEADS_PER_STEP)
        q = [q_ref[:, _head_cols(h)].astype(BF16) for h in heads]
        ct = [cc_ref[h] for h in heads]

        def tile(n, carry, diagonal):
            ks = pl.multiple_of(n * t, t)
            out = []
            for h in heads:
                acc, m, l = carry[h]
                kblk = k_ref[pl.ds(ks, t), _head_cols(h)].astype(BF16)
                vblk = v_ref[pl.ds(ks, t), _head_cols(h)].astype(BF16)
                logit = _dot_nt(q[h], kblk) * ATT_SCALE + ct[h] - cr_ref[h, n]
                if diagonal:
                    causal = lax.broadcasted_iota(jnp.int32, (t, t), 1) <= lax.broadcasted_iota(jnp.int32, (t, t), 0)
                    logit = jnp.where(causal, logit, NEG)
                m_new = jnp.maximum(m, jnp.max(logit, axis=1, keepdims=True))
                alpha = jnp.exp(m - m_new)
                pr = jnp.exp(logit - m_new)
                l = alpha * l + jnp.sum(pr, axis=1, keepdims=True)
                out.append((alpha * acc + _dot_nn(pr.astype(BF16), vblk), m_new, l))
            return tuple(out)

        init = tuple((jnp.zeros((t, HEAD), F32), jnp.full((t, 1), NEG, F32), jnp.zeros((t, 1), F32)) for _ in heads)
        carry = tile(i, lax.fori_loop(0, i, lambda n, c: tile(n, c, False), init), True)
        for h in heads:
            acc, m, l = carry[h]
            o_ref[:, _head_cols(h)] = acc / l
            lse_ref[h] = m + jnp.log(l)

    qspec, kspec, vspec = _head_specs(s, (2 * N_HEADS, 3 * N_HEADS, 4 * N_HEADS))
    ccol, crow = _fox_specs(s)
    ospec = pl.BlockSpec((t, GROUP_W), lambda g, i: (i, g))
    return _pcall(body, after=after, name=name, grid=(N_GROUPS, s // t), in_specs=[qspec, kspec, vspec, ccol, crow],
                  out_specs=[ospec, ccol],
                  out_shape=[jax.ShapeDtypeStruct((s, HALF), F32), jax.ShapeDtypeStruct((N_HEADS, s, 1), F32)],
                  compiler_params=_params("parallel", "parallel"))(p, p, p, c_col, c_row)


def _fox_bwd(p, c_col, c_row, lse, d_cd, d_out, name):
    s = p.shape[0]
    t = ATT_BLOCK

    def body(q_ref, k_ref, v_ref, cc_ref, cr_ref, lse_ref, do_ref, o_ref, dq_ref, dk_ref, dv_ref, dcol_ref, drow_ref):
        i = pl.program_id(1)

        @pl.when(i == 0)
        def _():
            dk_ref[...] = jnp.zeros_like(dk_ref)
            dv_ref[...] = jnp.zeros_like(dv_ref)
            dcol_ref[...] = jnp.zeros_like(dcol_ref)

        heads = range(HEADS_PER_STEP)
        q = [q_ref[:, _head_cols(h)].astype(BF16) for h in heads]
        do = [do_ref[:, _head_cols(h)].astype(BF16) for h in heads]
        delta = [jnp.sum(do_ref[:, _head_cols(h)] * o_ref[:, _head_cols(h)], axis=1, keepdims=True) for h in heads]
        ct = [cc_ref[h] for h in heads]
        lse_v = [lse_ref[h] for h in heads]
        ones = jnp.ones((t, HEAD), BF16)

        def tile(n, carry, diagonal):
            ks = pl.multiple_of(n * t, t)
            out = []
            for h in heads:
                dq, drow = carry[h]
                kblk = k_ref[pl.ds(ks, t), _head_cols(h)].astype(BF16)
                vblk = v_ref[pl.ds(ks, t), _head_cols(h)].astype(BF16)
                logit = _dot_nt(q[h], kblk) * ATT_SCALE + ct[h] - cr_ref[h, n]
                pr = jnp.exp(logit - lse_v[h])
                if diagonal:
                    causal = lax.broadcasted_iota(jnp.int32, (t, t), 1) <= lax.broadcasted_iota(jnp.int32, (t, t), 0)
                    pr = jnp.where(causal, pr, 0.0)
                ds = pr * (_dot_nt(do[h], vblk) - delta[h])
                dsb = ds.astype(BF16)
                dv_ref[pl.ds(ks, t), _head_cols(h)] += _dot_tn(pr.astype(BF16), do[h])
                dk_ref[pl.ds(ks, t), _head_cols(h)] += _dot_tn(dsb, q[h]) * ATT_SCALE
                dcol_ref[pl.ds(ks, t), _head_cols(h)] += _dot_tn(dsb, ones)
                out.append((dq + _dot_nn(dsb, kblk) * ATT_SCALE,
                            drow + jnp.sum(dsb.astype(F32), axis=1, keepdims=True)))
            return tuple(out)

        init = tuple((jnp.zeros((t, HEAD), F32), jnp.zeros((t, 1), F32)) for _ in heads)
        carry = tile(i, lax.fori_loop(0, i, lambda n, c: tile(n, c, False), init), True)
        for h in heads:
            dq_ref[:, _head_cols(h)] = carry[h][0]
            drow_ref[h] = carry[h][1]

    qspec, kspec, vspec = _head_specs(s, (2 * N_HEADS, 3 * N_HEADS, 4 * N_HEADS))
    ccol, crow = _fox_specs(s)
    dospec = pl.BlockSpec((t, GROUP_W), lambda g, i: (i, N_GROUPS + g))
    blk = pl.BlockSpec((t, GROUP_W), lambda g, i: (i, g))
    whole = pl.BlockSpec((s, GROUP_W), lambda g, i: (0, g))
    shape = jax.ShapeDtypeStruct((s, HALF), F32)
    return _pcall(body, name=name, grid=(N_GROUPS, s // t),
                  in_specs=[qspec, kspec, vspec, ccol, crow, ccol, dospec, blk],
                  out_specs=[blk, whole, whole, whole, ccol],
                  out_shape=[shape, shape, shape, shape, jax.ShapeDtypeStruct((N_HEADS, s, 1), F32)],
                  compiler_params=_params("parallel", "arbitrary"))(p, p, p, c_col, c_row, lse, d_cd, d_out)


def _row_tile(rows, cap):
    for t in (256, 128, 64, 32, 16, 8):
        if t <= cap and rows % t == 0:
            return t
    return rows


def _adamw(w, g, m, v, name):
    rows, cols = w.shape
    tr = _row_tile(rows, 128)
    c1 = 1.0 / (1.0 - ADAM_B1 ** ADAM_STEP)
    c2 = 1.0 / (1.0 - ADAM_B2 ** ADAM_STEP)

    def body(w_ref, g_ref, m_ref, v_ref, d_ref, nm_ref, nv_ref):
        gv = g_ref[...]
        nm = ADAM_B1 * m_ref[...] + (1.0 - ADAM_B1) * gv
        nv = ADAM_B2 * v_ref[...] + (1.0 - ADAM_B2) * (gv * gv)
        nm_ref[...] = nm
        nv_ref[...] = nv
        d_ref[...] = -ADAM_LR * ((nm * c1) / (jnp.sqrt(nv * c2) + ADAM_EPS) + ADAM_WD * w_ref[...])

    spec = pl.BlockSpec((tr, cols), lambda i: (i, 0))
    shape = jax.ShapeDtypeStruct((rows, cols), F32)
    return _pcall(body, name=name, grid=(rows // tr,), in_specs=[spec] * 4, out_specs=[spec] * 3,
                  out_shape=[shape] * 3, compiler_params=_params("parallel"))(w, g, m, v)


def _half_shape(whole_shape, kind):
    if kind == "col":
        return (whole_shape[0] // 2, whole_shape[1] // 4)
    if kind == "row":
        return (whole_shape[0] // 8, whole_shape[1])
    return (whole_shape[1] // 2, whole_shape[2])


def _own_half_spec(whole_shape, kind, tr):
    hr, hc = _half_shape(whole_shape, kind)
    nb = hr // tr
    if kind == "col":
        return pl.BlockSpec((tr, hc), lambda i, pos: (pos[1] * nb + i, pos[0]))
    if kind == "row":
        return pl.BlockSpec((tr, hc), lambda i, pos: ((2 * pos[0] + pos[1]) * nb + i, 0))
    return pl.BlockSpec((None, tr, hc), lambda i, pos: (pos[0], pos[1] * nb + i, 0))


def _sum_partials(pos, grad, landed, kind, name):
    hr, hc = _half_shape(grad.shape, kind)
    tr = _row_tile(hr, 64)

    def body(pos_ref, g_ref, p_ref, o_ref):
        acc = g_ref[...].astype(F32)
        for k in range(N_DEV - 1):
            acc = acc + p_ref[k].astype(F32)
        o_ref[...] = acc

    grid_spec = pltpu.PrefetchScalarGridSpec(
        num_scalar_prefetch=1, grid=(hr // tr,),
        in_specs=[_own_half_spec(grad.shape, kind, tr), pl.BlockSpec((N_DEV - 1, tr, hc), lambda i, pos: (0, i, 0))],
        out_specs=pl.BlockSpec((tr, hc), lambda i, pos: (i, 0)))
    return _pcall(body, name=name, grid_spec=grid_spec, out_shape=jax.ShapeDtypeStruct((hr, hc), F32),
                  compiler_params=_params("parallel"))(pos, grad, landed)


def _adamw_shard(pos, w, g_mine, g_sibling, m, v, name):
    hr, hc = g_mine.shape
    tr = _row_tile(hr, 128)
    nb = hr // tr
    c1 = 1.0 / (1.0 - ADAM_B1 ** ADAM_STEP)
    c2 = 1.0 / (1.0 - ADAM_B2 ** ADAM_STEP)

    def body(pos_ref, w_ref, gm_ref, gs_ref, m_ref, v_ref, g_ref, d_ref, nm_ref, nv_ref):
        mine = (pl.program_id(0) // nb) == pos_ref[1]
        gv = jnp.where(mine, gm_ref[...], gs_ref[...])
        nm = ADAM_B1 * m_ref[...] + (1.0 - ADAM_B1) * gv
        nv = ADAM_B2 * v_ref[...] + (1.0 - ADAM_B2) * (gv * gv)
        g_ref[...] = gv
        nm_ref[...] = nm
        nv_ref[...] = nv
        d_ref[...] = -ADAM_LR * ((nm * c1) / (jnp.sqrt(nv * c2) + ADAM_EPS) + ADAM_WD * w_ref[...])

    full = pl.BlockSpec((tr, hc), lambda i, pos: (i, 0))
    mine_spec = pl.BlockSpec((tr, hc), lambda i, pos: (jnp.clip(i - pos[1] * nb, 0, nb - 1), 0))
    sib_spec = pl.BlockSpec((tr, hc), lambda i, pos: (jnp.clip(i - (1 - pos[1]) * nb, 0, nb - 1), 0))
    grid_spec = pltpu.PrefetchScalarGridSpec(
        num_scalar_prefetch=1, grid=(2 * nb,), in_specs=[full, mine_spec, sib_spec, full, full], out_specs=[full] * 4)
    shape = jax.ShapeDtypeStruct((2 * hr, hc), F32)
    return _pcall(body, name=name, grid_spec=grid_spec, out_shape=[shape] * 4,
                  compiler_params=_params("parallel"))(pos, w, g_mine, g_sibling, m, v)


def _place_shard(pos, shard, kind, name, after=()):
    rows, cols = shard.shape
    tr = _row_tile(rows, 256)
    nb = rows // tr
    if kind == "col":
        out_spec = pl.BlockSpec((tr, cols), lambda i, pos: (i, pos[0]))
    elif kind == "row":
        out_spec = pl.BlockSpec((tr, cols), lambda i, pos: (pos[0] * nb + i, 0))
    else:
        out_spec = pl.BlockSpec((None, tr, cols), lambda i, pos: (pos[0], i, 0))

    def body(pos_ref, s_ref, *rest):
        rest[-1][...] = s_ref[...].astype(BF16)

    grid_spec = pltpu.PrefetchScalarGridSpec(
        num_scalar_prefetch=1, grid=(nb,),
        in_specs=[pl.BlockSpec((tr, cols), lambda i, pos: (i, 0))] + [pl.BlockSpec(memory_space=pl.ANY)] * len(after),
        out_specs=out_spec)
    return _pcall(body, name=name, grid_spec=grid_spec,
                  out_shape=jax.ShapeDtypeStruct(_whole_shape(shard.shape, kind), BF16),
                  compiler_params=_params("parallel"))(pos, shard, *after)


N_DEV = 8
RELATIONS = [(r >> 2 & 1, r >> 1 & 1, r & 1) for r in range(1, N_DEV)]


def _position():
    return lax.axis_index("x"), lax.axis_index("y"), lax.axis_index("c")


def _related(pos, rel):
    return tuple(1 - p if f else p for p, f in zip(pos, rel))


def _index(pos):
    return 4 * pos[0] + 2 * pos[1] + pos[2]


def _window(ref, kind, pos):
    px, py, pc = pos
    j = 2 * px + py
    if kind == "col":
        r, c = ref.shape
        return ref.at[pl.ds(pc * (r // 2), r // 2), pl.ds(pl.multiple_of(j * (c // 4), 128), c // 4)]
    if kind == "row":
        rj = ref.shape[0] // 4
        return ref.at[pl.ds(j * rj + pc * (rj // 2), rj // 2), :]
    r = ref.shape[1]
    return ref.at[j, pl.ds(pc * (r // 2), r // 2), :]


def _whole_shape(shard_shape, kind):
    r, c = shard_shape
    return {"col": (r, 4 * c), "row": (4 * r, c), "maj": (4, r, c)}[kind]


SEM_SPEC = pl.BlockSpec(memory_space=pltpu.SEMAPHORE)
ANY_SPEC = pl.BlockSpec(memory_space=pl.ANY)
DATAFLOW = pltpu.SideEffectType.DATAFLOW_SIDE_EFFECTING
TOKEN = jax.ShapeDtypeStruct((8, 128), F32)


def _hbm(a):
    return pltpu.with_memory_space_constraint(a, pltpu.HBM)


def _chips(x, y):
    return [(1 - x, y), (x, 1 - y), (1 - x, 1 - y)]


def _split_start(body, name, buffers, n_sems, after=()):
    n = len(buffers)

    def wrapped(*refs):
        body(refs[:n], refs[n], refs[n + 1])
        refs[-1][...] = jnp.zeros_like(refs[-1])

    outs = _pcall(
        wrapped, after=after, name=name, in_specs=[HBM_SPEC] * n,
        out_specs=[SEM_SPEC, SEM_SPEC] + [HBM_SPEC] * n + [VMEM_SPEC],
        out_shape=[pltpu.SemaphoreType.DMA(n_sems), pltpu.SemaphoreType.DMA(n_sems)]
        + [pltpu.HBM(b.shape, b.dtype) for b in buffers] + [TOKEN],
        input_output_aliases={i: 2 + i for i in range(n)},
        compiler_params=pltpu.CompilerParams(has_side_effects=DATAFLOW))(*[_hbm(b) for b in buffers])
    return outs[0], outs[1], list(outs[2:2 + n]), outs[2 + n]


def _split_wait(body, name, buffers, send_sems, recv_sems, after):
    n = len(buffers)

    def wrapped(*refs):
        body(refs[:n], refs[n], refs[n + 1])

    outs = _pcall(
        wrapped, name=name, in_specs=[HBM_SPEC] * n + [SEM_SPEC, SEM_SPEC, ANY_SPEC],
        out_specs=[HBM_SPEC] * n, out_shape=[pltpu.HBM(b.shape, b.dtype) for b in buffers],
        input_output_aliases={i: i for i in range(n)},
        compiler_params=pltpu.CompilerParams(has_side_effects=DATAFLOW))(*buffers, send_sems, recv_sems, after)
    return list(outs)


def _gather_start(wholes, kinds, name, after=()):
    def body(w_refs, send_sems, recv_sems):
        x, y, c = _position()
        for w, ref in enumerate(w_refs):
            mine = _window(ref, kinds[w], (x, y, c))
            for k, chip in enumerate(_chips(x, y)):
                pltpu.make_async_remote_copy(src_ref=mine, dst_ref=mine, send_sem=send_sems.at[3 * w + k],
                                             recv_sem=recv_sems.at[3 * w + k], device_id=(*chip, c),
                                             device_id_type=MESH).start()

    return _split_start(body, name, wholes, (3 * len(wholes),), after)


def _gather_forward(wholes, kinds, send1, recv1, after, name):
    n = len(wholes)

    def wrapped(*refs):
        w_refs, s1, r1, s2, r2 = refs[:n], refs[n], refs[n + 1], refs[n + 3], refs[n + 4]
        x, y, c = _position()
        for k, chip in enumerate(_chips(x, y)):
            for w, ref in enumerate(w_refs):
                theirs = _window(ref, kinds[w], (*chip, c))
                pltpu.make_async_remote_copy(src_ref=theirs, dst_ref=theirs, send_sem=s1.at[3 * w + k],
                                             recv_sem=r1.at[3 * w + k], device_id=(*chip, c),
                                             device_id_type=MESH).wait_recv()
                pltpu.make_async_remote_copy(src_ref=theirs, dst_ref=theirs, send_sem=s2.at[3 * w + k],
                                             recv_sem=r2.at[3 * w + k], device_id=(x, y, 1 - c),
                                             device_id_type=MESH).start()
        for w, ref in enumerate(w_refs):
            mine = _window(ref, kinds[w], (x, y, c))
            for k, chip in enumerate(_chips(x, y)):
                pltpu.make_async_remote_copy(src_ref=mine, dst_ref=mine, send_sem=s1.at[3 * w + k],
                                             recv_sem=r1.at[3 * w + k], device_id=(*chip, c),
                                             device_id_type=MESH).wait_send()
        refs[-1][...] = jnp.zeros_like(refs[-1])

    outs = _pcall(
        wrapped, name=name, in_specs=[HBM_SPEC] * n + [SEM_SPEC, SEM_SPEC, ANY_SPEC],
        out_specs=[SEM_SPEC, SEM_SPEC] + [HBM_SPEC] * n + [VMEM_SPEC],
        out_shape=[pltpu.SemaphoreType.DMA((3 * n,)), pltpu.SemaphoreType.DMA((3 * n,))]
        + [pltpu.HBM(b.shape, b.dtype) for b in wholes] + [TOKEN],
        input_output_aliases={i: 2 + i for i in range(n)},
        compiler_params=pltpu.CompilerParams(has_side_effects=DATAFLOW))(*wholes, send1, recv1, after)
    return outs[0], outs[1], list(outs[2:2 + n]), outs[2 + n]


def _gather_finish(wholes, kinds, send2, recv2, after, name):
    def body(w_refs, s2, r2):
        x, y, c = _position()
        for k, chip in enumerate(_chips(x, y)):
            for w, ref in enumerate(w_refs):
                sent = _window(ref, kinds[w], (*chip, c))
                got = _window(ref, kinds[w], (*chip, 1 - c))
                pltpu.make_async_remote_copy(src_ref=sent, dst_ref=got, send_sem=s2.at[3 * w + k],
                                             recv_sem=r2.at[3 * w + k], device_id=(x, y, 1 - c),
                                             device_id_type=MESH).wait()

    return _split_wait(body, name, wholes, send2, recv2, after)


def _gather_small(small, after=()):
    def body(s_ref, o_ref, send_sems, recv_sems, local_sem):
        x, y, c = _position()
        mine = pltpu.make_async_copy(s_ref, o_ref.at[2 * x + y], local_sem)
        mine.start()
        sends = []
        for k, chip in enumerate(_chips(x, y)):
            cp = pltpu.make_async_remote_copy(src_ref=s_ref, dst_ref=o_ref.at[2 * x + y], send_sem=send_sems.at[k],
                                              recv_sem=recv_sems.at[k], device_id=(*chip, c), device_id_type=MESH)
            cp.start()
            sends.append(cp)
        for k, chip in enumerate(_chips(x, y)):
            pltpu.make_async_remote_copy(src_ref=s_ref, dst_ref=o_ref.at[2 * chip[0] + chip[1]], send_sem=send_sems.at[k],
                                         recv_sem=recv_sems.at[k], device_id=(*chip, c), device_id_type=MESH).wait_recv()
        for cp in sends:
            cp.wait_send()
        mine.wait()

    return _pcall(body, after=after, name="gather_small", in_specs=[HBM_SPEC], out_specs=HBM_SPEC,
                  out_shape=jax.ShapeDtypeStruct((4,) + small.shape, small.dtype),
                  scratch_shapes=[pltpu.SemaphoreType.DMA((3,)), pltpu.SemaphoreType.DMA((3,)),
                                  pltpu.SemaphoreType.DMA(())])(small)


def _scatter_copies(g_refs, land_refs, kinds, send_sems, recv_sems):
    me = _position()
    copies = []
    for k, rel in enumerate(RELATIONS):
        peer = _related(me, rel)
        for w, (g_ref, land_ref) in enumerate(zip(g_refs, land_refs)):
            copies.append(pltpu.make_async_remote_copy(
                src_ref=_window(g_ref, kinds[w], peer), dst_ref=land_ref.at[k],
                send_sem=send_sems.at[7 * w + k], recv_sem=recv_sems.at[7 * w + k], device_id=peer,
                device_id_type=MESH))
    return copies


def _scatter_start(grads, kinds, name):
    n = len(grads)
    lands = [lax.empty((N_DEV - 1,) + _half_shape(g.shape, kd), g.dtype) for g, kd in zip(grads, kinds)]

    def body(refs, send_sems, recv_sems):
        for cp in _scatter_copies(refs[:n], refs[n:], kinds, send_sems, recv_sems):
            cp.start()

    send, recv, thru, token = _split_start(body, name, list(grads) + lands, ((N_DEV - 1) * n,))
    return send, recv, thru[:n], thru[n:], token


def _scatter_wait(grads, lands, kinds, send, recv, after, name):
    n = len(grads)

    def body(refs, send_sems, recv_sems):
        for cp in _scatter_copies(refs[:n], refs[n:], kinds, send_sems, recv_sems):
            cp.wait()

    out = _split_wait(body, name, list(grads) + list(lands), send, recv, after)
    return out[:n], out[n:]


def _swap_start(halves, name):
    n = len(halves)
    lands = [lax.empty(h.shape, h.dtype) for h in halves]

    def body(refs, send_sems, recv_sems):
        x, y, c = _position()
        for w in range(n):
            pltpu.make_async_remote_copy(src_ref=refs[w], dst_ref=refs[n + w], send_sem=send_sems.at[w],
                                         recv_sem=recv_sems.at[w], device_id=(x, y, 1 - c), device_id_type=MESH).start()

    send, recv, thru, token = _split_start(body, name, list(halves) + lands, (n,))
    return send, recv, thru[:n], thru[n:], token


def _swap_wait(halves, lands, send, recv, after, name):
    n = len(halves)

    def body(refs, send_sems, recv_sems):
        x, y, c = _position()
        for w in range(n):
            pltpu.make_async_remote_copy(src_ref=refs[w], dst_ref=refs[n + w], send_sem=send_sems.at[w],
                                         recv_sem=recv_sems.at[w], device_id=(x, y, 1 - c), device_id_type=MESH).wait()

    out = _split_wait(body, name, list(halves) + list(lands), send, recv, after)
    return out[:n], out[n:]


def _allreduce_small(v, after=()):
    rows = v.shape[0]

    def body(v_ref, o_ref, recv_ref, send_sems, recv_sems):
        me = _position()
        recv_ref[_index(me)] = v_ref[...]
        sends = []
        for k, rel in enumerate(RELATIONS):
            peer = _related(me, rel)
            cp = pltpu.make_async_remote_copy(
                src_ref=v_ref, dst_ref=recv_ref.at[_index(me)],
                send_sem=send_sems.at[k], recv_sem=recv_sems.at[k], device_id=peer, device_id_type=MESH)
            cp.start()
            sends.append(cp)
        for k, rel in enumerate(RELATIONS):
            peer = _related(me, rel)
            pltpu.make_async_remote_copy(
                src_ref=v_ref, dst_ref=recv_ref.at[_index(peer)],
                send_sem=send_sems.at[k], recv_sem=recv_sems.at[k], device_id=peer, device_id_type=MESH).wait_recv()
        for cp in sends:
            cp.wait_send()
        acc = recv_ref[0]
        for k in range(1, N_DEV):
            acc = acc + recv_ref[k]
        o_ref[...] = acc

    return _pcall(body, after=after, name="allreduce_small", in_specs=[VMEM_SPEC], out_specs=VMEM_SPEC,
                  out_shape=jax.ShapeDtypeStruct((rows, 128), F32),
                  scratch_shapes=[pltpu.VMEM((N_DEV, rows, 128), F32), pltpu.SemaphoreType.DMA((7,)),
                                  pltpu.SemaphoreType.DMA((7,))],
                  compiler_params=pltpu.CompilerParams(vmem_limit_bytes=VMEM_LIMIT_BYTES))(v)


def _pack(arrays):
    flat = []
    for a in arrays:
        a = a.reshape(-1)
        flat.append(jnp.pad(a, (0, -a.shape[0] % 128)))
    flat = jnp.concatenate(flat)
    flat = jnp.pad(flat, (0, -flat.shape[0] % 1024))
    return flat.reshape(-1, 128)


def _unpack(packed, shapes):
    flat = packed.reshape(-1)
    out, at = [], 0
    for shp in shapes:
        size = 1
        for d in shp:
            size *= d
        out.append(flat[at:at + size].reshape(shp))
        at += size + (-size % 128)
    return out


WEIGHTS = ['l0_mix_norm_g', 'l0_w_in', 'l0_sc_conv_w', 'l0_w_out', 'l0_ffn_norm_g', 'l0_ffn_up', 'l0_ffn_conv_w',
           'l0_ffn_down', 'l1_mix_norm_g', 'l1_w_in', 'l1_fox_b_f', 'l1_sg_w', 'l1_sg_b', 'l1_sg_norm_g', 'l1_w_out',
           'l1_ffn_norm_g', 'l1_ffn_up', 'l1_ffn_conv_w', 'l1_ffn_down', 'final_norm_g']
BIG = {'l0_w_in': 'col', 'l0_w_out': 'row', 'l0_ffn_up': 'col', 'l0_ffn_down': 'row',
       'l1_w_in': 'maj', 'l1_w_out': 'row', 'l1_ffn_up': 'col', 'l1_ffn_down': 'row'}
GATHER_GROUPS = [['l0_w_in'], ['l0_w_out'], ['l0_ffn_up', 'l0_ffn_down'], ['l1_w_in', 'l1_w_out'],
                 ['l1_ffn_up', 'l1_ffn_down']]
CONV = ['l0_sc_conv_w', 'l0_ffn_conv_w', 'l1_ffn_conv_w']
SMALL = [n for n in WEIGHTS if n not in BIG]
IN_CD = 5 * HALF + N_HEADS


def _ffn_forward(x, g, weights_after, conv_w, tag, behind_down=None):
    h = _rmsnorm_fwd(x, g, tag + "_norm")
    w_up, w_down = weights_after(h)
    u = _matmul(h, w_up, "nn", F32, tag + "_up")
    f = _ffn_act_fwd(u, conv_w, tag + "_act")
    after = [behind_down(f)] if behind_down else ()
    return _matmul(f, w_down, "nn", F32, tag + "_down", res=x, after=after), (h, u, f)


def _ffn_backward(x, g, w_up, conv_w, w_down, saved, d_out, send_grads, tag):
    h, u, f = saved
    dw_down = _matmul(f, d_out, "tn", BF16, tag + "_dwdown")
    d_f = _matmul(d_out, w_down, "nt", F32, tag + "_df")
    du_gate, du_up, dcw_gate, dcw_up = _ffn_act_bwd(u, conv_w, d_f, tag + "_dact")
    du = jnp.concatenate([du_gate, du_up], axis=1)
    dw_up = _matmul(h, du, "tn", BF16, tag + "_dwup")
    dh = _matmul(du, w_up, "nt", F32, tag + "_dh", after=[send_grads(dw_up, dw_down)])
    dx, dg = _rmsnorm_bwd(x, g, dh, d_out, tag + "_dnorm")
    return dx, dg, jnp.concatenate([dcw_gate, dcw_up], axis=1)


def kernel(x, l0_mix_norm_g, l0_w_in, l0_sc_conv_w, l0_w_out, l0_ffn_norm_g, l0_ffn_up, l0_ffn_conv_w, l0_ffn_down, l1_mix_norm_g, l1_w_in, l1_fox_b_f, l1_sg_w, l1_sg_b, l1_sg_norm_g, l1_w_out, l1_ffn_norm_g, l1_ffn_up, l1_ffn_conv_w, l1_ffn_down, final_norm_g, loss_target, m_l0_mix_norm_g, m_l0_w_in, m_l0_sc_conv_w, m_l0_w_out, m_l0_ffn_norm_g, m_l0_ffn_up, m_l0_ffn_conv_w, m_l0_ffn_down, m_l1_mix_norm_g, m_l1_w_in, m_l1_fox_b_f, m_l1_sg_w, m_l1_sg_b, m_l1_sg_norm_g, m_l1_w_out, m_l1_ffn_norm_g, m_l1_ffn_up, m_l1_ffn_conv_w, m_l1_ffn_down, m_final_norm_g, v_l0_mix_norm_g, v_l0_w_in, v_l0_sc_conv_w, v_l0_w_out, v_l0_ffn_norm_g, v_l0_ffn_up, v_l0_ffn_conv_w, v_l0_ffn_down, v_l1_mix_norm_g, v_l1_w_in, v_l1_fox_b_f, v_l1_sg_w, v_l1_sg_b, v_l1_sg_norm_g, v_l1_w_out, v_l1_ffn_norm_g, v_l1_ffn_up, v_l1_ffn_conv_w, v_l1_ffn_down, v_final_norm_g):
    given = (l0_mix_norm_g, l0_w_in, l0_sc_conv_w, l0_w_out, l0_ffn_norm_g, l0_ffn_up, l0_ffn_conv_w, l0_ffn_down, l1_mix_norm_g, l1_w_in, l1_fox_b_f, l1_sg_w, l1_sg_b, l1_sg_norm_g, l1_w_out, l1_ffn_norm_g, l1_ffn_up, l1_ffn_conv_w, l1_ffn_down, final_norm_g)
    given_m = (m_l0_mix_norm_g, m_l0_w_in, m_l0_sc_conv_w, m_l0_w_out, m_l0_ffn_norm_g, m_l0_ffn_up, m_l0_ffn_conv_w, m_l0_ffn_down, m_l1_mix_norm_g, m_l1_w_in, m_l1_fox_b_f, m_l1_sg_w, m_l1_sg_b, m_l1_sg_norm_g, m_l1_w_out, m_l1_ffn_norm_g, m_l1_ffn_up, m_l1_ffn_conv_w, m_l1_ffn_down, m_final_norm_g)
    given_v = (v_l0_mix_norm_g, v_l0_w_in, v_l0_sc_conv_w, v_l0_w_out, v_l0_ffn_norm_g, v_l0_ffn_up, v_l0_ffn_conv_w, v_l0_ffn_down, v_l1_mix_norm_g, v_l1_w_in, v_l1_fox_b_f, v_l1_sg_w, v_l1_sg_b, v_l1_sg_norm_g, v_l1_w_out, v_l1_ffn_norm_g, v_l1_ffn_up, v_l1_ffn_conv_w, v_l1_ffn_down, v_final_norm_g)
    wt = dict(zip(WEIGHTS, given))
    mom = dict(zip(WEIGHTS, given_m))
    var = dict(zip(WEIGHTS, given_v))
    s = x.shape[1]
    t = ATT_BLOCK
    x0, target = x[0], loss_target[0]
    chip = 2 * lax.axis_index("x") + lax.axis_index("y")

    pos = jnp.stack([chip, lax.axis_index("c")]).astype(jnp.int32)
    big_names = list(BIG)

    conv_widths = [wt[n].shape[1] for n in CONV]
    conv_all = _gather_small(jnp.concatenate([wt[n] for n in CONV], axis=1))
    conv_full, at = {}, 0
    for n, cw in zip(CONV, conv_widths):
        conv_full[n] = jnp.transpose(conv_all[:, :, at:at + cw], (1, 0, 2)).reshape(3, 4 * cw)
        at += cw
    gathers, token = [], conv_all
    for gi, names in enumerate(GATHER_GROUPS):
        placed = [_place_shard(pos, wt[n], BIG[n], "place_" + n, [token]) for n in names]
        send, recv, thru, token = _gather_start(placed, [BIG[n] for n in names], "gather_start_%d" % gi, [token])
        gathers.append((send, recv, thru))
    full = {}

    def forward_gather(gi, after):
        send, recv, thru = gathers[gi]
        kinds = [BIG[n] for n in GATHER_GROUPS[gi]]
        gathers[gi] = _gather_forward(thru, kinds, send, recv, after, "gather_forward_%d" % gi)
        return gathers[gi][3]

    def finish_gather(gi, after):
        send, recv, thru, tok = gathers[gi]
        names = GATHER_GROUPS[gi]
        wholes = _gather_finish(thru, [BIG[n] for n in names], send, recv, tok if after is None else after,
                                "gather_finish_%d" % gi)
        full.update(zip(names, wholes))

    def vec(name):
        return wt[name].reshape(1, -1)

    h0 = _rmsnorm_fwd(x0, vec('l0_mix_norm_g'), "l0_mix_norm", after=[token])
    forward_gather(0, h0)
    finish_gather(0, None)
    p0 = _matmul(h0, full['l0_w_in'], "nn", F32, "l0_in")
    a_out, sb_carries = _sb_fwd(p0, "l0_sb", after=[forward_gather(1, p0)])
    finish_gather(1, a_out)
    b_out = _sc_fwd(p0, conv_full['l0_sc_conv_w'], "l0_sc")
    ab0 = jnp.concatenate([a_out.astype(BF16), b_out], axis=1)
    x1 = _matmul(ab0, full['l0_w_out'], "nn", F32, "l0_out", res=x0, after=[forward_gather(2, b_out)])

    def ffn0_weights(h):
        finish_gather(2, h)
        return full['l0_ffn_up'], full['l0_ffn_down']

    x2, ffn0_saved = _ffn_forward(x1, vec('l0_ffn_norm_g'), ffn0_weights, conv_full['l0_ffn_conv_w'], "l0_ffn",
                                  behind_down=lambda f: forward_gather(3, f))
    h2 = _rmsnorm_fwd(x2, vec('l1_mix_norm_g'), "l1_mix_norm")
    finish_gather(3, h2)
    w_in1 = jnp.transpose(full['l1_w_in'], (1, 0, 2)).reshape(D_MODEL, IN_CD)
    w_in1_main = w_in1[:, :5 * HALF]
    w_in1_f = jnp.pad(w_in1[:, 5 * HALF:], ((0, 0), (0, 128 - N_HEADS)))
    p1 = _matmul(h2, w_in1_main, "nn", F32, "l1_in")
    f_logit = _matmul(h2, w_in1_f, "nn", F32, "l1_in_f")
    b_f = jnp.pad(wt['l1_fox_b_f'], (0, 128 - N_HEADS)).reshape(1, 128)
    c_heads = _fox_prep(f_logit, b_f, "l1_fox_prep")[:, :N_HEADS].T
    c_col = c_heads[:, :, None]
    c_row = c_heads.reshape(N_HEADS, s // t, 1, t)
    sg_bias = jnp.repeat(wt['l1_sg_b'].T, HEAD, axis=1)
    sg_gain = vec('l1_sg_norm_g')
    c_out = _sg_fwd(p1, wt['l1_sg_w'], sg_bias, sg_gain, "l1_sg")
    d_out, lse = _fox_fwd(p1, c_col, c_row, "l1_fox", after=[forward_gather(4, c_out)])
    cd1 = jnp.concatenate([c_out, d_out.astype(BF16)], axis=1)
    x3 = _matmul(cd1, full['l1_w_out'], "nn", F32, "l1_out", res=x2)

    def ffn1_weights(h):
        finish_gather(4, h)
        return full['l1_ffn_up'], full['l1_ffn_down']

    x4, ffn1_saved = _ffn_forward(x3, vec('l1_ffn_norm_g'), ffn1_weights, conv_full['l1_ffn_conv_w'], "l1_ffn")
    dx4, dg_final, loss_part = _loss_head(x4, vec('final_norm_g'), target, "loss_head")

    grads = {'final_norm_g': dg_final}
    scatters = []

    def send_grads(names):
        def start(*group):
            send, recv, thru, lands, tok = _scatter_start(list(group), [BIG[n] for n in names],
                                                          "scatter_start_%d" % len(scatters))
            scatters.append((names, send, recv, thru, lands))
            return tok
        return start

    dx3, grads['l1_ffn_norm_g'], grads['l1_ffn_conv_w'] = _ffn_backward(
        x3, vec('l1_ffn_norm_g'), full['l1_ffn_up'], conv_full['l1_ffn_conv_w'], full['l1_ffn_down'], ffn1_saved, dx4,
        send_grads(['l1_ffn_up', 'l1_ffn_down']), "l1_ffn")
    dw_out1 = _matmul(cd1, dx3, "tn", BF16, "l1_dwout")
    d_cd = _matmul(dx3, full['l1_w_out'], "nt", F32, "l1_dcd")
    du, dv, grads['l1_sg_w'], db_sg, grads['l1_sg_norm_g'] = _sg_bwd(p1, wt['l1_sg_w'], sg_bias, sg_gain, d_cd, "l1_dsg")
    grads['l1_sg_b'] = db_sg[:, :N_HEADS].T
    dq, dk, dvv, dcol, drow = _fox_bwd(p1, c_col, c_row, lse, d_cd, d_out, "l1_dfox")
    pad8 = ((0, 0), (0, 128 - N_HEADS))
    d_f_logit, d_b_f = _fox_post(jnp.pad(drow[:, :, 0].T, pad8), jnp.pad(dcol[:, ::HEAD], pad8), f_logit, b_f,
                                 "l1_fox_post")
    grads['l1_fox_b_f'] = d_b_f[0, :N_HEADS]
    dp1 = jnp.concatenate([a.astype(BF16) for a in (du, dv, dq, dk, dvv)], axis=1)
    dw_main = _matmul(h2, dp1, "tn", BF16, "l1_dwin")
    dw_f = _matmul(h2, d_f_logit, "tn", BF16, "l1_dwin_f")
    dw_in1 = jnp.concatenate([dw_main, dw_f[:, :N_HEADS]], axis=1)
    dw_in1 = jnp.transpose(dw_in1.reshape(D_MODEL, 4, IN_CD // 4), (1, 0, 2))
    dh2 = _matmul(dp1, w_in1_main, "nt", F32, "l1_dh", after=[send_grads(['l1_w_out', 'l1_w_in'])(dw_out1, dw_in1)])
    dh2 = _matmul(d_f_logit, w_in1_f, "nt", F32, "l1_dh_f", res=dh2)
    dx2, grads['l1_mix_norm_g'] = _rmsnorm_bwd(x2, vec('l1_mix_norm_g'), dh2, dx3, "l1_dmix_norm")
    dx1, grads['l0_ffn_norm_g'], grads['l0_ffn_conv_w'] = _ffn_backward(
        x1, vec('l0_ffn_norm_g'), full['l0_ffn_up'], conv_full['l0_ffn_conv_w'], full['l0_ffn_down'], ffn0_saved, dx2,
        send_grads(['l0_ffn_up', 'l0_ffn_down']), "l0_ffn")
    dw_out0 = _matmul(ab0, dx1, "tn", BF16, "l0_dwout")
    d_ab = _matmul(dx1, full['l0_w_out'], "nt", F32, "l0_dab")
    dq0, dk0, dv0 = _sb_bwd(p0, d_ab, sb_carries, "l0_dsb")
    dgb, dgc, dhin, grads['l0_sc_conv_w'] = _sc_bwd(p0, conv_full['l0_sc_conv_w'], d_ab, "l0_dsc")
    dp0 = jnp.concatenate([a.astype(BF16) for a in (dq0, dk0, dv0, dgb, dgc, dhin)], axis=1)
    dw_in0 = _matmul(h0, dp0, "tn", BF16, "l0_dwin")
    dh0 = _matmul(dp0, full['l0_w_in'], "nt", F32, "l0_dh", after=[send_grads(['l0_w_out', 'l0_w_in'])(dw_out0, dw_in0)])
    dx0, grads['l0_mix_norm_g'] = _rmsnorm_bwd(x0, vec('l0_mix_norm_g'), dh0, dx1, "l0_dmix_norm")

    shard_grads, delta, new_m, new_v, swaps = {}, {}, {}, {}, {}

    def reduce_group(gi, after):
        names, send, recv, thru, lands = scatters[gi]
        kinds = [BIG[n] for n in names]
        g_thru, landed = _scatter_wait(thru, lands, kinds, send, recv, after, "scatter_wait_%d" % gi)
        halves = [_sum_partials(pos, g, ld, kd, "sum_" + n) for n, g, ld, kd in zip(names, g_thru, landed, kinds)]
        s_send, s_recv, h_thru, s_lands, tok = _swap_start(halves, "swap_start_%d" % gi)
        swaps[gi] = (names, s_send, s_recv, h_thru, s_lands)
        return tok

    def update_group(gi, after):
        names, s_send, s_recv, h_thru, s_lands = swaps[gi]
        mine, theirs = _swap_wait(h_thru, s_lands, s_send, s_recv, after, "swap_wait_%d" % gi)
        for n, gm, gs in zip(names, mine, theirs):
            shard_grads[n], delta[n], new_m[n], new_v[n] = _adamw_shard(pos, wt[n], gm, gs, mom[n], var[n], "adamw_" + n)
        return delta[names[-1]]

    after = reduce_group(1, reduce_group(0, dx0))
    after = update_group(1, update_group(0, after))
    after = reduce_group(3, reduce_group(2, after))
    small_shapes = [conv_full[n].shape if n in CONV else wt[n].shape for n in SMALL] + [loss_part.shape]
    small_all = _allreduce_small(_pack([grads[n] for n in SMALL] + [loss_part]), [after])
    small_sums = _unpack(small_all, small_shapes)
    loss = small_sums[-1][0, 0]
    for n, g in zip(SMALL, small_sums):
        shard_grads[n] = lax.dynamic_slice_in_dim(g, chip * wt[n].shape[1], wt[n].shape[1], axis=1) if n in CONV else g
    update_group(3, update_group(2, small_all))
    shapes = [wt[n].shape for n in SMALL]
    packed = _adamw(_pack([wt[n] for n in SMALL]), _pack([shard_grads[n] for n in SMALL]),
                    _pack([mom[n] for n in SMALL]), _pack([var[n] for n in SMALL]), "adamw_small")
    for out, pk in zip((delta, new_m, new_v), packed):
        out.update(zip(SMALL, _unpack(pk, shapes)))

    return (loss, dx0[None], *[shard_grads[n] for n in WEIGHTS], *[delta[n] for n in WEIGHTS],
            *[new_m[n] for n in WEIGHTS], *[new_v[n] for n in WEIGHTS])
```

```python
import functools

import jax
import jax.numpy as jnp
from jax import lax
from jax.experimental import pallas as pl
from jax.experimental.pallas import tpu as pltpu

F32 = jnp.float32
BF16 = jnp.bfloat16

D_MODEL = 2048
HEAD = 128
N_HEADS = 8
HALF = N_HEADS * HEAD
D_FF = 5632
EPS = 1e-6
ATT_SCALE = HEAD ** -0.5
ATT_BLOCK = 256
NEG = -1e30

ADAM_LR = 0.001
ADAM_B1 = 0.9
ADAM_B2 = 0.999
ADAM_EPS = 1e-08
ADAM_WD = 0.01
ADAM_STEP = 10

VMEM_LIMIT_BYTES = 48 * 1024 * 1024
MESH = pl.DeviceIdType.MESH
HBM_SPEC = pl.BlockSpec(memory_space=pltpu.HBM)
VMEM_SPEC = pl.BlockSpec(memory_space=pltpu.VMEM)


def _pcall(body, after=(), **kw):
    if not after:
        return pl.pallas_call(body, **kw)
    n_in, n_after, inner = len(kw["in_specs"]), len(after), body
    kw["in_specs"] = list(kw["in_specs"]) + [pl.BlockSpec(memory_space=pl.ANY)] * n_after

    def body(*refs):
        inner(*refs[:n_in], *refs[n_in + n_after:])

    call = pl.pallas_call(body, **kw)
    return lambda *args: call(*args, *after)


def _params(*semantics):
    return pltpu.CompilerParams(dimension_semantics=semantics, vmem_limit_bytes=VMEM_LIMIT_BYTES)


def _pick(n, cap):
    best = None
    for t in range(128, min(n, cap) + 1, 128):
        if n % t == 0:
            best = t
    return n if best is None else best


def _dot(a, b, dims):
    return lax.dot_general(a, b, (dims, ((), ())), preferred_element_type=F32)


def _dot_nn(a, b):
    return _dot(a, b, ((1,), (0,)))


def _dot_nt(a, b):
    return _dot(a, b, ((1,), (1,)))


def _dot_tn(a, b):
    return _dot(a, b, ((0,), (0,)))


def _split3(x):
    hi = x.astype(BF16)
    r1 = x - hi.astype(F32)
    mid = r1.astype(BF16)
    lo = (r1 - mid.astype(F32)).astype(BF16)
    return hi, mid, lo


def _softplus(z):
    return jnp.maximum(z, 0.0) + jnp.log1p(jnp.exp(-jnp.abs(z)))


def _log_sigmoid(z):
    return jnp.minimum(z, 0.0) - jnp.log1p(jnp.exp(-jnp.abs(z)))


_GELU_K = 0.7978845608028654


def _gelu(x):
    return 0.5 * x * (1.0 + jnp.tanh(_GELU_K * (x + 0.044715 * x * x * x)))


def _gelu_grad(x):
    t = jnp.tanh(_GELU_K * (x + 0.044715 * x * x * x))
    return 0.5 * (1.0 + t) + 0.5 * x * (1.0 - t * t) * _GELU_K * (1.0 + 3.0 * 0.044715 * x * x)


SUBLANES = 8


def _shift_down(x, k):
    rolled = pltpu.roll(x, k, axis=0)
    head = rolled[:SUBLANES]
    head = jnp.where(lax.broadcasted_iota(jnp.int32, head.shape, 0) >= k, head, 0.0)
    return jnp.concatenate([head, rolled[SUBLANES:]], axis=0)


def _shift_up(x, k):
    n = x.shape[0]
    rolled = pltpu.roll(x, n - k, axis=0)
    tail = rolled[n - SUBLANES:]
    tail = jnp.where(lax.broadcasted_iota(jnp.int32, tail.shape, 0) < SUBLANES - k, tail, 0.0)
    return jnp.concatenate([rolled[:n - SUBLANES], tail], axis=0)


def _conv3(s, w, shifted=None):
    s1, s2 = shifted if shifted else (_shift_down(s, 1), _shift_down(s, 2))
    return w[0:1, :] * s2 + w[1:2, :] * s1 + w[2:3, :] * s


def _conv3_transpose(d, w):
    return w[2:3, :] * d + w[1:2, :] * _shift_up(d, 1) + w[0:1, :] * _shift_up(d, 2)


def _conv3_wgrad(d, s, shifted, dw_ref):
    s1, s2 = shifted
    dw_ref[0:1, :] = jnp.sum(d * s2, axis=0, keepdims=True)
    dw_ref[1:2, :] = jnp.sum(d * s1, axis=0, keepdims=True)
    dw_ref[2:3, :] = jnp.sum(d * s, axis=0, keepdims=True)


MM_TILE_M, MM_TILE_N, MM_TILE_K = 1408, 512, 2816


def _matmul(a, b, mode, out_dtype, name, res=None, after=()):
    a_parts = a.shape[0] if a.ndim == 3 else 1
    b_parts = b.shape[0] if b.ndim == 3 else 1
    a_shape = (a.shape[1], a_parts * a.shape[2]) if a.ndim == 3 else a.shape
    b_shape = (b.shape[1], b_parts * b.shape[2]) if b.ndim == 3 else b.shape
    assert (a_parts == 1 or mode != "tn") and (b_parts == 1 or mode == "tn")
    if mode == "nn":
        (m, k), (k2, n) = a_shape, b_shape
    elif mode == "nt":
        (m, k), (n, k2) = a_shape, b_shape
    else:
        (k, m), (k2, n) = a_shape, b_shape
    assert k == k2, (a.shape, b.shape, mode)
    tm, tn, tk = _pick(m, MM_TILE_M), _pick(n // b_parts, MM_TILE_N), _pick(k // a_parts, MM_TILE_K)
    nk = k // tk
    if mode == "tn":
        a_spec = pl.BlockSpec((tk, tm), lambda i, j, kk: (kk, i))
    elif a_parts > 1:
        per = nk // a_parts
        a_spec = pl.BlockSpec((None, tm, tk), lambda i, j, kk: (kk // per, i, kk % per))
    else:
        a_spec = pl.BlockSpec((tm, tk), lambda i, j, kk: (i, kk))
    if mode == "nt":
        b_spec = pl.BlockSpec((tn, tk), lambda i, j, kk: (j, kk))
    elif b_parts > 1:
        per = n // b_parts // tn
        b_spec = pl.BlockSpec((None, tk, tn), lambda i, j, kk: (j // per, kk, j % per))
    else:
        b_spec = pl.BlockSpec((tk, tn), lambda i, j, kk: (kk, j))
    o_spec = pl.BlockSpec((tm, tn), lambda i, j, kk: (i, j))
    dims = {"nn": ((1,), (0,)), "nt": ((1,), (1,)), "tn": ((0,), (0,))}[mode]
    has_res = res is not None

    def body(*refs):
        a_ref, b_ref = refs[0], refs[1]
        r_ref = refs[2] if has_res else None
        o_ref = refs[3] if has_res else refs[2]
        part = _dot(a_ref[...].astype(BF16), b_ref[...].astype(BF16), dims)

        def finish(total):
            if has_res:
                total = total + r_ref[...]
            o_ref[...] = total.astype(out_dtype)

        if nk == 1:
            finish(part)
        else:
            acc_ref = refs[-1]
            kk = pl.program_id(2)

            @pl.when(kk == 0)
            def _():
                acc_ref[...] = part

            @pl.when(kk > 0)
            def _():
                acc_ref[...] += part

            @pl.when(kk == nk - 1)
            def _():
                finish(acc_ref[...])

    in_specs = [a_spec, b_spec] + ([o_spec] if has_res else [])
    args = (a, b) + ((res,) if has_res else ())
    return _pcall(
        body, after=after, name=name, grid=(m // tm, n // tn, nk),
        in_specs=in_specs, out_specs=o_spec,
        out_shape=jax.ShapeDtypeStruct((m, n), out_dtype),
        scratch_shapes=[pltpu.VMEM((tm, tn), F32)] if nk > 1 else [],
        compiler_params=_params("parallel", "parallel", "arbitrary"),
    )(*args)


ROW_TILE = 256


def _rmsnorm_fwd(x, g, name, after=()):
    s, d = x.shape

    def body(x_ref, g_ref, o_ref):
        xf = x_ref[...]
        r = lax.rsqrt(jnp.mean(xf * xf, axis=-1, keepdims=True) + EPS)
        o_ref[...] = (xf * r * g_ref[...]).astype(BF16)

    row = pl.BlockSpec((ROW_TILE, d), lambda i: (i, 0))
    vec = pl.BlockSpec((1, d), lambda i: (0, 0))
    return _pcall(body, after=after, name=name, grid=(s // ROW_TILE,), in_specs=[row, vec], out_specs=row,
                  out_shape=jax.ShapeDtypeStruct((s, d), BF16), compiler_params=_params("parallel"))(x, g)


def _rmsnorm_bwd(x, g, dh, dres, name):
    s, d = x.shape

    def body(x_ref, g_ref, dh_ref, dres_ref, dx_ref, dg_ref):
        xf = x_ref[...]
        r = lax.rsqrt(jnp.mean(xf * xf, axis=-1, keepdims=True) + EPS)
        xhat = xf * r
        dh_v = dh_ref[...]
        dxh = dh_v * g_ref[...]
        proj = jnp.mean(dxh * xhat, axis=-1, keepdims=True)
        dx_ref[...] = dres_ref[...] + r * (dxh - xhat * proj)
        part = jnp.sum(dh_v * xhat, axis=0, keepdims=True)

        @pl.when(pl.program_id(0) == 0)
        def _():
            dg_ref[...] = part

        @pl.when(pl.program_id(0) > 0)
        def _():
            dg_ref[...] += part

    row = pl.BlockSpec((ROW_TILE, d), lambda i: (i, 0))
    vec = pl.BlockSpec((1, d), lambda i: (0, 0))
    return _pcall(body, name=name, grid=(s // ROW_TILE,), in_specs=[row, vec, row, row], out_specs=[row, vec],
                  out_shape=[jax.ShapeDtypeStruct((s, d), F32), jax.ShapeDtypeStruct((1, d), F32)],
                  compiler_params=_params("arbitrary"))(x, g, dh, dres)


def _loss_head(x, g, target, name):
    s, d = x.shape

    def body(x_ref, g_ref, t_ref, dx_ref, dg_ref, loss_ref):
        xf = x_ref[...]
        r = lax.rsqrt(jnp.mean(xf * xf, axis=-1, keepdims=True) + EPS)
        xhat = xf * r
        gv = g_ref[...]
        err = xhat * gv - t_ref[...]
        dy = err * (1.0 / d)
        dxh = dy * gv
        proj = jnp.mean(dxh * xhat, axis=-1, keepdims=True)
        dx_ref[...] = r * (dxh - xhat * proj)
        dg_part = jnp.sum(dy * xhat, axis=0, keepdims=True)
        row_loss = jnp.sum(err * err, axis=-1, keepdims=True) * (0.5 / d)
        loss_part = jnp.broadcast_to(jnp.sum(row_loss, axis=0, keepdims=True), (1, 128))

        @pl.when(pl.program_id(0) == 0)
        def _():
            dg_ref[...] = dg_part
            loss_ref[...] = loss_part

        @pl.when(pl.program_id(0) > 0)
        def _():
            dg_ref[...] += dg_part
            loss_ref[...] += loss_part

    row = pl.BlockSpec((ROW_TILE, d), lambda i: (i, 0))
    vec = pl.BlockSpec((1, d), lambda i: (0, 0))
    one = pl.BlockSpec((1, 128), lambda i: (0, 0))
    return _pcall(body, name=name, grid=(s // ROW_TILE,), in_specs=[row, vec, row], out_specs=[row, vec, one],
                  out_shape=[jax.ShapeDtypeStruct((s, d), F32), jax.ShapeDtypeStruct((1, d), F32),
                             jax.ShapeDtypeStruct((1, 128), F32)],
                  compiler_params=_params("arbitrary"))(x, g, target)


HEADS_PER_STEP = 2
GROUP_W = HEADS_PER_STEP * HEAD
N_GROUPS = N_HEADS // HEADS_PER_STEP


def _head_cols(h):
    return slice(h * HEAD, (h + 1) * HEAD)


def _head_specs(s, col0):
    t = ATT_BLOCK
    g0 = [c // HEADS_PER_STEP for c in col0]
    qspec = pl.BlockSpec((t, GROUP_W), lambda g, i: (i, g0[0] + g))
    kspec = pl.BlockSpec((s, GROUP_W), lambda g, i: (0, g0[1] + g))
    vspec = pl.BlockSpec((s, GROUP_W), lambda g, i: (0, g0[2] + g))
    return qspec, kspec, vspec


def _order_matrix(t, later):
    r, c = lax.broadcasted_iota(jnp.int32, (t, t), 0), lax.broadcasted_iota(jnp.int32, (t, t), 1)
    return (r > c if later else r < c).astype(BF16)


def _exact_dot(x, m):
    hi = x.astype(BF16)
    lo = (x - hi.astype(F32)).astype(BF16)
    return _dot_nn(hi, m) + _dot_nn(lo, m)


def _sb_block(q, kblk, carry_l, u, diagonal):
    t = ATT_BLOCK
    z = _dot_nt(q, kblk) * ATT_SCALE
    sp = jnp.maximum(z, 0.0) + jnp.log(1.0 + jnp.exp(-jnp.abs(z)))
    if not diagonal:
        l = -sp
        return z, None, l, jnp.exp(z + l + _exact_dot(l, u) + carry_l)
    mask = lax.broadcasted_iota(jnp.int32, (t, t), 1) < lax.broadcasted_iota(jnp.int32, (t, t), 0)
    l = jnp.where(mask, -sp, 0.0)
    a = jnp.where(mask, jnp.exp(z - sp + _exact_dot(l, u) + carry_l), 0.0)
    return z, mask, l, a


def _sb_carry_spec(s):
    t = ATT_BLOCK
    return pl.BlockSpec((HEADS_PER_STEP, None, s // t, t, 1), lambda g, i: (g, i, 0, 0, 0))


def _sb_fwd(p, name, after=()):
    s = p.shape[0]
    t = ATT_BLOCK
    nb = s // t

    def body(q_ref, k_ref, v_ref, o_ref, cl_ref):
        i = pl.program_id(1)
        heads = range(HEADS_PER_STEP)
        q = [q_ref[:, _head_cols(h)].astype(BF16) for h in heads]
        u = _order_matrix(t, True)
        cl_ref[...] = jnp.zeros_like(cl_ref)

        def tile(kb, carry, diagonal):
            ks = pl.multiple_of(kb * t, t)
            out = []
            for h in heads:
                acc, carry_l = carry[h]
                kblk = k_ref[pl.ds(ks, t), _head_cols(h)].astype(BF16)
                vblk = v_ref[pl.ds(ks, t), _head_cols(h)].astype(BF16)
                cl_ref[h, kb] = carry_l
                _, _, l, a = _sb_block(q[h], kblk, carry_l, u, diagonal)
                out.append((acc + _dot_nn(a.astype(BF16), vblk), carry_l + jnp.sum(l, axis=1, keepdims=True)))
            return tuple(out)

        carry = tile(i, tuple((jnp.zeros((t, HEAD), F32), jnp.zeros((t, 1), F32)) for _ in heads), True)
        carry = lax.fori_loop(0, i, lambda n, c: tile(i - 1 - n, c, False), carry)
        for h in heads:
            o_ref[:, _head_cols(h)] = carry[h][0]

    qspec, kspec, vspec = _head_specs(s, (0, N_HEADS, 2 * N_HEADS))
    ospec = pl.BlockSpec((t, GROUP_W), lambda g, i: (i, g))
    return _pcall(body, after=after, name=name, grid=(N_GROUPS, nb), in_specs=[qspec, kspec, vspec],
                  out_specs=[ospec, _sb_carry_spec(s)],
                  out_shape=[jax.ShapeDtypeStruct((s, HALF), F32), jax.ShapeDtypeStruct((N_HEADS, nb, nb, t, 1), F32)],
                  compiler_params=_params("parallel", "parallel"))(p, p, p)


def _sb_bwd(p, d_ab, carries, name):
    s = p.shape[0]
    t = ATT_BLOCK

    def body(q_ref, k_ref, v_ref, do_ref, cl_ref, dq_ref, dk_ref, dv_ref):
        i = pl.program_id(1)

        @pl.when(i == 0)
        def _():
            dk_ref[...] = jnp.zeros_like(dk_ref)
            dv_ref[...] = jnp.zeros_like(dv_ref)

        heads = range(HEADS_PER_STEP)
        q = [q_ref[:, _head_cols(h)].astype(BF16) for h in heads]
        do = [do_ref[:, _head_cols(h)].astype(BF16) for h in heads]
        u = _order_matrix(t, True)
        lower = _order_matrix(t, False)

        def tile(kb, carry, diagonal):
            ks = pl.multiple_of(kb * t, t)
            out = []
            for h in heads:
                dq, carry_g = carry[h]
                kblk = k_ref[pl.ds(ks, t), _head_cols(h)].astype(BF16)
                vblk = v_ref[pl.ds(ks, t), _head_cols(h)].astype(BF16)
                z, mask, _, a = _sb_block(q[h], kblk, cl_ref[h, kb], u, diagonal)
                g = a * _dot_nt(do[h], vblk)
                earlier_g = _exact_dot(g, lower) + carry_g
                sig = jax.nn.sigmoid(z)
                dz = g * (1.0 - sig) - sig * earlier_g
                if diagonal:
                    dz = jnp.where(mask, dz, 0.0)
                dz = dz.astype(BF16)
                dv_ref[pl.ds(ks, t), _head_cols(h)] += _dot_tn(a.astype(BF16), do[h])
                dk_ref[pl.ds(ks, t), _head_cols(h)] += _dot_tn(dz, q[h]) * ATT_SCALE
                out.append((dq + _dot_nn(dz, kblk) * ATT_SCALE, carry_g + jnp.sum(g, axis=1, keepdims=True)))
            return tuple(out)

        init = tuple((jnp.zeros((t, HEAD), F32), jnp.zeros((t, 1), F32)) for _ in heads)
        carry = tile(i, lax.fori_loop(0, i, lambda kb, c: tile(kb, c, False), init), True)
        for h in heads:
            dq_ref[:, _head_cols(h)] = carry[h][0]

    qspec, kspec, vspec = _head_specs(s, (0, N_HEADS, 2 * N_HEADS))
    blk = pl.BlockSpec((t, GROUP_W), lambda g, i: (i, g))
    whole = pl.BlockSpec((s, GROUP_W), lambda g, i: (0, g))
    shape = jax.ShapeDtypeStruct((s, HALF), F32)
    return _pcall(body, name=name, grid=(N_GROUPS, s // t), in_specs=[qspec, kspec, vspec, blk, _sb_carry_spec(s)],
                  out_specs=[blk, whole, whole], out_shape=[shape, shape, shape],
                  compiler_params=_params("parallel", "arbitrary"))(p, p, p, d_ab, carries)


COL_TILE = 256


def _sc_fwd(p, w, name):
    s = p.shape[0]
    nb = HALF // COL_TILE

    def body(gb_ref, gc_ref, h_ref, w_ref, o_ref):
        conv = _conv3(gc_ref[...] * h_ref[...], w_ref[...])
        o_ref[...] = (gb_ref[...] * conv).astype(BF16)

    def col(k):
        return pl.BlockSpec((s, COL_TILE), lambda j: (0, k * nb + j))

    wspec = pl.BlockSpec((3, COL_TILE), lambda j: (0, j))
    return _pcall(body, name=name, grid=(nb,), in_specs=[col(3), col(4), col(5), wspec], out_specs=col(0),
                  out_shape=jax.ShapeDtypeStruct((s, HALF), BF16), compiler_params=_params("parallel"))(p, p, p, w)


def _sc_bwd(p, w, d_ab, name):
    s = p.shape[0]
    nb = HALF // COL_TILE

    def body(gb_ref, gc_ref, h_ref, w_ref, d_ref, dgb_ref, dgc_ref, dh_ref, dw_ref):
        gc, hin, wv, d = gc_ref[...], h_ref[...], w_ref[...], d_ref[...]
        sig = gc * hin
        shifted = (_shift_down(sig, 1), _shift_down(sig, 2))
        dgb_ref[...] = d * _conv3(sig, wv, shifted)
        dconv = d * gb_ref[...]
        _conv3_wgrad(dconv, sig, shifted, dw_ref)
        dsig = _conv3_transpose(dconv, wv)
        dgc_ref[...] = dsig * hin
        dh_ref[...] = dsig * gc

    def col(k):
        return pl.BlockSpec((s, COL_TILE), lambda j: (0, k * nb + j))

    wspec = pl.BlockSpec((3, COL_TILE), lambda j: (0, j))
    act = jax.ShapeDtypeStruct((s, HALF), F32)
    return _pcall(body, name=name, grid=(nb,), in_specs=[col(3), col(4), col(5), wspec, col(1)],
                  out_specs=[col(0), col(0), col(0), wspec],
                  out_shape=[act, act, act, jax.ShapeDtypeStruct((3, HALF), F32)],
                  compiler_params=_params("parallel"))(p, p, p, w, d_ab)


def _ffn_act_fwd(u, w, name, after=()):
    s = u.shape[0]
    nb = D_FF // COL_TILE

    def body(ug_ref, uu_ref, wg_ref, wu_ref, o_ref):
        gate = _conv3(ug_ref[...], wg_ref[...])
        up = _conv3(uu_ref[...], wu_ref[...])
        o_ref[...] = (gate * jax.nn.sigmoid(gate) * up).astype(BF16)

    def col(k):
        return pl.BlockSpec((s, COL_TILE), lambda j: (0, k * nb + j))

    def wcol(k):
        return pl.BlockSpec((3, COL_TILE), lambda j: (0, k * nb + j))

    return _pcall(body, after=after, name=name, grid=(nb,), in_specs=[col(0), col(1), wcol(0), wcol(1)], out_specs=col(0),
                  out_shape=jax.ShapeDtypeStruct((s, D_FF), BF16),
                  compiler_params=_params("parallel"))(u, u, w, w)


def _ffn_act_bwd(u, w, d_f, name):
    s = u.shape[0]
    nb = D_FF // COL_TILE

    def body(ug_ref, uu_ref, wg_ref, wu_ref, d_ref, du_ref, dwg_ref, dwu_ref):
        ug, uu, wg, wu, d = ug_ref[...], uu_ref[...], wg_ref[...], wu_ref[...], d_ref[...]
        ug_shifted = (_shift_down(ug, 1), _shift_down(ug, 2))
        uu_shifted = (_shift_down(uu, 1), _shift_down(uu, 2))
        gate = _conv3(ug, wg, ug_shifted)
        up = _conv3(uu, wu, uu_shifted)
        sig = jax.nn.sigmoid(gate)
        d_up = d * gate * sig
        d_gate = d * up * sig * (1.0 + gate * (1.0 - sig))
        _conv3_wgrad(d_gate, ug, ug_shifted, dwg_ref)
        _conv3_wgrad(d_up, uu, uu_shifted, dwu_ref)
        du_ref[0] = _conv3_transpose(d_gate, wg).astype(BF16)
        du_ref[1] = _conv3_transpose(d_up, wu).astype(BF16)

    def col(k):
        return pl.BlockSpec((s, COL_TILE), lambda j: (0, k * nb + j))

    def wcol(k):
        return pl.BlockSpec((3, COL_TILE), lambda j: (0, k * nb + j))

    both = pl.BlockSpec((2, s, COL_TILE), lambda j: (0, 0, j))
    wsh = jax.ShapeDtypeStruct((3, D_FF), F32)
    return _pcall(body, name=name, grid=(nb,), in_specs=[col(0), col(1), wcol(0), wcol(1), col(0)],
                  out_specs=[both, wcol(0), wcol(0)], out_shape=[jax.ShapeDtypeStruct((2, s, D_FF), BF16), wsh, wsh],
                  compiler_params=_params("parallel"))(u, u, w, w, d_f)


def _sg_common(u, v, g, w_ref, bias, mixed_ref):
    rows = u.shape[0]
    gu = _gelu(u)
    gv = _gelu(v)
    xc = gv - jnp.mean(gv, axis=-1, keepdims=True)
    rstd = lax.rsqrt(jnp.mean(xc * xc, axis=-1, keepdims=True) + EPS)
    xhat = xc * rstd
    vn = xhat * g
    tril = lax.broadcasted_iota(jnp.int32, (HEAD, HEAD), 0) >= lax.broadcasted_iota(jnp.int32, (HEAD, HEAD), 1)
    wts = [jnp.where(tril, w_ref[grp], 0.0).astype(BF16) for grp in range(N_HEADS)]
    for n in range(rows // HEAD):
        for grp in range(N_HEADS):
            blk = vn[n * HEAD:(n + 1) * HEAD, grp * HEAD:(grp + 1) * HEAD].astype(BF16)
            mixed_ref[n * HEAD:(n + 1) * HEAD, grp * HEAD:(grp + 1) * HEAD] = _dot_nn(wts[grp], blk)
    mixed = mixed_ref[...] + jnp.concatenate([bias] * (rows // HEAD), axis=0)
    return gu, xhat, rstd, vn, mixed, wts, tril


def _sg_fwd(p, sg_w, bias, g, name):
    s = p.shape[0]

    def body(u_ref, v_ref, w_ref, b_ref, g_ref, o_ref, mixed_ref):
        gu, _, _, _, mixed, _, _ = _sg_common(u_ref[...], v_ref[...], g_ref[...], w_ref, b_ref[...], mixed_ref)
        o_ref[...] = (gu * mixed).astype(BF16)

    def half(k):
        return pl.BlockSpec((ROW_TILE, HALF), lambda i: (i, k))

    wspec = pl.BlockSpec((N_HEADS, HEAD, HEAD), lambda i: (0, 0, 0))
    bspec = pl.BlockSpec((HEAD, HALF), lambda i: (0, 0))
    gspec = pl.BlockSpec((1, HALF), lambda i: (0, 0))
    return _pcall(body, name=name, grid=(s // ROW_TILE,), in_specs=[half(0), half(1), wspec, bspec, gspec],
                  out_specs=half(0), out_shape=jax.ShapeDtypeStruct((s, HALF), BF16),
                  scratch_shapes=[pltpu.VMEM((ROW_TILE, HALF), F32)],
                  compiler_params=_params("parallel"))(p, p, sg_w, bias, g)


def _sg_bwd(p, sg_w, bias, g, d_cd, name):
    s = p.shape[0]
    nsteps = s // ROW_TILE

    def body(u_ref, v_ref, w_ref, b_ref, g_ref, d_ref, du_ref, dv_ref, dw_ref, db_ref, dg_ref,
             mixed_ref, dvn_ref, dbias_ref):
        i = pl.program_id(0)
        u, v, gain, d = u_ref[...], v_ref[...], g_ref[...], d_ref[...]
        gu, xhat, rstd, vn, mixed, wts, tril = _sg_common(u, v, gain, w_ref, b_ref[...], mixed_ref)

        @pl.when(i == 0)
        def _():
            dw_ref[...] = jnp.zeros_like(dw_ref)
            dg_ref[...] = jnp.zeros_like(dg_ref)
            dbias_ref[...] = jnp.zeros_like(dbias_ref)

        du_ref[...] = d * mixed * _gelu_grad(u)
        dm = d * gu
        for n in range(ROW_TILE // HEAD):
            rs = slice(n * HEAD, (n + 1) * HEAD)
            dbias_ref[...] += dm[rs, :]
            for grp in range(N_HEADS):
                cs = slice(grp * HEAD, (grp + 1) * HEAD)
                dm_blk = dm[rs, cs].astype(BF16)
                dw_ref[grp] += jnp.where(tril, _dot_nt(dm_blk, vn[rs, cs].astype(BF16)), 0.0)
                dvn_ref[rs, cs] = _dot_tn(wts[grp], dm_blk)
        dvn = dvn_ref[...]
        dg_ref[...] += jnp.sum(dvn * xhat, axis=0, keepdims=True)
        dxh = dvn * gain
        d_gv = rstd * (dxh - jnp.mean(dxh, axis=-1, keepdims=True) - xhat * jnp.mean(dxh * xhat, axis=-1, keepdims=True))
        dv_ref[...] = d_gv * _gelu_grad(v)

        @pl.when(i == nsteps - 1)
        def _():
            lane = lax.broadcasted_iota(jnp.int32, (HEAD, HEAD), 1)
            out = jnp.zeros((HEAD, HEAD), F32)
            for grp in range(N_HEADS):
                tot = jnp.sum(dbias_ref[:, grp * HEAD:(grp + 1) * HEAD], axis=1, keepdims=True)
                out = out + jnp.where(lane == grp, tot, 0.0)
            db_ref[...] = out

    def half(k):
        return pl.BlockSpec((ROW_TILE, HALF), lambda i: (i, k))

    wspec = pl.BlockSpec((N_HEADS, HEAD, HEAD), lambda i: (0, 0, 0))
    bspec = pl.BlockSpec((HEAD, HALF), lambda i: (0, 0))
    gspec = pl.BlockSpec((1, HALF), lambda i: (0, 0))
    dbspec = pl.BlockSpec((HEAD, HEAD), lambda i: (0, 0))
    act = jax.ShapeDtypeStruct((s, HALF), F32)
    return _pcall(body, name=name, grid=(nsteps,), in_specs=[half(0), half(1), wspec, bspec, gspec, half(0)],
                  out_specs=[half(0), half(0), wspec, dbspec, gspec],
                  out_shape=[act, act, jax.ShapeDtypeStruct((N_HEADS, HEAD, HEAD), F32),
                             jax.ShapeDtypeStruct((HEAD, HEAD), F32), jax.ShapeDtypeStruct((1, HALF), F32)],
                  scratch_shapes=[pltpu.VMEM((ROW_TILE, HALF), F32), pltpu.VMEM((ROW_TILE, HALF), F32),
                                  pltpu.VMEM((HEAD, HALF), F32)],
                  compiler_params=_params("arbitrary"))(p, p, sg_w, bias, g, d_cd)


def _fox_prep(f, b, name):
    s = f.shape[0]
    t = ATT_BLOCK

    def body(f_ref, b_ref, c_ref):
        tri = (lax.broadcasted_iota(jnp.int32, (t, t), 0) >= lax.broadcasted_iota(jnp.int32, (t, t), 1)).astype(BF16)
        carry = jnp.zeros((1, 128), F32)
        for n in range(s // t):
            lf = _log_sigmoid(f_ref[n * t:(n + 1) * t, :] + b_ref[...])
            hi, mid, lo = _split3(lf)
            c_ref[n * t:(n + 1) * t, :] = _dot_nn(tri, hi) + _dot_nn(tri, mid) + _dot_nn(tri, lo) + carry
            carry = carry + jnp.sum(lf, axis=0, keepdims=True)

    return _pcall(body, name=name, in_specs=[VMEM_SPEC, VMEM_SPEC], out_specs=VMEM_SPEC,
                  out_shape=jax.ShapeDtypeStruct((s, 128), F32))(f, b)


def _fox_post(drow, dcol, f, b, name):
    s = f.shape[0]
    t = ATT_BLOCK

    def body(drow_ref, dcol_ref, f_ref, b_ref, df_ref, db_ref):
        tri = (lax.broadcasted_iota(jnp.int32, (t, t), 1) >= lax.broadcasted_iota(jnp.int32, (t, t), 0)).astype(BF16)
        carry = jnp.zeros((1, 128), F32)
        db = jnp.zeros((1, 128), F32)
        for n in reversed(range(s // t)):
            rs = slice(n * t, (n + 1) * t)
            dc = drow_ref[rs, :] - dcol_ref[rs, :]
            hi, mid, lo = _split3(dc)
            dlogf = _dot_nn(tri, hi) + _dot_nn(tri, mid) + _dot_nn(tri, lo) + carry
            carry = carry + jnp.sum(dc, axis=0, keepdims=True)
            df = dlogf * jax.nn.sigmoid(-(f_ref[rs, :] + b_ref[...]))
            df_ref[rs, :] = df
            db = db + jnp.sum(df, axis=0, keepdims=True)
        db_ref[...] = db

    return _pcall(body, name=name, in_specs=[VMEM_SPEC] * 4, out_specs=[VMEM_SPEC, VMEM_SPEC],
                  out_shape=[jax.ShapeDtypeStruct((s, 128), F32), jax.ShapeDtypeStruct((1, 128), F32)])(drow, dcol, f, b)


def _fox_specs(s):
    t = ATT_BLOCK
    ccol = pl.BlockSpec((HEADS_PER_STEP, t, 1), lambda g, i: (g, i, 0))
    crow = pl.BlockSpec((HEADS_PER_STEP, s // t, 1, t), lambda g, i: (g, 0, 0, 0))
    return ccol, crow


def _fox_fwd(p, c_col, c_row, name, after=()):
    s = p.shape[0]
    t = ATT_BLOCK

    def body(q_ref, k_ref, v_ref, cc_ref, cr_ref, o_ref, lse_ref):
        i = pl.program_id(1)
        heads = range(HEADS_PER_STEP)
        q = [q_ref[:, _head_cols(h)].astype(BF16) for h in heads]
        ct = [cc_ref[h] for h in heads]

        def tile(n, carry, diagonal):
            ks = pl.multiple_of(n * t, t)
            out = []
            for h in heads:
                acc, m, l = carry[h]
                kblk = k_ref[pl.ds(ks, t), _head_cols(h)].astype(BF16)
                vblk = v_ref[pl.ds(ks, t), _head_cols(h)].astype(BF16)
                logit = _dot_nt(q[h], kblk) * ATT_SCALE + ct[h] - cr_ref[h, n]
                if diagonal:
                    causal = lax.broadcasted_iota(jnp.int32, (t, t), 1) <= lax.broadcasted_iota(jnp.int32, (t, t), 0)
                    logit = jnp.where(causal, logit, NEG)
                m_new = jnp.maximum(m, jnp.max(logit, axis=1, keepdims=True))
                alpha = jnp.exp(m - m_new)
                pr = jnp.exp(logit - m_new)
                l = alpha * l + jnp.sum(pr, axis=1, keepdims=True)
                out.append((alpha * acc + _dot_nn(pr.astype(BF16), vblk), m_new, l))
            return tuple(out)

        init = tuple((jnp.zeros((t, HEAD), F32), jnp.full((t, 1), NEG, F32), jnp.zeros((t, 1), F32)) for _ in heads)
        carry = tile(i, lax.fori_loop(0, i, lambda n, c: tile(n, c, False), init), True)
        for h in heads:
            acc, m, l = carry[h]
            o_ref[:, _head_cols(h)] = acc / l
            lse_ref[h] = m + jnp.log(l)

    qspec, kspec, vspec = _head_specs(s, (2 * N_HEADS, 3 * N_HEADS, 4 * N_HEADS))
    ccol, crow = _fox_specs(s)
    ospec = pl.BlockSpec((t, GROUP_W), lambda g, i: (i, g))
    return _pcall(body, after=after, name=name, grid=(N_GROUPS, s // t), in_specs=[qspec, kspec, vspec, ccol, crow],
                  out_specs=[ospec, ccol],
                  out_shape=[jax.ShapeDtypeStruct((s, HALF), F32), jax.ShapeDtypeStruct((N_HEADS, s, 1), F32)],
                  compiler_params=_params("parallel", "parallel"))(p, p, p, c_col, c_row)


def _fox_bwd(p, c_col, c_row, lse, d_cd, d_out, name):
    s = p.shape[0]
    t = ATT_BLOCK

    def body(q_ref, k_ref, v_ref, cc_ref, cr_ref, lse_ref, do_ref, o_ref, dq_ref, dk_ref, dv_ref, dcol_ref, drow_ref):
        i = pl.program_id(1)

        @pl.when(i == 0)
        def _():
            dk_ref[...] = jnp.zeros_like(dk_ref)
            dv_ref[...] = jnp.zeros_like(dv_ref)
            dcol_ref[...] = jnp.zeros_like(dcol_ref)

        heads = range(HEADS_PER_STEP)
        q = [q_ref[:, _head_cols(h)].astype(BF16) for h in heads]
        do = [do_ref[:, _head_cols(h)].astype(BF16) for h in heads]
        delta = [jnp.sum(do_ref[:, _head_cols(h)] * o_ref[:, _head_cols(h)], axis=1, keepdims=True) for h in heads]
        ct = [cc_ref[h] for h in heads]
        lse_v = [lse_ref[h] for h in heads]
        ones = jnp.ones((t, HEAD), BF16)

        def tile(n, carry, diagonal):
            ks = pl.multiple_of(n * t, t)
            out = []
            for h in heads:
                dq, drow = carry[h]
                kblk = k_ref[pl.ds(ks, t), _head_cols(h)].astype(BF16)
                vblk = v_ref[pl.ds(ks, t), _head_cols(h)].astype(BF16)
                logit = _dot_nt(q[h], kblk) * ATT_SCALE + ct[h] - cr_ref[h, n]
                pr = jnp.exp(logit - lse_v[h])
                if diagonal:
                    causal = lax.broadcasted_iota(jnp.int32, (t, t), 1) <= lax.broadcasted_iota(jnp.int32, (t, t), 0)
                    pr = jnp.where(causal, pr, 0.0)
                ds = pr * (_dot_nt(do[h], vblk) - delta[h])
                dsb = ds.astype(BF16)
                dv_ref[pl.ds(ks, t), _head_cols(h)] += _dot_tn(pr.astype(BF16), do[h])
                dk_ref[pl.ds(ks, t), _head_cols(h)] += _dot_tn(dsb, q[h]) * ATT_SCALE
                dcol_ref[pl.ds(ks, t), _head_cols(h)] += _dot_tn(dsb, ones)
                out.append((dq + _dot_nn(dsb, kblk) * ATT_SCALE,
                            drow + jnp.sum(dsb.astype(F32), axis=1, keepdims=True)))
            return tuple(out)

        init = tuple((jnp.zeros((t, HEAD), F32), jnp.zeros((t, 1), F32)) for _ in heads)
        carry = tile(i, lax.fori_loop(0, i, lambda n, c: tile(n, c, False), init), True)
        for h in heads:
            dq_ref[:, _head_cols(h)] = carry[h][0]
            drow_ref[h] = carry[h][1]

    qspec, kspec, vspec = _head_specs(s, (2 * N_HEADS, 3 * N_HEADS, 4 * N_HEADS))
    ccol, crow = _fox_specs(s)
    dospec = pl.BlockSpec((t, GROUP_W), lambda g, i: (i, N_GROUPS + g))
    blk = pl.BlockSpec((t, GROUP_W), lambda g, i: (i, g))
    whole = pl.BlockSpec((s, GROUP_W), lambda g, i: (0, g))
    shape = jax.ShapeDtypeStruct((s, HALF), F32)
    return _pcall(body, name=name, grid=(N_GROUPS, s // t),
                  in_specs=[qspec, kspec, vspec, ccol, crow, ccol, dospec, blk],
                  out_specs=[blk, whole, whole, whole, ccol],
                  out_shape=[shape, shape, shape, shape, jax.ShapeDtypeStruct((N_HEADS, s, 1), F32)],
                  compiler_params=_params("parallel", "arbitrary"))(p, p, p, c_col, c_row, lse, d_cd, d_out)


def _row_tile(rows, cap):
    for t in (256, 128, 64, 32, 16, 8):
        if t <= cap and rows % t == 0:
            return t
    return rows


def _adamw(w, g, m, v, name):
    rows, cols = w.shape
    tr = _row_tile(rows, 128)
    c1 = 1.0 / (1.0 - ADAM_B1 ** ADAM_STEP)
    c2 = 1.0 / (1.0 - ADAM_B2 ** ADAM_STEP)

    def body(w_ref, g_ref, m_ref, v_ref, d_ref, nm_ref, nv_ref):
        gv = g_ref[...]
        nm = ADAM_B1 * m_ref[...] + (1.0 - ADAM_B1) * gv
        nv = ADAM_B2 * v_ref[...] + (1.0 - ADAM_B2) * (gv * gv)
        nm_ref[...] = nm
        nv_ref[...] = nv
        d_ref[...] = -ADAM_LR * ((nm * c1) / (jnp.sqrt(nv * c2) + ADAM_EPS) + ADAM_WD * w_ref[...])

    spec = pl.BlockSpec((tr, cols), lambda i: (i, 0))
    shape = jax.ShapeDtypeStruct((rows, cols), F32)
    return _pcall(body, name=name, grid=(rows // tr,), in_specs=[spec] * 4, out_specs=[spec] * 3,
                  out_shape=[shape] * 3, compiler_params=_params("parallel"))(w, g, m, v)


def _half_shape(whole_shape, kind):
    if kind == "col":
        return (whole_shape[0] // 2, whole_shape[1] // 4)
    if kind == "row":
        return (whole_shape[0] // 8, whole_shape[1])
    return (whole_shape[1] // 2, whole_shape[2])


def _own_half_spec(whole_shape, kind, tr):
    hr, hc = _half_shape(whole_shape, kind)
    nb = hr // tr
    if kind == "col":
        return pl.BlockSpec((tr, hc), lambda i, pos: (pos[1] * nb + i, pos[0]))
    if kind == "row":
        return pl.BlockSpec((tr, hc), lambda i, pos: ((2 * pos[0] + pos[1]) * nb + i, 0))
    return pl.BlockSpec((None, tr, hc), lambda i, pos: (pos[0], pos[1] * nb + i, 0))


def _sum_partials(pos, grad, landed, kind, name):
    hr, hc = _half_shape(grad.shape, kind)
    tr = _row_tile(hr, 64)

    def body(pos_ref, g_ref, p_ref, o_ref):
        acc = g_ref[...].astype(F32)
        for k in range(N_DEV - 1):
            acc = acc + p_ref[k].astype(F32)
        o_ref[...] = acc

    grid_spec = pltpu.PrefetchScalarGridSpec(
        num_scalar_prefetch=1, grid=(hr // tr,),
        in_specs=[_own_half_spec(grad.shape, kind, tr), pl.BlockSpec((N_DEV - 1, tr, hc), lambda i, pos: (0, i, 0))],
        out_specs=pl.BlockSpec((tr, hc), lambda i, pos: (i, 0)))
    return _pcall(body, name=name, grid_spec=grid_spec, out_shape=jax.ShapeDtypeStruct((hr, hc), F32),
                  compiler_params=_params("parallel"))(pos, grad, landed)


def _adamw_shard(pos, w, g_mine, g_sibling, m, v, name):
    hr, hc = g_mine.shape
    tr = _row_tile(hr, 128)
    nb = hr // tr
    c1 = 1.0 / (1.0 - ADAM_B1 ** ADAM_STEP)
    c2 = 1.0 / (1.0 - ADAM_B2 ** ADAM_STEP)

    def body(pos_ref, w_ref, gm_ref, gs_ref, m_ref, v_ref, g_ref, d_ref, nm_ref, nv_ref):
        mine = (pl.program_id(0) // nb) == pos_ref[1]
        gv = jnp.where(mine, gm_ref[...], gs_ref[...])
        nm = ADAM_B1 * m_ref[...] + (1.0 - ADAM_B1) * gv
        nv = ADAM_B2 * v_ref[...] + (1.0 - ADAM_B2) * (gv * gv)
        g_ref[...] = gv
        nm_ref[...] = nm
        nv_ref[...] = nv
        d_ref[...] = -ADAM_LR * ((nm * c1) / (jnp.sqrt(nv * c2) + ADAM_EPS) + ADAM_WD * w_ref[...])

    full = pl.BlockSpec((tr, hc), lambda i, pos: (i, 0))
    mine_spec = pl.BlockSpec((tr, hc), lambda i, pos: (jnp.clip(i - pos[1] * nb, 0, nb - 1), 0))
    sib_spec = pl.BlockSpec((tr, hc), lambda i, pos: (jnp.clip(i - (1 - pos[1]) * nb, 0, nb - 1), 0))
    grid_spec = pltpu.PrefetchScalarGridSpec(
        num_scalar_prefetch=1, grid=(2 * nb,), in_specs=[full, mine_spec, sib_spec, full, full], out_specs=[full] * 4)
    shape = jax.ShapeDtypeStruct((2 * hr, hc), F32)
    return _pcall(body, name=name, grid_spec=grid_spec, out_shape=[shape] * 4,
                  compiler_params=_params("parallel"))(pos, w, g_mine, g_sibling, m, v)


def _place_shard(pos, shard, kind, name, after=()):
    rows, cols = shard.shape
    tr = _row_tile(rows, 256)
    nb = rows // tr
    if kind == "col":
        out_spec = pl.BlockSpec((tr, cols), lambda i, pos: (i, pos[0]))
    elif kind == "row":
        out_spec = pl.BlockSpec((tr, cols), lambda i, pos: (pos[0] * nb + i, 0))
    else:
        out_spec = pl.BlockSpec((None, tr, cols), lambda i, pos: (pos[0], i, 0))

    def body(pos_ref, s_ref, *rest):
        rest[-1][...] = s_ref[...].astype(BF16)

    grid_spec = pltpu.PrefetchScalarGridSpec(
        num_scalar_prefetch=1, grid=(nb,),
        in_specs=[pl.BlockSpec((tr, cols), lambda i, pos: (i, 0))] + [pl.BlockSpec(memory_space=pl.ANY)] * len(after),
        out_specs=out_spec)
    return _pcall(body, name=name, grid_spec=grid_spec,
                  out_shape=jax.ShapeDtypeStruct(_whole_shape(shard.shape, kind), BF16),
                  compiler_params=_params("parallel"))(pos, shard, *after)


N_DEV = 8
RELATIONS = [(r >> 2 & 1, r >> 1 & 1, r & 1) for r in range(1, N_DEV)]


def _position():
    return lax.axis_index("x"), lax.axis_index("y"), lax.axis_index("c")


def _related(pos, rel):
    return tuple(1 - p if f else p for p, f in zip(pos, rel))


def _index(pos):
    return 4 * pos[0] + 2 * pos[1] + pos[2]


def _window(ref, kind, pos):
    px, py, pc = pos
    j = 2 * px + py
    if kind == "col":
        r, c = ref.shape
        return ref.at[pl.ds(pc * (r // 2), r // 2), pl.ds(pl.multiple_of(j * (c // 4), 128), c // 4)]
    if kind == "row":
        rj = ref.shape[0] // 4
        return ref.at[pl.ds(j * rj + pc * (rj // 2), rj // 2), :]
    r = ref.shape[1]
    return ref.at[j, pl.ds(pc * (r // 2), r // 2), :]


def _whole_shape(shard_shape, kind):
    r, c = shard_shape
    return {"col": (r, 4 * c), "row": (4 * r, c), "maj": (4, r, c)}[kind]


SEM_SPEC = pl.BlockSpec(memory_space=pltpu.SEMAPHORE)
ANY_SPEC = pl.BlockSpec(memory_space=pl.ANY)
DATAFLOW = pltpu.SideEffectType.DATAFLOW_SIDE_EFFECTING
TOKEN = jax.ShapeDtypeStruct((8, 128), F32)


def _hbm(a):
    return pltpu.with_memory_space_constraint(a, pltpu.HBM)


def _chips(x, y):
    return [(1 - x, y), (x, 1 - y), (1 - x, 1 - y)]


def _split_start(body, name, buffers, n_sems, after=()):
    n = len(buffers)

    def wrapped(*refs):
        body(refs[:n], refs[n], refs[n + 1])
        refs[-1][...] = jnp.zeros_like(refs[-1])

    outs = _pcall(
        wrapped, after=after, name=name, in_specs=[HBM_SPEC] * n,
        out_specs=[SEM_SPEC, SEM_SPEC] + [HBM_SPEC] * n + [VMEM_SPEC],
        out_shape=[pltpu.SemaphoreType.DMA(n_sems), pltpu.SemaphoreType.DMA(n_sems)]
        + [pltpu.HBM(b.shape, b.dtype) for b in buffers] + [TOKEN],
        input_output_aliases={i: 2 + i for i in range(n)},
        compiler_params=pltpu.CompilerParams(has_side_effects=DATAFLOW))(*[_hbm(b) for b in buffers])
    return outs[0], outs[1], list(outs[2:2 + n]), outs[2 + n]


def _split_wait(body, name, buffers, send_sems, recv_sems, after):
    n = len(buffers)

    def wrapped(*refs):
        body(refs[:n], refs[n], refs[n + 1])

    outs = _pcall(
        wrapped, name=name, in_specs=[HBM_SPEC] * n + [SEM_SPEC, SEM_SPEC, ANY_SPEC],
        out_specs=[HBM_SPEC] * n, out_shape=[pltpu.HBM(b.shape, b.dtype) for b in buffers],
        input_output_aliases={i: i for i in range(n)},
        compiler_params=pltpu.CompilerParams(has_side_effects=DATAFLOW))(*buffers, send_sems, recv_sems, after)
    return list(outs)


def _gather_start(wholes, kinds, name, after=()):
    def body(w_refs, send_sems, recv_sems):
        x, y, c = _position()
        for w, ref in enumerate(w_refs):
            mine = _window(ref, kinds[w], (x, y, c))
            for k, chip in enumerate(_chips(x, y)):
                pltpu.make_async_remote_copy(src_ref=mine, dst_ref=mine, send_sem=send_sems.at[3 * w + k],
                                             recv_sem=recv_sems.at[3 * w + k], device_id=(*chip, c),
                                             device_id_type=MESH).start()

    return _split_start(body, name, wholes, (3 * len(wholes),), after)


def _gather_forward(wholes, kinds, send1, recv1, after, name):
    n = len(wholes)

    def wrapped(*refs):
        w_refs, s1, r1, s2, r2 = refs[:n], refs[n], refs[n + 1], refs[n + 3], refs[n + 4]
        x, y, c = _position()
        for k, chip in enumerate(_chips(x, y)):
            for w, ref in enumerate(w_refs):
                theirs = _window(ref, kinds[w], (*chip, c))
                pltpu.make_async_remote_copy(src_ref=theirs, dst_ref=theirs, send_sem=s1.at[3 * w + k],
                                             recv_sem=r1.at[3 * w + k], device_id=(*chip, c),
                                             device_id_type=MESH).wait_recv()
                pltpu.make_async_remote_copy(src_ref=theirs, dst_ref=theirs, send_sem=s2.at[3 * w + k],
                                             recv_sem=r2.at[3 * w + k], device_id=(x, y, 1 - c),
                                             device_id_type=MESH).start()
        for w, ref in enumerate(w_refs):
            mine = _window(ref, kinds[w], (x, y, c))
            for k, chip in enumerate(_chips(x, y)):
                pltpu.make_async_remote_copy(src_ref=mine, dst_ref=mine, send_sem=s1.at[3 * w + k],
                                             recv_sem=r1.at[3 * w + k], device_id=(*chip, c),
                                             device_id_type=MESH).wait_send()
        refs[-1][...] = jnp.zeros_like(refs[-1])

    outs = _pcall(
        wrapped, name=name, in_specs=[HBM_SPEC] * n + [SEM_SPEC, SEM_SPEC, ANY_SPEC],
        out_specs=[SEM_SPEC, SEM_SPEC] + [HBM_SPEC] * n + [VMEM_SPEC],
        out_shape=[pltpu.SemaphoreType.DMA((3 * n,)), pltpu.SemaphoreType.DMA((3 * n,))]
        + [pltpu.HBM(b.shape, b.dtype) for b in wholes] + [TOKEN],
        input_output_aliases={i: 2 + i for i in range(n)},
        compiler_params=pltpu.CompilerParams(has_side_effects=DATAFLOW))(*wholes, send1, recv1, after)
    return outs[0], outs[1], list(outs[2:2 + n]), outs[2 + n]


def _gather_finish(wholes, kinds, send2, recv2, after, name):
    def body(w_refs, s2, r2):
        x, y, c = _position()
        for k, chip in enumerate(_chips(x, y)):
            for w, ref in enumerate(w_refs):
                sent = _window(ref, kinds[w], (*chip, c))
                got = _window(ref, kinds[w], (*chip, 1 - c))
                pltpu.make_async_remote_copy(src_ref=sent, dst_ref=got, send_sem=s2.at[3 * w + k],
                                             recv_sem=r2.at[3 * w + k], device_id=(x, y, 1 - c),
                                             device_id_type=MESH).wait()

    return _split_wait(body, name, wholes, send2, recv2, after)


def _gather_small(small, after=()):
    def body(s_ref, o_ref, send_sems, recv_sems, local_sem):
        x, y, c = _position()
        mine = pltpu.make_async_copy(s_ref, o_ref.at[2 * x + y], local_sem)
        mine.start()
        sends = []
        for k, chip in enumerate(_chips(x, y)):
            cp = pltpu.make_async_remote_copy(src_ref=s_ref, dst_ref=o_ref.at[2 * x + y], send_sem=send_sems.at[k],
                                              recv_sem=recv_sems.at[k], device_id=(*chip, c), device_id_type=MESH)
            cp.start()
            sends.append(cp)
        for k, chip in enumerate(_chips(x, y)):
            pltpu.make_async_remote_copy(src_ref=s_ref, dst_ref=o_ref.at[2 * chip[0] + chip[1]], send_sem=send_sems.at[k],
                                         recv_sem=recv_sems.at[k], device_id=(*chip, c), device_id_type=MESH).wait_recv()
        for cp in sends:
            cp.wait_send()
        mine.wait()

    return _pcall(body, after=after, name="gather_small", in_specs=[HBM_SPEC], out_specs=HBM_SPEC,
                  out_shape=jax.ShapeDtypeStruct((4,) + small.shape, small.dtype),
                  scratch_shapes=[pltpu.SemaphoreType.DMA((3,)), pltpu.SemaphoreType.DMA((3,)),
                                  pltpu.SemaphoreType.DMA(())])(small)


def _scatter_copies(g_refs, land_refs, kinds, send_sems, recv_sems):
    me = _position()
    copies = []
    for k, rel in enumerate(RELATIONS):
        peer = _related(me, rel)
        for w, (g_ref, land_ref) in enumerate(zip(g_refs, land_refs)):
            copies.append(pltpu.make_async_remote_copy(
                src_ref=_window(g_ref, kinds[w], peer), dst_ref=land_ref.at[k],
                send_sem=send_sems.at[7 * w + k], recv_sem=recv_sems.at[7 * w + k], device_id=peer,
                device_id_type=MESH))
    return copies


def _scatter_start(grads, kinds, name):
    n = len(grads)
    lands = [lax.empty((N_DEV - 1,) + _half_shape(g.shape, kd), g.dtype) for g, kd in zip(grads, kinds)]

    def body(refs, send_sems, recv_sems):
        for cp in _scatter_copies(refs[:n], refs[n:], kinds, send_sems, recv_sems):
            cp.start()

    send, recv, thru, token = _split_start(body, name, list(grads) + lands, ((N_DEV - 1) * n,))
    return send, recv, thru[:n], thru[n:], token


def _scatter_wait(grads, lands, kinds, send, recv, after, name):
    n = len(grads)

    def body(refs, send_sems, recv_sems):
        for cp in _scatter_copies(refs[:n], refs[n:], kinds, send_sems, recv_sems):
            cp.wait()

    out = _split_wait(body, name, list(grads) + list(lands), send, recv, after)
    return out[:n], out[n:]


def _swap_start(halves, name):
    n = len(halves)
    lands = [lax.empty(h.shape, h.dtype) for h in halves]

    def body(refs, send_sems, recv_sems):
        x, y, c = _position()
        for w in range(n):
            pltpu.make_async_remote_copy(src_ref=refs[w], dst_ref=refs[n + w], send_sem=send_sems.at[w],
                                         recv_sem=recv_sems.at[w], device_id=(x, y, 1 - c), device_id_type=MESH).start()

    send, recv, thru, token = _split_start(body, name, list(halves) + lands, (n,))
    return send, recv, thru[:n], thru[n:], token


def _swap_wait(halves, lands, send, recv, after, name):
    n = len(halves)

    def body(refs, send_sems, recv_sems):
        x, y, c = _position()
        for w in range(n):
            pltpu.make_async_remote_copy(src_ref=refs[w], dst_ref=refs[n + w], send_sem=send_sems.at[w],
                                         recv_sem=recv_sems.at[w], device_id=(x, y, 1 - c), device_id_type=MESH).wait()

    out = _split_wait(body, name, list(halves) + list(lands), send, recv, after)
    return out[:n], out[n:]


def _allreduce_small(v, after=()):
    rows = v.shape[0]

    def body(v_ref, o_ref, recv_ref, send_sems, recv_sems):
        me = _position()
        recv_ref[_index(me)] = v_ref[...]
        sends = []
        for k, rel in enumerate(RELATIONS):
            peer = _related(me, rel)
            cp = pltpu.make_async_remote_copy(
                src_ref=v_ref, dst_ref=recv_ref.at[_index(me)],
                send_sem=send_sems.at[k], recv_sem=recv_sems.at[k], device_id=peer, device_id_type=MESH)
            cp.start()
            sends.append(cp)
        for k, rel in enumerate(RELATIONS):
            peer = _related(me, rel)
            pltpu.make_async_remote_copy(
                src_ref=v_ref, dst_ref=recv_ref.at[_index(peer)],
                send_sem=send_sems.at[k], recv_sem=recv_sems.at[k], device_id=peer, device_id_type=MESH).wait_recv()
        for cp in sends:
            cp.wait_send()
        acc = recv_ref[0]
        for k in range(1, N_DEV):
            acc = acc + recv_ref[k]
        o_ref[...] = acc

    return _pcall(body, after=after, name="allreduce_small", in_specs=[VMEM_SPEC], out_specs=VMEM_SPEC,
                  out_shape=jax.ShapeDtypeStruct((rows, 128), F32),
                  scratch_shapes=[pltpu.VMEM((N_DEV, rows, 128), F32), pltpu.SemaphoreType.DMA((7,)),
                                  pltpu.SemaphoreType.DMA((7,))],
                  compiler_params=pltpu.CompilerParams(vmem_limit_bytes=VMEM_LIMIT_BYTES))(v)


def _pack(arrays):
    flat = []
    for a in arrays:
        a = a.reshape(-1)
        flat.append(jnp.pad(a, (0, -a.shape[0] % 128)))
    flat = jnp.concatenate(flat)
    flat = jnp.pad(flat, (0, -flat.shape[0] % 1024))
    return flat.reshape(-1, 128)


def _unpack(packed, shapes):
    flat = packed.reshape(-1)
    out, at = [], 0
    for shp in shapes:
        size = 1
        for d in shp:
            size *= d
        out.append(flat[at:at + size].reshape(shp))
        at += size + (-size % 128)
    return out


WEIGHTS = ['l0_mix_norm_g', 'l0_w_in', 'l0_sc_conv_w', 'l0_w_out', 'l0_ffn_norm_g', 'l0_ffn_up', 'l0_ffn_conv_w',
           'l0_ffn_down', 'l1_mix_norm_g', 'l1_w_in', 'l1_fox_b_f', 'l1_sg_w', 'l1_sg_b', 'l1_sg_norm_g', 'l1_w_out',
           'l1_ffn_norm_g', 'l1_ffn_up', 'l1_ffn_conv_w', 'l1_ffn_down', 'final_norm_g']
BIG = {'l0_w_in': 'col', 'l0_w_out': 'row', 'l0_ffn_up': 'col', 'l0_ffn_down': 'row',
       'l1_w_in': 'maj', 'l1_w_out': 'row', 'l1_ffn_up': 'col', 'l1_ffn_down': 'row'}
GATHER_GROUPS = [['l0_w_in'], ['l0_w_out'], ['l0_ffn_up'], ['l0_ffn_down'], ['l1_w_in', 'l1_w_out'],
                 ['l1_ffn_up'], ['l1_ffn_down']]
CONV = ['l0_sc_conv_w', 'l0_ffn_conv_w', 'l1_ffn_conv_w']
SMALL = [n for n in WEIGHTS if n not in BIG]
IN_CD = 5 * HALF + N_HEADS


def _ffn_forward(x, g, get_up, behind_act, get_down, conv_w, tag):
    h = _rmsnorm_fwd(x, g, tag + "_norm")
    u = _matmul(h, get_up(h), "nn", F32, tag + "_up")
    f = _ffn_act_fwd(u, conv_w, tag + "_act", after=behind_act(u))
    w_down, tokens = get_down(f)
    return _matmul(f, w_down, "nn", F32, tag + "_down", res=x, after=tokens), (h, u, f)


def _ffn_backward(x, g, w_up, conv_w, w_down, saved, d_out, send_grads, tag):
    h, u, f = saved
    dw_down = _matmul(f, d_out, "tn", BF16, tag + "_dwdown")
    d_f = _matmul(d_out, w_down, "nt", F32, tag + "_df")
    du, dcw_gate, dcw_up = _ffn_act_bwd(u, conv_w, d_f, tag + "_dact")
    dw_up = _matmul(h, du, "tn", BF16, tag + "_dwup")
    dh = _matmul(du, w_up, "nt", F32, tag + "_dh", after=[send_grads(dw_up, dw_down)])
    dx, dg = _rmsnorm_bwd(x, g, dh, d_out, tag + "_dnorm")
    return dx, dg, jnp.concatenate([dcw_gate, dcw_up], axis=1)


def kernel(x, l0_mix_norm_g, l0_w_in, l0_sc_conv_w, l0_w_out, l0_ffn_norm_g, l0_ffn_up, l0_ffn_conv_w, l0_ffn_down, l1_mix_norm_g, l1_w_in, l1_fox_b_f, l1_sg_w, l1_sg_b, l1_sg_norm_g, l1_w_out, l1_ffn_norm_g, l1_ffn_up, l1_ffn_conv_w, l1_ffn_down, final_norm_g, loss_target, m_l0_mix_norm_g, m_l0_w_in, m_l0_sc_conv_w, m_l0_w_out, m_l0_ffn_norm_g, m_l0_ffn_up, m_l0_ffn_conv_w, m_l0_ffn_down, m_l1_mix_norm_g, m_l1_w_in, m_l1_fox_b_f, m_l1_sg_w, m_l1_sg_b, m_l1_sg_norm_g, m_l1_w_out, m_l1_ffn_norm_g, m_l1_ffn_up, m_l1_ffn_conv_w, m_l1_ffn_down, m_final_norm_g, v_l0_mix_norm_g, v_l0_w_in, v_l0_sc_conv_w, v_l0_w_out, v_l0_ffn_norm_g, v_l0_ffn_up, v_l0_ffn_conv_w, v_l0_ffn_down, v_l1_mix_norm_g, v_l1_w_in, v_l1_fox_b_f, v_l1_sg_w, v_l1_sg_b, v_l1_sg_norm_g, v_l1_w_out, v_l1_ffn_norm_g, v_l1_ffn_up, v_l1_ffn_conv_w, v_l1_ffn_down, v_final_norm_g):
    given = (l0_mix_norm_g, l0_w_in, l0_sc_conv_w, l0_w_out, l0_ffn_norm_g, l0_ffn_up, l0_ffn_conv_w, l0_ffn_down, l1_mix_norm_g, l1_w_in, l1_fox_b_f, l1_sg_w, l1_sg_b, l1_sg_norm_g, l1_w_out, l1_ffn_norm_g, l1_ffn_up, l1_ffn_conv_w, l1_ffn_down, final_norm_g)
    given_m = (m_l0_mix_norm_g, m_l0_w_in, m_l0_sc_conv_w, m_l0_w_out, m_l0_ffn_norm_g, m_l0_ffn_up, m_l0_ffn_conv_w, m_l0_ffn_down, m_l1_mix_norm_g, m_l1_w_in, m_l1_fox_b_f, m_l1_sg_w, m_l1_sg_b, m_l1_sg_norm_g, m_l1_w_out, m_l1_ffn_norm_g, m_l1_ffn_up, m_l1_ffn_conv_w, m_l1_ffn_down, m_final_norm_g)
    given_v = (v_l0_mix_norm_g, v_l0_w_in, v_l0_sc_conv_w, v_l0_w_out, v_l0_ffn_norm_g, v_l0_ffn_up, v_l0_ffn_conv_w, v_l0_ffn_down, v_l1_mix_norm_g, v_l1_w_in, v_l1_fox_b_f, v_l1_sg_w, v_l1_sg_b, v_l1_sg_norm_g, v_l1_w_out, v_l1_ffn_norm_g, v_l1_ffn_up, v_l1_ffn_conv_w, v_l1_ffn_down, v_final_norm_g)
    wt = dict(zip(WEIGHTS, given))
    mom = dict(zip(WEIGHTS, given_m))
    var = dict(zip(WEIGHTS, given_v))
    s = x.shape[1]
    t = ATT_BLOCK
    x0, target = x[0], loss_target[0]
    chip = 2 * lax.axis_index("x") + lax.axis_index("y")

    pos = jnp.stack([chip, lax.axis_index("c")]).astype(jnp.int32)
    big_names = list(BIG)

    conv_widths = [wt[n].shape[1] for n in CONV]
    conv_all = _gather_small(jnp.concatenate([wt[n] for n in CONV], axis=1))
    conv_full, at = {}, 0
    for n, cw in zip(CONV, conv_widths):
        conv_full[n] = jnp.transpose(conv_all[:, :, at:at + cw], (1, 0, 2)).reshape(3, 4 * cw)
        at += cw
    gathers, token = [], conv_all
    for gi, names in enumerate(GATHER_GROUPS):
        placed = [_place_shard(pos, wt[n], BIG[n], "place_" + n, [token]) for n in names]
        send, recv, thru, token = _gather_start(placed, [BIG[n] for n in names], "gather_start_%d" % gi, [token])
        gathers.append((send, recv, thru))
    full = {}

    def forward_gather(gi, after):
        send, recv, thru = gathers[gi]
        kinds = [BIG[n] for n in GATHER_GROUPS[gi]]
        gathers[gi] = _gather_forward(thru, kinds, send, recv, after, "gather_forward_%d" % gi)
        return gathers[gi][3]

    def finish_gather(gi, after):
        send, recv, thru, tok = gathers[gi]
        names = GATHER_GROUPS[gi]
        wholes = _gather_finish(thru, [BIG[n] for n in names], send, recv, tok if after is None else after,
                                "gather_finish_%d" % gi)
        full.update(zip(names, wholes))

    def vec(name):
        return wt[name].reshape(1, -1)

    h0 = _rmsnorm_fwd(x0, vec('l0_mix_norm_g'), "l0_mix_norm", after=[token])
    forward_gather(0, h0)
    finish_gather(0, None)
    p0 = _matmul(h0, full['l0_w_in'], "nn", F32, "l0_in")
    a_out, sb_carries = _sb_fwd(p0, "l0_sb", after=[forward_gather(1, p0)])
    finish_gather(1, a_out)
    b_out = _sc_fwd(p0, conv_full['l0_sc_conv_w'], "l0_sc")
    ab0 = jnp.concatenate([a_out.astype(BF16), b_out], axis=1)
    x1 = _matmul(ab0, full['l0_w_out'], "nn", F32, "l0_out", res=x0, after=[forward_gather(2, b_out)])

    def ffn_weights(up_group, next_group):
        def get_up(h):
            finish_gather(up_group, h)
            return full[GATHER_GROUPS[up_group][0]]

        def behind_act(u):
            return [forward_gather(up_group + 1, u)]

        def get_down(f):
            finish_gather(up_group + 1, f)
            return full[GATHER_GROUPS[up_group + 1][0]], ([forward_gather(next_group, f)] if next_group else ())

        return get_up, behind_act, get_down

    x2, ffn0_saved = _ffn_forward(x1, vec('l0_ffn_norm_g'), *ffn_weights(2, 4), conv_full['l0_ffn_conv_w'], "l0_ffn")
    h2 = _rmsnorm_fwd(x2, vec('l1_mix_norm_g'), "l1_mix_norm")
    finish_gather(4, h2)
    w_in1 = jnp.transpose(full['l1_w_in'], (1, 0, 2)).reshape(D_MODEL, IN_CD)
    w_in1_main = w_in1[:, :5 * HALF]
    w_in1_f = jnp.pad(w_in1[:, 5 * HALF:], ((0, 0), (0, 128 - N_HEADS)))
    p1 = _matmul(h2, w_in1_main, "nn", F32, "l1_in")
    f_logit = _matmul(h2, w_in1_f, "nn", F32, "l1_in_f")
    b_f = jnp.pad(wt['l1_fox_b_f'], (0, 128 - N_HEADS)).reshape(1, 128)
    c_heads = _fox_prep(f_logit, b_f, "l1_fox_prep")[:, :N_HEADS].T
    c_col = c_heads[:, :, None]
    c_row = c_heads.reshape(N_HEADS, s // t, 1, t)
    sg_bias = jnp.repeat(wt['l1_sg_b'].T, HEAD, axis=1)
    sg_gain = vec('l1_sg_norm_g')
    c_out = _sg_fwd(p1, wt['l1_sg_w'], sg_bias, sg_gain, "l1_sg")
    d_out, lse = _fox_fwd(p1, c_col, c_row, "l1_fox", after=[forward_gather(5, c_out)])
    cd1 = jnp.concatenate([c_out, d_out.astype(BF16)], axis=1)
    x3 = _matmul(cd1, full['l1_w_out'], "nn", F32, "l1_out", res=x2)
    x4, ffn1_saved = _ffn_forward(x3, vec('l1_ffn_norm_g'), *ffn_weights(5, None), conv_full['l1_ffn_conv_w'], "l1_ffn")
    dx4, dg_final, loss_part = _loss_head(x4, vec('final_norm_g'), target, "loss_head")

    grads = {'final_norm_g': dg_final}
    scatters = []

    def send_grads(names):
        def start(*group):
            send, recv, thru, lands, tok = _scatter_start(list(group), [BIG[n] for n in names],
                                                          "scatter_start_%d" % len(scatters))
            scatters.append((names, send, recv, thru, lands))
            return tok
        return start

    dx3, grads['l1_ffn_norm_g'], grads['l1_ffn_conv_w'] = _ffn_backward(
        x3, vec('l1_ffn_norm_g'), full['l1_ffn_up'], conv_full['l1_ffn_conv_w'], full['l1_ffn_down'], ffn1_saved, dx4,
        send_grads(['l1_ffn_up', 'l1_ffn_down']), "l1_ffn")
    dw_out1 = _matmul(cd1, dx3, "tn", BF16, "l1_dwout")
    d_cd = _matmul(dx3, full['l1_w_out'], "nt", F32, "l1_dcd")
    du, dv, grads['l1_sg_w'], db_sg, grads['l1_sg_norm_g'] = _sg_bwd(p1, wt['l1_sg_w'], sg_bias, sg_gain, d_cd, "l1_dsg")
    grads['l1_sg_b'] = db_sg[:, :N_HEADS].T
    dq, dk, dvv, dcol, drow = _fox_bwd(p1, c_col, c_row, lse, d_cd, d_out, "l1_dfox")
    pad8 = ((0, 0), (0, 128 - N_HEADS))
    d_f_logit, d_b_f = _fox_post(jnp.pad(drow[:, :, 0].T, pad8), jnp.pad(dcol[:, ::HEAD], pad8), f_logit, b_f,
                                 "l1_fox_post")
    grads['l1_fox_b_f'] = d_b_f[0, :N_HEADS]
    dp1 = jnp.concatenate([a.astype(BF16) for a in (du, dv, dq, dk, dvv)], axis=1)
    dw_main = _matmul(h2, dp1, "tn", BF16, "l1_dwin")
    dw_f = _matmul(h2, d_f_logit, "tn", BF16, "l1_dwin_f")
    dw_in1 = jnp.concatenate([dw_main, dw_f[:, :N_HEADS]], axis=1)
    dw_in1 = jnp.transpose(dw_in1.reshape(D_MODEL, 4, IN_CD // 4), (1, 0, 2))
    dh2 = _matmul(dp1, w_in1_main, "nt", F32, "l1_dh", after=[send_grads(['l1_w_out', 'l1_w_in'])(dw_out1, dw_in1)])
    dh2 = _matmul(d_f_logit, w_in1_f, "nt", F32, "l1_dh_f", res=dh2)
    dx2, grads['l1_mix_norm_g'] = _rmsnorm_bwd(x2, vec('l1_mix_norm_g'), dh2, dx3, "l1_dmix_norm")
    dx1, grads['l0_ffn_norm_g'], grads['l0_ffn_conv_w'] = _ffn_backward(
        x1, vec('l0_ffn_norm_g'), full['l0_ffn_up'], conv_full['l0_ffn_conv_w'], full['l0_ffn_down'], ffn0_saved, dx2,
        send_grads(['l0_ffn_up', 'l0_ffn_down']), "l0_ffn")
    dw_out0 = _matmul(ab0, dx1, "tn", BF16, "l0_dwout")
    d_ab = _matmul(dx1, full['l0_w_out'], "nt", F32, "l0_dab", after=[send_grads(['l0_w_out'])(dw_out0)])
    dq0, dk0, dv0 = _sb_bwd(p0, d_ab, sb_carries, "l0_dsb")
    dgb, dgc, dhin, grads['l0_sc_conv_w'] = _sc_bwd(p0, conv_full['l0_sc_conv_w'], d_ab, "l0_dsc")
    dp0 = jnp.concatenate([a.astype(BF16) for a in (dq0, dk0, dv0, dgb, dgc, dhin)], axis=1)
    dw_in0 = _matmul(h0, dp0, "tn", BF16, "l0_dwin")
    dh0 = _matmul(dp0, full['l0_w_in'], "nt", F32, "l0_dh", after=[send_grads(['l0_w_in'])(dw_in0)])
    dx0, grads['l0_mix_norm_g'] = _rmsnorm_bwd(x0, vec('l0_mix_norm_g'), dh0, dx1, "l0_dmix_norm")

    shard_grads, delta, new_m, new_v, swaps = {}, {}, {}, {}, {}

    def reduce_group(gi, after):
        names, send, recv, thru, lands = scatters[gi]
        kinds = [BIG[n] for n in names]
        g_thru, landed = _scatter_wait(thru, lands, kinds, send, recv, after, "scatter_wait_%d" % gi)
        halves = [_sum_partials(pos, g, ld, kd, "sum_" + n) for n, g, ld, kd in zip(names, g_thru, landed, kinds)]
        s_send, s_recv, h_thru, s_lands, tok = _swap_start(halves, "swap_start_%d" % gi)
        swaps[gi] = (names, s_send, s_recv, h_thru, s_lands)
        return tok

    def update_group(gi, after):
        names, s_send, s_recv, h_thru, s_lands = swaps[gi]
        mine, theirs = _swap_wait(h_thru, s_lands, s_send, s_recv, after, "swap_wait_%d" % gi)
        for n, gm, gs in zip(names, mine, theirs):
            shard_grads[n], delta[n], new_m[n], new_v[n] = _adamw_shard(pos, wt[n], gm, gs, mom[n], var[n], "adamw_" + n)
        return delta[names[-1]]

    after = reduce_group(1, reduce_group(0, dx0))
    after = update_group(1, update_group(0, after))
    after = update_group(2, reduce_group(2, after))
    after = update_group(3, reduce_group(3, after))
    after = reduce_group(4, after)
    small_shapes = [conv_full[n].shape if n in CONV else wt[n].shape for n in SMALL] + [loss_part.shape]
    small_all = _allreduce_small(_pack([grads[n] for n in SMALL] + [loss_part]), [after])
    small_sums = _unpack(small_all, small_shapes)
    loss = small_sums[-1][0, 0]
    for n, g in zip(SMALL, small_sums):
        shard_grads[n] = lax.dynamic_slice_in_dim(g, chip * wt[n].shape[1], wt[n].shape[1], axis=1) if n in CONV else g
    update_group(4, small_all)
    shapes = [wt[n].shape for n in SMALL]
    packed = _adamw(_pack([wt[n] for n in SMALL]), _pack([shard_grads[n] for n in SMALL]),
                    _pack([mom[n] for n in SMALL]), _pack([var[n] for n in SMALL]), "adamw_small")
    for out, pk in zip((delta, new_m, new_v), packed):
        out.update(zip(SMALL, _unpack(pk, shapes)))

    return (loss, dx0[None], *[shard_grads[n] for n in WEIGHTS], *[delta[n] for n in WEIGHTS],
            *[new_m[n] for n in WEIGHTS], *[new_v[n] for n in WEIGHTS])
```

```python
import functools

import jax
import jax.numpy as jnp
from jax import lax
from jax.experimental import pallas as pl
from jax.experimental.pallas import tpu as pltpu

F32 = jnp.float32
BF16 = jnp.bfloat16

D_MODEL = 2048
HEAD = 128
N_HEADS = 8
HALF = N_HEADS * HEAD
D_FF = 5632
EPS = 1e-6
ATT_SCALE = HEAD ** -0.5
ATT_BLOCK = 512
NEG = -1e30

ADAM_LR = 0.001
ADAM_B1 = 0.9
ADAM_B2 = 0.999
ADAM_EPS = 1e-08
ADAM_WD = 0.01
ADAM_STEP = 10

VMEM_LIMIT_BYTES = 48 * 1024 * 1024
MESH = pl.DeviceIdType.MESH
HBM_SPEC = pl.BlockSpec(memory_space=pltpu.HBM)
VMEM_SPEC = pl.BlockSpec(memory_space=pltpu.VMEM)


def _pcall(body, after=(), **kw):
    if not after:
        return pl.pallas_call(body, **kw)
    n_in, n_after, inner = len(kw["in_specs"]), len(after), body
    kw["in_specs"] = list(kw["in_specs"]) + [pl.BlockSpec(memory_space=pl.ANY)] * n_after

    def body(*refs):
        inner(*refs[:n_in], *refs[n_in + n_after:])

    call = pl.pallas_call(body, **kw)
    return lambda *args: call(*args, *after)


def _params(*semantics):
    return pltpu.CompilerParams(dimension_semantics=semantics, vmem_limit_bytes=VMEM_LIMIT_BYTES)


def _pick(n, cap):
    best = None
    for t in range(128, min(n, cap) + 1, 128):
        if n % t == 0:
            best = t
    return n if best is None else best


def _dot(a, b, dims):
    return lax.dot_general(a, b, (dims, ((), ())), preferred_element_type=F32)


def _dot_nn(a, b):
    return _dot(a, b, ((1,), (0,)))


def _dot_nt(a, b):
    return _dot(a, b, ((1,), (1,)))


def _dot_tn(a, b):
    return _dot(a, b, ((0,), (0,)))


def _split3(x):
    hi = x.astype(BF16)
    r1 = x - hi.astype(F32)
    mid = r1.astype(BF16)
    lo = (r1 - mid.astype(F32)).astype(BF16)
    return hi, mid, lo


def _softplus(z):
    return jnp.maximum(z, 0.0) + jnp.log1p(jnp.exp(-jnp.abs(z)))


def _log_sigmoid(z):
    return jnp.minimum(z, 0.0) - jnp.log1p(jnp.exp(-jnp.abs(z)))


_GELU_K = 0.7978845608028654


def _gelu(x):
    return 0.5 * x * (1.0 + jnp.tanh(_GELU_K * (x + 0.044715 * x * x * x)))


def _gelu_grad(x):
    t = jnp.tanh(_GELU_K * (x + 0.044715 * x * x * x))
    return 0.5 * (1.0 + t) + 0.5 * x * (1.0 - t * t) * _GELU_K * (1.0 + 3.0 * 0.044715 * x * x)


SUBLANES = 8


def _shift_down(x, k):
    rolled = pltpu.roll(x, k, axis=0)
    head = rolled[:SUBLANES]
    head = jnp.where(lax.broadcasted_iota(jnp.int32, head.shape, 0) >= k, head, 0.0)
    return jnp.concatenate([head, rolled[SUBLANES:]], axis=0)


def _shift_up(x, k):
    n = x.shape[0]
    rolled = pltpu.roll(x, n - k, axis=0)
    tail = rolled[n - SUBLANES:]
    tail = jnp.where(lax.broadcasted_iota(jnp.int32, tail.shape, 0) < SUBLANES - k, tail, 0.0)
    return jnp.concatenate([rolled[:n - SUBLANES], tail], axis=0)


def _conv3(s, w, shifted=None):
    s1, s2 = shifted if shifted else (_shift_down(s, 1), _shift_down(s, 2))
    return w[0:1, :] * s2 + w[1:2, :] * s1 + w[2:3, :] * s


def _conv3_transpose(d, w):
    return w[2:3, :] * d + w[1:2, :] * _shift_up(d, 1) + w[0:1, :] * _shift_up(d, 2)


def _conv3_wgrad(d, s, shifted, dw_ref):
    s1, s2 = shifted
    dw_ref[0:1, :] = jnp.sum(d * s2, axis=0, keepdims=True)
    dw_ref[1:2, :] = jnp.sum(d * s1, axis=0, keepdims=True)
    dw_ref[2:3, :] = jnp.sum(d * s, axis=0, keepdims=True)


MM_TILE_M, MM_TILE_N, MM_TILE_K = 1408, 512, 2816


def _matmul(a, b, mode, out_dtype, name, res=None, after=()):
    a_parts = a.shape[0] if a.ndim == 3 else 1
    b_parts = b.shape[0] if b.ndim == 3 else 1
    a_shape = (a.shape[1], a_parts * a.shape[2]) if a.ndim == 3 else a.shape
    b_shape = (b.shape[1], b_parts * b.shape[2]) if b.ndim == 3 else b.shape
    assert (a_parts == 1 or mode != "tn") and (b_parts == 1 or mode == "tn")
    if mode == "nn":
        (m, k), (k2, n) = a_shape, b_shape
    elif mode == "nt":
        (m, k), (n, k2) = a_shape, b_shape
    else:
        (k, m), (k2, n) = a_shape, b_shape
    assert k == k2, (a.shape, b.shape, mode)
    tm, tn, tk = _pick(m, MM_TILE_M), _pick(n // b_parts, MM_TILE_N), _pick(k // a_parts, MM_TILE_K)
    nk = k // tk
    if mode == "tn":
        a_spec = pl.BlockSpec((tk, tm), lambda i, j, kk: (kk, i))
    elif a_parts > 1:
        per = nk // a_parts
        a_spec = pl.BlockSpec((None, tm, tk), lambda i, j, kk: (kk // per, i, kk % per))
    else:
        a_spec = pl.BlockSpec((tm, tk), lambda i, j, kk: (i, kk))
    if mode == "nt":
        b_spec = pl.BlockSpec((tn, tk), lambda i, j, kk: (j, kk))
    elif b_parts > 1:
        per = n // b_parts // tn
        b_spec = pl.BlockSpec((None, tk, tn), lambda i, j, kk: (j // per, kk, j % per))
    else:
        b_spec = pl.BlockSpec((tk, tn), lambda i, j, kk: (kk, j))
    o_spec = pl.BlockSpec((tm, tn), lambda i, j, kk: (i, j))
    dims = {"nn": ((1,), (0,)), "nt": ((1,), (1,)), "tn": ((0,), (0,))}[mode]
    has_res = res is not None

    def body(*refs):
        a_ref, b_ref = refs[0], refs[1]
        r_ref = refs[2] if has_res else None
        o_ref = refs[3] if has_res else refs[2]
        part = _dot(a_ref[...].astype(BF16), b_ref[...].astype(BF16), dims)

        def finish(total):
            if has_res:
                total = total + r_ref[...]
            o_ref[...] = total.astype(out_dtype)

        if nk == 1:
            finish(part)
        else:
            acc_ref = refs[-1]
            kk = pl.program_id(2)

            @pl.when(kk == 0)
            def _():
                acc_ref[...] = part

            @pl.when(kk > 0)
            def _():
                acc_ref[...] += part

            @pl.when(kk == nk - 1)
            def _():
                finish(acc_ref[...])

    in_specs = [a_spec, b_spec] + ([o_spec] if has_res else [])
    args = (a, b) + ((res,) if has_res else ())
    return _pcall(
        body, after=after, name=name, grid=(m // tm, n // tn, nk),
        in_specs=in_specs, out_specs=o_spec,
        out_shape=jax.ShapeDtypeStruct((m, n), out_dtype),
        scratch_shapes=[pltpu.VMEM((tm, tn), F32)] if nk > 1 else [],
        compiler_params=_params("parallel", "parallel", "arbitrary"),
    )(*args)


ROW_TILE = 256


def _rmsnorm_fwd(x, g, name, after=()):
    s, d = x.shape

    def body(x_ref, g_ref, o_ref):
        xf = x_ref[...]
        r = lax.rsqrt(jnp.mean(xf * xf, axis=-1, keepdims=True) + EPS)
        o_ref[...] = (xf * r * g_ref[...]).astype(BF16)

    row = pl.BlockSpec((ROW_TILE, d), lambda i: (i, 0))
    vec = pl.BlockSpec((1, d), lambda i: (0, 0))
    return _pcall(body, after=after, name=name, grid=(s // ROW_TILE,), in_specs=[row, vec], out_specs=row,
                  out_shape=jax.ShapeDtypeStruct((s, d), BF16), compiler_params=_params("parallel"))(x, g)


def _rmsnorm_bwd(x, g, dh, dres, name):
    s, d = x.shape

    def body(x_ref, g_ref, dh_ref, dres_ref, dx_ref, dg_ref):
        xf = x_ref[...]
        r = lax.rsqrt(jnp.mean(xf * xf, axis=-1, keepdims=True) + EPS)
        xhat = xf * r
        dh_v = dh_ref[...]
        dxh = dh_v * g_ref[...]
        proj = jnp.mean(dxh * xhat, axis=-1, keepdims=True)
        dx_ref[...] = dres_ref[...] + r * (dxh - xhat * proj)
        part = jnp.sum(dh_v * xhat, axis=0, keepdims=True)

        @pl.when(pl.program_id(0) == 0)
        def _():
            dg_ref[...] = part

        @pl.when(pl.program_id(0) > 0)
        def _():
            dg_ref[...] += part

    row = pl.BlockSpec((ROW_TILE, d), lambda i: (i, 0))
    vec = pl.BlockSpec((1, d), lambda i: (0, 0))
    return _pcall(body, name=name, grid=(s // ROW_TILE,), in_specs=[row, vec, row, row], out_specs=[row, vec],
                  out_shape=[jax.ShapeDtypeStruct((s, d), F32), jax.ShapeDtypeStruct((1, d), F32)],
                  compiler_params=_params("arbitrary"))(x, g, dh, dres)


def _loss_head(x, g, target, name):
    s, d = x.shape

    def body(x_ref, g_ref, t_ref, dx_ref, dg_ref, loss_ref):
        xf = x_ref[...]
        r = lax.rsqrt(jnp.mean(xf * xf, axis=-1, keepdims=True) + EPS)
        xhat = xf * r
        gv = g_ref[...]
        err = xhat * gv - t_ref[...]
        dy = err * (1.0 / d)
        dxh = dy * gv
        proj = jnp.mean(dxh * xhat, axis=-1, keepdims=True)
        dx_ref[...] = r * (dxh - xhat * proj)
        dg_part = jnp.sum(dy * xhat, axis=0, keepdims=True)
        row_loss = jnp.sum(err * err, axis=-1, keepdims=True) * (0.5 / d)
        loss_part = jnp.broadcast_to(jnp.sum(row_loss, axis=0, keepdims=True), (1, 128))

        @pl.when(pl.program_id(0) == 0)
        def _():
            dg_ref[...] = dg_part
            loss_ref[...] = loss_part

        @pl.when(pl.program_id(0) > 0)
        def _():
            dg_ref[...] += dg_part
            loss_ref[...] += loss_part

    row = pl.BlockSpec((ROW_TILE, d), lambda i: (i, 0))
    vec = pl.BlockSpec((1, d), lambda i: (0, 0))
    one = pl.BlockSpec((1, 128), lambda i: (0, 0))
    return _pcall(body, name=name, grid=(s // ROW_TILE,), in_specs=[row, vec, row], out_specs=[row, vec, one],
                  out_shape=[jax.ShapeDtypeStruct((s, d), F32), jax.ShapeDtypeStruct((1, d), F32),
                             jax.ShapeDtypeStruct((1, 128), F32)],
                  compiler_params=_params("arbitrary"))(x, g, target)


HEADS_PER_STEP = 2
GROUP_W = HEADS_PER_STEP * HEAD
N_GROUPS = N_HEADS // HEADS_PER_STEP


def _head_cols(h):
    return slice(h * HEAD, (h + 1) * HEAD)


def _head_specs(s, col0):
    t = ATT_BLOCK
    g0 = [c // HEADS_PER_STEP for c in col0]
    qspec = pl.BlockSpec((t, GROUP_W), lambda g, i: (i, g0[0] + g))
    kspec = pl.BlockSpec((s, GROUP_W), lambda g, i: (0, g0[1] + g))
    vspec = pl.BlockSpec((s, GROUP_W), lambda g, i: (0, g0[2] + g))
    return qspec, kspec, vspec


TRI = 256


def _order_matrix(later):
    r, c = lax.broadcasted_iota(jnp.int32, (TRI, TRI), 0), lax.broadcasted_iota(jnp.int32, (TRI, TRI), 1)
    return (r > c if later else r < c).astype(BF16)


def _exact_dot(x, m, later):
    parts = [x[:, c:c + TRI] for c in range(0, x.shape[1], TRI)]
    totals = [jnp.sum(p, axis=1, keepdims=True) for p in parts] if len(parts) > 1 else None
    out = []
    for j, p in enumerate(parts):
        hi = p.astype(BF16)
        lo = (p - hi.astype(F32)).astype(BF16)
        acc = _dot_nn(hi, m) + _dot_nn(lo, m)
        for other in (range(j + 1, len(parts)) if later else range(j)):
            acc = acc + totals[other]
        out.append(acc)
    return out[0] if len(out) == 1 else jnp.concatenate(out, axis=1)


def _sb_block(q, kblk, carry_l, u, diagonal):
    t = ATT_BLOCK
    z = _dot_nt(q, kblk) * ATT_SCALE
    sp = jnp.maximum(z, 0.0) + jnp.log(1.0 + jnp.exp(-jnp.abs(z)))
    if not diagonal:
        l = -sp
        return z, None, l, jnp.exp(z + l + _exact_dot(l, u, True) + carry_l)
    mask = lax.broadcasted_iota(jnp.int32, (t, t), 1) < lax.broadcasted_iota(jnp.int32, (t, t), 0)
    l = jnp.where(mask, -sp, 0.0)
    a = jnp.where(mask, jnp.exp(z - sp + _exact_dot(l, u, True) + carry_l), 0.0)
    return z, mask, l, a


def _sb_carry_spec(s):
    t = ATT_BLOCK
    return pl.BlockSpec((HEADS_PER_STEP, None, s // t, t, 1), lambda g, i: (g, i, 0, 0, 0))


def _sb_fwd(p, name, after=()):
    s = p.shape[0]
    t = ATT_BLOCK
    nb = s // t

    def body(q_ref, k_ref, v_ref, o_ref, cl_ref):
        i = pl.program_id(1)
        heads = range(HEADS_PER_STEP)
        q = [q_ref[:, _head_cols(h)].astype(BF16) for h in heads]
        u = _order_matrix(True)
        cl_ref[...] = jnp.zeros_like(cl_ref)

        def tile(kb, carry, diagonal):
            ks = pl.multiple_of(kb * t, t)
            out = []
            for h in heads:
                acc, carry_l = carry[h]
                kblk = k_ref[pl.ds(ks, t), _head_cols(h)].astype(BF16)
                vblk = v_ref[pl.ds(ks, t), _head_cols(h)].astype(BF16)
                cl_ref[h, kb] = carry_l
                _, _, l, a = _sb_block(q[h], kblk, carry_l, u, diagonal)
                out.append((acc + _dot_nn(a.astype(BF16), vblk), carry_l + jnp.sum(l, axis=1, keepdims=True)))
            return tuple(out)

        carry = tile(i, tuple((jnp.zeros((t, HEAD), F32), jnp.zeros((t, 1), F32)) for _ in heads), True)
        carry = lax.fori_loop(0, i, lambda n, c: tile(i - 1 - n, c, False), carry)
        for h in heads:
            o_ref[:, _head_cols(h)] = carry[h][0]

    qspec, kspec, vspec = _head_specs(s, (0, N_HEADS, 2 * N_HEADS))
    ospec = pl.BlockSpec((t, GROUP_W), lambda g, i: (i, g))
    return _pcall(body, after=after, name=name, grid=(N_GROUPS, nb), in_specs=[qspec, kspec, vspec],
                  out_specs=[ospec, _sb_carry_spec(s)],
                  out_shape=[jax.ShapeDtypeStruct((s, HALF), F32), jax.ShapeDtypeStruct((N_HEADS, nb, nb, t, 1), F32)],
                  compiler_params=_params("parallel", "parallel"))(p, p, p)


def _sb_bwd(p, d_ab, carries, name):
    s = p.shape[0]
    t = ATT_BLOCK

    def body(q_ref, k_ref, v_ref, do_ref, cl_ref, dq_ref, dk_ref, dv_ref):
        i = pl.program_id(1)

        @pl.when(i == 0)
        def _():
            dk_ref[...] = jnp.zeros_like(dk_ref)
            dv_ref[...] = jnp.zeros_like(dv_ref)

        heads = range(HEADS_PER_STEP)
        q = [q_ref[:, _head_cols(h)].astype(BF16) for h in heads]
        do = [do_ref[:, _head_cols(h)].astype(BF16) for h in heads]
        u = _order_matrix(True)
        lower = _order_matrix(False)

        def tile(kb, carry, diagonal):
            ks = pl.multiple_of(kb * t, t)
            out = []
            for h in heads:
                dq, carry_g = carry[h]
                kblk = k_ref[pl.ds(ks, t), _head_cols(h)].astype(BF16)
                vblk = v_ref[pl.ds(ks, t), _head_cols(h)].astype(BF16)
                z, mask, _, a = _sb_block(q[h], kblk, cl_ref[h, kb], u, diagonal)
                g = a * _dot_nt(do[h], vblk)
                earlier_g = _exact_dot(g, lower, False) + carry_g
                sig = jax.nn.sigmoid(z)
                dz = g * (1.0 - sig) - sig * earlier_g
                if diagonal:
                    dz = jnp.where(mask, dz, 0.0)
                dz = dz.astype(BF16)
                dv_ref[pl.ds(ks, t), _head_cols(h)] += _dot_tn(a.astype(BF16), do[h])
                dk_ref[pl.ds(ks, t), _head_cols(h)] += _dot_tn(dz, q[h]) * ATT_SCALE
                out.append((dq + _dot_nn(dz, kblk) * ATT_SCALE, carry_g + jnp.sum(g, axis=1, keepdims=True)))
            return tuple(out)

        init = tuple((jnp.zeros((t, HEAD), F32), jnp.zeros((t, 1), F32)) for _ in heads)
        carry = tile(i, lax.fori_loop(0, i, lambda kb, c: tile(kb, c, False), init), True)
        for h in heads:
            dq_ref[:, _head_cols(h)] = carry[h][0]

    qspec, kspec, vspec = _head_specs(s, (0, N_HEADS, 2 * N_HEADS))
    blk = pl.BlockSpec((t, GROUP_W), lambda g, i: (i, g))
    whole = pl.BlockSpec((s, GROUP_W), lambda g, i: (0, g))
    shape = jax.ShapeDtypeStruct((s, HALF), F32)
    return _pcall(body, name=name, grid=(N_GROUPS, s // t), in_specs=[qspec, kspec, vspec, blk, _sb_carry_spec(s)],
                  out_specs=[blk, whole, whole], out_shape=[shape, shape, shape],
                  compiler_params=_params("parallel", "arbitrary"))(p, p, p, d_ab, carries)


COL_TILE = 256


def _sc_fwd(p, w, name):
    s = p.shape[0]
    nb = HALF // COL_TILE

    def body(gb_ref, gc_ref, h_ref, w_ref, o_ref):
        conv = _conv3(gc_ref[...] * h_ref[...], w_ref[...])
        o_ref[...] = (gb_ref[...] * conv).astype(BF16)

    def col(k):
        return pl.BlockSpec((s, COL_TILE), lambda j: (0, k * nb + j))

    wspec = pl.BlockSpec((3, COL_TILE), lambda j: (0, j))
    return _pcall(body, name=name, grid=(nb,), in_specs=[col(3), col(4), col(5), wspec], out_specs=col(0),
                  out_shape=jax.ShapeDtypeStruct((s, HALF), BF16), compiler_params=_params("parallel"))(p, p, p, w)


def _sc_bwd(p, w, d_ab, name):
    s = p.shape[0]
    nb = HALF // COL_TILE

    def body(gb_ref, gc_ref, h_ref, w_ref, d_ref, dgb_ref, dgc_ref, dh_ref, dw_ref):
        gc, hin, wv, d = gc_ref[...], h_ref[...], w_ref[...], d_ref[...]
        sig = gc * hin
        shifted = (_shift_down(sig, 1), _shift_down(sig, 2))
        dgb_ref[...] = d * _conv3(sig, wv, shifted)
        dconv = d * gb_ref[...]
        _conv3_wgrad(dconv, sig, shifted, dw_ref)
        dsig = _conv3_transpose(dconv, wv)
        dgc_ref[...] = dsig * hin
        dh_ref[...] = dsig * gc

    def col(k):
        return pl.BlockSpec((s, COL_TILE), lambda j: (0, k * nb + j))

    wspec = pl.BlockSpec((3, COL_TILE), lambda j: (0, j))
    act = jax.ShapeDtypeStruct((s, HALF), F32)
    return _pcall(body, name=name, grid=(nb,), in_specs=[col(3), col(4), col(5), wspec, col(1)],
                  out_specs=[col(0), col(0), col(0), wspec],
                  out_shape=[act, act, act, jax.ShapeDtypeStruct((3, HALF), F32)],
                  compiler_params=_params("parallel"))(p, p, p, w, d_ab)


def _ffn_act_fwd(u, w, name, after=()):
    s = u.shape[0]
    nb = D_FF // COL_TILE

    def body(ug_ref, uu_ref, wg_ref, wu_ref, o_ref):
        gate = _conv3(ug_ref[...], wg_ref[...])
        up = _conv3(uu_ref[...], wu_ref[...])
        o_ref[...] = (gate * jax.nn.sigmoid(gate) * up).astype(BF16)

    def col(k):
        return pl.BlockSpec((s, COL_TILE), lambda j: (0, k * nb + j))

    def wcol(k):
        return pl.BlockSpec((3, COL_TILE), lambda j: (0, k * nb + j))

    return _pcall(body, after=after, name=name, grid=(nb,), in_specs=[col(0), col(1), wcol(0), wcol(1)], out_specs=col(0),
                  out_shape=jax.ShapeDtypeStruct((s, D_FF), BF16),
                  compiler_params=_params("parallel"))(u, u, w, w)


def _ffn_act_bwd(u, w, d_f, name):
    s = u.shape[0]
    nb = D_FF // COL_TILE

    def body(ug_ref, uu_ref, wg_ref, wu_ref, d_ref, du_ref, dwg_ref, dwu_ref):
        ug, uu, wg, wu, d = ug_ref[...], uu_ref[...], wg_ref[...], wu_ref[...], d_ref[...]
        ug_shifted = (_shift_down(ug, 1), _shift_down(ug, 2))
        uu_shifted = (_shift_down(uu, 1), _shift_down(uu, 2))
        gate = _conv3(ug, wg, ug_shifted)
        up = _conv3(uu, wu, uu_shifted)
        sig = jax.nn.sigmoid(gate)
        d_up = d * gate * sig
        d_gate = d * up * sig * (1.0 + gate * (1.0 - sig))
        _conv3_wgrad(d_gate, ug, ug_shifted, dwg_ref)
        _conv3_wgrad(d_up, uu, uu_shifted, dwu_ref)
        du_ref[0] = _conv3_transpose(d_gate, wg).astype(BF16)
        du_ref[1] = _conv3_transpose(d_up, wu).astype(BF16)

    def col(k):
        return pl.BlockSpec((s, COL_TILE), lambda j: (0, k * nb + j))

    def wcol(k):
        return pl.BlockSpec((3, COL_TILE), lambda j: (0, k * nb + j))

    both = pl.BlockSpec((2, s, COL_TILE), lambda j: (0, 0, j))
    wsh = jax.ShapeDtypeStruct((3, D_FF), F32)
    return _pcall(body, name=name, grid=(nb,), in_specs=[col(0), col(1), wcol(0), wcol(1), col(0)],
                  out_specs=[both, wcol(0), wcol(0)], out_shape=[jax.ShapeDtypeStruct((2, s, D_FF), BF16), wsh, wsh],
                  compiler_params=_params("parallel"))(u, u, w, w, d_f)


def _sg_common(u, v, g, w_ref, bias, mixed_ref):
    rows = u.shape[0]
    gu = _gelu(u)
    gv = _gelu(v)
    xc = gv - jnp.mean(gv, axis=-1, keepdims=True)
    rstd = lax.rsqrt(jnp.mean(xc * xc, axis=-1, keepdims=True) + EPS)
    xhat = xc * rstd
    vn = xhat * g
    tril = lax.broadcasted_iota(jnp.int32, (HEAD, HEAD), 0) >= lax.broadcasted_iota(jnp.int32, (HEAD, HEAD), 1)
    wts = [jnp.where(tril, w_ref[grp], 0.0).astype(BF16) for grp in range(N_HEADS)]
    for n in range(rows // HEAD):
        for grp in range(N_HEADS):
            blk = vn[n * HEAD:(n + 1) * HEAD, grp * HEAD:(grp + 1) * HEAD].astype(BF16)
            mixed_ref[n * HEAD:(n + 1) * HEAD, grp * HEAD:(grp + 1) * HEAD] = _dot_nn(wts[grp], blk)
    mixed = mixed_ref[...] + jnp.concatenate([bias] * (rows // HEAD), axis=0)
    return gu, xhat, rstd, vn, mixed, wts, tril


def _sg_fwd(p, sg_w, bias, g, name):
    s = p.shape[0]

    def body(u_ref, v_ref, w_ref, b_ref, g_ref, o_ref, mixed_ref):
        gu, _, _, _, mixed, _, _ = _sg_common(u_ref[...], v_ref[...], g_ref[...], w_ref, b_ref[...], mixed_ref)
        o_ref[...] = (gu * mixed).astype(BF16)

    def half(k):
        return pl.BlockSpec((ROW_TILE, HALF), lambda i: (i, k))

    wspec = pl.BlockSpec((N_HEADS, HEAD, HEAD), lambda i: (0, 0, 0))
    bspec = pl.BlockSpec((HEAD, HALF), lambda i: (0, 0))
    gspec = pl.BlockSpec((1, HALF), lambda i: (0, 0))
    return _pcall(body, name=name, grid=(s // ROW_TILE,), in_specs=[half(0), half(1), wspec, bspec, gspec],
                  out_specs=half(0), out_shape=jax.ShapeDtypeStruct((s, HALF), BF16),
                  scratch_shapes=[pltpu.VMEM((ROW_TILE, HALF), F32)],
                  compiler_params=_params("parallel"))(p, p, sg_w, bias, g)


def _sg_bwd(p, sg_w, bias, g, d_cd, name):
    s = p.shape[0]
    nsteps = s // ROW_TILE

    def body(u_ref, v_ref, w_ref, b_ref, g_ref, d_ref, du_ref, dv_ref, dw_ref, db_ref, dg_ref,
             mixed_ref, dvn_ref, dbias_ref):
        i = pl.program_id(0)
        u, v, gain, d = u_ref[...], v_ref[...], g_ref[...], d_ref[...]
        gu, xhat, rstd, vn, mixed, wts, tril = _sg_common(u, v, gain, w_ref, b_ref[...], mixed_ref)

        @pl.when(i == 0)
        def _():
            dw_ref[...] = jnp.zeros_like(dw_ref)
            dg_ref[...] = jnp.zeros_like(dg_ref)
            dbias_ref[...] = jnp.zeros_like(dbias_ref)

        du_ref[...] = d * mixed * _gelu_grad(u)
        dm = d * gu
        for n in range(ROW_TILE // HEAD):
            rs = slice(n * HEAD, (n + 1) * HEAD)
            dbias_ref[...] += dm[rs, :]
            for grp in range(N_HEADS):
                cs = slice(grp * HEAD, (grp + 1) * HEAD)
                dm_blk = dm[rs, cs].astype(BF16)
                dw_ref[grp] += jnp.where(tril, _dot_nt(dm_blk, vn[rs, cs].astype(BF16)), 0.0)
                dvn_ref[rs, cs] = _dot_tn(wts[grp], dm_blk)
        dvn = dvn_ref[...]
        dg_ref[...] += jnp.sum(dvn * xhat, axis=0, keepdims=True)
        dxh = dvn * gain
        d_gv = rstd * (dxh - jnp.mean(dxh, axis=-1, keepdims=True) - xhat * jnp.mean(dxh * xhat, axis=-1, keepdims=True))
        dv_ref[...] = d_gv * _gelu_grad(v)

        @pl.when(i == nsteps - 1)
        def _():
            lane = lax.broadcasted_iota(jnp.int32, (HEAD, HEAD), 1)
            out = jnp.zeros((HEAD, HEAD), F32)
            for grp in range(N_HEADS):
                tot = jnp.sum(dbias_ref[:, grp * HEAD:(grp + 1) * HEAD], axis=1, keepdims=True)
                out = out + jnp.where(lane == grp, tot, 0.0)
            db_ref[...] = out

    def half(k):
        return pl.BlockSpec((ROW_TILE, HALF), lambda i: (i, k))

    wspec = pl.BlockSpec((N_HEADS, HEAD, HEAD), lambda i: (0, 0, 0))
    bspec = pl.BlockSpec((HEAD, HALF), lambda i: (0, 0))
    gspec = pl.BlockSpec((1, HALF), lambda i: (0, 0))
    dbspec = pl.BlockSpec((HEAD, HEAD), lambda i: (0, 0))
    act = jax.ShapeDtypeStruct((s, HALF), F32)
    return _pcall(body, name=name, grid=(nsteps,), in_specs=[half(0), half(1), wspec, bspec, gspec, half(0)],
                  out_specs=[half(0), half(0), wspec, dbspec, gspec],
                  out_shape=[act, act, jax.ShapeDtypeStruct((N_HEADS, HEAD, HEAD), F32),
                             jax.ShapeDtypeStruct((HEAD, HEAD), F32), jax.ShapeDtypeStruct((1, HALF), F32)],
                  scratch_shapes=[pltpu.VMEM((ROW_TILE, HALF), F32), pltpu.VMEM((ROW_TILE, HALF), F32),
                                  pltpu.VMEM((HEAD, HALF), F32)],
                  compiler_params=_params("arbitrary"))(p, p, sg_w, bias, g, d_cd)


def _fox_prep(f, b, name):
    s = f.shape[0]
    t = ATT_BLOCK

    def body(f_ref, b_ref, c_ref):
        tri = (lax.broadcasted_iota(jnp.int32, (t, t), 0) >= lax.broadcasted_iota(jnp.int32, (t, t), 1)).astype(BF16)
        carry = jnp.zeros((1, 128), F32)
        for n in range(s // t):
            lf = _log_sigmoid(f_ref[n * t:(n + 1) * t, :] + b_ref[...])
            hi, mid, lo = _split3(lf)
            c_ref[n * t:(n + 1) * t, :] = _dot_nn(tri, hi) + _dot_nn(tri, mid) + _dot_nn(tri, lo) + carry
            carry = carry + jnp.sum(lf, axis=0, keepdims=True)

    return _pcall(body, name=name, in_specs=[VMEM_SPEC, VMEM_SPEC], out_specs=VMEM_SPEC,
                  out_shape=jax.ShapeDtypeStruct((s, 128), F32))(f, b)


def _fox_post(drow, dcol, f, b, name):
    s = f.shape[0]
    t = ATT_BLOCK

    def body(drow_ref, dcol_ref, f_ref, b_ref, df_ref, db_ref):
        tri = (lax.broadcasted_iota(jnp.int32, (t, t), 1) >= lax.broadcasted_iota(jnp.int32, (t, t), 0)).astype(BF16)
        carry = jnp.zeros((1, 128), F32)
        db = jnp.zeros((1, 128), F32)
        for n in reversed(range(s // t)):
            rs = slice(n * t, (n + 1) * t)
            dc = drow_ref[rs, :] - dcol_ref[rs, :]
            hi, mid, lo = _split3(dc)
            dlogf = _dot_nn(tri, hi) + _dot_nn(tri, mid) + _dot_nn(tri, lo) + carry
            carry = carry + jnp.sum(dc, axis=0, keepdims=True)
            df = dlogf * jax.nn.sigmoid(-(f_ref[rs, :] + b_ref[...]))
            df_ref[rs, :] = df
            db = db + jnp.sum(df, axis=0, keepdims=True)
        db_ref[...] = db

    return _pcall(body, name=name, in_specs=[VMEM_SPEC] * 4, out_specs=[VMEM_SPEC, VMEM_SPEC],
                  out_shape=[jax.ShapeDtypeStruct((s, 128), F32), jax.ShapeDtypeStruct((1, 128), F32)])(drow, dcol, f, b)


def _fox_specs(s):
    t = ATT_BLOCK
    ccol = pl.BlockSpec((HEADS_PER_STEP, t, 1), lambda g, i: (g, i, 0))
    crow = pl.BlockSpec((HEADS_PER_STEP, s // t, 1, t), lambda g, i: (g, 0, 0, 0))
    return ccol, crow


def _fox_fwd(p, c_col, c_row, name, after=()):
    s = p.shape[0]
    t = ATT_BLOCK

    def body(q_ref, k_ref, v_ref, cc_ref, cr_ref, o_ref, lse_ref):
        i = pl.program_id(1)
        heads = range(HEADS_PER_STEP)
        q = [q_ref[:, _head_cols(h)].astype(BF16) for h in heads]
        ct = [cc_ref[h] for h in heads]

        def tile(n, carry, diagonal):
            ks = pl.multiple_of(n * t, t)
            out = []
            for h in heads:
                acc, m, l = carry[h]
                kblk = k_ref[pl.ds(ks, t), _head_cols(h)].astype(BF16)
                vblk = v_ref[pl.ds(ks, t), _head_cols(h)].astype(BF16)
                logit = _dot_nt(q[h], kblk) * ATT_SCALE + ct[h] - cr_ref[h, n]
                if diagonal:
                    causal = lax.broadcasted_iota(jnp.int32, (t, t), 1) <= lax.broadcasted_iota(jnp.int32, (t, t), 0)
                    logit = jnp.where(causal, logit, NEG)
                m_new = jnp.maximum(m, jnp.max(logit, axis=1, keepdims=True))
                alpha = jnp.exp(m - m_new)
                pr = jnp.exp(logit - m_new)
                l = alpha * l + jnp.sum(pr, axis=1, keepdims=True)
                out.append((alpha * acc + _dot_nn(pr.astype(BF16), vblk), m_new, l))
            return tuple(out)

        init = tuple((jnp.zeros((t, HEAD), F32), jnp.full((t, 1), NEG, F32), jnp.zeros((t, 1), F32)) for _ in heads)
        carry = tile(i, lax.fori_loop(0, i, lambda n, c: tile(n, c, False), init), True)
        for h in heads:
            acc, m, l = carry[h]
            o_ref[:, _head_cols(h)] = acc / l
            lse_ref[h] = m + jnp.log(l)

    qspec, kspec, vspec = _head_specs(s, (2 * N_HEADS, 3 * N_HEADS, 4 * N_HEADS))
    ccol, crow = _fox_specs(s)
    ospec = pl.BlockSpec((t, GROUP_W), lambda g, i: (i, g))
    return _pcall(body, after=after, name=name, grid=(N_GROUPS, s // t), in_specs=[qspec, kspec, vspec, ccol, crow],
                  out_specs=[ospec, ccol],
                  out_shape=[jax.ShapeDtypeStruct((s, HALF), F32), jax.ShapeDtypeStruct((N_HEADS, s, 1), F32)],
                  compiler_params=_params("parallel", "parallel"))(p, p, p, c_col, c_row)


def _fox_bwd(p, c_col, c_row, lse, d_cd, d_out, name):
    s = p.shape[0]
    t = ATT_BLOCK

    def body(q_ref, k_ref, v_ref, cc_ref, cr_ref, lse_ref, do_ref, o_ref, dq_ref, dk_ref, dv_ref, dcol_ref, drow_ref):
        i = pl.program_id(1)

        @pl.when(i == 0)
        def _():
            dk_ref[...] = jnp.zeros_like(dk_ref)
            dv_ref[...] = jnp.zeros_like(dv_ref)
            dcol_ref[...] = jnp.zeros_like(dcol_ref)

        heads = range(HEADS_PER_STEP)
        q = [q_ref[:, _head_cols(h)].astype(BF16) for h in heads]
        do = [do_ref[:, _head_cols(h)].astype(BF16) for h in heads]
        delta = [jnp.sum(do_ref[:, _head_cols(h)] * o_ref[:, _head_cols(h)], axis=1, keepdims=True) for h in heads]
        ct = [cc_ref[h] for h in heads]
        lse_v = [lse_ref[h] for h in heads]
        ones = jnp.ones((t, HEAD), BF16)

        def tile(n, carry, diagonal):
            ks = pl.multiple_of(n * t, t)
            out = []
            for h in heads:
                dq, drow = carry[h]
                kblk = k_ref[pl.ds(ks, t), _head_cols(h)].astype(BF16)
                vblk = v_ref[pl.ds(ks, t), _head_cols(h)].astype(BF16)
                logit = _dot_nt(q[h], kblk) * ATT_SCALE + ct[h] - cr_ref[h, n]
                pr = jnp.exp(logit - lse_v[h])
                if diagonal:
                    causal = lax.broadcasted_iota(jnp.int32, (t, t), 1) <= lax.broadcasted_iota(jnp.int32, (t, t), 0)
                    pr = jnp.where(causal, pr, 0.0)
                ds = pr * (_dot_nt(do[h], vblk) - delta[h])
                dsb = ds.astype(BF16)
                dv_ref[pl.ds(ks, t), _head_cols(h)] += _dot_tn(pr.astype(BF16), do[h])
                dk_ref[pl.ds(ks, t), _head_cols(h)] += _dot_tn(dsb, q[h]) * ATT_SCALE
                dcol_ref[pl.ds(ks, t), _head_cols(h)] += _dot_tn(dsb, ones)
                out.append((dq + _dot_nn(dsb, kblk) * ATT_SCALE,
                            drow + jnp.sum(dsb.astype(F32), axis=1, keepdims=True)))
            return tuple(out)

        init = tuple((jnp.zeros((t, HEAD), F32), jnp.zeros((t, 1), F32)) for _ in heads)
        carry = tile(i, lax.fori_loop(0, i, lambda n, c: tile(n, c, False), init), True)
        for h in heads:
            dq_ref[:, _head_cols(h)] = carry[h][0]
            drow_ref[h] = carry[h][1]

    qspec, kspec, vspec = _head_specs(s, (2 * N_HEADS, 3 * N_HEADS, 4 * N_HEADS))
    ccol, crow = _fox_specs(s)
    dospec = pl.BlockSpec((t, GROUP_W), lambda g, i: (i, N_GROUPS + g))
    blk = pl.BlockSpec((t, GROUP_W), lambda g, i: (i, g))
    whole = pl.BlockSpec((s, GROUP_W), lambda g, i: (0, g))
    shape = jax.ShapeDtypeStruct((s, HALF), F32)
    return _pcall(body, name=name, grid=(N_GROUPS, s // t),
                  in_specs=[qspec, kspec, vspec, ccol, crow, ccol, dospec, blk],
                  out_specs=[blk, whole, whole, whole, ccol],
                  out_shape=[shape, shape, shape, shape, jax.ShapeDtypeStruct((N_HEADS, s, 1), F32)],
                  compiler_params=_params("parallel", "arbitrary"))(p, p, p, c_col, c_row, lse, d_cd, d_out)


def _row_tile(rows, cap):
    for t in (256, 128, 64, 32, 16, 8):
        if t <= cap and rows % t == 0:
            return t
    return rows


def _adamw(w, g, m, v, name):
    rows, cols = w.shape
    tr = _row_tile(rows, 128)
    c1 = 1.0 / (1.0 - ADAM_B1 ** ADAM_STEP)
    c2 = 1.0 / (1.0 - ADAM_B2 ** ADAM_STEP)

    def body(w_ref, g_ref, m_ref, v_ref, d_ref, nm_ref, nv_ref):
        gv = g_ref[...]
        nm = ADAM_B1 * m_ref[...] + (1.0 - ADAM_B1) * gv
        nv = ADAM_B2 * v_ref[...] + (1.0 - ADAM_B2) * (gv * gv)
        nm_ref[...] = nm
        nv_ref[...] = nv
        d_ref[...] = -ADAM_LR * ((nm * c1) / (jnp.sqrt(nv * c2) + ADAM_EPS) + ADAM_WD * w_ref[...])

    spec = pl.BlockSpec((tr, cols), lambda i: (i, 0))
    shape = jax.ShapeDtypeStruct((rows, cols), F32)
    return _pcall(body, name=name, grid=(rows // tr,), in_specs=[spec] * 4, out_specs=[spec] * 3,
                  out_shape=[shape] * 3, compiler_params=_params("parallel"))(w, g, m, v)


def _half_shape(whole_shape, kind):
    if kind == "col":
        return (whole_shape[0] // 2, whole_shape[1] // 4)
    if kind == "row":
        return (whole_shape[0] // 8, whole_shape[1])
    return (whole_shape[1] // 2, whole_shape[2])


def _own_half_spec(whole_shape, kind, tr):
    hr, hc = _half_shape(whole_shape, kind)
    nb = hr // tr
    if kind == "col":
        return pl.BlockSpec((tr, hc), lambda i, pos: (pos[1] * nb + i, pos[0]))
    if kind == "row":
        return pl.BlockSpec((tr, hc), lambda i, pos: ((2 * pos[0] + pos[1]) * nb + i, 0))
    return pl.BlockSpec((None, tr, hc), lambda i, pos: (pos[0], pos[1] * nb + i, 0))


def _sum_partials(pos, grad, landed, kind, name):
    hr, hc = _half_shape(grad.shape, kind)
    tr = _row_tile(hr, 64)

    def body(pos_ref, g_ref, p_ref, o_ref):
        acc = g_ref[...].astype(F32)
        for k in range(N_DEV - 1):
            acc = acc + p_ref[k].astype(F32)
        o_ref[...] = acc

    grid_spec = pltpu.PrefetchScalarGridSpec(
        num_scalar_prefetch=1, grid=(hr // tr,),
        in_specs=[_own_half_spec(grad.shape, kind, tr), pl.BlockSpec((N_DEV - 1, tr, hc), lambda i, pos: (0, i, 0))],
        out_specs=pl.BlockSpec((tr, hc), lambda i, pos: (i, 0)))
    return _pcall(body, name=name, grid_spec=grid_spec, out_shape=jax.ShapeDtypeStruct((hr, hc), F32),
                  compiler_params=_params("parallel"))(pos, grad, landed)


def _adamw_shard(pos, w, g_mine, g_sibling, m, v, name):
    hr, hc = g_mine.shape
    tr = _row_tile(hr, 128)
    nb = hr // tr
    c1 = 1.0 / (1.0 - ADAM_B1 ** ADAM_STEP)
    c2 = 1.0 / (1.0 - ADAM_B2 ** ADAM_STEP)

    def body(pos_ref, w_ref, gm_ref, gs_ref, m_ref, v_ref, g_ref, d_ref, nm_ref, nv_ref):
        mine = (pl.program_id(0) // nb) == pos_ref[1]
        gv = jnp.where(mine, gm_ref[...], gs_ref[...])
        nm = ADAM_B1 * m_ref[...] + (1.0 - ADAM_B1) * gv
        nv = ADAM_B2 * v_ref[...] + (1.0 - ADAM_B2) * (gv * gv)
        g_ref[...] = gv
        nm_ref[...] = nm
        nv_ref[...] = nv
        d_ref[...] = -ADAM_LR * ((nm * c1) / (jnp.sqrt(nv * c2) + ADAM_EPS) + ADAM_WD * w_ref[...])

    full = pl.BlockSpec((tr, hc), lambda i, pos: (i, 0))
    mine_spec = pl.BlockSpec((tr, hc), lambda i, pos: (jnp.clip(i - pos[1] * nb, 0, nb - 1), 0))
    sib_spec = pl.BlockSpec((tr, hc), lambda i, pos: (jnp.clip(i - (1 - pos[1]) * nb, 0, nb - 1), 0))
    grid_spec = pltpu.PrefetchScalarGridSpec(
        num_scalar_prefetch=1, grid=(2 * nb,), in_specs=[full, mine_spec, sib_spec, full, full], out_specs=[full] * 4)
    shape = jax.ShapeDtypeStruct((2 * hr, hc), F32)
    return _pcall(body, name=name, grid_spec=grid_spec, out_shape=[shape] * 4,
                  compiler_params=_params("parallel"))(pos, w, g_mine, g_sibling, m, v)


def _place_shard(pos, shard, kind, name, after=()):
    rows, cols = shard.shape
    tr = _row_tile(rows, 256)
    nb = rows // tr
    if kind == "col":
        out_spec = pl.BlockSpec((tr, cols), lambda i, pos: (i, pos[0]))
    elif kind == "row":
        out_spec = pl.BlockSpec((tr, cols), lambda i, pos: (pos[0] * nb + i, 0))
    else:
        out_spec = pl.BlockSpec((None, tr, cols), lambda i, pos: (pos[0], i, 0))

    def body(pos_ref, s_ref, *rest):
        rest[-1][...] = s_ref[...].astype(BF16)

    grid_spec = pltpu.PrefetchScalarGridSpec(
        num_scalar_prefetch=1, grid=(nb,),
        in_specs=[pl.BlockSpec((tr, cols), lambda i, pos: (i, 0))] + [pl.BlockSpec(memory_space=pl.ANY)] * len(after),
        out_specs=out_spec)
    return _pcall(body, name=name, grid_spec=grid_spec,
                  out_shape=jax.ShapeDtypeStruct(_whole_shape(shard.shape, kind), BF16),
                  compiler_params=_params("parallel"))(pos, shard, *after)


N_DEV = 8
RELATIONS = [(r >> 2 & 1, r >> 1 & 1, r & 1) for r in range(1, N_DEV)]


def _position():
    return lax.axis_index("x"), lax.axis_index("y"), lax.axis_index("c")


def _related(pos, rel):
    return tuple(1 - p if f else p for p, f in zip(pos, rel))


def _index(pos):
    return 4 * pos[0] + 2 * pos[1] + pos[2]


def _window(ref, kind, pos):
    px, py, pc = pos
    j = 2 * px + py
    if kind == "col":
        r, c = ref.shape
        return ref.at[pl.ds(pc * (r // 2), r // 2), pl.ds(pl.multiple_of(j * (c // 4), 128), c // 4)]
    if kind == "row":
        rj = ref.shape[0] // 4
        return ref.at[pl.ds(j * rj + pc * (rj // 2), rj // 2), :]
    r = ref.shape[1]
    return ref.at[j, pl.ds(pc * (r // 2), r // 2), :]


def _whole_shape(shard_shape, kind):
    r, c = shard_shape
    return {"col": (r, 4 * c), "row": (4 * r, c), "maj": (4, r, c)}[kind]


SEM_SPEC = pl.BlockSpec(memory_space=pltpu.SEMAPHORE)
ANY_SPEC = pl.BlockSpec(memory_space=pl.ANY)
DATAFLOW = pltpu.SideEffectType.DATAFLOW_SIDE_EFFECTING
TOKEN = jax.ShapeDtypeStruct((8, 128), F32)


def _hbm(a):
    return pltpu.with_memory_space_constraint(a, pltpu.HBM)


def _chips(x, y):
    return [(1 - x, y), (x, 1 - y), (1 - x, 1 - y)]


def _split_start(body, name, buffers, n_sems, after=()):
    n = len(buffers)

    def wrapped(*refs):
        body(refs[:n], refs[n], refs[n + 1])
        refs[-1][...] = jnp.zeros_like(refs[-1])

    outs = _pcall(
        wrapped, after=after, name=name, in_specs=[HBM_SPEC] * n,
        out_specs=[SEM_SPEC, SEM_SPEC] + [HBM_SPEC] * n + [VMEM_SPEC],
        out_shape=[pltpu.SemaphoreType.DMA(n_sems), pltpu.SemaphoreType.DMA(n_sems)]
        + [pltpu.HBM(b.shape, b.dtype) for b in buffers] + [TOKEN],
        input_output_aliases={i: 2 + i for i in range(n)},
        compiler_params=pltpu.CompilerParams(has_side_effects=DATAFLOW))(*[_hbm(b) for b in buffers])
    return outs[0], outs[1], list(outs[2:2 + n]), outs[2 + n]


def _split_wait(body, name, buffers, send_sems, recv_sems, after):
    n = len(buffers)
    after = list(after) if isinstance(after, (list, tuple)) else [after]

    def wrapped(*refs):
        body(refs[:n], refs[n], refs[n + 1])

    outs = _pcall(
        wrapped, name=name, in_specs=[HBM_SPEC] * n + [SEM_SPEC, SEM_SPEC] + [ANY_SPEC] * len(after),
        out_specs=[HBM_SPEC] * n, out_shape=[pltpu.HBM(b.shape, b.dtype) for b in buffers],
        input_output_aliases={i: i for i in range(n)},
        compiler_params=pltpu.CompilerParams(has_side_effects=DATAFLOW))(*buffers, send_sems, recv_sems, *after)
    return list(outs)


def _gather_start(wholes, kinds, name, after=()):
    def body(w_refs, send_sems, recv_sems):
        x, y, c = _position()
        for w, ref in enumerate(w_refs):
            mine = _window(ref, kinds[w], (x, y, c))
            for k, chip in enumerate(_chips(x, y)):
                pltpu.make_async_remote_copy(src_ref=mine, dst_ref=mine, send_sem=send_sems.at[3 * w + k],
                                             recv_sem=recv_sems.at[3 * w + k], device_id=(*chip, c),
                                             device_id_type=MESH).start()

    return _split_start(body, name, wholes, (3 * len(wholes),), after)


def _gather_forward(wholes, kinds, send1, recv1, after, name):
    n = len(wholes)

    def wrapped(*refs):
        w_refs, s1, r1, s2, r2 = refs[:n], refs[n], refs[n + 1], refs[n + 3], refs[n + 4]
        x, y, c = _position()
        for k, chip in enumerate(_chips(x, y)):
            for w, ref in enumerate(w_refs):
                theirs = _window(ref, kinds[w], (*chip, c))
                pltpu.make_async_remote_copy(src_ref=theirs, dst_ref=theirs, send_sem=s1.at[3 * w + k],
                                             recv_sem=r1.at[3 * w + k], device_id=(*chip, c),
                                             device_id_type=MESH).wait_recv()
                pltpu.make_async_remote_copy(src_ref=theirs, dst_ref=theirs, send_sem=s2.at[3 * w + k],
                                             recv_sem=r2.at[3 * w + k], device_id=(x, y, 1 - c),
                                             device_id_type=MESH).start()
        for w, ref in enumerate(w_refs):
            mine = _window(ref, kinds[w], (x, y, c))
            for k, chip in enumerate(_chips(x, y)):
                pltpu.make_async_remote_copy(src_ref=mine, dst_ref=mine, send_sem=s1.at[3 * w + k],
                                             recv_sem=r1.at[3 * w + k], device_id=(*chip, c),
                                             device_id_type=MESH).wait_send()
        refs[-1][...] = jnp.zeros_like(refs[-1])

    outs = _pcall(
        wrapped, name=name, in_specs=[HBM_SPEC] * n + [SEM_SPEC, SEM_SPEC, ANY_SPEC],
        out_specs=[SEM_SPEC, SEM_SPEC] + [HBM_SPEC] * n + [VMEM_SPEC],
        out_shape=[pltpu.SemaphoreType.DMA((3 * n,)), pltpu.SemaphoreType.DMA((3 * n,))]
        + [pltpu.HBM(b.shape, b.dtype) for b in wholes] + [TOKEN],
        input_output_aliases={i: 2 + i for i in range(n)},
        compiler_params=pltpu.CompilerParams(has_side_effects=DATAFLOW))(*wholes, send1, recv1, after)
    return outs[0], outs[1], list(outs[2:2 + n]), outs[2 + n]


def _gather_finish(wholes, kinds, send2, recv2, after, name):
    def body(w_refs, s2, r2):
        x, y, c = _position()
        for k, chip in enumerate(_chips(x, y)):
            for w, ref in enumerate(w_refs):
                sent = _window(ref, kinds[w], (*chip, c))
                got = _window(ref, kinds[w], (*chip, 1 - c))
                pltpu.make_async_remote_copy(src_ref=sent, dst_ref=got, send_sem=s2.at[3 * w + k],
                                             recv_sem=r2.at[3 * w + k], device_id=(x, y, 1 - c),
                                             device_id_type=MESH).wait()

    return _split_wait(body, name, wholes, send2, recv2, after)


def _gather_small(small, after=()):
    def body(s_ref, o_ref, send_sems, recv_sems, local_sem):
        x, y, c = _position()
        mine = pltpu.make_async_copy(s_ref, o_ref.at[2 * x + y], local_sem)
        mine.start()
        sends = []
        for k, chip in enumerate(_chips(x, y)):
            cp = pltpu.make_async_remote_copy(src_ref=s_ref, dst_ref=o_ref.at[2 * x + y], send_sem=send_sems.at[k],
                                              recv_sem=recv_sems.at[k], device_id=(*chip, c), device_id_type=MESH)
            cp.start()
            sends.append(cp)
        for k, chip in enumerate(_chips(x, y)):
            pltpu.make_async_remote_copy(src_ref=s_ref, dst_ref=o_ref.at[2 * chip[0] + chip[1]], send_sem=send_sems.at[k],
                                         recv_sem=recv_sems.at[k], device_id=(*chip, c), device_id_type=MESH).wait_recv()
        for cp in sends:
            cp.wait_send()
        mine.wait()

    return _pcall(body, after=after, name="gather_small", in_specs=[HBM_SPEC], out_specs=HBM_SPEC,
                  out_shape=jax.ShapeDtypeStruct((4,) + small.shape, small.dtype),
                  scratch_shapes=[pltpu.SemaphoreType.DMA((3,)), pltpu.SemaphoreType.DMA((3,)),
                                  pltpu.SemaphoreType.DMA(())])(small)


def _scatter_copies(g_refs, land_refs, kinds, send_sems, recv_sems):
    me = _position()
    copies = []
    for k, rel in enumerate(RELATIONS):
        peer = _related(me, rel)
        for w, (g_ref, land_ref) in enumerate(zip(g_refs, land_refs)):
            copies.append(pltpu.make_async_remote_copy(
                src_ref=_window(g_ref, kinds[w], peer), dst_ref=land_ref.at[k],
                send_sem=send_sems.at[7 * w + k], recv_sem=recv_sems.at[7 * w + k], device_id=peer,
                device_id_type=MESH))
    return copies


def _scatter_start(grads, kinds, name):
    n = len(grads)
    lands = [lax.empty((N_DEV - 1,) + _half_shape(g.shape, kd), g.dtype) for g, kd in zip(grads, kinds)]

    def body(refs, send_sems, recv_sems):
        for cp in _scatter_copies(refs[:n], refs[n:], kinds, send_sems, recv_sems):
            cp.start()

    send, recv, thru, token = _split_start(body, name, list(grads) + lands, ((N_DEV - 1) * n,))
    return send, recv, thru[:n], thru[n:], token


def _scatter_wait(grads, lands, kinds, send, recv, after, name):
    n = len(grads)

    def body(refs, send_sems, recv_sems):
        for cp in _scatter_copies(refs[:n], refs[n:], kinds, send_sems, recv_sems):
            cp.wait()

    out = _split_wait(body, name, list(grads) + list(lands), send, recv, after)
    return out[:n], out[n:]


def _swap_start(halves, name):
    n = len(halves)
    lands = [lax.empty(h.shape, h.dtype) for h in halves]

    def body(refs, send_sems, recv_sems):
        x, y, c = _position()
        for w in range(n):
            pltpu.make_async_remote_copy(src_ref=refs[w], dst_ref=refs[n + w], send_sem=send_sems.at[w],
                                         recv_sem=recv_sems.at[w], device_id=(x, y, 1 - c), device_id_type=MESH).start()

    send, recv, thru, token = _split_start(body, name, list(halves) + lands, (n,))
    return send, recv, thru[:n], thru[n:], token


def _swap_wait(halves, lands, send, recv, after, name):
    n = len(halves)

    def body(refs, send_sems, recv_sems):
        x, y, c = _position()
        for w in range(n):
            pltpu.make_async_remote_copy(src_ref=refs[w], dst_ref=refs[n + w], send_sem=send_sems.at[w],
                                         recv_sem=recv_sems.at[w], device_id=(x, y, 1 - c), device_id_type=MESH).wait()

    out = _split_wait(body, name, list(halves) + list(lands), send, recv, after)
    return out[:n], out[n:]


def _allreduce_small(v, after=()):
    rows = v.shape[0]

    def body(v_ref, o_ref, recv_ref, send_sems, recv_sems):
        me = _position()
        recv_ref[_index(me)] = v_ref[...]
        sends = []
        for k, rel in enumerate(RELATIONS):
            peer = _related(me, rel)
            cp = pltpu.make_async_remote_copy(
                src_ref=v_ref, dst_ref=recv_ref.at[_index(me)],
                send_sem=send_sems.at[k], recv_sem=recv_sems.at[k], device_id=peer, device_id_type=MESH)
            cp.start()
            sends.append(cp)
        for k, rel in enumerate(RELATIONS):
            peer = _related(me, rel)
            pltpu.make_async_remote_copy(
                src_ref=v_ref, dst_ref=recv_ref.at[_index(peer)],
                send_sem=send_sems.at[k], recv_sem=recv_sems.at[k], device_id=peer, device_id_type=MESH).wait_recv()
        for cp in sends:
            cp.wait_send()
        acc = recv_ref[0]
        for k in range(1, N_DEV):
            acc = acc + recv_ref[k]
        o_ref[...] = acc

    return _pcall(body, after=after, name="allreduce_small", in_specs=[VMEM_SPEC], out_specs=VMEM_SPEC,
                  out_shape=jax.ShapeDtypeStruct((rows, 128), F32),
                  scratch_shapes=[pltpu.VMEM((N_DEV, rows, 128), F32), pltpu.SemaphoreType.DMA((7,)),
                                  pltpu.SemaphoreType.DMA((7,))],
                  compiler_params=pltpu.CompilerParams(vmem_limit_bytes=VMEM_LIMIT_BYTES))(v)


def _pack(arrays):
    flat = []
    for a in arrays:
        a = a.reshape(-1)
        flat.append(jnp.pad(a, (0, -a.shape[0] % 128)))
    flat = jnp.concatenate(flat)
    flat = jnp.pad(flat, (0, -flat.shape[0] % 1024))
    return flat.reshape(-1, 128)


def _unpack(packed, shapes):
    flat = packed.reshape(-1)
    out, at = [], 0
    for shp in shapes:
        size = 1
        for d in shp:
            size *= d
        out.append(flat[at:at + size].reshape(shp))
        at += size + (-size % 128)
    return out


WEIGHTS = ['l0_mix_norm_g', 'l0_w_in', 'l0_sc_conv_w', 'l0_w_out', 'l0_ffn_norm_g', 'l0_ffn_up', 'l0_ffn_conv_w',
           'l0_ffn_down', 'l1_mix_norm_g', 'l1_w_in', 'l1_fox_b_f', 'l1_sg_w', 'l1_sg_b', 'l1_sg_norm_g', 'l1_w_out',
           'l1_ffn_norm_g', 'l1_ffn_up', 'l1_ffn_conv_w', 'l1_ffn_down', 'final_norm_g']
BIG = {'l0_w_in': 'col', 'l0_w_out': 'row', 'l0_ffn_up': 'col', 'l0_ffn_down': 'row',
       'l1_w_in': 'maj', 'l1_w_out': 'row', 'l1_ffn_up': 'col', 'l1_ffn_down': 'row'}
GATHER_GROUPS = [['l0_w_in'], ['l0_w_out'], ['l0_ffn_up'], ['l0_ffn_down'], ['l1_w_in', 'l1_w_out'],
                 ['l1_ffn_up'], ['l1_ffn_down']]
CONV = ['l0_sc_conv_w', 'l0_ffn_conv_w', 'l1_ffn_conv_w']
SMALL = [n for n in WEIGHTS if n not in BIG]
IN_CD = 5 * HALF + N_HEADS


def _ffn_forward(x, g, get_up, behind_act, get_down, conv_w, tag):
    h = _rmsnorm_fwd(x, g, tag + "_norm")
    u = _matmul(h, get_up(h), "nn", F32, tag + "_up")
    f = _ffn_act_fwd(u, conv_w, tag + "_act", after=behind_act(u))
    w_down, tokens = get_down(f)
    return _matmul(f, w_down, "nn", F32, tag + "_down", res=x, after=tokens), (h, u, f)


def _ffn_backward(x, g, w_up, conv_w, w_down, saved, d_out, send_grads, tag):
    h, u, f = saved
    dw_down = _matmul(f, d_out, "tn", BF16, tag + "_dwdown")
    d_f = _matmul(d_out, w_down, "nt", F32, tag + "_df")
    du, dcw_gate, dcw_up = _ffn_act_bwd(u, conv_w, d_f, tag + "_dact")
    dw_up = _matmul(h, du, "tn", BF16, tag + "_dwup")
    dh = _matmul(du, w_up, "nt", F32, tag + "_dh", after=[send_grads(dw_up, dw_down)])
    dx, dg = _rmsnorm_bwd(x, g, dh, d_out, tag + "_dnorm")
    return dx, dg, jnp.concatenate([dcw_gate, dcw_up], axis=1)


def kernel(x, l0_mix_norm_g, l0_w_in, l0_sc_conv_w, l0_w_out, l0_ffn_norm_g, l0_ffn_up, l0_ffn_conv_w, l0_ffn_down, l1_mix_norm_g, l1_w_in, l1_fox_b_f, l1_sg_w, l1_sg_b, l1_sg_norm_g, l1_w_out, l1_ffn_norm_g, l1_ffn_up, l1_ffn_conv_w, l1_ffn_down, final_norm_g, loss_target, m_l0_mix_norm_g, m_l0_w_in, m_l0_sc_conv_w, m_l0_w_out, m_l0_ffn_norm_g, m_l0_ffn_up, m_l0_ffn_conv_w, m_l0_ffn_down, m_l1_mix_norm_g, m_l1_w_in, m_l1_fox_b_f, m_l1_sg_w, m_l1_sg_b, m_l1_sg_norm_g, m_l1_w_out, m_l1_ffn_norm_g, m_l1_ffn_up, m_l1_ffn_conv_w, m_l1_ffn_down, m_final_norm_g, v_l0_mix_norm_g, v_l0_w_in, v_l0_sc_conv_w, v_l0_w_out, v_l0_ffn_norm_g, v_l0_ffn_up, v_l0_ffn_conv_w, v_l0_ffn_down, v_l1_mix_norm_g, v_l1_w_in, v_l1_fox_b_f, v_l1_sg_w, v_l1_sg_b, v_l1_sg_norm_g, v_l1_w_out, v_l1_ffn_norm_g, v_l1_ffn_up, v_l1_ffn_conv_w, v_l1_ffn_down, v_final_norm_g):
    given = (l0_mix_norm_g, l0_w_in, l0_sc_conv_w, l0_w_out, l0_ffn_norm_g, l0_ffn_up, l0_ffn_conv_w, l0_ffn_down, l1_mix_norm_g, l1_w_in, l1_fox_b_f, l1_sg_w, l1_sg_b, l1_sg_norm_g, l1_w_out, l1_ffn_norm_g, l1_ffn_up, l1_ffn_conv_w, l1_ffn_down, final_norm_g)
    given_m = (m_l0_mix_norm_g, m_l0_w_in, m_l0_sc_conv_w, m_l0_w_out, m_l0_ffn_norm_g, m_l0_ffn_up, m_l0_ffn_conv_w, m_l0_ffn_down, m_l1_mix_norm_g, m_l1_w_in, m_l1_fox_b_f, m_l1_sg_w, m_l1_sg_b, m_l1_sg_norm_g, m_l1_w_out, m_l1_ffn_norm_g, m_l1_ffn_up, m_l1_ffn_conv_w, m_l1_ffn_down, m_final_norm_g)
    given_v = (v_l0_mix_norm_g, v_l0_w_in, v_l0_sc_conv_w, v_l0_w_out, v_l0_ffn_norm_g, v_l0_ffn_up, v_l0_ffn_conv_w, v_l0_ffn_down, v_l1_mix_norm_g, v_l1_w_in, v_l1_fox_b_f, v_l1_sg_w, v_l1_sg_b, v_l1_sg_norm_g, v_l1_w_out, v_l1_ffn_norm_g, v_l1_ffn_up, v_l1_ffn_conv_w, v_l1_ffn_down, v_final_norm_g)
    wt = dict(zip(WEIGHTS, given))
    mom = dict(zip(WEIGHTS, given_m))
    var = dict(zip(WEIGHTS, given_v))
    s = x.shape[1]
    t = ATT_BLOCK
    x0, target = x[0], loss_target[0]
    chip = 2 * lax.axis_index("x") + lax.axis_index("y")

    pos = jnp.stack([chip, lax.axis_index("c")]).astype(jnp.int32)
    big_names = list(BIG)

    conv_widths = [wt[n].shape[1] for n in CONV]
    conv_all = _gather_small(jnp.concatenate([wt[n] for n in CONV], axis=1))
    conv_full, at = {}, 0
    for n, cw in zip(CONV, conv_widths):
        conv_full[n] = jnp.transpose(conv_all[:, :, at:at + cw], (1, 0, 2)).reshape(3, 4 * cw)
        at += cw
    gathers, token = [], conv_all
    for gi, names in enumerate(GATHER_GROUPS):
        placed = [_place_shard(pos, wt[n], BIG[n], "place_" + n, [token]) for n in names]
        send, recv, thru, token = _gather_start(placed, [BIG[n] for n in names], "gather_start_%d" % gi, [token])
        gathers.append((send, recv, thru))
    full = {}

    def forward_gather(gi, after):
        send, recv, thru = gathers[gi]
        kinds = [BIG[n] for n in GATHER_GROUPS[gi]]
        gathers[gi] = _gather_forward(thru, kinds, send, recv, after, "gather_forward_%d" % gi)
        return gathers[gi][3]

    def finish_gather(gi, after):
        send, recv, thru, tok = gathers[gi]
        names = GATHER_GROUPS[gi]
        wholes = _gather_finish(thru, [BIG[n] for n in names], send, recv, tok if after is None else after,
                                "gather_finish_%d" % gi)
        full.update(zip(names, wholes))

    def vec(name):
        return wt[name].reshape(1, -1)

    h0 = _rmsnorm_fwd(x0, vec('l0_mix_norm_g'), "l0_mix_norm", after=[token])
    forward_gather(0, h0)
    finish_gather(0, None)
    p0 = _matmul(h0, full['l0_w_in'], "nn", F32, "l0_in")
    a_out, sb_carries = _sb_fwd(p0, "l0_sb", after=[forward_gather(1, p0)])
    finish_gather(1, a_out)
    b_out = _sc_fwd(p0, conv_full['l0_sc_conv_w'], "l0_sc")
    ab0 = jnp.concatenate([a_out.astype(BF16), b_out], axis=1)
    x1 = _matmul(ab0, full['l0_w_out'], "nn", F32, "l0_out", res=x0, after=[forward_gather(2, b_out)])

    def ffn_weights(up_group, next_group):
        def get_up(h):
            finish_gather(up_group, h)
            return full[GATHER_GROUPS[up_group][0]]

        def behind_act(u):
            return [forward_gather(up_group + 1, u)]

        def get_down(f):
            finish_gather(up_group + 1, f)
            return full[GATHER_GROUPS[up_group + 1][0]], ([forward_gather(next_group, f)] if next_group else ())

        return get_up, behind_act, get_down

    x2, ffn0_saved = _ffn_forward(x1, vec('l0_ffn_norm_g'), *ffn_weights(2, 4), conv_full['l0_ffn_conv_w'], "l0_ffn")
    h2 = _rmsnorm_fwd(x2, vec('l1_mix_norm_g'), "l1_mix_norm")
    finish_gather(4, h2)
    w_in1 = jnp.transpose(full['l1_w_in'], (1, 0, 2)).reshape(D_MODEL, IN_CD)
    w_in1_main = w_in1[:, :5 * HALF]
    w_in1_f = jnp.pad(w_in1[:, 5 * HALF:], ((0, 0), (0, 128 - N_HEADS)))
    p1 = _matmul(h2, w_in1_main, "nn", F32, "l1_in")
    f_logit = _matmul(h2, w_in1_f, "nn", F32, "l1_in_f")
    b_f = jnp.pad(wt['l1_fox_b_f'], (0, 128 - N_HEADS)).reshape(1, 128)
    c_heads = _fox_prep(f_logit, b_f, "l1_fox_prep")[:, :N_HEADS].T
    c_col = c_heads[:, :, None]
    c_row = c_heads.reshape(N_HEADS, s // t, 1, t)
    sg_bias = jnp.repeat(wt['l1_sg_b'].T, HEAD, axis=1)
    sg_gain = vec('l1_sg_norm_g')
    c_out = _sg_fwd(p1, wt['l1_sg_w'], sg_bias, sg_gain, "l1_sg")
    d_out, lse = _fox_fwd(p1, c_col, c_row, "l1_fox", after=[forward_gather(5, c_out)])
    cd1 = jnp.concatenate([c_out, d_out.astype(BF16)], axis=1)
    x3 = _matmul(cd1, full['l1_w_out'], "nn", F32, "l1_out", res=x2)
    x4, ffn1_saved = _ffn_forward(x3, vec('l1_ffn_norm_g'), *ffn_weights(5, None), conv_full['l1_ffn_conv_w'], "l1_ffn")
    dx4, dg_final, loss_part = _loss_head(x4, vec('final_norm_g'), target, "loss_head")

    grads = {'final_norm_g': dg_final}
    scatters = []

    def send_grads(names):
        def start(*group):
            send, recv, thru, lands, tok = _scatter_start(list(group), [BIG[n] for n in names],
                                                          "scatter_start_%d" % len(scatters))
            scatters.append((names, send, recv, thru, lands))
            return tok
        return start

    dx3, grads['l1_ffn_norm_g'], grads['l1_ffn_conv_w'] = _ffn_backward(
        x3, vec('l1_ffn_norm_g'), full['l1_ffn_up'], conv_full['l1_ffn_conv_w'], full['l1_ffn_down'], ffn1_saved, dx4,
        send_grads(['l1_ffn_up', 'l1_ffn_down']), "l1_ffn")
    dw_out1 = _matmul(cd1, dx3, "tn", BF16, "l1_dwout")
    d_cd = _matmul(dx3, full['l1_w_out'], "nt", F32, "l1_dcd")
    du, dv, grads['l1_sg_w'], db_sg, grads['l1_sg_norm_g'] = _sg_bwd(p1, wt['l1_sg_w'], sg_bias, sg_gain, d_cd, "l1_dsg")
    grads['l1_sg_b'] = db_sg[:, :N_HEADS].T
    dq, dk, dvv, dcol, drow = _fox_bwd(p1, c_col, c_row, lse, d_cd, d_out, "l1_dfox")
    pad8 = ((0, 0), (0, 128 - N_HEADS))
    d_f_logit, d_b_f = _fox_post(jnp.pad(drow[:, :, 0].T, pad8), jnp.pad(dcol[:, ::HEAD], pad8), f_logit, b_f,
                                 "l1_fox_post")
    grads['l1_fox_b_f'] = d_b_f[0, :N_HEADS]
    dp1 = jnp.concatenate([a.astype(BF16) for a in (du, dv, dq, dk, dvv)], axis=1)
    dw_main = _matmul(h2, dp1, "tn", BF16, "l1_dwin")
    dw_f = _matmul(h2, d_f_logit, "tn", BF16, "l1_dwin_f")
    dw_in1 = jnp.concatenate([dw_main, dw_f[:, :N_HEADS]], axis=1)
    dw_in1 = jnp.transpose(dw_in1.reshape(D_MODEL, 4, IN_CD // 4), (1, 0, 2))
    dh2 = _matmul(dp1, w_in1_main, "nt", F32, "l1_dh", after=[send_grads(['l1_w_out', 'l1_w_in'])(dw_out1, dw_in1)])
    dh2 = _matmul(d_f_logit, w_in1_f, "nt", F32, "l1_dh_f", res=dh2)
    dx2, grads['l1_mix_norm_g'] = _rmsnorm_bwd(x2, vec('l1_mix_norm_g'), dh2, dx3, "l1_dmix_norm")
    dx1, grads['l0_ffn_norm_g'], grads['l0_ffn_conv_w'] = _ffn_backward(
        x1, vec('l0_ffn_norm_g'), full['l0_ffn_up'], conv_full['l0_ffn_conv_w'], full['l0_ffn_down'], ffn0_saved, dx2,
        send_grads(['l0_ffn_up', 'l0_ffn_down']), "l0_ffn")
    dw_out0 = _matmul(ab0, dx1, "tn", BF16, "l0_dwout")
    d_ab = _matmul(dx1, full['l0_w_out'], "nt", F32, "l0_dab", after=[send_grads(['l0_w_out'])(dw_out0)])
    dq0, dk0, dv0 = _sb_bwd(p0, d_ab, sb_carries, "l0_dsb")
    dgb, dgc, dhin, grads['l0_sc_conv_w'] = _sc_bwd(p0, conv_full['l0_sc_conv_w'], d_ab, "l0_dsc")
    dp0 = jnp.concatenate([a.astype(BF16) for a in (dq0, dk0, dv0, dgb, dgc, dhin)], axis=1)
    dw_in0 = _matmul(h0, dp0, "tn", BF16, "l0_dwin")
    dh0 = _matmul(dp0, full['l0_w_in'], "nt", F32, "l0_dh", after=[send_grads(['l0_w_in'])(dw_in0)])
    dx0, grads['l0_mix_norm_g'] = _rmsnorm_bwd(x0, vec('l0_mix_norm_g'), dh0, dx1, "l0_dmix_norm")

    shard_grads, delta, new_m, new_v, swaps = {}, {}, {}, {}, {}

    def reduce_group(gi, after):
        names, send, recv, thru, lands = scatters[gi]
        kinds = [BIG[n] for n in names]
        g_thru, landed = _scatter_wait(thru, lands, kinds, send, recv, after, "scatter_wait_%d" % gi)
        halves = [_sum_partials(pos, g, ld, kd, "sum_" + n) for n, g, ld, kd in zip(names, g_thru, landed, kinds)]
        s_send, s_recv, h_thru, s_lands, tok = _swap_start(halves, "swap_start_%d" % gi)
        swaps[gi] = (names, s_send, s_recv, h_thru, s_lands)
        return tok

    def update_group(gi, after):
        names, s_send, s_recv, h_thru, s_lands = swaps[gi]
        mine, theirs = _swap_wait(h_thru, s_lands, s_send, s_recv, after, "swap_wait_%d" % gi)
        for n, gm, gs in zip(names, mine, theirs):
            shard_grads[n], delta[n], new_m[n], new_v[n] = _adamw_shard(pos, wt[n], gm, gs, mom[n], var[n], "adamw_" + n)
        return [delta[n] for n in names]

    after = reduce_group(1, reduce_group(0, dx0))
    after = update_group(1, update_group(0, after))
    after = update_group(2, reduce_group(2, after))
    after = update_group(3, reduce_group(3, after))
    after = reduce_group(4, after)
    small_shapes = [conv_full[n].shape if n in CONV else wt[n].shape for n in SMALL] + [loss_part.shape]
    small_all = _allreduce_small(_pack([grads[n] for n in SMALL] + [loss_part]), [after])
    small_sums = _unpack(small_all, small_shapes)
    loss = small_sums[-1][0, 0]
    for n, g in zip(SMALL, small_sums):
        shard_grads[n] = lax.dynamic_slice_in_dim(g, chip * wt[n].shape[1], wt[n].shape[1], axis=1) if n in CONV else g
    update_group(4, small_all)
    shapes = [wt[n].shape for n in SMALL]
    packed = _adamw(_pack([wt[n] for n in SMALL]), _pack([shard_grads[n] for n in SMALL]),
                    _pack([mom[n] for n in SMALL]), _pack([var[n] for n in SMALL]), "adamw_small")
    for out, pk in zip((delta, new_m, new_v), packed):
        out.update(zip(SMALL, _unpack(pk, shapes)))

    return (loss, dx0[None], *[shard_grads[n] for n in WEIGHTS], *[delta[n] for n in WEIGHTS],
            *[new_m[n] for n in WEIGHTS], *[new_v[n] for n in WEIGHTS])
```

```python
import jax
import jax.numpy as jnp
from jax import lax
from jax.experimental import pallas as pl
from jax.experimental.pallas import tpu as pltpu

F32 = jnp.float32
BF16 = jnp.bfloat16

D_MODEL = 2048
HEAD = 128
N_HEADS = 8
HALF = N_HEADS * HEAD
D_FF = 5632
EPS = 1e-6
ATT_SCALE = HEAD ** -0.5
ATT_BLOCK = 512
NEG = -1e30

ADAM_LR = 0.001
ADAM_B1 = 0.9
ADAM_B2 = 0.999
ADAM_EPS = 1e-08
ADAM_WD = 0.01
ADAM_STEP = 10

VMEM_LIMIT_BYTES = 56 * 1024 * 1024
MESH = pl.DeviceIdType.MESH
HBM_SPEC = pl.BlockSpec(memory_space=pltpu.HBM)
VMEM_SPEC = pl.BlockSpec(memory_space=pltpu.VMEM)


def _pcall(body, after=(), **kw):
    if not after:
        return pl.pallas_call(body, **kw)
    n_in, n_after, inner = len(kw["in_specs"]), len(after), body
    kw["in_specs"] = list(kw["in_specs"]) + [pl.BlockSpec(memory_space=pl.ANY)] * n_after

    def body(*refs):
        inner(*refs[:n_in], *refs[n_in + n_after:])

    call = pl.pallas_call(body, **kw)
    return lambda *args: call(*args, *after)


def _params(*semantics):
    return pltpu.CompilerParams(dimension_semantics=semantics, vmem_limit_bytes=VMEM_LIMIT_BYTES)


def _pick(n, cap):
    best = None
    for t in range(128, min(n, cap) + 1, 128):
        if n % t == 0:
            best = t
    return n if best is None else best


def _dot(a, b, dims):
    return lax.dot_general(a, b, (dims, ((), ())), preferred_element_type=F32)


def _dot_nn(a, b):
    return _dot(a, b, ((1,), (0,)))


def _dot_nt(a, b):
    return _dot(a, b, ((1,), (1,)))


def _dot_tn(a, b):
    return _dot(a, b, ((0,), (0,)))


def _split3(x):
    hi = x.astype(BF16)
    r1 = x - hi.astype(F32)
    mid = r1.astype(BF16)
    lo = (r1 - mid.astype(F32)).astype(BF16)
    return hi, mid, lo


def _log_sigmoid(z):
    return jnp.minimum(z, 0.0) - jnp.log1p(jnp.exp(-jnp.abs(z)))


_GELU_K = 0.7978845608028654


def _gelu(x):
    return 0.5 * x * (1.0 + jnp.tanh(_GELU_K * (x + 0.044715 * x * x * x)))


def _gelu_grad(x):
    t = jnp.tanh(_GELU_K * (x + 0.044715 * x * x * x))
    return 0.5 * (1.0 + t) + 0.5 * x * (1.0 - t * t) * _GELU_K * (1.0 + 3.0 * 0.044715 * x * x)


SUBLANES = 8


def _shift_down(x, k):
    rolled = pltpu.roll(x, k, axis=0)
    head = rolled[:SUBLANES]
    head = jnp.where(lax.broadcasted_iota(jnp.int32, head.shape, 0) >= k, head, 0.0)
    return jnp.concatenate([head, rolled[SUBLANES:]], axis=0)


def _shift_up(x, k):
    n = x.shape[0]
    rolled = pltpu.roll(x, n - k, axis=0)
    tail = rolled[n - SUBLANES:]
    tail = jnp.where(lax.broadcasted_iota(jnp.int32, tail.shape, 0) < SUBLANES - k, tail, 0.0)
    return jnp.concatenate([rolled[:n - SUBLANES], tail], axis=0)


def _conv3(s, w, shifted=None):
    s1, s2 = shifted if shifted else (_shift_down(s, 1), _shift_down(s, 2))
    return w[0:1, :] * s2 + w[1:2, :] * s1 + w[2:3, :] * s


def _conv3_transpose(d, w):
    return w[2:3, :] * d + w[1:2, :] * _shift_up(d, 1) + w[0:1, :] * _shift_up(d, 2)


def _conv3_wgrad(d, s, shifted, dw_ref):
    s1, s2 = shifted
    dw_ref[0:1, :] = jnp.sum(d * s2, axis=0, keepdims=True)
    dw_ref[1:2, :] = jnp.sum(d * s1, axis=0, keepdims=True)
    dw_ref[2:3, :] = jnp.sum(d * s, axis=0, keepdims=True)


MM_TILE_M, MM_TILE_N, MM_TILE_K = 1408, 512, 5632


def _matmul(a, b, mode, out_dtype, name, res=None, after=()):
    a_parts = a.shape[0] if a.ndim == 3 else 1
    b_parts = b.shape[0] if b.ndim == 3 else 1
    a_shape = (a.shape[1], a_parts * a.shape[2]) if a.ndim == 3 else a.shape
    b_shape = (b.shape[1], b_parts * b.shape[2]) if b.ndim == 3 else b.shape
    assert (a_parts == 1 or mode != "tn") and (b_parts == 1 or mode == "tn")
    if mode == "nn":
        (m, k), (k2, n) = a_shape, b_shape
    elif mode == "nt":
        (m, k), (n, k2) = a_shape, b_shape
    else:
        (k, m), (k2, n) = a_shape, b_shape
    assert k == k2, (a.shape, b.shape, mode)
    tm, tn, tk = _pick(m, MM_TILE_M), _pick(n // b_parts, MM_TILE_N), _pick(k // a_parts, MM_TILE_K)
    nk = k // tk
    if mode == "tn":
        a_spec = pl.BlockSpec((tk, tm), lambda i, j, kk: (kk, i))
    elif a_parts > 1:
        per = nk // a_parts
        a_spec = pl.BlockSpec((None, tm, tk), lambda i, j, kk: (kk // per, i, kk % per))
    else:
        a_spec = pl.BlockSpec((tm, tk), lambda i, j, kk: (i, kk))
    if mode == "nt":
        b_spec = pl.BlockSpec((tn, tk), lambda i, j, kk: (j, kk))
    elif b_parts > 1:
        per = n // b_parts // tn
        b_spec = pl.BlockSpec((None, tk, tn), lambda i, j, kk: (j // per, kk, j % per))
    else:
        b_spec = pl.BlockSpec((tk, tn), lambda i, j, kk: (kk, j))
    o_spec = pl.BlockSpec((tm, tn), lambda i, j, kk: (i, j))
    dims = {"nn": ((1,), (0,)), "nt": ((1,), (1,)), "tn": ((0,), (0,))}[mode]
    has_res = res is not None

    def body(*refs):
        a_ref, b_ref = refs[0], refs[1]
        r_ref = refs[2] if has_res else None
        o_ref = refs[3] if has_res else refs[2]
        part = _dot(a_ref[...].astype(BF16), b_ref[...].astype(BF16), dims)

        def finish(total):
            if has_res:
                total = total + r_ref[...]
            o_ref[...] = total.astype(out_dtype)

        if nk == 1:
            finish(part)
        else:
            acc_ref = refs[-1]
            kk = pl.program_id(2)

            @pl.when(kk == 0)
            def _():
                acc_ref[...] = part

            @pl.when(kk > 0)
            def _():
                acc_ref[...] += part

            @pl.when(kk == nk - 1)
            def _():
                finish(acc_ref[...])

    in_specs = [a_spec, b_spec] + ([o_spec] if has_res else [])
    args = (a, b) + ((res,) if has_res else ())
    return _pcall(
        body, after=after, name=name, grid=(m // tm, n // tn, nk),
        in_specs=in_specs, out_specs=o_spec,
        out_shape=jax.ShapeDtypeStruct((m, n), out_dtype),
        scratch_shapes=[pltpu.VMEM((tm, tn), F32)] if nk > 1 else [],
        compiler_params=_params("parallel", "parallel", "arbitrary"),
    )(*args)


ROW_TILE = 256


def _rmsnorm_fwd(x, g, name, after=()):
    s, d = x.shape

    def body(x_ref, g_ref, o_ref):
        xf = x_ref[...]
        r = lax.rsqrt(jnp.mean(xf * xf, axis=-1, keepdims=True) + EPS)
        o_ref[...] = (xf * r * g_ref[...]).astype(BF16)

    row = pl.BlockSpec((ROW_TILE, d), lambda i: (i, 0))
    vec = pl.BlockSpec((1, d), lambda i: (0, 0))
    return _pcall(body, after=after, name=name, grid=(s // ROW_TILE,), in_specs=[row, vec], out_specs=row,
                  out_shape=jax.ShapeDtypeStruct((s, d), BF16), compiler_params=_params("parallel"))(x, g)


def _rmsnorm_bwd(x, g, dh, dres, name):
    s, d = x.shape

    def body(x_ref, g_ref, dh_ref, dres_ref, dx_ref, dg_ref):
        xf = x_ref[...]
        r = lax.rsqrt(jnp.mean(xf * xf, axis=-1, keepdims=True) + EPS)
        xhat = xf * r
        dh_v = dh_ref[...]
        dxh = dh_v * g_ref[...]
        proj = jnp.mean(dxh * xhat, axis=-1, keepdims=True)
        dx_ref[...] = dres_ref[...] + r * (dxh - xhat * proj)
        part = jnp.sum(dh_v * xhat, axis=0, keepdims=True)

        @pl.when(pl.program_id(0) == 0)
        def _():
            dg_ref[...] = part

        @pl.when(pl.program_id(0) > 0)
        def _():
            dg_ref[...] += part

    row = pl.BlockSpec((ROW_TILE, d), lambda i: (i, 0))
    vec = pl.BlockSpec((1, d), lambda i: (0, 0))
    return _pcall(body, name=name, grid=(s // ROW_TILE,), in_specs=[row, vec, row, row], out_specs=[row, vec],
                  out_shape=[jax.ShapeDtypeStruct((s, d), F32), jax.ShapeDtypeStruct((1, d), F32)],
                  compiler_params=_params("arbitrary"))(x, g, dh, dres)


def _loss_head(x, g, target, name):
    s, d = x.shape

    def body(x_ref, g_ref, t_ref, dx_ref, dg_ref, loss_ref):
        xf = x_ref[...]
        r = lax.rsqrt(jnp.mean(xf * xf, axis=-1, keepdims=True) + EPS)
        xhat = xf * r
        gv = g_ref[...]
        err = xhat * gv - t_ref[...]
        dy = err * (1.0 / d)
        dxh = dy * gv
        proj = jnp.mean(dxh * xhat, axis=-1, keepdims=True)
        dx_ref[...] = r * (dxh - xhat * proj)
        dg_part = jnp.sum(dy * xhat, axis=0, keepdims=True)
        row_loss = jnp.sum(err * err, axis=-1, keepdims=True) * (0.5 / d)
        loss_part = jnp.broadcast_to(jnp.sum(row_loss, axis=0, keepdims=True), (1, 128))

        @pl.when(pl.program_id(0) == 0)
        def _():
            dg_ref[...] = dg_part
            loss_ref[...] = loss_part

        @pl.when(pl.program_id(0) > 0)
        def _():
            dg_ref[...] += dg_part
            loss_ref[...] += loss_part

    row = pl.BlockSpec((ROW_TILE, d), lambda i: (i, 0))
    vec = pl.BlockSpec((1, d), lambda i: (0, 0))
    one = pl.BlockSpec((1, 128), lambda i: (0, 0))
    return _pcall(body, name=name, grid=(s // ROW_TILE,), in_specs=[row, vec, row], out_specs=[row, vec, one],
                  out_shape=[jax.ShapeDtypeStruct((s, d), F32), jax.ShapeDtypeStruct((1, d), F32),
                             jax.ShapeDtypeStruct((1, 128), F32)],
                  compiler_params=_params("arbitrary"))(x, g, target)


HEADS_PER_STEP = 2
GROUP_W = HEADS_PER_STEP * HEAD
N_GROUPS = N_HEADS // HEADS_PER_STEP


def _head_cols(h):
    return slice(h * HEAD, (h + 1) * HEAD)


def _head_specs(s, col0):
    t = ATT_BLOCK
    g0 = [c // HEADS_PER_STEP for c in col0]
    qspec = pl.BlockSpec((t, GROUP_W), lambda g, i: (i, g0[0] + g))
    kspec = pl.BlockSpec((s, GROUP_W), lambda g, i: (0, g0[1] + g))
    vspec = pl.BlockSpec((s, GROUP_W), lambda g, i: (0, g0[2] + g))
    return qspec, kspec, vspec


TRI = 256


def _order_matrix(later):
    r, c = lax.broadcasted_iota(jnp.int32, (TRI, TRI), 0), lax.broadcasted_iota(jnp.int32, (TRI, TRI), 1)
    return (r > c if later else r < c).astype(BF16)


def _exact_dot(x, m, later):
    parts = [x[:, c:c + TRI] for c in range(0, x.shape[1], TRI)]
    totals = [jnp.sum(p, axis=1, keepdims=True) for p in parts] if len(parts) > 1 else None
    out = []
    for j, p in enumerate(parts):
        hi = p.astype(BF16)
        lo = (p - hi.astype(F32)).astype(BF16)
        acc = _dot_nn(hi, m) + _dot_nn(lo, m)
        for other in (range(j + 1, len(parts)) if later else range(j)):
            acc = acc + totals[other]
        out.append(acc)
    return out[0] if len(out) == 1 else jnp.concatenate(out, axis=1)


def _sb_block(q, kblk, carry_l, u, diagonal):
    t = ATT_BLOCK
    z = _dot_nt(q, kblk) * ATT_SCALE
    sp = jnp.maximum(z, 0.0) + jnp.log(1.0 + jnp.exp(-jnp.abs(z)))
    if not diagonal:
        l = -sp
        return z, None, l, jnp.exp(z + l + _exact_dot(l, u, True) + carry_l)
    mask = lax.broadcasted_iota(jnp.int32, (t, t), 1) < lax.broadcasted_iota(jnp.int32, (t, t), 0)
    l = jnp.where(mask, -sp, 0.0)
    a = jnp.where(mask, jnp.exp(z - sp + _exact_dot(l, u, True) + carry_l), 0.0)
    return z, mask, l, a


def _sb_carry_spec(s):
    t = ATT_BLOCK
    return pl.BlockSpec((HEADS_PER_STEP, None, s // t, t, 1), lambda g, i: (g, i, 0, 0, 0))


def _sb_fwd(p, name, after=()):
    s = p.shape[0]
    t = ATT_BLOCK
    nb = s // t

    def body(q_ref, k_ref, v_ref, o_ref, cl_ref):
        i = pl.program_id(1)
        heads = range(HEADS_PER_STEP)
        q = [q_ref[:, _head_cols(h)].astype(BF16) for h in heads]
        u = _order_matrix(True)
        cl_ref[...] = jnp.zeros_like(cl_ref)

        def tile(kb, carry, diagonal):
            ks = pl.multiple_of(kb * t, t)
            out = []
            for h in heads:
                acc, carry_l = carry[h]
                kblk = k_ref[pl.ds(ks, t), _head_cols(h)].astype(BF16)
                vblk = v_ref[pl.ds(ks, t), _head_cols(h)].astype(BF16)
                cl_ref[h, kb] = carry_l
                _, _, l, a = _sb_block(q[h], kblk, carry_l, u, diagonal)
                out.append((acc + _dot_nn(a.astype(BF16), vblk), carry_l + jnp.sum(l, axis=1, keepdims=True)))
            return tuple(out)

        carry = tile(i, tuple((jnp.zeros((t, HEAD), F32), jnp.zeros((t, 1), F32)) for _ in heads), True)
        carry = lax.fori_loop(0, i, lambda n, c: tile(i - 1 - n, c, False), carry)
        for h in heads:
            o_ref[:, _head_cols(h)] = carry[h][0]

    qspec, kspec, vspec = _head_specs(s, (0, N_HEADS, 2 * N_HEADS))
    ospec = pl.BlockSpec((t, GROUP_W), lambda g, i: (i, g))
    return _pcall(body, after=after, name=name, grid=(N_GROUPS, nb), in_specs=[qspec, kspec, vspec],
                  out_specs=[ospec, _sb_carry_spec(s)],
                  out_shape=[jax.ShapeDtypeStruct((s, HALF), F32), jax.ShapeDtypeStruct((N_HEADS, nb, nb, t, 1), F32)],
                  compiler_params=_params("parallel", "parallel"))(p, p, p)


def _sb_bwd(p, d_ab, carries, name):
    s = p.shape[0]
    t = ATT_BLOCK

    def body(q_ref, k_ref, v_ref, do_ref, cl_ref, dq_ref, dk_ref, dv_ref):
        i = pl.program_id(1)

        @pl.when(i == 0)
        def _():
            dk_ref[...] = jnp.zeros_like(dk_ref)
            dv_ref[...] = jnp.zeros_like(dv_ref)

        heads = range(HEADS_PER_STEP)
        q = [q_ref[:, _head_cols(h)].astype(BF16) for h in heads]
        do = [do_ref[:, _head_cols(h)].astype(BF16) for h in heads]
        u = _order_matrix(True)
        lower = _order_matrix(False)

        def tile(kb, carry, diagonal):
            ks = pl.multiple_of(kb * t, t)
            out = []
            for h in heads:
                dq, carry_g = carry[h]
                kblk = k_ref[pl.ds(ks, t), _head_cols(h)].astype(BF16)
                vblk = v_ref[pl.ds(ks, t), _head_cols(h)].astype(BF16)
                z, mask, _, a = _sb_block(q[h], kblk, cl_ref[h, kb], u, diagonal)
                g = a * _dot_nt(do[h], vblk)
                earlier_g = _exact_dot(g, lower, False) + carry_g
                sig = jax.nn.sigmoid(z)
                dz = g * (1.0 - sig) - sig * earlier_g
                if diagonal:
                    dz = jnp.where(mask, dz, 0.0)
                dz = dz.astype(BF16)
                dv_ref[pl.ds(ks, t), _head_cols(h)] += _dot_tn(a.astype(BF16), do[h])
                dk_ref[pl.ds(ks, t), _head_cols(h)] += _dot_tn(dz, q[h]) * ATT_SCALE
                out.append((dq + _dot_nn(dz, kblk) * ATT_SCALE, carry_g + jnp.sum(g, axis=1, keepdims=True)))
            return tuple(out)

        init = tuple((jnp.zeros((t, HEAD), F32), jnp.zeros((t, 1), F32)) for _ in heads)
        carry = tile(i, lax.fori_loop(0, i, lambda kb, c: tile(kb, c, False), init), True)
        for h in heads:
            dq_ref[:, _head_cols(h)] = carry[h][0]

    qspec, kspec, vspec = _head_specs(s, (0, N_HEADS, 2 * N_HEADS))
    blk = pl.BlockSpec((t, GROUP_W), lambda g, i: (i, g))
    whole = pl.BlockSpec((s, GROUP_W), lambda g, i: (0, g))
    shape = jax.ShapeDtypeStruct((s, HALF), F32)
    return _pcall(body, name=name, grid=(N_GROUPS, s // t), in_specs=[qspec, kspec, vspec, blk, _sb_carry_spec(s)],
                  out_specs=[blk, whole, whole], out_shape=[shape, shape, shape],
                  compiler_params=_params("parallel", "arbitrary"))(p, p, p, d_ab, carries)


COL_TILE = 256


def _sc_fwd(p, w, name):
    s = p.shape[0]
    nb = HALF // COL_TILE

    def body(gb_ref, gc_ref, h_ref, w_ref, o_ref):
        conv = _conv3(gc_ref[...] * h_ref[...], w_ref[...])
        o_ref[...] = (gb_ref[...] * conv).astype(BF16)

    def col(k):
        return pl.BlockSpec((s, COL_TILE), lambda j: (0, k * nb + j))

    wspec = pl.BlockSpec((3, COL_TILE), lambda j: (0, j))
    return _pcall(body, name=name, grid=(nb,), in_specs=[col(3), col(4), col(5), wspec], out_specs=col(0),
                  out_shape=jax.ShapeDtypeStruct((s, HALF), BF16), compiler_params=_params("parallel"))(p, p, p, w)


def _sc_bwd(p, w, d_ab, name):
    s = p.shape[0]
    nb = HALF // COL_TILE

    def body(gb_ref, gc_ref, h_ref, w_ref, d_ref, dgb_ref, dgc_ref, dh_ref, dw_ref):
        gc, hin, wv, d = gc_ref[...], h_ref[...], w_ref[...], d_ref[...]
        sig = gc * hin
        shifted = (_shift_down(sig, 1), _shift_down(sig, 2))
        dgb_ref[...] = d * _conv3(sig, wv, shifted)
        dconv = d * gb_ref[...]
        _conv3_wgrad(dconv, sig, shifted, dw_ref)
        dsig = _conv3_transpose(dconv, wv)
        dgc_ref[...] = dsig * hin
        dh_ref[...] = dsig * gc

    def col(k):
        return pl.BlockSpec((s, COL_TILE), lambda j: (0, k * nb + j))

    wspec = pl.BlockSpec((3, COL_TILE), lambda j: (0, j))
    act = jax.ShapeDtypeStruct((s, HALF), F32)
    return _pcall(body, name=name, grid=(nb,), in_specs=[col(3), col(4), col(5), wspec, col(1)],
                  out_specs=[col(0), col(0), col(0), wspec],
                  out_shape=[act, act, act, jax.ShapeDtypeStruct((3, HALF), F32)],
                  compiler_params=_params("parallel"))(p, p, p, w, d_ab)


def _ffn_act_fwd(u, w, name, after=()):
    s = u.shape[0]
    nb = D_FF // COL_TILE

    def body(ug_ref, uu_ref, wg_ref, wu_ref, o_ref):
        gate = _conv3(ug_ref[...], wg_ref[...])
        up = _conv3(uu_ref[...], wu_ref[...])
        o_ref[...] = (gate * jax.nn.sigmoid(gate) * up).astype(BF16)

    def col(k):
        return pl.BlockSpec((s, COL_TILE), lambda j: (0, k * nb + j))

    def wcol(k):
        return pl.BlockSpec((3, COL_TILE), lambda j: (0, k * nb + j))

    return _pcall(body, after=after, name=name, grid=(nb,), in_specs=[col(0), col(1), wcol(0), wcol(1)], out_specs=col(0),
                  out_shape=jax.ShapeDtypeStruct((s, D_FF), BF16),
                  compiler_params=_params("parallel"))(u, u, w, w)


def _ffn_act_bwd(u, w, d_f, name):
    s = u.shape[0]
    nb = D_FF // COL_TILE

    def body(ug_ref, uu_ref, wg_ref, wu_ref, d_ref, du_ref, dwg_ref, dwu_ref):
        ug, uu, wg, wu, d = ug_ref[...], uu_ref[...], wg_ref[...], wu_ref[...], d_ref[...]
        ug_shifted = (_shift_down(ug, 1), _shift_down(ug, 2))
        uu_shifted = (_shift_down(uu, 1), _shift_down(uu, 2))
        gate = _conv3(ug, wg, ug_shifted)
        up = _conv3(uu, wu, uu_shifted)
        sig = jax.nn.sigmoid(gate)
        d_up = d * gate * sig
        d_gate = d * up * sig * (1.0 + gate * (1.0 - sig))
        _conv3_wgrad(d_gate, ug, ug_shifted, dwg_ref)
        _conv3_wgrad(d_up, uu, uu_shifted, dwu_ref)
        du_ref[0] = _conv3_transpose(d_gate, wg).astype(BF16)
        du_ref[1] = _conv3_transpose(d_up, wu).astype(BF16)

    def col(k):
        return pl.BlockSpec((s, COL_TILE), lambda j: (0, k * nb + j))

    def wcol(k):
        return pl.BlockSpec((3, COL_TILE), lambda j: (0, k * nb + j))

    both = pl.BlockSpec((2, s, COL_TILE), lambda j: (0, 0, j))
    wsh = jax.ShapeDtypeStruct((3, D_FF), F32)
    return _pcall(body, name=name, grid=(nb,), in_specs=[col(0), col(1), wcol(0), wcol(1), col(0)],
                  out_specs=[both, wcol(0), wcol(0)], out_shape=[jax.ShapeDtypeStruct((2, s, D_FF), BF16), wsh, wsh],
                  compiler_params=_params("parallel"))(u, u, w, w, d_f)


def _sg_common(u, v, g, w_ref, bias, mixed_ref):
    rows = u.shape[0]
    gu = _gelu(u)
    gv = _gelu(v)
    xc = gv - jnp.mean(gv, axis=-1, keepdims=True)
    rstd = lax.rsqrt(jnp.mean(xc * xc, axis=-1, keepdims=True) + EPS)
    xhat = xc * rstd
    vn = xhat * g
    tril = lax.broadcasted_iota(jnp.int32, (HEAD, HEAD), 0) >= lax.broadcasted_iota(jnp.int32, (HEAD, HEAD), 1)
    wts = [jnp.where(tril, w_ref[grp], 0.0).astype(BF16) for grp in range(N_HEADS)]
    for n in range(rows // HEAD):
        for grp in range(N_HEADS):
            blk = vn[n * HEAD:(n + 1) * HEAD, grp * HEAD:(grp + 1) * HEAD].astype(BF16)
            mixed_ref[n * HEAD:(n + 1) * HEAD, grp * HEAD:(grp + 1) * HEAD] = _dot_nn(wts[grp], blk)
    mixed = mixed_ref[...] + jnp.concatenate([bias] * (rows // HEAD), axis=0)
    return gu, xhat, rstd, vn, mixed, wts, tril


def _sg_fwd(p, sg_w, bias, g, name):
    s = p.shape[0]

    def body(u_ref, v_ref, w_ref, b_ref, g_ref, o_ref, mixed_ref):
        gu, _, _, _, mixed, _, _ = _sg_common(u_ref[...], v_ref[...], g_ref[...], w_ref, b_ref[...], mixed_ref)
        o_ref[...] = (gu * mixed).astype(BF16)

    def half(k):
        return pl.BlockSpec((ROW_TILE, HALF), lambda i: (i, k))

    wspec = pl.BlockSpec((N_HEADS, HEAD, HEAD), lambda i: (0, 0, 0))
    bspec = pl.BlockSpec((HEAD, HALF), lambda i: (0, 0))
    gspec = pl.BlockSpec((1, HALF), lambda i: (0, 0))
    return _pcall(body, name=name, grid=(s // ROW_TILE,), in_specs=[half(0), half(1), wspec, bspec, gspec],
                  out_specs=half(0), out_shape=jax.ShapeDtypeStruct((s, HALF), BF16),
                  scratch_shapes=[pltpu.VMEM((ROW_TILE, HALF), F32)],
                  compiler_params=_params("parallel"))(p, p, sg_w, bias, g)


def _sg_bwd(p, sg_w, bias, g, d_cd, name):
    s = p.shape[0]
    nsteps = s // ROW_TILE

    def body(u_ref, v_ref, w_ref, b_ref, g_ref, d_ref, du_ref, dv_ref, dw_ref, db_ref, dg_ref,
             mixed_ref, dvn_ref, dbias_ref):
        i = pl.program_id(0)
        u, v, gain, d = u_ref[...], v_ref[...], g_ref[...], d_ref[...]
        gu, xhat, rstd, vn, mixed, wts, tril = _sg_common(u, v, gain, w_ref, b_ref[...], mixed_ref)

        @pl.when(i == 0)
        def _():
            dw_ref[...] = jnp.zeros_like(dw_ref)
            dg_ref[...] = jnp.zeros_like(dg_ref)
            dbias_ref[...] = jnp.zeros_like(dbias_ref)

        du_ref[...] = d * mixed * _gelu_grad(u)
        dm = d * gu
        for n in range(ROW_TILE // HEAD):
            rs = slice(n * HEAD, (n + 1) * HEAD)
            dbias_ref[...] += dm[rs, :]
            for grp in range(N_HEADS):
                cs = slice(grp * HEAD, (grp + 1) * HEAD)
                dm_blk = dm[rs, cs].astype(BF16)
                dw_ref[grp] += jnp.where(tril, _dot_nt(dm_blk, vn[rs, cs].astype(BF16)), 0.0)
                dvn_ref[rs, cs] = _dot_tn(wts[grp], dm_blk)
        dvn = dvn_ref[...]
        dg_ref[...] += jnp.sum(dvn * xhat, axis=0, keepdims=True)
        dxh = dvn * gain
        d_gv = rstd * (dxh - jnp.mean(dxh, axis=-1, keepdims=True) - xhat * jnp.mean(dxh * xhat, axis=-1, keepdims=True))
        dv_ref[...] = d_gv * _gelu_grad(v)

        @pl.when(i == nsteps - 1)
        def _():
            lane = lax.broadcasted_iota(jnp.int32, (HEAD, HEAD), 1)
            out = jnp.zeros((HEAD, HEAD), F32)
            for grp in range(N_HEADS):
                tot = jnp.sum(dbias_ref[:, grp * HEAD:(grp + 1) * HEAD], axis=1, keepdims=True)
                out = out + jnp.where(lane == grp, tot, 0.0)
            db_ref[...] = out

    def half(k):
        return pl.BlockSpec((ROW_TILE, HALF), lambda i: (i, k))

    wspec = pl.BlockSpec((N_HEADS, HEAD, HEAD), lambda i: (0, 0, 0))
    bspec = pl.BlockSpec((HEAD, HALF), lambda i: (0, 0))
    gspec = pl.BlockSpec((1, HALF), lambda i: (0, 0))
    dbspec = pl.BlockSpec((HEAD, HEAD), lambda i: (0, 0))
    act = jax.ShapeDtypeStruct((s, HALF), F32)
    return _pcall(body, name=name, grid=(nsteps,), in_specs=[half(0), half(1), wspec, bspec, gspec, half(0)],
                  out_specs=[half(0), half(0), wspec, dbspec, gspec],
                  out_shape=[act, act, jax.ShapeDtypeStruct((N_HEADS, HEAD, HEAD), F32),
                             jax.ShapeDtypeStruct((HEAD, HEAD), F32), jax.ShapeDtypeStruct((1, HALF), F32)],
                  scratch_shapes=[pltpu.VMEM((ROW_TILE, HALF), F32), pltpu.VMEM((ROW_TILE, HALF), F32),
                                  pltpu.VMEM((HEAD, HALF), F32)],
                  compiler_params=_params("arbitrary"))(p, p, sg_w, bias, g, d_cd)


def _fox_prep(f, b, name):
    s = f.shape[0]
    t = ATT_BLOCK

    def body(f_ref, b_ref, c_ref):
        tri = (lax.broadcasted_iota(jnp.int32, (t, t), 0) >= lax.broadcasted_iota(jnp.int32, (t, t), 1)).astype(BF16)
        carry = jnp.zeros((1, 128), F32)
        for n in range(s // t):
            lf = _log_sigmoid(f_ref[n * t:(n + 1) * t, :] + b_ref[...])
            hi, mid, lo = _split3(lf)
            c_ref[n * t:(n + 1) * t, :] = _dot_nn(tri, hi) + _dot_nn(tri, mid) + _dot_nn(tri, lo) + carry
            carry = carry + jnp.sum(lf, axis=0, keepdims=True)

    return _pcall(body, name=name, in_specs=[VMEM_SPEC, VMEM_SPEC], out_specs=VMEM_SPEC,
                  out_shape=jax.ShapeDtypeStruct((s, 128), F32))(f, b)


def _fox_post(drow, dcol, f, b, name):
    s = f.shape[0]
    t = ATT_BLOCK

    def body(drow_ref, dcol_ref, f_ref, b_ref, df_ref, db_ref):
        tri = (lax.broadcasted_iota(jnp.int32, (t, t), 1) >= lax.broadcasted_iota(jnp.int32, (t, t), 0)).astype(BF16)
        carry = jnp.zeros((1, 128), F32)
        db = jnp.zeros((1, 128), F32)
        for n in reversed(range(s // t)):
            rs = slice(n * t, (n + 1) * t)
            dc = drow_ref[rs, :] - dcol_ref[rs, :]
            hi, mid, lo = _split3(dc)
            dlogf = _dot_nn(tri, hi) + _dot_nn(tri, mid) + _dot_nn(tri, lo) + carry
            carry = carry + jnp.sum(dc, axis=0, keepdims=True)
            df = dlogf * jax.nn.sigmoid(-(f_ref[rs, :] + b_ref[...]))
            df_ref[rs, :] = df
            db = db + jnp.sum(df, axis=0, keepdims=True)
        db_ref[...] = db

    return _pcall(body, name=name, in_specs=[VMEM_SPEC] * 4, out_specs=[VMEM_SPEC, VMEM_SPEC],
                  out_shape=[jax.ShapeDtypeStruct((s, 128), F32), jax.ShapeDtypeStruct((1, 128), F32)])(drow, dcol, f, b)


def _fox_specs(s):
    t = ATT_BLOCK
    ccol = pl.BlockSpec((HEADS_PER_STEP, t, 1), lambda g, i: (g, i, 0))
    crow = pl.BlockSpec((HEADS_PER_STEP, s // t, 1, t), lambda g, i: (g, 0, 0, 0))
    return ccol, crow


def _fox_fwd(p, c_col, c_row, name, after=()):
    s = p.shape[0]
    t = ATT_BLOCK

    def body(q_ref, k_ref, v_ref, cc_ref, cr_ref, o_ref, lse_ref):
        i = pl.program_id(1)
        heads = range(HEADS_PER_STEP)
        q = [q_ref[:, _head_cols(h)].astype(BF16) for h in heads]
        ct = [cc_ref[h] for h in heads]

        def tile(n, carry, diagonal):
            ks = pl.multiple_of(n * t, t)
            out = []
            for h in heads:
                acc, m, l = carry[h]
                kblk = k_ref[pl.ds(ks, t), _head_cols(h)].astype(BF16)
                vblk = v_ref[pl.ds(ks, t), _head_cols(h)].astype(BF16)
                logit = _dot_nt(q[h], kblk) * ATT_SCALE + ct[h] - cr_ref[h, n]
                if diagonal:
                    causal = lax.broadcasted_iota(jnp.int32, (t, t), 1) <= lax.broadcasted_iota(jnp.int32, (t, t), 0)
                    logit = jnp.where(causal, logit, NEG)
                m_new = jnp.maximum(m, jnp.max(logit, axis=1, keepdims=True))
                alpha = jnp.exp(m - m_new)
                pr = jnp.exp(logit - m_new)
                l = alpha * l + jnp.sum(pr, axis=1, keepdims=True)
                out.append((alpha * acc + _dot_nn(pr.astype(BF16), vblk), m_new, l))
            return tuple(out)

        init = tuple((jnp.zeros((t, HEAD), F32), jnp.full((t, 1), NEG, F32), jnp.zeros((t, 1), F32)) for _ in heads)
        carry = tile(i, lax.fori_loop(0, i, lambda n, c: tile(n, c, False), init), True)
        for h in heads:
            acc, m, l = carry[h]
            o_ref[:, _head_cols(h)] = acc / l
            lse_ref[h] = m + jnp.log(l)

    qspec, kspec, vspec = _head_specs(s, (2 * N_HEADS, 3 * N_HEADS, 4 * N_HEADS))
    ccol, crow = _fox_specs(s)
    ospec = pl.BlockSpec((t, GROUP_W), lambda g, i: (i, g))
    return _pcall(body, after=after, name=name, grid=(N_GROUPS, s // t), in_specs=[qspec, kspec, vspec, ccol, crow],
                  out_specs=[ospec, ccol],
                  out_shape=[jax.ShapeDtypeStruct((s, HALF), F32), jax.ShapeDtypeStruct((N_HEADS, s, 1), F32)],
                  compiler_params=_params("parallel", "parallel"))(p, p, p, c_col, c_row)


def _fox_bwd(p, c_col, c_row, lse, d_cd, d_out, name):
    s = p.shape[0]
    t = ATT_BLOCK

    def body(q_ref, k_ref, v_ref, cc_ref, cr_ref, lse_ref, do_ref, o_ref, dq_ref, dk_ref, dv_ref, dcol_ref, drow_ref):
        i = pl.program_id(1)

        @pl.when(i == 0)
        def _():
            dk_ref[...] = jnp.zeros_like(dk_ref)
            dv_ref[...] = jnp.zeros_like(dv_ref)
            dcol_ref[...] = jnp.zeros_like(dcol_ref)

        heads = range(HEADS_PER_STEP)
        q = [q_ref[:, _head_cols(h)].astype(BF16) for h in heads]
        do = [do_ref[:, _head_cols(h)].astype(BF16) for h in heads]
        delta = [jnp.sum(do_ref[:, _head_cols(h)] * o_ref[:, _head_cols(h)], axis=1, keepdims=True) for h in heads]
        ct = [cc_ref[h] for h in heads]
        lse_v = [lse_ref[h] for h in heads]
        ones = jnp.ones((t, HEAD), BF16)

        def tile(n, carry, diagonal):
            ks = pl.multiple_of(n * t, t)
            out = []
            for h in heads:
                dq, drow = carry[h]
                kblk = k_ref[pl.ds(ks, t), _head_cols(h)].astype(BF16)
                vblk = v_ref[pl.ds(ks, t), _head_cols(h)].astype(BF16)
                logit = _dot_nt(q[h], kblk) * ATT_SCALE + ct[h] - cr_ref[h, n]
                pr = jnp.exp(logit - lse_v[h])
                if diagonal:
                    causal = lax.broadcasted_iota(jnp.int32, (t, t), 1) <= lax.broadcasted_iota(jnp.int32, (t, t), 0)
                    pr = jnp.where(causal, pr, 0.0)
                ds = pr * (_dot_nt(do[h], vblk) - delta[h])
                dsb = ds.astype(BF16)
                dv_ref[pl.ds(ks, t), _head_cols(h)] += _dot_tn(pr.astype(BF16), do[h])
                dk_ref[pl.ds(ks, t), _head_cols(h)] += _dot_tn(dsb, q[h]) * ATT_SCALE
                dcol_ref[pl.ds(ks, t), _head_cols(h)] += _dot_tn(dsb, ones)
                out.append((dq + _dot_nn(dsb, kblk) * ATT_SCALE,
                            drow + jnp.sum(dsb.astype(F32), axis=1, keepdims=True)))
            return tuple(out)

        init = tuple((jnp.zeros((t, HEAD), F32), jnp.zeros((t, 1), F32)) for _ in heads)
        carry = tile(i, lax.fori_loop(0, i, lambda n, c: tile(n, c, False), init), True)
        for h in heads:
            dq_ref[:, _head_cols(h)] = carry[h][0]
            drow_ref[h] = carry[h][1]

    qspec, kspec, vspec = _head_specs(s, (2 * N_HEADS, 3 * N_HEADS, 4 * N_HEADS))
    ccol, crow = _fox_specs(s)
    dospec = pl.BlockSpec((t, GROUP_W), lambda g, i: (i, N_GROUPS + g))
    blk = pl.BlockSpec((t, GROUP_W), lambda g, i: (i, g))
    whole = pl.BlockSpec((s, GROUP_W), lambda g, i: (0, g))
    shape = jax.ShapeDtypeStruct((s, HALF), F32)
    return _pcall(body, name=name, grid=(N_GROUPS, s // t),
                  in_specs=[qspec, kspec, vspec, ccol, crow, ccol, dospec, blk],
                  out_specs=[blk, whole, whole, whole, ccol],
                  out_shape=[shape, shape, shape, shape, jax.ShapeDtypeStruct((N_HEADS, s, 1), F32)],
                  compiler_params=_params("parallel", "arbitrary"))(p, p, p, c_col, c_row, lse, d_cd, d_out)


def _row_tile(rows, cap):
    for t in (256, 128, 64, 32, 16, 8):
        if t <= cap and rows % t == 0:
            return t
    return rows


def _adamw(w, g, m, v, name):
    rows, cols = w.shape
    tr = _row_tile(rows, 128)
    c1 = 1.0 / (1.0 - ADAM_B1 ** ADAM_STEP)
    c2 = 1.0 / (1.0 - ADAM_B2 ** ADAM_STEP)

    def body(w_ref, g_ref, m_ref, v_ref, d_ref, nm_ref, nv_ref):
        gv = g_ref[...]
        nm = ADAM_B1 * m_ref[...] + (1.0 - ADAM_B1) * gv
        nv = ADAM_B2 * v_ref[...] + (1.0 - ADAM_B2) * (gv * gv)
        nm_ref[...] = nm
        nv_ref[...] = nv
        d_ref[...] = -ADAM_LR * ((nm * c1) / (jnp.sqrt(nv * c2) + ADAM_EPS) + ADAM_WD * w_ref[...])

    spec = pl.BlockSpec((tr, cols), lambda i: (i, 0))
    shape = jax.ShapeDtypeStruct((rows, cols), F32)
    return _pcall(body, name=name, grid=(rows // tr,), in_specs=[spec] * 4, out_specs=[spec] * 3,
                  out_shape=[shape] * 3, compiler_params=_params("parallel"))(w, g, m, v)


def _half_shape(whole_shape, kind):
    if kind == "col":
        return (whole_shape[0] // 2, whole_shape[1] // 4)
    if kind == "row":
        return (whole_shape[0] // 8, whole_shape[1])
    return (whole_shape[1] // 2, whole_shape[2])


def _own_half_spec(whole_shape, kind, tr):
    hr, hc = _half_shape(whole_shape, kind)
    nb = hr // tr
    if kind == "col":
        return pl.BlockSpec((tr, hc), lambda i, pos: (pos[1] * nb + i, pos[0]))
    if kind == "row":
        return pl.BlockSpec((tr, hc), lambda i, pos: ((2 * pos[0] + pos[1]) * nb + i, 0))
    return pl.BlockSpec((None, tr, hc), lambda i, pos: (pos[0], pos[1] * nb + i, 0))


def _sum_partials(pos, grad, landed, kind, name):
    hr, hc = _half_shape(grad.shape, kind)
    tr = _row_tile(hr, 64)

    def body(pos_ref, g_ref, p_ref, o_ref):
        acc = g_ref[...].astype(F32)
        for k in range(N_DEV - 1):
            acc = acc + p_ref[k].astype(F32)
        o_ref[...] = acc

    grid_spec = pltpu.PrefetchScalarGridSpec(
        num_scalar_prefetch=1, grid=(hr // tr,),
        in_specs=[_own_half_spec(grad.shape, kind, tr), pl.BlockSpec((N_DEV - 1, tr, hc), lambda i, pos: (0, i, 0))],
        out_specs=pl.BlockSpec((tr, hc), lambda i, pos: (i, 0)))
    return _pcall(body, name=name, grid_spec=grid_spec, out_shape=jax.ShapeDtypeStruct((hr, hc), F32),
                  compiler_params=_params("parallel"))(pos, grad, landed)


def _adamw_shard(pos, w, g_mine, g_sibling, m, v, name):
    hr, hc = g_mine.shape
    tr = _row_tile(hr, 128)
    nb = hr // tr
    c1 = 1.0 / (1.0 - ADAM_B1 ** ADAM_STEP)
    c2 = 1.0 / (1.0 - ADAM_B2 ** ADAM_STEP)

    def body(pos_ref, w_ref, gm_ref, gs_ref, m_ref, v_ref, g_ref, d_ref, nm_ref, nv_ref):
        mine = (pl.program_id(0) // nb) == pos_ref[1]
        gv = jnp.where(mine, gm_ref[...], gs_ref[...])
        nm = ADAM_B1 * m_ref[...] + (1.0 - ADAM_B1) * gv
        nv = ADAM_B2 * v_ref[...] + (1.0 - ADAM_B2) * (gv * gv)
        g_ref[...] = gv
        nm_ref[...] = nm
        nv_ref[...] = nv
        d_ref[...] = -ADAM_LR * ((nm * c1) / (jnp.sqrt(nv * c2) + ADAM_EPS) + ADAM_WD * w_ref[...])

    full = pl.BlockSpec((tr, hc), lambda i, pos: (i, 0))
    mine_spec = pl.BlockSpec((tr, hc), lambda i, pos: (jnp.clip(i - pos[1] * nb, 0, nb - 1), 0))
    sib_spec = pl.BlockSpec((tr, hc), lambda i, pos: (jnp.clip(i - (1 - pos[1]) * nb, 0, nb - 1), 0))
    grid_spec = pltpu.PrefetchScalarGridSpec(
        num_scalar_prefetch=1, grid=(2 * nb,), in_specs=[full, mine_spec, sib_spec, full, full], out_specs=[full] * 4)
    shape = jax.ShapeDtypeStruct((2 * hr, hc), F32)
    return _pcall(body, name=name, grid_spec=grid_spec, out_shape=[shape] * 4,
                  compiler_params=_params("parallel"))(pos, w, g_mine, g_sibling, m, v)


def _place_shard(pos, shard, kind, name, after=()):
    rows, cols = shard.shape
    tr = _row_tile(rows, 256)
    nb = rows // tr
    if kind == "col":
        out_spec = pl.BlockSpec((tr, cols), lambda i, pos: (i, pos[0]))
    elif kind == "row":
        out_spec = pl.BlockSpec((tr, cols), lambda i, pos: (pos[0] * nb + i, 0))
    else:
        out_spec = pl.BlockSpec((None, tr, cols), lambda i, pos: (pos[0], i, 0))

    def body(pos_ref, s_ref, *rest):
        rest[-1][...] = s_ref[...].astype(BF16)

    grid_spec = pltpu.PrefetchScalarGridSpec(
        num_scalar_prefetch=1, grid=(nb,),
        in_specs=[pl.BlockSpec((tr, cols), lambda i, pos: (i, 0))] + [pl.BlockSpec(memory_space=pl.ANY)] * len(after),
        out_specs=out_spec)
    return _pcall(body, name=name, grid_spec=grid_spec,
                  out_shape=jax.ShapeDtypeStruct(_whole_shape(shard.shape, kind), BF16),
                  compiler_params=_params("parallel"))(pos, shard, *after)


N_DEV = 8
RELATIONS = [(r >> 2 & 1, r >> 1 & 1, r & 1) for r in range(1, N_DEV)]


def _position():
    return lax.axis_index("x"), lax.axis_index("y"), lax.axis_index("c")


def _related(pos, rel):
    return tuple(1 - p if f else p for p, f in zip(pos, rel))


def _index(pos):
    return 4 * pos[0] + 2 * pos[1] + pos[2]


def _window(ref, kind, pos):
    px, py, pc = pos
    j = 2 * px + py
    if kind == "col":
        r, c = ref.shape
        return ref.at[pl.ds(pc * (r // 2), r // 2), pl.ds(pl.multiple_of(j * (c // 4), 128), c // 4)]
    if kind == "row":
        rj = ref.shape[0] // 4
        return ref.at[pl.ds(j * rj + pc * (rj // 2), rj // 2), :]
    r = ref.shape[1]
    return ref.at[j, pl.ds(pc * (r // 2), r // 2), :]


def _whole_shape(shard_shape, kind):
    r, c = shard_shape
    return {"col": (r, 4 * c), "row": (4 * r, c), "maj": (4, r, c)}[kind]


SEM_SPEC = pl.BlockSpec(memory_space=pltpu.SEMAPHORE)
ANY_SPEC = pl.BlockSpec(memory_space=pl.ANY)
DATAFLOW = pltpu.SideEffectType.DATAFLOW_SIDE_EFFECTING
TOKEN = jax.ShapeDtypeStruct((8, 128), F32)


def _hbm(a):
    return pltpu.with_memory_space_constraint(a, pltpu.HBM)


def _chips(x, y):
    return [(1 - x, y), (x, 1 - y), (1 - x, 1 - y)]


def _split_start(body, name, buffers, n_sems, after=()):
    n = len(buffers)

    def wrapped(*refs):
        body(refs[:n], refs[n], refs[n + 1])
        refs[-1][...] = jnp.zeros_like(refs[-1])

    outs = _pcall(
        wrapped, after=after, name=name, in_specs=[HBM_SPEC] * n,
        out_specs=[SEM_SPEC, SEM_SPEC] + [HBM_SPEC] * n + [VMEM_SPEC],
        out_shape=[pltpu.SemaphoreType.DMA(n_sems), pltpu.SemaphoreType.DMA(n_sems)]
        + [pltpu.HBM(b.shape, b.dtype) for b in buffers] + [TOKEN],
        input_output_aliases={i: 2 + i for i in range(n)},
        compiler_params=pltpu.CompilerParams(has_side_effects=DATAFLOW))(*[_hbm(b) for b in buffers])
    return outs[0], outs[1], list(outs[2:2 + n]), outs[2 + n]


def _split_wait(body, name, buffers, send_sems, recv_sems, after):
    n = len(buffers)
    after = list(after) if isinstance(after, (list, tuple)) else [after]

    def wrapped(*refs):
        body(refs[:n], refs[n], refs[n + 1])

    outs = _pcall(
        wrapped, name=name, in_specs=[HBM_SPEC] * n + [SEM_SPEC, SEM_SPEC] + [ANY_SPEC] * len(after),
        out_specs=[HBM_SPEC] * n, out_shape=[pltpu.HBM(b.shape, b.dtype) for b in buffers],
        input_output_aliases={i: i for i in range(n)},
        compiler_params=pltpu.CompilerParams(has_side_effects=DATAFLOW))(*buffers, send_sems, recv_sems, *after)
    return list(outs)


def _gather_start(wholes, kinds, name, after=()):
    def body(w_refs, send_sems, recv_sems):
        x, y, c = _position()
        for w, ref in enumerate(w_refs):
            mine = _window(ref, kinds[w], (x, y, c))
            for k, chip in enumerate(_chips(x, y)):
                pltpu.make_async_remote_copy(src_ref=mine, dst_ref=mine, send_sem=send_sems.at[3 * w + k],
                                             recv_sem=recv_sems.at[3 * w + k], device_id=(*chip, c),
                                             device_id_type=MESH).start()

    return _split_start(body, name, wholes, (3 * len(wholes),), after)


def _gather_forward(wholes, kinds, send1, recv1, after, name):
    n = len(wholes)

    def wrapped(*refs):
        w_refs, s1, r1, s2, r2 = refs[:n], refs[n], refs[n + 1], refs[n + 3], refs[n + 4]
        x, y, c = _position()
        for k, chip in enumerate(_chips(x, y)):
            for w, ref in enumerate(w_refs):
                theirs = _window(ref, kinds[w], (*chip, c))
                pltpu.make_async_remote_copy(src_ref=theirs, dst_ref=theirs, send_sem=s1.at[3 * w + k],
                                             recv_sem=r1.at[3 * w + k], device_id=(*chip, c),
                                             device_id_type=MESH).wait_recv()
                pltpu.make_async_remote_copy(src_ref=theirs, dst_ref=theirs, send_sem=s2.at[3 * w + k],
                                             recv_sem=r2.at[3 * w + k], device_id=(x, y, 1 - c),
                                             device_id_type=MESH).start()
        for w, ref in enumerate(w_refs):
            mine = _window(ref, kinds[w], (x, y, c))
            for k, chip in enumerate(_chips(x, y)):
                pltpu.make_async_remote_copy(src_ref=mine, dst_ref=mine, send_sem=s1.at[3 * w + k],
                                             recv_sem=r1.at[3 * w + k], device_id=(*chip, c),
                                             device_id_type=MESH).wait_send()
        refs[-1][...] = jnp.zeros_like(refs[-1])

    outs = _pcall(
        wrapped, name=name, in_specs=[HBM_SPEC] * n + [SEM_SPEC, SEM_SPEC, ANY_SPEC],
        out_specs=[SEM_SPEC, SEM_SPEC] + [HBM_SPEC] * n + [VMEM_SPEC],
        out_shape=[pltpu.SemaphoreType.DMA((3 * n,)), pltpu.SemaphoreType.DMA((3 * n,))]
        + [pltpu.HBM(b.shape, b.dtype) for b in wholes] + [TOKEN],
        input_output_aliases={i: 2 + i for i in range(n)},
        compiler_params=pltpu.CompilerParams(has_side_effects=DATAFLOW))(*wholes, send1, recv1, after)
    return outs[0], outs[1], list(outs[2:2 + n]), outs[2 + n]


def _gather_finish(wholes, kinds, send2, recv2, after, name):
    def body(w_refs, s2, r2):
        x, y, c = _position()
        for k, chip in enumerate(_chips(x, y)):
            for w, ref in enumerate(w_refs):
                sent = _window(ref, kinds[w], (*chip, c))
                got = _window(ref, kinds[w], (*chip, 1 - c))
                pltpu.make_async_remote_copy(src_ref=sent, dst_ref=got, send_sem=s2.at[3 * w + k],
                                             recv_sem=r2.at[3 * w + k], device_id=(x, y, 1 - c),
                                             device_id_type=MESH).wait()

    return _split_wait(body, name, wholes, send2, recv2, after)


def _gather_small(small, after=()):
    def body(s_ref, o_ref, send_sems, recv_sems, local_sem):
        x, y, c = _position()
        mine = pltpu.make_async_copy(s_ref, o_ref.at[2 * x + y], local_sem)
        mine.start()
        sends = []
        for k, chip in enumerate(_chips(x, y)):
            cp = pltpu.make_async_remote_copy(src_ref=s_ref, dst_ref=o_ref.at[2 * x + y], send_sem=send_sems.at[k],
                                              recv_sem=recv_sems.at[k], device_id=(*chip, c), device_id_type=MESH)
            cp.start()
            sends.append(cp)
        for k, chip in enumerate(_chips(x, y)):
            pltpu.make_async_remote_copy(src_ref=s_ref, dst_ref=o_ref.at[2 * chip[0] + chip[1]], send_sem=send_sems.at[k],
                                         recv_sem=recv_sems.at[k], device_id=(*chip, c), device_id_type=MESH).wait_recv()
        for cp in sends:
            cp.wait_send()
        mine.wait()

    return _pcall(body, after=after, name="gather_small", in_specs=[HBM_SPEC], out_specs=HBM_SPEC,
                  out_shape=jax.ShapeDtypeStruct((4,) + small.shape, small.dtype),
                  scratch_shapes=[pltpu.SemaphoreType.DMA((3,)), pltpu.SemaphoreType.DMA((3,)),
                                  pltpu.SemaphoreType.DMA(())])(small)


def _scatter_copies(g_refs, land_refs, kinds, send_sems, recv_sems):
    me = _position()
    copies = []
    for k, rel in enumerate(RELATIONS):
        peer = _related(me, rel)
        for w, (g_ref, land_ref) in enumerate(zip(g_refs, land_refs)):
            copies.append(pltpu.make_async_remote_copy(
                src_ref=_window(g_ref, kinds[w], peer), dst_ref=land_ref.at[k],
                send_sem=send_sems.at[7 * w + k], recv_sem=recv_sems.at[7 * w + k], device_id=peer,
                device_id_type=MESH))
    return copies


def _scatter_start(grads, kinds, name):
    n = len(grads)
    lands = [lax.empty((N_DEV - 1,) + _half_shape(g.shape, kd), g.dtype) for g, kd in zip(grads, kinds)]

    def body(refs, send_sems, recv_sems):
        for cp in _scatter_copies(refs[:n], refs[n:], kinds, send_sems, recv_sems):
            cp.start()

    send, recv, thru, token = _split_start(body, name, list(grads) + lands, ((N_DEV - 1) * n,))
    return send, recv, thru[:n], thru[n:], token


def _scatter_wait(grads, lands, kinds, send, recv, after, name):
    n = len(grads)

    def body(refs, send_sems, recv_sems):
        for cp in _scatter_copies(refs[:n], refs[n:], kinds, send_sems, recv_sems):
            cp.wait()

    out = _split_wait(body, name, list(grads) + list(lands), send, recv, after)
    return out[:n], out[n:]


def _swap_start(halves, name):
    n = len(halves)
    lands = [lax.empty(h.shape, h.dtype) for h in halves]

    def body(refs, send_sems, recv_sems):
        x, y, c = _position()
        for w in range(n):
            pltpu.make_async_remote_copy(src_ref=refs[w], dst_ref=refs[n + w], send_sem=send_sems.at[w],
                                         recv_sem=recv_sems.at[w], device_id=(x, y, 1 - c), device_id_type=MESH).start()

    send, recv, thru, token = _split_start(body, name, list(halves) + lands, (n,))
    return send, recv, thru[:n], thru[n:], token


def _swap_wait(halves, lands, send, recv, after, name):
    n = len(halves)

    def body(refs, send_sems, recv_sems):
        x, y, c = _position()
        for w in range(n):
            pltpu.make_async_remote_copy(src_ref=refs[w], dst_ref=refs[n + w], send_sem=send_sems.at[w],
                                         recv_sem=recv_sems.at[w], device_id=(x, y, 1 - c), device_id_type=MESH).wait()

    out = _split_wait(body, name, list(halves) + list(lands), send, recv, after)
    return out[:n], out[n:]


def _allreduce_small(v, after=()):
    rows = v.shape[0]

    def body(v_ref, o_ref, recv_ref, send_sems, recv_sems):
        me = _position()
        recv_ref[_index(me)] = v_ref[...]
        sends = []
        for k, rel in enumerate(RELATIONS):
            peer = _related(me, rel)
            cp = pltpu.make_async_remote_copy(
                src_ref=v_ref, dst_ref=recv_ref.at[_index(me)],
                send_sem=send_sems.at[k], recv_sem=recv_sems.at[k], device_id=peer, device_id_type=MESH)
            cp.start()
            sends.append(cp)
        for k, rel in enumerate(RELATIONS):
            peer = _related(me, rel)
            pltpu.make_async_remote_copy(
                src_ref=v_ref, dst_ref=recv_ref.at[_index(peer)],
                send_sem=send_sems.at[k], recv_sem=recv_sems.at[k], device_id=peer, device_id_type=MESH).wait_recv()
        for cp in sends:
            cp.wait_send()
        acc = recv_ref[0]
        for k in range(1, N_DEV):
            acc = acc + recv_ref[k]
        o_ref[...] = acc

    return _pcall(body, after=after, name="allreduce_small", in_specs=[VMEM_SPEC], out_specs=VMEM_SPEC,
                  out_shape=jax.ShapeDtypeStruct((rows, 128), F32),
                  scratch_shapes=[pltpu.VMEM((N_DEV, rows, 128), F32), pltpu.SemaphoreType.DMA((7,)),
                                  pltpu.SemaphoreType.DMA((7,))],
                  compiler_params=pltpu.CompilerParams(vmem_limit_bytes=VMEM_LIMIT_BYTES))(v)


def _pack(arrays):
    flat = []
    for a in arrays:
        a = a.reshape(-1)
        flat.append(jnp.pad(a, (0, -a.shape[0] % 128)))
    flat = jnp.concatenate(flat)
    flat = jnp.pad(flat, (0, -flat.shape[0] % 1024))
    return flat.reshape(-1, 128)


def _unpack(packed, shapes):
    flat = packed.reshape(-1)
    out, at = [], 0
    for shp in shapes:
        size = 1
        for d in shp:
            size *= d
        out.append(flat[at:at + size].reshape(shp))
        at += size + (-size % 128)
    return out


WEIGHTS = ['l0_mix_norm_g', 'l0_w_in', 'l0_sc_conv_w', 'l0_w_out', 'l0_ffn_norm_g', 'l0_ffn_up', 'l0_ffn_conv_w',
           'l0_ffn_down', 'l1_mix_norm_g', 'l1_w_in', 'l1_fox_b_f', 'l1_sg_w', 'l1_sg_b', 'l1_sg_norm_g', 'l1_w_out',
           'l1_ffn_norm_g', 'l1_ffn_up', 'l1_ffn_conv_w', 'l1_ffn_down', 'final_norm_g']
BIG = {'l0_w_in': 'col', 'l0_w_out': 'row', 'l0_ffn_up': 'col', 'l0_ffn_down': 'row',
       'l1_w_in': 'maj', 'l1_w_out': 'row', 'l1_ffn_up': 'col', 'l1_ffn_down': 'row'}
GATHER_GROUPS = [['l0_w_in'], ['l0_w_out'], ['l0_ffn_up'], ['l0_ffn_down'], ['l1_w_in', 'l1_w_out'],
                 ['l1_ffn_up'], ['l1_ffn_down']]
CONV = ['l0_sc_conv_w', 'l0_ffn_conv_w', 'l1_ffn_conv_w']
SMALL = [n for n in WEIGHTS if n not in BIG]
IN_CD = 5 * HALF + N_HEADS


def _ffn_forward(x, g, get_up, behind_act, get_down, conv_w, tag):
    h = _rmsnorm_fwd(x, g, tag + "_norm")
    u = _matmul(h, get_up(h), "nn", F32, tag + "_up")
    f = _ffn_act_fwd(u, conv_w, tag + "_act", after=behind_act(u))
    w_down, tokens = get_down(f)
    return _matmul(f, w_down, "nn", F32, tag + "_down", res=x, after=tokens), (h, u, f)


def _ffn_backward(x, g, w_up, conv_w, w_down, saved, d_out, send_up, send_down, tag):
    h, u, f = saved
    dw_down = _matmul(f, d_out, "tn", BF16, tag + "_dwdown")
    d_f = _matmul(d_out, w_down, "nt", F32, tag + "_df", after=[send_down(dw_down)])
    du, dcw_gate, dcw_up = _ffn_act_bwd(u, conv_w, d_f, tag + "_dact")
    dw_up = _matmul(h, du, "tn", BF16, tag + "_dwup")
    dh = _matmul(du, w_up, "nt", F32, tag + "_dh", after=[send_up(dw_up)])
    dx, dg = _rmsnorm_bwd(x, g, dh, d_out, tag + "_dnorm")
    return dx, dg, jnp.concatenate([dcw_gate, dcw_up], axis=1)


def kernel(x, l0_mix_norm_g, l0_w_in, l0_sc_conv_w, l0_w_out, l0_ffn_norm_g, l0_ffn_up, l0_ffn_conv_w, l0_ffn_down, l1_mix_norm_g, l1_w_in, l1_fox_b_f, l1_sg_w, l1_sg_b, l1_sg_norm_g, l1_w_out, l1_ffn_norm_g, l1_ffn_up, l1_ffn_conv_w, l1_ffn_down, final_norm_g, loss_target, m_l0_mix_norm_g, m_l0_w_in, m_l0_sc_conv_w, m_l0_w_out, m_l0_ffn_norm_g, m_l0_ffn_up, m_l0_ffn_conv_w, m_l0_ffn_down, m_l1_mix_norm_g, m_l1_w_in, m_l1_fox_b_f, m_l1_sg_w, m_l1_sg_b, m_l1_sg_norm_g, m_l1_w_out, m_l1_ffn_norm_g, m_l1_ffn_up, m_l1_ffn_conv_w, m_l1_ffn_down, m_final_norm_g, v_l0_mix_norm_g, v_l0_w_in, v_l0_sc_conv_w, v_l0_w_out, v_l0_ffn_norm_g, v_l0_ffn_up, v_l0_ffn_conv_w, v_l0_ffn_down, v_l1_mix_norm_g, v_l1_w_in, v_l1_fox_b_f, v_l1_sg_w, v_l1_sg_b, v_l1_sg_norm_g, v_l1_w_out, v_l1_ffn_norm_g, v_l1_ffn_up, v_l1_ffn_conv_w, v_l1_ffn_down, v_final_norm_g):
    given = (l0_mix_norm_g, l0_w_in, l0_sc_conv_w, l0_w_out, l0_ffn_norm_g, l0_ffn_up, l0_ffn_conv_w, l0_ffn_down, l1_mix_norm_g, l1_w_in, l1_fox_b_f, l1_sg_w, l1_sg_b, l1_sg_norm_g, l1_w_out, l1_ffn_norm_g, l1_ffn_up, l1_ffn_conv_w, l1_ffn_down, final_norm_g)
    given_m = (m_l0_mix_norm_g, m_l0_w_in, m_l0_sc_conv_w, m_l0_w_out, m_l0_ffn_norm_g, m_l0_ffn_up, m_l0_ffn_conv_w, m_l0_ffn_down, m_l1_mix_norm_g, m_l1_w_in, m_l1_fox_b_f, m_l1_sg_w, m_l1_sg_b, m_l1_sg_norm_g, m_l1_w_out, m_l1_ffn_norm_g, m_l1_ffn_up, m_l1_ffn_conv_w, m_l1_ffn_down, m_final_norm_g)
    given_v = (v_l0_mix_norm_g, v_l0_w_in, v_l0_sc_conv_w, v_l0_w_out, v_l0_ffn_norm_g, v_l0_ffn_up, v_l0_ffn_conv_w, v_l0_ffn_down, v_l1_mix_norm_g, v_l1_w_in, v_l1_fox_b_f, v_l1_sg_w, v_l1_sg_b, v_l1_sg_norm_g, v_l1_w_out, v_l1_ffn_norm_g, v_l1_ffn_up, v_l1_ffn_conv_w, v_l1_ffn_down, v_final_norm_g)
    wt = dict(zip(WEIGHTS, given))
    mom = dict(zip(WEIGHTS, given_m))
    var = dict(zip(WEIGHTS, given_v))
    s = x.shape[1]
    t = ATT_BLOCK
    x0, target = x[0], loss_target[0]
    chip = 2 * lax.axis_index("x") + lax.axis_index("y")

    pos = jnp.stack([chip, lax.axis_index("c")]).astype(jnp.int32)

    conv_widths = [wt[n].shape[1] for n in CONV]
    conv_all = _gather_small(jnp.concatenate([wt[n] for n in CONV], axis=1))
    conv_full, at = {}, 0
    for n, cw in zip(CONV, conv_widths):
        conv_full[n] = jnp.transpose(conv_all[:, :, at:at + cw], (1, 0, 2)).reshape(3, 4 * cw)
        at += cw
    gathers, token = [], conv_all
    for gi, names in enumerate(GATHER_GROUPS):
        placed = [_place_shard(pos, wt[n], BIG[n], "place_" + n, [token]) for n in names]
        send, recv, thru, token = _gather_start(placed, [BIG[n] for n in names], "gather_start_%d" % gi, [token])
        gathers.append((send, recv, thru))
    full = {}

    def forward_gather(gi, after):
        send, recv, thru = gathers[gi]
        kinds = [BIG[n] for n in GATHER_GROUPS[gi]]
        gathers[gi] = _gather_forward(thru, kinds, send, recv, after, "gather_forward_%d" % gi)
        return gathers[gi][3]

    def finish_gather(gi, after):
        send, recv, thru, tok = gathers[gi]
        names = GATHER_GROUPS[gi]
        wholes = _gather_finish(thru, [BIG[n] for n in names], send, recv, tok if after is None else after,
                                "gather_finish_%d" % gi)
        full.update(zip(names, wholes))

    def vec(name):
        return wt[name].reshape(1, -1)

    h0 = _rmsnorm_fwd(x0, vec('l0_mix_norm_g'), "l0_mix_norm", after=[token])
    forward_gather(0, h0)
    finish_gather(0, None)
    p0 = _matmul(h0, full['l0_w_in'], "nn", F32, "l0_in")
    a_out, sb_carries = _sb_fwd(p0, "l0_sb", after=[forward_gather(1, p0)])
    finish_gather(1, a_out)
    b_out = _sc_fwd(p0, conv_full['l0_sc_conv_w'], "l0_sc")
    ab0 = jnp.concatenate([a_out.astype(BF16), b_out], axis=1)
    x1 = _matmul(ab0, full['l0_w_out'], "nn", F32, "l0_out", res=x0, after=[forward_gather(2, b_out)])

    def ffn_weights(up_group, next_group):
        def get_up(h):
            finish_gather(up_group, h)
            return full[GATHER_GROUPS[up_group][0]]

        def behind_act(u):
            return [forward_gather(up_group + 1, u)]

        def get_down(f):
            finish_gather(up_group + 1, f)
            return full[GATHER_GROUPS[up_group + 1][0]], ([forward_gather(next_group, f)] if next_group else ())

        return get_up, behind_act, get_down

    x2, ffn0_saved = _ffn_forward(x1, vec('l0_ffn_norm_g'), *ffn_weights(2, 4), conv_full['l0_ffn_conv_w'], "l0_ffn")
    h2 = _rmsnorm_fwd(x2, vec('l1_mix_norm_g'), "l1_mix_norm")
    finish_gather(4, h2)
    w_in1 = jnp.transpose(full['l1_w_in'], (1, 0, 2)).reshape(D_MODEL, IN_CD)
    w_in1_main = w_in1[:, :5 * HALF]
    w_in1_f = jnp.pad(w_in1[:, 5 * HALF:], ((0, 0), (0, 128 - N_HEADS)))
    p1 = _matmul(h2, w_in1_main, "nn", F32, "l1_in")
    f_logit = _matmul(h2, w_in1_f, "nn", F32, "l1_in_f")
    b_f = jnp.pad(wt['l1_fox_b_f'], (0, 128 - N_HEADS)).reshape(1, 128)
    c_heads = _fox_prep(f_logit, b_f, "l1_fox_prep")[:, :N_HEADS].T
    c_col = c_heads[:, :, None]
    c_row = c_heads.reshape(N_HEADS, s // t, 1, t)
    sg_bias = jnp.repeat(wt['l1_sg_b'].T, HEAD, axis=1)
    sg_gain = vec('l1_sg_norm_g')
    c_out = _sg_fwd(p1, wt['l1_sg_w'], sg_bias, sg_gain, "l1_sg")
    d_out, lse = _fox_fwd(p1, c_col, c_row, "l1_fox", after=[forward_gather(5, c_out)])
    cd1 = jnp.concatenate([c_out, d_out.astype(BF16)], axis=1)
    x3 = _matmul(cd1, full['l1_w_out'], "nn", F32, "l1_out", res=x2)
    x4, ffn1_saved = _ffn_forward(x3, vec('l1_ffn_norm_g'), *ffn_weights(5, None), conv_full['l1_ffn_conv_w'], "l1_ffn")
    dx4, dg_final, loss_part = _loss_head(x4, vec('final_norm_g'), target, "loss_head")

    grads = {'final_norm_g': dg_final}
    scatters = []

    def send_grads(names):
        def start(*group):
            send, recv, thru, lands, tok = _scatter_start(list(group), [BIG[n] for n in names],
                                                          "scatter_start_%d" % len(scatters))
            scatters.append((names, send, recv, thru, lands))
            return tok
        return start

    dx3, grads['l1_ffn_norm_g'], grads['l1_ffn_conv_w'] = _ffn_backward(
        x3, vec('l1_ffn_norm_g'), full['l1_ffn_up'], conv_full['l1_ffn_conv_w'], full['l1_ffn_down'], ffn1_saved, dx4,
        send_grads(['l1_ffn_up']), send_grads(['l1_ffn_down']), "l1_ffn")
    dw_out1 = _matmul(cd1, dx3, "tn", BF16, "l1_dwout")
    d_cd = _matmul(dx3, full['l1_w_out'], "nt", F32, "l1_dcd")
    du, dv, grads['l1_sg_w'], db_sg, grads['l1_sg_norm_g'] = _sg_bwd(p1, wt['l1_sg_w'], sg_bias, sg_gain, d_cd, "l1_dsg")
    grads['l1_sg_b'] = db_sg[:, :N_HEADS].T
    dq, dk, dvv, dcol, drow = _fox_bwd(p1, c_col, c_row, lse, d_cd, d_out, "l1_dfox")
    pad8 = ((0, 0), (0, 128 - N_HEADS))
    d_f_logit, d_b_f = _fox_post(jnp.pad(drow[:, :, 0].T, pad8), jnp.pad(dcol[:, ::HEAD], pad8), f_logit, b_f,
                                 "l1_fox_post")
    grads['l1_fox_b_f'] = d_b_f[0, :N_HEADS]
    dp1 = jnp.concatenate([a.astype(BF16) for a in (du, dv, dq, dk, dvv)], axis=1)
    dw_main = _matmul(h2, dp1, "tn", BF16, "l1_dwin")
    dw_f = _matmul(h2, d_f_logit, "tn", BF16, "l1_dwin_f")
    dw_in1 = jnp.concatenate([dw_main, dw_f[:, :N_HEADS]], axis=1)
    dw_in1 = jnp.transpose(dw_in1.reshape(D_MODEL, 4, IN_CD // 4), (1, 0, 2))
    dh2 = _matmul(dp1, w_in1_main, "nt", F32, "l1_dh", after=[send_grads(['l1_w_out', 'l1_w_in'])(dw_out1, dw_in1)])
    dh2 = _matmul(d_f_logit, w_in1_f, "nt", F32, "l1_dh_f", res=dh2)
    dx2, grads['l1_mix_norm_g'] = _rmsnorm_bwd(x2, vec('l1_mix_norm_g'), dh2, dx3, "l1_dmix_norm")
    dx1, grads['l0_ffn_norm_g'], grads['l0_ffn_conv_w'] = _ffn_backward(
        x1, vec('l0_ffn_norm_g'), full['l0_ffn_up'], conv_full['l0_ffn_conv_w'], full['l0_ffn_down'], ffn0_saved, dx2,
        send_grads(['l0_ffn_up']), send_grads(['l0_ffn_down']), "l0_ffn")
    dw_out0 = _matmul(ab0, dx1, "tn", BF16, "l0_dwout")
    d_ab = _matmul(dx1, full['l0_w_out'], "nt", F32, "l0_dab", after=[send_grads(['l0_w_out'])(dw_out0)])
    dq0, dk0, dv0 = _sb_bwd(p0, d_ab, sb_carries, "l0_dsb")
    dgb, dgc, dhin, grads['l0_sc_conv_w'] = _sc_bwd(p0, conv_full['l0_sc_conv_w'], d_ab, "l0_dsc")
    dp0 = jnp.concatenate([a.astype(BF16) for a in (dq0, dk0, dv0, dgb, dgc, dhin)], axis=1)
    dw_in0 = _matmul(h0, dp0, "tn", BF16, "l0_dwin")
    dh0 = _matmul(dp0, full['l0_w_in'], "nt", F32, "l0_dh", after=[send_grads(['l0_w_in'])(dw_in0)])
    dx0, grads['l0_mix_norm_g'] = _rmsnorm_bwd(x0, vec('l0_mix_norm_g'), dh0, dx1, "l0_dmix_norm")

    shard_grads, delta, new_m, new_v, swaps = {}, {}, {}, {}, {}

    def reduce_group(gi, after):
        names, send, recv, thru, lands = scatters[gi]
        kinds = [BIG[n] for n in names]
        g_thru, landed = _scatter_wait(thru, lands, kinds, send, recv, after, "scatter_wait_%d" % gi)
        halves = [_sum_partials(pos, g, ld, kd, "sum_" + n) for n, g, ld, kd in zip(names, g_thru, landed, kinds)]
        s_send, s_recv, h_thru, s_lands, tok = _swap_start(halves, "swap_start_%d" % gi)
        swaps[gi] = (names, s_send, s_recv, h_thru, s_lands)
        return tok

    def update_group(gi, after):
        names, s_send, s_recv, h_thru, s_lands = swaps[gi]
        mine, theirs = _swap_wait(h_thru, s_lands, s_send, s_recv, after, "swap_wait_%d" % gi)
        for n, gm, gs in zip(names, mine, theirs):
            shard_grads[n], delta[n], new_m[n], new_v[n] = _adamw_shard(pos, wt[n], gm, gs, mom[n], var[n], "adamw_" + n)
        return [delta[n] for n in names]

    after = reduce_group(2, reduce_group(1, reduce_group(0, dx0)))
    after = update_group(2, update_group(1, update_group(0, after)))
    after = reduce_group(5, reduce_group(4, reduce_group(3, after)))
    after = update_group(5, update_group(4, update_group(3, after)))
    after = reduce_group(6, after)
    small_shapes = [conv_full[n].shape if n in CONV else wt[n].shape for n in SMALL] + [loss_part.shape]
    small_all = _allreduce_small(_pack([grads[n] for n in SMALL] + [loss_part]), [after])
    small_sums = _unpack(small_all, small_shapes)
    loss = small_sums[-1][0, 0]
    for n, g in zip(SMALL, small_sums):
        shard_grads[n] = lax.dynamic_slice_in_dim(g, chip * wt[n].shape[1], wt[n].shape[1], axis=1) if n in CONV else g
    update_group(6, small_all)
    shapes = [wt[n].shape for n in SMALL]
    packed = _adamw(_pack([wt[n] for n in SMALL]), _pack([shard_grads[n] for n in SMALL]),
                    _pack([mom[n] for n in SMALL]), _pack([var[n] for n in SMALL]), "adamw_small")
    for out, pk in zip((delta, new_m, new_v), packed):
        out.update(zip(SMALL, _unpack(pk, shapes)))

    return (loss, dx0[None], *[shard_grads[n] for n in WEIGHTS], *[delta[n] for n in WEIGHTS],
            *[new_m[n] for n in WEIGHTS], *[new_v[n] for n in WEIGHTS])
```

```python
import jax
import jax.numpy as jnp
from jax import lax
from jax.experimental import pallas as pl
from jax.experimental.pallas import tpu as pltpu

F32 = jnp.float32
BF16 = jnp.bfloat16

D_MODEL = 2048
HEAD = 128
N_HEADS = 8
HALF = N_HEADS * HEAD
D_FF = 5632
EPS = 1e-6
ATT_SCALE = HEAD ** -0.5
ATT_BLOCK = 512
NEG = -1e30

ADAM_LR = 0.001
ADAM_B1 = 0.9
ADAM_B2 = 0.999
ADAM_EPS = 1e-08
ADAM_WD = 0.01
ADAM_STEP = 10

VMEM_LIMIT_BYTES = 48 * 1024 * 1024
MM_VMEM_LIMIT_BYTES = 56 * 1024 * 1024
MESH = pl.DeviceIdType.MESH
HBM_SPEC = pl.BlockSpec(memory_space=pltpu.HBM)
VMEM_SPEC = pl.BlockSpec(memory_space=pltpu.VMEM)


def _pcall(body, after=(), **kw):
    if not after:
        return pl.pallas_call(body, **kw)
    n_in, n_after, inner = len(kw["in_specs"]), len(after), body
    kw["in_specs"] = list(kw["in_specs"]) + [pl.BlockSpec(memory_space=pl.ANY)] * n_after

    def body(*refs):
        inner(*refs[:n_in], *refs[n_in + n_after:])

    call = pl.pallas_call(body, **kw)
    return lambda *args: call(*args, *after)


def _params(*semantics, vmem_limit=VMEM_LIMIT_BYTES):
    return pltpu.CompilerParams(dimension_semantics=semantics, vmem_limit_bytes=vmem_limit)


def _pick(n, cap):
    best = None
    for t in range(128, min(n, cap) + 1, 128):
        if n % t == 0:
            best = t
    return n if best is None else best


def _dot(a, b, dims):
    return lax.dot_general(a, b, (dims, ((), ())), preferred_element_type=F32)


def _dot_nn(a, b):
    return _dot(a, b, ((1,), (0,)))


def _dot_nt(a, b):
    return _dot(a, b, ((1,), (1,)))


def _dot_tn(a, b):
    return _dot(a, b, ((0,), (0,)))


def _split3(x):
    hi = x.astype(BF16)
    r1 = x - hi.astype(F32)
    mid = r1.astype(BF16)
    lo = (r1 - mid.astype(F32)).astype(BF16)
    return hi, mid, lo


def _log_sigmoid(z):
    return jnp.minimum(z, 0.0) - jnp.log1p(jnp.exp(-jnp.abs(z)))


_GELU_K = 0.7978845608028654


def _gelu(x):
    return 0.5 * x * (1.0 + jnp.tanh(_GELU_K * (x + 0.044715 * x * x * x)))


def _gelu_grad(x):
    t = jnp.tanh(_GELU_K * (x + 0.044715 * x * x * x))
    return 0.5 * (1.0 + t) + 0.5 * x * (1.0 - t * t) * _GELU_K * (1.0 + 3.0 * 0.044715 * x * x)


SUBLANES = 8


def _shift_down(x, k):
    rolled = pltpu.roll(x, k, axis=0)
    head = rolled[:SUBLANES]
    head = jnp.where(lax.broadcasted_iota(jnp.int32, head.shape, 0) >= k, head, 0.0)
    return jnp.concatenate([head, rolled[SUBLANES:]], axis=0)


def _shift_up(x, k):
    n = x.shape[0]
    rolled = pltpu.roll(x, n - k, axis=0)
    tail = rolled[n - SUBLANES:]
    tail = jnp.where(lax.broadcasted_iota(jnp.int32, tail.shape, 0) < SUBLANES - k, tail, 0.0)
    return jnp.concatenate([rolled[:n - SUBLANES], tail], axis=0)


def _conv3(s, w, shifted=None):
    s1, s2 = shifted if shifted else (_shift_down(s, 1), _shift_down(s, 2))
    return w[0:1, :] * s2 + w[1:2, :] * s1 + w[2:3, :] * s


def _conv3_transpose(d, w):
    return w[2:3, :] * d + w[1:2, :] * _shift_up(d, 1) + w[0:1, :] * _shift_up(d, 2)


def _conv3_wgrad(d, s, shifted, dw_ref):
    s1, s2 = shifted
    dw_ref[0:1, :] = jnp.sum(d * s2, axis=0, keepdims=True)
    dw_ref[1:2, :] = jnp.sum(d * s1, axis=0, keepdims=True)
    dw_ref[2:3, :] = jnp.sum(d * s, axis=0, keepdims=True)


MM_TILE_M, MM_TILE_N, MM_TILE_K = 1408, 512, 5632


def _matmul(a, b, mode, out_dtype, name, res=None, after=()):
    a_parts = a.shape[0] if a.ndim == 3 else 1
    b_parts = b.shape[0] if b.ndim == 3 else 1
    a_shape = (a.shape[1], a_parts * a.shape[2]) if a.ndim == 3 else a.shape
    b_shape = (b.shape[1], b_parts * b.shape[2]) if b.ndim == 3 else b.shape
    assert (a_parts == 1 or mode != "tn") and (b_parts == 1 or mode == "tn")
    if mode == "nn":
        (m, k), (k2, n) = a_shape, b_shape
    elif mode == "nt":
        (m, k), (n, k2) = a_shape, b_shape
    else:
        (k, m), (k2, n) = a_shape, b_shape
    assert k == k2, (a.shape, b.shape, mode)
    tm, tn, tk = _pick(m, MM_TILE_M), _pick(n // b_parts, MM_TILE_N), _pick(k // a_parts, MM_TILE_K)
    nk = k // tk
    if mode == "tn":
        a_spec = pl.BlockSpec((tk, tm), lambda i, j, kk: (kk, i))
    elif a_parts > 1:
        per = nk // a_parts
        a_spec = pl.BlockSpec((None, tm, tk), lambda i, j, kk: (kk // per, i, kk % per))
    else:
        a_spec = pl.BlockSpec((tm, tk), lambda i, j, kk: (i, kk))
    if mode == "nt":
        b_spec = pl.BlockSpec((tn, tk), lambda i, j, kk: (j, kk))
    elif b_parts > 1:
        per = n // b_parts // tn
        b_spec = pl.BlockSpec((None, tk, tn), lambda i, j, kk: (j // per, kk, j % per))
    else:
        b_spec = pl.BlockSpec((tk, tn), lambda i, j, kk: (kk, j))
    o_spec = pl.BlockSpec((tm, tn), lambda i, j, kk: (i, j))
    dims = {"nn": ((1,), (0,)), "nt": ((1,), (1,)), "tn": ((0,), (0,))}[mode]
    has_res = res is not None

    def body(*refs):
        a_ref, b_ref = refs[0], refs[1]
        r_ref = refs[2] if has_res else None
        o_ref = refs[3] if has_res else refs[2]
        part = _dot(a_ref[...].astype(BF16), b_ref[...].astype(BF16), dims)

        def finish(total):
            if has_res:
                total = total + r_ref[...]
            o_ref[...] = total.astype(out_dtype)

        if nk == 1:
            finish(part)
        else:
            acc_ref = refs[-1]
            kk = pl.program_id(2)

            @pl.when(kk == 0)
            def _():
                acc_ref[...] = part

            @pl.when(kk > 0)
            def _():
                acc_ref[...] += part

            @pl.when(kk == nk - 1)
            def _():
                finish(acc_ref[...])

    in_specs = [a_spec, b_spec] + ([o_spec] if has_res else [])
    args = (a, b) + ((res,) if has_res else ())
    return _pcall(
        body, after=after, name=name, grid=(m // tm, n // tn, nk),
        in_specs=in_specs, out_specs=o_spec,
        out_shape=jax.ShapeDtypeStruct((m, n), out_dtype),
        scratch_shapes=[pltpu.VMEM((tm, tn), F32)] if nk > 1 else [],
        compiler_params=_params("parallel", "parallel", "arbitrary", vmem_limit=MM_VMEM_LIMIT_BYTES),
    )(*args)


ROW_TILE = 256


def _rmsnorm_fwd(x, g, name, after=()):
    s, d = x.shape

    def body(x_ref, g_ref, o_ref):
        xf = x_ref[...]
        r = lax.rsqrt(jnp.mean(xf * xf, axis=-1, keepdims=True) + EPS)
        o_ref[...] = (xf * r * g_ref[...]).astype(BF16)

    row = pl.BlockSpec((ROW_TILE, d), lambda i: (i, 0))
    vec = pl.BlockSpec((1, d), lambda i: (0, 0))
    return _pcall(body, after=after, name=name, grid=(s // ROW_TILE,), in_specs=[row, vec], out_specs=row,
                  out_shape=jax.ShapeDtypeStruct((s, d), BF16), compiler_params=_params("parallel"))(x, g)


def _rmsnorm_bwd(x, g, dh, dres, name):
    s, d = x.shape

    def body(x_ref, g_ref, dh_ref, dres_ref, dx_ref, dg_ref):
        xf = x_ref[...]
        r = lax.rsqrt(jnp.mean(xf * xf, axis=-1, keepdims=True) + EPS)
        xhat = xf * r
        dh_v = dh_ref[...]
        dxh = dh_v * g_ref[...]
        proj = jnp.mean(dxh * xhat, axis=-1, keepdims=True)
        dx_ref[...] = dres_ref[...] + r * (dxh - xhat * proj)
        part = jnp.sum(dh_v * xhat, axis=0, keepdims=True)

        @pl.when(pl.program_id(0) == 0)
        def _():
            dg_ref[...] = part

        @pl.when(pl.program_id(0) > 0)
        def _():
            dg_ref[...] += part

    row = pl.BlockSpec((ROW_TILE, d), lambda i: (i, 0))
    vec = pl.BlockSpec((1, d), lambda i: (0, 0))
    return _pcall(body, name=name, grid=(s // ROW_TILE,), in_specs=[row, vec, row, row], out_specs=[row, vec],
                  out_shape=[jax.ShapeDtypeStruct((s, d), F32), jax.ShapeDtypeStruct((1, d), F32)],
                  compiler_params=_params("arbitrary"))(x, g, dh, dres)


def _loss_head(x, g, target, name):
    s, d = x.shape

    def body(x_ref, g_ref, t_ref, dx_ref, dg_ref, loss_ref):
        xf = x_ref[...]
        r = lax.rsqrt(jnp.mean(xf * xf, axis=-1, keepdims=True) + EPS)
        xhat = xf * r
        gv = g_ref[...]
        err = xhat * gv - t_ref[...]
        dy = err * (1.0 / d)
        dxh = dy * gv
        proj = jnp.mean(dxh * xhat, axis=-1, keepdims=True)
        dx_ref[...] = r * (dxh - xhat * proj)
        dg_part = jnp.sum(dy * xhat, axis=0, keepdims=True)
        row_loss = jnp.sum(err * err, axis=-1, keepdims=True) * (0.5 / d)
        loss_part = jnp.broadcast_to(jnp.sum(row_loss, axis=0, keepdims=True), (1, 128))

        @pl.when(pl.program_id(0) == 0)
        def _():
            dg_ref[...] = dg_part
            loss_ref[...] = loss_part

        @pl.when(pl.program_id(0) > 0)
        def _():
            dg_ref[...] += dg_part
            loss_ref[...] += loss_part

    row = pl.BlockSpec((ROW_TILE, d), lambda i: (i, 0))
    vec = pl.BlockSpec((1, d), lambda i: (0, 0))
    one = pl.BlockSpec((1, 128), lambda i: (0, 0))
    return _pcall(body, name=name, grid=(s // ROW_TILE,), in_specs=[row, vec, row], out_specs=[row, vec, one],
                  out_shape=[jax.ShapeDtypeStruct((s, d), F32), jax.ShapeDtypeStruct((1, d), F32),
                             jax.ShapeDtypeStruct((1, 128), F32)],
                  compiler_params=_params("arbitrary"))(x, g, target)


HEADS_PER_STEP = 2
GROUP_W = HEADS_PER_STEP * HEAD
N_GROUPS = N_HEADS // HEADS_PER_STEP


def _head_cols(h):
    return slice(h * HEAD, (h + 1) * HEAD)


def _head_specs(s, col0):
    t = ATT_BLOCK
    g0 = [c // HEADS_PER_STEP for c in col0]
    qspec = pl.BlockSpec((t, GROUP_W), lambda g, i: (i, g0[0] + g))
    kspec = pl.BlockSpec((s, GROUP_W), lambda g, i: (0, g0[1] + g))
    vspec = pl.BlockSpec((s, GROUP_W), lambda g, i: (0, g0[2] + g))
    return qspec, kspec, vspec


TRI = 256


def _order_matrix(later):
    r, c = lax.broadcasted_iota(jnp.int32, (TRI, TRI), 0), lax.broadcasted_iota(jnp.int32, (TRI, TRI), 1)
    return (r > c if later else r < c).astype(BF16)


def _exact_dot(x, m, later):
    parts = [x[:, c:c + TRI] for c in range(0, x.shape[1], TRI)]
    totals = [jnp.sum(p, axis=1, keepdims=True) for p in parts] if len(parts) > 1 else None
    out = []
    for j, p in enumerate(parts):
        hi = p.astype(BF16)
        lo = (p - hi.astype(F32)).astype(BF16)
        acc = _dot_nn(hi, m) + _dot_nn(lo, m)
        for other in (range(j + 1, len(parts)) if later else range(j)):
            acc = acc + totals[other]
        out.append(acc)
    return out[0] if len(out) == 1 else jnp.concatenate(out, axis=1)


def _sb_block(q, kblk, carry_l, u, diagonal):
    t = ATT_BLOCK
    z = _dot_nt(q, kblk) * ATT_SCALE
    sp = jnp.maximum(z, 0.0) + jnp.log(1.0 + jnp.exp(-jnp.abs(z)))
    if not diagonal:
        l = -sp
        return z, None, l, jnp.exp(z + l + _exact_dot(l, u, True) + carry_l)
    mask = lax.broadcasted_iota(jnp.int32, (t, t), 1) < lax.broadcasted_iota(jnp.int32, (t, t), 0)
    l = jnp.where(mask, -sp, 0.0)
    a = jnp.where(mask, jnp.exp(z - sp + _exact_dot(l, u, True) + carry_l), 0.0)
    return z, mask, l, a


def _sb_carry_spec(s):
    t = ATT_BLOCK
    return pl.BlockSpec((HEADS_PER_STEP, None, s // t, t, 1), lambda g, i: (g, i, 0, 0, 0))


def _sb_fwd(p, name, after=()):
    s = p.shape[0]
    t = ATT_BLOCK
    nb = s // t

    def body(q_ref, k_ref, v_ref, o_ref, cl_ref):
        i = pl.program_id(1)
        heads = range(HEADS_PER_STEP)
        q = [q_ref[:, _head_cols(h)].astype(BF16) for h in heads]
        u = _order_matrix(True)
        cl_ref[...] = jnp.zeros_like(cl_ref)

        def tile(kb, carry, diagonal):
            ks = pl.multiple_of(kb * t, t)
            out = []
            for h in heads:
                acc, carry_l = carry[h]
                kblk = k_ref[pl.ds(ks, t), _head_cols(h)].astype(BF16)
                vblk = v_ref[pl.ds(ks, t), _head_cols(h)].astype(BF16)
                cl_ref[h, kb] = carry_l
                _, _, l, a = _sb_block(q[h], kblk, carry_l, u, diagonal)
                out.append((acc + _dot_nn(a.astype(BF16), vblk), carry_l + jnp.sum(l, axis=1, keepdims=True)))
            return tuple(out)

        carry = tile(i, tuple((jnp.zeros((t, HEAD), F32), jnp.zeros((t, 1), F32)) for _ in heads), True)
        carry = lax.fori_loop(0, i, lambda n, c: tile(i - 1 - n, c, False), carry)
        for h in heads:
            o_ref[:, _head_cols(h)] = carry[h][0]

    qspec, kspec, vspec = _head_specs(s, (0, N_HEADS, 2 * N_HEADS))
    ospec = pl.BlockSpec((t, GROUP_W), lambda g, i: (i, g))
    return _pcall(body, after=after, name=name, grid=(N_GROUPS, nb), in_specs=[qspec, kspec, vspec],
                  out_specs=[ospec, _sb_carry_spec(s)],
                  out_shape=[jax.ShapeDtypeStruct((s, HALF), F32), jax.ShapeDtypeStruct((N_HEADS, nb, nb, t, 1), F32)],
                  compiler_params=_params("parallel", "parallel"))(p, p, p)


def _sb_bwd(p, d_ab, carries, name):
    s = p.shape[0]
    t = ATT_BLOCK

    def body(q_ref, k_ref, v_ref, do_ref, cl_ref, dq_ref, dk_ref, dv_ref):
        i = pl.program_id(1)

        @pl.when(i == 0)
        def _():
            dk_ref[...] = jnp.zeros_like(dk_ref)
            dv_ref[...] = jnp.zeros_like(dv_ref)

        heads = range(HEADS_PER_STEP)
        q = [q_ref[:, _head_cols(h)].astype(BF16) for h in heads]
        do = [do_ref[:, _head_cols(h)].astype(BF16) for h in heads]
        u = _order_matrix(True)
        lower = _order_matrix(False)

        def tile(kb, carry, diagonal):
            ks = pl.multiple_of(kb * t, t)
            out = []
            for h in heads:
                dq, carry_g = carry[h]
                kblk = k_ref[pl.ds(ks, t), _head_cols(h)].astype(BF16)
                vblk = v_ref[pl.ds(ks, t), _head_cols(h)].astype(BF16)
                z, mask, _, a = _sb_block(q[h], kblk, cl_ref[h, kb], u, diagonal)
                g = a * _dot_nt(do[h], vblk)
                earlier_g = _exact_dot(g, lower, False) + carry_g
                sig = jax.nn.sigmoid(z)
                dz = g * (1.0 - sig) - sig * earlier_g
                if diagonal:
                    dz = jnp.where(mask, dz, 0.0)
                dz = dz.astype(BF16)
                dv_ref[pl.ds(ks, t), _head_cols(h)] += _dot_tn(a.astype(BF16), do[h])
                dk_ref[pl.ds(ks, t), _head_cols(h)] += _dot_tn(dz, q[h]) * ATT_SCALE
                out.append((dq + _dot_nn(dz, kblk) * ATT_SCALE, carry_g + jnp.sum(g, axis=1, keepdims=True)))
            return tuple(out)

        init = tuple((jnp.zeros((t, HEAD), F32), jnp.zeros((t, 1), F32)) for _ in heads)
        carry = tile(i, lax.fori_loop(0, i, lambda kb, c: tile(kb, c, False), init), True)
        for h in heads:
            dq_ref[:, _head_cols(h)] = carry[h][0]

    qspec, kspec, vspec = _head_specs(s, (0, N_HEADS, 2 * N_HEADS))
    blk = pl.BlockSpec((t, GROUP_W), lambda g, i: (i, g))
    whole = pl.BlockSpec((s, GROUP_W), lambda g, i: (0, g))
    shape = jax.ShapeDtypeStruct((s, HALF), F32)
    return _pcall(body, name=name, grid=(N_GROUPS, s // t), in_specs=[qspec, kspec, vspec, blk, _sb_carry_spec(s)],
                  out_specs=[blk, whole, whole], out_shape=[shape, shape, shape],
                  compiler_params=_params("parallel", "arbitrary"))(p, p, p, d_ab, carries)


COL_TILE = 256


def _sc_fwd(p, w, name):
    s = p.shape[0]
    nb = HALF // COL_TILE

    def body(gb_ref, gc_ref, h_ref, w_ref, o_ref):
        conv = _conv3(gc_ref[...] * h_ref[...], w_ref[...])
        o_ref[...] = (gb_ref[...] * conv).astype(BF16)

    def col(k):
        return pl.BlockSpec((s, COL_TILE), lambda j: (0, k * nb + j))

    wspec = pl.BlockSpec((3, COL_TILE), lambda j: (0, j))
    return _pcall(body, name=name, grid=(nb,), in_specs=[col(3), col(4), col(5), wspec], out_specs=col(0),
                  out_shape=jax.ShapeDtypeStruct((s, HALF), BF16), compiler_params=_params("parallel"))(p, p, p, w)


def _sc_bwd(p, w, d_ab, name):
    s = p.shape[0]
    nb = HALF // COL_TILE

    def body(gb_ref, gc_ref, h_ref, w_ref, d_ref, dgb_ref, dgc_ref, dh_ref, dw_ref):
        gc, hin, wv, d = gc_ref[...], h_ref[...], w_ref[...], d_ref[...]
        sig = gc * hin
        shifted = (_shift_down(sig, 1), _shift_down(sig, 2))
        dgb_ref[...] = d * _conv3(sig, wv, shifted)
        dconv = d * gb_ref[...]
        _conv3_wgrad(dconv, sig, shifted, dw_ref)
        dsig = _conv3_transpose(dconv, wv)
        dgc_ref[...] = dsig * hin
        dh_ref[...] = dsig * gc

    def col(k):
        return pl.BlockSpec((s, COL_TILE), lambda j: (0, k * nb + j))

    wspec = pl.BlockSpec((3, COL_TILE), lambda j: (0, j))
    act = jax.ShapeDtypeStruct((s, HALF), F32)
    return _pcall(body, name=name, grid=(nb,), in_specs=[col(3), col(4), col(5), wspec, col(1)],
                  out_specs=[col(0), col(0), col(0), wspec],
                  out_shape=[act, act, act, jax.ShapeDtypeStruct((3, HALF), F32)],
                  compiler_params=_params("parallel"))(p, p, p, w, d_ab)


def _ffn_act_fwd(u, w, name, after=()):
    s = u.shape[0]
    nb = D_FF // COL_TILE

    def body(ug_ref, uu_ref, wg_ref, wu_ref, o_ref):
        gate = _conv3(ug_ref[...], wg_ref[...])
        up = _conv3(uu_ref[...], wu_ref[...])
        o_ref[...] = (gate * jax.nn.sigmoid(gate) * up).astype(BF16)

    def col(k):
        return pl.BlockSpec((s, COL_TILE), lambda j: (0, k * nb + j))

    def wcol(k):
        return pl.BlockSpec((3, COL_TILE), lambda j: (0, k * nb + j))

    return _pcall(body, after=after, name=name, grid=(nb,), in_specs=[col(0), col(1), wcol(0), wcol(1)], out_specs=col(0),
                  out_shape=jax.ShapeDtypeStruct((s, D_FF), BF16),
                  compiler_params=_params("parallel"))(u, u, w, w)


def _ffn_act_bwd(u, w, d_f, name):
    s = u.shape[0]
    nb = D_FF // COL_TILE

    def body(ug_ref, uu_ref, wg_ref, wu_ref, d_ref, du_ref, dwg_ref, dwu_ref):
        ug, uu, wg, wu, d = ug_ref[...], uu_ref[...], wg_ref[...], wu_ref[...], d_ref[...]
        ug_shifted = (_shift_down(ug, 1), _shift_down(ug, 2))
        uu_shifted = (_shift_down(uu, 1), _shift_down(uu, 2))
        gate = _conv3(ug, wg, ug_shifted)
        up = _conv3(uu, wu, uu_shifted)
        sig = jax.nn.sigmoid(gate)
        d_up = d * gate * sig
        d_gate = d * up * sig * (1.0 + gate * (1.0 - sig))
        _conv3_wgrad(d_gate, ug, ug_shifted, dwg_ref)
        _conv3_wgrad(d_up, uu, uu_shifted, dwu_ref)
        du_ref[0] = _conv3_transpose(d_gate, wg).astype(BF16)
        du_ref[1] = _conv3_transpose(d_up, wu).astype(BF16)

    def col(k):
        return pl.BlockSpec((s, COL_TILE), lambda j: (0, k * nb + j))

    def wcol(k):
        return pl.BlockSpec((3, COL_TILE), lambda j: (0, k * nb + j))

    both = pl.BlockSpec((2, s, COL_TILE), lambda j: (0, 0, j))
    wsh = jax.ShapeDtypeStruct((3, D_FF), F32)
    return _pcall(body, name=name, grid=(nb,), in_specs=[col(0), col(1), wcol(0), wcol(1), col(0)],
                  out_specs=[both, wcol(0), wcol(0)], out_shape=[jax.ShapeDtypeStruct((2, s, D_FF), BF16), wsh, wsh],
                  compiler_params=_params("parallel"))(u, u, w, w, d_f)


def _sg_common(u, v, g, w_ref, bias, mixed_ref):
    rows = u.shape[0]
    gu = _gelu(u)
    gv = _gelu(v)
    xc = gv - jnp.mean(gv, axis=-1, keepdims=True)
    rstd = lax.rsqrt(jnp.mean(xc * xc, axis=-1, keepdims=True) + EPS)
    xhat = xc * rstd
    vn = xhat * g
    tril = lax.broadcasted_iota(jnp.int32, (HEAD, HEAD), 0) >= lax.broadcasted_iota(jnp.int32, (HEAD, HEAD), 1)
    wts = [jnp.where(tril, w_ref[grp], 0.0).astype(BF16) for grp in range(N_HEADS)]
    for n in range(rows // HEAD):
        for grp in range(N_HEADS):
            blk = vn[n * HEAD:(n + 1) * HEAD, grp * HEAD:(grp + 1) * HEAD].astype(BF16)
            mixed_ref[n * HEAD:(n + 1) * HEAD, grp * HEAD:(grp + 1) * HEAD] = _dot_nn(wts[grp], blk)
    mixed = mixed_ref[...] + jnp.concatenate([bias] * (rows // HEAD), axis=0)
    return gu, xhat, rstd, vn, mixed, wts, tril


def _sg_fwd(p, sg_w, bias, g, name):
    s = p.shape[0]

    def body(u_ref, v_ref, w_ref, b_ref, g_ref, o_ref, mixed_ref):
        gu, _, _, _, mixed, _, _ = _sg_common(u_ref[...], v_ref[...], g_ref[...], w_ref, b_ref[...], mixed_ref)
        o_ref[...] = (gu * mixed).astype(BF16)

    def half(k):
        return pl.BlockSpec((ROW_TILE, HALF), lambda i: (i, k))

    wspec = pl.BlockSpec((N_HEADS, HEAD, HEAD), lambda i: (0, 0, 0))
    bspec = pl.BlockSpec((HEAD, HALF), lambda i: (0, 0))
    gspec = pl.BlockSpec((1, HALF), lambda i: (0, 0))
    return _pcall(body, name=name, grid=(s // ROW_TILE,), in_specs=[half(0), half(1), wspec, bspec, gspec],
                  out_specs=half(0), out_shape=jax.ShapeDtypeStruct((s, HALF), BF16),
                  scratch_shapes=[pltpu.VMEM((ROW_TILE, HALF), F32)],
                  compiler_params=_params("parallel"))(p, p, sg_w, bias, g)


def _sg_bwd(p, sg_w, bias, g, d_cd, name):
    s = p.shape[0]
    nsteps = s // ROW_TILE

    def body(u_ref, v_ref, w_ref, b_ref, g_ref, d_ref, du_ref, dv_ref, dw_ref, db_ref, dg_ref,
             mixed_ref, dvn_ref, dbias_ref):
        i = pl.program_id(0)
        u, v, gain, d = u_ref[...], v_ref[...], g_ref[...], d_ref[...]
        gu, xhat, rstd, vn, mixed, wts, tril = _sg_common(u, v, gain, w_ref, b_ref[...], mixed_ref)

        @pl.when(i == 0)
        def _():
            dw_ref[...] = jnp.zeros_like(dw_ref)
            dg_ref[...] = jnp.zeros_like(dg_ref)
            dbias_ref[...] = jnp.zeros_like(dbias_ref)

        du_ref[...] = d * mixed * _gelu_grad(u)
        dm = d * gu
        for n in range(ROW_TILE // HEAD):
            rs = slice(n * HEAD, (n + 1) * HEAD)
            dbias_ref[...] += dm[rs, :]
            for grp in range(N_HEADS):
                cs = slice(grp * HEAD, (grp + 1) * HEAD)
                dm_blk = dm[rs, cs].astype(BF16)
                dw_ref[grp] += jnp.where(tril, _dot_nt(dm_blk, vn[rs, cs].astype(BF16)), 0.0)
                dvn_ref[rs, cs] = _dot_tn(wts[grp], dm_blk)
        dvn = dvn_ref[...]
        dg_ref[...] += jnp.sum(dvn * xhat, axis=0, keepdims=True)
        dxh = dvn * gain
        d_gv = rstd * (dxh - jnp.mean(dxh, axis=-1, keepdims=True) - xhat * jnp.mean(dxh * xhat, axis=-1, keepdims=True))
        dv_ref[...] = d_gv * _gelu_grad(v)

        @pl.when(i == nsteps - 1)
        def _():
            lane = lax.broadcasted_iota(jnp.int32, (HEAD, HEAD), 1)
            out = jnp.zeros((HEAD, HEAD), F32)
            for grp in range(N_HEADS):
                tot = jnp.sum(dbias_ref[:, grp * HEAD:(grp + 1) * HEAD], axis=1, keepdims=True)
                out = out + jnp.where(lane == grp, tot, 0.0)
            db_ref[...] = out

    def half(k):
        return pl.BlockSpec((ROW_TILE, HALF), lambda i: (i, k))

    wspec = pl.BlockSpec((N_HEADS, HEAD, HEAD), lambda i: (0, 0, 0))
    bspec = pl.BlockSpec((HEAD, HALF), lambda i: (0, 0))
    gspec = pl.BlockSpec((1, HALF), lambda i: (0, 0))
    dbspec = pl.BlockSpec((HEAD, HEAD), lambda i: (0, 0))
    act = jax.ShapeDtypeStruct((s, HALF), F32)
    return _pcall(body, name=name, grid=(nsteps,), in_specs=[half(0), half(1), wspec, bspec, gspec, half(0)],
                  out_specs=[half(0), half(0), wspec, dbspec, gspec],
                  out_shape=[act, act, jax.ShapeDtypeStruct((N_HEADS, HEAD, HEAD), F32),
                             jax.ShapeDtypeStruct((HEAD, HEAD), F32), jax.ShapeDtypeStruct((1, HALF), F32)],
                  scratch_shapes=[pltpu.VMEM((ROW_TILE, HALF), F32), pltpu.VMEM((ROW_TILE, HALF), F32),
                                  pltpu.VMEM((HEAD, HALF), F32)],
                  compiler_params=_params("arbitrary"))(p, p, sg_w, bias, g, d_cd)


def _fox_prep(f, b, name):
    s = f.shape[0]
    t = ATT_BLOCK

    def body(f_ref, b_ref, c_ref):
        tri = (lax.broadcasted_iota(jnp.int32, (t, t), 0) >= lax.broadcasted_iota(jnp.int32, (t, t), 1)).astype(BF16)
        carry = jnp.zeros((1, 128), F32)
        for n in range(s // t):
            lf = _log_sigmoid(f_ref[n * t:(n + 1) * t, :] + b_ref[...])
            hi, mid, lo = _split3(lf)
            c_ref[n * t:(n + 1) * t, :] = _dot_nn(tri, hi) + _dot_nn(tri, mid) + _dot_nn(tri, lo) + carry
            carry = carry + jnp.sum(lf, axis=0, keepdims=True)

    return _pcall(body, name=name, in_specs=[VMEM_SPEC, VMEM_SPEC], out_specs=VMEM_SPEC,
                  out_shape=jax.ShapeDtypeStruct((s, 128), F32))(f, b)


def _fox_post(drow, dcol, f, b, name):
    s = f.shape[0]
    t = ATT_BLOCK

    def body(drow_ref, dcol_ref, f_ref, b_ref, df_ref, db_ref):
        tri = (lax.broadcasted_iota(jnp.int32, (t, t), 1) >= lax.broadcasted_iota(jnp.int32, (t, t), 0)).astype(BF16)
        carry = jnp.zeros((1, 128), F32)
        db = jnp.zeros((1, 128), F32)
        for n in reversed(range(s // t)):
            rs = slice(n * t, (n + 1) * t)
            dc = drow_ref[rs, :] - dcol_ref[rs, :]
            hi, mid, lo = _split3(dc)
            dlogf = _dot_nn(tri, hi) + _dot_nn(tri, mid) + _dot_nn(tri, lo) + carry
            carry = carry + jnp.sum(dc, axis=0, keepdims=True)
            df = dlogf * jax.nn.sigmoid(-(f_ref[rs, :] + b_ref[...]))
            df_ref[rs, :] = df
            db = db + jnp.sum(df, axis=0, keepdims=True)
        db_ref[...] = db

    return _pcall(body, name=name, in_specs=[VMEM_SPEC] * 4, out_specs=[VMEM_SPEC, VMEM_SPEC],
                  out_shape=[jax.ShapeDtypeStruct((s, 128), F32), jax.ShapeDtypeStruct((1, 128), F32)])(drow, dcol, f, b)


def _fox_specs(s):
    t = ATT_BLOCK
    ccol = pl.BlockSpec((HEADS_PER_STEP, t, 1), lambda g, i: (g, i, 0))
    crow = pl.BlockSpec((HEADS_PER_STEP, s // t, 1, t), lambda g, i: (g, 0, 0, 0))
    return ccol, crow


def _fox_fwd(p, c_col, c_row, name, after=()):
    s = p.shape[0]
    t = ATT_BLOCK

    def body(q_ref, k_ref, v_ref, cc_ref, cr_ref, o_ref, lse_ref):
        i = pl.program_id(1)
        heads = range(HEADS_PER_STEP)
        q = [q_ref[:, _head_cols(h)].astype(BF16) for h in heads]
        ct = [cc_ref[h] for h in heads]

        def tile(n, carry, diagonal):
            ks = pl.multiple_of(n * t, t)
            out = []
            for h in heads:
                acc, m, l = carry[h]
                kblk = k_ref[pl.ds(ks, t), _head_cols(h)].astype(BF16)
                vblk = v_ref[pl.ds(ks, t), _head_cols(h)].astype(BF16)
                logit = _dot_nt(q[h], kblk) * ATT_SCALE + ct[h] - cr_ref[h, n]
                if diagonal:
                    causal = lax.broadcasted_iota(jnp.int32, (t, t), 1) <= lax.broadcasted_iota(jnp.int32, (t, t), 0)
                    logit = jnp.where(causal, logit, NEG)
                m_new = jnp.maximum(m, jnp.max(logit, axis=1, keepdims=True))
                alpha = jnp.exp(m - m_new)
                pr = jnp.exp(logit - m_new)
                l = alpha * l + jnp.sum(pr, axis=1, keepdims=True)
                out.append((alpha * acc + _dot_nn(pr.astype(BF16), vblk), m_new, l))
            return tuple(out)

        init = tuple((jnp.zeros((t, HEAD), F32), jnp.full((t, 1), NEG, F32), jnp.zeros((t, 1), F32)) for _ in heads)
        carry = tile(i, lax.fori_loop(0, i, lambda n, c: tile(n, c, False), init), True)
        for h in heads:
            acc, m, l = carry[h]
            o_ref[:, _head_cols(h)] = acc / l
            lse_ref[h] = m + jnp.log(l)

    qspec, kspec, vspec = _head_specs(s, (2 * N_HEADS, 3 * N_HEADS, 4 * N_HEADS))
    ccol, crow = _fox_specs(s)
    ospec = pl.BlockSpec((t, GROUP_W), lambda g, i: (i, g))
    return _pcall(body, after=after, name=name, grid=(N_GROUPS, s // t), in_specs=[qspec, kspec, vspec, ccol, crow],
                  out_specs=[ospec, ccol],
                  out_shape=[jax.ShapeDtypeStruct((s, HALF), F32), jax.ShapeDtypeStruct((N_HEADS, s, 1), F32)],
                  compiler_params=_params("parallel", "parallel"))(p, p, p, c_col, c_row)


def _fox_bwd(p, c_col, c_row, lse, d_cd, d_out, name):
    s = p.shape[0]
    t = ATT_BLOCK

    def body(q_ref, k_ref, v_ref, cc_ref, cr_ref, lse_ref, do_ref, o_ref, dq_ref, dk_ref, dv_ref, dcol_ref, drow_ref):
        i = pl.program_id(1)

        @pl.when(i == 0)
        def _():
            dk_ref[...] = jnp.zeros_like(dk_ref)
            dv_ref[...] = jnp.zeros_like(dv_ref)
            dcol_ref[...] = jnp.zeros_like(dcol_ref)

        heads = range(HEADS_PER_STEP)
        q = [q_ref[:, _head_cols(h)].astype(BF16) for h in heads]
        do = [do_ref[:, _head_cols(h)].astype(BF16) for h in heads]
        delta = [jnp.sum(do_ref[:, _head_cols(h)] * o_ref[:, _head_cols(h)], axis=1, keepdims=True) for h in heads]
        ct = [cc_ref[h] for h in heads]
        lse_v = [lse_ref[h] for h in heads]
        ones = jnp.ones((t, HEAD), BF16)

        def tile(n, carry, diagonal):
            ks = pl.multiple_of(n * t, t)
            out = []
            for h in heads:
                dq, drow = carry[h]
                kblk = k_ref[pl.ds(ks, t), _head_cols(h)].astype(BF16)
                vblk = v_ref[pl.ds(ks, t), _head_cols(h)].astype(BF16)
                logit = _dot_nt(q[h], kblk) * ATT_SCALE + ct[h] - cr_ref[h, n]
                pr = jnp.exp(logit - lse_v[h])
                if diagonal:
                    causal = lax.broadcasted_iota(jnp.int32, (t, t), 1) <= lax.broadcasted_iota(jnp.int32, (t, t), 0)
                    pr = jnp.where(causal, pr, 0.0)
                ds = pr * (_dot_nt(do[h], vblk) - delta[h])
                dsb = ds.astype(BF16)
                dv_ref[pl.ds(ks, t), _head_cols(h)] += _dot_tn(pr.astype(BF16), do[h])
                dk_ref[pl.ds(ks, t), _head_cols(h)] += _dot_tn(dsb, q[h]) * ATT_SCALE
                dcol_ref[pl.ds(ks, t), _head_cols(h)] += _dot_tn(dsb, ones)
                out.append((dq + _dot_nn(dsb, kblk) * ATT_SCALE,
                            drow + jnp.sum(dsb.astype(F32), axis=1, keepdims=True)))
            return tuple(out)

        init = tuple((jnp.zeros((t, HEAD), F32), jnp.zeros((t, 1), F32)) for _ in heads)
        carry = tile(i, lax.fori_loop(0, i, lambda n, c: tile(n, c, False), init), True)
        for h in heads:
            dq_ref[:, _head_cols(h)] = carry[h][0]
            drow_ref[h] = carry[h][1]

    qspec, kspec, vspec = _head_specs(s, (2 * N_HEADS, 3 * N_HEADS, 4 * N_HEADS))
    ccol, crow = _fox_specs(s)
    dospec = pl.BlockSpec((t, GROUP_W), lambda g, i: (i, N_GROUPS + g))
    blk = pl.BlockSpec((t, GROUP_W), lambda g, i: (i, g))
    whole = pl.BlockSpec((s, GROUP_W), lambda g, i: (0, g))
    shape = jax.ShapeDtypeStruct((s, HALF), F32)
    return _pcall(body, name=name, grid=(N_GROUPS, s // t),
                  in_specs=[qspec, kspec, vspec, ccol, crow, ccol, dospec, blk],
                  out_specs=[blk, whole, whole, whole, ccol],
                  out_shape=[shape, shape, shape, shape, jax.ShapeDtypeStruct((N_HEADS, s, 1), F32)],
                  compiler_params=_params("parallel", "arbitrary"))(p, p, p, c_col, c_row, lse, d_cd, d_out)


def _row_tile(rows, cap):
    for t in (256, 128, 64, 32, 16, 8):
        if t <= cap and rows % t == 0:
            return t
    return rows


def _adamw(w, g, m, v, name):
    rows, cols = w.shape
    tr = _row_tile(rows, 128)
    c1 = 1.0 / (1.0 - ADAM_B1 ** ADAM_STEP)
    c2 = 1.0 / (1.0 - ADAM_B2 ** ADAM_STEP)

    def body(w_ref, g_ref, m_ref, v_ref, d_ref, nm_ref, nv_ref):
        gv = g_ref[...]
        nm = ADAM_B1 * m_ref[...] + (1.0 - ADAM_B1) * gv
        nv = ADAM_B2 * v_ref[...] + (1.0 - ADAM_B2) * (gv * gv)
        nm_ref[...] = nm
        nv_ref[...] = nv
        d_ref[...] = -ADAM_LR * ((nm * c1) / (jnp.sqrt(nv * c2) + ADAM_EPS) + ADAM_WD * w_ref[...])

    spec = pl.BlockSpec((tr, cols), lambda i: (i, 0))
    shape = jax.ShapeDtypeStruct((rows, cols), F32)
    return _pcall(body, name=name, grid=(rows // tr,), in_specs=[spec] * 4, out_specs=[spec] * 3,
                  out_shape=[shape] * 3, compiler_params=_params("parallel"))(w, g, m, v)


def _half_shape(whole_shape, kind):
    if kind == "col":
        return (whole_shape[0] // 2, whole_shape[1] // 4)
    if kind == "row":
        return (whole_shape[0] // 8, whole_shape[1])
    return (whole_shape[1] // 2, whole_shape[2])


def _own_half_spec(whole_shape, kind, tr):
    hr, hc = _half_shape(whole_shape, kind)
    nb = hr // tr
    if kind == "col":
        return pl.BlockSpec((tr, hc), lambda i, pos: (pos[1] * nb + i, pos[0]))
    if kind == "row":
        return pl.BlockSpec((tr, hc), lambda i, pos: ((2 * pos[0] + pos[1]) * nb + i, 0))
    return pl.BlockSpec((None, tr, hc), lambda i, pos: (pos[0], pos[1] * nb + i, 0))


def _sum_partials(pos, grad, landed, kind, name):
    hr, hc = _half_shape(grad.shape, kind)
    tr = _row_tile(hr, 64)

    def body(pos_ref, g_ref, p_ref, o_ref):
        acc = g_ref[...].astype(F32)
        for k in range(N_DEV - 1):
            acc = acc + p_ref[k].astype(F32)
        o_ref[...] = acc

    grid_spec = pltpu.PrefetchScalarGridSpec(
        num_scalar_prefetch=1, grid=(hr // tr,),
        in_specs=[_own_half_spec(grad.shape, kind, tr), pl.BlockSpec((N_DEV - 1, tr, hc), lambda i, pos: (0, i, 0))],
        out_specs=pl.BlockSpec((tr, hc), lambda i, pos: (i, 0)))
    return _pcall(body, name=name, grid_spec=grid_spec, out_shape=jax.ShapeDtypeStruct((hr, hc), F32),
                  compiler_params=_params("parallel"))(pos, grad, landed)


def _adamw_shard(pos, w, g_mine, g_sibling, m, v, name):
    hr, hc = g_mine.shape
    tr = _row_tile(hr, 128)
    nb = hr // tr
    c1 = 1.0 / (1.0 - ADAM_B1 ** ADAM_STEP)
    c2 = 1.0 / (1.0 - ADAM_B2 ** ADAM_STEP)

    def body(pos_ref, w_ref, gm_ref, gs_ref, m_ref, v_ref, g_ref, d_ref, nm_ref, nv_ref):
        mine = (pl.program_id(0) // nb) == pos_ref[1]
        gv = jnp.where(mine, gm_ref[...], gs_ref[...])
        nm = ADAM_B1 * m_ref[...] + (1.0 - ADAM_B1) * gv
        nv = ADAM_B2 * v_ref[...] + (1.0 - ADAM_B2) * (gv * gv)
        g_ref[...] = gv
        nm_ref[...] = nm
        nv_ref[...] = nv
        d_ref[...] = -ADAM_LR * ((nm * c1) / (jnp.sqrt(nv * c2) + ADAM_EPS) + ADAM_WD * w_ref[...])

    full = pl.BlockSpec((tr, hc), lambda i, pos: (i, 0))
    mine_spec = pl.BlockSpec((tr, hc), lambda i, pos: (jnp.clip(i - pos[1] * nb, 0, nb - 1), 0))
    sib_spec = pl.BlockSpec((tr, hc), lambda i, pos: (jnp.clip(i - (1 - pos[1]) * nb, 0, nb - 1), 0))
    grid_spec = pltpu.PrefetchScalarGridSpec(
        num_scalar_prefetch=1, grid=(2 * nb,), in_specs=[full, mine_spec, sib_spec, full, full], out_specs=[full] * 4)
    shape = jax.ShapeDtypeStruct((2 * hr, hc), F32)
    return _pcall(body, name=name, grid_spec=grid_spec, out_shape=[shape] * 4,
                  compiler_params=_params("parallel"))(pos, w, g_mine, g_sibling, m, v)


def _place_shard(pos, shard, kind, name, after=()):
    rows, cols = shard.shape
    tr = _row_tile(rows, 256)
    nb = rows // tr
    if kind == "col":
        out_spec = pl.BlockSpec((tr, cols), lambda i, pos: (i, pos[0]))
    elif kind == "row":
        out_spec = pl.BlockSpec((tr, cols), lambda i, pos: (pos[0] * nb + i, 0))
    else:
        out_spec = pl.BlockSpec((None, tr, cols), lambda i, pos: (pos[0], i, 0))

    def body(pos_ref, s_ref, *rest):
        rest[-1][...] = s_ref[...].astype(BF16)

    grid_spec = pltpu.PrefetchScalarGridSpec(
        num_scalar_prefetch=1, grid=(nb,),
        in_specs=[pl.BlockSpec((tr, cols), lambda i, pos: (i, 0))] + [pl.BlockSpec(memory_space=pl.ANY)] * len(after),
        out_specs=out_spec)
    return _pcall(body, name=name, grid_spec=grid_spec,
                  out_shape=jax.ShapeDtypeStruct(_whole_shape(shard.shape, kind), BF16),
                  compiler_params=_params("parallel"))(pos, shard, *after)


N_DEV = 8
RELATIONS = [(r >> 2 & 1, r >> 1 & 1, r & 1) for r in range(1, N_DEV)]


def _position():
    return lax.axis_index("x"), lax.axis_index("y"), lax.axis_index("c")


def _related(pos, rel):
    return tuple(1 - p if f else p for p, f in zip(pos, rel))


def _index(pos):
    return 4 * pos[0] + 2 * pos[1] + pos[2]


def _window(ref, kind, pos):
    px, py, pc = pos
    j = 2 * px + py
    if kind == "col":
        r, c = ref.shape
        return ref.at[pl.ds(pc * (r // 2), r // 2), pl.ds(pl.multiple_of(j * (c // 4), 128), c // 4)]
    if kind == "row":
        rj = ref.shape[0] // 4
        return ref.at[pl.ds(j * rj + pc * (rj // 2), rj // 2), :]
    r = ref.shape[1]
    return ref.at[j, pl.ds(pc * (r // 2), r // 2), :]


def _whole_shape(shard_shape, kind):
    r, c = shard_shape
    return {"col": (r, 4 * c), "row": (4 * r, c), "maj": (4, r, c)}[kind]


SEM_SPEC = pl.BlockSpec(memory_space=pltpu.SEMAPHORE)
ANY_SPEC = pl.BlockSpec(memory_space=pl.ANY)
DATAFLOW = pltpu.SideEffectType.DATAFLOW_SIDE_EFFECTING
TOKEN = jax.ShapeDtypeStruct((8, 128), F32)


def _hbm(a):
    return pltpu.with_memory_space_constraint(a, pltpu.HBM)


def _chips(x, y):
    return [(1 - x, y), (x, 1 - y), (1 - x, 1 - y)]


def _split_start(body, name, buffers, n_sems, after=()):
    n = len(buffers)

    def wrapped(*refs):
        body(refs[:n], refs[n], refs[n + 1])
        refs[-1][...] = jnp.zeros_like(refs[-1])

    outs = _pcall(
        wrapped, after=after, name=name, in_specs=[HBM_SPEC] * n,
        out_specs=[SEM_SPEC, SEM_SPEC] + [HBM_SPEC] * n + [VMEM_SPEC],
        out_shape=[pltpu.SemaphoreType.DMA(n_sems), pltpu.SemaphoreType.DMA(n_sems)]
        + [pltpu.HBM(b.shape, b.dtype) for b in buffers] + [TOKEN],
        input_output_aliases={i: 2 + i for i in range(n)},
        compiler_params=pltpu.CompilerParams(has_side_effects=DATAFLOW))(*[_hbm(b) for b in buffers])
    return outs[0], outs[1], list(outs[2:2 + n]), outs[2 + n]


def _split_wait(body, name, buffers, send_sems, recv_sems, after):
    n = len(buffers)
    after = list(after) if isinstance(after, (list, tuple)) else [after]

    def wrapped(*refs):
        body(refs[:n], refs[n], refs[n + 1])

    outs = _pcall(
        wrapped, name=name, in_specs=[HBM_SPEC] * n + [SEM_SPEC, SEM_SPEC] + [ANY_SPEC] * len(after),
        out_specs=[HBM_SPEC] * n, out_shape=[pltpu.HBM(b.shape, b.dtype) for b in buffers],
        input_output_aliases={i: i for i in range(n)},
        compiler_params=pltpu.CompilerParams(has_side_effects=DATAFLOW))(*buffers, send_sems, recv_sems, *after)
    return list(outs)


def _gather_start(wholes, kinds, name, after=()):
    def body(w_refs, send_sems, recv_sems):
        x, y, c = _position()
        for w, ref in enumerate(w_refs):
            mine = _window(ref, kinds[w], (x, y, c))
            for k, chip in enumerate(_chips(x, y)):
                pltpu.make_async_remote_copy(src_ref=mine, dst_ref=mine, send_sem=send_sems.at[3 * w + k],
                                             recv_sem=recv_sems.at[3 * w + k], device_id=(*chip, c),
                                             device_id_type=MESH).start()

    return _split_start(body, name, wholes, (3 * len(wholes),), after)


def _gather_forward(wholes, kinds, send1, recv1, after, name):
    n = len(wholes)

    def wrapped(*refs):
        w_refs, s1, r1, s2, r2 = refs[:n], refs[n], refs[n + 1], refs[n + 3], refs[n + 4]
        x, y, c = _position()
        for k, chip in enumerate(_chips(x, y)):
            for w, ref in enumerate(w_refs):
                theirs = _window(ref, kinds[w], (*chip, c))
                pltpu.make_async_remote_copy(src_ref=theirs, dst_ref=theirs, send_sem=s1.at[3 * w + k],
                                             recv_sem=r1.at[3 * w + k], device_id=(*chip, c),
                                             device_id_type=MESH).wait_recv()
                pltpu.make_async_remote_copy(src_ref=theirs, dst_ref=theirs, send_sem=s2.at[3 * w + k],
                                             recv_sem=r2.at[3 * w + k], device_id=(x, y, 1 - c),
                                             device_id_type=MESH).start()
        for w, ref in enumerate(w_refs):
            mine = _window(ref, kinds[w], (x, y, c))
            for k, chip in enumerate(_chips(x, y)):
                pltpu.make_async_remote_copy(src_ref=mine, dst_ref=mine, send_sem=s1.at[3 * w + k],
                                             recv_sem=r1.at[3 * w + k], device_id=(*chip, c),
                                             device_id_type=MESH).wait_send()
        refs[-1][...] = jnp.zeros_like(refs[-1])

    outs = _pcall(
        wrapped, name=name, in_specs=[HBM_SPEC] * n + [SEM_SPEC, SEM_SPEC, ANY_SPEC],
        out_specs=[SEM_SPEC, SEM_SPEC] + [HBM_SPEC] * n + [VMEM_SPEC],
        out_shape=[pltpu.SemaphoreType.DMA((3 * n,)), pltpu.SemaphoreType.DMA((3 * n,))]
        + [pltpu.HBM(b.shape, b.dtype) for b in wholes] + [TOKEN],
        input_output_aliases={i: 2 + i for i in range(n)},
        compiler_params=pltpu.CompilerParams(has_side_effects=DATAFLOW))(*wholes, send1, recv1, after)
    return outs[0], outs[1], list(outs[2:2 + n]), outs[2 + n]


def _gather_finish(wholes, kinds, send2, recv2, after, name):
    def body(w_refs, s2, r2):
        x, y, c = _position()
        for k, chip in enumerate(_chips(x, y)):
            for w, ref in enumerate(w_refs):
                sent = _window(ref, kinds[w], (*chip, c))
                got = _window(ref, kinds[w], (*chip, 1 - c))
                pltpu.make_async_remote_copy(src_ref=sent, dst_ref=got, send_sem=s2.at[3 * w + k],
                                             recv_sem=r2.at[3 * w + k], device_id=(x, y, 1 - c),
                                             device_id_type=MESH).wait()

    return _split_wait(body, name, wholes, send2, recv2, after)


def _gather_small(small, after=()):
    def body(s_ref, o_ref, send_sems, recv_sems, local_sem):
        x, y, c = _position()
        mine = pltpu.make_async_copy(s_ref, o_ref.at[2 * x + y], local_sem)
        mine.start()
        sends = []
        for k, chip in enumerate(_chips(x, y)):
            cp = pltpu.make_async_remote_copy(src_ref=s_ref, dst_ref=o_ref.at[2 * x + y], send_sem=send_sems.at[k],
                                              recv_sem=recv_sems.at[k], device_id=(*chip, c), device_id_type=MESH)
            cp.start()
            sends.append(cp)
        for k, chip in enumerate(_chips(x, y)):
            pltpu.make_async_remote_copy(src_ref=s_ref, dst_ref=o_ref.at[2 * chip[0] + chip[1]], send_sem=send_sems.at[k],
                                         recv_sem=recv_sems.at[k], device_id=(*chip, c), device_id_type=MESH).wait_recv()
        for cp in sends:
            cp.wait_send()
        mine.wait()

    return _pcall(body, after=after, name="gather_small", in_specs=[HBM_SPEC], out_specs=HBM_SPEC,
                  out_shape=jax.ShapeDtypeStruct((4,) + small.shape, small.dtype),
                  scratch_shapes=[pltpu.SemaphoreType.DMA((3,)), pltpu.SemaphoreType.DMA((3,)),
                                  pltpu.SemaphoreType.DMA(())])(small)


def _scatter_copies(g_refs, land_refs, kinds, send_sems, recv_sems):
    me = _position()
    copies = []
    for k, rel in enumerate(RELATIONS):
        peer = _related(me, rel)
        for w, (g_ref, land_ref) in enumerate(zip(g_refs, land_refs)):
            copies.append(pltpu.make_async_remote_copy(
                src_ref=_window(g_ref, kinds[w], peer), dst_ref=land_ref.at[k],
                send_sem=send_sems.at[7 * w + k], recv_sem=recv_sems.at[7 * w + k], device_id=peer,
                device_id_type=MESH))
    return copies


def _scatter_start(grads, kinds, name):
    n = len(grads)
    lands = [lax.empty((N_DEV - 1,) + _half_shape(g.shape, kd), g.dtype) for g, kd in zip(grads, kinds)]

    def body(refs, send_sems, recv_sems):
        for cp in _scatter_copies(refs[:n], refs[n:], kinds, send_sems, recv_sems):
            cp.start()

    send, recv, thru, token = _split_start(body, name, list(grads) + lands, ((N_DEV - 1) * n,))
    return send, recv, thru[:n], thru[n:], token


def _scatter_wait(grads, lands, kinds, send, recv, after, name):
    n = len(grads)

    def body(refs, send_sems, recv_sems):
        for cp in _scatter_copies(refs[:n], refs[n:], kinds, send_sems, recv_sems):
            cp.wait()

    out = _split_wait(body, name, list(grads) + list(lands), send, recv, after)
    return out[:n], out[n:]


def _swap_start(halves, name):
    n = len(halves)
    lands = [lax.empty(h.shape, h.dtype) for h in halves]

    def body(refs, send_sems, recv_sems):
        x, y, c = _position()
        for w in range(n):
            pltpu.make_async_remote_copy(src_ref=refs[w], dst_ref=refs[n + w], send_sem=send_sems.at[w],
                                         recv_sem=recv_sems.at[w], device_id=(x, y, 1 - c), device_id_type=MESH).start()

    send, recv, thru, token = _split_start(body, name, list(halves) + lands, (n,))
    return send, recv, thru[:n], thru[n:], token


def _swap_wait(halves, lands, send, recv, after, name):
    n = len(halves)

    def body(refs, send_sems, recv_sems):
        x, y, c = _position()
        for w in range(n):
            pltpu.make_async_remote_copy(src_ref=refs[w], dst_ref=refs[n + w], send_sem=send_sems.at[w],
                                         recv_sem=recv_sems.at[w], device_id=(x, y, 1 - c), device_id_type=MESH).wait()

    out = _split_wait(body, name, list(halves) + list(lands), send, recv, after)
    return out[:n], out[n:]


def _allreduce_small(v, after=()):
    rows = v.shape[0]

    def body(v_ref, o_ref, recv_ref, send_sems, recv_sems):
        me = _position()
        recv_ref[_index(me)] = v_ref[...]
        sends = []
        for k, rel in enumerate(RELATIONS):
            peer = _related(me, rel)
            cp = pltpu.make_async_remote_copy(
                src_ref=v_ref, dst_ref=recv_ref.at[_index(me)],
                send_sem=send_sems.at[k], recv_sem=recv_sems.at[k], device_id=peer, device_id_type=MESH)
            cp.start()
            sends.append(cp)
        for k, rel in enumerate(RELATIONS):
            peer = _related(me, rel)
            pltpu.make_async_remote_copy(
                src_ref=v_ref, dst_ref=recv_ref.at[_index(peer)],
                send_sem=send_sems.at[k], recv_sem=recv_sems.at[k], device_id=peer, device_id_type=MESH).wait_recv()
        for cp in sends:
            cp.wait_send()
        acc = recv_ref[0]
        for k in range(1, N_DEV):
            acc = acc + recv_ref[k]
        o_ref[...] = acc

    return _pcall(body, after=after, name="allreduce_small", in_specs=[VMEM_SPEC], out_specs=VMEM_SPEC,
                  out_shape=jax.ShapeDtypeStruct((rows, 128), F32),
                  scratch_shapes=[pltpu.VMEM((N_DEV, rows, 128), F32), pltpu.SemaphoreType.DMA((7,)),
                                  pltpu.SemaphoreType.DMA((7,))],
                  compiler_params=pltpu.CompilerParams(vmem_limit_bytes=VMEM_LIMIT_BYTES))(v)


def _pack(arrays):
    flat = []
    for a in arrays:
        a = a.reshape(-1)
        flat.append(jnp.pad(a, (0, -a.shape[0] % 128)))
    flat = jnp.concatenate(flat)
    flat = jnp.pad(flat, (0, -flat.shape[0] % 1024))
    return flat.reshape(-1, 128)


def _unpack(packed, shapes):
    flat = packed.reshape(-1)
    out, at = [], 0
    for shp in shapes:
        size = 1
        for d in shp:
            size *= d
        out.append(flat[at:at + size].reshape(shp))
        at += size + (-size % 128)
    return out


WEIGHTS = ['l0_mix_norm_g', 'l0_w_in', 'l0_sc_conv_w', 'l0_w_out', 'l0_ffn_norm_g', 'l0_ffn_up', 'l0_ffn_conv_w',
           'l0_ffn_down', 'l1_mix_norm_g', 'l1_w_in', 'l1_fox_b_f', 'l1_sg_w', 'l1_sg_b', 'l1_sg_norm_g', 'l1_w_out',
           'l1_ffn_norm_g', 'l1_ffn_up', 'l1_ffn_conv_w', 'l1_ffn_down', 'final_norm_g']
BIG = {'l0_w_in': 'col', 'l0_w_out': 'row', 'l0_ffn_up': 'col', 'l0_ffn_down': 'row',
       'l1_w_in': 'maj', 'l1_w_out': 'row', 'l1_ffn_up': 'col', 'l1_ffn_down': 'row'}
GATHER_GROUPS = [['l0_w_in'], ['l0_w_out'], ['l0_ffn_up'], ['l0_ffn_down'], ['l1_w_in', 'l1_w_out'],
                 ['l1_ffn_up'], ['l1_ffn_down']]
CONV = ['l0_sc_conv_w', 'l0_ffn_conv_w', 'l1_ffn_conv_w']
SMALL = [n for n in WEIGHTS if n not in BIG]
IN_CD = 5 * HALF + N_HEADS


def _ffn_forward(x, g, get_up, behind_act, get_down, conv_w, tag):
    h = _rmsnorm_fwd(x, g, tag + "_norm")
    u = _matmul(h, get_up(h), "nn", F32, tag + "_up")
    f = _ffn_act_fwd(u, conv_w, tag + "_act", after=behind_act(u))
    w_down, tokens = get_down(f)
    return _matmul(f, w_down, "nn", F32, tag + "_down", res=x, after=tokens), (h, u, f)


def _ffn_backward(x, g, w_up, conv_w, w_down, saved, d_out, send_up, send_down, tag):
    h, u, f = saved
    dw_down = _matmul(f, d_out, "tn", BF16, tag + "_dwdown")
    d_f = _matmul(d_out, w_down, "nt", F32, tag + "_df", after=[send_down(dw_down)])
    du, dcw_gate, dcw_up = _ffn_act_bwd(u, conv_w, d_f, tag + "_dact")
    dw_up = _matmul(h, du, "tn", BF16, tag + "_dwup")
    dh = _matmul(du, w_up, "nt", F32, tag + "_dh", after=[send_up(dw_up)])
    dx, dg = _rmsnorm_bwd(x, g, dh, d_out, tag + "_dnorm")
    return dx, dg, jnp.concatenate([dcw_gate, dcw_up], axis=1)


def kernel(x, l0_mix_norm_g, l0_w_in, l0_sc_conv_w, l0_w_out, l0_ffn_norm_g, l0_ffn_up, l0_ffn_conv_w, l0_ffn_down, l1_mix_norm_g, l1_w_in, l1_fox_b_f, l1_sg_w, l1_sg_b, l1_sg_norm_g, l1_w_out, l1_ffn_norm_g, l1_ffn_up, l1_ffn_conv_w, l1_ffn_down, final_norm_g, loss_target, m_l0_mix_norm_g, m_l0_w_in, m_l0_sc_conv_w, m_l0_w_out, m_l0_ffn_norm_g, m_l0_ffn_up, m_l0_ffn_conv_w, m_l0_ffn_down, m_l1_mix_norm_g, m_l1_w_in, m_l1_fox_b_f, m_l1_sg_w, m_l1_sg_b, m_l1_sg_norm_g, m_l1_w_out, m_l1_ffn_norm_g, m_l1_ffn_up, m_l1_ffn_conv_w, m_l1_ffn_down, m_final_norm_g, v_l0_mix_norm_g, v_l0_w_in, v_l0_sc_conv_w, v_l0_w_out, v_l0_ffn_norm_g, v_l0_ffn_up, v_l0_ffn_conv_w, v_l0_ffn_down, v_l1_mix_norm_g, v_l1_w_in, v_l1_fox_b_f, v_l1_sg_w, v_l1_sg_b, v_l1_sg_norm_g, v_l1_w_out, v_l1_ffn_norm_g, v_l1_ffn_up, v_l1_ffn_conv_w, v_l1_ffn_down, v_final_norm_g):
    given = (l0_mix_norm_g, l0_w_in, l0_sc_conv_w, l0_w_out, l0_ffn_norm_g, l0_ffn_up, l0_ffn_conv_w, l0_ffn_down, l1_mix_norm_g, l1_w_in, l1_fox_b_f, l1_sg_w, l1_sg_b, l1_sg_norm_g, l1_w_out, l1_ffn_norm_g, l1_ffn_up, l1_ffn_conv_w, l1_ffn_down, final_norm_g)
    given_m = (m_l0_mix_norm_g, m_l0_w_in, m_l0_sc_conv_w, m_l0_w_out, m_l0_ffn_norm_g, m_l0_ffn_up, m_l0_ffn_conv_w, m_l0_ffn_down, m_l1_mix_norm_g, m_l1_w_in, m_l1_fox_b_f, m_l1_sg_w, m_l1_sg_b, m_l1_sg_norm_g, m_l1_w_out, m_l1_ffn_norm_g, m_l1_ffn_up, m_l1_ffn_conv_w, m_l1_ffn_down, m_final_norm_g)
    given_v = (v_l0_mix_norm_g, v_l0_w_in, v_l0_sc_conv_w, v_l0_w_out, v_l0_ffn_norm_g, v_l0_ffn_up, v_l0_ffn_conv_w, v_l0_ffn_down, v_l1_mix_norm_g, v_l1_w_in, v_l1_fox_b_f, v_l1_sg_w, v_l1_sg_b, v_l1_sg_norm_g, v_l1_w_out, v_l1_ffn_norm_g, v_l1_ffn_up, v_l1_ffn_conv_w, v_l1_ffn_down, v_final_norm_g)
    wt = dict(zip(WEIGHTS, given))
    mom = dict(zip(WEIGHTS, given_m))
    var = dict(zip(WEIGHTS, given_v))
    s = x.shape[1]
    t = ATT_BLOCK
    x0, target = x[0], loss_target[0]
    chip = 2 * lax.axis_index("x") + lax.axis_index("y")

    pos = jnp.stack([chip, lax.axis_index("c")]).astype(jnp.int32)

    conv_widths = [wt[n].shape[1] for n in CONV]
    conv_all = _gather_small(jnp.concatenate([wt[n] for n in CONV], axis=1))
    conv_full, at = {}, 0
    for n, cw in zip(CONV, conv_widths):
        conv_full[n] = jnp.transpose(conv_all[:, :, at:at + cw], (1, 0, 2)).reshape(3, 4 * cw)
        at += cw
    gathers, token = [], conv_all
    for gi, names in enumerate(GATHER_GROUPS):
        placed = [_place_shard(pos, wt[n], BIG[n], "place_" + n, [token]) for n in names]
        send, recv, thru, token = _gather_start(placed, [BIG[n] for n in names], "gather_start_%d" % gi, [token])
        gathers.append((send, recv, thru))
    full = {}

    def forward_gather(gi, after):
        send, recv, thru = gathers[gi]
        kinds = [BIG[n] for n in GATHER_GROUPS[gi]]
        gathers[gi] = _gather_forward(thru, kinds, send, recv, after, "gather_forward_%d" % gi)
        return gathers[gi][3]

    def finish_gather(gi, after):
        send, recv, thru, tok = gathers[gi]
        names = GATHER_GROUPS[gi]
        wholes = _gather_finish(thru, [BIG[n] for n in names], send, recv, tok if after is None else after,
                                "gather_finish_%d" % gi)
        full.update(zip(names, wholes))

    def vec(name):
        return wt[name].reshape(1, -1)

    h0 = _rmsnorm_fwd(x0, vec('l0_mix_norm_g'), "l0_mix_norm", after=[token])
    forward_gather(0, h0)
    finish_gather(0, None)
    p0 = _matmul(h0, full['l0_w_in'], "nn", F32, "l0_in")
    a_out, sb_carries = _sb_fwd(p0, "l0_sb", after=[forward_gather(1, p0)])
    finish_gather(1, a_out)
    b_out = _sc_fwd(p0, conv_full['l0_sc_conv_w'], "l0_sc")
    ab0 = jnp.concatenate([a_out.astype(BF16), b_out], axis=1)
    x1 = _matmul(ab0, full['l0_w_out'], "nn", F32, "l0_out", res=x0, after=[forward_gather(2, b_out)])

    def ffn_weights(up_group, next_group):
        def get_up(h):
            finish_gather(up_group, h)
            return full[GATHER_GROUPS[up_group][0]]

        def behind_act(u):
            return [forward_gather(up_group + 1, u)]

        def get_down(f):
            finish_gather(up_group + 1, f)
            return full[GATHER_GROUPS[up_group + 1][0]], ([forward_gather(next_group, f)] if next_group else ())

        return get_up, behind_act, get_down

    x2, ffn0_saved = _ffn_forward(x1, vec('l0_ffn_norm_g'), *ffn_weights(2, 4), conv_full['l0_ffn_conv_w'], "l0_ffn")
    h2 = _rmsnorm_fwd(x2, vec('l1_mix_norm_g'), "l1_mix_norm")
    finish_gather(4, h2)
    w_in1 = jnp.transpose(full['l1_w_in'], (1, 0, 2)).reshape(D_MODEL, IN_CD)
    w_in1_main = w_in1[:, :5 * HALF]
    w_in1_f = jnp.pad(w_in1[:, 5 * HALF:], ((0, 0), (0, 128 - N_HEADS)))
    p1 = _matmul(h2, w_in1_main, "nn", F32, "l1_in")
    f_logit = _matmul(h2, w_in1_f, "nn", F32, "l1_in_f")
    b_f = jnp.pad(wt['l1_fox_b_f'], (0, 128 - N_HEADS)).reshape(1, 128)
    c_heads = _fox_prep(f_logit, b_f, "l1_fox_prep")[:, :N_HEADS].T
    c_col = c_heads[:, :, None]
    c_row = c_heads.reshape(N_HEADS, s // t, 1, t)
    sg_bias = jnp.repeat(wt['l1_sg_b'].T, HEAD, axis=1)
    sg_gain = vec('l1_sg_norm_g')
    c_out = _sg_fwd(p1, wt['l1_sg_w'], sg_bias, sg_gain, "l1_sg")
    d_out, lse = _fox_fwd(p1, c_col, c_row, "l1_fox", after=[forward_gather(5, c_out)])
    cd1 = jnp.concatenate([c_out, d_out.astype(BF16)], axis=1)
    x3 = _matmul(cd1, full['l1_w_out'], "nn", F32, "l1_out", res=x2)
    x4, ffn1_saved = _ffn_forward(x3, vec('l1_ffn_norm_g'), *ffn_weights(5, None), conv_full['l1_ffn_conv_w'], "l1_ffn")
    dx4, dg_final, loss_part = _loss_head(x4, vec('final_norm_g'), target, "loss_head")

    grads = {'final_norm_g': dg_final}
    scatters = []

    def send_grads(names):
        def start(*group):
            send, recv, thru, lands, tok = _scatter_start(list(group), [BIG[n] for n in names],
                                                          "scatter_start_%d" % len(scatters))
            scatters.append((names, send, recv, thru, lands))
            return tok
        return start

    dx3, grads['l1_ffn_norm_g'], grads['l1_ffn_conv_w'] = _ffn_backward(
        x3, vec('l1_ffn_norm_g'), full['l1_ffn_up'], conv_full['l1_ffn_conv_w'], full['l1_ffn_down'], ffn1_saved, dx4,
        send_grads(['l1_ffn_up']), send_grads(['l1_ffn_down']), "l1_ffn")
    dw_out1 = _matmul(cd1, dx3, "tn", BF16, "l1_dwout")
    d_cd = _matmul(dx3, full['l1_w_out'], "nt", F32, "l1_dcd")
    du, dv, grads['l1_sg_w'], db_sg, grads['l1_sg_norm_g'] = _sg_bwd(p1, wt['l1_sg_w'], sg_bias, sg_gain, d_cd, "l1_dsg")
    grads['l1_sg_b'] = db_sg[:, :N_HEADS].T
    dq, dk, dvv, dcol, drow = _fox_bwd(p1, c_col, c_row, lse, d_cd, d_out, "l1_dfox")
    pad8 = ((0, 0), (0, 128 - N_HEADS))
    d_f_logit, d_b_f = _fox_post(jnp.pad(drow[:, :, 0].T, pad8), jnp.pad(dcol[:, ::HEAD], pad8), f_logit, b_f,
                                 "l1_fox_post")
    grads['l1_fox_b_f'] = d_b_f[0, :N_HEADS]
    dp1 = jnp.concatenate([a.astype(BF16) for a in (du, dv, dq, dk, dvv)], axis=1)
    dw_main = _matmul(h2, dp1, "tn", BF16, "l1_dwin")
    dw_f = _matmul(h2, d_f_logit, "tn", BF16, "l1_dwin_f")
    dw_in1 = jnp.concatenate([dw_main, dw_f[:, :N_HEADS]], axis=1)
    dw_in1 = jnp.transpose(dw_in1.reshape(D_MODEL, 4, IN_CD // 4), (1, 0, 2))
    dh2 = _matmul(dp1, w_in1_main, "nt", F32, "l1_dh", after=[send_grads(['l1_w_out', 'l1_w_in'])(dw_out1, dw_in1)])
    dh2 = _matmul(d_f_logit, w_in1_f, "nt", F32, "l1_dh_f", res=dh2)
    dx2, grads['l1_mix_norm_g'] = _rmsnorm_bwd(x2, vec('l1_mix_norm_g'), dh2, dx3, "l1_dmix_norm")
    dx1, grads['l0_ffn_norm_g'], grads['l0_ffn_conv_w'] = _ffn_backward(
        x1, vec('l0_ffn_norm_g'), full['l0_ffn_up'], conv_full['l0_ffn_conv_w'], full['l0_ffn_down'], ffn0_saved, dx2,
        send_grads(['l0_ffn_up']), send_grads(['l0_ffn_down']), "l0_ffn")
    dw_out0 = _matmul(ab0, dx1, "tn", BF16, "l0_dwout")
    d_ab = _matmul(dx1, full['l0_w_out'], "nt", F32, "l0_dab", after=[send_grads(['l0_w_out'])(dw_out0)])
    dq0, dk0, dv0 = _sb_bwd(p0, d_ab, sb_carries, "l0_dsb")
    dgb, dgc, dhin, grads['l0_sc_conv_w'] = _sc_bwd(p0, conv_full['l0_sc_conv_w'], d_ab, "l0_dsc")
    dp0 = jnp.concatenate([a.astype(BF16) for a in (dq0, dk0, dv0, dgb, dgc, dhin)], axis=1)
    dw_in0 = _matmul(h0, dp0, "tn", BF16, "l0_dwin")
    dh0 = _matmul(dp0, full['l0_w_in'], "nt", F32, "l0_dh", after=[send_grads(['l0_w_in'])(dw_in0)])
    dx0, grads['l0_mix_norm_g'] = _rmsnorm_bwd(x0, vec('l0_mix_norm_g'), dh0, dx1, "l0_dmix_norm")

    shard_grads, delta, new_m, new_v, swaps = {}, {}, {}, {}, {}

    def reduce_group(gi, after):
        names, send, recv, thru, lands = scatters[gi]
        kinds = [BIG[n] for n in names]
        g_thru, landed = _scatter_wait(thru, lands, kinds, send, recv, after, "scatter_wait_%d" % gi)
        halves = [_sum_partials(pos, g, ld, kd, "sum_" + n) for n, g, ld, kd in zip(names, g_thru, landed, kinds)]
        s_send, s_recv, h_thru, s_lands, tok = _swap_start(halves, "swap_start_%d" % gi)
        swaps[gi] = (names, s_send, s_recv, h_thru, s_lands)
        return tok

    def update_group(gi, after):
        names, s_send, s_recv, h_thru, s_lands = swaps[gi]
        mine, theirs = _swap_wait(h_thru, s_lands, s_send, s_recv, after, "swap_wait_%d" % gi)
        for n, gm, gs in zip(names, mine, theirs):
            shard_grads[n], delta[n], new_m[n], new_v[n] = _adamw_shard(pos, wt[n], gm, gs, mom[n], var[n], "adamw_" + n)
        return [delta[n] for n in names]

    after = reduce_group(2, reduce_group(1, reduce_group(0, dx0)))
    after = update_group(2, update_group(1, update_group(0, after)))
    after = reduce_group(5, reduce_group(4, reduce_group(3, after)))
    after = update_group(5, update_group(4, update_group(3, after)))
    after = reduce_group(6, after)
    small_shapes = [conv_full[n].shape if n in CONV else wt[n].shape for n in SMALL] + [loss_part.shape]
    small_all = _allreduce_small(_pack([grads[n] for n in SMALL] + [loss_part]), [after])
    small_sums = _unpack(small_all, small_shapes)
    loss = small_sums[-1][0, 0]
    for n, g in zip(SMALL, small_sums):
        shard_grads[n] = lax.dynamic_slice_in_dim(g, chip * wt[n].shape[1], wt[n].shape[1], axis=1) if n in CONV else g
    update_group(6, small_all)
    shapes = [wt[n].shape for n in SMALL]
    packed = _adamw(_pack([wt[n] for n in SMALL]), _pack([shard_grads[n] for n in SMALL]),
                    _pack([mom[n] for n in SMALL]), _pack([var[n] for n in SMALL]), "adamw_small")
    for out, pk in zip((delta, new_m, new_v), packed):
        out.update(zip(SMALL, _unpack(pk, shapes)))

    return (loss, dx0[None], *[shard_grads[n] for n in WEIGHTS], *[delta[n] for n in WEIGHTS],
            *[new_m[n] for n in WEIGHTS], *[new_v[n] for n in WEIGHTS])
```

```python
import jax
import jax.numpy as jnp
from jax import lax
from jax.experimental import pallas as pl
from jax.experimental.pallas import tpu as pltpu

F32 = jnp.float32
BF16 = jnp.bfloat16

D_MODEL = 2048
HEAD = 128
N_HEADS = 8
HALF = N_HEADS * HEAD
D_FF = 5632
EPS = 1e-6
ATT_SCALE = HEAD ** -0.5
ATT_BLOCK = 512
NEG = -1e30

ADAM_LR = 0.001
ADAM_B1 = 0.9
ADAM_B2 = 0.999
ADAM_EPS = 1e-08
ADAM_WD = 0.01
ADAM_STEP = 10

VMEM_LIMIT_BYTES = 48 * 1024 * 1024
MM_VMEM_LIMIT_BYTES = 56 * 1024 * 1024
ATT_VMEM_LIMIT_BYTES = 48 * 1024 * 1024
MESH = pl.DeviceIdType.MESH
HBM_SPEC = pl.BlockSpec(memory_space=pltpu.HBM)
VMEM_SPEC = pl.BlockSpec(memory_space=pltpu.VMEM)


def _pcall(body, after=(), **kw):
    if not after:
        return pl.pallas_call(body, **kw)
    n_in, n_after, inner = len(kw["in_specs"]), len(after), body
    kw["in_specs"] = list(kw["in_specs"]) + [pl.BlockSpec(memory_space=pl.ANY)] * n_after

    def body(*refs):
        inner(*refs[:n_in], *refs[n_in + n_after:])

    call = pl.pallas_call(body, **kw)
    return lambda *args: call(*args, *after)


def _params(*semantics, vmem_limit=VMEM_LIMIT_BYTES):
    return pltpu.CompilerParams(dimension_semantics=semantics, vmem_limit_bytes=vmem_limit)


def _pick(n, cap):
    best = None
    for t in range(128, min(n, cap) + 1, 128):
        if n % t == 0:
            best = t
    return n if best is None else best


def _dot(a, b, dims):
    return lax.dot_general(a, b, (dims, ((), ())), preferred_element_type=F32)


def _dot_nn(a, b):
    return _dot(a, b, ((1,), (0,)))


def _dot_nt(a, b):
    return _dot(a, b, ((1,), (1,)))


def _dot_tn(a, b):
    return _dot(a, b, ((0,), (0,)))


def _split3(x):
    hi = x.astype(BF16)
    r1 = x - hi.astype(F32)
    mid = r1.astype(BF16)
    lo = (r1 - mid.astype(F32)).astype(BF16)
    return hi, mid, lo


def _log_sigmoid(z):
    return jnp.minimum(z, 0.0) - jnp.log1p(jnp.exp(-jnp.abs(z)))


_GELU_K = 0.7978845608028654


def _gelu(x):
    return 0.5 * x * (1.0 + jnp.tanh(_GELU_K * (x + 0.044715 * x * x * x)))


def _gelu_grad(x):
    t = jnp.tanh(_GELU_K * (x + 0.044715 * x * x * x))
    return 0.5 * (1.0 + t) + 0.5 * x * (1.0 - t * t) * _GELU_K * (1.0 + 3.0 * 0.044715 * x * x)


SUBLANES = 8


def _shift_down(x, k):
    rolled = pltpu.roll(x, k, axis=0)
    head = rolled[:SUBLANES]
    head = jnp.where(lax.broadcasted_iota(jnp.int32, head.shape, 0) >= k, head, 0.0)
    return jnp.concatenate([head, rolled[SUBLANES:]], axis=0)


def _shift_up(x, k):
    n = x.shape[0]
    rolled = pltpu.roll(x, n - k, axis=0)
    tail = rolled[n - SUBLANES:]
    tail = jnp.where(lax.broadcasted_iota(jnp.int32, tail.shape, 0) < SUBLANES - k, tail, 0.0)
    return jnp.concatenate([rolled[:n - SUBLANES], tail], axis=0)


def _conv3(s, w, shifted=None):
    s1, s2 = shifted if shifted else (_shift_down(s, 1), _shift_down(s, 2))
    return w[0:1, :] * s2 + w[1:2, :] * s1 + w[2:3, :] * s


def _conv3_transpose(d, w):
    return w[2:3, :] * d + w[1:2, :] * _shift_up(d, 1) + w[0:1, :] * _shift_up(d, 2)


def _conv3_wgrad(d, s, shifted, dw_ref):
    s1, s2 = shifted
    dw_ref[0:1, :] = jnp.sum(d * s2, axis=0, keepdims=True)
    dw_ref[1:2, :] = jnp.sum(d * s1, axis=0, keepdims=True)
    dw_ref[2:3, :] = jnp.sum(d * s, axis=0, keepdims=True)


MM_TILE_M, MM_TILE_N, MM_TILE_K = 1408, 512, 5632


def _matmul(a, b, mode, out_dtype, name, res=None, after=()):
    a_parts = a.shape[0] if a.ndim == 3 else 1
    b_parts = b.shape[0] if b.ndim == 3 else 1
    a_shape = (a.shape[1], a_parts * a.shape[2]) if a.ndim == 3 else a.shape
    b_shape = (b.shape[1], b_parts * b.shape[2]) if b.ndim == 3 else b.shape
    assert (a_parts == 1 or mode != "tn") and (b_parts == 1 or mode == "tn")
    if mode == "nn":
        (m, k), (k2, n) = a_shape, b_shape
    elif mode == "nt":
        (m, k), (n, k2) = a_shape, b_shape
    else:
        (k, m), (k2, n) = a_shape, b_shape
    assert k == k2, (a.shape, b.shape, mode)
    tm, tn, tk = _pick(m, MM_TILE_M), _pick(n // b_parts, MM_TILE_N), _pick(k // a_parts, MM_TILE_K)
    nk = k // tk
    if mode == "tn":
        a_spec = pl.BlockSpec((tk, tm), lambda i, j, kk: (kk, i))
    elif a_parts > 1:
        per = nk // a_parts
        a_spec = pl.BlockSpec((None, tm, tk), lambda i, j, kk: (kk // per, i, kk % per))
    else:
        a_spec = pl.BlockSpec((tm, tk), lambda i, j, kk: (i, kk))
    if mode == "nt":
        b_spec = pl.BlockSpec((tn, tk), lambda i, j, kk: (j, kk))
    elif b_parts > 1:
        per = n // b_parts // tn
        b_spec = pl.BlockSpec((None, tk, tn), lambda i, j, kk: (j // per, kk, j % per))
    else:
        b_spec = pl.BlockSpec((tk, tn), lambda i, j, kk: (kk, j))
    o_spec = pl.BlockSpec((tm, tn), lambda i, j, kk: (i, j))
    dims = {"nn": ((1,), (0,)), "nt": ((1,), (1,)), "tn": ((0,), (0,))}[mode]
    has_res = res is not None

    def body(*refs):
        a_ref, b_ref = refs[0], refs[1]
        r_ref = refs[2] if has_res else None
        o_ref = refs[3] if has_res else refs[2]
        part = _dot(a_ref[...].astype(BF16), b_ref[...].astype(BF16), dims)

        def finish(total):
            if has_res:
                total = total + r_ref[...]
            o_ref[...] = total.astype(out_dtype)

        if nk == 1:
            finish(part)
        else:
            acc_ref = refs[-1]
            kk = pl.program_id(2)

            @pl.when(kk == 0)
            def _():
                acc_ref[...] = part

            @pl.when(kk > 0)
            def _():
                acc_ref[...] += part

            @pl.when(kk == nk - 1)
            def _():
                finish(acc_ref[...])

    in_specs = [a_spec, b_spec] + ([o_spec] if has_res else [])
    args = (a, b) + ((res,) if has_res else ())
    return _pcall(
        body, after=after, name=name, grid=(m // tm, n // tn, nk),
        in_specs=in_specs, out_specs=o_spec,
        out_shape=jax.ShapeDtypeStruct((m, n), out_dtype),
        scratch_shapes=[pltpu.VMEM((tm, tn), F32)] if nk > 1 else [],
        compiler_params=_params("parallel", "parallel", "arbitrary", vmem_limit=MM_VMEM_LIMIT_BYTES),
    )(*args)


ROW_TILE = 256


def _rmsnorm_fwd(x, g, name, after=()):
    s, d = x.shape

    def body(x_ref, g_ref, o_ref):
        xf = x_ref[...]
        r = lax.rsqrt(jnp.mean(xf * xf, axis=-1, keepdims=True) + EPS)
        o_ref[...] = (xf * r * g_ref[...]).astype(BF16)

    row = pl.BlockSpec((ROW_TILE, d), lambda i: (i, 0))
    vec = pl.BlockSpec((1, d), lambda i: (0, 0))
    return _pcall(body, after=after, name=name, grid=(s // ROW_TILE,), in_specs=[row, vec], out_specs=row,
                  out_shape=jax.ShapeDtypeStruct((s, d), BF16), compiler_params=_params("parallel"))(x, g)


def _rmsnorm_bwd(x, g, dh, dres, name):
    s, d = x.shape

    def body(x_ref, g_ref, dh_ref, dres_ref, dx_ref, dg_ref):
        xf = x_ref[...]
        r = lax.rsqrt(jnp.mean(xf * xf, axis=-1, keepdims=True) + EPS)
        xhat = xf * r
        dh_v = dh_ref[...]
        dxh = dh_v * g_ref[...]
        proj = jnp.mean(dxh * xhat, axis=-1, keepdims=True)
        dx_ref[...] = dres_ref[...] + r * (dxh - xhat * proj)
        part = jnp.sum(dh_v * xhat, axis=0, keepdims=True)

        @pl.when(pl.program_id(0) == 0)
        def _():
            dg_ref[...] = part

        @pl.when(pl.program_id(0) > 0)
        def _():
            dg_ref[...] += part

    row = pl.BlockSpec((ROW_TILE, d), lambda i: (i, 0))
    vec = pl.BlockSpec((1, d), lambda i: (0, 0))
    return _pcall(body, name=name, grid=(s // ROW_TILE,), in_specs=[row, vec, row, row], out_specs=[row, vec],
                  out_shape=[jax.ShapeDtypeStruct((s, d), F32), jax.ShapeDtypeStruct((1, d), F32)],
                  compiler_params=_params("arbitrary"))(x, g, dh, dres)


def _loss_head(x, g, target, name):
    s, d = x.shape

    def body(x_ref, g_ref, t_ref, dx_ref, dg_ref, loss_ref):
        xf = x_ref[...]
        r = lax.rsqrt(jnp.mean(xf * xf, axis=-1, keepdims=True) + EPS)
        xhat = xf * r
        gv = g_ref[...]
        err = xhat * gv - t_ref[...]
        dy = err * (1.0 / d)
        dxh = dy * gv
        proj = jnp.mean(dxh * xhat, axis=-1, keepdims=True)
        dx_ref[...] = r * (dxh - xhat * proj)
        dg_part = jnp.sum(dy * xhat, axis=0, keepdims=True)
        row_loss = jnp.sum(err * err, axis=-1, keepdims=True) * (0.5 / d)
        loss_part = jnp.broadcast_to(jnp.sum(row_loss, axis=0, keepdims=True), (1, 128))

        @pl.when(pl.program_id(0) == 0)
        def _():
            dg_ref[...] = dg_part
            loss_ref[...] = loss_part

        @pl.when(pl.program_id(0) > 0)
        def _():
            dg_ref[...] += dg_part
            loss_ref[...] += loss_part

    row = pl.BlockSpec((ROW_TILE, d), lambda i: (i, 0))
    vec = pl.BlockSpec((1, d), lambda i: (0, 0))
    one = pl.BlockSpec((1, 128), lambda i: (0, 0))
    return _pcall(body, name=name, grid=(s // ROW_TILE,), in_specs=[row, vec, row], out_specs=[row, vec, one],
                  out_shape=[jax.ShapeDtypeStruct((s, d), F32), jax.ShapeDtypeStruct((1, d), F32),
                             jax.ShapeDtypeStruct((1, 128), F32)],
                  compiler_params=_params("arbitrary"))(x, g, target)


HEADS_PER_STEP = 2
GROUP_W = HEADS_PER_STEP * HEAD
N_GROUPS = N_HEADS // HEADS_PER_STEP


def _head_cols(h):
    return slice(h * HEAD, (h + 1) * HEAD)


def _head_specs(s, col0):
    t = ATT_BLOCK
    g0 = [c // HEADS_PER_STEP for c in col0]
    qspec = pl.BlockSpec((t, GROUP_W), lambda g, i: (i, g0[0] + g))
    kspec = pl.BlockSpec((s, GROUP_W), lambda g, i: (0, g0[1] + g))
    vspec = pl.BlockSpec((s, GROUP_W), lambda g, i: (0, g0[2] + g))
    return qspec, kspec, vspec


TRI = 256


def _order_matrix(later):
    r, c = lax.broadcasted_iota(jnp.int32, (TRI, TRI), 0), lax.broadcasted_iota(jnp.int32, (TRI, TRI), 1)
    return (r > c if later else r < c).astype(BF16)


def _exact_dot(x, m, later):
    parts = [x[:, c:c + TRI] for c in range(0, x.shape[1], TRI)]
    totals = [jnp.sum(p, axis=1, keepdims=True) for p in parts] if len(parts) > 1 else None
    out = []
    for j, p in enumerate(parts):
        hi = p.astype(BF16)
        lo = (p - hi.astype(F32)).astype(BF16)
        acc = _dot_nn(hi, m) + _dot_nn(lo, m)
        for other in (range(j + 1, len(parts)) if later else range(j)):
            acc = acc + totals[other]
        out.append(acc)
    return out[0] if len(out) == 1 else jnp.concatenate(out, axis=1)


def _sb_block(q, kblk, carry_l, u, diagonal):
    t = ATT_BLOCK
    z = _dot_nt(q, kblk) * ATT_SCALE
    sp = jnp.maximum(z, 0.0) + jnp.log(1.0 + jnp.exp(-jnp.abs(z)))
    if not diagonal:
        l = -sp
        return z, None, l, jnp.exp(z + l + _exact_dot(l, u, True) + carry_l)
    mask = lax.broadcasted_iota(jnp.int32, (t, t), 1) < lax.broadcasted_iota(jnp.int32, (t, t), 0)
    l = jnp.where(mask, -sp, 0.0)
    a = jnp.where(mask, jnp.exp(z - sp + _exact_dot(l, u, True) + carry_l), 0.0)
    return z, mask, l, a


def _sb_carry_spec(s):
    t = ATT_BLOCK
    return pl.BlockSpec((HEADS_PER_STEP, None, s // t, t, 1), lambda g, i: (g, i, 0, 0, 0))


def _sb_fwd(p, name, after=()):
    s = p.shape[0]
    t = ATT_BLOCK
    nb = s // t

    def body(q_ref, k_ref, v_ref, o_ref, cl_ref):
        i = pl.program_id(1)
        heads = range(HEADS_PER_STEP)
        q = [q_ref[:, _head_cols(h)].astype(BF16) for h in heads]
        u = _order_matrix(True)
        cl_ref[...] = jnp.zeros_like(cl_ref)

        def tile(kb, carry, diagonal):
            ks = pl.multiple_of(kb * t, t)
            out = []
            for h in heads:
                acc, carry_l = carry[h]
                kblk = k_ref[pl.ds(ks, t), _head_cols(h)].astype(BF16)
                vblk = v_ref[pl.ds(ks, t), _head_cols(h)].astype(BF16)
                cl_ref[h, kb] = carry_l
                _, _, l, a = _sb_block(q[h], kblk, carry_l, u, diagonal)
                out.append((acc + _dot_nn(a.astype(BF16), vblk), carry_l + jnp.sum(l, axis=1, keepdims=True)))
            return tuple(out)

        carry = tile(i, tuple((jnp.zeros((t, HEAD), F32), jnp.zeros((t, 1), F32)) for _ in heads), True)
        carry = lax.fori_loop(0, i, lambda n, c: tile(i - 1 - n, c, False), carry)
        for h in heads:
            o_ref[:, _head_cols(h)] = carry[h][0]

    qspec, kspec, vspec = _head_specs(s, (0, N_HEADS, 2 * N_HEADS))
    ospec = pl.BlockSpec((t, GROUP_W), lambda g, i: (i, g))
    return _pcall(body, after=after, name=name, grid=(N_GROUPS, nb), in_specs=[qspec, kspec, vspec],
                  out_specs=[ospec, _sb_carry_spec(s)],
                  out_shape=[jax.ShapeDtypeStruct((s, HALF), F32), jax.ShapeDtypeStruct((N_HEADS, nb, nb, t, 1), F32)],
                  compiler_params=_params("parallel", "parallel", vmem_limit=ATT_VMEM_LIMIT_BYTES))(p, p, p)


def _sb_bwd(p, d_ab, carries, name):
    s = p.shape[0]
    t = ATT_BLOCK

    def body(q_ref, k_ref, v_ref, do_ref, cl_ref, dq_ref, dk_ref, dv_ref):
        i = pl.program_id(1)

        @pl.when(i == 0)
        def _():
            dk_ref[...] = jnp.zeros_like(dk_ref)
            dv_ref[...] = jnp.zeros_like(dv_ref)

        heads = range(HEADS_PER_STEP)
        q = [q_ref[:, _head_cols(h)].astype(BF16) for h in heads]
        do = [do_ref[:, _head_cols(h)].astype(BF16) for h in heads]
        u = _order_matrix(True)
        lower = _order_matrix(False)

        def tile(kb, carry, diagonal):
            ks = pl.multiple_of(kb * t, t)
            out = []
            for h in heads:
                dq, carry_g = carry[h]
                kblk = k_ref[pl.ds(ks, t), _head_cols(h)].astype(BF16)
                vblk = v_ref[pl.ds(ks, t), _head_cols(h)].astype(BF16)
                z, mask, _, a = _sb_block(q[h], kblk, cl_ref[h, kb], u, diagonal)
                g = a * _dot_nt(do[h], vblk)
                earlier_g = _exact_dot(g, lower, False) + carry_g
                sig = jax.nn.sigmoid(z)
                dz = g * (1.0 - sig) - sig * earlier_g
                if diagonal:
                    dz = jnp.where(mask, dz, 0.0)
                dz = dz.astype(BF16)
                dv_ref[pl.ds(ks, t), _head_cols(h)] += _dot_tn(a.astype(BF16), do[h])
                dk_ref[pl.ds(ks, t), _head_cols(h)] += _dot_tn(dz, q[h]) * ATT_SCALE
                out.append((dq + _dot_nn(dz, kblk) * ATT_SCALE, carry_g + jnp.sum(g, axis=1, keepdims=True)))
            return tuple(out)

        init = tuple((jnp.zeros((t, HEAD), F32), jnp.zeros((t, 1), F32)) for _ in heads)
        carry = tile(i, lax.fori_loop(0, i, lambda kb, c: tile(kb, c, False), init), True)
        for h in heads:
            dq_ref[:, _head_cols(h)] = carry[h][0]

    qspec, kspec, vspec = _head_specs(s, (0, N_HEADS, 2 * N_HEADS))
    blk = pl.BlockSpec((t, GROUP_W), lambda g, i: (i, g))
    whole = pl.BlockSpec((s, GROUP_W), lambda g, i: (0, g))
    shape = jax.ShapeDtypeStruct((s, HALF), F32)
    return _pcall(body, name=name, grid=(N_GROUPS, s // t), in_specs=[qspec, kspec, vspec, blk, _sb_carry_spec(s)],
                  out_specs=[blk, whole, whole], out_shape=[shape, shape, shape],
                  compiler_params=_params("parallel", "arbitrary", vmem_limit=ATT_VMEM_LIMIT_BYTES))(p, p, p, d_ab, carries)


COL_TILE = 256


def _sc_fwd(p, w, name):
    s = p.shape[0]
    nb = HALF // COL_TILE

    def body(gb_ref, gc_ref, h_ref, w_ref, o_ref):
        conv = _conv3(gc_ref[...] * h_ref[...], w_ref[...])
        o_ref[...] = (gb_ref[...] * conv).astype(BF16)

    def col(k):
        return pl.BlockSpec((s, COL_TILE), lambda j: (0, k * nb + j))

    wspec = pl.BlockSpec((3, COL_TILE), lambda j: (0, j))
    return _pcall(body, name=name, grid=(nb,), in_specs=[col(3), col(4), col(5), wspec], out_specs=col(0),
                  out_shape=jax.ShapeDtypeStruct((s, HALF), BF16), compiler_params=_params("parallel"))(p, p, p, w)


def _sc_bwd(p, w, d_ab, name):
    s = p.shape[0]
    nb = HALF // COL_TILE

    def body(gb_ref, gc_ref, h_ref, w_ref, d_ref, dgb_ref, dgc_ref, dh_ref, dw_ref):
        gc, hin, wv, d = gc_ref[...], h_ref[...], w_ref[...], d_ref[...]
        sig = gc * hin
        shifted = (_shift_down(sig, 1), _shift_down(sig, 2))
        dgb_ref[...] = d * _conv3(sig, wv, shifted)
        dconv = d * gb_ref[...]
        _conv3_wgrad(dconv, sig, shifted, dw_ref)
        dsig = _conv3_transpose(dconv, wv)
        dgc_ref[...] = dsig * hin
        dh_ref[...] = dsig * gc

    def col(k):
        return pl.BlockSpec((s, COL_TILE), lambda j: (0, k * nb + j))

    wspec = pl.BlockSpec((3, COL_TILE), lambda j: (0, j))
    act = jax.ShapeDtypeStruct((s, HALF), F32)
    return _pcall(body, name=name, grid=(nb,), in_specs=[col(3), col(4), col(5), wspec, col(1)],
                  out_specs=[col(0), col(0), col(0), wspec],
                  out_shape=[act, act, act, jax.ShapeDtypeStruct((3, HALF), F32)],
                  compiler_params=_params("parallel"))(p, p, p, w, d_ab)


def _ffn_act_fwd(u, w, name, after=()):
    s = u.shape[0]
    nb = D_FF // COL_TILE

    def body(ug_ref, uu_ref, wg_ref, wu_ref, o_ref):
        gate = _conv3(ug_ref[...], wg_ref[...])
        up = _conv3(uu_ref[...], wu_ref[...])
        o_ref[...] = (gate * jax.nn.sigmoid(gate) * up).astype(BF16)

    def col(k):
        return pl.BlockSpec((s, COL_TILE), lambda j: (0, k * nb + j))

    def wcol(k):
        return pl.BlockSpec((3, COL_TILE), lambda j: (0, k * nb + j))

    return _pcall(body, after=after, name=name, grid=(nb,), in_specs=[col(0), col(1), wcol(0), wcol(1)], out_specs=col(0),
                  out_shape=jax.ShapeDtypeStruct((s, D_FF), BF16),
                  compiler_params=_params("parallel"))(u, u, w, w)


def _ffn_act_bwd(u, w, d_f, name):
    s = u.shape[0]
    nb = D_FF // COL_TILE

    def body(ug_ref, uu_ref, wg_ref, wu_ref, d_ref, du_ref, dwg_ref, dwu_ref):
        ug, uu, wg, wu, d = ug_ref[...], uu_ref[...], wg_ref[...], wu_ref[...], d_ref[...]
        ug_shifted = (_shift_down(ug, 1), _shift_down(ug, 2))
        uu_shifted = (_shift_down(uu, 1), _shift_down(uu, 2))
        gate = _conv3(ug, wg, ug_shifted)
        up = _conv3(uu, wu, uu_shifted)
        sig = jax.nn.sigmoid(gate)
        d_up = d * gate * sig
        d_gate = d * up * sig * (1.0 + gate * (1.0 - sig))
        _conv3_wgrad(d_gate, ug, ug_shifted, dwg_ref)
        _conv3_wgrad(d_up, uu, uu_shifted, dwu_ref)
        du_ref[0] = _conv3_transpose(d_gate, wg).astype(BF16)
        du_ref[1] = _conv3_transpose(d_up, wu).astype(BF16)

    def col(k):
        return pl.BlockSpec((s, COL_TILE), lambda j: (0, k * nb + j))

    def wcol(k):
        return pl.BlockSpec((3, COL_TILE), lambda j: (0, k * nb + j))

    both = pl.BlockSpec((2, s, COL_TILE), lambda j: (0, 0, j))
    wsh = jax.ShapeDtypeStruct((3, D_FF), F32)
    return _pcall(body, name=name, grid=(nb,), in_specs=[col(0), col(1), wcol(0), wcol(1), col(0)],
                  out_specs=[both, wcol(0), wcol(0)], out_shape=[jax.ShapeDtypeStruct((2, s, D_FF), BF16), wsh, wsh],
                  compiler_params=_params("parallel"))(u, u, w, w, d_f)


def _sg_common(u, v, g, w_ref, bias, mixed_ref):
    rows = u.shape[0]
    gu = _gelu(u)
    gv = _gelu(v)
    xc = gv - jnp.mean(gv, axis=-1, keepdims=True)
    rstd = lax.rsqrt(jnp.mean(xc * xc, axis=-1, keepdims=True) + EPS)
    xhat = xc * rstd
    vn = xhat * g
    tril = lax.broadcasted_iota(jnp.int32, (HEAD, HEAD), 0) >= lax.broadcasted_iota(jnp.int32, (HEAD, HEAD), 1)
    wts = [jnp.where(tril, w_ref[grp], 0.0).astype(BF16) for grp in range(N_HEADS)]
    for n in range(rows // HEAD):
        for grp in range(N_HEADS):
            blk = vn[n * HEAD:(n + 1) * HEAD, grp * HEAD:(grp + 1) * HEAD].astype(BF16)
            mixed_ref[n * HEAD:(n + 1) * HEAD, grp * HEAD:(grp + 1) * HEAD] = _dot_nn(wts[grp], blk)
    mixed = mixed_ref[...] + jnp.concatenate([bias] * (rows // HEAD), axis=0)
    return gu, xhat, rstd, vn, mixed, wts, tril


def _sg_fwd(p, sg_w, bias, g, name):
    s = p.shape[0]

    def body(u_ref, v_ref, w_ref, b_ref, g_ref, o_ref, mixed_ref):
        gu, _, _, _, mixed, _, _ = _sg_common(u_ref[...], v_ref[...], g_ref[...], w_ref, b_ref[...], mixed_ref)
        o_ref[...] = (gu * mixed).astype(BF16)

    def half(k):
        return pl.BlockSpec((ROW_TILE, HALF), lambda i: (i, k))

    wspec = pl.BlockSpec((N_HEADS, HEAD, HEAD), lambda i: (0, 0, 0))
    bspec = pl.BlockSpec((HEAD, HALF), lambda i: (0, 0))
    gspec = pl.BlockSpec((1, HALF), lambda i: (0, 0))
    return _pcall(body, name=name, grid=(s // ROW_TILE,), in_specs=[half(0), half(1), wspec, bspec, gspec],
                  out_specs=half(0), out_shape=jax.ShapeDtypeStruct((s, HALF), BF16),
                  scratch_shapes=[pltpu.VMEM((ROW_TILE, HALF), F32)],
                  compiler_params=_params("parallel"))(p, p, sg_w, bias, g)


def _sg_bwd(p, sg_w, bias, g, d_cd, name):
    s = p.shape[0]
    nsteps = s // ROW_TILE

    def body(u_ref, v_ref, w_ref, b_ref, g_ref, d_ref, du_ref, dv_ref, dw_ref, db_ref, dg_ref,
             mixed_ref, dvn_ref, dbias_ref):
        i = pl.program_id(0)
        u, v, gain, d = u_ref[...], v_ref[...], g_ref[...], d_ref[...]
        gu, xhat, rstd, vn, mixed, wts, tril = _sg_common(u, v, gain, w_ref, b_ref[...], mixed_ref)

        @pl.when(i == 0)
        def _():
            dw_ref[...] = jnp.zeros_like(dw_ref)
            dg_ref[...] = jnp.zeros_like(dg_ref)
            dbias_ref[...] = jnp.zeros_like(dbias_ref)

        du_ref[...] = d * mixed * _gelu_grad(u)
        dm = d * gu
        for n in range(ROW_TILE // HEAD):
            rs = slice(n * HEAD, (n + 1) * HEAD)
            dbias_ref[...] += dm[rs, :]
            for grp in range(N_HEADS):
                cs = slice(grp * HEAD, (grp + 1) * HEAD)
                dm_blk = dm[rs, cs].astype(BF16)
                dw_ref[grp] += jnp.where(tril, _dot_nt(dm_blk, vn[rs, cs].astype(BF16)), 0.0)
                dvn_ref[rs, cs] = _dot_tn(wts[grp], dm_blk)
        dvn = dvn_ref[...]
        dg_ref[...] += jnp.sum(dvn * xhat, axis=0, keepdims=True)
        dxh = dvn * gain
        d_gv = rstd * (dxh - jnp.mean(dxh, axis=-1, keepdims=True) - xhat * jnp.mean(dxh * xhat, axis=-1, keepdims=True))
        dv_ref[...] = d_gv * _gelu_grad(v)

        @pl.when(i == nsteps - 1)
        def _():
            lane = lax.broadcasted_iota(jnp.int32, (HEAD, HEAD), 1)
            out = jnp.zeros((HEAD, HEAD), F32)
            for grp in range(N_HEADS):
                tot = jnp.sum(dbias_ref[:, grp * HEAD:(grp + 1) * HEAD], axis=1, keepdims=True)
                out = out + jnp.where(lane == grp, tot, 0.0)
            db_ref[...] = out

    def half(k):
        return pl.BlockSpec((ROW_TILE, HALF), lambda i: (i, k))

    wspec = pl.BlockSpec((N_HEADS, HEAD, HEAD), lambda i: (0, 0, 0))
    bspec = pl.BlockSpec((HEAD, HALF), lambda i: (0, 0))
    gspec = pl.BlockSpec((1, HALF), lambda i: (0, 0))
    dbspec = pl.BlockSpec((HEAD, HEAD), lambda i: (0, 0))
    act = jax.ShapeDtypeStruct((s, HALF), F32)
    return _pcall(body, name=name, grid=(nsteps,), in_specs=[half(0), half(1), wspec, bspec, gspec, half(0)],
                  out_specs=[half(0), half(0), wspec, dbspec, gspec],
                  out_shape=[act, act, jax.ShapeDtypeStruct((N_HEADS, HEAD, HEAD), F32),
                             jax.ShapeDtypeStruct((HEAD, HEAD), F32), jax.ShapeDtypeStruct((1, HALF), F32)],
                  scratch_shapes=[pltpu.VMEM((ROW_TILE, HALF), F32), pltpu.VMEM((ROW_TILE, HALF), F32),
                                  pltpu.VMEM((HEAD, HALF), F32)],
                  compiler_params=_params("arbitrary"))(p, p, sg_w, bias, g, d_cd)


def _fox_prep(f, b, name):
    s = f.shape[0]
    t = ATT_BLOCK

    def body(f_ref, b_ref, c_ref):
        tri = (lax.broadcasted_iota(jnp.int32, (t, t), 0) >= lax.broadcasted_iota(jnp.int32, (t, t), 1)).astype(BF16)
        carry = jnp.zeros((1, 128), F32)
        for n in range(s // t):
            lf = _log_sigmoid(f_ref[n * t:(n + 1) * t, :] + b_ref[...])
            hi, mid, lo = _split3(lf)
            c_ref[n * t:(n + 1) * t, :] = _dot_nn(tri, hi) + _dot_nn(tri, mid) + _dot_nn(tri, lo) + carry
            carry = carry + jnp.sum(lf, axis=0, keepdims=True)

    return _pcall(body, name=name, in_specs=[VMEM_SPEC, VMEM_SPEC], out_specs=VMEM_SPEC,
                  out_shape=jax.ShapeDtypeStruct((s, 128), F32))(f, b)


def _fox_post(drow, dcol, f, b, name):
    s = f.shape[0]
    t = ATT_BLOCK

    def body(drow_ref, dcol_ref, f_ref, b_ref, df_ref, db_ref):
        tri = (lax.broadcasted_iota(jnp.int32, (t, t), 1) >= lax.broadcasted_iota(jnp.int32, (t, t), 0)).astype(BF16)
        lane = lax.broadcasted_iota(jnp.int32, (t, 128), 1)
        carry = jnp.zeros((1, 128), F32)
        db = jnp.zeros((1, 128), F32)
        for n in reversed(range(s // t)):
            rs = slice(n * t, (n + 1) * t)
            dc = jnp.zeros((t, 128), F32)
            for h in range(N_HEADS):
                dc = jnp.where(lane == h, drow_ref[h, rs, :] - dcol_ref[rs, _head_cols(h)], dc)
            hi, mid, lo = _split3(dc)
            dlogf = _dot_nn(tri, hi) + _dot_nn(tri, mid) + _dot_nn(tri, lo) + carry
            carry = carry + jnp.sum(dc, axis=0, keepdims=True)
            df = dlogf * jax.nn.sigmoid(-(f_ref[rs, :] + b_ref[...]))
            df_ref[rs, :] = df
            db = db + jnp.sum(df, axis=0, keepdims=True)
        db_ref[...] = db

    return _pcall(body, name=name, in_specs=[VMEM_SPEC] * 4, out_specs=[VMEM_SPEC, VMEM_SPEC],
                  out_shape=[jax.ShapeDtypeStruct((s, 128), F32), jax.ShapeDtypeStruct((1, 128), F32)],
                  compiler_params=pltpu.CompilerParams(vmem_limit_bytes=VMEM_LIMIT_BYTES))(drow, dcol, f, b)


def _fox_specs(s):
    t = ATT_BLOCK
    ccol = pl.BlockSpec((HEADS_PER_STEP, t, 1), lambda g, i: (g, i, 0))
    crow = pl.BlockSpec((HEADS_PER_STEP, s // t, 1, t), lambda g, i: (g, 0, 0, 0))
    return ccol, crow


def _fox_fwd(p, c_col, c_row, name, after=()):
    s = p.shape[0]
    t = ATT_BLOCK

    def body(q_ref, k_ref, v_ref, cc_ref, cr_ref, o_ref, lse_ref):
        i = pl.program_id(1)
        heads = range(HEADS_PER_STEP)
        q = [q_ref[:, _head_cols(h)].astype(BF16) for h in heads]
        ct = [cc_ref[h] for h in heads]

        def tile(n, carry, diagonal):
            ks = pl.multiple_of(n * t, t)
            out = []
            for h in heads:
                acc, m, l = carry[h]
                kblk = k_ref[pl.ds(ks, t), _head_cols(h)].astype(BF16)
                vblk = v_ref[pl.ds(ks, t), _head_cols(h)].astype(BF16)
                logit = _dot_nt(q[h], kblk) * ATT_SCALE + ct[h] - cr_ref[h, n]
                if diagonal:
                    causal = lax.broadcasted_iota(jnp.int32, (t, t), 1) <= lax.broadcasted_iota(jnp.int32, (t, t), 0)
                    logit = jnp.where(causal, logit, NEG)
                m_new = jnp.maximum(m, jnp.max(logit, axis=1, keepdims=True))
                alpha = jnp.exp(m - m_new)
                pr = jnp.exp(logit - m_new)
                l = alpha * l + jnp.sum(pr, axis=1, keepdims=True)
                out.append((alpha * acc + _dot_nn(pr.astype(BF16), vblk), m_new, l))
            return tuple(out)

        init = tuple((jnp.zeros((t, HEAD), F32), jnp.full((t, 1), NEG, F32), jnp.zeros((t, 1), F32)) for _ in heads)
        carry = tile(i, lax.fori_loop(0, i, lambda n, c: tile(n, c, False), init), True)
        for h in heads:
            acc, m, l = carry[h]
            o_ref[:, _head_cols(h)] = acc / l
            lse_ref[h] = m + jnp.log(l)

    qspec, kspec, vspec = _head_specs(s, (2 * N_HEADS, 3 * N_HEADS, 4 * N_HEADS))
    ccol, crow = _fox_specs(s)
    ospec = pl.BlockSpec((t, GROUP_W), lambda g, i: (i, g))
    return _pcall(body, after=after, name=name, grid=(N_GROUPS, s // t), in_specs=[qspec, kspec, vspec, ccol, crow],
                  out_specs=[ospec, ccol],
                  out_shape=[jax.ShapeDtypeStruct((s, HALF), F32), jax.ShapeDtypeStruct((N_HEADS, s, 1), F32)],
                  compiler_params=_params("parallel", "parallel", vmem_limit=ATT_VMEM_LIMIT_BYTES))(p, p, p, c_col, c_row)


def _fox_bwd(p, c_col, c_row, lse, d_cd, d_out, name):
    s = p.shape[0]
    t = ATT_BLOCK

    def body(q_ref, k_ref, v_ref, cc_ref, cr_ref, lse_ref, do_ref, o_ref, dq_ref, dk_ref, dv_ref, dcol_ref, drow_ref):
        i = pl.program_id(1)

        @pl.when(i == 0)
        def _():
            dk_ref[...] = jnp.zeros_like(dk_ref)
            dv_ref[...] = jnp.zeros_like(dv_ref)
            dcol_ref[...] = jnp.zeros_like(dcol_ref)

        heads = range(HEADS_PER_STEP)
        q = [q_ref[:, _head_cols(h)].astype(BF16) for h in heads]
        do = [do_ref[:, _head_cols(h)].astype(BF16) for h in heads]
        delta = [jnp.sum(do_ref[:, _head_cols(h)] * o_ref[:, _head_cols(h)], axis=1, keepdims=True) for h in heads]
        ct = [cc_ref[h] for h in heads]
        lse_v = [lse_ref[h] for h in heads]
        ones = jnp.ones((t, HEAD), BF16)

        def tile(n, carry, diagonal):
            ks = pl.multiple_of(n * t, t)
            out = []
            for h in heads:
                dq, drow = carry[h]
                kblk = k_ref[pl.ds(ks, t), _head_cols(h)].astype(BF16)
                vblk = v_ref[pl.ds(ks, t), _head_cols(h)].astype(BF16)
                logit = _dot_nt(q[h], kblk) * ATT_SCALE + ct[h] - cr_ref[h, n]
                pr = jnp.exp(logit - lse_v[h])
                if diagonal:
                    causal = lax.broadcasted_iota(jnp.int32, (t, t), 1) <= lax.broadcasted_iota(jnp.int32, (t, t), 0)
                    pr = jnp.where(causal, pr, 0.0)
                ds = pr * (_dot_nt(do[h], vblk) - delta[h])
                dsb = ds.astype(BF16)
                dv_ref[pl.ds(ks, t), _head_cols(h)] += _dot_tn(pr.astype(BF16), do[h])
                dk_ref[pl.ds(ks, t), _head_cols(h)] += _dot_tn(dsb, q[h]) * ATT_SCALE
                dcol_ref[pl.ds(ks, t), _head_cols(h)] += _dot_tn(dsb, ones)
                out.append((dq + _dot_nn(dsb, kblk) * ATT_SCALE,
                            drow + jnp.sum(dsb.astype(F32), axis=1, keepdims=True)))
            return tuple(out)

        init = tuple((jnp.zeros((t, HEAD), F32), jnp.zeros((t, 1), F32)) for _ in heads)
        carry = tile(i, lax.fori_loop(0, i, lambda n, c: tile(n, c, False), init), True)
        for h in heads:
            dq_ref[:, _head_cols(h)] = carry[h][0]
            drow_ref[h] = carry[h][1]

    qspec, kspec, vspec = _head_specs(s, (2 * N_HEADS, 3 * N_HEADS, 4 * N_HEADS))
    ccol, crow = _fox_specs(s)
    dospec = pl.BlockSpec((t, GROUP_W), lambda g, i: (i, N_GROUPS + g))
    blk = pl.BlockSpec((t, GROUP_W), lambda g, i: (i, g))
    whole = pl.BlockSpec((s, GROUP_W), lambda g, i: (0, g))
    shape = jax.ShapeDtypeStruct((s, HALF), F32)
    return _pcall(body, name=name, grid=(N_GROUPS, s // t),
                  in_specs=[qspec, kspec, vspec, ccol, crow, ccol, dospec, blk],
                  out_specs=[blk, whole, whole, whole, ccol],
                  out_shape=[shape, shape, shape, shape, jax.ShapeDtypeStruct((N_HEADS, s, 1), F32)],
                  compiler_params=_params("parallel", "arbitrary", vmem_limit=ATT_VMEM_LIMIT_BYTES))(p, p, p, c_col, c_row, lse, d_cd, d_out)


def _row_tile(rows, cap):
    for t in (256, 128, 64, 32, 16, 8):
        if t <= cap and rows % t == 0:
            return t
    return rows


def _adamw_small(ws, gs, ms, vs, name):
    n = len(ws)
    c1 = 1.0 / (1.0 - ADAM_B1 ** ADAM_STEP)
    c2 = 1.0 / (1.0 - ADAM_B2 ** ADAM_STEP)

    def body(*refs):
        for k in range(n):
            w_ref, g_ref, m_ref, v_ref = (refs[j * n + k] for j in range(4))
            d_ref, nm_ref, nv_ref = (refs[(4 + j) * n + k] for j in range(3))
            gv = g_ref[...]
            nm = ADAM_B1 * m_ref[...] + (1.0 - ADAM_B1) * gv
            nv = ADAM_B2 * v_ref[...] + (1.0 - ADAM_B2) * (gv * gv)
            nm_ref[...] = nm
            nv_ref[...] = nv
            d_ref[...] = -ADAM_LR * ((nm * c1) / (jnp.sqrt(nv * c2) + ADAM_EPS) + ADAM_WD * w_ref[...])

    shapes = [jax.ShapeDtypeStruct(w.shape, F32) for w in ws] * 3
    outs = _pcall(body, name=name, in_specs=[VMEM_SPEC] * (4 * n), out_specs=[VMEM_SPEC] * (3 * n), out_shape=shapes,
                  compiler_params=pltpu.CompilerParams(vmem_limit_bytes=VMEM_LIMIT_BYTES))(*ws, *gs, *ms, *vs)
    return outs[:n], outs[n:2 * n], outs[2 * n:]


def _half_shape(whole_shape, kind):
    if kind == "col":
        return (whole_shape[0] // 2, whole_shape[1] // 4)
    if kind == "row":
        return (whole_shape[0] // 8, whole_shape[1])
    return (whole_shape[1] // 2, whole_shape[2])


def _own_half_spec(whole_shape, kind, tr):
    hr, hc = _half_shape(whole_shape, kind)
    nb = hr // tr
    if kind == "col":
        return pl.BlockSpec((tr, hc), lambda i, pos: (pos[1] * nb + i, pos[0]))
    if kind == "row":
        return pl.BlockSpec((tr, hc), lambda i, pos: ((2 * pos[0] + pos[1]) * nb + i, 0))
    return pl.BlockSpec((None, tr, hc), lambda i, pos: (pos[0], pos[1] * nb + i, 0))


def _sum_partials(pos, grad, landed, kind, name):
    hr, hc = _half_shape(grad.shape, kind)
    tr = _row_tile(hr, 64)

    def body(pos_ref, g_ref, p_ref, o_ref):
        acc = g_ref[...].astype(F32)
        for k in range(N_DEV - 1):
            acc = acc + p_ref[k].astype(F32)
        o_ref[...] = acc

    grid_spec = pltpu.PrefetchScalarGridSpec(
        num_scalar_prefetch=1, grid=(hr // tr,),
        in_specs=[_own_half_spec(grad.shape, kind, tr), pl.BlockSpec((N_DEV - 1, tr, hc), lambda i, pos: (0, i, 0))],
        out_specs=pl.BlockSpec((tr, hc), lambda i, pos: (i, 0)))
    return _pcall(body, name=name, grid_spec=grid_spec, out_shape=jax.ShapeDtypeStruct((hr, hc), F32),
                  compiler_params=_params("parallel"))(pos, grad, landed)


def _adamw_shard(pos, w, g_mine, g_sibling, m, v, name):
    hr, hc = g_mine.shape
    tr = _row_tile(hr, 128)
    nb = hr // tr
    c1 = 1.0 / (1.0 - ADAM_B1 ** ADAM_STEP)
    c2 = 1.0 / (1.0 - ADAM_B2 ** ADAM_STEP)

    def body(pos_ref, w_ref, gm_ref, gs_ref, m_ref, v_ref, g_ref, d_ref, nm_ref, nv_ref):
        mine = (pl.program_id(0) // nb) == pos_ref[1]
        gv = jnp.where(mine, gm_ref[...], gs_ref[...])
        nm = ADAM_B1 * m_ref[...] + (1.0 - ADAM_B1) * gv
        nv = ADAM_B2 * v_ref[...] + (1.0 - ADAM_B2) * (gv * gv)
        g_ref[...] = gv
        nm_ref[...] = nm
        nv_ref[...] = nv
        d_ref[...] = -ADAM_LR * ((nm * c1) / (jnp.sqrt(nv * c2) + ADAM_EPS) + ADAM_WD * w_ref[...])

    full = pl.BlockSpec((tr, hc), lambda i, pos: (i, 0))
    mine_spec = pl.BlockSpec((tr, hc), lambda i, pos: (jnp.clip(i - pos[1] * nb, 0, nb - 1), 0))
    sib_spec = pl.BlockSpec((tr, hc), lambda i, pos: (jnp.clip(i - (1 - pos[1]) * nb, 0, nb - 1), 0))
    grid_spec = pltpu.PrefetchScalarGridSpec(
        num_scalar_prefetch=1, grid=(2 * nb,), in_specs=[full, mine_spec, sib_spec, full, full], out_specs=[full] * 4)
    shape = jax.ShapeDtypeStruct((2 * hr, hc), F32)
    return _pcall(body, name=name, grid_spec=grid_spec, out_shape=[shape] * 4,
                  compiler_params=_params("parallel"))(pos, w, g_mine, g_sibling, m, v)


def _place_shard(pos, shard, kind, name, after=()):
    rows, cols = shard.shape
    tr = _row_tile(rows, 256)
    nb = rows // tr
    if kind == "col":
        out_spec = pl.BlockSpec((tr, cols), lambda i, pos: (i, pos[0]))
    elif kind == "row":
        out_spec = pl.BlockSpec((tr, cols), lambda i, pos: (pos[0] * nb + i, 0))
    else:
        out_spec = pl.BlockSpec((None, tr, cols), lambda i, pos: (pos[0], i, 0))

    def body(pos_ref, s_ref, *rest):
        rest[-1][...] = s_ref[...].astype(BF16)

    grid_spec = pltpu.PrefetchScalarGridSpec(
        num_scalar_prefetch=1, grid=(nb,),
        in_specs=[pl.BlockSpec((tr, cols), lambda i, pos: (i, 0))] + [pl.BlockSpec(memory_space=pl.ANY)] * len(after),
        out_specs=out_spec)
    return _pcall(body, name=name, grid_spec=grid_spec,
                  out_shape=jax.ShapeDtypeStruct(_whole_shape(shard.shape, kind), BF16),
                  compiler_params=_params("parallel"))(pos, shard, *after)


N_DEV = 8
RELATIONS = [(r >> 2 & 1, r >> 1 & 1, r & 1) for r in range(1, N_DEV)]


def _position():
    return lax.axis_index("x"), lax.axis_index("y"), lax.axis_index("c")


def _related(pos, rel):
    return tuple(1 - p if f else p for p, f in zip(pos, rel))


def _index(pos):
    return 4 * pos[0] + 2 * pos[1] + pos[2]


def _window(ref, kind, pos):
    px, py, pc = pos
    j = 2 * px + py
    if kind == "col":
        r, c = ref.shape
        return ref.at[pl.ds(pc * (r // 2), r // 2), pl.ds(pl.multiple_of(j * (c // 4), 128), c // 4)]
    if kind == "row":
        rj = ref.shape[0] // 4
        return ref.at[pl.ds(j * rj + pc * (rj // 2), rj // 2), :]
    r = ref.shape[1]
    return ref.at[j, pl.ds(pc * (r // 2), r // 2), :]


def _whole_shape(shard_shape, kind):
    r, c = shard_shape
    return {"col": (r, 4 * c), "row": (4 * r, c), "maj": (4, r, c)}[kind]


SEM_SPEC = pl.BlockSpec(memory_space=pltpu.SEMAPHORE)
ANY_SPEC = pl.BlockSpec(memory_space=pl.ANY)
DATAFLOW = pltpu.SideEffectType.DATAFLOW_SIDE_EFFECTING
TOKEN = jax.ShapeDtypeStruct((8, 128), F32)


def _hbm(a):
    return pltpu.with_memory_space_constraint(a, pltpu.HBM)


def _chips(x, y):
    return [(1 - x, y), (x, 1 - y), (1 - x, 1 - y)]


def _split_start(body, name, buffers, n_sems, after=()):
    n = len(buffers)

    def wrapped(*refs):
        body(refs[:n], refs[n], refs[n + 1])
        refs[-1][...] = jnp.zeros_like(refs[-1])

    outs = _pcall(
        wrapped, after=after, name=name, in_specs=[HBM_SPEC] * n,
        out_specs=[SEM_SPEC, SEM_SPEC] + [HBM_SPEC] * n + [VMEM_SPEC],
        out_shape=[pltpu.SemaphoreType.DMA(n_sems), pltpu.SemaphoreType.DMA(n_sems)]
        + [pltpu.HBM(b.shape, b.dtype) for b in buffers] + [TOKEN],
        input_output_aliases={i: 2 + i for i in range(n)},
        compiler_params=pltpu.CompilerParams(has_side_effects=DATAFLOW))(*[_hbm(b) for b in buffers])
    return outs[0], outs[1], list(outs[2:2 + n]), outs[2 + n]


def _split_wait(body, name, buffers, send_sems, recv_sems, after):
    n = len(buffers)
    after = list(after) if isinstance(after, (list, tuple)) else [after]

    def wrapped(*refs):
        body(refs[:n], refs[n], refs[n + 1])

    outs = _pcall(
        wrapped, name=name, in_specs=[HBM_SPEC] * n + [SEM_SPEC, SEM_SPEC] + [ANY_SPEC] * len(after),
        out_specs=[HBM_SPEC] * n, out_shape=[pltpu.HBM(b.shape, b.dtype) for b in buffers],
        input_output_aliases={i: i for i in range(n)},
        compiler_params=pltpu.CompilerParams(has_side_effects=DATAFLOW))(*buffers, send_sems, recv_sems, *after)
    return list(outs)


def _gather_start(wholes, kinds, name, after=()):
    def body(w_refs, send_sems, recv_sems):
        x, y, c = _position()
        for w, ref in enumerate(w_refs):
            mine = _window(ref, kinds[w], (x, y, c))
            for k, chip in enumerate(_chips(x, y)):
                pltpu.make_async_remote_copy(src_ref=mine, dst_ref=mine, send_sem=send_sems.at[3 * w + k],
                                             recv_sem=recv_sems.at[3 * w + k], device_id=(*chip, c),
                                             device_id_type=MESH).start()

    return _split_start(body, name, wholes, (3 * len(wholes),), after)


def _gather_forward(wholes, kinds, send1, recv1, after, name):
    n = len(wholes)

    def wrapped(*refs):
        w_refs, s1, r1, s2, r2 = refs[:n], refs[n], refs[n + 1], refs[n + 3], refs[n + 4]
        x, y, c = _position()
        for k, chip in enumerate(_chips(x, y)):
            for w, ref in enumerate(w_refs):
                theirs = _window(ref, kinds[w], (*chip, c))
                pltpu.make_async_remote_copy(src_ref=theirs, dst_ref=theirs, send_sem=s1.at[3 * w + k],
                                             recv_sem=r1.at[3 * w + k], device_id=(*chip, c),
                                             device_id_type=MESH).wait_recv()
                pltpu.make_async_remote_copy(src_ref=theirs, dst_ref=theirs, send_sem=s2.at[3 * w + k],
                                             recv_sem=r2.at[3 * w + k], device_id=(x, y, 1 - c),
                                             device_id_type=MESH).start()
        for w, ref in enumerate(w_refs):
            mine = _window(ref, kinds[w], (x, y, c))
            for k, chip in enumerate(_chips(x, y)):
                pltpu.make_async_remote_copy(src_ref=mine, dst_ref=mine, send_sem=s1.at[3 * w + k],
                                             recv_sem=r1.at[3 * w + k], device_id=(*chip, c),
                                             device_id_type=MESH).wait_send()
        refs[-1][...] = jnp.zeros_like(refs[-1])

    outs = _pcall(
        wrapped, name=name, in_specs=[HBM_SPEC] * n + [SEM_SPEC, SEM_SPEC, ANY_SPEC],
        out_specs=[SEM_SPEC, SEM_SPEC] + [HBM_SPEC] * n + [VMEM_SPEC],
        out_shape=[pltpu.SemaphoreType.DMA((3 * n,)), pltpu.SemaphoreType.DMA((3 * n,))]
        + [pltpu.HBM(b.shape, b.dtype) for b in wholes] + [TOKEN],
        input_output_aliases={i: 2 + i for i in range(n)},
        compiler_params=pltpu.CompilerParams(has_side_effects=DATAFLOW))(*wholes, send1, recv1, after)
    return outs[0], outs[1], list(outs[2:2 + n]), outs[2 + n]


def _gather_finish(wholes, kinds, send2, recv2, after, name):
    def body(w_refs, s2, r2):
        x, y, c = _position()
        for k, chip in enumerate(_chips(x, y)):
            for w, ref in enumerate(w_refs):
                sent = _window(ref, kinds[w], (*chip, c))
                got = _window(ref, kinds[w], (*chip, 1 - c))
                pltpu.make_async_remote_copy(src_ref=sent, dst_ref=got, send_sem=s2.at[3 * w + k],
                                             recv_sem=r2.at[3 * w + k], device_id=(x, y, 1 - c),
                                             device_id_type=MESH).wait()

    return _split_wait(body, name, wholes, send2, recv2, after)


def _gather_small(small, after=()):
    def body(s_ref, o_ref, send_sems, recv_sems, local_sem):
        x, y, c = _position()
        mine = pltpu.make_async_copy(s_ref, o_ref.at[2 * x + y], local_sem)
        mine.start()
        sends = []
        for k, chip in enumerate(_chips(x, y)):
            cp = pltpu.make_async_remote_copy(src_ref=s_ref, dst_ref=o_ref.at[2 * x + y], send_sem=send_sems.at[k],
                                              recv_sem=recv_sems.at[k], device_id=(*chip, c), device_id_type=MESH)
            cp.start()
            sends.append(cp)
        for k, chip in enumerate(_chips(x, y)):
            pltpu.make_async_remote_copy(src_ref=s_ref, dst_ref=o_ref.at[2 * chip[0] + chip[1]], send_sem=send_sems.at[k],
                                         recv_sem=recv_sems.at[k], device_id=(*chip, c), device_id_type=MESH).wait_recv()
        for cp in sends:
            cp.wait_send()
        mine.wait()

    return _pcall(body, after=after, name="gather_small", in_specs=[HBM_SPEC], out_specs=HBM_SPEC,
                  out_shape=jax.ShapeDtypeStruct((4,) + small.shape, small.dtype),
                  scratch_shapes=[pltpu.SemaphoreType.DMA((3,)), pltpu.SemaphoreType.DMA((3,)),
                                  pltpu.SemaphoreType.DMA(())])(small)


def _scatter_copies(g_refs, land_refs, kinds, send_sems, recv_sems):
    me = _position()
    copies = []
    for k, rel in enumerate(RELATIONS):
        peer = _related(me, rel)
        for w, (g_ref, land_ref) in enumerate(zip(g_refs, land_refs)):
            copies.append(pltpu.make_async_remote_copy(
                src_ref=_window(g_ref, kinds[w], peer), dst_ref=land_ref.at[k],
                send_sem=send_sems.at[7 * w + k], recv_sem=recv_sems.at[7 * w + k], device_id=peer,
                device_id_type=MESH))
    return copies


def _scatter_start(grads, kinds, name):
    n = len(grads)
    lands = [lax.empty((N_DEV - 1,) + _half_shape(g.shape, kd), g.dtype) for g, kd in zip(grads, kinds)]

    def body(refs, send_sems, recv_sems):
        for cp in _scatter_copies(refs[:n], refs[n:], kinds, send_sems, recv_sems):
            cp.start()

    send, recv, thru, token = _split_start(body, name, list(grads) + lands, ((N_DEV - 1) * n,))
    return send, recv, thru[:n], thru[n:], token


def _scatter_wait(grads, lands, kinds, send, recv, after, name):
    n = len(grads)

    def body(refs, send_sems, recv_sems):
        for cp in _scatter_copies(refs[:n], refs[n:], kinds, send_sems, recv_sems):
            cp.wait()

    out = _split_wait(body, name, list(grads) + list(lands), send, recv, after)
    return out[:n], out[n:]


def _swap_start(halves, name):
    n = len(halves)
    lands = [lax.empty(h.shape, h.dtype) for h in halves]

    def body(refs, send_sems, recv_sems):
        x, y, c = _position()
        for w in range(n):
            pltpu.make_async_remote_copy(src_ref=refs[w], dst_ref=refs[n + w], send_sem=send_sems.at[w],
                                         recv_sem=recv_sems.at[w], device_id=(x, y, 1 - c), device_id_type=MESH).start()

    send, recv, thru, token = _split_start(body, name, list(halves) + lands, (n,))
    return send, recv, thru[:n], thru[n:], token


def _swap_wait(halves, lands, send, recv, after, name):
    n = len(halves)

    def body(refs, send_sems, recv_sems):
        x, y, c = _position()
        for w in range(n):
            pltpu.make_async_remote_copy(src_ref=refs[w], dst_ref=refs[n + w], send_sem=send_sems.at[w],
                                         recv_sem=recv_sems.at[w], device_id=(x, y, 1 - c), device_id_type=MESH).wait()

    out = _split_wait(body, name, list(halves) + list(lands), send, recv, after)
    return out[:n], out[n:]


def _allreduce_small(v, after=()):
    rows = v.shape[0]

    def body(v_ref, o_ref, recv_ref, send_sems, recv_sems):
        me = _position()
        recv_ref[_index(me)] = v_ref[...]
        sends = []
        for k, rel in enumerate(RELATIONS):
            peer = _related(me, rel)
            cp = pltpu.make_async_remote_copy(
                src_ref=v_ref, dst_ref=recv_ref.at[_index(me)],
                send_sem=send_sems.at[k], recv_sem=recv_sems.at[k], device_id=peer, device_id_type=MESH)
            cp.start()
            sends.append(cp)
        for k, rel in enumerate(RELATIONS):
            peer = _related(me, rel)
            pltpu.make_async_remote_copy(
                src_ref=v_ref, dst_ref=recv_ref.at[_index(peer)],
                send_sem=send_sems.at[k], recv_sem=recv_sems.at[k], device_id=peer, device_id_type=MESH).wait_recv()
        for cp in sends:
            cp.wait_send()
        acc = recv_ref[0]
        for k in range(1, N_DEV):
            acc = acc + recv_ref[k]
        o_ref[...] = acc

    return _pcall(body, after=after, name="allreduce_small", in_specs=[VMEM_SPEC], out_specs=VMEM_SPEC,
                  out_shape=jax.ShapeDtypeStruct((rows, 128), F32),
                  scratch_shapes=[pltpu.VMEM((N_DEV, rows, 128), F32), pltpu.SemaphoreType.DMA((7,)),
                                  pltpu.SemaphoreType.DMA((7,))],
                  compiler_params=pltpu.CompilerParams(vmem_limit_bytes=VMEM_LIMIT_BYTES))(v)


def _pack(arrays):
    flat = []
    for a in arrays:
        a = a.reshape(-1)
        flat.append(jnp.pad(a, (0, -a.shape[0] % 128)))
    flat = jnp.concatenate(flat)
    flat = jnp.pad(flat, (0, -flat.shape[0] % 1024))
    return flat.reshape(-1, 128)


def _unpack(packed, shapes):
    flat = packed.reshape(-1)
    out, at = [], 0
    for shp in shapes:
        size = 1
        for d in shp:
            size *= d
        out.append(flat[at:at + size].reshape(shp))
        at += size + (-size % 128)
    return out


WEIGHTS = ['l0_mix_norm_g', 'l0_w_in', 'l0_sc_conv_w', 'l0_w_out', 'l0_ffn_norm_g', 'l0_ffn_up', 'l0_ffn_conv_w',
           'l0_ffn_down', 'l1_mix_norm_g', 'l1_w_in', 'l1_fox_b_f', 'l1_sg_w', 'l1_sg_b', 'l1_sg_norm_g', 'l1_w_out',
           'l1_ffn_norm_g', 'l1_ffn_up', 'l1_ffn_conv_w', 'l1_ffn_down', 'final_norm_g']
BIG = {'l0_w_in': 'col', 'l0_w_out': 'row', 'l0_ffn_up': 'col', 'l0_ffn_down': 'row',
       'l1_w_in': 'maj', 'l1_w_out': 'row', 'l1_ffn_up': 'col', 'l1_ffn_down': 'row'}
GATHER_GROUPS = [['l0_w_in'], ['l0_w_out'], ['l0_ffn_up'], ['l0_ffn_down'], ['l1_w_in', 'l1_w_out'],
                 ['l1_ffn_up'], ['l1_ffn_down']]
CONV = ['l0_sc_conv_w', 'l0_ffn_conv_w', 'l1_ffn_conv_w']
SMALL = [n for n in WEIGHTS if n not in BIG]
IN_CD = 5 * HALF + N_HEADS


def _ffn_forward(x, g, get_up, behind_act, get_down, conv_w, tag):
    h = _rmsnorm_fwd(x, g, tag + "_norm")
    u = _matmul(h, get_up(h), "nn", F32, tag + "_up")
    f = _ffn_act_fwd(u, conv_w, tag + "_act", after=behind_act(u))
    w_down, tokens = get_down(f)
    return _matmul(f, w_down, "nn", F32, tag + "_down", res=x, after=tokens), (h, u, f)


def _ffn_backward(x, g, w_up, conv_w, w_down, saved, d_out, send_up, send_down, tag):
    h, u, f = saved
    dw_down = _matmul(f, d_out, "tn", BF16, tag + "_dwdown")
    d_f = _matmul(d_out, w_down, "nt", F32, tag + "_df", after=[send_down(dw_down)])
    du, dcw_gate, dcw_up = _ffn_act_bwd(u, conv_w, d_f, tag + "_dact")
    dw_up = _matmul(h, du, "tn", BF16, tag + "_dwup")
    dh = _matmul(du, w_up, "nt", F32, tag + "_dh", after=[send_up(dw_up)])
    dx, dg = _rmsnorm_bwd(x, g, dh, d_out, tag + "_dnorm")
    return dx, dg, jnp.concatenate([dcw_gate, dcw_up], axis=1)


def kernel(x, l0_mix_norm_g, l0_w_in, l0_sc_conv_w, l0_w_out, l0_ffn_norm_g, l0_ffn_up, l0_ffn_conv_w, l0_ffn_down, l1_mix_norm_g, l1_w_in, l1_fox_b_f, l1_sg_w, l1_sg_b, l1_sg_norm_g, l1_w_out, l1_ffn_norm_g, l1_ffn_up, l1_ffn_conv_w, l1_ffn_down, final_norm_g, loss_target, m_l0_mix_norm_g, m_l0_w_in, m_l0_sc_conv_w, m_l0_w_out, m_l0_ffn_norm_g, m_l0_ffn_up, m_l0_ffn_conv_w, m_l0_ffn_down, m_l1_mix_norm_g, m_l1_w_in, m_l1_fox_b_f, m_l1_sg_w, m_l1_sg_b, m_l1_sg_norm_g, m_l1_w_out, m_l1_ffn_norm_g, m_l1_ffn_up, m_l1_ffn_conv_w, m_l1_ffn_down, m_final_norm_g, v_l0_mix_norm_g, v_l0_w_in, v_l0_sc_conv_w, v_l0_w_out, v_l0_ffn_norm_g, v_l0_ffn_up, v_l0_ffn_conv_w, v_l0_ffn_down, v_l1_mix_norm_g, v_l1_w_in, v_l1_fox_b_f, v_l1_sg_w, v_l1_sg_b, v_l1_sg_norm_g, v_l1_w_out, v_l1_ffn_norm_g, v_l1_ffn_up, v_l1_ffn_conv_w, v_l1_ffn_down, v_final_norm_g):
    given = (l0_mix_norm_g, l0_w_in, l0_sc_conv_w, l0_w_out, l0_ffn_norm_g, l0_ffn_up, l0_ffn_conv_w, l0_ffn_down, l1_mix_norm_g, l1_w_in, l1_fox_b_f, l1_sg_w, l1_sg_b, l1_sg_norm_g, l1_w_out, l1_ffn_norm_g, l1_ffn_up, l1_ffn_conv_w, l1_ffn_down, final_norm_g)
    given_m = (m_l0_mix_norm_g, m_l0_w_in, m_l0_sc_conv_w, m_l0_w_out, m_l0_ffn_norm_g, m_l0_ffn_up, m_l0_ffn_conv_w, m_l0_ffn_down, m_l1_mix_norm_g, m_l1_w_in, m_l1_fox_b_f, m_l1_sg_w, m_l1_sg_b, m_l1_sg_norm_g, m_l1_w_out, m_l1_ffn_norm_g, m_l1_ffn_up, m_l1_ffn_conv_w, m_l1_ffn_down, m_final_norm_g)
    given_v = (v_l0_mix_norm_g, v_l0_w_in, v_l0_sc_conv_w, v_l0_w_out, v_l0_ffn_norm_g, v_l0_ffn_up, v_l0_ffn_conv_w, v_l0_ffn_down, v_l1_mix_norm_g, v_l1_w_in, v_l1_fox_b_f, v_l1_sg_w, v_l1_sg_b, v_l1_sg_norm_g, v_l1_w_out, v_l1_ffn_norm_g, v_l1_ffn_up, v_l1_ffn_conv_w, v_l1_ffn_down, v_final_norm_g)
    wt = dict(zip(WEIGHTS, given))
    mom = dict(zip(WEIGHTS, given_m))
    var = dict(zip(WEIGHTS, given_v))
    s = x.shape[1]
    t = ATT_BLOCK
    x0, target = x[0], loss_target[0]
    chip = 2 * lax.axis_index("x") + lax.axis_index("y")

    pos = jnp.stack([chip, lax.axis_index("c")]).astype(jnp.int32)

    conv_widths = [wt[n].shape[1] for n in CONV]
    conv_all = _gather_small(jnp.concatenate([wt[n] for n in CONV], axis=1))
    conv_full, at = {}, 0
    for n, cw in zip(CONV, conv_widths):
        conv_full[n] = jnp.transpose(conv_all[:, :, at:at + cw], (1, 0, 2)).reshape(3, 4 * cw)
        at += cw
    gathers, token = [], conv_all
    for gi, names in enumerate(GATHER_GROUPS):
        placed = [_place_shard(pos, wt[n], BIG[n], "place_" + n, [token]) for n in names]
        send, recv, thru, token = _gather_start(placed, [BIG[n] for n in names], "gather_start_%d" % gi, [token])
        gathers.append((send, recv, thru))
    full = {}

    def forward_gather(gi, after):
        send, recv, thru = gathers[gi]
        kinds = [BIG[n] for n in GATHER_GROUPS[gi]]
        gathers[gi] = _gather_forward(thru, kinds, send, recv, after, "gather_forward_%d" % gi)
        return gathers[gi][3]

    def finish_gather(gi, after):
        send, recv, thru, tok = gathers[gi]
        names = GATHER_GROUPS[gi]
        wholes = _gather_finish(thru, [BIG[n] for n in names], send, recv, tok if after is None else after,
                                "gather_finish_%d" % gi)
        full.update(zip(names, wholes))

    def vec(name):
        return wt[name].reshape(1, -1)

    h0 = _rmsnorm_fwd(x0, vec('l0_mix_norm_g'), "l0_mix_norm", after=[token])
    forward_gather(0, h0)
    finish_gather(0, None)
    p0 = _matmul(h0, full['l0_w_in'], "nn", F32, "l0_in")
    a_out, sb_carries = _sb_fwd(p0, "l0_sb", after=[forward_gather(1, p0)])
    finish_gather(1, a_out)
    b_out = _sc_fwd(p0, conv_full['l0_sc_conv_w'], "l0_sc")
    ab0 = jnp.concatenate([a_out.astype(BF16), b_out], axis=1)
    x1 = _matmul(ab0, full['l0_w_out'], "nn", F32, "l0_out", res=x0, after=[forward_gather(2, b_out)])

    def ffn_weights(up_group, next_group):
        def get_up(h):
            finish_gather(up_group, h)
            return full[GATHER_GROUPS[up_group][0]]

        def behind_act(u):
            return [forward_gather(up_group + 1, u)]

        def get_down(f):
            finish_gather(up_group + 1, f)
            return full[GATHER_GROUPS[up_group + 1][0]], ([forward_gather(next_group, f)] if next_group else ())

        return get_up, behind_act, get_down

    x2, ffn0_saved = _ffn_forward(x1, vec('l0_ffn_norm_g'), *ffn_weights(2, 4), conv_full['l0_ffn_conv_w'], "l0_ffn")
    h2 = _rmsnorm_fwd(x2, vec('l1_mix_norm_g'), "l1_mix_norm")
    finish_gather(4, h2)
    w_in1 = jnp.transpose(full['l1_w_in'], (1, 0, 2)).reshape(D_MODEL, IN_CD)
    w_in1_main = w_in1[:, :5 * HALF]
    w_in1_f = jnp.pad(w_in1[:, 5 * HALF:], ((0, 0), (0, 128 - N_HEADS)))
    p1 = _matmul(h2, w_in1_main, "nn", F32, "l1_in")
    f_logit = _matmul(h2, w_in1_f, "nn", F32, "l1_in_f")
    b_f = jnp.pad(wt['l1_fox_b_f'], (0, 128 - N_HEADS)).reshape(1, 128)
    c_heads = _fox_prep(f_logit, b_f, "l1_fox_prep")[:, :N_HEADS].T
    c_col = c_heads[:, :, None]
    c_row = c_heads.reshape(N_HEADS, s // t, 1, t)
    sg_bias = jnp.repeat(wt['l1_sg_b'].T, HEAD, axis=1)
    sg_gain = vec('l1_sg_norm_g')
    c_out = _sg_fwd(p1, wt['l1_sg_w'], sg_bias, sg_gain, "l1_sg")
    d_out, lse = _fox_fwd(p1, c_col, c_row, "l1_fox", after=[forward_gather(5, c_out)])
    cd1 = jnp.concatenate([c_out, d_out.astype(BF16)], axis=1)
    x3 = _matmul(cd1, full['l1_w_out'], "nn", F32, "l1_out", res=x2)
    x4, ffn1_saved = _ffn_forward(x3, vec('l1_ffn_norm_g'), *ffn_weights(5, None), conv_full['l1_ffn_conv_w'], "l1_ffn")
    dx4, dg_final, loss_part = _loss_head(x4, vec('final_norm_g'), target, "loss_head")

    grads = {'final_norm_g': dg_final}
    scatters = []

    def send_grads(names):
        def start(*group):
            send, recv, thru, lands, tok = _scatter_start(list(group), [BIG[n] for n in names],
                                                          "scatter_start_%d" % len(scatters))
            scatters.append((names, send, recv, thru, lands))
            return tok
        return start

    dx3, grads['l1_ffn_norm_g'], grads['l1_ffn_conv_w'] = _ffn_backward(
        x3, vec('l1_ffn_norm_g'), full['l1_ffn_up'], conv_full['l1_ffn_conv_w'], full['l1_ffn_down'], ffn1_saved, dx4,
        send_grads(['l1_ffn_up']), send_grads(['l1_ffn_down']), "l1_ffn")
    dw_out1 = _matmul(cd1, dx3, "tn", BF16, "l1_dwout")
    d_cd = _matmul(dx3, full['l1_w_out'], "nt", F32, "l1_dcd")
    du, dv, grads['l1_sg_w'], db_sg, grads['l1_sg_norm_g'] = _sg_bwd(p1, wt['l1_sg_w'], sg_bias, sg_gain, d_cd, "l1_dsg")
    grads['l1_sg_b'] = db_sg[:, :N_HEADS].T
    dq, dk, dvv, dcol, drow = _fox_bwd(p1, c_col, c_row, lse, d_cd, d_out, "l1_dfox")
    d_f_logit, d_b_f = _fox_post(drow, dcol, f_logit, b_f, "l1_fox_post")
    grads['l1_fox_b_f'] = d_b_f[0, :N_HEADS]
    dp1 = jnp.concatenate([a.astype(BF16) for a in (du, dv, dq, dk, dvv)], axis=1)
    dw_main = _matmul(h2, dp1, "tn", BF16, "l1_dwin")
    dw_f = _matmul(h2, d_f_logit, "tn", BF16, "l1_dwin_f")
    dw_in1 = jnp.concatenate([dw_main, dw_f[:, :N_HEADS]], axis=1)
    dw_in1 = jnp.transpose(dw_in1.reshape(D_MODEL, 4, IN_CD // 4), (1, 0, 2))
    dh2 = _matmul(dp1, w_in1_main, "nt", F32, "l1_dh", after=[send_grads(['l1_w_out', 'l1_w_in'])(dw_out1, dw_in1)])
    dh2 = _matmul(d_f_logit, w_in1_f, "nt", F32, "l1_dh_f", res=dh2)
    dx2, grads['l1_mix_norm_g'] = _rmsnorm_bwd(x2, vec('l1_mix_norm_g'), dh2, dx3, "l1_dmix_norm")
    dx1, grads['l0_ffn_norm_g'], grads['l0_ffn_conv_w'] = _ffn_backward(
        x1, vec('l0_ffn_norm_g'), full['l0_ffn_up'], conv_full['l0_ffn_conv_w'], full['l0_ffn_down'], ffn0_saved, dx2,
        send_grads(['l0_ffn_up']), send_grads(['l0_ffn_down']), "l0_ffn")
    dw_out0 = _matmul(ab0, dx1, "tn", BF16, "l0_dwout")
    d_ab = _matmul(dx1, full['l0_w_out'], "nt", F32, "l0_dab", after=[send_grads(['l0_w_out'])(dw_out0)])
    dq0, dk0, dv0 = _sb_bwd(p0, d_ab, sb_carries, "l0_dsb")
    dgb, dgc, dhin, grads['l0_sc_conv_w'] = _sc_bwd(p0, conv_full['l0_sc_conv_w'], d_ab, "l0_dsc")
    dp0 = jnp.concatenate([a.astype(BF16) for a in (dq0, dk0, dv0, dgb, dgc, dhin)], axis=1)
    dw_in0 = _matmul(h0, dp0, "tn", BF16, "l0_dwin")
    dh0 = _matmul(dp0, full['l0_w_in'], "nt", F32, "l0_dh", after=[send_grads(['l0_w_in'])(dw_in0)])
    dx0, grads['l0_mix_norm_g'] = _rmsnorm_bwd(x0, vec('l0_mix_norm_g'), dh0, dx1, "l0_dmix_norm")

    shard_grads, delta, new_m, new_v, swaps = {}, {}, {}, {}, {}

    def reduce_group(gi, after):
        names, send, recv, thru, lands = scatters[gi]
        kinds = [BIG[n] for n in names]
        g_thru, landed = _scatter_wait(thru, lands, kinds, send, recv, after, "scatter_wait_%d" % gi)
        halves = [_sum_partials(pos, g, ld, kd, "sum_" + n) for n, g, ld, kd in zip(names, g_thru, landed, kinds)]
        s_send, s_recv, h_thru, s_lands, tok = _swap_start(halves, "swap_start_%d" % gi)
        swaps[gi] = (names, s_send, s_recv, h_thru, s_lands)
        return tok

    def update_group(gi, after):
        names, s_send, s_recv, h_thru, s_lands = swaps[gi]
        mine, theirs = _swap_wait(h_thru, s_lands, s_send, s_recv, after, "swap_wait_%d" % gi)
        for n, gm, gs in zip(names, mine, theirs):
            shard_grads[n], delta[n], new_m[n], new_v[n] = _adamw_shard(pos, wt[n], gm, gs, mom[n], var[n], "adamw_" + n)
        return [delta[n] for n in names]

    after = reduce_group(2, reduce_group(1, reduce_group(0, dx0)))
    after = update_group(2, update_group(1, update_group(0, after)))
    after = reduce_group(5, reduce_group(4, reduce_group(3, after)))
    after = update_group(5, update_group(4, update_group(3, after)))
    after = reduce_group(6, after)
    small_shapes = [conv_full[n].shape if n in CONV else wt[n].shape for n in SMALL] + [loss_part.shape]
    small_all = _allreduce_small(_pack([grads[n] for n in SMALL] + [loss_part]), [after])
    small_sums = _unpack(small_all, small_shapes)
    loss = small_sums[-1][0, 0]
    for n, g in zip(SMALL, small_sums):
        shard_grads[n] = lax.dynamic_slice_in_dim(g, chip * wt[n].shape[1], wt[n].shape[1], axis=1) if n in CONV else g
    update_group(6, small_all)
    def flat2d(a):
        return a.reshape(-1, a.shape[-1])

    small_out = _adamw_small(*[[flat2d(src[n]) for n in SMALL] for src in (wt, shard_grads, mom, var)], "adamw_small")
    for out, arrays in zip((delta, new_m, new_v), small_out):
        out.update((n, a.reshape(wt[n].shape)) for n, a in zip(SMALL, arrays))

    return (loss, dx0[None], *[shard_grads[n] for n in WEIGHTS], *[delta[n] for n in WEIGHTS],
            *[new_m[n] for n in WEIGHTS], *[new_v[n] for n in WEIGHTS])
```

```python
import jax
import jax.numpy as jnp
from jax import lax
from jax.experimental import pallas as pl
from jax.experimental.pallas import tpu as pltpu

F32 = jnp.float32
BF16 = jnp.bfloat16

D_MODEL = 2048
HEAD = 128
N_HEADS = 8
HALF = N_HEADS * HEAD
D_FF = 5632
EPS = 1e-6
ATT_SCALE = HEAD ** -0.5
ATT_BLOCK = 512
NEG = -1e30

ADAM_LR = 0.001
ADAM_B1 = 0.9
ADAM_B2 = 0.999
ADAM_EPS = 1e-08
ADAM_WD = 0.01
ADAM_STEP = 10

VMEM_LIMIT_BYTES = 48 * 1024 * 1024
MM_VMEM_LIMIT_BYTES = 56 * 1024 * 1024
ATT_VMEM_LIMIT_BYTES = 48 * 1024 * 1024
MESH = pl.DeviceIdType.MESH
HBM_SPEC = pl.BlockSpec(memory_space=pltpu.HBM)
VMEM_SPEC = pl.BlockSpec(memory_space=pltpu.VMEM)


def _pcall(body, after=(), **kw):
    if not after:
        return pl.pallas_call(body, **kw)
    n_in, n_after, inner = len(kw["in_specs"]), len(after), body
    kw["in_specs"] = list(kw["in_specs"]) + [pl.BlockSpec(memory_space=pl.ANY)] * n_after

    def body(*refs):
        inner(*refs[:n_in], *refs[n_in + n_after:])

    call = pl.pallas_call(body, **kw)
    return lambda *args: call(*args, *after)


def _params(*semantics, vmem_limit=VMEM_LIMIT_BYTES):
    return pltpu.CompilerParams(dimension_semantics=semantics, vmem_limit_bytes=vmem_limit)


def _pick(n, cap):
    best = None
    for t in range(128, min(n, cap) + 1, 128):
        if n % t == 0:
            best = t
    return n if best is None else best


def _dot(a, b, dims):
    return lax.dot_general(a, b, (dims, ((), ())), preferred_element_type=F32)


def _dot_nn(a, b):
    return _dot(a, b, ((1,), (0,)))


def _dot_nt(a, b):
    return _dot(a, b, ((1,), (1,)))


def _dot_tn(a, b):
    return _dot(a, b, ((0,), (0,)))


def _split3(x):
    hi = x.astype(BF16)
    r1 = x - hi.astype(F32)
    mid = r1.astype(BF16)
    lo = (r1 - mid.astype(F32)).astype(BF16)
    return hi, mid, lo


def _log_sigmoid(z):
    return jnp.minimum(z, 0.0) - jnp.log1p(jnp.exp(-jnp.abs(z)))


_GELU_K = 0.7978845608028654


def _gelu(x):
    return 0.5 * x * (1.0 + jnp.tanh(_GELU_K * (x + 0.044715 * x * x * x)))


def _gelu_grad(x):
    t = jnp.tanh(_GELU_K * (x + 0.044715 * x * x * x))
    return 0.5 * (1.0 + t) + 0.5 * x * (1.0 - t * t) * _GELU_K * (1.0 + 3.0 * 0.044715 * x * x)


SUBLANES = 8


def _shift_down(x, k):
    rolled = pltpu.roll(x, k, axis=0)
    head = rolled[:SUBLANES]
    head = jnp.where(lax.broadcasted_iota(jnp.int32, head.shape, 0) >= k, head, 0.0)
    return jnp.concatenate([head, rolled[SUBLANES:]], axis=0)


def _shift_up(x, k):
    n = x.shape[0]
    rolled = pltpu.roll(x, n - k, axis=0)
    tail = rolled[n - SUBLANES:]
    tail = jnp.where(lax.broadcasted_iota(jnp.int32, tail.shape, 0) < SUBLANES - k, tail, 0.0)
    return jnp.concatenate([rolled[:n - SUBLANES], tail], axis=0)


def _conv3(s, w, shifted=None):
    s1, s2 = shifted if shifted else (_shift_down(s, 1), _shift_down(s, 2))
    return w[0:1, :] * s2 + w[1:2, :] * s1 + w[2:3, :] * s


def _conv3_transpose(d, w):
    return w[2:3, :] * d + w[1:2, :] * _shift_up(d, 1) + w[0:1, :] * _shift_up(d, 2)


def _conv3_wgrad(d, s, shifted, dw_ref):
    s1, s2 = shifted
    dw_ref[0:1, :] = jnp.sum(d * s2, axis=0, keepdims=True)
    dw_ref[1:2, :] = jnp.sum(d * s1, axis=0, keepdims=True)
    dw_ref[2:3, :] = jnp.sum(d * s, axis=0, keepdims=True)


MM_TILE_M, MM_TILE_N, MM_TILE_K = 1408, 512, 5632


def _matmul(a, b, mode, out_dtype, name, res=None, after=()):
    a_parts = a.shape[0] if a.ndim == 3 else 1
    b_parts = b.shape[0] if b.ndim == 3 else 1
    a_shape = (a.shape[1], a_parts * a.shape[2]) if a.ndim == 3 else a.shape
    b_shape = (b.shape[1], b_parts * b.shape[2]) if b.ndim == 3 else b.shape
    assert (a_parts == 1 or mode != "tn") and (b_parts == 1 or mode == "tn")
    if mode == "nn":
        (m, k), (k2, n) = a_shape, b_shape
    elif mode == "nt":
        (m, k), (n, k2) = a_shape, b_shape
    else:
        (k, m), (k2, n) = a_shape, b_shape
    assert k == k2, (a.shape, b.shape, mode)
    tm, tn, tk = _pick(m, MM_TILE_M), _pick(n // b_parts, MM_TILE_N), _pick(k // a_parts, MM_TILE_K)
    nk = k // tk
    if mode == "tn":
        a_spec = pl.BlockSpec((tk, tm), lambda i, j, kk: (kk, i))
    elif a_parts > 1:
        per = nk // a_parts
        a_spec = pl.BlockSpec((None, tm, tk), lambda i, j, kk: (kk // per, i, kk % per))
    else:
        a_spec = pl.BlockSpec((tm, tk), lambda i, j, kk: (i, kk))
    if mode == "nt":
        b_spec = pl.BlockSpec((tn, tk), lambda i, j, kk: (j, kk))
    elif b_parts > 1:
        per = n // b_parts // tn
        b_spec = pl.BlockSpec((None, tk, tn), lambda i, j, kk: (j // per, kk, j % per))
    else:
        b_spec = pl.BlockSpec((tk, tn), lambda i, j, kk: (kk, j))
    o_spec = pl.BlockSpec((tm, tn), lambda i, j, kk: (i, j))
    dims = {"nn": ((1,), (0,)), "nt": ((1,), (1,)), "tn": ((0,), (0,))}[mode]
    has_res = res is not None

    def body(*refs):
        a_ref, b_ref = refs[0], refs[1]
        r_ref = refs[2] if has_res else None
        o_ref = refs[3] if has_res else refs[2]
        part = _dot(a_ref[...].astype(BF16), b_ref[...].astype(BF16), dims)

        def finish(total):
            if has_res:
                total = total + r_ref[...]
            o_ref[...] = total.astype(out_dtype)

        if nk == 1:
            finish(part)
        else:
            acc_ref = refs[-1]
            kk = pl.program_id(2)

            @pl.when(kk == 0)
            def _():
                acc_ref[...] = part

            @pl.when(kk > 0)
            def _():
                acc_ref[...] += part

            @pl.when(kk == nk - 1)
            def _():
                finish(acc_ref[...])

    in_specs = [a_spec, b_spec] + ([o_spec] if has_res else [])
    args = (a, b) + ((res,) if has_res else ())
    return _pcall(
        body, after=after, name=name, grid=(m // tm, n // tn, nk),
        in_specs=in_specs, out_specs=o_spec,
        out_shape=jax.ShapeDtypeStruct((m, n), out_dtype),
        scratch_shapes=[pltpu.VMEM((tm, tn), F32)] if nk > 1 else [],
        compiler_params=_params("parallel", "parallel", "arbitrary", vmem_limit=MM_VMEM_LIMIT_BYTES),
    )(*args)


ROW_TILE = 256


def _rmsnorm_fwd(x, g, name, after=()):
    s, d = x.shape

    def body(x_ref, g_ref, o_ref):
        xf = x_ref[...]
        r = lax.rsqrt(jnp.mean(xf * xf, axis=-1, keepdims=True) + EPS)
        o_ref[...] = (xf * r * g_ref[...]).astype(BF16)

    row = pl.BlockSpec((ROW_TILE, d), lambda i: (i, 0))
    vec = pl.BlockSpec((1, d), lambda i: (0, 0))
    return _pcall(body, after=after, name=name, grid=(s // ROW_TILE,), in_specs=[row, vec], out_specs=row,
                  out_shape=jax.ShapeDtypeStruct((s, d), BF16), compiler_params=_params("parallel"))(x, g)


def _rmsnorm_bwd(x, g, dh, dres, name):
    s, d = x.shape

    def body(x_ref, g_ref, dh_ref, dres_ref, dx_ref, dg_ref):
        xf = x_ref[...]
        r = lax.rsqrt(jnp.mean(xf * xf, axis=-1, keepdims=True) + EPS)
        xhat = xf * r
        dh_v = dh_ref[...]
        dxh = dh_v * g_ref[...]
        proj = jnp.mean(dxh * xhat, axis=-1, keepdims=True)
        dx_ref[...] = dres_ref[...] + r * (dxh - xhat * proj)
        part = jnp.sum(dh_v * xhat, axis=0, keepdims=True)

        @pl.when(pl.program_id(0) == 0)
        def _():
            dg_ref[...] = part

        @pl.when(pl.program_id(0) > 0)
        def _():
            dg_ref[...] += part

    row = pl.BlockSpec((ROW_TILE, d), lambda i: (i, 0))
    vec = pl.BlockSpec((1, d), lambda i: (0, 0))
    return _pcall(body, name=name, grid=(s // ROW_TILE,), in_specs=[row, vec, row, row], out_specs=[row, vec],
                  out_shape=[jax.ShapeDtypeStruct((s, d), F32), jax.ShapeDtypeStruct((1, d), F32)],
                  compiler_params=_params("arbitrary"))(x, g, dh, dres)


def _loss_head(x, g, target, name):
    s, d = x.shape

    def body(x_ref, g_ref, t_ref, dx_ref, dg_ref, loss_ref):
        xf = x_ref[...]
        r = lax.rsqrt(jnp.mean(xf * xf, axis=-1, keepdims=True) + EPS)
        xhat = xf * r
        gv = g_ref[...]
        err = xhat * gv - t_ref[...]
        dy = err * (1.0 / d)
        dxh = dy * gv
        proj = jnp.mean(dxh * xhat, axis=-1, keepdims=True)
        dx_ref[...] = r * (dxh - xhat * proj)
        dg_part = jnp.sum(dy * xhat, axis=0, keepdims=True)
        row_loss = jnp.sum(err * err, axis=-1, keepdims=True) * (0.5 / d)
        loss_part = jnp.broadcast_to(jnp.sum(row_loss, axis=0, keepdims=True), (1, 128))

        @pl.when(pl.program_id(0) == 0)
        def _():
            dg_ref[...] = dg_part
            loss_ref[...] = loss_part

        @pl.when(pl.program_id(0) > 0)
        def _():
            dg_ref[...] += dg_part
            loss_ref[...] += loss_part

    row = pl.BlockSpec((ROW_TILE, d), lambda i: (i, 0))
    vec = pl.BlockSpec((1, d), lambda i: (0, 0))
    one = pl.BlockSpec((1, 128), lambda i: (0, 0))
    return _pcall(body, name=name, grid=(s // ROW_TILE,), in_specs=[row, vec, row], out_specs=[row, vec, one],
                  out_shape=[jax.ShapeDtypeStruct((s, d), F32), jax.ShapeDtypeStruct((1, d), F32),
                             jax.ShapeDtypeStruct((1, 128), F32)],
                  compiler_params=_params("arbitrary"))(x, g, target)


HEADS_PER_STEP = 2
GROUP_W = HEADS_PER_STEP * HEAD
N_GROUPS = N_HEADS // HEADS_PER_STEP


def _head_cols(h):
    return slice(h * HEAD, (h + 1) * HEAD)


def _head_specs(s, col0):
    t = ATT_BLOCK
    g0 = [c // HEADS_PER_STEP for c in col0]
    qspec = pl.BlockSpec((t, GROUP_W), lambda g, i: (i, g0[0] + g))
    kspec = pl.BlockSpec((s, GROUP_W), lambda g, i: (0, g0[1] + g))
    vspec = pl.BlockSpec((s, GROUP_W), lambda g, i: (0, g0[2] + g))
    return qspec, kspec, vspec


TRI = 256


def _order_matrix(later):
    r, c = lax.broadcasted_iota(jnp.int32, (TRI, TRI), 0), lax.broadcasted_iota(jnp.int32, (TRI, TRI), 1)
    return (r > c if later else r < c).astype(BF16)


def _exact_dot(x, m, later):
    parts = [x[:, c:c + TRI] for c in range(0, x.shape[1], TRI)]
    totals = [jnp.sum(p, axis=1, keepdims=True) for p in parts] if len(parts) > 1 else None
    out = []
    for j, p in enumerate(parts):
        hi = p.astype(BF16)
        lo = (p - hi.astype(F32)).astype(BF16)
        acc = _dot_nn(hi, m) + _dot_nn(lo, m)
        for other in (range(j + 1, len(parts)) if later else range(j)):
            acc = acc + totals[other]
        out.append(acc)
    return out[0] if len(out) == 1 else jnp.concatenate(out, axis=1)


def _sb_block(q, kblk, carry_l, u, diagonal):
    t = ATT_BLOCK
    z = _dot_nt(q, kblk) * ATT_SCALE
    sp = jnp.maximum(z, 0.0) + jnp.log(1.0 + jnp.exp(-jnp.abs(z)))
    if not diagonal:
        l = -sp
        return z, None, l, jnp.exp(z + l + _exact_dot(l, u, True) + carry_l)
    mask = lax.broadcasted_iota(jnp.int32, (t, t), 1) < lax.broadcasted_iota(jnp.int32, (t, t), 0)
    l = jnp.where(mask, -sp, 0.0)
    a = jnp.where(mask, jnp.exp(z - sp + _exact_dot(l, u, True) + carry_l), 0.0)
    return z, mask, l, a


def _sb_carry_spec(s):
    t = ATT_BLOCK
    return pl.BlockSpec((HEADS_PER_STEP, None, s // t, t, 1), lambda g, i: (g, i, 0, 0, 0))


def _sb_fwd(p, name, after=()):
    s = p.shape[0]
    t = ATT_BLOCK
    nb = s // t

    def body(q_ref, k_ref, v_ref, o_ref, cl_ref):
        i = pl.program_id(1)
        heads = range(HEADS_PER_STEP)
        q = [q_ref[:, _head_cols(h)].astype(BF16) for h in heads]
        u = _order_matrix(True)
        cl_ref[...] = jnp.zeros_like(cl_ref)

        def tile(kb, carry, diagonal):
            ks = pl.multiple_of(kb * t, t)
            out = []
            for h in heads:
                acc, carry_l = carry[h]
                kblk = k_ref[pl.ds(ks, t), _head_cols(h)].astype(BF16)
                vblk = v_ref[pl.ds(ks, t), _head_cols(h)].astype(BF16)
                cl_ref[h, kb] = carry_l
                _, _, l, a = _sb_block(q[h], kblk, carry_l, u, diagonal)
                out.append((acc + _dot_nn(a.astype(BF16), vblk), carry_l + jnp.sum(l, axis=1, keepdims=True)))
            return tuple(out)

        carry = tile(i, tuple((jnp.zeros((t, HEAD), F32), jnp.zeros((t, 1), F32)) for _ in heads), True)
        carry = lax.fori_loop(0, i, lambda n, c: tile(i - 1 - n, c, False), carry)
        for h in heads:
            o_ref[:, _head_cols(h)] = carry[h][0]

    qspec, kspec, vspec = _head_specs(s, (0, N_HEADS, 2 * N_HEADS))
    ospec = pl.BlockSpec((t, GROUP_W), lambda g, i: (i, g))
    return _pcall(body, after=after, name=name, grid=(N_GROUPS, nb), in_specs=[qspec, kspec, vspec],
                  out_specs=[ospec, _sb_carry_spec(s)],
                  out_shape=[jax.ShapeDtypeStruct((s, HALF), F32), jax.ShapeDtypeStruct((N_HEADS, nb, nb, t, 1), F32)],
                  compiler_params=_params("parallel", "parallel", vmem_limit=ATT_VMEM_LIMIT_BYTES))(p, p, p)


def _sb_bwd(p, d_ab, carries, name):
    s = p.shape[0]
    t = ATT_BLOCK
    nb = s // t

    def body(q_ref, k_ref, v_ref, do_ref, cl_ref, out_ref, dk_ref, dv_ref):
        i = pl.program_id(1)

        @pl.when(i == 0)
        def _():
            dk_ref[...] = jnp.zeros_like(dk_ref)
            dv_ref[...] = jnp.zeros_like(dv_ref)

        heads = range(HEADS_PER_STEP)
        q = [q_ref[:, _head_cols(h)].astype(BF16) for h in heads]
        do = [do_ref[:, _head_cols(h)].astype(BF16) for h in heads]
        u = _order_matrix(True)
        lower = _order_matrix(False)

        def tile(kb, carry, diagonal):
            ks = pl.multiple_of(kb * t, t)
            out = []
            for h in heads:
                dq, carry_g = carry[h]
                kblk = k_ref[pl.ds(ks, t), _head_cols(h)].astype(BF16)
                vblk = v_ref[pl.ds(ks, t), _head_cols(h)].astype(BF16)
                z, mask, _, a = _sb_block(q[h], kblk, cl_ref[h, kb], u, diagonal)
                g = a * _dot_nt(do[h], vblk)
                earlier_g = _exact_dot(g, lower, False) + carry_g
                sig = jax.nn.sigmoid(z)
                dz = g * (1.0 - sig) - sig * earlier_g
                if diagonal:
                    dz = jnp.where(mask, dz, 0.0)
                dz = dz.astype(BF16)
                dv_ref[pl.ds(ks, t), _head_cols(h)] += _dot_tn(a.astype(BF16), do[h])
                dk_ref[pl.ds(ks, t), _head_cols(h)] += _dot_tn(dz, q[h]) * ATT_SCALE
                out.append((dq + _dot_nn(dz, kblk) * ATT_SCALE, carry_g + jnp.sum(g, axis=1, keepdims=True)))
            return tuple(out)

        init = tuple((jnp.zeros((t, HEAD), F32), jnp.zeros((t, 1), F32)) for _ in heads)
        carry = tile(i, lax.fori_loop(0, i, lambda kb, c: tile(kb, c, False), init), True)
        for h in heads:
            out_ref[0, pl.ds(pl.multiple_of(i * t, t), t), _head_cols(h)] = carry[h][0].astype(BF16)

        @pl.when(i == nb - 1)
        def _():
            out_ref[1] = dk_ref[...].astype(BF16)
            out_ref[2] = dv_ref[...].astype(BF16)

    qspec, kspec, vspec = _head_specs(s, (0, N_HEADS, 2 * N_HEADS))
    blk = pl.BlockSpec((t, GROUP_W), lambda g, i: (i, g))
    stacked = pl.BlockSpec((3, s, GROUP_W), lambda g, i: (0, 0, g))
    return _pcall(body, name=name, grid=(N_GROUPS, nb), in_specs=[qspec, kspec, vspec, blk, _sb_carry_spec(s)],
                  out_specs=stacked, out_shape=jax.ShapeDtypeStruct((3, s, HALF), BF16),
                  scratch_shapes=[pltpu.VMEM((s, GROUP_W), F32), pltpu.VMEM((s, GROUP_W), F32)],
                  compiler_params=_params("parallel", "arbitrary", vmem_limit=ATT_VMEM_LIMIT_BYTES))(p, p, p, d_ab, carries)


COL_TILE = 256


def _sc_fwd(p, w, name):
    s = p.shape[0]
    nb = HALF // COL_TILE

    def body(gb_ref, gc_ref, h_ref, w_ref, o_ref):
        conv = _conv3(gc_ref[...] * h_ref[...], w_ref[...])
        o_ref[...] = (gb_ref[...] * conv).astype(BF16)

    def col(k):
        return pl.BlockSpec((s, COL_TILE), lambda j: (0, k * nb + j))

    wspec = pl.BlockSpec((3, COL_TILE), lambda j: (0, j))
    return _pcall(body, name=name, grid=(nb,), in_specs=[col(3), col(4), col(5), wspec], out_specs=col(0),
                  out_shape=jax.ShapeDtypeStruct((s, HALF), BF16), compiler_params=_params("parallel"))(p, p, p, w)


def _sc_bwd(p, w, d_ab, name):
    s = p.shape[0]
    nb = HALF // COL_TILE

    def body(gb_ref, gc_ref, h_ref, w_ref, d_ref, out_ref, dw_ref):
        gc, hin, wv, d = gc_ref[...], h_ref[...], w_ref[...], d_ref[...]
        sig = gc * hin
        shifted = (_shift_down(sig, 1), _shift_down(sig, 2))
        out_ref[0] = (d * _conv3(sig, wv, shifted)).astype(BF16)
        dconv = d * gb_ref[...]
        _conv3_wgrad(dconv, sig, shifted, dw_ref)
        dsig = _conv3_transpose(dconv, wv)
        out_ref[1] = (dsig * hin).astype(BF16)
        out_ref[2] = (dsig * gc).astype(BF16)

    def col(k):
        return pl.BlockSpec((s, COL_TILE), lambda j: (0, k * nb + j))

    wspec = pl.BlockSpec((3, COL_TILE), lambda j: (0, j))
    stacked = pl.BlockSpec((3, s, COL_TILE), lambda j: (0, 0, j))
    return _pcall(body, name=name, grid=(nb,), in_specs=[col(3), col(4), col(5), wspec, col(1)],
                  out_specs=[stacked, wspec],
                  out_shape=[jax.ShapeDtypeStruct((3, s, HALF), BF16), jax.ShapeDtypeStruct((3, HALF), F32)],
                  compiler_params=_params("parallel"))(p, p, p, w, d_ab)


def _ffn_act_fwd(u, w, name, after=()):
    s = u.shape[0]
    nb = D_FF // COL_TILE

    def body(ug_ref, uu_ref, wg_ref, wu_ref, o_ref):
        gate = _conv3(ug_ref[...], wg_ref[...])
        up = _conv3(uu_ref[...], wu_ref[...])
        o_ref[...] = (gate * jax.nn.sigmoid(gate) * up).astype(BF16)

    def col(k):
        return pl.BlockSpec((s, COL_TILE), lambda j: (0, k * nb + j))

    def wcol(k):
        return pl.BlockSpec((3, COL_TILE), lambda j: (0, k * nb + j))

    return _pcall(body, after=after, name=name, grid=(nb,), in_specs=[col(0), col(1), wcol(0), wcol(1)], out_specs=col(0),
                  out_shape=jax.ShapeDtypeStruct((s, D_FF), BF16),
                  compiler_params=_params("parallel"))(u, u, w, w)


def _ffn_act_bwd(u, w, d_f, name):
    s = u.shape[0]
    nb = D_FF // COL_TILE

    def body(ug_ref, uu_ref, wg_ref, wu_ref, d_ref, du_ref, dwg_ref, dwu_ref):
        ug, uu, wg, wu, d = ug_ref[...], uu_ref[...], wg_ref[...], wu_ref[...], d_ref[...]
        ug_shifted = (_shift_down(ug, 1), _shift_down(ug, 2))
        uu_shifted = (_shift_down(uu, 1), _shift_down(uu, 2))
        gate = _conv3(ug, wg, ug_shifted)
        up = _conv3(uu, wu, uu_shifted)
        sig = jax.nn.sigmoid(gate)
        d_up = d * gate * sig
        d_gate = d * up * sig * (1.0 + gate * (1.0 - sig))
        _conv3_wgrad(d_gate, ug, ug_shifted, dwg_ref)
        _conv3_wgrad(d_up, uu, uu_shifted, dwu_ref)
        du_ref[0] = _conv3_transpose(d_gate, wg).astype(BF16)
        du_ref[1] = _conv3_transpose(d_up, wu).astype(BF16)

    def col(k):
        return pl.BlockSpec((s, COL_TILE), lambda j: (0, k * nb + j))

    def wcol(k):
        return pl.BlockSpec((3, COL_TILE), lambda j: (0, k * nb + j))

    both = pl.BlockSpec((2, s, COL_TILE), lambda j: (0, 0, j))
    wsh = jax.ShapeDtypeStruct((3, D_FF), F32)
    return _pcall(body, name=name, grid=(nb,), in_specs=[col(0), col(1), wcol(0), wcol(1), col(0)],
                  out_specs=[both, wcol(0), wcol(0)], out_shape=[jax.ShapeDtypeStruct((2, s, D_FF), BF16), wsh, wsh],
                  compiler_params=_params("parallel"))(u, u, w, w, d_f)


def _sg_common(u, v, g, w_ref, bias, mixed_ref):
    rows = u.shape[0]
    gu = _gelu(u)
    gv = _gelu(v)
    xc = gv - jnp.mean(gv, axis=-1, keepdims=True)
    rstd = lax.rsqrt(jnp.mean(xc * xc, axis=-1, keepdims=True) + EPS)
    xhat = xc * rstd
    vn = xhat * g
    tril = lax.broadcasted_iota(jnp.int32, (HEAD, HEAD), 0) >= lax.broadcasted_iota(jnp.int32, (HEAD, HEAD), 1)
    wts = [jnp.where(tril, w_ref[grp], 0.0).astype(BF16) for grp in range(N_HEADS)]
    for n in range(rows // HEAD):
        for grp in range(N_HEADS):
            blk = vn[n * HEAD:(n + 1) * HEAD, grp * HEAD:(grp + 1) * HEAD].astype(BF16)
            mixed_ref[n * HEAD:(n + 1) * HEAD, grp * HEAD:(grp + 1) * HEAD] = _dot_nn(wts[grp], blk)
    mixed = mixed_ref[...] + jnp.concatenate([bias] * (rows // HEAD), axis=0)
    return gu, xhat, rstd, vn, mixed, wts, tril


def _sg_fwd(p, sg_w, bias, g, name):
    s = p.shape[0]

    def body(u_ref, v_ref, w_ref, b_ref, g_ref, o_ref, mixed_ref):
        gu, _, _, _, mixed, _, _ = _sg_common(u_ref[...], v_ref[...], g_ref[...], w_ref, b_ref[...], mixed_ref)
        o_ref[...] = (gu * mixed).astype(BF16)

    def half(k):
        return pl.BlockSpec((ROW_TILE, HALF), lambda i: (i, k))

    wspec = pl.BlockSpec((N_HEADS, HEAD, HEAD), lambda i: (0, 0, 0))
    bspec = pl.BlockSpec((HEAD, HALF), lambda i: (0, 0))
    gspec = pl.BlockSpec((1, HALF), lambda i: (0, 0))
    return _pcall(body, name=name, grid=(s // ROW_TILE,), in_specs=[half(0), half(1), wspec, bspec, gspec],
                  out_specs=half(0), out_shape=jax.ShapeDtypeStruct((s, HALF), BF16),
                  scratch_shapes=[pltpu.VMEM((ROW_TILE, HALF), F32)],
                  compiler_params=_params("parallel"))(p, p, sg_w, bias, g)


def _sg_bwd(p, sg_w, bias, g, d_cd, name):
    s = p.shape[0]
    nsteps = s // ROW_TILE

    def body(u_ref, v_ref, w_ref, b_ref, g_ref, d_ref, duv_ref, dw_ref, db_ref, dg_ref,
             mixed_ref, dvn_ref, dbias_ref):
        i = pl.program_id(0)
        u, v, gain, d = u_ref[...], v_ref[...], g_ref[...], d_ref[...]
        gu, xhat, rstd, vn, mixed, wts, tril = _sg_common(u, v, gain, w_ref, b_ref[...], mixed_ref)

        @pl.when(i == 0)
        def _():
            dw_ref[...] = jnp.zeros_like(dw_ref)
            dg_ref[...] = jnp.zeros_like(dg_ref)
            dbias_ref[...] = jnp.zeros_like(dbias_ref)

        duv_ref[0] = (d * mixed * _gelu_grad(u)).astype(BF16)
        dm = d * gu
        for n in range(ROW_TILE // HEAD):
            rs = slice(n * HEAD, (n + 1) * HEAD)
            dbias_ref[...] += dm[rs, :]
            for grp in range(N_HEADS):
                cs = slice(grp * HEAD, (grp + 1) * HEAD)
                dm_blk = dm[rs, cs].astype(BF16)
                dw_ref[grp] += jnp.where(tril, _dot_nt(dm_blk, vn[rs, cs].astype(BF16)), 0.0)
                dvn_ref[rs, cs] = _dot_tn(wts[grp], dm_blk)
        dvn = dvn_ref[...]
        dg_ref[...] += jnp.sum(dvn * xhat, axis=0, keepdims=True)
        dxh = dvn * gain
        d_gv = rstd * (dxh - jnp.mean(dxh, axis=-1, keepdims=True) - xhat * jnp.mean(dxh * xhat, axis=-1, keepdims=True))
        duv_ref[1] = (d_gv * _gelu_grad(v)).astype(BF16)

        @pl.when(i == nsteps - 1)
        def _():
            lane = lax.broadcasted_iota(jnp.int32, (HEAD, HEAD), 1)
            out = jnp.zeros((HEAD, HEAD), F32)
            for grp in range(N_HEADS):
                tot = jnp.sum(dbias_ref[:, grp * HEAD:(grp + 1) * HEAD], axis=1, keepdims=True)
                out = out + jnp.where(lane == grp, tot, 0.0)
            db_ref[...] = out

    def half(k):
        return pl.BlockSpec((ROW_TILE, HALF), lambda i: (i, k))

    wspec = pl.BlockSpec((N_HEADS, HEAD, HEAD), lambda i: (0, 0, 0))
    bspec = pl.BlockSpec((HEAD, HALF), lambda i: (0, 0))
    gspec = pl.BlockSpec((1, HALF), lambda i: (0, 0))
    dbspec = pl.BlockSpec((HEAD, HEAD), lambda i: (0, 0))
    stacked = pl.BlockSpec((2, ROW_TILE, HALF), lambda i: (0, i, 0))
    return _pcall(body, name=name, grid=(nsteps,), in_specs=[half(0), half(1), wspec, bspec, gspec, half(0)],
                  out_specs=[stacked, wspec, dbspec, gspec],
                  out_shape=[jax.ShapeDtypeStruct((2, s, HALF), BF16), jax.ShapeDtypeStruct((N_HEADS, HEAD, HEAD), F32),
                             jax.ShapeDtypeStruct((HEAD, HEAD), F32), jax.ShapeDtypeStruct((1, HALF), F32)],
                  scratch_shapes=[pltpu.VMEM((ROW_TILE, HALF), F32), pltpu.VMEM((ROW_TILE, HALF), F32),
                                  pltpu.VMEM((HEAD, HALF), F32)],
                  compiler_params=_params("arbitrary"))(p, p, sg_w, bias, g, d_cd)


def _fox_prep(f, b, name):
    s = f.shape[0]
    t = ATT_BLOCK

    def body(f_ref, b_ref, c_ref):
        tri = (lax.broadcasted_iota(jnp.int32, (t, t), 0) >= lax.broadcasted_iota(jnp.int32, (t, t), 1)).astype(BF16)
        carry = jnp.zeros((1, 128), F32)
        for n in range(s // t):
            lf = _log_sigmoid(f_ref[n * t:(n + 1) * t, :] + b_ref[...])
            hi, mid, lo = _split3(lf)
            c_ref[n * t:(n + 1) * t, :] = _dot_nn(tri, hi) + _dot_nn(tri, mid) + _dot_nn(tri, lo) + carry
            carry = carry + jnp.sum(lf, axis=0, keepdims=True)

    return _pcall(body, name=name, in_specs=[VMEM_SPEC, VMEM_SPEC], out_specs=VMEM_SPEC,
                  out_shape=jax.ShapeDtypeStruct((s, 128), F32))(f, b)


def _fox_post(drow, dcol, f, b, name):
    s = f.shape[0]
    t = ATT_BLOCK

    def body(drow_ref, dcol_ref, f_ref, b_ref, df_ref, db_ref):
        tri = (lax.broadcasted_iota(jnp.int32, (t, t), 1) >= lax.broadcasted_iota(jnp.int32, (t, t), 0)).astype(BF16)
        lane = lax.broadcasted_iota(jnp.int32, (t, 128), 1)
        carry = jnp.zeros((1, 128), F32)
        db = jnp.zeros((1, 128), F32)
        for n in reversed(range(s // t)):
            rs = slice(n * t, (n + 1) * t)
            dc = jnp.zeros((t, 128), F32)
            for h in range(N_HEADS):
                dc = jnp.where(lane == h, drow_ref[h, rs, :] - dcol_ref[rs, _head_cols(h)], dc)
            hi, mid, lo = _split3(dc)
            dlogf = _dot_nn(tri, hi) + _dot_nn(tri, mid) + _dot_nn(tri, lo) + carry
            carry = carry + jnp.sum(dc, axis=0, keepdims=True)
            df = dlogf * jax.nn.sigmoid(-(f_ref[rs, :] + b_ref[...]))
            df_ref[rs, :] = df
            db = db + jnp.sum(df, axis=0, keepdims=True)
        db_ref[...] = db

    return _pcall(body, name=name, in_specs=[VMEM_SPEC] * 4, out_specs=[VMEM_SPEC, VMEM_SPEC],
                  out_shape=[jax.ShapeDtypeStruct((s, 128), F32), jax.ShapeDtypeStruct((1, 128), F32)],
                  compiler_params=pltpu.CompilerParams(vmem_limit_bytes=VMEM_LIMIT_BYTES))(drow, dcol, f, b)


def _fox_specs(s):
    t = ATT_BLOCK
    ccol = pl.BlockSpec((HEADS_PER_STEP, t, 1), lambda g, i: (g, i, 0))
    crow = pl.BlockSpec((HEADS_PER_STEP, s // t, 1, t), lambda g, i: (g, 0, 0, 0))
    return ccol, crow


def _fox_fwd(p, c_col, c_row, name, after=()):
    s = p.shape[0]
    t = ATT_BLOCK

    def body(q_ref, k_ref, v_ref, cc_ref, cr_ref, o_ref, lse_ref):
        i = pl.program_id(1)
        heads = range(HEADS_PER_STEP)
        q = [q_ref[:, _head_cols(h)].astype(BF16) for h in heads]
        ct = [cc_ref[h] for h in heads]

        def tile(n, carry, diagonal):
            ks = pl.multiple_of(n * t, t)
            out = []
            for h in heads:
                acc, m, l = carry[h]
                kblk = k_ref[pl.ds(ks, t), _head_cols(h)].astype(BF16)
                vblk = v_ref[pl.ds(ks, t), _head_cols(h)].astype(BF16)
                logit = _dot_nt(q[h], kblk) * ATT_SCALE + ct[h] - cr_ref[h, n]
                if diagonal:
                    causal = lax.broadcasted_iota(jnp.int32, (t, t), 1) <= lax.broadcasted_iota(jnp.int32, (t, t), 0)
                    logit = jnp.where(causal, logit, NEG)
                m_new = jnp.maximum(m, jnp.max(logit, axis=1, keepdims=True))
                alpha = jnp.exp(m - m_new)
                pr = jnp.exp(logit - m_new)
                l = alpha * l + jnp.sum(pr, axis=1, keepdims=True)
                out.append((alpha * acc + _dot_nn(pr.astype(BF16), vblk), m_new, l))
            return tuple(out)

        init = tuple((jnp.zeros((t, HEAD), F32), jnp.full((t, 1), NEG, F32), jnp.zeros((t, 1), F32)) for _ in heads)
        carry = tile(i, lax.fori_loop(0, i, lambda n, c: tile(n, c, False), init), True)
        for h in heads:
            acc, m, l = carry[h]
            o_ref[:, _head_cols(h)] = acc / l
            lse_ref[h] = m + jnp.log(l)

    qspec, kspec, vspec = _head_specs(s, (2 * N_HEADS, 3 * N_HEADS, 4 * N_HEADS))
    ccol, crow = _fox_specs(s)
    ospec = pl.BlockSpec((t, GROUP_W), lambda g, i: (i, g))
    return _pcall(body, after=after, name=name, grid=(N_GROUPS, s // t), in_specs=[qspec, kspec, vspec, ccol, crow],
                  out_specs=[ospec, ccol],
                  out_shape=[jax.ShapeDtypeStruct((s, HALF), F32), jax.ShapeDtypeStruct((N_HEADS, s, 1), F32)],
                  compiler_params=_params("parallel", "parallel", vmem_limit=ATT_VMEM_LIMIT_BYTES))(p, p, p, c_col, c_row)


def _fox_bwd(p, c_col, c_row, lse, d_cd, d_out, name):
    s = p.shape[0]
    t = ATT_BLOCK
    nb = s // t

    def body(q_ref, k_ref, v_ref, cc_ref, cr_ref, lse_ref, do_ref, o_ref, out_ref, dcol_ref, drow_ref, dk_ref, dv_ref):
        i = pl.program_id(1)

        @pl.when(i == 0)
        def _():
            dk_ref[...] = jnp.zeros_like(dk_ref)
            dv_ref[...] = jnp.zeros_like(dv_ref)
            dcol_ref[...] = jnp.zeros_like(dcol_ref)

        heads = range(HEADS_PER_STEP)
        q = [q_ref[:, _head_cols(h)].astype(BF16) for h in heads]
        do = [do_ref[:, _head_cols(h)].astype(BF16) for h in heads]
        delta = [jnp.sum(do_ref[:, _head_cols(h)] * o_ref[:, _head_cols(h)], axis=1, keepdims=True) for h in heads]
        ct = [cc_ref[h] for h in heads]
        lse_v = [lse_ref[h] for h in heads]
        ones = jnp.ones((t, HEAD), BF16)

        def tile(n, carry, diagonal):
            ks = pl.multiple_of(n * t, t)
            out = []
            for h in heads:
                dq, drow = carry[h]
                kblk = k_ref[pl.ds(ks, t), _head_cols(h)].astype(BF16)
                vblk = v_ref[pl.ds(ks, t), _head_cols(h)].astype(BF16)
                logit = _dot_nt(q[h], kblk) * ATT_SCALE + ct[h] - cr_ref[h, n]
                pr = jnp.exp(logit - lse_v[h])
                if diagonal:
                    causal = lax.broadcasted_iota(jnp.int32, (t, t), 1) <= lax.broadcasted_iota(jnp.int32, (t, t), 0)
                    pr = jnp.where(causal, pr, 0.0)
                ds = pr * (_dot_nt(do[h], vblk) - delta[h])
                dsb = ds.astype(BF16)
                dv_ref[pl.ds(ks, t), _head_cols(h)] += _dot_tn(pr.astype(BF16), do[h])
                dk_ref[pl.ds(ks, t), _head_cols(h)] += _dot_tn(dsb, q[h]) * ATT_SCALE
                dcol_ref[pl.ds(ks, t), _head_cols(h)] += _dot_tn(dsb, ones)
                out.append((dq + _dot_nn(dsb, kblk) * ATT_SCALE,
                            drow + jnp.sum(dsb.astype(F32), axis=1, keepdims=True)))
            return tuple(out)

        init = tuple((jnp.zeros((t, HEAD), F32), jnp.zeros((t, 1), F32)) for _ in heads)
        carry = tile(i, lax.fori_loop(0, i, lambda n, c: tile(n, c, False), init), True)
        for h in heads:
            out_ref[0, pl.ds(pl.multiple_of(i * t, t), t), _head_cols(h)] = carry[h][0].astype(BF16)
            drow_ref[h] = carry[h][1]

        @pl.when(i == nb - 1)
        def _():
            out_ref[1] = dk_ref[...].astype(BF16)
            out_ref[2] = dv_ref[...].astype(BF16)

    qspec, kspec, vspec = _head_specs(s, (2 * N_HEADS, 3 * N_HEADS, 4 * N_HEADS))
    ccol, crow = _fox_specs(s)
    dospec = pl.BlockSpec((t, GROUP_W), lambda g, i: (i, N_GROUPS + g))
    blk = pl.BlockSpec((t, GROUP_W), lambda g, i: (i, g))
    whole = pl.BlockSpec((s, GROUP_W), lambda g, i: (0, g))
    stacked = pl.BlockSpec((3, s, GROUP_W), lambda g, i: (0, 0, g))
    return _pcall(body, name=name, grid=(N_GROUPS, nb),
                  in_specs=[qspec, kspec, vspec, ccol, crow, ccol, dospec, blk],
                  out_specs=[stacked, whole, ccol],
                  out_shape=[jax.ShapeDtypeStruct((3, s, HALF), BF16), jax.ShapeDtypeStruct((s, HALF), F32),
                             jax.ShapeDtypeStruct((N_HEADS, s, 1), F32)],
                  scratch_shapes=[pltpu.VMEM((s, GROUP_W), F32), pltpu.VMEM((s, GROUP_W), F32)],
                  compiler_params=_params("parallel", "arbitrary", vmem_limit=ATT_VMEM_LIMIT_BYTES))(p, p, p, c_col, c_row, lse, d_cd, d_out)


def _row_tile(rows, cap):
    for t in (256, 128, 64, 32, 16, 8):
        if t <= cap and rows % t == 0:
            return t
    return rows


def _adamw_small(ws, gs, ms, vs, name):
    n = len(ws)
    c1 = 1.0 / (1.0 - ADAM_B1 ** ADAM_STEP)
    c2 = 1.0 / (1.0 - ADAM_B2 ** ADAM_STEP)

    def body(*refs):
        for k in range(n):
            w_ref, g_ref, m_ref, v_ref = (refs[j * n + k] for j in range(4))
            d_ref, nm_ref, nv_ref = (refs[(4 + j) * n + k] for j in range(3))
            gv = g_ref[...]
            nm = ADAM_B1 * m_ref[...] + (1.0 - ADAM_B1) * gv
            nv = ADAM_B2 * v_ref[...] + (1.0 - ADAM_B2) * (gv * gv)
            nm_ref[...] = nm
            nv_ref[...] = nv
            d_ref[...] = -ADAM_LR * ((nm * c1) / (jnp.sqrt(nv * c2) + ADAM_EPS) + ADAM_WD * w_ref[...])

    shapes = [jax.ShapeDtypeStruct(w.shape, F32) for w in ws] * 3
    outs = _pcall(body, name=name, in_specs=[VMEM_SPEC] * (4 * n), out_specs=[VMEM_SPEC] * (3 * n), out_shape=shapes,
                  compiler_params=pltpu.CompilerParams(vmem_limit_bytes=VMEM_LIMIT_BYTES))(*ws, *gs, *ms, *vs)
    return outs[:n], outs[n:2 * n], outs[2 * n:]


def _half_shape(whole_shape, kind):
    if kind == "col":
        return (whole_shape[0] // 2, whole_shape[1] // 4)
    if kind == "row":
        return (whole_shape[0] // 8, whole_shape[1])
    return (whole_shape[1] // 2, whole_shape[2])


def _own_half_spec(whole_shape, kind, tr):
    hr, hc = _half_shape(whole_shape, kind)
    nb = hr // tr
    if kind == "col":
        return pl.BlockSpec((tr, hc), lambda i, pos: (pos[1] * nb + i, pos[0]))
    if kind == "row":
        return pl.BlockSpec((tr, hc), lambda i, pos: ((2 * pos[0] + pos[1]) * nb + i, 0))
    return pl.BlockSpec((None, tr, hc), lambda i, pos: (pos[0], pos[1] * nb + i, 0))


def _sum_partials(pos, grad, landed, kind, name):
    hr, hc = _half_shape(grad.shape, kind)
    tr = _row_tile(hr, 64)

    def body(pos_ref, g_ref, p_ref, o_ref):
        acc = g_ref[...].astype(F32)
        for k in range(N_DEV - 1):
            acc = acc + p_ref[k].astype(F32)
        o_ref[...] = acc

    grid_spec = pltpu.PrefetchScalarGridSpec(
        num_scalar_prefetch=1, grid=(hr // tr,),
        in_specs=[_own_half_spec(grad.shape, kind, tr), pl.BlockSpec((N_DEV - 1, tr, hc), lambda i, pos: (0, i, 0))],
        out_specs=pl.BlockSpec((tr, hc), lambda i, pos: (i, 0)))
    return _pcall(body, name=name, grid_spec=grid_spec, out_shape=jax.ShapeDtypeStruct((hr, hc), F32),
                  compiler_params=_params("parallel"))(pos, grad, landed)


def _adamw_shard(pos, w, g_mine, g_sibling, m, v, name):
    hr, hc = g_mine.shape
    tr = _row_tile(hr, 128)
    nb = hr // tr
    c1 = 1.0 / (1.0 - ADAM_B1 ** ADAM_STEP)
    c2 = 1.0 / (1.0 - ADAM_B2 ** ADAM_STEP)

    def body(pos_ref, w_ref, gm_ref, gs_ref, m_ref, v_ref, g_ref, d_ref, nm_ref, nv_ref):
        mine = (pl.program_id(0) // nb) == pos_ref[1]
        gv = jnp.where(mine, gm_ref[...], gs_ref[...])
        nm = ADAM_B1 * m_ref[...] + (1.0 - ADAM_B1) * gv
        nv = ADAM_B2 * v_ref[...] + (1.0 - ADAM_B2) * (gv * gv)
        g_ref[...] = gv
        nm_ref[...] = nm
        nv_ref[...] = nv
        d_ref[...] = -ADAM_LR * ((nm * c1) / (jnp.sqrt(nv * c2) + ADAM_EPS) + ADAM_WD * w_ref[...])

    full = pl.BlockSpec((tr, hc), lambda i, pos: (i, 0))
    mine_spec = pl.BlockSpec((tr, hc), lambda i, pos: (jnp.clip(i - pos[1] * nb, 0, nb - 1), 0))
    sib_spec = pl.BlockSpec((tr, hc), lambda i, pos: (jnp.clip(i - (1 - pos[1]) * nb, 0, nb - 1), 0))
    grid_spec = pltpu.PrefetchScalarGridSpec(
        num_scalar_prefetch=1, grid=(2 * nb,), in_specs=[full, mine_spec, sib_spec, full, full], out_specs=[full] * 4)
    shape = jax.ShapeDtypeStruct((2 * hr, hc), F32)
    return _pcall(body, name=name, grid_spec=grid_spec, out_shape=[shape] * 4,
                  compiler_params=_params("parallel"))(pos, w, g_mine, g_sibling, m, v)


def _place_shard(pos, shard, kind, name, after=()):
    rows, cols = shard.shape
    tr = _row_tile(rows, 256)
    nb = rows // tr
    if kind == "col":
        out_spec = pl.BlockSpec((tr, cols), lambda i, pos: (i, pos[0]))
    elif kind == "row":
        out_spec = pl.BlockSpec((tr, cols), lambda i, pos: (pos[0] * nb + i, 0))
    else:
        out_spec = pl.BlockSpec((None, tr, cols), lambda i, pos: (pos[0], i, 0))

    def body(pos_ref, s_ref, *rest):
        rest[-1][...] = s_ref[...].astype(BF16)

    grid_spec = pltpu.PrefetchScalarGridSpec(
        num_scalar_prefetch=1, grid=(nb,),
        in_specs=[pl.BlockSpec((tr, cols), lambda i, pos: (i, 0))] + [pl.BlockSpec(memory_space=pl.ANY)] * len(after),
        out_specs=out_spec)
    return _pcall(body, name=name, grid_spec=grid_spec,
                  out_shape=jax.ShapeDtypeStruct(_whole_shape(shard.shape, kind), BF16),
                  compiler_params=_params("parallel"))(pos, shard, *after)


N_DEV = 8
RELATIONS = [(r >> 2 & 1, r >> 1 & 1, r & 1) for r in range(1, N_DEV)]


def _position():
    return lax.axis_index("x"), lax.axis_index("y"), lax.axis_index("c")


def _related(pos, rel):
    return tuple(1 - p if f else p for p, f in zip(pos, rel))


def _index(pos):
    return 4 * pos[0] + 2 * pos[1] + pos[2]


def _window(ref, kind, pos):
    px, py, pc = pos
    j = 2 * px + py
    if kind == "col":
        r, c = ref.shape
        return ref.at[pl.ds(pc * (r // 2), r // 2), pl.ds(pl.multiple_of(j * (c // 4), 128), c // 4)]
    if kind == "row":
        rj = ref.shape[0] // 4
        return ref.at[pl.ds(j * rj + pc * (rj // 2), rj // 2), :]
    r = ref.shape[1]
    return ref.at[j, pl.ds(pc * (r // 2), r // 2), :]


def _whole_shape(shard_shape, kind):
    r, c = shard_shape
    return {"col": (r, 4 * c), "row": (4 * r, c), "maj": (4, r, c)}[kind]


SEM_SPEC = pl.BlockSpec(memory_space=pltpu.SEMAPHORE)
ANY_SPEC = pl.BlockSpec(memory_space=pl.ANY)
DATAFLOW = pltpu.SideEffectType.DATAFLOW_SIDE_EFFECTING
TOKEN = jax.ShapeDtypeStruct((8, 128), F32)


def _hbm(a):
    return pltpu.with_memory_space_constraint(a, pltpu.HBM)


def _chips(x, y):
    return [(1 - x, y), (x, 1 - y), (1 - x, 1 - y)]


def _split_start(body, name, buffers, n_sems, after=()):
    n = len(buffers)

    def wrapped(*refs):
        body(refs[:n], refs[n], refs[n + 1])
        refs[-1][...] = jnp.zeros_like(refs[-1])

    outs = _pcall(
        wrapped, after=after, name=name, in_specs=[HBM_SPEC] * n,
        out_specs=[SEM_SPEC, SEM_SPEC] + [HBM_SPEC] * n + [VMEM_SPEC],
        out_shape=[pltpu.SemaphoreType.DMA(n_sems), pltpu.SemaphoreType.DMA(n_sems)]
        + [pltpu.HBM(b.shape, b.dtype) for b in buffers] + [TOKEN],
        input_output_aliases={i: 2 + i for i in range(n)},
        compiler_params=pltpu.CompilerParams(has_side_effects=DATAFLOW))(*[_hbm(b) for b in buffers])
    return outs[0], outs[1], list(outs[2:2 + n]), outs[2 + n]


def _split_wait(body, name, buffers, send_sems, recv_sems, after):
    n = len(buffers)
    after = list(after) if isinstance(after, (list, tuple)) else [after]

    def wrapped(*refs):
        body(refs[:n], refs[n], refs[n + 1])

    outs = _pcall(
        wrapped, name=name, in_specs=[HBM_SPEC] * n + [SEM_SPEC, SEM_SPEC] + [ANY_SPEC] * len(after),
        out_specs=[HBM_SPEC] * n, out_shape=[pltpu.HBM(b.shape, b.dtype) for b in buffers],
        input_output_aliases={i: i for i in range(n)},
        compiler_params=pltpu.CompilerParams(has_side_effects=DATAFLOW))(*buffers, send_sems, recv_sems, *after)
    return list(outs)


def _gather_start(wholes, kinds, name, after=()):
    def body(w_refs, send_sems, recv_sems):
        x, y, c = _position()
        for w, ref in enumerate(w_refs):
            mine = _window(ref, kinds[w], (x, y, c))
            for k, chip in enumerate(_chips(x, y)):
                pltpu.make_async_remote_copy(src_ref=mine, dst_ref=mine, send_sem=send_sems.at[3 * w + k],
                                             recv_sem=recv_sems.at[3 * w + k], device_id=(*chip, c),
                                             device_id_type=MESH).start()

    return _split_start(body, name, wholes, (3 * len(wholes),), after)


def _gather_forward(wholes, kinds, send1, recv1, after, name):
    n = len(wholes)

    def wrapped(*refs):
        w_refs, s1, r1, s2, r2 = refs[:n], refs[n], refs[n + 1], refs[n + 3], refs[n + 4]
        x, y, c = _position()
        for k, chip in enumerate(_chips(x, y)):
            for w, ref in enumerate(w_refs):
                theirs = _window(ref, kinds[w], (*chip, c))
                pltpu.make_async_remote_copy(src_ref=theirs, dst_ref=theirs, send_sem=s1.at[3 * w + k],
                                             recv_sem=r1.at[3 * w + k], device_id=(*chip, c),
                                             device_id_type=MESH).wait_recv()
                pltpu.make_async_remote_copy(src_ref=theirs, dst_ref=theirs, send_sem=s2.at[3 * w + k],
                                             recv_sem=r2.at[3 * w + k], device_id=(x, y, 1 - c),
                                             device_id_type=MESH).start()
        for w, ref in enumerate(w_refs):
            mine = _window(ref, kinds[w], (x, y, c))
            for k, chip in enumerate(_chips(x, y)):
                pltpu.make_async_remote_copy(src_ref=mine, dst_ref=mine, send_sem=s1.at[3 * w + k],
                                             recv_sem=r1.at[3 * w + k], device_id=(*chip, c),
                                             device_id_type=MESH).wait_send()
        refs[-1][...] = jnp.zeros_like(refs[-1])

    outs = _pcall(
        wrapped, name=name, in_specs=[HBM_SPEC] * n + [SEM_SPEC, SEM_SPEC, ANY_SPEC],
        out_specs=[SEM_SPEC, SEM_SPEC] + [HBM_SPEC] * n + [VMEM_SPEC],
        out_shape=[pltpu.SemaphoreType.DMA((3 * n,)), pltpu.SemaphoreType.DMA((3 * n,))]
        + [pltpu.HBM(b.shape, b.dtype) for b in wholes] + [TOKEN],
        input_output_aliases={i: 2 + i for i in range(n)},
        compiler_params=pltpu.CompilerParams(has_side_effects=DATAFLOW))(*wholes, send1, recv1, after)
    return outs[0], outs[1], list(outs[2:2 + n]), outs[2 + n]


def _gather_finish(wholes, kinds, send2, recv2, after, name):
    def body(w_refs, s2, r2):
        x, y, c = _position()
        for k, chip in enumerate(_chips(x, y)):
            for w, ref in enumerate(w_refs):
                sent = _window(ref, kinds[w], (*chip, c))
                got = _window(ref, kinds[w], (*chip, 1 - c))
                pltpu.make_async_remote_copy(src_ref=sent, dst_ref=got, send_sem=s2.at[3 * w + k],
                                             recv_sem=r2.at[3 * w + k], device_id=(x, y, 1 - c),
                                             device_id_type=MESH).wait()

    return _split_wait(body, name, wholes, send2, recv2, after)


def _gather_small(small, after=()):
    def body(s_ref, o_ref, send_sems, recv_sems, local_sem):
        x, y, c = _position()
        mine = pltpu.make_async_copy(s_ref, o_ref.at[2 * x + y], local_sem)
        mine.start()
        sends = []
        for k, chip in enumerate(_chips(x, y)):
            cp = pltpu.make_async_remote_copy(src_ref=s_ref, dst_ref=o_ref.at[2 * x + y], send_sem=send_sems.at[k],
                                              recv_sem=recv_sems.at[k], device_id=(*chip, c), device_id_type=MESH)
            cp.start()
            sends.append(cp)
        for k, chip in enumerate(_chips(x, y)):
            pltpu.make_async_remote_copy(src_ref=s_ref, dst_ref=o_ref.at[2 * chip[0] + chip[1]], send_sem=send_sems.at[k],
                                         recv_sem=recv_sems.at[k], device_id=(*chip, c), device_id_type=MESH).wait_recv()
        for cp in sends:
            cp.wait_send()
        mine.wait()

    return _pcall(body, after=after, name="gather_small", in_specs=[HBM_SPEC], out_specs=HBM_SPEC,
                  out_shape=jax.ShapeDtypeStruct((4,) + small.shape, small.dtype),
                  scratch_shapes=[pltpu.SemaphoreType.DMA((3,)), pltpu.SemaphoreType.DMA((3,)),
                                  pltpu.SemaphoreType.DMA(())])(small)


def _scatter_copies(g_refs, land_refs, kinds, send_sems, recv_sems):
    me = _position()
    copies = []
    for k, rel in enumerate(RELATIONS):
        peer = _related(me, rel)
        for w, (g_ref, land_ref) in enumerate(zip(g_refs, land_refs)):
            copies.append(pltpu.make_async_remote_copy(
                src_ref=_window(g_ref, kinds[w], peer), dst_ref=land_ref.at[k],
                send_sem=send_sems.at[7 * w + k], recv_sem=recv_sems.at[7 * w + k], device_id=peer,
                device_id_type=MESH))
    return copies


def _scatter_start(grads, kinds, name):
    n = len(grads)
    lands = [lax.empty((N_DEV - 1,) + _half_shape(g.shape, kd), g.dtype) for g, kd in zip(grads, kinds)]

    def body(refs, send_sems, recv_sems):
        for cp in _scatter_copies(refs[:n], refs[n:], kinds, send_sems, recv_sems):
            cp.start()

    send, recv, thru, token = _split_start(body, name, list(grads) + lands, ((N_DEV - 1) * n,))
    return send, recv, thru[:n], thru[n:], token


def _scatter_wait(grads, lands, kinds, send, recv, after, name):
    n = len(grads)

    def body(refs, send_sems, recv_sems):
        for cp in _scatter_copies(refs[:n], refs[n:], kinds, send_sems, recv_sems):
            cp.wait()

    out = _split_wait(body, name, list(grads) + list(lands), send, recv, after)
    return out[:n], out[n:]


def _swap_start(halves, name):
    n = len(halves)
    lands = [lax.empty(h.shape, h.dtype) for h in halves]

    def body(refs, send_sems, recv_sems):
        x, y, c = _position()
        for w in range(n):
            pltpu.make_async_remote_copy(src_ref=refs[w], dst_ref=refs[n + w], send_sem=send_sems.at[w],
                                         recv_sem=recv_sems.at[w], device_id=(x, y, 1 - c), device_id_type=MESH).start()

    send, recv, thru, token = _split_start(body, name, list(halves) + lands, (n,))
    return send, recv, thru[:n], thru[n:], token


def _swap_wait(halves, lands, send, recv, after, name):
    n = len(halves)

    def body(refs, send_sems, recv_sems):
        x, y, c = _position()
        for w in range(n):
            pltpu.make_async_remote_copy(src_ref=refs[w], dst_ref=refs[n + w], send_sem=send_sems.at[w],
                                         recv_sem=recv_sems.at[w], device_id=(x, y, 1 - c), device_id_type=MESH).wait()

    out = _split_wait(body, name, list(halves) + list(lands), send, recv, after)
    return out[:n], out[n:]


def _allreduce_small(v, after=()):
    rows = v.shape[0]

    def body(v_ref, o_ref, recv_ref, send_sems, recv_sems):
        me = _position()
        recv_ref[_index(me)] = v_ref[...]
        sends = []
        for k, rel in enumerate(RELATIONS):
            peer = _related(me, rel)
            cp = pltpu.make_async_remote_copy(
                src_ref=v_ref, dst_ref=recv_ref.at[_index(me)],
                send_sem=send_sems.at[k], recv_sem=recv_sems.at[k], device_id=peer, device_id_type=MESH)
            cp.start()
            sends.append(cp)
        for k, rel in enumerate(RELATIONS):
            peer = _related(me, rel)
            pltpu.make_async_remote_copy(
                src_ref=v_ref, dst_ref=recv_ref.at[_index(peer)],
                send_sem=send_sems.at[k], recv_sem=recv_sems.at[k], device_id=peer, device_id_type=MESH).wait_recv()
        for cp in sends:
            cp.wait_send()
        acc = recv_ref[0]
        for k in range(1, N_DEV):
            acc = acc + recv_ref[k]
        o_ref[...] = acc

    return _pcall(body, after=after, name="allreduce_small", in_specs=[VMEM_SPEC], out_specs=VMEM_SPEC,
                  out_shape=jax.ShapeDtypeStruct((rows, 128), F32),
                  scratch_shapes=[pltpu.VMEM((N_DEV, rows, 128), F32), pltpu.SemaphoreType.DMA((7,)),
                                  pltpu.SemaphoreType.DMA((7,))],
                  compiler_params=pltpu.CompilerParams(vmem_limit_bytes=VMEM_LIMIT_BYTES))(v)


def _pack(arrays):
    flat = []
    for a in arrays:
        a = a.reshape(-1)
        flat.append(jnp.pad(a, (0, -a.shape[0] % 128)))
    flat = jnp.concatenate(flat)
    flat = jnp.pad(flat, (0, -flat.shape[0] % 1024))
    return flat.reshape(-1, 128)


def _unpack(packed, shapes):
    flat = packed.reshape(-1)
    out, at = [], 0
    for shp in shapes:
        size = 1
        for d in shp:
            size *= d
        out.append(flat[at:at + size].reshape(shp))
        at += size + (-size % 128)
    return out


WEIGHTS = ['l0_mix_norm_g', 'l0_w_in', 'l0_sc_conv_w', 'l0_w_out', 'l0_ffn_norm_g', 'l0_ffn_up', 'l0_ffn_conv_w',
           'l0_ffn_down', 'l1_mix_norm_g', 'l1_w_in', 'l1_fox_b_f', 'l1_sg_w', 'l1_sg_b', 'l1_sg_norm_g', 'l1_w_out',
           'l1_ffn_norm_g', 'l1_ffn_up', 'l1_ffn_conv_w', 'l1_ffn_down', 'final_norm_g']
BIG = {'l0_w_in': 'col', 'l0_w_out': 'row', 'l0_ffn_up': 'col', 'l0_ffn_down': 'row',
       'l1_w_in': 'maj', 'l1_w_out': 'row', 'l1_ffn_up': 'col', 'l1_ffn_down': 'row'}
GATHER_GROUPS = [['l0_w_in'], ['l0_w_out'], ['l0_ffn_up'], ['l0_ffn_down'], ['l1_w_in', 'l1_w_out'],
                 ['l1_ffn_up'], ['l1_ffn_down']]
CONV = ['l0_sc_conv_w', 'l0_ffn_conv_w', 'l1_ffn_conv_w']
SMALL = [n for n in WEIGHTS if n not in BIG]
IN_CD = 5 * HALF + N_HEADS


def _ffn_forward(x, g, get_up, behind_act, get_down, conv_w, tag):
    h = _rmsnorm_fwd(x, g, tag + "_norm")
    u = _matmul(h, get_up(h), "nn", F32, tag + "_up")
    f = _ffn_act_fwd(u, conv_w, tag + "_act", after=behind_act(u))
    w_down, tokens = get_down(f)
    return _matmul(f, w_down, "nn", F32, tag + "_down", res=x, after=tokens), (h, u, f)


def _ffn_backward(x, g, w_up, conv_w, w_down, saved, d_out, send_up, send_down, tag):
    h, u, f = saved
    dw_down = _matmul(f, d_out, "tn", BF16, tag + "_dwdown")
    d_f = _matmul(d_out, w_down, "nt", F32, tag + "_df", after=[send_down(dw_down)])
    du, dcw_gate, dcw_up = _ffn_act_bwd(u, conv_w, d_f, tag + "_dact")
    dw_up = _matmul(h, du, "tn", BF16, tag + "_dwup")
    dh = _matmul(du, w_up, "nt", F32, tag + "_dh", after=[send_up(dw_up)])
    dx, dg = _rmsnorm_bwd(x, g, dh, d_out, tag + "_dnorm")
    return dx, dg, jnp.concatenate([dcw_gate, dcw_up], axis=1)


def kernel(x, l0_mix_norm_g, l0_w_in, l0_sc_conv_w, l0_w_out, l0_ffn_norm_g, l0_ffn_up, l0_ffn_conv_w, l0_ffn_down, l1_mix_norm_g, l1_w_in, l1_fox_b_f, l1_sg_w, l1_sg_b, l1_sg_norm_g, l1_w_out, l1_ffn_norm_g, l1_ffn_up, l1_ffn_conv_w, l1_ffn_down, final_norm_g, loss_target, m_l0_mix_norm_g, m_l0_w_in, m_l0_sc_conv_w, m_l0_w_out, m_l0_ffn_norm_g, m_l0_ffn_up, m_l0_ffn_conv_w, m_l0_ffn_down, m_l1_mix_norm_g, m_l1_w_in, m_l1_fox_b_f, m_l1_sg_w, m_l1_sg_b, m_l1_sg_norm_g, m_l1_w_out, m_l1_ffn_norm_g, m_l1_ffn_up, m_l1_ffn_conv_w, m_l1_ffn_down, m_final_norm_g, v_l0_mix_norm_g, v_l0_w_in, v_l0_sc_conv_w, v_l0_w_out, v_l0_ffn_norm_g, v_l0_ffn_up, v_l0_ffn_conv_w, v_l0_ffn_down, v_l1_mix_norm_g, v_l1_w_in, v_l1_fox_b_f, v_l1_sg_w, v_l1_sg_b, v_l1_sg_norm_g, v_l1_w_out, v_l1_ffn_norm_g, v_l1_ffn_up, v_l1_ffn_conv_w, v_l1_ffn_down, v_final_norm_g):
    given = (l0_mix_norm_g, l0_w_in, l0_sc_conv_w, l0_w_out, l0_ffn_norm_g, l0_ffn_up, l0_ffn_conv_w, l0_ffn_down, l1_mix_norm_g, l1_w_in, l1_fox_b_f, l1_sg_w, l1_sg_b, l1_sg_norm_g, l1_w_out, l1_ffn_norm_g, l1_ffn_up, l1_ffn_conv_w, l1_ffn_down, final_norm_g)
    given_m = (m_l0_mix_norm_g, m_l0_w_in, m_l0_sc_conv_w, m_l0_w_out, m_l0_ffn_norm_g, m_l0_ffn_up, m_l0_ffn_conv_w, m_l0_ffn_down, m_l1_mix_norm_g, m_l1_w_in, m_l1_fox_b_f, m_l1_sg_w, m_l1_sg_b, m_l1_sg_norm_g, m_l1_w_out, m_l1_ffn_norm_g, m_l1_ffn_up, m_l1_ffn_conv_w, m_l1_ffn_down, m_final_norm_g)
    given_v = (v_l0_mix_norm_g, v_l0_w_in, v_l0_sc_conv_w, v_l0_w_out, v_l0_ffn_norm_g, v_l0_ffn_up, v_l0_ffn_conv_w, v_l0_ffn_down, v_l1_mix_norm_g, v_l1_w_in, v_l1_fox_b_f, v_l1_sg_w, v_l1_sg_b, v_l1_sg_norm_g, v_l1_w_out, v_l1_ffn_norm_g, v_l1_ffn_up, v_l1_ffn_conv_w, v_l1_ffn_down, v_final_norm_g)
    wt = dict(zip(WEIGHTS, given))
    mom = dict(zip(WEIGHTS, given_m))
    var = dict(zip(WEIGHTS, given_v))
    s = x.shape[1]
    t = ATT_BLOCK
    x0, target = x[0], loss_target[0]
    chip = 2 * lax.axis_index("x") + lax.axis_index("y")

    pos = jnp.stack([chip, lax.axis_index("c")]).astype(jnp.int32)

    conv_widths = [wt[n].shape[1] for n in CONV]
    conv_all = _gather_small(jnp.concatenate([wt[n] for n in CONV], axis=1))
    conv_full, at = {}, 0
    for n, cw in zip(CONV, conv_widths):
        conv_full[n] = jnp.transpose(conv_all[:, :, at:at + cw], (1, 0, 2)).reshape(3, 4 * cw)
        at += cw
    gathers, token = [], conv_all
    for gi, names in enumerate(GATHER_GROUPS):
        placed = [_place_shard(pos, wt[n], BIG[n], "place_" + n, [token]) for n in names]
        send, recv, thru, token = _gather_start(placed, [BIG[n] for n in names], "gather_start_%d" % gi, [token])
        gathers.append((send, recv, thru))
    full = {}

    def forward_gather(gi, after):
        send, recv, thru = gathers[gi]
        kinds = [BIG[n] for n in GATHER_GROUPS[gi]]
        gathers[gi] = _gather_forward(thru, kinds, send, recv, after, "gather_forward_%d" % gi)
        return gathers[gi][3]

    def finish_gather(gi, after):
        send, recv, thru, tok = gathers[gi]
        names = GATHER_GROUPS[gi]
        wholes = _gather_finish(thru, [BIG[n] for n in names], send, recv, tok if after is None else after,
                                "gather_finish_%d" % gi)
        full.update(zip(names, wholes))

    def vec(name):
        return wt[name].reshape(1, -1)

    h0 = _rmsnorm_fwd(x0, vec('l0_mix_norm_g'), "l0_mix_norm", after=[token])
    forward_gather(0, h0)
    finish_gather(0, None)
    p0 = _matmul(h0, full['l0_w_in'], "nn", F32, "l0_in")
    a_out, sb_carries = _sb_fwd(p0, "l0_sb", after=[forward_gather(1, p0)])
    finish_gather(1, a_out)
    b_out = _sc_fwd(p0, conv_full['l0_sc_conv_w'], "l0_sc")
    ab0 = jnp.concatenate([a_out.astype(BF16), b_out], axis=1)
    x1 = _matmul(ab0, full['l0_w_out'], "nn", F32, "l0_out", res=x0, after=[forward_gather(2, b_out)])

    def ffn_weights(up_group, next_group):
        def get_up(h):
            finish_gather(up_group, h)
            return full[GATHER_GROUPS[up_group][0]]

        def behind_act(u):
            return [forward_gather(up_group + 1, u)]

        def get_down(f):
            finish_gather(up_group + 1, f)
            return full[GATHER_GROUPS[up_group + 1][0]], ([forward_gather(next_group, f)] if next_group else ())

        return get_up, behind_act, get_down

    x2, ffn0_saved = _ffn_forward(x1, vec('l0_ffn_norm_g'), *ffn_weights(2, 4), conv_full['l0_ffn_conv_w'], "l0_ffn")
    h2 = _rmsnorm_fwd(x2, vec('l1_mix_norm_g'), "l1_mix_norm")
    finish_gather(4, h2)
    w_in1 = jnp.transpose(full['l1_w_in'], (1, 0, 2)).reshape(D_MODEL, IN_CD)
    w_in1_main = w_in1[:, :5 * HALF]
    w_in1_f = jnp.pad(w_in1[:, 5 * HALF:], ((0, 0), (0, 128 - N_HEADS)))
    p1 = _matmul(h2, w_in1_main, "nn", F32, "l1_in")
    f_logit = _matmul(h2, w_in1_f, "nn", F32, "l1_in_f")
    b_f = jnp.pad(wt['l1_fox_b_f'], (0, 128 - N_HEADS)).reshape(1, 128)
    c_heads = _fox_prep(f_logit, b_f, "l1_fox_prep")[:, :N_HEADS].T
    c_col = c_heads[:, :, None]
    c_row = c_heads.reshape(N_HEADS, s // t, 1, t)
    sg_bias = jnp.repeat(wt['l1_sg_b'].T, HEAD, axis=1)
    sg_gain = vec('l1_sg_norm_g')
    c_out = _sg_fwd(p1, wt['l1_sg_w'], sg_bias, sg_gain, "l1_sg")
    d_out, lse = _fox_fwd(p1, c_col, c_row, "l1_fox", after=[forward_gather(5, c_out)])
    cd1 = jnp.concatenate([c_out, d_out.astype(BF16)], axis=1)
    x3 = _matmul(cd1, full['l1_w_out'], "nn", F32, "l1_out", res=x2)
    x4, ffn1_saved = _ffn_forward(x3, vec('l1_ffn_norm_g'), *ffn_weights(5, None), conv_full['l1_ffn_conv_w'], "l1_ffn")
    dx4, dg_final, loss_part = _loss_head(x4, vec('final_norm_g'), target, "loss_head")

    grads = {'final_norm_g': dg_final}
    scatters = []

    def send_grads(names):
        def start(*group):
            send, recv, thru, lands, tok = _scatter_start(list(group), [BIG[n] for n in names],
                                                          "scatter_start_%d" % len(scatters))
            scatters.append((names, send, recv, thru, lands))
            return tok
        return start

    dx3, grads['l1_ffn_norm_g'], grads['l1_ffn_conv_w'] = _ffn_backward(
        x3, vec('l1_ffn_norm_g'), full['l1_ffn_up'], conv_full['l1_ffn_conv_w'], full['l1_ffn_down'], ffn1_saved, dx4,
        send_grads(['l1_ffn_up']), send_grads(['l1_ffn_down']), "l1_ffn")
    dw_out1 = _matmul(cd1, dx3, "tn", BF16, "l1_dwout")
    d_cd = _matmul(dx3, full['l1_w_out'], "nt", F32, "l1_dcd")
    duv, grads['l1_sg_w'], db_sg, grads['l1_sg_norm_g'] = _sg_bwd(p1, wt['l1_sg_w'], sg_bias, sg_gain, d_cd, "l1_dsg")
    grads['l1_sg_b'] = db_sg[:, :N_HEADS].T
    dqkv1, dcol, drow = _fox_bwd(p1, c_col, c_row, lse, d_cd, d_out, "l1_dfox")
    d_f_logit, d_b_f = _fox_post(drow, dcol, f_logit, b_f, "l1_fox_post")
    grads['l1_fox_b_f'] = d_b_f[0, :N_HEADS]
    dp1 = jnp.concatenate([duv, dqkv1], axis=0)
    dw_main = _matmul(h2, dp1, "tn", BF16, "l1_dwin")
    dw_f = _matmul(h2, d_f_logit, "tn", BF16, "l1_dwin_f")
    dw_in1 = jnp.concatenate([dw_main, dw_f[:, :N_HEADS]], axis=1)
    dw_in1 = jnp.transpose(dw_in1.reshape(D_MODEL, 4, IN_CD // 4), (1, 0, 2))
    dh2 = _matmul(dp1, w_in1_main, "nt", F32, "l1_dh", after=[send_grads(['l1_w_out', 'l1_w_in'])(dw_out1, dw_in1)])
    dh2 = _matmul(d_f_logit, w_in1_f, "nt", F32, "l1_dh_f", res=dh2)
    dx2, grads['l1_mix_norm_g'] = _rmsnorm_bwd(x2, vec('l1_mix_norm_g'), dh2, dx3, "l1_dmix_norm")
    dx1, grads['l0_ffn_norm_g'], grads['l0_ffn_conv_w'] = _ffn_backward(
        x1, vec('l0_ffn_norm_g'), full['l0_ffn_up'], conv_full['l0_ffn_conv_w'], full['l0_ffn_down'], ffn0_saved, dx2,
        send_grads(['l0_ffn_up']), send_grads(['l0_ffn_down']), "l0_ffn")
    dw_out0 = _matmul(ab0, dx1, "tn", BF16, "l0_dwout")
    d_ab = _matmul(dx1, full['l0_w_out'], "nt", F32, "l0_dab", after=[send_grads(['l0_w_out'])(dw_out0)])
    dqkv0 = _sb_bwd(p0, d_ab, sb_carries, "l0_dsb")
    dconv0, grads['l0_sc_conv_w'] = _sc_bwd(p0, conv_full['l0_sc_conv_w'], d_ab, "l0_dsc")
    dp0 = jnp.concatenate([dqkv0, dconv0], axis=0)
    dw_in0 = _matmul(h0, dp0, "tn", BF16, "l0_dwin")
    dh0 = _matmul(dp0, full['l0_w_in'], "nt", F32, "l0_dh", after=[send_grads(['l0_w_in'])(dw_in0)])
    dx0, grads['l0_mix_norm_g'] = _rmsnorm_bwd(x0, vec('l0_mix_norm_g'), dh0, dx1, "l0_dmix_norm")

    shard_grads, delta, new_m, new_v, swaps = {}, {}, {}, {}, {}

    def reduce_group(gi, after):
        names, send, recv, thru, lands = scatters[gi]
        kinds = [BIG[n] for n in names]
        g_thru, landed = _scatter_wait(thru, lands, kinds, send, recv, after, "scatter_wait_%d" % gi)
        halves = [_sum_partials(pos, g, ld, kd, "sum_" + n) for n, g, ld, kd in zip(names, g_thru, landed, kinds)]
        s_send, s_recv, h_thru, s_lands, tok = _swap_start(halves, "swap_start_%d" % gi)
        swaps[gi] = (names, s_send, s_recv, h_thru, s_lands)
        return tok

    def update_group(gi, after):
        names, s_send, s_recv, h_thru, s_lands = swaps[gi]
        mine, theirs = _swap_wait(h_thru, s_lands, s_send, s_recv, after, "swap_wait_%d" % gi)
        for n, gm, gs in zip(names, mine, theirs):
            shard_grads[n], delta[n], new_m[n], new_v[n] = _adamw_shard(pos, wt[n], gm, gs, mom[n], var[n], "adamw_" + n)
        return [delta[n] for n in names]

    after = reduce_group(2, reduce_group(1, reduce_group(0, dx0)))
    after = update_group(2, update_group(1, update_group(0, after)))
    after = reduce_group(5, reduce_group(4, reduce_group(3, after)))
    after = update_group(5, update_group(4, update_group(3, after)))
    after = reduce_group(6, after)
    small_shapes = [conv_full[n].shape if n in CONV else wt[n].shape for n in SMALL] + [loss_part.shape]
    small_all = _allreduce_small(_pack([grads[n] for n in SMALL] + [loss_part]), [after])
    small_sums = _unpack(small_all, small_shapes)
    loss = small_sums[-1][0, 0]
    for n, g in zip(SMALL, small_sums):
        shard_grads[n] = lax.dynamic_slice_in_dim(g, chip * wt[n].shape[1], wt[n].shape[1], axis=1) if n in CONV else g
    update_group(6, small_all)
    def flat2d(a):
        return a.reshape(-1, a.shape[-1])

    small_out = _adamw_small(*[[flat2d(src[n]) for n in SMALL] for src in (wt, shard_grads, mom, var)], "adamw_small")
    for out, arrays in zip((delta, new_m, new_v), small_out):
        out.update((n, a.reshape(wt[n].shape)) for n, a in zip(SMALL, arrays))

    return (loss, dx0[None], *[shard_grads[n] for n in WEIGHTS], *[delta[n] for n in WEIGHTS],
            *[new_m[n] for n in WEIGHTS], *[new_v[n] for n in WEIGHTS])
```

```python
import jax
import jax.numpy as jnp
from jax import lax
from jax.experimental import pallas as pl
from jax.experimental.pallas import tpu as pltpu

F32 = jnp.float32
BF16 = jnp.bfloat16

D_MODEL = 2048
HEAD = 128
N_HEADS = 8
HALF = N_HEADS * HEAD
D_FF = 5632
EPS = 1e-6
ATT_SCALE = HEAD ** -0.5
ATT_BLOCK = 512
NEG = -1e30

ADAM_LR = 0.001
ADAM_B1 = 0.9
ADAM_B2 = 0.999
ADAM_EPS = 1e-08
ADAM_WD = 0.01
ADAM_STEP = 10

VMEM_LIMIT_BYTES = 48 * 1024 * 1024
MM_VMEM_LIMIT_BYTES = 56 * 1024 * 1024
ATT_VMEM_LIMIT_BYTES = 48 * 1024 * 1024
MESH = pl.DeviceIdType.MESH
HBM_SPEC = pl.BlockSpec(memory_space=pltpu.HBM)
VMEM_SPEC = pl.BlockSpec(memory_space=pltpu.VMEM)


def _pcall(body, after=(), **kw):
    if not after:
        return pl.pallas_call(body, **kw)
    n_in, n_after, inner = len(kw["in_specs"]), len(after), body
    kw["in_specs"] = list(kw["in_specs"]) + [pl.BlockSpec(memory_space=pl.ANY)] * n_after

    def body(*refs):
        inner(*refs[:n_in], *refs[n_in + n_after:])

    call = pl.pallas_call(body, **kw)
    return lambda *args: call(*args, *after)


def _params(*semantics, vmem_limit=VMEM_LIMIT_BYTES):
    return pltpu.CompilerParams(dimension_semantics=semantics, vmem_limit_bytes=vmem_limit)


def _pick(n, cap):
    best = None
    for t in range(128, min(n, cap) + 1, 128):
        if n % t == 0:
            best = t
    return n if best is None else best


def _dot(a, b, dims):
    return lax.dot_general(a, b, (dims, ((), ())), preferred_element_type=F32)


def _dot_nn(a, b):
    return _dot(a, b, ((1,), (0,)))


def _dot_nt(a, b):
    return _dot(a, b, ((1,), (1,)))


def _dot_tn(a, b):
    return _dot(a, b, ((0,), (0,)))


def _split3(x):
    hi = x.astype(BF16)
    r1 = x - hi.astype(F32)
    mid = r1.astype(BF16)
    lo = (r1 - mid.astype(F32)).astype(BF16)
    return hi, mid, lo


def _log_sigmoid(z):
    return jnp.minimum(z, 0.0) - jnp.log1p(jnp.exp(-jnp.abs(z)))


_GELU_K = 0.7978845608028654


def _gelu(x):
    return 0.5 * x * (1.0 + jnp.tanh(_GELU_K * (x + 0.044715 * x * x * x)))


def _gelu_grad(x):
    t = jnp.tanh(_GELU_K * (x + 0.044715 * x * x * x))
    return 0.5 * (1.0 + t) + 0.5 * x * (1.0 - t * t) * _GELU_K * (1.0 + 3.0 * 0.044715 * x * x)


SUBLANES = 8


def _shift_down(x, k):
    rolled = pltpu.roll(x, k, axis=0)
    head = rolled[:SUBLANES]
    head = jnp.where(lax.broadcasted_iota(jnp.int32, head.shape, 0) >= k, head, 0.0)
    return jnp.concatenate([head, rolled[SUBLANES:]], axis=0)


def _shift_up(x, k):
    n = x.shape[0]
    rolled = pltpu.roll(x, n - k, axis=0)
    tail = rolled[n - SUBLANES:]
    tail = jnp.where(lax.broadcasted_iota(jnp.int32, tail.shape, 0) < SUBLANES - k, tail, 0.0)
    return jnp.concatenate([rolled[:n - SUBLANES], tail], axis=0)


def _conv3(s, w, shifted=None):
    s1, s2 = shifted if shifted else (_shift_down(s, 1), _shift_down(s, 2))
    return w[0:1, :] * s2 + w[1:2, :] * s1 + w[2:3, :] * s


def _conv3_transpose(d, w):
    return w[2:3, :] * d + w[1:2, :] * _shift_up(d, 1) + w[0:1, :] * _shift_up(d, 2)


def _conv3_wgrad(d, s, shifted, dw_ref):
    s1, s2 = shifted
    dw_ref[0:1, :] = jnp.sum(d * s2, axis=0, keepdims=True)
    dw_ref[1:2, :] = jnp.sum(d * s1, axis=0, keepdims=True)
    dw_ref[2:3, :] = jnp.sum(d * s, axis=0, keepdims=True)


MM_TILE_M, MM_TILE_N, MM_TILE_K = 1408, 512, 5632


def _matmul(a, b, mode, out_dtype, name, res=None, after=()):
    a_parts = a.shape[0] if a.ndim == 3 else 1
    b_parts = b.shape[0] if b.ndim == 3 else 1
    a_shape = (a.shape[1], a_parts * a.shape[2]) if a.ndim == 3 else a.shape
    b_shape = (b.shape[1], b_parts * b.shape[2]) if b.ndim == 3 else b.shape
    assert (a_parts == 1 or mode != "tn") and (b_parts == 1 or mode == "tn")
    if mode == "nn":
        (m, k), (k2, n) = a_shape, b_shape
    elif mode == "nt":
        (m, k), (n, k2) = a_shape, b_shape
    else:
        (k, m), (k2, n) = a_shape, b_shape
    assert k == k2, (a.shape, b.shape, mode)
    tm, tn, tk = _pick(m, MM_TILE_M), _pick(n // b_parts, MM_TILE_N), _pick(k // a_parts, MM_TILE_K)
    nk = k // tk
    if mode == "tn":
        a_spec = pl.BlockSpec((tk, tm), lambda i, j, kk: (kk, i))
    elif a_parts > 1:
        per = nk // a_parts
        a_spec = pl.BlockSpec((None, tm, tk), lambda i, j, kk: (kk // per, i, kk % per))
    else:
        a_spec = pl.BlockSpec((tm, tk), lambda i, j, kk: (i, kk))
    if mode == "nt":
        b_spec = pl.BlockSpec((tn, tk), lambda i, j, kk: (j, kk))
    elif b_parts > 1:
        per = n // b_parts // tn
        b_spec = pl.BlockSpec((None, tk, tn), lambda i, j, kk: (j // per, kk, j % per))
    else:
        b_spec = pl.BlockSpec((tk, tn), lambda i, j, kk: (kk, j))
    o_spec = pl.BlockSpec((tm, tn), lambda i, j, kk: (i, j))
    dims = {"nn": ((1,), (0,)), "nt": ((1,), (1,)), "tn": ((0,), (0,))}[mode]
    has_res = res is not None

    def body(*refs):
        a_ref, b_ref = refs[0], refs[1]
        r_ref = refs[2] if has_res else None
        o_ref = refs[3] if has_res else refs[2]
        part = _dot(a_ref[...].astype(BF16), b_ref[...].astype(BF16), dims)

        def finish(total):
            if has_res:
                total = total + r_ref[...]
            o_ref[...] = total.astype(out_dtype)

        if nk == 1:
            finish(part)
        else:
            acc_ref = refs[-1]
            kk = pl.program_id(2)

            @pl.when(kk == 0)
            def _():
                acc_ref[...] = part

            @pl.when(kk > 0)
            def _():
                acc_ref[...] += part

            @pl.when(kk == nk - 1)
            def _():
                finish(acc_ref[...])

    in_specs = [a_spec, b_spec] + ([o_spec] if has_res else [])
    args = (a, b) + ((res,) if has_res else ())
    return _pcall(
        body, after=after, name=name, grid=(m // tm, n // tn, nk),
        in_specs=in_specs, out_specs=o_spec,
        out_shape=jax.ShapeDtypeStruct((m, n), out_dtype),
        scratch_shapes=[pltpu.VMEM((tm, tn), F32)] if nk > 1 else [],
        compiler_params=_params("parallel", "parallel", "arbitrary", vmem_limit=MM_VMEM_LIMIT_BYTES),
    )(*args)


ROW_TILE = 256


def _rmsnorm_fwd(x, g, name, after=()):
    s, d = x.shape

    def body(x_ref, g_ref, o_ref):
        xf = x_ref[...]
        r = lax.rsqrt(jnp.mean(xf * xf, axis=-1, keepdims=True) + EPS)
        o_ref[...] = (xf * r * g_ref[...]).astype(BF16)

    row = pl.BlockSpec((ROW_TILE, d), lambda i: (i, 0))
    vec = pl.BlockSpec((1, d), lambda i: (0, 0))
    return _pcall(body, after=after, name=name, grid=(s // ROW_TILE,), in_specs=[row, vec], out_specs=row,
                  out_shape=jax.ShapeDtypeStruct((s, d), BF16), compiler_params=_params("parallel"))(x, g)


def _rmsnorm_bwd(x, g, dh, dres, name):
    s, d = x.shape

    def body(x_ref, g_ref, dh_ref, dres_ref, dx_ref, dg_ref):
        xf = x_ref[...]
        r = lax.rsqrt(jnp.mean(xf * xf, axis=-1, keepdims=True) + EPS)
        xhat = xf * r
        dh_v = dh_ref[...]
        dxh = dh_v * g_ref[...]
        proj = jnp.mean(dxh * xhat, axis=-1, keepdims=True)
        dx_ref[...] = dres_ref[...] + r * (dxh - xhat * proj)
        part = jnp.sum(dh_v * xhat, axis=0, keepdims=True)

        @pl.when(pl.program_id(0) == 0)
        def _():
            dg_ref[...] = part

        @pl.when(pl.program_id(0) > 0)
        def _():
            dg_ref[...] += part

    row = pl.BlockSpec((ROW_TILE, d), lambda i: (i, 0))
    vec = pl.BlockSpec((1, d), lambda i: (0, 0))
    return _pcall(body, name=name, grid=(s // ROW_TILE,), in_specs=[row, vec, row, row], out_specs=[row, vec],
                  out_shape=[jax.ShapeDtypeStruct((s, d), F32), jax.ShapeDtypeStruct((1, d), F32)],
                  compiler_params=_params("arbitrary"))(x, g, dh, dres)


def _loss_head(x, g, target, name):
    s, d = x.shape

    def body(x_ref, g_ref, t_ref, dx_ref, dg_ref, loss_ref):
        xf = x_ref[...]
        r = lax.rsqrt(jnp.mean(xf * xf, axis=-1, keepdims=True) + EPS)
        xhat = xf * r
        gv = g_ref[...]
        err = xhat * gv - t_ref[...]
        dy = err * (1.0 / d)
        dxh = dy * gv
        proj = jnp.mean(dxh * xhat, axis=-1, keepdims=True)
        dx_ref[...] = r * (dxh - xhat * proj)
        dg_part = jnp.sum(dy * xhat, axis=0, keepdims=True)
        row_loss = jnp.sum(err * err, axis=-1, keepdims=True) * (0.5 / d)
        loss_part = jnp.broadcast_to(jnp.sum(row_loss, axis=0, keepdims=True), (1, 128))

        @pl.when(pl.program_id(0) == 0)
        def _():
            dg_ref[...] = dg_part
            loss_ref[...] = loss_part

        @pl.when(pl.program_id(0) > 0)
        def _():
            dg_ref[...] += dg_part
            loss_ref[...] += loss_part

    row = pl.BlockSpec((ROW_TILE, d), lambda i: (i, 0))
    vec = pl.BlockSpec((1, d), lambda i: (0, 0))
    one = pl.BlockSpec((1, 128), lambda i: (0, 0))
    return _pcall(body, name=name, grid=(s // ROW_TILE,), in_specs=[row, vec, row], out_specs=[row, vec, one],
                  out_shape=[jax.ShapeDtypeStruct((s, d), F32), jax.ShapeDtypeStruct((1, d), F32),
                             jax.ShapeDtypeStruct((1, 128), F32)],
                  compiler_params=_params("arbitrary"))(x, g, target)


HEADS_PER_STEP = 2
GROUP_W = HEADS_PER_STEP * HEAD
N_GROUPS = N_HEADS // HEADS_PER_STEP


def _head_cols(h):
    return slice(h * HEAD, (h + 1) * HEAD)


def _head_specs(s, col0):
    t = ATT_BLOCK
    g0 = [c // HEADS_PER_STEP for c in col0]
    qspec = pl.BlockSpec((t, GROUP_W), lambda g, i: (i, g0[0] + g))
    kspec = pl.BlockSpec((s, GROUP_W), lambda g, i: (0, g0[1] + g))
    vspec = pl.BlockSpec((s, GROUP_W), lambda g, i: (0, g0[2] + g))
    return qspec, kspec, vspec


TRI = 256


def _order_matrix(later):
    r, c = lax.broadcasted_iota(jnp.int32, (TRI, TRI), 0), lax.broadcasted_iota(jnp.int32, (TRI, TRI), 1)
    return (r > c if later else r < c).astype(BF16)


def _exact_dot(x, m, later):
    parts = [x[:, c:c + TRI] for c in range(0, x.shape[1], TRI)]
    totals = [jnp.sum(p, axis=1, keepdims=True) for p in parts] if len(parts) > 1 else None
    out = []
    for j, p in enumerate(parts):
        hi = p.astype(BF16)
        lo = (p - hi.astype(F32)).astype(BF16)
        acc = _dot_nn(hi, m) + _dot_nn(lo, m)
        for other in (range(j + 1, len(parts)) if later else range(j)):
            acc = acc + totals[other]
        out.append(acc)
    return out[0] if len(out) == 1 else jnp.concatenate(out, axis=1)


def _sb_block(q, kblk, carry_l, u, diagonal):
    t = ATT_BLOCK
    z = _dot_nt(q, kblk) * ATT_SCALE
    sp = jnp.maximum(z, 0.0) + jnp.log(1.0 + jnp.exp(-jnp.abs(z)))
    if not diagonal:
        l = -sp
        return z, None, l, jnp.exp(z + l + _exact_dot(l, u, True) + carry_l)
    mask = lax.broadcasted_iota(jnp.int32, (t, t), 1) < lax.broadcasted_iota(jnp.int32, (t, t), 0)
    l = jnp.where(mask, -sp, 0.0)
    a = jnp.where(mask, jnp.exp(z - sp + _exact_dot(l, u, True) + carry_l), 0.0)
    return z, mask, l, a


def _sb_carry_spec(s):
    t = ATT_BLOCK
    return pl.BlockSpec((HEADS_PER_STEP, None, s // t, t, 1), lambda g, i: (g, i, 0, 0, 0))


def _sb_fwd(p, name, after=()):
    s = p.shape[0]
    t = ATT_BLOCK
    nb = s // t

    def body(q_ref, k_ref, v_ref, o_ref, cl_ref):
        i = pl.program_id(1)
        heads = range(HEADS_PER_STEP)
        q = [q_ref[:, _head_cols(h)].astype(BF16) for h in heads]
        u = _order_matrix(True)
        cl_ref[...] = jnp.zeros_like(cl_ref)

        def tile(kb, carry, diagonal):
            ks = pl.multiple_of(kb * t, t)
            out = []
            for h in heads:
                acc, carry_l = carry[h]
                kblk = k_ref[pl.ds(ks, t), _head_cols(h)].astype(BF16)
                vblk = v_ref[pl.ds(ks, t), _head_cols(h)].astype(BF16)
                cl_ref[h, kb] = carry_l
                _, _, l, a = _sb_block(q[h], kblk, carry_l, u, diagonal)
                out.append((acc + _dot_nn(a.astype(BF16), vblk), carry_l + jnp.sum(l, axis=1, keepdims=True)))
            return tuple(out)

        carry = tile(i, tuple((jnp.zeros((t, HEAD), F32), jnp.zeros((t, 1), F32)) for _ in heads), True)
        carry = lax.fori_loop(0, i, lambda n, c: tile(i - 1 - n, c, False), carry)
        for h in heads:
            o_ref[:, _head_cols(h)] = carry[h][0]

    qspec, kspec, vspec = _head_specs(s, (0, N_HEADS, 2 * N_HEADS))
    ospec = pl.BlockSpec((t, GROUP_W), lambda g, i: (i, g))
    return _pcall(body, after=after, name=name, grid=(N_GROUPS, nb), in_specs=[qspec, kspec, vspec],
                  out_specs=[ospec, _sb_carry_spec(s)],
                  out_shape=[jax.ShapeDtypeStruct((s, HALF), F32), jax.ShapeDtypeStruct((N_HEADS, nb, nb, t, 1), F32)],
                  compiler_params=_params("parallel", "parallel", vmem_limit=ATT_VMEM_LIMIT_BYTES))(p, p, p)


def _sb_bwd(p, d_ab, carries, name):
    s = p.shape[0]
    t = ATT_BLOCK

    def body(q_ref, k_ref, v_ref, do_ref, cl_ref, dq_ref, dk_ref, dv_ref):
        i = pl.program_id(1)

        @pl.when(i == 0)
        def _():
            dk_ref[...] = jnp.zeros_like(dk_ref)
            dv_ref[...] = jnp.zeros_like(dv_ref)

        heads = range(HEADS_PER_STEP)
        q = [q_ref[:, _head_cols(h)].astype(BF16) for h in heads]
        do = [do_ref[:, _head_cols(h)].astype(BF16) for h in heads]
        u = _order_matrix(True)
        lower = _order_matrix(False)

        def tile(kb, carry, diagonal):
            ks = pl.multiple_of(kb * t, t)
            out = []
            for h in heads:
                dq, carry_g = carry[h]
                kblk = k_ref[pl.ds(ks, t), _head_cols(h)].astype(BF16)
                vblk = v_ref[pl.ds(ks, t), _head_cols(h)].astype(BF16)
                z, mask, _, a = _sb_block(q[h], kblk, cl_ref[h, kb], u, diagonal)
                g = a * _dot_nt(do[h], vblk)
                earlier_g = _exact_dot(g, lower, False) + carry_g
                sig = jax.nn.sigmoid(z)
                dz = g * (1.0 - sig) - sig * earlier_g
                if diagonal:
                    dz = jnp.where(mask, dz, 0.0)
                dz = dz.astype(BF16)
                dv_ref[pl.ds(ks, t), _head_cols(h)] += _dot_tn(a.astype(BF16), do[h])
                dk_ref[pl.ds(ks, t), _head_cols(h)] += _dot_tn(dz, q[h]) * ATT_SCALE
                out.append((dq + _dot_nn(dz, kblk) * ATT_SCALE, carry_g + jnp.sum(g, axis=1, keepdims=True)))
            return tuple(out)

        init = tuple((jnp.zeros((t, HEAD), F32), jnp.zeros((t, 1), F32)) for _ in heads)
        carry = tile(i, lax.fori_loop(0, i, lambda kb, c: tile(kb, c, False), init), True)
        for h in heads:
            dq_ref[:, _head_cols(h)] = carry[h][0]

    qspec, kspec, vspec = _head_specs(s, (0, N_HEADS, 2 * N_HEADS))
    blk = pl.BlockSpec((t, GROUP_W), lambda g, i: (i, g))
    whole = pl.BlockSpec((s, GROUP_W), lambda g, i: (0, g))
    shape = jax.ShapeDtypeStruct((s, HALF), F32)
    return _pcall(body, name=name, grid=(N_GROUPS, s // t), in_specs=[qspec, kspec, vspec, blk, _sb_carry_spec(s)],
                  out_specs=[blk, whole, whole], out_shape=[shape, shape, shape],
                  compiler_params=_params("parallel", "arbitrary", vmem_limit=ATT_VMEM_LIMIT_BYTES))(p, p, p, d_ab, carries)


COL_TILE = 256


def _sc_fwd(p, w, name):
    s = p.shape[0]
    nb = HALF // COL_TILE

    def body(gb_ref, gc_ref, h_ref, w_ref, o_ref):
        conv = _conv3(gc_ref[...] * h_ref[...], w_ref[...])
        o_ref[...] = (gb_ref[...] * conv).astype(BF16)

    def col(k):
        return pl.BlockSpec((s, COL_TILE), lambda j: (0, k * nb + j))

    wspec = pl.BlockSpec((3, COL_TILE), lambda j: (0, j))
    return _pcall(body, name=name, grid=(nb,), in_specs=[col(3), col(4), col(5), wspec], out_specs=col(0),
                  out_shape=jax.ShapeDtypeStruct((s, HALF), BF16), compiler_params=_params("parallel"))(p, p, p, w)


def _sc_bwd(p, w, d_ab, name):
    s = p.shape[0]
    nb = HALF // COL_TILE

    def body(gb_ref, gc_ref, h_ref, w_ref, d_ref, dgb_ref, dgc_ref, dh_ref, dw_ref):
        gc, hin, wv, d = gc_ref[...], h_ref[...], w_ref[...], d_ref[...]
        sig = gc * hin
        shifted = (_shift_down(sig, 1), _shift_down(sig, 2))
        dgb_ref[...] = d * _conv3(sig, wv, shifted)
        dconv = d * gb_ref[...]
        _conv3_wgrad(dconv, sig, shifted, dw_ref)
        dsig = _conv3_transpose(dconv, wv)
        dgc_ref[...] = dsig * hin
        dh_ref[...] = dsig * gc

    def col(k):
        return pl.BlockSpec((s, COL_TILE), lambda j: (0, k * nb + j))

    wspec = pl.BlockSpec((3, COL_TILE), lambda j: (0, j))
    act = jax.ShapeDtypeStruct((s, HALF), F32)
    return _pcall(body, name=name, grid=(nb,), in_specs=[col(3), col(4), col(5), wspec, col(1)],
                  out_specs=[col(0), col(0), col(0), wspec],
                  out_shape=[act, act, act, jax.ShapeDtypeStruct((3, HALF), F32)],
                  compiler_params=_params("parallel"))(p, p, p, w, d_ab)


def _ffn_act_fwd(u, w, name, after=()):
    s = u.shape[0]
    nb = D_FF // COL_TILE

    def body(ug_ref, uu_ref, wg_ref, wu_ref, o_ref):
        gate = _conv3(ug_ref[...], wg_ref[...])
        up = _conv3(uu_ref[...], wu_ref[...])
        o_ref[...] = (gate * jax.nn.sigmoid(gate) * up).astype(BF16)

    def col(k):
        return pl.BlockSpec((s, COL_TILE), lambda j: (0, k * nb + j))

    def wcol(k):
        return pl.BlockSpec((3, COL_TILE), lambda j: (0, k * nb + j))

    return _pcall(body, after=after, name=name, grid=(nb,), in_specs=[col(0), col(1), wcol(0), wcol(1)], out_specs=col(0),
                  out_shape=jax.ShapeDtypeStruct((s, D_FF), BF16),
                  compiler_params=_params("parallel"))(u, u, w, w)


def _ffn_act_bwd(u, w, d_f, name):
    s = u.shape[0]
    nb = D_FF // COL_TILE

    def body(ug_ref, uu_ref, wg_ref, wu_ref, d_ref, du_ref, dwg_ref, dwu_ref):
        ug, uu, wg, wu, d = ug_ref[...], uu_ref[...], wg_ref[...], wu_ref[...], d_ref[...]
        ug_shifted = (_shift_down(ug, 1), _shift_down(ug, 2))
        uu_shifted = (_shift_down(uu, 1), _shift_down(uu, 2))
        gate = _conv3(ug, wg, ug_shifted)
        up = _conv3(uu, wu, uu_shifted)
        sig = jax.nn.sigmoid(gate)
        d_up = d * gate * sig
        d_gate = d * up * sig * (1.0 + gate * (1.0 - sig))
        _conv3_wgrad(d_gate, ug, ug_shifted, dwg_ref)
        _conv3_wgrad(d_up, uu, uu_shifted, dwu_ref)
        du_ref[0] = _conv3_transpose(d_gate, wg).astype(BF16)
        du_ref[1] = _conv3_transpose(d_up, wu).astype(BF16)

    def col(k):
        return pl.BlockSpec((s, COL_TILE), lambda j: (0, k * nb + j))

    def wcol(k):
        return pl.BlockSpec((3, COL_TILE), lambda j: (0, k * nb + j))

    both = pl.BlockSpec((2, s, COL_TILE), lambda j: (0, 0, j))
    wsh = jax.ShapeDtypeStruct((3, D_FF), F32)
    return _pcall(body, name=name, grid=(nb,), in_specs=[col(0), col(1), wcol(0), wcol(1), col(0)],
                  out_specs=[both, wcol(0), wcol(0)], out_shape=[jax.ShapeDtypeStruct((2, s, D_FF), BF16), wsh, wsh],
                  compiler_params=_params("parallel"))(u, u, w, w, d_f)


def _sg_common(u, v, g, w_ref, bias, mixed_ref):
    rows = u.shape[0]
    gu = _gelu(u)
    gv = _gelu(v)
    xc = gv - jnp.mean(gv, axis=-1, keepdims=True)
    rstd = lax.rsqrt(jnp.mean(xc * xc, axis=-1, keepdims=True) + EPS)
    xhat = xc * rstd
    vn = xhat * g
    tril = lax.broadcasted_iota(jnp.int32, (HEAD, HEAD), 0) >= lax.broadcasted_iota(jnp.int32, (HEAD, HEAD), 1)
    wts = [jnp.where(tril, w_ref[grp], 0.0).astype(BF16) for grp in range(N_HEADS)]
    for n in range(rows // HEAD):
        for grp in range(N_HEADS):
            blk = vn[n * HEAD:(n + 1) * HEAD, grp * HEAD:(grp + 1) * HEAD].astype(BF16)
            mixed_ref[n * HEAD:(n + 1) * HEAD, grp * HEAD:(grp + 1) * HEAD] = _dot_nn(wts[grp], blk)
    mixed = mixed_ref[...] + jnp.concatenate([bias] * (rows // HEAD), axis=0)
    return gu, xhat, rstd, vn, mixed, wts, tril


def _sg_fwd(p, sg_w, bias, g, name):
    s = p.shape[0]

    def body(u_ref, v_ref, w_ref, b_ref, g_ref, o_ref, mixed_ref):
        gu, _, _, _, mixed, _, _ = _sg_common(u_ref[...], v_ref[...], g_ref[...], w_ref, b_ref[...], mixed_ref)
        o_ref[...] = (gu * mixed).astype(BF16)

    def half(k):
        return pl.BlockSpec((ROW_TILE, HALF), lambda i: (i, k))

    wspec = pl.BlockSpec((N_HEADS, HEAD, HEAD), lambda i: (0, 0, 0))
    bspec = pl.BlockSpec((HEAD, HALF), lambda i: (0, 0))
    gspec = pl.BlockSpec((1, HALF), lambda i: (0, 0))
    return _pcall(body, name=name, grid=(s // ROW_TILE,), in_specs=[half(0), half(1), wspec, bspec, gspec],
                  out_specs=half(0), out_shape=jax.ShapeDtypeStruct((s, HALF), BF16),
                  scratch_shapes=[pltpu.VMEM((ROW_TILE, HALF), F32)],
                  compiler_params=_params("parallel"))(p, p, sg_w, bias, g)


def _sg_bwd(p, sg_w, bias, g, d_cd, name):
    s = p.shape[0]
    nsteps = s // ROW_TILE

    def body(u_ref, v_ref, w_ref, b_ref, g_ref, d_ref, du_ref, dv_ref, dw_ref, db_ref, dg_ref,
             mixed_ref, dvn_ref, dbias_ref):
        i = pl.program_id(0)
        u, v, gain, d = u_ref[...], v_ref[...], g_ref[...], d_ref[...]
        gu, xhat, rstd, vn, mixed, wts, tril = _sg_common(u, v, gain, w_ref, b_ref[...], mixed_ref)

        @pl.when(i == 0)
        def _():
            dw_ref[...] = jnp.zeros_like(dw_ref)
            dg_ref[...] = jnp.zeros_like(dg_ref)
            dbias_ref[...] = jnp.zeros_like(dbias_ref)

        du_ref[...] = d * mixed * _gelu_grad(u)
        dm = d * gu
        for n in range(ROW_TILE // HEAD):
            rs = slice(n * HEAD, (n + 1) * HEAD)
            dbias_ref[...] += dm[rs, :]
            for grp in range(N_HEADS):
                cs = slice(grp * HEAD, (grp + 1) * HEAD)
                dm_blk = dm[rs, cs].astype(BF16)
                dw_ref[grp] += jnp.where(tril, _dot_nt(dm_blk, vn[rs, cs].astype(BF16)), 0.0)
                dvn_ref[rs, cs] = _dot_tn(wts[grp], dm_blk)
        dvn = dvn_ref[...]
        dg_ref[...] += jnp.sum(dvn * xhat, axis=0, keepdims=True)
        dxh = dvn * gain
        d_gv = rstd * (dxh - jnp.mean(dxh, axis=-1, keepdims=True) - xhat * jnp.mean(dxh * xhat, axis=-1, keepdims=True))
        dv_ref[...] = d_gv * _gelu_grad(v)

        @pl.when(i == nsteps - 1)
        def _():
            lane = lax.broadcasted_iota(jnp.int32, (HEAD, HEAD), 1)
            out = jnp.zeros((HEAD, HEAD), F32)
            for grp in range(N_HEADS):
                tot = jnp.sum(dbias_ref[:, grp * HEAD:(grp + 1) * HEAD], axis=1, keepdims=True)
                out = out + jnp.where(lane == grp, tot, 0.0)
            db_ref[...] = out

    def half(k):
        return pl.BlockSpec((ROW_TILE, HALF), lambda i: (i, k))

    wspec = pl.BlockSpec((N_HEADS, HEAD, HEAD), lambda i: (0, 0, 0))
    bspec = pl.BlockSpec((HEAD, HALF), lambda i: (0, 0))
    gspec = pl.BlockSpec((1, HALF), lambda i: (0, 0))
    dbspec = pl.BlockSpec((HEAD, HEAD), lambda i: (0, 0))
    act = jax.ShapeDtypeStruct((s, HALF), F32)
    return _pcall(body, name=name, grid=(nsteps,), in_specs=[half(0), half(1), wspec, bspec, gspec, half(0)],
                  out_specs=[half(0), half(0), wspec, dbspec, gspec],
                  out_shape=[act, act, jax.ShapeDtypeStruct((N_HEADS, HEAD, HEAD), F32),
                             jax.ShapeDtypeStruct((HEAD, HEAD), F32), jax.ShapeDtypeStruct((1, HALF), F32)],
                  scratch_shapes=[pltpu.VMEM((ROW_TILE, HALF), F32), pltpu.VMEM((ROW_TILE, HALF), F32),
                                  pltpu.VMEM((HEAD, HALF), F32)],
                  compiler_params=_params("arbitrary"))(p, p, sg_w, bias, g, d_cd)


def _fox_prep(f, b, name):
    s = f.shape[0]
    t = ATT_BLOCK

    def body(f_ref, b_ref, c_ref):
        tri = (lax.broadcasted_iota(jnp.int32, (t, t), 0) >= lax.broadcasted_iota(jnp.int32, (t, t), 1)).astype(BF16)
        carry = jnp.zeros((1, 128), F32)
        for n in range(s // t):
            lf = _log_sigmoid(f_ref[n * t:(n + 1) * t, :] + b_ref[...])
            hi, mid, lo = _split3(lf)
            c_ref[n * t:(n + 1) * t, :] = _dot_nn(tri, hi) + _dot_nn(tri, mid) + _dot_nn(tri, lo) + carry
            carry = carry + jnp.sum(lf, axis=0, keepdims=True)

    return _pcall(body, name=name, in_specs=[VMEM_SPEC, VMEM_SPEC], out_specs=VMEM_SPEC,
                  out_shape=jax.ShapeDtypeStruct((s, 128), F32))(f, b)


def _fox_post(drow, dcol, f, b, name):
    s = f.shape[0]
    t = ATT_BLOCK

    def body(drow_ref, dcol_ref, f_ref, b_ref, df_ref, db_ref):
        tri = (lax.broadcasted_iota(jnp.int32, (t, t), 1) >= lax.broadcasted_iota(jnp.int32, (t, t), 0)).astype(BF16)
        lane = lax.broadcasted_iota(jnp.int32, (t, 128), 1)
        carry = jnp.zeros((1, 128), F32)
        db = jnp.zeros((1, 128), F32)
        for n in reversed(range(s // t)):
            rs = slice(n * t, (n + 1) * t)
            dc = jnp.zeros((t, 128), F32)
            for h in range(N_HEADS):
                dc = jnp.where(lane == h, drow_ref[h, rs, :] - dcol_ref[rs, _head_cols(h)], dc)
            hi, mid, lo = _split3(dc)
            dlogf = _dot_nn(tri, hi) + _dot_nn(tri, mid) + _dot_nn(tri, lo) + carry
            carry = carry + jnp.sum(dc, axis=0, keepdims=True)
            df = dlogf * jax.nn.sigmoid(-(f_ref[rs, :] + b_ref[...]))
            df_ref[rs, :] = df
            db = db + jnp.sum(df, axis=0, keepdims=True)
        db_ref[...] = db

    return _pcall(body, name=name, in_specs=[VMEM_SPEC] * 4, out_specs=[VMEM_SPEC, VMEM_SPEC],
                  out_shape=[jax.ShapeDtypeStruct((s, 128), F32), jax.ShapeDtypeStruct((1, 128), F32)],
                  compiler_params=pltpu.CompilerParams(vmem_limit_bytes=VMEM_LIMIT_BYTES))(drow, dcol, f, b)


def _fox_specs(s):
    t = ATT_BLOCK
    ccol = pl.BlockSpec((HEADS_PER_STEP, t, 1), lambda g, i: (g, i, 0))
    crow = pl.BlockSpec((HEADS_PER_STEP, s // t, 1, t), lambda g, i: (g, 0, 0, 0))
    return ccol, crow


def _fox_fwd(p, c_col, c_row, name, after=()):
    s = p.shape[0]
    t = ATT_BLOCK

    def body(q_ref, k_ref, v_ref, cc_ref, cr_ref, o_ref, lse_ref):
        i = pl.program_id(1)
        heads = range(HEADS_PER_STEP)
        q = [q_ref[:, _head_cols(h)].astype(BF16) for h in heads]
        ct = [cc_ref[h] for h in heads]

        def tile(n, carry, diagonal):
            ks = pl.multiple_of(n * t, t)
            out = []
            for h in heads:
                acc, m, l = carry[h]
                kblk = k_ref[pl.ds(ks, t), _head_cols(h)].astype(BF16)
                vblk = v_ref[pl.ds(ks, t), _head_cols(h)].astype(BF16)
                logit = _dot_nt(q[h], kblk) * ATT_SCALE + ct[h] - cr_ref[h, n]
                if diagonal:
                    causal = lax.broadcasted_iota(jnp.int32, (t, t), 1) <= lax.broadcasted_iota(jnp.int32, (t, t), 0)
                    logit = jnp.where(causal, logit, NEG)
                m_new = jnp.maximum(m, jnp.max(logit, axis=1, keepdims=True))
                alpha = jnp.exp(m - m_new)
                pr = jnp.exp(logit - m_new)
                l = alpha * l + jnp.sum(pr, axis=1, keepdims=True)
                out.append((alpha * acc + _dot_nn(pr.astype(BF16), vblk), m_new, l))
            return tuple(out)

        init = tuple((jnp.zeros((t, HEAD), F32), jnp.full((t, 1), NEG, F32), jnp.zeros((t, 1), F32)) for _ in heads)
        carry = tile(i, lax.fori_loop(0, i, lambda n, c: tile(n, c, False), init), True)
        for h in heads:
            acc, m, l = carry[h]
            o_ref[:, _head_cols(h)] = acc / l
            lse_ref[h] = m + jnp.log(l)

    qspec, kspec, vspec = _head_specs(s, (2 * N_HEADS, 3 * N_HEADS, 4 * N_HEADS))
    ccol, crow = _fox_specs(s)
    ospec = pl.BlockSpec((t, GROUP_W), lambda g, i: (i, g))
    return _pcall(body, after=after, name=name, grid=(N_GROUPS, s // t), in_specs=[qspec, kspec, vspec, ccol, crow],
                  out_specs=[ospec, ccol],
                  out_shape=[jax.ShapeDtypeStruct((s, HALF), F32), jax.ShapeDtypeStruct((N_HEADS, s, 1), F32)],
                  compiler_params=_params("parallel", "parallel", vmem_limit=ATT_VMEM_LIMIT_BYTES))(p, p, p, c_col, c_row)


def _fox_bwd(p, c_col, c_row, lse, d_cd, d_out, name):
    s = p.shape[0]
    t = ATT_BLOCK

    def body(q_ref, k_ref, v_ref, cc_ref, cr_ref, lse_ref, do_ref, o_ref, dq_ref, dk_ref, dv_ref, dcol_ref, drow_ref):
        i = pl.program_id(1)

        @pl.when(i == 0)
        def _():
            dk_ref[...] = jnp.zeros_like(dk_ref)
            dv_ref[...] = jnp.zeros_like(dv_ref)
            dcol_ref[...] = jnp.zeros_like(dcol_ref)

        heads = range(HEADS_PER_STEP)
        q = [q_ref[:, _head_cols(h)].astype(BF16) for h in heads]
        do = [do_ref[:, _head_cols(h)].astype(BF16) for h in heads]
        delta = [jnp.sum(do_ref[:, _head_cols(h)] * o_ref[:, _head_cols(h)], axis=1, keepdims=True) for h in heads]
        ct = [cc_ref[h] for h in heads]
        lse_v = [lse_ref[h] for h in heads]
        ones = jnp.ones((t, HEAD), BF16)

        def tile(n, carry, diagonal):
            ks = pl.multiple_of(n * t, t)
            out = []
            for h in heads:
                dq, drow = carry[h]
                kblk = k_ref[pl.ds(ks, t), _head_cols(h)].astype(BF16)
                vblk = v_ref[pl.ds(ks, t), _head_cols(h)].astype(BF16)
                logit = _dot_nt(q[h], kblk) * ATT_SCALE + ct[h] - cr_ref[h, n]
                pr = jnp.exp(logit - lse_v[h])
                if diagonal:
                    causal = lax.broadcasted_iota(jnp.int32, (t, t), 1) <= lax.broadcasted_iota(jnp.int32, (t, t), 0)
                    pr = jnp.where(causal, pr, 0.0)
                ds = pr * (_dot_nt(do[h], vblk) - delta[h])
                dsb = ds.astype(BF16)
                dv_ref[pl.ds(ks, t), _head_cols(h)] += _dot_tn(pr.astype(BF16), do[h])
                dk_ref[pl.ds(ks, t), _head_cols(h)] += _dot_tn(dsb, q[h]) * ATT_SCALE
                dcol_ref[pl.ds(ks, t), _head_cols(h)] += _dot_tn(dsb, ones)
                out.append((dq + _dot_nn(dsb, kblk) * ATT_SCALE,
                            drow + jnp.sum(dsb.astype(F32), axis=1, keepdims=True)))
            return tuple(out)

        init = tuple((jnp.zeros((t, HEAD), F32), jnp.zeros((t, 1), F32)) for _ in heads)
        carry = tile(i, lax.fori_loop(0, i, lambda n, c: tile(n, c, False), init), True)
        for h in heads:
            dq_ref[:, _head_cols(h)] = carry[h][0]
            drow_ref[h] = carry[h][1]

    qspec, kspec, vspec = _head_specs(s, (2 * N_HEADS, 3 * N_HEADS, 4 * N_HEADS))
    ccol, crow = _fox_specs(s)
    dospec = pl.BlockSpec((t, GROUP_W), lambda g, i: (i, N_GROUPS + g))
    blk = pl.BlockSpec((t, GROUP_W), lambda g, i: (i, g))
    whole = pl.BlockSpec((s, GROUP_W), lambda g, i: (0, g))
    shape = jax.ShapeDtypeStruct((s, HALF), F32)
    return _pcall(body, name=name, grid=(N_GROUPS, s // t),
                  in_specs=[qspec, kspec, vspec, ccol, crow, ccol, dospec, blk],
                  out_specs=[blk, whole, whole, whole, ccol],
                  out_shape=[shape, shape, shape, shape, jax.ShapeDtypeStruct((N_HEADS, s, 1), F32)],
                  compiler_params=_params("parallel", "arbitrary", vmem_limit=ATT_VMEM_LIMIT_BYTES))(p, p, p, c_col, c_row, lse, d_cd, d_out)


def _row_tile(rows, cap):
    for t in (256, 128, 64, 32, 16, 8):
        if t <= cap and rows % t == 0:
            return t
    return rows


def _adamw_small(ws, gs, ms, vs, name):
    n = len(ws)
    c1 = 1.0 / (1.0 - ADAM_B1 ** ADAM_STEP)
    c2 = 1.0 / (1.0 - ADAM_B2 ** ADAM_STEP)

    def body(*refs):
        for k in range(n):
            w_ref, g_ref, m_ref, v_ref = (refs[j * n + k] for j in range(4))
            d_ref, nm_ref, nv_ref = (refs[(4 + j) * n + k] for j in range(3))
            gv = g_ref[...]
            nm = ADAM_B1 * m_ref[...] + (1.0 - ADAM_B1) * gv
            nv = ADAM_B2 * v_ref[...] + (1.0 - ADAM_B2) * (gv * gv)
            nm_ref[...] = nm
            nv_ref[...] = nv
            d_ref[...] = -ADAM_LR * ((nm * c1) / (jnp.sqrt(nv * c2) + ADAM_EPS) + ADAM_WD * w_ref[...])

    shapes = [jax.ShapeDtypeStruct(w.shape, F32) for w in ws] * 3
    outs = _pcall(body, name=name, in_specs=[VMEM_SPEC] * (4 * n), out_specs=[VMEM_SPEC] * (3 * n), out_shape=shapes,
                  compiler_params=pltpu.CompilerParams(vmem_limit_bytes=VMEM_LIMIT_BYTES))(*ws, *gs, *ms, *vs)
    return outs[:n], outs[n:2 * n], outs[2 * n:]


def _half_shape(whole_shape, kind):
    if kind == "col":
        return (whole_shape[0] // 2, whole_shape[1] // 4)
    if kind == "row":
        return (whole_shape[0] // 8, whole_shape[1])
    return (whole_shape[1] // 2, whole_shape[2])


def _own_half_spec(whole_shape, kind, tr):
    hr, hc = _half_shape(whole_shape, kind)
    nb = hr // tr
    if kind == "col":
        return pl.BlockSpec((tr, hc), lambda i, pos: (pos[1] * nb + i, pos[0]))
    if kind == "row":
        return pl.BlockSpec((tr, hc), lambda i, pos: ((2 * pos[0] + pos[1]) * nb + i, 0))
    return pl.BlockSpec((None, tr, hc), lambda i, pos: (pos[0], pos[1] * nb + i, 0))


def _sum_partials(pos, grad, landed, kind, name):
    hr, hc = _half_shape(grad.shape, kind)
    tr = _row_tile(hr, 64)

    def body(pos_ref, g_ref, p_ref, o_ref):
        acc = g_ref[...].astype(F32)
        for k in range(N_DEV - 1):
            acc = acc + p_ref[k].astype(F32)
        o_ref[...] = acc

    grid_spec = pltpu.PrefetchScalarGridSpec(
        num_scalar_prefetch=1, grid=(hr // tr,),
        in_specs=[_own_half_spec(grad.shape, kind, tr), pl.BlockSpec((N_DEV - 1, tr, hc), lambda i, pos: (0, i, 0))],
        out_specs=pl.BlockSpec((tr, hc), lambda i, pos: (i, 0)))
    return _pcall(body, name=name, grid_spec=grid_spec, out_shape=jax.ShapeDtypeStruct((hr, hc), F32),
                  compiler_params=_params("parallel"))(pos, grad, landed)


def _adamw_shard(pos, w, g_mine, g_sibling, m, v, name):
    hr, hc = g_mine.shape
    tr = _row_tile(hr, 128)
    nb = hr // tr
    c1 = 1.0 / (1.0 - ADAM_B1 ** ADAM_STEP)
    c2 = 1.0 / (1.0 - ADAM_B2 ** ADAM_STEP)

    def body(pos_ref, w_ref, gm_ref, gs_ref, m_ref, v_ref, g_ref, d_ref, nm_ref, nv_ref):
        mine = (pl.program_id(0) // nb) == pos_ref[1]
        gv = jnp.where(mine, gm_ref[...], gs_ref[...])
        nm = ADAM_B1 * m_ref[...] + (1.0 - ADAM_B1) * gv
        nv = ADAM_B2 * v_ref[...] + (1.0 - ADAM_B2) * (gv * gv)
        g_ref[...] = gv
        nm_ref[...] = nm
        nv_ref[...] = nv
        d_ref[...] = -ADAM_LR * ((nm * c1) / (jnp.sqrt(nv * c2) + ADAM_EPS) + ADAM_WD * w_ref[...])

    full = pl.BlockSpec((tr, hc), lambda i, pos: (i, 0))
    mine_spec = pl.BlockSpec((tr, hc), lambda i, pos: (jnp.clip(i - pos[1] * nb, 0, nb - 1), 0))
    sib_spec = pl.BlockSpec((tr, hc), lambda i, pos: (jnp.clip(i - (1 - pos[1]) * nb, 0, nb - 1), 0))
    grid_spec = pltpu.PrefetchScalarGridSpec(
        num_scalar_prefetch=1, grid=(2 * nb,), in_specs=[full, mine_spec, sib_spec, full, full], out_specs=[full] * 4)
    shape = jax.ShapeDtypeStruct((2 * hr, hc), F32)
    return _pcall(body, name=name, grid_spec=grid_spec, out_shape=[shape] * 4,
                  compiler_params=_params("parallel"))(pos, w, g_mine, g_sibling, m, v)


def _place_shard(pos, shard, kind, name, after=()):
    rows, cols = shard.shape
    tr = _row_tile(rows, 256)
    nb = rows // tr
    if kind == "col":
        out_spec = pl.BlockSpec((tr, cols), lambda i, pos: (i, pos[0]))
    elif kind == "row":
        out_spec = pl.BlockSpec((tr, cols), lambda i, pos: (pos[0] * nb + i, 0))
    else:
        out_spec = pl.BlockSpec((None, tr, cols), lambda i, pos: (pos[0], i, 0))

    def body(pos_ref, s_ref, *rest):
        rest[-1][...] = s_ref[...].astype(BF16)

    grid_spec = pltpu.PrefetchScalarGridSpec(
        num_scalar_prefetch=1, grid=(nb,),
        in_specs=[pl.BlockSpec((tr, cols), lambda i, pos: (i, 0))] + [pl.BlockSpec(memory_space=pl.ANY)] * len(after),
        out_specs=out_spec)
    return _pcall(body, name=name, grid_spec=grid_spec,
                  out_shape=jax.ShapeDtypeStruct(_whole_shape(shard.shape, kind), BF16),
                  compiler_params=_params("parallel"))(pos, shard, *after)


N_DEV = 8
RELATIONS = [(r >> 2 & 1, r >> 1 & 1, r & 1) for r in range(1, N_DEV)]


def _position():
    return lax.axis_index("x"), lax.axis_index("y"), lax.axis_index("c")


def _related(pos, rel):
    return tuple(1 - p if f else p for p, f in zip(pos, rel))


def _index(pos):
    return 4 * pos[0] + 2 * pos[1] + pos[2]


def _window(ref, kind, pos):
    px, py, pc = pos
    j = 2 * px + py
    if kind == "col":
        r, c = ref.shape
        return ref.at[pl.ds(pc * (r // 2), r // 2), pl.ds(pl.multiple_of(j * (c // 4), 128), c // 4)]
    if kind == "row":
        rj = ref.shape[0] // 4
        return ref.at[pl.ds(j * rj + pc * (rj // 2), rj // 2), :]
    r = ref.shape[1]
    return ref.at[j, pl.ds(pc * (r // 2), r // 2), :]


def _whole_shape(shard_shape, kind):
    r, c = shard_shape
    return {"col": (r, 4 * c), "row": (4 * r, c), "maj": (4, r, c)}[kind]


SEM_SPEC = pl.BlockSpec(memory_space=pltpu.SEMAPHORE)
ANY_SPEC = pl.BlockSpec(memory_space=pl.ANY)
DATAFLOW = pltpu.SideEffectType.DATAFLOW_SIDE_EFFECTING
TOKEN = jax.ShapeDtypeStruct((8, 128), F32)


def _hbm(a):
    return pltpu.with_memory_space_constraint(a, pltpu.HBM)


def _chips(x, y):
    return [(1 - x, y), (x, 1 - y), (1 - x, 1 - y)]


def _split_start(body, name, buffers, n_sems, after=()):
    n = len(buffers)

    def wrapped(*refs):
        body(refs[:n], refs[n], refs[n + 1])
        refs[-1][...] = jnp.zeros_like(refs[-1])

    outs = _pcall(
        wrapped, after=after, name=name, in_specs=[HBM_SPEC] * n,
        out_specs=[SEM_SPEC, SEM_SPEC] + [HBM_SPEC] * n + [VMEM_SPEC],
        out_shape=[pltpu.SemaphoreType.DMA(n_sems), pltpu.SemaphoreType.DMA(n_sems)]
        + [pltpu.HBM(b.shape, b.dtype) for b in buffers] + [TOKEN],
        input_output_aliases={i: 2 + i for i in range(n)},
        compiler_params=pltpu.CompilerParams(has_side_effects=DATAFLOW))(*[_hbm(b) for b in buffers])
    return outs[0], outs[1], list(outs[2:2 + n]), outs[2 + n]


def _split_wait(body, name, buffers, send_sems, recv_sems, after):
    n = len(buffers)
    after = list(after) if isinstance(after, (list, tuple)) else [after]

    def wrapped(*refs):
        body(refs[:n], refs[n], refs[n + 1])

    outs = _pcall(
        wrapped, name=name, in_specs=[HBM_SPEC] * n + [SEM_SPEC, SEM_SPEC] + [ANY_SPEC] * len(after),
        out_specs=[HBM_SPEC] * n, out_shape=[pltpu.HBM(b.shape, b.dtype) for b in buffers],
        input_output_aliases={i: i for i in range(n)},
        compiler_params=pltpu.CompilerParams(has_side_effects=DATAFLOW))(*buffers, send_sems, recv_sems, *after)
    return list(outs)


def _gather_start(wholes, kinds, name, after=()):
    def body(w_refs, send_sems, recv_sems):
        x, y, c = _position()
        for w, ref in enumerate(w_refs):
            mine = _window(ref, kinds[w], (x, y, c))
            for k, chip in enumerate(_chips(x, y)):
                pltpu.make_async_remote_copy(src_ref=mine, dst_ref=mine, send_sem=send_sems.at[3 * w + k],
                                             recv_sem=recv_sems.at[3 * w + k], device_id=(*chip, c),
                                             device_id_type=MESH).start()

    return _split_start(body, name, wholes, (3 * len(wholes),), after)


def _gather_forward(wholes, kinds, send1, recv1, after, name):
    n = len(wholes)

    def wrapped(*refs):
        w_refs, s1, r1, s2, r2 = refs[:n], refs[n], refs[n + 1], refs[n + 3], refs[n + 4]
        x, y, c = _position()
        for k, chip in enumerate(_chips(x, y)):
            for w, ref in enumerate(w_refs):
                theirs = _window(ref, kinds[w], (*chip, c))
                pltpu.make_async_remote_copy(src_ref=theirs, dst_ref=theirs, send_sem=s1.at[3 * w + k],
                                             recv_sem=r1.at[3 * w + k], device_id=(*chip, c),
                                             device_id_type=MESH).wait_recv()
                pltpu.make_async_remote_copy(src_ref=theirs, dst_ref=theirs, send_sem=s2.at[3 * w + k],
                                             recv_sem=r2.at[3 * w + k], device_id=(x, y, 1 - c),
                                             device_id_type=MESH).start()
        for w, ref in enumerate(w_refs):
            mine = _window(ref, kinds[w], (x, y, c))
            for k, chip in enumerate(_chips(x, y)):
                pltpu.make_async_remote_copy(src_ref=mine, dst_ref=mine, send_sem=s1.at[3 * w + k],
                                             recv_sem=r1.at[3 * w + k], device_id=(*chip, c),
                                             device_id_type=MESH).wait_send()
        refs[-1][...] = jnp.zeros_like(refs[-1])

    outs = _pcall(
        wrapped, name=name, in_specs=[HBM_SPEC] * n + [SEM_SPEC, SEM_SPEC, ANY_SPEC],
        out_specs=[SEM_SPEC, SEM_SPEC] + [HBM_SPEC] * n + [VMEM_SPEC],
        out_shape=[pltpu.SemaphoreType.DMA((3 * n,)), pltpu.SemaphoreType.DMA((3 * n,))]
        + [pltpu.HBM(b.shape, b.dtype) for b in wholes] + [TOKEN],
        input_output_aliases={i: 2 + i for i in range(n)},
        compiler_params=pltpu.CompilerParams(has_side_effects=DATAFLOW))(*wholes, send1, recv1, after)
    return outs[0], outs[1], list(outs[2:2 + n]), outs[2 + n]


def _gather_finish(wholes, kinds, send2, recv2, after, name):
    def body(w_refs, s2, r2):
        x, y, c = _position()
        for k, chip in enumerate(_chips(x, y)):
            for w, ref in enumerate(w_refs):
                sent = _window(ref, kinds[w], (*chip, c))
                got = _window(ref, kinds[w], (*chip, 1 - c))
                pltpu.make_async_remote_copy(src_ref=sent, dst_ref=got, send_sem=s2.at[3 * w + k],
                                             recv_sem=r2.at[3 * w + k], device_id=(x, y, 1 - c),
                                             device_id_type=MESH).wait()

    return _split_wait(body, name, wholes, send2, recv2, after)


def _gather_small(small, after=()):
    def body(s_ref, o_ref, send_sems, recv_sems, local_sem):
        x, y, c = _position()
        mine = pltpu.make_async_copy(s_ref, o_ref.at[2 * x + y], local_sem)
        mine.start()
        sends = []
        for k, chip in enumerate(_chips(x, y)):
            cp = pltpu.make_async_remote_copy(src_ref=s_ref, dst_ref=o_ref.at[2 * x + y], send_sem=send_sems.at[k],
                                              recv_sem=recv_sems.at[k], device_id=(*chip, c), device_id_type=MESH)
            cp.start()
            sends.append(cp)
        for k, chip in enumerate(_chips(x, y)):
            pltpu.make_async_remote_copy(src_ref=s_ref, dst_ref=o_ref.at[2 * chip[0] + chip[1]], send_sem=send_sems.at[k],
                                         recv_sem=recv_sems.at[k], device_id=(*chip, c), device_id_type=MESH).wait_recv()
        for cp in sends:
            cp.wait_send()
        mine.wait()

    return _pcall(body, after=after, name="gather_small", in_specs=[HBM_SPEC], out_specs=HBM_SPEC,
                  out_shape=jax.ShapeDtypeStruct((4,) + small.shape, small.dtype),
                  scratch_shapes=[pltpu.SemaphoreType.DMA((3,)), pltpu.SemaphoreType.DMA((3,)),
                                  pltpu.SemaphoreType.DMA(())])(small)


def _scatter_copies(g_refs, land_refs, kinds, send_sems, recv_sems):
    me = _position()
    copies = []
    for k, rel in enumerate(RELATIONS):
        peer = _related(me, rel)
        for w, (g_ref, land_ref) in enumerate(zip(g_refs, land_refs)):
            copies.append(pltpu.make_async_remote_copy(
                src_ref=_window(g_ref, kinds[w], peer), dst_ref=land_ref.at[k],
                send_sem=send_sems.at[7 * w + k], recv_sem=recv_sems.at[7 * w + k], device_id=peer,
                device_id_type=MESH))
    return copies


def _scatter_start(grads, kinds, name):
    n = len(grads)
    lands = [lax.empty((N_DEV - 1,) + _half_shape(g.shape, kd), g.dtype) for g, kd in zip(grads, kinds)]

    def body(refs, send_sems, recv_sems):
        for cp in _scatter_copies(refs[:n], refs[n:], kinds, send_sems, recv_sems):
            cp.start()

    send, recv, thru, token = _split_start(body, name, list(grads) + lands, ((N_DEV - 1) * n,))
    return send, recv, thru[:n], thru[n:], token


def _scatter_wait(grads, lands, kinds, send, recv, after, name):
    n = len(grads)

    def body(refs, send_sems, recv_sems):
        for cp in _scatter_copies(refs[:n], refs[n:], kinds, send_sems, recv_sems):
            cp.wait()

    out = _split_wait(body, name, list(grads) + list(lands), send, recv, after)
    return out[:n], out[n:]


def _swap_start(halves, name):
    n = len(halves)
    lands = [lax.empty(h.shape, h.dtype) for h in halves]

    def body(refs, send_sems, recv_sems):
        x, y, c = _position()
        for w in range(n):
            pltpu.make_async_remote_copy(src_ref=refs[w], dst_ref=refs[n + w], send_sem=send_sems.at[w],
                                         recv_sem=recv_sems.at[w], device_id=(x, y, 1 - c), device_id_type=MESH).start()

    send, recv, thru, token = _split_start(body, name, list(halves) + lands, (n,))
    return send, recv, thru[:n], thru[n:], token


def _swap_wait(halves, lands, send, recv, after, name):
    n = len(halves)

    def body(refs, send_sems, recv_sems):
        x, y, c = _position()
        for w in range(n):
            pltpu.make_async_remote_copy(src_ref=refs[w], dst_ref=refs[n + w], send_sem=send_sems.at[w],
                                         recv_sem=recv_sems.at[w], device_id=(x, y, 1 - c), device_id_type=MESH).wait()

    out = _split_wait(body, name, list(halves) + list(lands), send, recv, after)
    return out[:n], out[n:]


def _allreduce_small(v, after=()):
    rows = v.shape[0]

    def body(v_ref, o_ref, recv_ref, send_sems, recv_sems):
        me = _position()
        recv_ref[_index(me)] = v_ref[...]
        sends = []
        for k, rel in enumerate(RELATIONS):
            peer = _related(me, rel)
            cp = pltpu.make_async_remote_copy(
                src_ref=v_ref, dst_ref=recv_ref.at[_index(me)],
                send_sem=send_sems.at[k], recv_sem=recv_sems.at[k], device_id=peer, device_id_type=MESH)
            cp.start()
            sends.append(cp)
        for k, rel in enumerate(RELATIONS):
            peer = _related(me, rel)
            pltpu.make_async_remote_copy(
                src_ref=v_ref, dst_ref=recv_ref.at[_index(peer)],
                send_sem=send_sems.at[k], recv_sem=recv_sems.at[k], device_id=peer, device_id_type=MESH).wait_recv()
        for cp in sends:
            cp.wait_send()
        acc = recv_ref[0]
        for k in range(1, N_DEV):
            acc = acc + recv_ref[k]
        o_ref[...] = acc

    return _pcall(body, after=after, name="allreduce_small", in_specs=[VMEM_SPEC], out_specs=VMEM_SPEC,
                  out_shape=jax.ShapeDtypeStruct((rows, 128), F32),
                  scratch_shapes=[pltpu.VMEM((N_DEV, rows, 128), F32), pltpu.SemaphoreType.DMA((7,)),
                                  pltpu.SemaphoreType.DMA((7,))],
                  compiler_params=pltpu.CompilerParams(vmem_limit_bytes=VMEM_LIMIT_BYTES))(v)


def _small_copies(refs, send_sems, recv_sems):
    v_ref, land_ref = refs
    me = _position()
    return [pltpu.make_async_remote_copy(src_ref=v_ref, dst_ref=land_ref.at[_index(me)], send_sem=send_sems.at[k],
                                         recv_sem=recv_sems.at[k], device_id=_related(me, rel), device_id_type=MESH)
            for k, rel in enumerate(RELATIONS)]


def _small_wait_copies(refs, send_sems, recv_sems):
    v_ref, land_ref = refs
    me = _position()
    return [pltpu.make_async_remote_copy(src_ref=v_ref, dst_ref=land_ref.at[_index(_related(me, rel))],
                                         send_sem=send_sems.at[k], recv_sem=recv_sems.at[k],
                                         device_id=_related(me, rel), device_id_type=MESH)
            for k, rel in enumerate(RELATIONS)]


def _small_reduce_start(v, name, after):
    def body(refs, send_sems, recv_sems):
        for cp in _small_copies(refs, send_sems, recv_sems):
            cp.start()

    send, recv, thru, token = _split_start(body, name, [v, lax.empty((N_DEV,) + v.shape, v.dtype)], (N_DEV - 1,), after)
    return send, recv, thru, token


def _small_reduce_finish(me, send, recv, thru, after, name):
    def body(refs, send_sems, recv_sems):
        for cp in _small_wait_copies(refs, send_sems, recv_sems):
            cp.wait()

    v, landed = _split_wait(body, name + "_wait", thru, send, recv, after)

    def add(me_ref, v_ref, land_ref, o_ref):
        acc = jnp.where(me_ref[0] == 0, v_ref[...], land_ref[0])
        for d in range(1, N_DEV):
            acc = acc + jnp.where(me_ref[0] == d, v_ref[...], land_ref[d])
        o_ref[...] = acc

    return _pcall(add, name=name + "_sum",
                  in_specs=[pl.BlockSpec(memory_space=pltpu.SMEM), VMEM_SPEC, VMEM_SPEC], out_specs=VMEM_SPEC,
                  out_shape=jax.ShapeDtypeStruct(v.shape, F32),
                  compiler_params=pltpu.CompilerParams(vmem_limit_bytes=VMEM_LIMIT_BYTES))(me, v, landed)


def _pack(arrays):
    flat = []
    for a in arrays:
        a = a.reshape(-1)
        flat.append(jnp.pad(a, (0, -a.shape[0] % 128)))
    flat = jnp.concatenate(flat)
    flat = jnp.pad(flat, (0, -flat.shape[0] % 1024))
    return flat.reshape(-1, 128)


def _unpack(packed, shapes):
    flat = packed.reshape(-1)
    out, at = [], 0
    for shp in shapes:
        size = 1
        for d in shp:
            size *= d
        out.append(flat[at:at + size].reshape(shp))
        at += size + (-size % 128)
    return out


WEIGHTS = ['l0_mix_norm_g', 'l0_w_in', 'l0_sc_conv_w', 'l0_w_out', 'l0_ffn_norm_g', 'l0_ffn_up', 'l0_ffn_conv_w',
           'l0_ffn_down', 'l1_mix_norm_g', 'l1_w_in', 'l1_fox_b_f', 'l1_sg_w', 'l1_sg_b', 'l1_sg_norm_g', 'l1_w_out',
           'l1_ffn_norm_g', 'l1_ffn_up', 'l1_ffn_conv_w', 'l1_ffn_down', 'final_norm_g']
BIG = {'l0_w_in': 'col', 'l0_w_out': 'row', 'l0_ffn_up': 'col', 'l0_ffn_down': 'row',
       'l1_w_in': 'maj', 'l1_w_out': 'row', 'l1_ffn_up': 'col', 'l1_ffn_down': 'row'}
GATHER_GROUPS = [['l0_w_in'], ['l0_w_out'], ['l0_ffn_up'], ['l0_ffn_down'], ['l1_w_in', 'l1_w_out'],
                 ['l1_ffn_up'], ['l1_ffn_down']]
CONV = ['l0_sc_conv_w', 'l0_ffn_conv_w', 'l1_ffn_conv_w']
SMALL = [n for n in WEIGHTS if n not in BIG]
LATE_SMALL = ['l0_sc_conv_w', 'l0_mix_norm_g']
IN_CD = 5 * HALF + N_HEADS


def _ffn_forward(x, g, get_up, behind_act, get_down, conv_w, tag):
    h = _rmsnorm_fwd(x, g, tag + "_norm")
    u = _matmul(h, get_up(h), "nn", F32, tag + "_up")
    f = _ffn_act_fwd(u, conv_w, tag + "_act", after=behind_act(u))
    w_down, tokens = get_down(f)
    return _matmul(f, w_down, "nn", F32, tag + "_down", res=x, after=tokens), (h, u, f)


def _ffn_backward(x, g, w_up, conv_w, w_down, saved, d_out, send_up, send_down, tag):
    h, u, f = saved
    dw_down = _matmul(f, d_out, "tn", BF16, tag + "_dwdown")
    d_f = _matmul(d_out, w_down, "nt", F32, tag + "_df", after=[send_down(dw_down)])
    du, dcw_gate, dcw_up = _ffn_act_bwd(u, conv_w, d_f, tag + "_dact")
    dw_up = _matmul(h, du, "tn", BF16, tag + "_dwup")
    dh = _matmul(du, w_up, "nt", F32, tag + "_dh", after=[send_up(dw_up)])
    dx, dg = _rmsnorm_bwd(x, g, dh, d_out, tag + "_dnorm")
    return dx, dg, jnp.concatenate([dcw_gate, dcw_up], axis=1)


def kernel(x, l0_mix_norm_g, l0_w_in, l0_sc_conv_w, l0_w_out, l0_ffn_norm_g, l0_ffn_up, l0_ffn_conv_w, l0_ffn_down, l1_mix_norm_g, l1_w_in, l1_fox_b_f, l1_sg_w, l1_sg_b, l1_sg_norm_g, l1_w_out, l1_ffn_norm_g, l1_ffn_up, l1_ffn_conv_w, l1_ffn_down, final_norm_g, loss_target, m_l0_mix_norm_g, m_l0_w_in, m_l0_sc_conv_w, m_l0_w_out, m_l0_ffn_norm_g, m_l0_ffn_up, m_l0_ffn_conv_w, m_l0_ffn_down, m_l1_mix_norm_g, m_l1_w_in, m_l1_fox_b_f, m_l1_sg_w, m_l1_sg_b, m_l1_sg_norm_g, m_l1_w_out, m_l1_ffn_norm_g, m_l1_ffn_up, m_l1_ffn_conv_w, m_l1_ffn_down, m_final_norm_g, v_l0_mix_norm_g, v_l0_w_in, v_l0_sc_conv_w, v_l0_w_out, v_l0_ffn_norm_g, v_l0_ffn_up, v_l0_ffn_conv_w, v_l0_ffn_down, v_l1_mix_norm_g, v_l1_w_in, v_l1_fox_b_f, v_l1_sg_w, v_l1_sg_b, v_l1_sg_norm_g, v_l1_w_out, v_l1_ffn_norm_g, v_l1_ffn_up, v_l1_ffn_conv_w, v_l1_ffn_down, v_final_norm_g):
    given = (l0_mix_norm_g, l0_w_in, l0_sc_conv_w, l0_w_out, l0_ffn_norm_g, l0_ffn_up, l0_ffn_conv_w, l0_ffn_down, l1_mix_norm_g, l1_w_in, l1_fox_b_f, l1_sg_w, l1_sg_b, l1_sg_norm_g, l1_w_out, l1_ffn_norm_g, l1_ffn_up, l1_ffn_conv_w, l1_ffn_down, final_norm_g)
    given_m = (m_l0_mix_norm_g, m_l0_w_in, m_l0_sc_conv_w, m_l0_w_out, m_l0_ffn_norm_g, m_l0_ffn_up, m_l0_ffn_conv_w, m_l0_ffn_down, m_l1_mix_norm_g, m_l1_w_in, m_l1_fox_b_f, m_l1_sg_w, m_l1_sg_b, m_l1_sg_norm_g, m_l1_w_out, m_l1_ffn_norm_g, m_l1_ffn_up, m_l1_ffn_conv_w, m_l1_ffn_down, m_final_norm_g)
    given_v = (v_l0_mix_norm_g, v_l0_w_in, v_l0_sc_conv_w, v_l0_w_out, v_l0_ffn_norm_g, v_l0_ffn_up, v_l0_ffn_conv_w, v_l0_ffn_down, v_l1_mix_norm_g, v_l1_w_in, v_l1_fox_b_f, v_l1_sg_w, v_l1_sg_b, v_l1_sg_norm_g, v_l1_w_out, v_l1_ffn_norm_g, v_l1_ffn_up, v_l1_ffn_conv_w, v_l1_ffn_down, v_final_norm_g)
    wt = dict(zip(WEIGHTS, given))
    mom = dict(zip(WEIGHTS, given_m))
    var = dict(zip(WEIGHTS, given_v))
    s = x.shape[1]
    t = ATT_BLOCK
    x0, target = x[0], loss_target[0]
    chip = 2 * lax.axis_index("x") + lax.axis_index("y")

    pos = jnp.stack([chip, lax.axis_index("c")]).astype(jnp.int32)

    conv_widths = [wt[n].shape[1] for n in CONV]
    conv_all = _gather_small(jnp.concatenate([wt[n] for n in CONV], axis=1))
    conv_full, at = {}, 0
    for n, cw in zip(CONV, conv_widths):
        conv_full[n] = jnp.transpose(conv_all[:, :, at:at + cw], (1, 0, 2)).reshape(3, 4 * cw)
        at += cw
    gathers, token = [], conv_all
    for gi, names in enumerate(GATHER_GROUPS):
        placed = [_place_shard(pos, wt[n], BIG[n], "place_" + n, [token]) for n in names]
        send, recv, thru, token = _gather_start(placed, [BIG[n] for n in names], "gather_start_%d" % gi, [token])
        gathers.append((send, recv, thru))
    full = {}

    def forward_gather(gi, after):
        send, recv, thru = gathers[gi]
        kinds = [BIG[n] for n in GATHER_GROUPS[gi]]
        gathers[gi] = _gather_forward(thru, kinds, send, recv, after, "gather_forward_%d" % gi)
        return gathers[gi][3]

    def finish_gather(gi, after):
        send, recv, thru, tok = gathers[gi]
        names = GATHER_GROUPS[gi]
        wholes = _gather_finish(thru, [BIG[n] for n in names], send, recv, tok if after is None else after,
                                "gather_finish_%d" % gi)
        full.update(zip(names, wholes))

    def vec(name):
        return wt[name].reshape(1, -1)

    h0 = _rmsnorm_fwd(x0, vec('l0_mix_norm_g'), "l0_mix_norm", after=[token])
    forward_gather(0, h0)
    finish_gather(0, None)
    p0 = _matmul(h0, full['l0_w_in'], "nn", F32, "l0_in")
    a_out, sb_carries = _sb_fwd(p0, "l0_sb", after=[forward_gather(1, p0)])
    finish_gather(1, a_out)
    b_out = _sc_fwd(p0, conv_full['l0_sc_conv_w'], "l0_sc")
    ab0 = jnp.concatenate([a_out.astype(BF16), b_out], axis=1)
    x1 = _matmul(ab0, full['l0_w_out'], "nn", F32, "l0_out", res=x0, after=[forward_gather(2, b_out)])

    def ffn_weights(up_group, next_group):
        def get_up(h):
            finish_gather(up_group, h)
            return full[GATHER_GROUPS[up_group][0]]

        def behind_act(u):
            return [forward_gather(up_group + 1, u)]

        def get_down(f):
            finish_gather(up_group + 1, f)
            return full[GATHER_GROUPS[up_group + 1][0]], ([forward_gather(next_group, f)] if next_group else ())

        return get_up, behind_act, get_down

    x2, ffn0_saved = _ffn_forward(x1, vec('l0_ffn_norm_g'), *ffn_weights(2, 4), conv_full['l0_ffn_conv_w'], "l0_ffn")
    h2 = _rmsnorm_fwd(x2, vec('l1_mix_norm_g'), "l1_mix_norm")
    finish_gather(4, h2)
    w_in1 = jnp.transpose(full['l1_w_in'], (1, 0, 2)).reshape(D_MODEL, IN_CD)
    w_in1_main = w_in1[:, :5 * HALF]
    w_in1_f = jnp.pad(w_in1[:, 5 * HALF:], ((0, 0), (0, 128 - N_HEADS)))
    p1 = _matmul(h2, w_in1_main, "nn", F32, "l1_in")
    f_logit = _matmul(h2, w_in1_f, "nn", F32, "l1_in_f")
    b_f = jnp.pad(wt['l1_fox_b_f'], (0, 128 - N_HEADS)).reshape(1, 128)
    c_heads = _fox_prep(f_logit, b_f, "l1_fox_prep")[:, :N_HEADS].T
    c_col = c_heads[:, :, None]
    c_row = c_heads.reshape(N_HEADS, s // t, 1, t)
    sg_bias = jnp.repeat(wt['l1_sg_b'].T, HEAD, axis=1)
    sg_gain = vec('l1_sg_norm_g')
    c_out = _sg_fwd(p1, wt['l1_sg_w'], sg_bias, sg_gain, "l1_sg")
    d_out, lse = _fox_fwd(p1, c_col, c_row, "l1_fox", after=[forward_gather(5, c_out)])
    cd1 = jnp.concatenate([c_out, d_out.astype(BF16)], axis=1)
    x3 = _matmul(cd1, full['l1_w_out'], "nn", F32, "l1_out", res=x2)
    x4, ffn1_saved = _ffn_forward(x3, vec('l1_ffn_norm_g'), *ffn_weights(5, None), conv_full['l1_ffn_conv_w'], "l1_ffn")
    dx4, dg_final, loss_part = _loss_head(x4, vec('final_norm_g'), target, "loss_head")

    grads = {'final_norm_g': dg_final}
    scatters = []

    def send_grads(names):
        def start(*group):
            send, recv, thru, lands, tok = _scatter_start(list(group), [BIG[n] for n in names],
                                                          "scatter_start_%d" % len(scatters))
            scatters.append((names, send, recv, thru, lands))
            return tok
        return start

    dx3, grads['l1_ffn_norm_g'], grads['l1_ffn_conv_w'] = _ffn_backward(
        x3, vec('l1_ffn_norm_g'), full['l1_ffn_up'], conv_full['l1_ffn_conv_w'], full['l1_ffn_down'], ffn1_saved, dx4,
        send_grads(['l1_ffn_up']), send_grads(['l1_ffn_down']), "l1_ffn")
    dw_out1 = _matmul(cd1, dx3, "tn", BF16, "l1_dwout")
    d_cd = _matmul(dx3, full['l1_w_out'], "nt", F32, "l1_dcd")
    du, dv, grads['l1_sg_w'], db_sg, grads['l1_sg_norm_g'] = _sg_bwd(p1, wt['l1_sg_w'], sg_bias, sg_gain, d_cd, "l1_dsg")
    grads['l1_sg_b'] = db_sg[:, :N_HEADS].T
    dq, dk, dvv, dcol, drow = _fox_bwd(p1, c_col, c_row, lse, d_cd, d_out, "l1_dfox")
    d_f_logit, d_b_f = _fox_post(drow, dcol, f_logit, b_f, "l1_fox_post")
    grads['l1_fox_b_f'] = d_b_f[0, :N_HEADS]
    dp1 = jnp.concatenate([a.astype(BF16) for a in (du, dv, dq, dk, dvv)], axis=1)
    dw_main = _matmul(h2, dp1, "tn", BF16, "l1_dwin")
    dw_f = _matmul(h2, d_f_logit, "tn", BF16, "l1_dwin_f")
    dw_in1 = jnp.concatenate([dw_main, dw_f[:, :N_HEADS]], axis=1)
    dw_in1 = jnp.transpose(dw_in1.reshape(D_MODEL, 4, IN_CD // 4), (1, 0, 2))
    dh2 = _matmul(dp1, w_in1_main, "nt", F32, "l1_dh", after=[send_grads(['l1_w_out', 'l1_w_in'])(dw_out1, dw_in1)])
    dh2 = _matmul(d_f_logit, w_in1_f, "nt", F32, "l1_dh_f", res=dh2)
    dx2, grads['l1_mix_norm_g'] = _rmsnorm_bwd(x2, vec('l1_mix_norm_g'), dh2, dx3, "l1_dmix_norm")
    dx1, grads['l0_ffn_norm_g'], grads['l0_ffn_conv_w'] = _ffn_backward(
        x1, vec('l0_ffn_norm_g'), full['l0_ffn_up'], conv_full['l0_ffn_conv_w'], full['l0_ffn_down'], ffn0_saved, dx2,
        send_grads(['l0_ffn_up']), send_grads(['l0_ffn_down']), "l0_ffn")
    early_names = [n for n in SMALL if n not in LATE_SMALL]
    early = _small_reduce_start(_pack([grads[n] for n in early_names] + [loss_part]), "small_start", [dx1])
    dw_out0 = _matmul(ab0, dx1, "tn", BF16, "l0_dwout", after=[early[3]])
    d_ab = _matmul(dx1, full['l0_w_out'], "nt", F32, "l0_dab", after=[send_grads(['l0_w_out'])(dw_out0)])
    dq0, dk0, dv0 = _sb_bwd(p0, d_ab, sb_carries, "l0_dsb")
    dgb, dgc, dhin, grads['l0_sc_conv_w'] = _sc_bwd(p0, conv_full['l0_sc_conv_w'], d_ab, "l0_dsc")
    dp0 = jnp.concatenate([a.astype(BF16) for a in (dq0, dk0, dv0, dgb, dgc, dhin)], axis=1)
    dw_in0 = _matmul(h0, dp0, "tn", BF16, "l0_dwin")
    dh0 = _matmul(dp0, full['l0_w_in'], "nt", F32, "l0_dh", after=[send_grads(['l0_w_in'])(dw_in0)])
    dx0, grads['l0_mix_norm_g'] = _rmsnorm_bwd(x0, vec('l0_mix_norm_g'), dh0, dx1, "l0_dmix_norm")

    shard_grads, delta, new_m, new_v, swaps = {}, {}, {}, {}, {}

    def reduce_group(gi, after):
        names, send, recv, thru, lands = scatters[gi]
        kinds = [BIG[n] for n in names]
        g_thru, landed = _scatter_wait(thru, lands, kinds, send, recv, after, "scatter_wait_%d" % gi)
        halves = [_sum_partials(pos, g, ld, kd, "sum_" + n) for n, g, ld, kd in zip(names, g_thru, landed, kinds)]
        s_send, s_recv, h_thru, s_lands, tok = _swap_start(halves, "swap_start_%d" % gi)
        swaps[gi] = (names, s_send, s_recv, h_thru, s_lands)
        return tok

    def update_group(gi, after):
        names, s_send, s_recv, h_thru, s_lands = swaps[gi]
        mine, theirs = _swap_wait(h_thru, s_lands, s_send, s_recv, after, "swap_wait_%d" % gi)
        for n, gm, gs in zip(names, mine, theirs):
            shard_grads[n], delta[n], new_m[n], new_v[n] = _adamw_shard(pos, wt[n], gm, gs, mom[n], var[n], "adamw_" + n)
        return [delta[n] for n in names]

    after = reduce_group(2, reduce_group(1, reduce_group(0, dx0)))
    after = update_group(2, update_group(1, update_group(0, after)))
    after = reduce_group(5, reduce_group(4, reduce_group(3, after)))
    after = update_group(5, update_group(4, update_group(3, after)))
    after = reduce_group(6, after)
    def small_shapes(names):
        return [conv_full[n].shape if n in CONV else wt[n].shape for n in names]

    me = (2 * chip + lax.axis_index("c")).astype(jnp.int32).reshape(1)
    early_all = _small_reduce_finish(me, early[0], early[1], early[2], after, "small_early")
    early_sums = _unpack(early_all, small_shapes(early_names) + [loss_part.shape])
    loss = early_sums[-1][0, 0]
    late_all = _allreduce_small(_pack([grads[n] for n in LATE_SMALL]), [early_all])
    small_sums = dict(zip(early_names + LATE_SMALL, early_sums[:-1] + _unpack(late_all, small_shapes(LATE_SMALL))))
    for n in SMALL:
        g = small_sums[n]
        shard_grads[n] = lax.dynamic_slice_in_dim(g, chip * wt[n].shape[1], wt[n].shape[1], axis=1) if n in CONV else g
    update_group(6, late_all)
    def flat2d(a):
        return a.reshape(-1, a.shape[-1])

    small_out = _adamw_small(*[[flat2d(src[n]) for n in SMALL] for src in (wt, shard_grads, mom, var)], "adamw_small")
    for out, arrays in zip((delta, new_m, new_v), small_out):
        out.update((n, a.reshape(wt[n].shape)) for n, a in zip(SMALL, arrays))

    return (loss, dx0[None], *[shard_grads[n] for n in WEIGHTS], *[delta[n] for n in WEIGHTS],
            *[new_m[n] for n in WEIGHTS], *[new_v[n] for n in WEIGHTS])
```

```python
import jax
import jax.numpy as jnp
from jax import lax
from jax.experimental import pallas as pl
from jax.experimental.pallas import tpu as pltpu

F32 = jnp.float32
BF16 = jnp.bfloat16

D_MODEL = 2048
HEAD = 128
N_HEADS = 8
HALF = N_HEADS * HEAD
D_FF = 5632
EPS = 1e-6
ATT_SCALE = HEAD ** -0.5
ATT_BLOCK = 512
NEG = -1e30

ADAM_LR = 0.001
ADAM_B1 = 0.9
ADAM_B2 = 0.999
ADAM_EPS = 1e-08
ADAM_WD = 0.01
ADAM_STEP = 10

VMEM_LIMIT_BYTES = 48 * 1024 * 1024
MM_VMEM_LIMIT_BYTES = 56 * 1024 * 1024
ATT_VMEM_LIMIT_BYTES = 48 * 1024 * 1024
MESH = pl.DeviceIdType.MESH
HBM_SPEC = pl.BlockSpec(memory_space=pltpu.HBM)
VMEM_SPEC = pl.BlockSpec(memory_space=pltpu.VMEM)


def _pcall(body, after=(), **kw):
    if not after:
        return pl.pallas_call(body, **kw)
    n_in, n_after, inner = len(kw["in_specs"]), len(after), body
    kw["in_specs"] = list(kw["in_specs"]) + [pl.BlockSpec(memory_space=pl.ANY)] * n_after

    def body(*refs):
        inner(*refs[:n_in], *refs[n_in + n_after:])

    call = pl.pallas_call(body, **kw)
    return lambda *args: call(*args, *after)


def _params(*semantics, vmem_limit=VMEM_LIMIT_BYTES):
    return pltpu.CompilerParams(dimension_semantics=semantics, vmem_limit_bytes=vmem_limit)


def _pick(n, cap):
    best = None
    for t in range(128, min(n, cap) + 1, 128):
        if n % t == 0:
            best = t
    return n if best is None else best


def _dot(a, b, dims):
    return lax.dot_general(a, b, (dims, ((), ())), preferred_element_type=F32)


def _dot_nn(a, b):
    return _dot(a, b, ((1,), (0,)))


def _dot_nt(a, b):
    return _dot(a, b, ((1,), (1,)))


def _dot_tn(a, b):
    return _dot(a, b, ((0,), (0,)))


def _split3(x):
    hi = x.astype(BF16)
    r1 = x - hi.astype(F32)
    mid = r1.astype(BF16)
    lo = (r1 - mid.astype(F32)).astype(BF16)
    return hi, mid, lo


def _log_sigmoid(z):
    return jnp.minimum(z, 0.0) - jnp.log1p(jnp.exp(-jnp.abs(z)))


_GELU_K = 0.7978845608028654


def _gelu(x):
    return 0.5 * x * (1.0 + jnp.tanh(_GELU_K * (x + 0.044715 * x * x * x)))


def _gelu_grad(x):
    t = jnp.tanh(_GELU_K * (x + 0.044715 * x * x * x))
    return 0.5 * (1.0 + t) + 0.5 * x * (1.0 - t * t) * _GELU_K * (1.0 + 3.0 * 0.044715 * x * x)


SUBLANES = 8


def _shift_down(x, k):
    rolled = pltpu.roll(x, k, axis=0)
    head = rolled[:SUBLANES]
    head = jnp.where(lax.broadcasted_iota(jnp.int32, head.shape, 0) >= k, head, 0.0)
    return jnp.concatenate([head, rolled[SUBLANES:]], axis=0)


def _shift_up(x, k):
    n = x.shape[0]
    rolled = pltpu.roll(x, n - k, axis=0)
    tail = rolled[n - SUBLANES:]
    tail = jnp.where(lax.broadcasted_iota(jnp.int32, tail.shape, 0) < SUBLANES - k, tail, 0.0)
    return jnp.concatenate([rolled[:n - SUBLANES], tail], axis=0)


def _conv3(s, w, shifted=None):
    s1, s2 = shifted if shifted else (_shift_down(s, 1), _shift_down(s, 2))
    return w[0:1, :] * s2 + w[1:2, :] * s1 + w[2:3, :] * s


def _conv3_transpose(d, w):
    return w[2:3, :] * d + w[1:2, :] * _shift_up(d, 1) + w[0:1, :] * _shift_up(d, 2)


def _conv3_wgrad(d, s, shifted, dw_ref):
    s1, s2 = shifted
    dw_ref[0:1, :] = jnp.sum(d * s2, axis=0, keepdims=True)
    dw_ref[1:2, :] = jnp.sum(d * s1, axis=0, keepdims=True)
    dw_ref[2:3, :] = jnp.sum(d * s, axis=0, keepdims=True)


MM_TILE_M, MM_TILE_N, MM_TILE_K = 1408, 512, 5632


def _matmul(a, b, mode, out_dtype, name, res=None, after=()):
    a_parts = a.shape[0] if a.ndim == 3 else 1
    b_parts = b.shape[0] if b.ndim == 3 else 1
    a_shape = (a.shape[1], a_parts * a.shape[2]) if a.ndim == 3 else a.shape
    b_shape = (b.shape[1], b_parts * b.shape[2]) if b.ndim == 3 else b.shape
    assert (a_parts == 1 or mode != "tn") and (b_parts == 1 or mode == "tn")
    if mode == "nn":
        (m, k), (k2, n) = a_shape, b_shape
    elif mode == "nt":
        (m, k), (n, k2) = a_shape, b_shape
    else:
        (k, m), (k2, n) = a_shape, b_shape
    assert k == k2, (a.shape, b.shape, mode)
    tm, tn, tk = _pick(m, MM_TILE_M), _pick(n // b_parts, MM_TILE_N), _pick(k // a_parts, MM_TILE_K)
    nk = k // tk
    if mode == "tn":
        a_spec = pl.BlockSpec((tk, tm), lambda i, j, kk: (kk, i))
    elif a_parts > 1:
        per = nk // a_parts
        a_spec = pl.BlockSpec((None, tm, tk), lambda i, j, kk: (kk // per, i, kk % per))
    else:
        a_spec = pl.BlockSpec((tm, tk), lambda i, j, kk: (i, kk))
    if mode == "nt":
        b_spec = pl.BlockSpec((tn, tk), lambda i, j, kk: (j, kk))
    elif b_parts > 1:
        per = n // b_parts // tn
        b_spec = pl.BlockSpec((None, tk, tn), lambda i, j, kk: (j // per, kk, j % per))
    else:
        b_spec = pl.BlockSpec((tk, tn), lambda i, j, kk: (kk, j))
    o_spec = pl.BlockSpec((tm, tn), lambda i, j, kk: (i, j))
    dims = {"nn": ((1,), (0,)), "nt": ((1,), (1,)), "tn": ((0,), (0,))}[mode]
    has_res = res is not None

    def body(*refs):
        a_ref, b_ref = refs[0], refs[1]
        r_ref = refs[2] if has_res else None
        o_ref = refs[3] if has_res else refs[2]
        part = _dot(a_ref[...].astype(BF16), b_ref[...].astype(BF16), dims)

        def finish(total):
            if has_res:
                total = total + r_ref[...]
            o_ref[...] = total.astype(out_dtype)

        if nk == 1:
            finish(part)
        else:
            acc_ref = refs[-1]
            kk = pl.program_id(2)

            @pl.when(kk == 0)
            def _():
                acc_ref[...] = part

            @pl.when(kk > 0)
            def _():
                acc_ref[...] += part

            @pl.when(kk == nk - 1)
            def _():
                finish(acc_ref[...])

    in_specs = [a_spec, b_spec] + ([o_spec] if has_res else [])
    args = (a, b) + ((res,) if has_res else ())
    return _pcall(
        body, after=after, name=name, grid=(m // tm, n // tn, nk),
        in_specs=in_specs, out_specs=o_spec,
        out_shape=jax.ShapeDtypeStruct((m, n), out_dtype),
        scratch_shapes=[pltpu.VMEM((tm, tn), F32)] if nk > 1 else [],
        compiler_params=_params("parallel", "parallel", "arbitrary", vmem_limit=MM_VMEM_LIMIT_BYTES),
    )(*args)


ROW_TILE = 256


def _rmsnorm_fwd(x, g, name, after=()):
    s, d = x.shape

    def body(x_ref, g_ref, o_ref):
        xf = x_ref[...]
        r = lax.rsqrt(jnp.mean(xf * xf, axis=-1, keepdims=True) + EPS)
        o_ref[...] = (xf * r * g_ref[...]).astype(BF16)

    row = pl.BlockSpec((ROW_TILE, d), lambda i: (i, 0))
    vec = pl.BlockSpec((1, d), lambda i: (0, 0))
    return _pcall(body, after=after, name=name, grid=(s // ROW_TILE,), in_specs=[row, vec], out_specs=row,
                  out_shape=jax.ShapeDtypeStruct((s, d), BF16), compiler_params=_params("parallel"))(x, g)


def _rmsnorm_bwd(x, g, dh, dres, name):
    s, d = x.shape

    def body(x_ref, g_ref, dh_ref, dres_ref, dx_ref, dg_ref):
        xf = x_ref[...]
        r = lax.rsqrt(jnp.mean(xf * xf, axis=-1, keepdims=True) + EPS)
        xhat = xf * r
        dh_v = dh_ref[...]
        dxh = dh_v * g_ref[...]
        proj = jnp.mean(dxh * xhat, axis=-1, keepdims=True)
        dx_ref[...] = dres_ref[...] + r * (dxh - xhat * proj)
        part = jnp.sum(dh_v * xhat, axis=0, keepdims=True)

        @pl.when(pl.program_id(0) == 0)
        def _():
            dg_ref[...] = part

        @pl.when(pl.program_id(0) > 0)
        def _():
            dg_ref[...] += part

    row = pl.BlockSpec((ROW_TILE, d), lambda i: (i, 0))
    vec = pl.BlockSpec((1, d), lambda i: (0, 0))
    return _pcall(body, name=name, grid=(s // ROW_TILE,), in_specs=[row, vec, row, row], out_specs=[row, vec],
                  out_shape=[jax.ShapeDtypeStruct((s, d), F32), jax.ShapeDtypeStruct((1, d), F32)],
                  compiler_params=_params("arbitrary"))(x, g, dh, dres)


def _loss_head(x, g, target, name):
    s, d = x.shape

    def body(x_ref, g_ref, t_ref, dx_ref, dg_ref, loss_ref):
        xf = x_ref[...]
        r = lax.rsqrt(jnp.mean(xf * xf, axis=-1, keepdims=True) + EPS)
        xhat = xf * r
        gv = g_ref[...]
        err = xhat * gv - t_ref[...]
        dy = err * (1.0 / d)
        dxh = dy * gv
        proj = jnp.mean(dxh * xhat, axis=-1, keepdims=True)
        dx_ref[...] = r * (dxh - xhat * proj)
        dg_part = jnp.sum(dy * xhat, axis=0, keepdims=True)
        row_loss = jnp.sum(err * err, axis=-1, keepdims=True) * (0.5 / d)
        loss_part = jnp.broadcast_to(jnp.sum(row_loss, axis=0, keepdims=True), (1, 128))

        @pl.when(pl.program_id(0) == 0)
        def _():
            dg_ref[...] = dg_part
            loss_ref[...] = loss_part

        @pl.when(pl.program_id(0) > 0)
        def _():
            dg_ref[...] += dg_part
            loss_ref[...] += loss_part

    row = pl.BlockSpec((ROW_TILE, d), lambda i: (i, 0))
    vec = pl.BlockSpec((1, d), lambda i: (0, 0))
    one = pl.BlockSpec((1, 128), lambda i: (0, 0))
    return _pcall(body, name=name, grid=(s // ROW_TILE,), in_specs=[row, vec, row], out_specs=[row, vec, one],
                  out_shape=[jax.ShapeDtypeStruct((s, d), F32), jax.ShapeDtypeStruct((1, d), F32),
                             jax.ShapeDtypeStruct((1, 128), F32)],
                  compiler_params=_params("arbitrary"))(x, g, target)


HEADS_PER_STEP = 2
GROUP_W = HEADS_PER_STEP * HEAD
N_GROUPS = N_HEADS // HEADS_PER_STEP


def _head_cols(h):
    return slice(h * HEAD, (h + 1) * HEAD)


def _head_specs(s, col0):
    t = ATT_BLOCK
    g0 = [c // HEADS_PER_STEP for c in col0]
    qspec = pl.BlockSpec((t, GROUP_W), lambda g, i: (i, g0[0] + g))
    kspec = pl.BlockSpec((s, GROUP_W), lambda g, i: (0, g0[1] + g))
    vspec = pl.BlockSpec((s, GROUP_W), lambda g, i: (0, g0[2] + g))
    return qspec, kspec, vspec


TRI = 256


def _order_matrix(later):
    r, c = lax.broadcasted_iota(jnp.int32, (TRI, TRI), 0), lax.broadcasted_iota(jnp.int32, (TRI, TRI), 1)
    return (r > c if later else r < c).astype(BF16)


def _exact_dot(x, m, later):
    parts = [x[:, c:c + TRI] for c in range(0, x.shape[1], TRI)]
    totals = [jnp.sum(p, axis=1, keepdims=True) for p in parts] if len(parts) > 1 else None
    out = []
    for j, p in enumerate(parts):
        hi = p.astype(BF16)
        lo = (p - hi.astype(F32)).astype(BF16)
        acc = _dot_nn(hi, m) + _dot_nn(lo, m)
        for other in (range(j + 1, len(parts)) if later else range(j)):
            acc = acc + totals[other]
        out.append(acc)
    return out[0] if len(out) == 1 else jnp.concatenate(out, axis=1)


def _sb_block(q, kblk, carry_l, u, diagonal):
    t = ATT_BLOCK
    z = _dot_nt(q, kblk) * ATT_SCALE
    sp = jnp.maximum(z, 0.0) + jnp.log(1.0 + jnp.exp(-jnp.abs(z)))
    if not diagonal:
        l = -sp
        return z, None, l, jnp.exp(z + l + _exact_dot(l, u, True) + carry_l)
    mask = lax.broadcasted_iota(jnp.int32, (t, t), 1) < lax.broadcasted_iota(jnp.int32, (t, t), 0)
    l = jnp.where(mask, -sp, 0.0)
    a = jnp.where(mask, jnp.exp(z - sp + _exact_dot(l, u, True) + carry_l), 0.0)
    return z, mask, l, a


def _sb_carry_spec(s):
    t = ATT_BLOCK
    return pl.BlockSpec((HEADS_PER_STEP, None, s // t, t, 1), lambda g, i: (g, i, 0, 0, 0))


def _sb_fwd(p, name, after=()):
    s = p.shape[0]
    t = ATT_BLOCK
    nb = s // t

    def body(q_ref, k_ref, v_ref, o_ref, cl_ref):
        i = pl.program_id(1)
        heads = range(HEADS_PER_STEP)
        q = [q_ref[:, _head_cols(h)].astype(BF16) for h in heads]
        u = _order_matrix(True)
        cl_ref[...] = jnp.zeros_like(cl_ref)

        def tile(kb, carry, diagonal):
            ks = pl.multiple_of(kb * t, t)
            out = []
            for h in heads:
                acc, carry_l = carry[h]
                kblk = k_ref[pl.ds(ks, t), _head_cols(h)].astype(BF16)
                vblk = v_ref[pl.ds(ks, t), _head_cols(h)].astype(BF16)
                cl_ref[h, kb] = carry_l
                _, _, l, a = _sb_block(q[h], kblk, carry_l, u, diagonal)
                out.append((acc + _dot_nn(a.astype(BF16), vblk), carry_l + jnp.sum(l, axis=1, keepdims=True)))
            return tuple(out)

        carry = tile(i, tuple((jnp.zeros((t, HEAD), F32), jnp.zeros((t, 1), F32)) for _ in heads), True)
        carry = lax.fori_loop(0, i, lambda n, c: tile(i - 1 - n, c, False), carry)
        for h in heads:
            o_ref[:, _head_cols(h)] = carry[h][0]

    qspec, kspec, vspec = _head_specs(s, (0, N_HEADS, 2 * N_HEADS))
    ospec = pl.BlockSpec((t, GROUP_W), lambda g, i: (i, g))
    return _pcall(body, after=after, name=name, grid=(N_GROUPS, nb), in_specs=[qspec, kspec, vspec],
                  out_specs=[ospec, _sb_carry_spec(s)],
                  out_shape=[jax.ShapeDtypeStruct((s, HALF), F32), jax.ShapeDtypeStruct((N_HEADS, nb, nb, t, 1), F32)],
                  compiler_params=_params("parallel", "parallel", vmem_limit=ATT_VMEM_LIMIT_BYTES))(p, p, p)


def _sb_bwd(p, d_ab, carries, name):
    s = p.shape[0]
    t = ATT_BLOCK

    def body(q_ref, k_ref, v_ref, do_ref, cl_ref, dq_ref, dk_ref, dv_ref):
        i = pl.program_id(1)

        @pl.when(i == 0)
        def _():
            dk_ref[...] = jnp.zeros_like(dk_ref)
            dv_ref[...] = jnp.zeros_like(dv_ref)

        heads = range(HEADS_PER_STEP)
        q = [q_ref[:, _head_cols(h)].astype(BF16) for h in heads]
        do = [do_ref[:, _head_cols(h)].astype(BF16) for h in heads]
        u = _order_matrix(True)
        lower = _order_matrix(False)

        def tile(kb, carry, diagonal):
            ks = pl.multiple_of(kb * t, t)
            out = []
            for h in heads:
                dq, carry_g = carry[h]
                kblk = k_ref[pl.ds(ks, t), _head_cols(h)].astype(BF16)
                vblk = v_ref[pl.ds(ks, t), _head_cols(h)].astype(BF16)
                z, mask, _, a = _sb_block(q[h], kblk, cl_ref[h, kb], u, diagonal)
                g = a * _dot_nt(do[h], vblk)
                earlier_g = _exact_dot(g, lower, False) + carry_g
                sig = jax.nn.sigmoid(z)
                dz = g * (1.0 - sig) - sig * earlier_g
                if diagonal:
                    dz = jnp.where(mask, dz, 0.0)
                dz = dz.astype(BF16)
                dv_ref[pl.ds(ks, t), _head_cols(h)] += _dot_tn(a.astype(BF16), do[h])
                dk_ref[pl.ds(ks, t), _head_cols(h)] += _dot_tn(dz, q[h]) * ATT_SCALE
                out.append((dq + _dot_nn(dz, kblk) * ATT_SCALE, carry_g + jnp.sum(g, axis=1, keepdims=True)))
            return tuple(out)

        init = tuple((jnp.zeros((t, HEAD), F32), jnp.zeros((t, 1), F32)) for _ in heads)
        carry = tile(i, lax.fori_loop(0, i, lambda kb, c: tile(kb, c, False), init), True)
        for h in heads:
            dq_ref[:, _head_cols(h)] = carry[h][0]

    qspec, kspec, vspec = _head_specs(s, (0, N_HEADS, 2 * N_HEADS))
    blk = pl.BlockSpec((t, GROUP_W), lambda g, i: (i, g))
    whole = pl.BlockSpec((s, GROUP_W), lambda g, i: (0, g))
    shape = jax.ShapeDtypeStruct((s, HALF), F32)
    return _pcall(body, name=name, grid=(N_GROUPS, s // t), in_specs=[qspec, kspec, vspec, blk, _sb_carry_spec(s)],
                  out_specs=[blk, whole, whole], out_shape=[shape, shape, shape],
                  compiler_params=_params("parallel", "arbitrary", vmem_limit=ATT_VMEM_LIMIT_BYTES))(p, p, p, d_ab, carries)


COL_TILE = 256


def _sc_fwd(p, w, name):
    s = p.shape[0]
    nb = HALF // COL_TILE

    def body(gb_ref, gc_ref, h_ref, w_ref, o_ref):
        conv = _conv3(gc_ref[...] * h_ref[...], w_ref[...])
        o_ref[...] = (gb_ref[...] * conv).astype(BF16)

    def col(k):
        return pl.BlockSpec((s, COL_TILE), lambda j: (0, k * nb + j))

    wspec = pl.BlockSpec((3, COL_TILE), lambda j: (0, j))
    return _pcall(body, name=name, grid=(nb,), in_specs=[col(3), col(4), col(5), wspec], out_specs=col(0),
                  out_shape=jax.ShapeDtypeStruct((s, HALF), BF16), compiler_params=_params("parallel"))(p, p, p, w)


def _sc_bwd(p, w, d_ab, name):
    s = p.shape[0]
    nb = HALF // COL_TILE

    def body(gb_ref, gc_ref, h_ref, w_ref, d_ref, dgb_ref, dgc_ref, dh_ref, dw_ref):
        gc, hin, wv, d = gc_ref[...], h_ref[...], w_ref[...], d_ref[...]
        sig = gc * hin
        shifted = (_shift_down(sig, 1), _shift_down(sig, 2))
        dgb_ref[...] = d * _conv3(sig, wv, shifted)
        dconv = d * gb_ref[...]
        _conv3_wgrad(dconv, sig, shifted, dw_ref)
        dsig = _conv3_transpose(dconv, wv)
        dgc_ref[...] = dsig * hin
        dh_ref[...] = dsig * gc

    def col(k):
        return pl.BlockSpec((s, COL_TILE), lambda j: (0, k * nb + j))

    wspec = pl.BlockSpec((3, COL_TILE), lambda j: (0, j))
    act = jax.ShapeDtypeStruct((s, HALF), F32)
    return _pcall(body, name=name, grid=(nb,), in_specs=[col(3), col(4), col(5), wspec, col(1)],
                  out_specs=[col(0), col(0), col(0), wspec],
                  out_shape=[act, act, act, jax.ShapeDtypeStruct((3, HALF), F32)],
                  compiler_params=_params("parallel"))(p, p, p, w, d_ab)


def _ffn_act_fwd(u, w, name, after=()):
    s = u.shape[0]
    nb = D_FF // COL_TILE

    def body(ug_ref, uu_ref, wg_ref, wu_ref, o_ref):
        gate = _conv3(ug_ref[...], wg_ref[...])
        up = _conv3(uu_ref[...], wu_ref[...])
        o_ref[...] = (gate * jax.nn.sigmoid(gate) * up).astype(BF16)

    def col(k):
        return pl.BlockSpec((s, COL_TILE), lambda j: (0, k * nb + j))

    def wcol(k):
        return pl.BlockSpec((3, COL_TILE), lambda j: (0, k * nb + j))

    return _pcall(body, after=after, name=name, grid=(nb,), in_specs=[col(0), col(1), wcol(0), wcol(1)], out_specs=col(0),
                  out_shape=jax.ShapeDtypeStruct((s, D_FF), BF16),
                  compiler_params=_params("parallel"))(u, u, w, w)


def _ffn_act_bwd(u, w, d_f, name):
    s = u.shape[0]
    nb = D_FF // COL_TILE

    def body(ug_ref, uu_ref, wg_ref, wu_ref, d_ref, du_ref, dwg_ref, dwu_ref):
        ug, uu, wg, wu, d = ug_ref[...], uu_ref[...], wg_ref[...], wu_ref[...], d_ref[...]
        ug_shifted = (_shift_down(ug, 1), _shift_down(ug, 2))
        uu_shifted = (_shift_down(uu, 1), _shift_down(uu, 2))
        gate = _conv3(ug, wg, ug_shifted)
        up = _conv3(uu, wu, uu_shifted)
        sig = jax.nn.sigmoid(gate)
        d_up = d * gate * sig
        d_gate = d * up * sig * (1.0 + gate * (1.0 - sig))
        _conv3_wgrad(d_gate, ug, ug_shifted, dwg_ref)
        _conv3_wgrad(d_up, uu, uu_shifted, dwu_ref)
        du_ref[0] = _conv3_transpose(d_gate, wg).astype(BF16)
        du_ref[1] = _conv3_transpose(d_up, wu).astype(BF16)

    def col(k):
        return pl.BlockSpec((s, COL_TILE), lambda j: (0, k * nb + j))

    def wcol(k):
        return pl.BlockSpec((3, COL_TILE), lambda j: (0, k * nb + j))

    both = pl.BlockSpec((2, s, COL_TILE), lambda j: (0, 0, j))
    wsh = jax.ShapeDtypeStruct((3, D_FF), F32)
    return _pcall(body, name=name, grid=(nb,), in_specs=[col(0), col(1), wcol(0), wcol(1), col(0)],
                  out_specs=[both, wcol(0), wcol(0)], out_shape=[jax.ShapeDtypeStruct((2, s, D_FF), BF16), wsh, wsh],
                  compiler_params=_params("parallel"))(u, u, w, w, d_f)


def _sg_common(u, v, g, w_ref, bias, mixed_ref):
    rows = u.shape[0]
    gu = _gelu(u)
    gv = _gelu(v)
    xc = gv - jnp.mean(gv, axis=-1, keepdims=True)
    rstd = lax.rsqrt(jnp.mean(xc * xc, axis=-1, keepdims=True) + EPS)
    xhat = xc * rstd
    vn = xhat * g
    tril = lax.broadcasted_iota(jnp.int32, (HEAD, HEAD), 0) >= lax.broadcasted_iota(jnp.int32, (HEAD, HEAD), 1)
    wts = [jnp.where(tril, w_ref[grp], 0.0).astype(BF16) for grp in range(N_HEADS)]
    for n in range(rows // HEAD):
        for grp in range(N_HEADS):
            blk = vn[n * HEAD:(n + 1) * HEAD, grp * HEAD:(grp + 1) * HEAD].astype(BF16)
            mixed_ref[n * HEAD:(n + 1) * HEAD, grp * HEAD:(grp + 1) * HEAD] = _dot_nn(wts[grp], blk)
    mixed = mixed_ref[...] + jnp.concatenate([bias] * (rows // HEAD), axis=0)
    return gu, xhat, rstd, vn, mixed, wts, tril


def _sg_fwd(p, sg_w, bias, g, name):
    s = p.shape[0]

    def body(u_ref, v_ref, w_ref, b_ref, g_ref, o_ref, mixed_ref):
        gu, _, _, _, mixed, _, _ = _sg_common(u_ref[...], v_ref[...], g_ref[...], w_ref, b_ref[...], mixed_ref)
        o_ref[...] = (gu * mixed).astype(BF16)

    def half(k):
        return pl.BlockSpec((ROW_TILE, HALF), lambda i: (i, k))

    wspec = pl.BlockSpec((N_HEADS, HEAD, HEAD), lambda i: (0, 0, 0))
    bspec = pl.BlockSpec((HEAD, HALF), lambda i: (0, 0))
    gspec = pl.BlockSpec((1, HALF), lambda i: (0, 0))
    return _pcall(body, name=name, grid=(s // ROW_TILE,), in_specs=[half(0), half(1), wspec, bspec, gspec],
                  out_specs=half(0), out_shape=jax.ShapeDtypeStruct((s, HALF), BF16),
                  scratch_shapes=[pltpu.VMEM((ROW_TILE, HALF), F32)],
                  compiler_params=_params("parallel"))(p, p, sg_w, bias, g)


def _sg_bwd(p, sg_w, bias, g, d_cd, name):
    s = p.shape[0]
    nsteps = s // ROW_TILE

    def body(u_ref, v_ref, w_ref, b_ref, g_ref, d_ref, du_ref, dv_ref, dw_ref, db_ref, dg_ref,
             mixed_ref, dvn_ref, dbias_ref):
        i = pl.program_id(0)
        u, v, gain, d = u_ref[...], v_ref[...], g_ref[...], d_ref[...]
        gu, xhat, rstd, vn, mixed, wts, tril = _sg_common(u, v, gain, w_ref, b_ref[...], mixed_ref)

        @pl.when(i == 0)
        def _():
            dw_ref[...] = jnp.zeros_like(dw_ref)
            dg_ref[...] = jnp.zeros_like(dg_ref)
            dbias_ref[...] = jnp.zeros_like(dbias_ref)

        du_ref[...] = d * mixed * _gelu_grad(u)
        dm = d * gu
        for n in range(ROW_TILE // HEAD):
            rs = slice(n * HEAD, (n + 1) * HEAD)
            dbias_ref[...] += dm[rs, :]
            for grp in range(N_HEADS):
                cs = slice(grp * HEAD, (grp + 1) * HEAD)
                dm_blk = dm[rs, cs].astype(BF16)
                dw_ref[grp] += jnp.where(tril, _dot_nt(dm_blk, vn[rs, cs].astype(BF16)), 0.0)
                dvn_ref[rs, cs] = _dot_tn(wts[grp], dm_blk)
        dvn = dvn_ref[...]
        dg_ref[...] += jnp.sum(dvn * xhat, axis=0, keepdims=True)
        dxh = dvn * gain
        d_gv = rstd * (dxh - jnp.mean(dxh, axis=-1, keepdims=True) - xhat * jnp.mean(dxh * xhat, axis=-1, keepdims=True))
        dv_ref[...] = d_gv * _gelu_grad(v)

        @pl.when(i == nsteps - 1)
        def _():
            lane = lax.broadcasted_iota(jnp.int32, (HEAD, HEAD), 1)
            out = jnp.zeros((HEAD, HEAD), F32)
            for grp in range(N_HEADS):
                tot = jnp.sum(dbias_ref[:, grp * HEAD:(grp + 1) * HEAD], axis=1, keepdims=True)
                out = out + jnp.where(lane == grp, tot, 0.0)
            db_ref[...] = out

    def half(k):
        return pl.BlockSpec((ROW_TILE, HALF), lambda i: (i, k))

    wspec = pl.BlockSpec((N_HEADS, HEAD, HEAD), lambda i: (0, 0, 0))
    bspec = pl.BlockSpec((HEAD, HALF), lambda i: (0, 0))
    gspec = pl.BlockSpec((1, HALF), lambda i: (0, 0))
    dbspec = pl.BlockSpec((HEAD, HEAD), lambda i: (0, 0))
    act = jax.ShapeDtypeStruct((s, HALF), F32)
    return _pcall(body, name=name, grid=(nsteps,), in_specs=[half(0), half(1), wspec, bspec, gspec, half(0)],
                  out_specs=[half(0), half(0), wspec, dbspec, gspec],
                  out_shape=[act, act, jax.ShapeDtypeStruct((N_HEADS, HEAD, HEAD), F32),
                             jax.ShapeDtypeStruct((HEAD, HEAD), F32), jax.ShapeDtypeStruct((1, HALF), F32)],
                  scratch_shapes=[pltpu.VMEM((ROW_TILE, HALF), F32), pltpu.VMEM((ROW_TILE, HALF), F32),
                                  pltpu.VMEM((HEAD, HALF), F32)],
                  compiler_params=_params("arbitrary"))(p, p, sg_w, bias, g, d_cd)


def _fox_prep(f, b, name):
    s = f.shape[0]
    t = ATT_BLOCK

    def body(f_ref, b_ref, c_ref):
        tri = (lax.broadcasted_iota(jnp.int32, (t, t), 0) >= lax.broadcasted_iota(jnp.int32, (t, t), 1)).astype(BF16)
        carry = jnp.zeros((1, 128), F32)
        for n in range(s // t):
            lf = _log_sigmoid(f_ref[n * t:(n + 1) * t, :] + b_ref[...])
            hi, mid, lo = _split3(lf)
            c_ref[n * t:(n + 1) * t, :] = _dot_nn(tri, hi) + _dot_nn(tri, mid) + _dot_nn(tri, lo) + carry
            carry = carry + jnp.sum(lf, axis=0, keepdims=True)

    return _pcall(body, name=name, in_specs=[VMEM_SPEC, VMEM_SPEC], out_specs=VMEM_SPEC,
                  out_shape=jax.ShapeDtypeStruct((s, 128), F32))(f, b)


def _fox_post(drow, dcol, f, b, name):
    s = f.shape[0]
    t = ATT_BLOCK

    def body(drow_ref, dcol_ref, f_ref, b_ref, df_ref, db_ref):
        tri = (lax.broadcasted_iota(jnp.int32, (t, t), 1) >= lax.broadcasted_iota(jnp.int32, (t, t), 0)).astype(BF16)
        lane = lax.broadcasted_iota(jnp.int32, (t, 128), 1)
        carry = jnp.zeros((1, 128), F32)
        db = jnp.zeros((1, 128), F32)
        for n in reversed(range(s // t)):
            rs = slice(n * t, (n + 1) * t)
            dc = jnp.zeros((t, 128), F32)
            for h in range(N_HEADS):
                dc = jnp.where(lane == h, drow_ref[h, rs, :] - dcol_ref[rs, _head_cols(h)], dc)
            hi, mid, lo = _split3(dc)
            dlogf = _dot_nn(tri, hi) + _dot_nn(tri, mid) + _dot_nn(tri, lo) + carry
            carry = carry + jnp.sum(dc, axis=0, keepdims=True)
            df = dlogf * jax.nn.sigmoid(-(f_ref[rs, :] + b_ref[...]))
            df_ref[rs, :] = df
            db = db + jnp.sum(df, axis=0, keepdims=True)
        db_ref[...] = db

    return _pcall(body, name=name, in_specs=[VMEM_SPEC] * 4, out_specs=[VMEM_SPEC, VMEM_SPEC],
                  out_shape=[jax.ShapeDtypeStruct((s, 128), F32), jax.ShapeDtypeStruct((1, 128), F32)],
                  compiler_params=pltpu.CompilerParams(vmem_limit_bytes=VMEM_LIMIT_BYTES))(drow, dcol, f, b)


def _fox_specs(s):
    t = ATT_BLOCK
    ccol = pl.BlockSpec((HEADS_PER_STEP, t, 1), lambda g, i: (g, i, 0))
    crow = pl.BlockSpec((HEADS_PER_STEP, s // t, 1, t), lambda g, i: (g, 0, 0, 0))
    return ccol, crow


def _fox_fwd(p, c_col, c_row, name, after=()):
    s = p.shape[0]
    t = ATT_BLOCK

    def body(q_ref, k_ref, v_ref, cc_ref, cr_ref, o_ref, lse_ref):
        i = pl.program_id(1)
        heads = range(HEADS_PER_STEP)
        q = [q_ref[:, _head_cols(h)].astype(BF16) for h in heads]
        ct = [cc_ref[h] for h in heads]

        def tile(n, carry, diagonal):
            ks = pl.multiple_of(n * t, t)
            out = []
            for h in heads:
                acc, m, l = carry[h]
                kblk = k_ref[pl.ds(ks, t), _head_cols(h)].astype(BF16)
                vblk = v_ref[pl.ds(ks, t), _head_cols(h)].astype(BF16)
                logit = _dot_nt(q[h], kblk) * ATT_SCALE + ct[h] - cr_ref[h, n]
                if diagonal:
                    causal = lax.broadcasted_iota(jnp.int32, (t, t), 1) <= lax.broadcasted_iota(jnp.int32, (t, t), 0)
                    logit = jnp.where(causal, logit, NEG)
                m_new = jnp.maximum(m, jnp.max(logit, axis=1, keepdims=True))
                alpha = jnp.exp(m - m_new)
                pr = jnp.exp(logit - m_new)
                l = alpha * l + jnp.sum(pr, axis=1, keepdims=True)
                out.append((alpha * acc + _dot_nn(pr.astype(BF16), vblk), m_new, l))
            return tuple(out)

        init = tuple((jnp.zeros((t, HEAD), F32), jnp.full((t, 1), NEG, F32), jnp.zeros((t, 1), F32)) for _ in heads)
        carry = tile(i, lax.fori_loop(0, i, lambda n, c: tile(n, c, False), init), True)
        for h in heads:
            acc, m, l = carry[h]
            o_ref[:, _head_cols(h)] = acc / l
            lse_ref[h] = m + jnp.log(l)

    qspec, kspec, vspec = _head_specs(s, (2 * N_HEADS, 3 * N_HEADS, 4 * N_HEADS))
    ccol, crow = _fox_specs(s)
    ospec = pl.BlockSpec((t, GROUP_W), lambda g, i: (i, g))
    return _pcall(body, after=after, name=name, grid=(N_GROUPS, s // t), in_specs=[qspec, kspec, vspec, ccol, crow],
                  out_specs=[ospec, ccol],
                  out_shape=[jax.ShapeDtypeStruct((s, HALF), F32), jax.ShapeDtypeStruct((N_HEADS, s, 1), F32)],
                  compiler_params=_params("parallel", "parallel", vmem_limit=ATT_VMEM_LIMIT_BYTES))(p, p, p, c_col, c_row)


def _fox_bwd(p, c_col, c_row, lse, d_cd, d_out, name):
    s = p.shape[0]
    t = ATT_BLOCK

    def body(q_ref, k_ref, v_ref, cc_ref, cr_ref, lse_ref, do_ref, o_ref, dq_ref, dk_ref, dv_ref, dcol_ref, drow_ref):
        i = pl.program_id(1)

        @pl.when(i == 0)
        def _():
            dk_ref[...] = jnp.zeros_like(dk_ref)
            dv_ref[...] = jnp.zeros_like(dv_ref)
            dcol_ref[...] = jnp.zeros_like(dcol_ref)

        heads = range(HEADS_PER_STEP)
        q = [q_ref[:, _head_cols(h)].astype(BF16) for h in heads]
        do = [do_ref[:, _head_cols(h)].astype(BF16) for h in heads]
        delta = [jnp.sum(do_ref[:, _head_cols(h)] * o_ref[:, _head_cols(h)], axis=1, keepdims=True) for h in heads]
        ct = [cc_ref[h] for h in heads]
        lse_v = [lse_ref[h] for h in heads]
        ones = jnp.ones((t, HEAD), BF16)

        def tile(n, carry, diagonal):
            ks = pl.multiple_of(n * t, t)
            out = []
            for h in heads:
                dq, drow = carry[h]
                kblk = k_ref[pl.ds(ks, t), _head_cols(h)].astype(BF16)
                vblk = v_ref[pl.ds(ks, t), _head_cols(h)].astype(BF16)
                logit = _dot_nt(q[h], kblk) * ATT_SCALE + ct[h] - cr_ref[h, n]
                pr = jnp.exp(logit - lse_v[h])
                if diagonal:
                    causal = lax.broadcasted_iota(jnp.int32, (t, t), 1) <= lax.broadcasted_iota(jnp.int32, (t, t), 0)
                    pr = jnp.where(causal, pr, 0.0)
                ds = pr * (_dot_nt(do[h], vblk) - delta[h])
                dsb = ds.astype(BF16)
                dv_ref[pl.ds(ks, t), _head_cols(h)] += _dot_tn(pr.astype(BF16), do[h])
                dk_ref[pl.ds(ks, t), _head_cols(h)] += _dot_tn(dsb, q[h]) * ATT_SCALE
                dcol_ref[pl.ds(ks, t), _head_cols(h)] += _dot_tn(dsb, ones)
                out.append((dq + _dot_nn(dsb, kblk) * ATT_SCALE,
                            drow + jnp.sum(dsb.astype(F32), axis=1, keepdims=True)))
            return tuple(out)

        init = tuple((jnp.zeros((t, HEAD), F32), jnp.zeros((t, 1), F32)) for _ in heads)
        carry = tile(i, lax.fori_loop(0, i, lambda n, c: tile(n, c, False), init), True)
        for h in heads:
            dq_ref[:, _head_cols(h)] = carry[h][0]
            drow_ref[h] = carry[h][1]

    qspec, kspec, vspec = _head_specs(s, (2 * N_HEADS, 3 * N_HEADS, 4 * N_HEADS))
    ccol, crow = _fox_specs(s)
    dospec = pl.BlockSpec((t, GROUP_W), lambda g, i: (i, N_GROUPS + g))
    blk = pl.BlockSpec((t, GROUP_W), lambda g, i: (i, g))
    whole = pl.BlockSpec((s, GROUP_W), lambda g, i: (0, g))
    shape = jax.ShapeDtypeStruct((s, HALF), F32)
    return _pcall(body, name=name, grid=(N_GROUPS, s // t),
                  in_specs=[qspec, kspec, vspec, ccol, crow, ccol, dospec, blk],
                  out_specs=[blk, whole, whole, whole, ccol],
                  out_shape=[shape, shape, shape, shape, jax.ShapeDtypeStruct((N_HEADS, s, 1), F32)],
                  compiler_params=_params("parallel", "arbitrary", vmem_limit=ATT_VMEM_LIMIT_BYTES))(p, p, p, c_col, c_row, lse, d_cd, d_out)


def _row_tile(rows, cap):
    for t in (256, 128, 64, 32, 16, 8):
        if t <= cap and rows % t == 0:
            return t
    return rows


def _adamw_small(ws, gs, ms, vs, name):
    n = len(ws)
    c1 = 1.0 / (1.0 - ADAM_B1 ** ADAM_STEP)
    c2 = 1.0 / (1.0 - ADAM_B2 ** ADAM_STEP)

    def body(*refs):
        for k in range(n):
            w_ref, g_ref, m_ref, v_ref = (refs[j * n + k] for j in range(4))
            d_ref, nm_ref, nv_ref = (refs[(4 + j) * n + k] for j in range(3))
            gv = g_ref[...]
            nm = ADAM_B1 * m_ref[...] + (1.0 - ADAM_B1) * gv
            nv = ADAM_B2 * v_ref[...] + (1.0 - ADAM_B2) * (gv * gv)
            nm_ref[...] = nm
            nv_ref[...] = nv
            d_ref[...] = -ADAM_LR * ((nm * c1) / (jnp.sqrt(nv * c2) + ADAM_EPS) + ADAM_WD * w_ref[...])

    shapes = [jax.ShapeDtypeStruct(w.shape, F32) for w in ws] * 3
    outs = _pcall(body, name=name, in_specs=[VMEM_SPEC] * (4 * n), out_specs=[VMEM_SPEC] * (3 * n), out_shape=shapes,
                  compiler_params=pltpu.CompilerParams(vmem_limit_bytes=VMEM_LIMIT_BYTES))(*ws, *gs, *ms, *vs)
    return outs[:n], outs[n:2 * n], outs[2 * n:]


def _half_shape(whole_shape, kind):
    if kind == "col":
        return (whole_shape[0] // 2, whole_shape[1] // 4)
    if kind == "row":
        return (whole_shape[0] // 8, whole_shape[1])
    return (whole_shape[1] // 2, whole_shape[2])


def _own_half_spec(whole_shape, kind, tr):
    hr, hc = _half_shape(whole_shape, kind)
    nb = hr // tr
    if kind == "col":
        return pl.BlockSpec((tr, hc), lambda i, pos: (pos[1] * nb + i, pos[0]))
    if kind == "row":
        return pl.BlockSpec((tr, hc), lambda i, pos: ((2 * pos[0] + pos[1]) * nb + i, 0))
    return pl.BlockSpec((None, tr, hc), lambda i, pos: (pos[0], pos[1] * nb + i, 0))


def _sum_partials(pos, grad, landed, kind, name):
    hr, hc = _half_shape(grad.shape, kind)
    tr = _row_tile(hr, 64)

    def body(pos_ref, g_ref, p_ref, o_ref):
        acc = g_ref[...].astype(F32)
        for k in range(N_DEV - 1):
            acc = acc + p_ref[k].astype(F32)
        o_ref[...] = acc

    grid_spec = pltpu.PrefetchScalarGridSpec(
        num_scalar_prefetch=1, grid=(hr // tr,),
        in_specs=[_own_half_spec(grad.shape, kind, tr), pl.BlockSpec((N_DEV - 1, tr, hc), lambda i, pos: (0, i, 0))],
        out_specs=pl.BlockSpec((tr, hc), lambda i, pos: (i, 0)))
    return _pcall(body, name=name, grid_spec=grid_spec, out_shape=jax.ShapeDtypeStruct((hr, hc), F32),
                  compiler_params=_params("parallel"))(pos, grad, landed)


def _adamw_shard(pos, w, g_mine, g_sibling, m, v, name):
    hr, hc = g_mine.shape
    tr = _row_tile(hr, 128)
    nb = hr // tr
    c1 = 1.0 / (1.0 - ADAM_B1 ** ADAM_STEP)
    c2 = 1.0 / (1.0 - ADAM_B2 ** ADAM_STEP)

    def body(pos_ref, w_ref, gm_ref, gs_ref, m_ref, v_ref, g_ref, d_ref, nm_ref, nv_ref):
        mine = (pl.program_id(0) // nb) == pos_ref[1]
        gv = jnp.where(mine, gm_ref[...], gs_ref[...])
        nm = ADAM_B1 * m_ref[...] + (1.0 - ADAM_B1) * gv
        nv = ADAM_B2 * v_ref[...] + (1.0 - ADAM_B2) * (gv * gv)
        g_ref[...] = gv
        nm_ref[...] = nm
        nv_ref[...] = nv
        d_ref[...] = -ADAM_LR * ((nm * c1) / (jnp.sqrt(nv * c2) + ADAM_EPS) + ADAM_WD * w_ref[...])

    full = pl.BlockSpec((tr, hc), lambda i, pos: (i, 0))
    mine_spec = pl.BlockSpec((tr, hc), lambda i, pos: (jnp.clip(i - pos[1] * nb, 0, nb - 1), 0))
    sib_spec = pl.BlockSpec((tr, hc), lambda i, pos: (jnp.clip(i - (1 - pos[1]) * nb, 0, nb - 1), 0))
    grid_spec = pltpu.PrefetchScalarGridSpec(
        num_scalar_prefetch=1, grid=(2 * nb,), in_specs=[full, mine_spec, sib_spec, full, full], out_specs=[full] * 4)
    shape = jax.ShapeDtypeStruct((2 * hr, hc), F32)
    return _pcall(body, name=name, grid_spec=grid_spec, out_shape=[shape] * 4,
                  compiler_params=_params("parallel"))(pos, w, g_mine, g_sibling, m, v)


def _place_shard(pos, shard, kind, name, after=()):
    rows, cols = shard.shape
    tr = _row_tile(rows, 256)
    nb = rows // tr
    if kind == "col":
        out_spec = pl.BlockSpec((tr, cols), lambda i, pos: (i, pos[0]))
    elif kind == "row":
        out_spec = pl.BlockSpec((tr, cols), lambda i, pos: (pos[0] * nb + i, 0))
    else:
        out_spec = pl.BlockSpec((None, tr, cols), lambda i, pos: (pos[0], i, 0))

    def body(pos_ref, s_ref, *rest):
        rest[-1][...] = s_ref[...].astype(BF16)

    grid_spec = pltpu.PrefetchScalarGridSpec(
        num_scalar_prefetch=1, grid=(nb,),
        in_specs=[pl.BlockSpec((tr, cols), lambda i, pos: (i, 0))] + [pl.BlockSpec(memory_space=pl.ANY)] * len(after),
        out_specs=out_spec)
    return _pcall(body, name=name, grid_spec=grid_spec,
                  out_shape=jax.ShapeDtypeStruct(_whole_shape(shard.shape, kind), BF16),
                  compiler_params=_params("parallel"))(pos, shard, *after)


N_DEV = 8
RELATIONS = [(r >> 2 & 1, r >> 1 & 1, r & 1) for r in range(1, N_DEV)]


def _position():
    return lax.axis_index("x"), lax.axis_index("y"), lax.axis_index("c")


def _related(pos, rel):
    return tuple(1 - p if f else p for p, f in zip(pos, rel))


def _index(pos):
    return 4 * pos[0] + 2 * pos[1] + pos[2]


def _window(ref, kind, pos):
    px, py, pc = pos
    j = 2 * px + py
    if kind == "col":
        r, c = ref.shape
        return ref.at[pl.ds(pc * (r // 2), r // 2), pl.ds(pl.multiple_of(j * (c // 4), 128), c // 4)]
    if kind == "row":
        rj = ref.shape[0] // 4
        return ref.at[pl.ds(j * rj + pc * (rj // 2), rj // 2), :]
    r = ref.shape[1]
    return ref.at[j, pl.ds(pc * (r // 2), r // 2), :]


def _whole_shape(shard_shape, kind):
    r, c = shard_shape
    return {"col": (r, 4 * c), "row": (4 * r, c), "maj": (4, r, c)}[kind]


SEM_SPEC = pl.BlockSpec(memory_space=pltpu.SEMAPHORE)
ANY_SPEC = pl.BlockSpec(memory_space=pl.ANY)
DATAFLOW = pltpu.SideEffectType.DATAFLOW_SIDE_EFFECTING
TOKEN = jax.ShapeDtypeStruct((8, 128), F32)


def _hbm(a):
    return pltpu.with_memory_space_constraint(a, pltpu.HBM)


def _chips(x, y):
    return [(1 - x, y), (x, 1 - y), (1 - x, 1 - y)]


def _split_start(body, name, buffers, n_sems, after=()):
    n = len(buffers)

    def wrapped(*refs):
        body(refs[:n], refs[n], refs[n + 1])
        refs[-1][...] = jnp.zeros_like(refs[-1])

    outs = _pcall(
        wrapped, after=after, name=name, in_specs=[HBM_SPEC] * n,
        out_specs=[SEM_SPEC, SEM_SPEC] + [HBM_SPEC] * n + [VMEM_SPEC],
        out_shape=[pltpu.SemaphoreType.DMA(n_sems), pltpu.SemaphoreType.DMA(n_sems)]
        + [pltpu.HBM(b.shape, b.dtype) for b in buffers] + [TOKEN],
        input_output_aliases={i: 2 + i for i in range(n)},
        compiler_params=pltpu.CompilerParams(has_side_effects=DATAFLOW))(*[_hbm(b) for b in buffers])
    return outs[0], outs[1], list(outs[2:2 + n]), outs[2 + n]


def _split_wait(body, name, buffers, send_sems, recv_sems, after):
    n = len(buffers)
    after = list(after) if isinstance(after, (list, tuple)) else [after]

    def wrapped(*refs):
        body(refs[:n], refs[n], refs[n + 1])

    outs = _pcall(
        wrapped, name=name, in_specs=[HBM_SPEC] * n + [SEM_SPEC, SEM_SPEC] + [ANY_SPEC] * len(after),
        out_specs=[HBM_SPEC] * n, out_shape=[pltpu.HBM(b.shape, b.dtype) for b in buffers],
        input_output_aliases={i: i for i in range(n)},
        compiler_params=pltpu.CompilerParams(has_side_effects=DATAFLOW))(*buffers, send_sems, recv_sems, *after)
    return list(outs)


def _gather_start(wholes, kinds, name, after=()):
    def body(w_refs, send_sems, recv_sems):
        x, y, c = _position()
        for w, ref in enumerate(w_refs):
            mine = _window(ref, kinds[w], (x, y, c))
            for k, chip in enumerate(_chips(x, y)):
                pltpu.make_async_remote_copy(src_ref=mine, dst_ref=mine, send_sem=send_sems.at[3 * w + k],
                                             recv_sem=recv_sems.at[3 * w + k], device_id=(*chip, c),
                                             device_id_type=MESH).start()

    return _split_start(body, name, wholes, (3 * len(wholes),), after)


def _gather_forward(wholes, kinds, send1, recv1, after, name):
    n = len(wholes)

    def wrapped(*refs):
        w_refs, s1, r1, s2, r2 = refs[:n], refs[n], refs[n + 1], refs[n + 3], refs[n + 4]
        x, y, c = _position()
        for k, chip in enumerate(_chips(x, y)):
            for w, ref in enumerate(w_refs):
                theirs = _window(ref, kinds[w], (*chip, c))
                pltpu.make_async_remote_copy(src_ref=theirs, dst_ref=theirs, send_sem=s1.at[3 * w + k],
                                             recv_sem=r1.at[3 * w + k], device_id=(*chip, c),
                                             device_id_type=MESH).wait_recv()
                pltpu.make_async_remote_copy(src_ref=theirs, dst_ref=theirs, send_sem=s2.at[3 * w + k],
                                             recv_sem=r2.at[3 * w + k], device_id=(x, y, 1 - c),
                                             device_id_type=MESH).start()
        for w, ref in enumerate(w_refs):
            mine = _window(ref, kinds[w], (x, y, c))
            for k, chip in enumerate(_chips(x, y)):
                pltpu.make_async_remote_copy(src_ref=mine, dst_ref=mine, send_sem=s1.at[3 * w + k],
                                             recv_sem=r1.at[3 * w + k], device_id=(*chip, c),
                                             device_id_type=MESH).wait_send()
        refs[-1][...] = jnp.zeros_like(refs[-1])

    outs = _pcall(
        wrapped, name=name, in_specs=[HBM_SPEC] * n + [SEM_SPEC, SEM_SPEC, ANY_SPEC],
        out_specs=[SEM_SPEC, SEM_SPEC] + [HBM_SPEC] * n + [VMEM_SPEC],
        out_shape=[pltpu.SemaphoreType.DMA((3 * n,)), pltpu.SemaphoreType.DMA((3 * n,))]
        + [pltpu.HBM(b.shape, b.dtype) for b in wholes] + [TOKEN],
        input_output_aliases={i: 2 + i for i in range(n)},
        compiler_params=pltpu.CompilerParams(has_side_effects=DATAFLOW))(*wholes, send1, recv1, after)
    return outs[0], outs[1], list(outs[2:2 + n]), outs[2 + n]


def _gather_finish(wholes, kinds, send2, recv2, after, name):
    def body(w_refs, s2, r2):
        x, y, c = _position()
        for k, chip in enumerate(_chips(x, y)):
            for w, ref in enumerate(w_refs):
                sent = _window(ref, kinds[w], (*chip, c))
                got = _window(ref, kinds[w], (*chip, 1 - c))
                pltpu.make_async_remote_copy(src_ref=sent, dst_ref=got, send_sem=s2.at[3 * w + k],
                                             recv_sem=r2.at[3 * w + k], device_id=(x, y, 1 - c),
                                             device_id_type=MESH).wait()

    return _split_wait(body, name, wholes, send2, recv2, after)


def _gather_small_start(small, after):
    def body(refs, send_sems, recv_sems):
        s_ref, land_ref = refs
        x, y, c = _position()
        for k, chip in enumerate(_chips(x, y)):
            pltpu.make_async_remote_copy(src_ref=s_ref, dst_ref=land_ref.at[2 * x + y], send_sem=send_sems.at[k],
                                         recv_sem=recv_sems.at[k], device_id=(*chip, c), device_id_type=MESH).start()

    return _split_start(body, "gather_small_start", [small, lax.empty((4,) + small.shape, small.dtype)], (3,), after)


def _gather_small_finish(send, recv, thru, after):
    def body(refs, send_sems, recv_sems):
        s_ref, land_ref = refs
        x, y, c = _position()
        for k, chip in enumerate(_chips(x, y)):
            pltpu.make_async_remote_copy(src_ref=s_ref, dst_ref=land_ref.at[2 * chip[0] + chip[1]],
                                         send_sem=send_sems.at[k], recv_sem=recv_sems.at[k], device_id=(*chip, c),
                                         device_id_type=MESH).wait()

    return _split_wait(body, "gather_small_finish", thru, send, recv, after)[1]


def _scatter_copies(g_refs, land_refs, kinds, send_sems, recv_sems):
    me = _position()
    copies = []
    for k, rel in enumerate(RELATIONS):
        peer = _related(me, rel)
        for w, (g_ref, land_ref) in enumerate(zip(g_refs, land_refs)):
            copies.append(pltpu.make_async_remote_copy(
                src_ref=_window(g_ref, kinds[w], peer), dst_ref=land_ref.at[k],
                send_sem=send_sems.at[7 * w + k], recv_sem=recv_sems.at[7 * w + k], device_id=peer,
                device_id_type=MESH))
    return copies


def _scatter_start(grads, kinds, name):
    n = len(grads)
    lands = [lax.empty((N_DEV - 1,) + _half_shape(g.shape, kd), g.dtype) for g, kd in zip(grads, kinds)]

    def body(refs, send_sems, recv_sems):
        for cp in _scatter_copies(refs[:n], refs[n:], kinds, send_sems, recv_sems):
            cp.start()

    send, recv, thru, token = _split_start(body, name, list(grads) + lands, ((N_DEV - 1) * n,))
    return send, recv, thru[:n], thru[n:], token


def _scatter_wait(grads, lands, kinds, send, recv, after, name):
    n = len(grads)

    def body(refs, send_sems, recv_sems):
        for cp in _scatter_copies(refs[:n], refs[n:], kinds, send_sems, recv_sems):
            cp.wait()

    out = _split_wait(body, name, list(grads) + list(lands), send, recv, after)
    return out[:n], out[n:]


def _swap_start(halves, name):
    n = len(halves)
    lands = [lax.empty(h.shape, h.dtype) for h in halves]

    def body(refs, send_sems, recv_sems):
        x, y, c = _position()
        for w in range(n):
            pltpu.make_async_remote_copy(src_ref=refs[w], dst_ref=refs[n + w], send_sem=send_sems.at[w],
                                         recv_sem=recv_sems.at[w], device_id=(x, y, 1 - c), device_id_type=MESH).start()

    send, recv, thru, token = _split_start(body, name, list(halves) + lands, (n,))
    return send, recv, thru[:n], thru[n:], token


def _swap_wait(halves, lands, send, recv, after, name):
    n = len(halves)

    def body(refs, send_sems, recv_sems):
        x, y, c = _position()
        for w in range(n):
            pltpu.make_async_remote_copy(src_ref=refs[w], dst_ref=refs[n + w], send_sem=send_sems.at[w],
                                         recv_sem=recv_sems.at[w], device_id=(x, y, 1 - c), device_id_type=MESH).wait()

    out = _split_wait(body, name, list(halves) + list(lands), send, recv, after)
    return out[:n], out[n:]


def _allreduce_small(v, after=()):
    rows = v.shape[0]

    def body(v_ref, o_ref, recv_ref, send_sems, recv_sems):
        me = _position()
        recv_ref[_index(me)] = v_ref[...]
        sends = []
        for k, rel in enumerate(RELATIONS):
            peer = _related(me, rel)
            cp = pltpu.make_async_remote_copy(
                src_ref=v_ref, dst_ref=recv_ref.at[_index(me)],
                send_sem=send_sems.at[k], recv_sem=recv_sems.at[k], device_id=peer, device_id_type=MESH)
            cp.start()
            sends.append(cp)
        for k, rel in enumerate(RELATIONS):
            peer = _related(me, rel)
            pltpu.make_async_remote_copy(
                src_ref=v_ref, dst_ref=recv_ref.at[_index(peer)],
                send_sem=send_sems.at[k], recv_sem=recv_sems.at[k], device_id=peer, device_id_type=MESH).wait_recv()
        for cp in sends:
            cp.wait_send()
        acc = recv_ref[0]
        for k in range(1, N_DEV):
            acc = acc + recv_ref[k]
        o_ref[...] = acc

    return _pcall(body, after=after, name="allreduce_small", in_specs=[VMEM_SPEC], out_specs=VMEM_SPEC,
                  out_shape=jax.ShapeDtypeStruct((rows, 128), F32),
                  scratch_shapes=[pltpu.VMEM((N_DEV, rows, 128), F32), pltpu.SemaphoreType.DMA((7,)),
                                  pltpu.SemaphoreType.DMA((7,))],
                  compiler_params=pltpu.CompilerParams(vmem_limit_bytes=VMEM_LIMIT_BYTES))(v)


def _small_copies(refs, send_sems, recv_sems):
    v_ref, land_ref = refs
    me = _position()
    return [pltpu.make_async_remote_copy(src_ref=v_ref, dst_ref=land_ref.at[_index(me)], send_sem=send_sems.at[k],
                                         recv_sem=recv_sems.at[k], device_id=_related(me, rel), device_id_type=MESH)
            for k, rel in enumerate(RELATIONS)]


def _small_wait_copies(refs, send_sems, recv_sems):
    v_ref, land_ref = refs
    me = _position()
    return [pltpu.make_async_remote_copy(src_ref=v_ref, dst_ref=land_ref.at[_index(_related(me, rel))],
                                         send_sem=send_sems.at[k], recv_sem=recv_sems.at[k],
                                         device_id=_related(me, rel), device_id_type=MESH)
            for k, rel in enumerate(RELATIONS)]


def _small_reduce_start(v, name, after):
    def body(refs, send_sems, recv_sems):
        for cp in _small_copies(refs, send_sems, recv_sems):
            cp.start()

    send, recv, thru, token = _split_start(body, name, [v, lax.empty((N_DEV,) + v.shape, v.dtype)], (N_DEV - 1,), after)
    return send, recv, thru, token


def _small_reduce_finish(me, send, recv, thru, after, name):
    def body(refs, send_sems, recv_sems):
        for cp in _small_wait_copies(refs, send_sems, recv_sems):
            cp.wait()

    v, landed = _split_wait(body, name + "_wait", thru, send, recv, after)

    def add(me_ref, v_ref, land_ref, o_ref):
        acc = jnp.where(me_ref[0] == 0, v_ref[...], land_ref[0])
        for d in range(1, N_DEV):
            acc = acc + jnp.where(me_ref[0] == d, v_ref[...], land_ref[d])
        o_ref[...] = acc

    return _pcall(add, name=name + "_sum",
                  in_specs=[pl.BlockSpec(memory_space=pltpu.SMEM), VMEM_SPEC, VMEM_SPEC], out_specs=VMEM_SPEC,
                  out_shape=jax.ShapeDtypeStruct(v.shape, F32),
                  compiler_params=pltpu.CompilerParams(vmem_limit_bytes=VMEM_LIMIT_BYTES))(me, v, landed)


def _pack(arrays):
    flat = []
    for a in arrays:
        a = a.reshape(-1)
        flat.append(jnp.pad(a, (0, -a.shape[0] % 128)))
    flat = jnp.concatenate(flat)
    flat = jnp.pad(flat, (0, -flat.shape[0] % 1024))
    return flat.reshape(-1, 128)


def _unpack(packed, shapes):
    flat = packed.reshape(-1)
    out, at = [], 0
    for shp in shapes:
        size = 1
        for d in shp:
            size *= d
        out.append(flat[at:at + size].reshape(shp))
        at += size + (-size % 128)
    return out


WEIGHTS = ['l0_mix_norm_g', 'l0_w_in', 'l0_sc_conv_w', 'l0_w_out', 'l0_ffn_norm_g', 'l0_ffn_up', 'l0_ffn_conv_w',
           'l0_ffn_down', 'l1_mix_norm_g', 'l1_w_in', 'l1_fox_b_f', 'l1_sg_w', 'l1_sg_b', 'l1_sg_norm_g', 'l1_w_out',
           'l1_ffn_norm_g', 'l1_ffn_up', 'l1_ffn_conv_w', 'l1_ffn_down', 'final_norm_g']
BIG = {'l0_w_in': 'col', 'l0_w_out': 'row', 'l0_ffn_up': 'col', 'l0_ffn_down': 'row',
       'l1_w_in': 'maj', 'l1_w_out': 'row', 'l1_ffn_up': 'col', 'l1_ffn_down': 'row'}
GATHER_GROUPS = [['l0_w_in'], ['l0_w_out'], ['l0_ffn_up'], ['l0_ffn_down'], ['l1_w_in'], ['l1_w_out'],
                 ['l1_ffn_up'], ['l1_ffn_down']]
CONV = ['l0_sc_conv_w', 'l0_ffn_conv_w', 'l1_ffn_conv_w']
SMALL = [n for n in WEIGHTS if n not in BIG]
LATE_SMALL = ['l0_sc_conv_w', 'l0_mix_norm_g']
IN_CD = 5 * HALF + N_HEADS


def _ffn_forward(x, g, get_up, behind_act, get_down, conv_w, tag):
    h = _rmsnorm_fwd(x, g, tag + "_norm")
    u = _matmul(h, get_up(h), "nn", F32, tag + "_up")
    f = _ffn_act_fwd(u, conv_w, tag + "_act", after=behind_act(u))
    w_down, tokens = get_down(f)
    return _matmul(f, w_down, "nn", F32, tag + "_down", res=x, after=tokens), (h, u, f)


def _ffn_backward(x, g, w_up, conv_w, w_down, saved, d_out, send_up, send_down, tag):
    h, u, f = saved
    dw_down = _matmul(f, d_out, "tn", BF16, tag + "_dwdown")
    d_f = _matmul(d_out, w_down, "nt", F32, tag + "_df", after=[send_down(dw_down)])
    du, dcw_gate, dcw_up = _ffn_act_bwd(u, conv_w, d_f, tag + "_dact")
    dw_up = _matmul(h, du, "tn", BF16, tag + "_dwup")
    dh = _matmul(du, w_up, "nt", F32, tag + "_dh", after=[send_up(dw_up)])
    dx, dg = _rmsnorm_bwd(x, g, dh, d_out, tag + "_dnorm")
    return dx, dg, jnp.concatenate([dcw_gate, dcw_up], axis=1)


def kernel(x, l0_mix_norm_g, l0_w_in, l0_sc_conv_w, l0_w_out, l0_ffn_norm_g, l0_ffn_up, l0_ffn_conv_w, l0_ffn_down, l1_mix_norm_g, l1_w_in, l1_fox_b_f, l1_sg_w, l1_sg_b, l1_sg_norm_g, l1_w_out, l1_ffn_norm_g, l1_ffn_up, l1_ffn_conv_w, l1_ffn_down, final_norm_g, loss_target, m_l0_mix_norm_g, m_l0_w_in, m_l0_sc_conv_w, m_l0_w_out, m_l0_ffn_norm_g, m_l0_ffn_up, m_l0_ffn_conv_w, m_l0_ffn_down, m_l1_mix_norm_g, m_l1_w_in, m_l1_fox_b_f, m_l1_sg_w, m_l1_sg_b, m_l1_sg_norm_g, m_l1_w_out, m_l1_ffn_norm_g, m_l1_ffn_up, m_l1_ffn_conv_w, m_l1_ffn_down, m_final_norm_g, v_l0_mix_norm_g, v_l0_w_in, v_l0_sc_conv_w, v_l0_w_out, v_l0_ffn_norm_g, v_l0_ffn_up, v_l0_ffn_conv_w, v_l0_ffn_down, v_l1_mix_norm_g, v_l1_w_in, v_l1_fox_b_f, v_l1_sg_w, v_l1_sg_b, v_l1_sg_norm_g, v_l1_w_out, v_l1_ffn_norm_g, v_l1_ffn_up, v_l1_ffn_conv_w, v_l1_ffn_down, v_final_norm_g):
    given = (l0_mix_norm_g, l0_w_in, l0_sc_conv_w, l0_w_out, l0_ffn_norm_g, l0_ffn_up, l0_ffn_conv_w, l0_ffn_down, l1_mix_norm_g, l1_w_in, l1_fox_b_f, l1_sg_w, l1_sg_b, l1_sg_norm_g, l1_w_out, l1_ffn_norm_g, l1_ffn_up, l1_ffn_conv_w, l1_ffn_down, final_norm_g)
    given_m = (m_l0_mix_norm_g, m_l0_w_in, m_l0_sc_conv_w, m_l0_w_out, m_l0_ffn_norm_g, m_l0_ffn_up, m_l0_ffn_conv_w, m_l0_ffn_down, m_l1_mix_norm_g, m_l1_w_in, m_l1_fox_b_f, m_l1_sg_w, m_l1_sg_b, m_l1_sg_norm_g, m_l1_w_out, m_l1_ffn_norm_g, m_l1_ffn_up, m_l1_ffn_conv_w, m_l1_ffn_down, m_final_norm_g)
    given_v = (v_l0_mix_norm_g, v_l0_w_in, v_l0_sc_conv_w, v_l0_w_out, v_l0_ffn_norm_g, v_l0_ffn_up, v_l0_ffn_conv_w, v_l0_ffn_down, v_l1_mix_norm_g, v_l1_w_in, v_l1_fox_b_f, v_l1_sg_w, v_l1_sg_b, v_l1_sg_norm_g, v_l1_w_out, v_l1_ffn_norm_g, v_l1_ffn_up, v_l1_ffn_conv_w, v_l1_ffn_down, v_final_norm_g)
    wt = dict(zip(WEIGHTS, given))
    mom = dict(zip(WEIGHTS, given_m))
    var = dict(zip(WEIGHTS, given_v))
    s = x.shape[1]
    t = ATT_BLOCK
    x0, target = x[0], loss_target[0]
    chip = 2 * lax.axis_index("x") + lax.axis_index("y")

    pos = jnp.stack([chip, lax.axis_index("c")]).astype(jnp.int32)

    conv_shard = jnp.concatenate([wt[n] for n in CONV], axis=1)
    gathers, token = [], ()
    for gi, names in enumerate(GATHER_GROUPS):
        placed = [_place_shard(pos, wt[n], BIG[n], "place_" + n, token) for n in names]
        send, recv, thru, tok = _gather_start(placed, [BIG[n] for n in names], "gather_start_%d" % gi, token)
        gathers.append((send, recv, thru))
        if gi == 0:
            conv_started = _gather_small_start(conv_shard, [tok])
            tok = conv_started[3]
        token = [tok]
    token = token[0]
    full = {}
    conv_full = {}

    def finish_conv(after):
        landed = _gather_small_finish(*conv_started[:3], after)
        conv_all = lax.dynamic_update_slice(landed, conv_shard[None], (chip, 0, 0))
        at = 0
        for n in CONV:
            cw = wt[n].shape[1]
            conv_full[n] = jnp.transpose(conv_all[:, :, at:at + cw], (1, 0, 2)).reshape(3, 4 * cw)
            at += cw

    def forward_gather(gi, after):
        send, recv, thru = gathers[gi]
        kinds = [BIG[n] for n in GATHER_GROUPS[gi]]
        gathers[gi] = _gather_forward(thru, kinds, send, recv, after, "gather_forward_%d" % gi)
        return gathers[gi][3]

    def finish_gather(gi, after):
        send, recv, thru, tok = gathers[gi]
        names = GATHER_GROUPS[gi]
        wholes = _gather_finish(thru, [BIG[n] for n in names], send, recv, tok if after is None else after,
                                "gather_finish_%d" % gi)
        full.update(zip(names, wholes))

    def vec(name):
        return wt[name].reshape(1, -1)

    h0 = _rmsnorm_fwd(x0, vec('l0_mix_norm_g'), "l0_mix_norm", after=[token])
    forward_gather(0, h0)
    finish_gather(0, None)
    p0 = _matmul(h0, full['l0_w_in'], "nn", F32, "l0_in")
    a_out, sb_carries = _sb_fwd(p0, "l0_sb", after=[forward_gather(1, p0)])
    finish_gather(1, a_out)
    finish_conv(p0)
    b_out = _sc_fwd(p0, conv_full['l0_sc_conv_w'], "l0_sc")
    ab0 = jnp.concatenate([a_out.astype(BF16), b_out], axis=1)
    x1 = _matmul(ab0, full['l0_w_out'], "nn", F32, "l0_out", res=x0, after=[forward_gather(2, b_out)])

    def ffn_weights(up_group, next_group):
        def get_up(h):
            finish_gather(up_group, h)
            return full[GATHER_GROUPS[up_group][0]]

        def behind_act(u):
            return [forward_gather(up_group + 1, u)]

        def get_down(f):
            finish_gather(up_group + 1, f)
            return full[GATHER_GROUPS[up_group + 1][0]], ([forward_gather(next_group, f)] if next_group else ())

        return get_up, behind_act, get_down

    x2, ffn0_saved = _ffn_forward(x1, vec('l0_ffn_norm_g'), *ffn_weights(2, 4), conv_full['l0_ffn_conv_w'], "l0_ffn")
    h2 = _rmsnorm_fwd(x2, vec('l1_mix_norm_g'), "l1_mix_norm")
    finish_gather(4, h2)
    w_in1 = jnp.transpose(full['l1_w_in'], (1, 0, 2)).reshape(D_MODEL, IN_CD)
    w_in1_main = w_in1[:, :5 * HALF]
    w_in1_f = jnp.pad(w_in1[:, 5 * HALF:], ((0, 0), (0, 128 - N_HEADS)))
    p1 = _matmul(h2, w_in1_main, "nn", F32, "l1_in")
    f_logit = _matmul(h2, w_in1_f, "nn", F32, "l1_in_f", after=[forward_gather(5, p1)])
    b_f = jnp.pad(wt['l1_fox_b_f'], (0, 128 - N_HEADS)).reshape(1, 128)
    c_heads = _fox_prep(f_logit, b_f, "l1_fox_prep")[:, :N_HEADS].T
    c_col = c_heads[:, :, None]
    c_row = c_heads.reshape(N_HEADS, s // t, 1, t)
    sg_bias = jnp.repeat(wt['l1_sg_b'].T, HEAD, axis=1)
    sg_gain = vec('l1_sg_norm_g')
    c_out = _sg_fwd(p1, wt['l1_sg_w'], sg_bias, sg_gain, "l1_sg")
    d_out, lse = _fox_fwd(p1, c_col, c_row, "l1_fox", after=[forward_gather(6, c_out)])
    finish_gather(5, d_out)
    cd1 = jnp.concatenate([c_out, d_out.astype(BF16)], axis=1)
    x3 = _matmul(cd1, full['l1_w_out'], "nn", F32, "l1_out", res=x2)
    x4, ffn1_saved = _ffn_forward(x3, vec('l1_ffn_norm_g'), *ffn_weights(6, None), conv_full['l1_ffn_conv_w'], "l1_ffn")
    dx4, dg_final, loss_part = _loss_head(x4, vec('final_norm_g'), target, "loss_head")

    grads = {'final_norm_g': dg_final}
    scatters = []

    def send_grads(names):
        def start(*group):
            send, recv, thru, lands, tok = _scatter_start(list(group), [BIG[n] for n in names],
                                                          "scatter_start_%d" % len(scatters))
            scatters.append((names, send, recv, thru, lands))
            return tok
        return start

    dx3, grads['l1_ffn_norm_g'], grads['l1_ffn_conv_w'] = _ffn_backward(
        x3, vec('l1_ffn_norm_g'), full['l1_ffn_up'], conv_full['l1_ffn_conv_w'], full['l1_ffn_down'], ffn1_saved, dx4,
        send_grads(['l1_ffn_up']), send_grads(['l1_ffn_down']), "l1_ffn")
    dw_out1 = _matmul(cd1, dx3, "tn", BF16, "l1_dwout")
    d_cd = _matmul(dx3, full['l1_w_out'], "nt", F32, "l1_dcd")
    du, dv, grads['l1_sg_w'], db_sg, grads['l1_sg_norm_g'] = _sg_bwd(p1, wt['l1_sg_w'], sg_bias, sg_gain, d_cd, "l1_dsg")
    grads['l1_sg_b'] = db_sg[:, :N_HEADS].T
    dq, dk, dvv, dcol, drow = _fox_bwd(p1, c_col, c_row, lse, d_cd, d_out, "l1_dfox")
    d_f_logit, d_b_f = _fox_post(drow, dcol, f_logit, b_f, "l1_fox_post")
    grads['l1_fox_b_f'] = d_b_f[0, :N_HEADS]
    dp1 = jnp.concatenate([a.astype(BF16) for a in (du, dv, dq, dk, dvv)], axis=1)
    dw_main = _matmul(h2, dp1, "tn", BF16, "l1_dwin")
    dw_f = _matmul(h2, d_f_logit, "tn", BF16, "l1_dwin_f")
    dw_in1 = jnp.concatenate([dw_main, dw_f[:, :N_HEADS]], axis=1)
    dw_in1 = jnp.transpose(dw_in1.reshape(D_MODEL, 4, IN_CD // 4), (1, 0, 2))
    dh2 = _matmul(dp1, w_in1_main, "nt", F32, "l1_dh", after=[send_grads(['l1_w_out', 'l1_w_in'])(dw_out1, dw_in1)])
    dh2 = _matmul(d_f_logit, w_in1_f, "nt", F32, "l1_dh_f", res=dh2)
    dx2, grads['l1_mix_norm_g'] = _rmsnorm_bwd(x2, vec('l1_mix_norm_g'), dh2, dx3, "l1_dmix_norm")
    dx1, grads['l0_ffn_norm_g'], grads['l0_ffn_conv_w'] = _ffn_backward(
        x1, vec('l0_ffn_norm_g'), full['l0_ffn_up'], conv_full['l0_ffn_conv_w'], full['l0_ffn_down'], ffn0_saved, dx2,
        send_grads(['l0_ffn_up']), send_grads(['l0_ffn_down']), "l0_ffn")
    early_names = [n for n in SMALL if n not in LATE_SMALL]
    early = _small_reduce_start(_pack([grads[n] for n in early_names] + [loss_part]), "small_start", [dx1])
    dw_out0 = _matmul(ab0, dx1, "tn", BF16, "l0_dwout", after=[early[3]])
    d_ab = _matmul(dx1, full['l0_w_out'], "nt", F32, "l0_dab", after=[send_grads(['l0_w_out'])(dw_out0)])
    dq0, dk0, dv0 = _sb_bwd(p0, d_ab, sb_carries, "l0_dsb")
    dgb, dgc, dhin, grads['l0_sc_conv_w'] = _sc_bwd(p0, conv_full['l0_sc_conv_w'], d_ab, "l0_dsc")
    dp0 = jnp.concatenate([a.astype(BF16) for a in (dq0, dk0, dv0, dgb, dgc, dhin)], axis=1)
    dw_in0 = _matmul(h0, dp0, "tn", BF16, "l0_dwin")
    dh0 = _matmul(dp0, full['l0_w_in'], "nt", F32, "l0_dh", after=[send_grads(['l0_w_in'])(dw_in0)])
    dx0, grads['l0_mix_norm_g'] = _rmsnorm_bwd(x0, vec('l0_mix_norm_g'), dh0, dx1, "l0_dmix_norm")

    shard_grads, delta, new_m, new_v, swaps = {}, {}, {}, {}, {}

    def reduce_group(gi, after):
        names, send, recv, thru, lands = scatters[gi]
        kinds = [BIG[n] for n in names]
        g_thru, landed = _scatter_wait(thru, lands, kinds, send, recv, after, "scatter_wait_%d" % gi)
        halves = [_sum_partials(pos, g, ld, kd, "sum_" + n) for n, g, ld, kd in zip(names, g_thru, landed, kinds)]
        s_send, s_recv, h_thru, s_lands, tok = _swap_start(halves, "swap_start_%d" % gi)
        swaps[gi] = (names, s_send, s_recv, h_thru, s_lands)
        return tok

    def update_group(gi, after):
        names, s_send, s_recv, h_thru, s_lands = swaps[gi]
        mine, theirs = _swap_wait(h_thru, s_lands, s_send, s_recv, after, "swap_wait_%d" % gi)
        for n, gm, gs in zip(names, mine, theirs):
            shard_grads[n], delta[n], new_m[n], new_v[n] = _adamw_shard(pos, wt[n], gm, gs, mom[n], var[n], "adamw_" + n)
        return [delta[n] for n in names]

    after = reduce_group(2, reduce_group(1, reduce_group(0, dx0)))
    after = update_group(2, update_group(1, update_group(0, after)))
    after = reduce_group(5, reduce_group(4, reduce_group(3, after)))
    after = update_group(5, update_group(4, update_group(3, after)))
    after = reduce_group(6, after)
    def small_shapes(names):
        return [conv_full[n].shape if n in CONV else wt[n].shape for n in names]

    me = (2 * chip + lax.axis_index("c")).astype(jnp.int32).reshape(1)
    early_all = _small_reduce_finish(me, early[0], early[1], early[2], after, "small_early")
    early_sums = _unpack(early_all, small_shapes(early_names) + [loss_part.shape])
    loss = early_sums[-1][0, 0]
    late_all = _allreduce_small(_pack([grads[n] for n in LATE_SMALL]), [early_all])
    small_sums = dict(zip(early_names + LATE_SMALL, early_sums[:-1] + _unpack(late_all, small_shapes(LATE_SMALL))))
    for n in SMALL:
        g = small_sums[n]
        shard_grads[n] = lax.dynamic_slice_in_dim(g, chip * wt[n].shape[1], wt[n].shape[1], axis=1) if n in CONV else g
    update_group(6, late_all)
    def flat2d(a):
        return a.reshape(-1, a.shape[-1])

    small_out = _adamw_small(*[[flat2d(src[n]) for n in SMALL] for src in (wt, shard_grads, mom, var)], "adamw_small")
    for out, arrays in zip((delta, new_m, new_v), small_out):
        out.update((n, a.reshape(wt[n].shape)) for n, a in zip(SMALL, arrays))

    return (loss, dx0[None], *[shard_grads[n] for n in WEIGHTS], *[delta[n] for n in WEIGHTS],
            *[new_m[n] for n in WEIGHTS], *[new_v[n] for n in WEIGHTS])
```

```python
import jax
import jax.numpy as jnp
from jax import lax
from jax.experimental import pallas as pl
from jax.experimental.pallas import tpu as pltpu

F32 = jnp.float32
BF16 = jnp.bfloat16

D_MODEL = 2048
HEAD = 128
N_HEADS = 8
HALF = N_HEADS * HEAD
D_FF = 5632
EPS = 1e-6
ATT_SCALE = HEAD ** -0.5
ATT_BLOCK = 512
NEG = -1e30

ADAM_LR = 0.001
ADAM_B1 = 0.9
ADAM_B2 = 0.999
ADAM_EPS = 1e-08
ADAM_WD = 0.01
ADAM_STEP = 10

VMEM_LIMIT_BYTES = 48 * 1024 * 1024
MM_VMEM_LIMIT_BYTES = 56 * 1024 * 1024
ATT_VMEM_LIMIT_BYTES = 48 * 1024 * 1024
MESH = pl.DeviceIdType.MESH
HBM_SPEC = pl.BlockSpec(memory_space=pltpu.HBM)
VMEM_SPEC = pl.BlockSpec(memory_space=pltpu.VMEM)


def _pcall(body, after=(), **kw):
    if not after:
        return pl.pallas_call(body, **kw)
    n_in, n_after, inner = len(kw["in_specs"]), len(after), body
    kw["in_specs"] = list(kw["in_specs"]) + [pl.BlockSpec(memory_space=pl.ANY)] * n_after

    def body(*refs):
        inner(*refs[:n_in], *refs[n_in + n_after:])

    call = pl.pallas_call(body, **kw)
    return lambda *args: call(*args, *after)


def _params(*semantics, vmem_limit=VMEM_LIMIT_BYTES):
    return pltpu.CompilerParams(dimension_semantics=semantics, vmem_limit_bytes=vmem_limit)


def _pick(n, cap):
    best = None
    for t in range(128, min(n, cap) + 1, 128):
        if n % t == 0:
            best = t
    return n if best is None else best


def _dot(a, b, dims):
    return lax.dot_general(a, b, (dims, ((), ())), preferred_element_type=F32)


def _dot_nn(a, b):
    return _dot(a, b, ((1,), (0,)))


def _dot_nt(a, b):
    return _dot(a, b, ((1,), (1,)))


def _dot_tn(a, b):
    return _dot(a, b, ((0,), (0,)))


def _split3(x):
    hi = x.astype(BF16)
    r1 = x - hi.astype(F32)
    mid = r1.astype(BF16)
    lo = (r1 - mid.astype(F32)).astype(BF16)
    return hi, mid, lo


def _log_sigmoid(z):
    return jnp.minimum(z, 0.0) - jnp.log1p(jnp.exp(-jnp.abs(z)))


_GELU_K = 0.7978845608028654


def _gelu(x):
    return 0.5 * x * (1.0 + jnp.tanh(_GELU_K * (x + 0.044715 * x * x * x)))


def _gelu_grad(x):
    t = jnp.tanh(_GELU_K * (x + 0.044715 * x * x * x))
    return 0.5 * (1.0 + t) + 0.5 * x * (1.0 - t * t) * _GELU_K * (1.0 + 3.0 * 0.044715 * x * x)


SUBLANES = 8


def _shift_down(x, k):
    rolled = pltpu.roll(x, k, axis=0)
    head = rolled[:SUBLANES]
    head = jnp.where(lax.broadcasted_iota(jnp.int32, head.shape, 0) >= k, head, 0.0)
    return jnp.concatenate([head, rolled[SUBLANES:]], axis=0)


def _shift_up(x, k):
    n = x.shape[0]
    rolled = pltpu.roll(x, n - k, axis=0)
    tail = rolled[n - SUBLANES:]
    tail = jnp.where(lax.broadcasted_iota(jnp.int32, tail.shape, 0) < SUBLANES - k, tail, 0.0)
    return jnp.concatenate([rolled[:n - SUBLANES], tail], axis=0)


def _conv3(s, w, shifted=None):
    s1, s2 = shifted if shifted else (_shift_down(s, 1), _shift_down(s, 2))
    return w[0:1, :] * s2 + w[1:2, :] * s1 + w[2:3, :] * s


def _conv3_transpose(d, w):
    return w[2:3, :] * d + w[1:2, :] * _shift_up(d, 1) + w[0:1, :] * _shift_up(d, 2)


def _conv3_wgrad(d, s, shifted, dw_ref):
    s1, s2 = shifted
    dw_ref[0:1, :] = jnp.sum(d * s2, axis=0, keepdims=True)
    dw_ref[1:2, :] = jnp.sum(d * s1, axis=0, keepdims=True)
    dw_ref[2:3, :] = jnp.sum(d * s, axis=0, keepdims=True)


MM_TILE_M, MM_TILE_N, MM_TILE_K = 1408, 512, 5632


def _matmul(a, b, mode, out_dtype, name, res=None, after=()):
    a_parts = a.shape[0] if a.ndim == 3 else 1
    b_parts = b.shape[0] if b.ndim == 3 else 1
    a_shape = (a.shape[1], a_parts * a.shape[2]) if a.ndim == 3 else a.shape
    b_shape = (b.shape[1], b_parts * b.shape[2]) if b.ndim == 3 else b.shape
    assert (a_parts == 1 or mode != "tn") and (b_parts == 1 or mode == "tn")
    if mode == "nn":
        (m, k), (k2, n) = a_shape, b_shape
    elif mode == "nt":
        (m, k), (n, k2) = a_shape, b_shape
    else:
        (k, m), (k2, n) = a_shape, b_shape
    assert k == k2, (a.shape, b.shape, mode)
    tm, tn, tk = _pick(m, MM_TILE_M), _pick(n // b_parts, MM_TILE_N), _pick(k // a_parts, MM_TILE_K)
    nk = k // tk
    if mode == "tn":
        a_spec = pl.BlockSpec((tk, tm), lambda i, j, kk: (kk, i))
    elif a_parts > 1:
        per = nk // a_parts
        a_spec = pl.BlockSpec((None, tm, tk), lambda i, j, kk: (kk // per, i, kk % per))
    else:
        a_spec = pl.BlockSpec((tm, tk), lambda i, j, kk: (i, kk))
    if mode == "nt":
        b_spec = pl.BlockSpec((tn, tk), lambda i, j, kk: (j, kk))
    elif b_parts > 1:
        per = n // b_parts // tn
        b_spec = pl.BlockSpec((None, tk, tn), lambda i, j, kk: (j // per, kk, j % per))
    else:
        b_spec = pl.BlockSpec((tk, tn), lambda i, j, kk: (kk, j))
    o_spec = pl.BlockSpec((tm, tn), lambda i, j, kk: (i, j))
    dims = {"nn": ((1,), (0,)), "nt": ((1,), (1,)), "tn": ((0,), (0,))}[mode]
    has_res = res is not None

    def body(*refs):
        a_ref, b_ref = refs[0], refs[1]
        r_ref = refs[2] if has_res else None
        o_ref = refs[3] if has_res else refs[2]
        part = _dot(a_ref[...].astype(BF16), b_ref[...].astype(BF16), dims)

        def finish(total):
            if has_res:
                total = total + r_ref[...]
            o_ref[...] = total.astype(out_dtype)

        if nk == 1:
            finish(part)
        else:
            acc_ref = refs[-1]
            kk = pl.program_id(2)

            @pl.when(kk == 0)
            def _():
                acc_ref[...] = part

            @pl.when(kk > 0)
            def _():
                acc_ref[...] += part

            @pl.when(kk == nk - 1)
            def _():
                finish(acc_ref[...])

    in_specs = [a_spec, b_spec] + ([o_spec] if has_res else [])
    args = (a, b) + ((res,) if has_res else ())
    return _pcall(
        body, after=after, name=name, grid=(m // tm, n // tn, nk),
        in_specs=in_specs, out_specs=o_spec,
        out_shape=jax.ShapeDtypeStruct((m, n), out_dtype),
        scratch_shapes=[pltpu.VMEM((tm, tn), F32)] if nk > 1 else [],
        compiler_params=_params("parallel", "parallel", "arbitrary", vmem_limit=MM_VMEM_LIMIT_BYTES),
    )(*args)


ROW_TILE = 256


def _rmsnorm_fwd(x, g, name, after=()):
    s, d = x.shape

    def body(x_ref, g_ref, o_ref):
        xf = x_ref[...]
        r = lax.rsqrt(jnp.mean(xf * xf, axis=-1, keepdims=True) + EPS)
        o_ref[...] = (xf * r * g_ref[...]).astype(BF16)

    row = pl.BlockSpec((ROW_TILE, d), lambda i: (i, 0))
    vec = pl.BlockSpec((1, d), lambda i: (0, 0))
    return _pcall(body, after=after, name=name, grid=(s // ROW_TILE,), in_specs=[row, vec], out_specs=row,
                  out_shape=jax.ShapeDtypeStruct((s, d), BF16), compiler_params=_params("parallel"))(x, g)


def _rmsnorm_bwd(x, g, dh, dres, name):
    s, d = x.shape

    def body(x_ref, g_ref, dh_ref, dres_ref, dx_ref, dg_ref):
        xf = x_ref[...]
        r = lax.rsqrt(jnp.mean(xf * xf, axis=-1, keepdims=True) + EPS)
        xhat = xf * r
        dh_v = dh_ref[...]
        dxh = dh_v * g_ref[...]
        proj = jnp.mean(dxh * xhat, axis=-1, keepdims=True)
        dx_ref[...] = dres_ref[...] + r * (dxh - xhat * proj)
        part = jnp.sum(dh_v * xhat, axis=0, keepdims=True)

        @pl.when(pl.program_id(0) == 0)
        def _():
            dg_ref[...] = part

        @pl.when(pl.program_id(0) > 0)
        def _():
            dg_ref[...] += part

    row = pl.BlockSpec((ROW_TILE, d), lambda i: (i, 0))
    vec = pl.BlockSpec((1, d), lambda i: (0, 0))
    return _pcall(body, name=name, grid=(s // ROW_TILE,), in_specs=[row, vec, row, row], out_specs=[row, vec],
                  out_shape=[jax.ShapeDtypeStruct((s, d), F32), jax.ShapeDtypeStruct((1, d), F32)],
                  compiler_params=_params("arbitrary"))(x, g, dh, dres)


def _loss_head(x, g, target, name):
    s, d = x.shape

    def body(x_ref, g_ref, t_ref, dx_ref, dg_ref, loss_ref):
        xf = x_ref[...]
        r = lax.rsqrt(jnp.mean(xf * xf, axis=-1, keepdims=True) + EPS)
        xhat = xf * r
        gv = g_ref[...]
        err = xhat * gv - t_ref[...]
        dy = err * (1.0 / d)
        dxh = dy * gv
        proj = jnp.mean(dxh * xhat, axis=-1, keepdims=True)
        dx_ref[...] = r * (dxh - xhat * proj)
        dg_part = jnp.sum(dy * xhat, axis=0, keepdims=True)
        row_loss = jnp.sum(err * err, axis=-1, keepdims=True) * (0.5 / d)
        loss_part = jnp.broadcast_to(jnp.sum(row_loss, axis=0, keepdims=True), (1, 128))

        @pl.when(pl.program_id(0) == 0)
        def _():
            dg_ref[...] = dg_part
            loss_ref[...] = loss_part

        @pl.when(pl.program_id(0) > 0)
        def _():
            dg_ref[...] += dg_part
            loss_ref[...] += loss_part

    row = pl.BlockSpec((ROW_TILE, d), lambda i: (i, 0))
    vec = pl.BlockSpec((1, d), lambda i: (0, 0))
    one = pl.BlockSpec((1, 128), lambda i: (0, 0))
    return _pcall(body, name=name, grid=(s // ROW_TILE,), in_specs=[row, vec, row], out_specs=[row, vec, one],
                  out_shape=[jax.ShapeDtypeStruct((s, d), F32), jax.ShapeDtypeStruct((1, d), F32),
                             jax.ShapeDtypeStruct((1, 128), F32)],
                  compiler_params=_params("arbitrary"))(x, g, target)


HEADS_PER_STEP = 2
GROUP_W = HEADS_PER_STEP * HEAD
N_GROUPS = N_HEADS // HEADS_PER_STEP


def _head_cols(h):
    return slice(h * HEAD, (h + 1) * HEAD)


def _head_specs(s, col0):
    t = ATT_BLOCK
    g0 = [c // HEADS_PER_STEP for c in col0]
    qspec = pl.BlockSpec((t, GROUP_W), lambda g, i: (i, g0[0] + g))
    kspec = pl.BlockSpec((s, GROUP_W), lambda g, i: (0, g0[1] + g))
    vspec = pl.BlockSpec((s, GROUP_W), lambda g, i: (0, g0[2] + g))
    return qspec, kspec, vspec


TRI = 256


def _order_matrix(later):
    r, c = lax.broadcasted_iota(jnp.int32, (TRI, TRI), 0), lax.broadcasted_iota(jnp.int32, (TRI, TRI), 1)
    return (r > c if later else r < c).astype(BF16)


def _exact_dot(x, m, later):
    parts = [x[:, c:c + TRI] for c in range(0, x.shape[1], TRI)]
    totals = [jnp.sum(p, axis=1, keepdims=True) for p in parts] if len(parts) > 1 else None
    out = []
    for j, p in enumerate(parts):
        hi = p.astype(BF16)
        lo = (p - hi.astype(F32)).astype(BF16)
        acc = _dot_nn(hi, m) + _dot_nn(lo, m)
        for other in (range(j + 1, len(parts)) if later else range(j)):
            acc = acc + totals[other]
        out.append(acc)
    return out[0] if len(out) == 1 else jnp.concatenate(out, axis=1)


def _sb_block(q, kblk, carry_l, u, diagonal):
    t = ATT_BLOCK
    z = _dot_nt(q, kblk) * ATT_SCALE
    sp = jnp.maximum(z, 0.0) + jnp.log(1.0 + jnp.exp(-jnp.abs(z)))
    if not diagonal:
        l = -sp
        return z, None, l, jnp.exp(z + l + _exact_dot(l, u, True) + carry_l)
    mask = lax.broadcasted_iota(jnp.int32, (t, t), 1) < lax.broadcasted_iota(jnp.int32, (t, t), 0)
    l = jnp.where(mask, -sp, 0.0)
    a = jnp.where(mask, jnp.exp(z - sp + _exact_dot(l, u, True) + carry_l), 0.0)
    return z, mask, l, a


def _sb_carry_spec(s):
    t = ATT_BLOCK
    return pl.BlockSpec((HEADS_PER_STEP, None, s // t, t, 1), lambda g, i: (g, i, 0, 0, 0))


def _sb_fwd(p, name, after=()):
    s = p.shape[0]
    t = ATT_BLOCK
    nb = s // t

    def body(q_ref, k_ref, v_ref, o_ref, cl_ref):
        i = pl.program_id(1)
        heads = range(HEADS_PER_STEP)
        q = [q_ref[:, _head_cols(h)].astype(BF16) for h in heads]
        u = _order_matrix(True)
        cl_ref[...] = jnp.zeros_like(cl_ref)

        def tile(kb, carry, diagonal):
            ks = pl.multiple_of(kb * t, t)
            out = []
            for h in heads:
                acc, carry_l = carry[h]
                kblk = k_ref[pl.ds(ks, t), _head_cols(h)].astype(BF16)
                vblk = v_ref[pl.ds(ks, t), _head_cols(h)].astype(BF16)
                cl_ref[h, kb] = carry_l
                _, _, l, a = _sb_block(q[h], kblk, carry_l, u, diagonal)
                out.append((acc + _dot_nn(a.astype(BF16), vblk), carry_l + jnp.sum(l, axis=1, keepdims=True)))
            return tuple(out)

        carry = tile(i, tuple((jnp.zeros((t, HEAD), F32), jnp.zeros((t, 1), F32)) for _ in heads), True)
        carry = lax.fori_loop(0, i, lambda n, c: tile(i - 1 - n, c, False), carry)
        for h in heads:
            o_ref[:, _head_cols(h)] = carry[h][0].astype(BF16)

    qspec, kspec, vspec = _head_specs(s, (0, N_HEADS, 2 * N_HEADS))
    ospec = pl.BlockSpec((t, GROUP_W), lambda g, i: (i, g))
    return _pcall(body, after=after, name=name, grid=(N_GROUPS, nb), in_specs=[qspec, kspec, vspec],
                  out_specs=[ospec, _sb_carry_spec(s)],
                  out_shape=[jax.ShapeDtypeStruct((s, HALF), BF16), jax.ShapeDtypeStruct((N_HEADS, nb, nb, t, 1), F32)],
                  compiler_params=_params("parallel", "parallel", vmem_limit=ATT_VMEM_LIMIT_BYTES))(p, p, p)


def _sb_bwd(p, d_ab, carries, name):
    s = p.shape[0]
    t = ATT_BLOCK

    def body(q_ref, k_ref, v_ref, do_ref, cl_ref, dq_ref, dk_ref, dv_ref):
        i = pl.program_id(1)

        @pl.when(i == 0)
        def _():
            dk_ref[...] = jnp.zeros_like(dk_ref)
            dv_ref[...] = jnp.zeros_like(dv_ref)

        heads = range(HEADS_PER_STEP)
        q = [q_ref[:, _head_cols(h)].astype(BF16) for h in heads]
        do = [do_ref[:, _head_cols(h)].astype(BF16) for h in heads]
        u = _order_matrix(True)
        lower = _order_matrix(False)

        def tile(kb, carry, diagonal):
            ks = pl.multiple_of(kb * t, t)
            out = []
            for h in heads:
                dq, carry_g = carry[h]
                kblk = k_ref[pl.ds(ks, t), _head_cols(h)].astype(BF16)
                vblk = v_ref[pl.ds(ks, t), _head_cols(h)].astype(BF16)
                z, mask, _, a = _sb_block(q[h], kblk, cl_ref[h, kb], u, diagonal)
                g = a * _dot_nt(do[h], vblk)
                earlier_g = _exact_dot(g, lower, False) + carry_g
                sig = jax.nn.sigmoid(z)
                dz = g * (1.0 - sig) - sig * earlier_g
                if diagonal:
                    dz = jnp.where(mask, dz, 0.0)
                dz = dz.astype(BF16)
                dv_ref[pl.ds(ks, t), _head_cols(h)] += _dot_tn(a.astype(BF16), do[h])
                dk_ref[pl.ds(ks, t), _head_cols(h)] += _dot_tn(dz, q[h]) * ATT_SCALE
                out.append((dq + _dot_nn(dz, kblk) * ATT_SCALE, carry_g + jnp.sum(g, axis=1, keepdims=True)))
            return tuple(out)

        init = tuple((jnp.zeros((t, HEAD), F32), jnp.zeros((t, 1), F32)) for _ in heads)
        carry = tile(i, lax.fori_loop(0, i, lambda kb, c: tile(kb, c, False), init), True)
        for h in heads:
            dq_ref[:, _head_cols(h)] = carry[h][0]

    qspec, kspec, vspec = _head_specs(s, (0, N_HEADS, 2 * N_HEADS))
    blk = pl.BlockSpec((t, GROUP_W), lambda g, i: (i, g))
    whole = pl.BlockSpec((s, GROUP_W), lambda g, i: (0, g))
    shape = jax.ShapeDtypeStruct((s, HALF), F32)
    return _pcall(body, name=name, grid=(N_GROUPS, s // t), in_specs=[qspec, kspec, vspec, blk, _sb_carry_spec(s)],
                  out_specs=[blk, whole, whole], out_shape=[shape, shape, shape],
                  compiler_params=_params("parallel", "arbitrary", vmem_limit=ATT_VMEM_LIMIT_BYTES))(p, p, p, d_ab, carries)


COL_TILE = 256


def _sc_fwd(p, w, name):
    s = p.shape[0]
    nb = HALF // COL_TILE

    def body(gb_ref, gc_ref, h_ref, w_ref, o_ref):
        conv = _conv3(gc_ref[...] * h_ref[...], w_ref[...])
        o_ref[...] = (gb_ref[...] * conv).astype(BF16)

    def col(k):
        return pl.BlockSpec((s, COL_TILE), lambda j: (0, k * nb + j))

    wspec = pl.BlockSpec((3, COL_TILE), lambda j: (0, j))
    return _pcall(body, name=name, grid=(nb,), in_specs=[col(3), col(4), col(5), wspec], out_specs=col(0),
                  out_shape=jax.ShapeDtypeStruct((s, HALF), BF16), compiler_params=_params("parallel"))(p, p, p, w)


def _sc_bwd(p, w, d_ab, name):
    s = p.shape[0]
    nb = HALF // COL_TILE

    def body(gb_ref, gc_ref, h_ref, w_ref, d_ref, dgb_ref, dgc_ref, dh_ref, dw_ref):
        gc, hin, wv, d = gc_ref[...], h_ref[...], w_ref[...], d_ref[...]
        sig = gc * hin
        shifted = (_shift_down(sig, 1), _shift_down(sig, 2))
        dgb_ref[...] = d * _conv3(sig, wv, shifted)
        dconv = d * gb_ref[...]
        _conv3_wgrad(dconv, sig, shifted, dw_ref)
        dsig = _conv3_transpose(dconv, wv)
        dgc_ref[...] = dsig * hin
        dh_ref[...] = dsig * gc

    def col(k):
        return pl.BlockSpec((s, COL_TILE), lambda j: (0, k * nb + j))

    wspec = pl.BlockSpec((3, COL_TILE), lambda j: (0, j))
    act = jax.ShapeDtypeStruct((s, HALF), F32)
    return _pcall(body, name=name, grid=(nb,), in_specs=[col(3), col(4), col(5), wspec, col(1)],
                  out_specs=[col(0), col(0), col(0), wspec],
                  out_shape=[act, act, act, jax.ShapeDtypeStruct((3, HALF), F32)],
                  compiler_params=_params("parallel"))(p, p, p, w, d_ab)


def _ffn_act_fwd(u, w, name, after=()):
    s = u.shape[0]
    nb = D_FF // COL_TILE

    def body(ug_ref, uu_ref, wg_ref, wu_ref, o_ref):
        gate = _conv3(ug_ref[...], wg_ref[...])
        up = _conv3(uu_ref[...], wu_ref[...])
        o_ref[...] = (gate * jax.nn.sigmoid(gate) * up).astype(BF16)

    def col(k):
        return pl.BlockSpec((s, COL_TILE), lambda j: (0, k * nb + j))

    def wcol(k):
        return pl.BlockSpec((3, COL_TILE), lambda j: (0, k * nb + j))

    return _pcall(body, after=after, name=name, grid=(nb,), in_specs=[col(0), col(1), wcol(0), wcol(1)], out_specs=col(0),
                  out_shape=jax.ShapeDtypeStruct((s, D_FF), BF16),
                  compiler_params=_params("parallel"))(u, u, w, w)


def _ffn_act_bwd(u, w, d_f, name):
    s = u.shape[0]
    nb = D_FF // COL_TILE

    def body(ug_ref, uu_ref, wg_ref, wu_ref, d_ref, du_ref, dwg_ref, dwu_ref):
        ug, uu, wg, wu, d = ug_ref[...], uu_ref[...], wg_ref[...], wu_ref[...], d_ref[...]
        ug_shifted = (_shift_down(ug, 1), _shift_down(ug, 2))
        uu_shifted = (_shift_down(uu, 1), _shift_down(uu, 2))
        gate = _conv3(ug, wg, ug_shifted)
        up = _conv3(uu, wu, uu_shifted)
        sig = jax.nn.sigmoid(gate)
        d_up = d * gate * sig
        d_gate = d * up * sig * (1.0 + gate * (1.0 - sig))
        _conv3_wgrad(d_gate, ug, ug_shifted, dwg_ref)
        _conv3_wgrad(d_up, uu, uu_shifted, dwu_ref)
        du_ref[0] = _conv3_transpose(d_gate, wg).astype(BF16)
        du_ref[1] = _conv3_transpose(d_up, wu).astype(BF16)

    def col(k):
        return pl.BlockSpec((s, COL_TILE), lambda j: (0, k * nb + j))

    def wcol(k):
        return pl.BlockSpec((3, COL_TILE), lambda j: (0, k * nb + j))

    both = pl.BlockSpec((2, s, COL_TILE), lambda j: (0, 0, j))
    wsh = jax.ShapeDtypeStruct((3, D_FF), F32)
    return _pcall(body, name=name, grid=(nb,), in_specs=[col(0), col(1), wcol(0), wcol(1), col(0)],
                  out_specs=[both, wcol(0), wcol(0)], out_shape=[jax.ShapeDtypeStruct((2, s, D_FF), BF16), wsh, wsh],
                  compiler_params=_params("parallel"))(u, u, w, w, d_f)


def _sg_common(u, v, g, w_ref, bias, mixed_ref):
    rows = u.shape[0]
    gu = _gelu(u)
    gv = _gelu(v)
    xc = gv - jnp.mean(gv, axis=-1, keepdims=True)
    rstd = lax.rsqrt(jnp.mean(xc * xc, axis=-1, keepdims=True) + EPS)
    xhat = xc * rstd
    vn = xhat * g
    tril = lax.broadcasted_iota(jnp.int32, (HEAD, HEAD), 0) >= lax.broadcasted_iota(jnp.int32, (HEAD, HEAD), 1)
    wts = [jnp.where(tril, w_ref[grp], 0.0).astype(BF16) for grp in range(N_HEADS)]
    for n in range(rows // HEAD):
        for grp in range(N_HEADS):
            blk = vn[n * HEAD:(n + 1) * HEAD, grp * HEAD:(grp + 1) * HEAD].astype(BF16)
            mixed_ref[n * HEAD:(n + 1) * HEAD, grp * HEAD:(grp + 1) * HEAD] = _dot_nn(wts[grp], blk)
    mixed = mixed_ref[...] + jnp.concatenate([bias] * (rows // HEAD), axis=0)
    return gu, xhat, rstd, vn, mixed, wts, tril


def _sg_fwd(p, sg_w, bias, g, name):
    s = p.shape[0]

    def body(u_ref, v_ref, w_ref, b_ref, g_ref, o_ref, mixed_ref):
        gu, _, _, _, mixed, _, _ = _sg_common(u_ref[...], v_ref[...], g_ref[...], w_ref, b_ref[...], mixed_ref)
        o_ref[...] = (gu * mixed).astype(BF16)

    def half(k):
        return pl.BlockSpec((ROW_TILE, HALF), lambda i: (i, k))

    wspec = pl.BlockSpec((N_HEADS, HEAD, HEAD), lambda i: (0, 0, 0))
    bspec = pl.BlockSpec((HEAD, HALF), lambda i: (0, 0))
    gspec = pl.BlockSpec((1, HALF), lambda i: (0, 0))
    return _pcall(body, name=name, grid=(s // ROW_TILE,), in_specs=[half(0), half(1), wspec, bspec, gspec],
                  out_specs=half(0), out_shape=jax.ShapeDtypeStruct((s, HALF), BF16),
                  scratch_shapes=[pltpu.VMEM((ROW_TILE, HALF), F32)],
                  compiler_params=_params("parallel"))(p, p, sg_w, bias, g)


def _sg_bwd(p, sg_w, bias, g, d_cd, name):
    s = p.shape[0]
    nsteps = s // ROW_TILE

    def body(u_ref, v_ref, w_ref, b_ref, g_ref, d_ref, du_ref, dv_ref, dw_ref, db_ref, dg_ref,
             mixed_ref, dvn_ref, dbias_ref):
        i = pl.program_id(0)
        u, v, gain, d = u_ref[...], v_ref[...], g_ref[...], d_ref[...]
        gu, xhat, rstd, vn, mixed, wts, tril = _sg_common(u, v, gain, w_ref, b_ref[...], mixed_ref)

        @pl.when(i == 0)
        def _():
            dw_ref[...] = jnp.zeros_like(dw_ref)
            dg_ref[...] = jnp.zeros_like(dg_ref)
            dbias_ref[...] = jnp.zeros_like(dbias_ref)

        du_ref[...] = d * mixed * _gelu_grad(u)
        dm = d * gu
        for n in range(ROW_TILE // HEAD):
            rs = slice(n * HEAD, (n + 1) * HEAD)
            dbias_ref[...] += dm[rs, :]
            for grp in range(N_HEADS):
                cs = slice(grp * HEAD, (grp + 1) * HEAD)
                dm_blk = dm[rs, cs].astype(BF16)
                dw_ref[grp] += jnp.where(tril, _dot_nt(dm_blk, vn[rs, cs].astype(BF16)), 0.0)
                dvn_ref[rs, cs] = _dot_tn(wts[grp], dm_blk)
        dvn = dvn_ref[...]
        dg_ref[...] += jnp.sum(dvn * xhat, axis=0, keepdims=True)
        dxh = dvn * gain
        d_gv = rstd * (dxh - jnp.mean(dxh, axis=-1, keepdims=True) - xhat * jnp.mean(dxh * xhat, axis=-1, keepdims=True))
        dv_ref[...] = d_gv * _gelu_grad(v)

        @pl.when(i == nsteps - 1)
        def _():
            lane = lax.broadcasted_iota(jnp.int32, (HEAD, HEAD), 1)
            out = jnp.zeros((HEAD, HEAD), F32)
            for grp in range(N_HEADS):
                tot = jnp.sum(dbias_ref[:, grp * HEAD:(grp + 1) * HEAD], axis=1, keepdims=True)
                out = out + jnp.where(lane == grp, tot, 0.0)
            db_ref[...] = out

    def half(k):
        return pl.BlockSpec((ROW_TILE, HALF), lambda i: (i, k))

    wspec = pl.BlockSpec((N_HEADS, HEAD, HEAD), lambda i: (0, 0, 0))
    bspec = pl.BlockSpec((HEAD, HALF), lambda i: (0, 0))
    gspec = pl.BlockSpec((1, HALF), lambda i: (0, 0))
    dbspec = pl.BlockSpec((HEAD, HEAD), lambda i: (0, 0))
    act = jax.ShapeDtypeStruct((s, HALF), F32)
    return _pcall(body, name=name, grid=(nsteps,), in_specs=[half(0), half(1), wspec, bspec, gspec, half(0)],
                  out_specs=[half(0), half(0), wspec, dbspec, gspec],
                  out_shape=[act, act, jax.ShapeDtypeStruct((N_HEADS, HEAD, HEAD), F32),
                             jax.ShapeDtypeStruct((HEAD, HEAD), F32), jax.ShapeDtypeStruct((1, HALF), F32)],
                  scratch_shapes=[pltpu.VMEM((ROW_TILE, HALF), F32), pltpu.VMEM((ROW_TILE, HALF), F32),
                                  pltpu.VMEM((HEAD, HALF), F32)],
                  compiler_params=_params("arbitrary"))(p, p, sg_w, bias, g, d_cd)


def _fox_prep(f, b, name):
    s = f.shape[0]
    t = ATT_BLOCK

    def body(f_ref, b_ref, c_ref):
        tri = (lax.broadcasted_iota(jnp.int32, (t, t), 0) >= lax.broadcasted_iota(jnp.int32, (t, t), 1)).astype(BF16)
        carry = jnp.zeros((1, 128), F32)
        for n in range(s // t):
            lf = _log_sigmoid(f_ref[n * t:(n + 1) * t, :] + b_ref[...])
            hi, mid, lo = _split3(lf)
            c_ref[n * t:(n + 1) * t, :] = _dot_nn(tri, hi) + _dot_nn(tri, mid) + _dot_nn(tri, lo) + carry
            carry = carry + jnp.sum(lf, axis=0, keepdims=True)

    return _pcall(body, name=name, in_specs=[VMEM_SPEC, VMEM_SPEC], out_specs=VMEM_SPEC,
                  out_shape=jax.ShapeDtypeStruct((s, 128), F32))(f, b)


def _fox_post(drow, dcol, f, b, name):
    s = f.shape[0]
    t = ATT_BLOCK

    def body(drow_ref, dcol_ref, f_ref, b_ref, df_ref, db_ref):
        tri = (lax.broadcasted_iota(jnp.int32, (t, t), 1) >= lax.broadcasted_iota(jnp.int32, (t, t), 0)).astype(BF16)
        lane = lax.broadcasted_iota(jnp.int32, (t, 128), 1)
        carry = jnp.zeros((1, 128), F32)
        db = jnp.zeros((1, 128), F32)
        for n in reversed(range(s // t)):
            rs = slice(n * t, (n + 1) * t)
            dc = jnp.zeros((t, 128), F32)
            for h in range(N_HEADS):
                dc = jnp.where(lane == h, drow_ref[h, rs, :] - dcol_ref[rs, _head_cols(h)], dc)
            hi, mid, lo = _split3(dc)
            dlogf = _dot_nn(tri, hi) + _dot_nn(tri, mid) + _dot_nn(tri, lo) + carry
            carry = carry + jnp.sum(dc, axis=0, keepdims=True)
            df = dlogf * jax.nn.sigmoid(-(f_ref[rs, :] + b_ref[...]))
            df_ref[rs, :] = df
            db = db + jnp.sum(df, axis=0, keepdims=True)
        db_ref[...] = db

    return _pcall(body, name=name, in_specs=[VMEM_SPEC] * 4, out_specs=[VMEM_SPEC, VMEM_SPEC],
                  out_shape=[jax.ShapeDtypeStruct((s, 128), F32), jax.ShapeDtypeStruct((1, 128), F32)],
                  compiler_params=pltpu.CompilerParams(vmem_limit_bytes=VMEM_LIMIT_BYTES))(drow, dcol, f, b)


def _fox_specs(s):
    t = ATT_BLOCK
    ccol = pl.BlockSpec((HEADS_PER_STEP, t, 1), lambda g, i: (g, i, 0))
    crow = pl.BlockSpec((HEADS_PER_STEP, s // t, 1, t), lambda g, i: (g, 0, 0, 0))
    return ccol, crow


def _fox_fwd(p, c_col, c_row, name, after=()):
    s = p.shape[0]
    t = ATT_BLOCK

    def body(q_ref, k_ref, v_ref, cc_ref, cr_ref, o_ref, lse_ref):
        i = pl.program_id(1)
        heads = range(HEADS_PER_STEP)
        q = [q_ref[:, _head_cols(h)].astype(BF16) for h in heads]
        ct = [cc_ref[h] for h in heads]

        def tile(n, carry, diagonal):
            ks = pl.multiple_of(n * t, t)
            out = []
            for h in heads:
                acc, m, l = carry[h]
                kblk = k_ref[pl.ds(ks, t), _head_cols(h)].astype(BF16)
                vblk = v_ref[pl.ds(ks, t), _head_cols(h)].astype(BF16)
                logit = _dot_nt(q[h], kblk) * ATT_SCALE + ct[h] - cr_ref[h, n]
                if diagonal:
                    causal = lax.broadcasted_iota(jnp.int32, (t, t), 1) <= lax.broadcasted_iota(jnp.int32, (t, t), 0)
                    logit = jnp.where(causal, logit, NEG)
                m_new = jnp.maximum(m, jnp.max(logit, axis=1, keepdims=True))
                alpha = jnp.exp(m - m_new)
                pr = jnp.exp(logit - m_new)
                l = alpha * l + jnp.sum(pr, axis=1, keepdims=True)
                out.append((alpha * acc + _dot_nn(pr.astype(BF16), vblk), m_new, l))
            return tuple(out)

        init = tuple((jnp.zeros((t, HEAD), F32), jnp.full((t, 1), NEG, F32), jnp.zeros((t, 1), F32)) for _ in heads)
        carry = tile(i, lax.fori_loop(0, i, lambda n, c: tile(n, c, False), init), True)
        for h in heads:
            acc, m, l = carry[h]
            o_ref[:, _head_cols(h)] = acc / l
            lse_ref[h] = m + jnp.log(l)

    qspec, kspec, vspec = _head_specs(s, (2 * N_HEADS, 3 * N_HEADS, 4 * N_HEADS))
    ccol, crow = _fox_specs(s)
    ospec = pl.BlockSpec((t, GROUP_W), lambda g, i: (i, g))
    return _pcall(body, after=after, name=name, grid=(N_GROUPS, s // t), in_specs=[qspec, kspec, vspec, ccol, crow],
                  out_specs=[ospec, ccol],
                  out_shape=[jax.ShapeDtypeStruct((s, HALF), F32), jax.ShapeDtypeStruct((N_HEADS, s, 1), F32)],
                  compiler_params=_params("parallel", "parallel", vmem_limit=ATT_VMEM_LIMIT_BYTES))(p, p, p, c_col, c_row)


def _fox_bwd(p, c_col, c_row, lse, d_cd, d_out, name):
    s = p.shape[0]
    t = ATT_BLOCK

    def body(q_ref, k_ref, v_ref, cc_ref, cr_ref, lse_ref, do_ref, o_ref, dq_ref, dk_ref, dv_ref, dcol_ref, drow_ref):
        i = pl.program_id(1)

        @pl.when(i == 0)
        def _():
            dk_ref[...] = jnp.zeros_like(dk_ref)
            dv_ref[...] = jnp.zeros_like(dv_ref)
            dcol_ref[...] = jnp.zeros_like(dcol_ref)

        heads = range(HEADS_PER_STEP)
        q = [q_ref[:, _head_cols(h)].astype(BF16) for h in heads]
        do = [do_ref[:, _head_cols(h)].astype(BF16) for h in heads]
        delta = [jnp.sum(do_ref[:, _head_cols(h)] * o_ref[:, _head_cols(h)], axis=1, keepdims=True) for h in heads]
        ct = [cc_ref[h] for h in heads]
        lse_v = [lse_ref[h] for h in heads]
        ones = jnp.ones((t, HEAD), BF16)

        def tile(n, carry, diagonal):
            ks = pl.multiple_of(n * t, t)
            out = []
            for h in heads:
                dq, drow = carry[h]
                kblk = k_ref[pl.ds(ks, t), _head_cols(h)].astype(BF16)
                vblk = v_ref[pl.ds(ks, t), _head_cols(h)].astype(BF16)
                logit = _dot_nt(q[h], kblk) * ATT_SCALE + ct[h] - cr_ref[h, n]
                pr = jnp.exp(logit - lse_v[h])
                if diagonal:
                    causal = lax.broadcasted_iota(jnp.int32, (t, t), 1) <= lax.broadcasted_iota(jnp.int32, (t, t), 0)
                    pr = jnp.where(causal, pr, 0.0)
                ds = pr * (_dot_nt(do[h], vblk) - delta[h])
                dsb = ds.astype(BF16)
                dv_ref[pl.ds(ks, t), _head_cols(h)] += _dot_tn(pr.astype(BF16), do[h])
                dk_ref[pl.ds(ks, t), _head_cols(h)] += _dot_tn(dsb, q[h]) * ATT_SCALE
                dcol_ref[pl.ds(ks, t), _head_cols(h)] += _dot_tn(dsb, ones)
                out.append((dq + _dot_nn(dsb, kblk) * ATT_SCALE,
                            drow + jnp.sum(dsb.astype(F32), axis=1, keepdims=True)))
            return tuple(out)

        init = tuple((jnp.zeros((t, HEAD), F32), jnp.zeros((t, 1), F32)) for _ in heads)
        carry = tile(i, lax.fori_loop(0, i, lambda n, c: tile(n, c, False), init), True)
        for h in heads:
            dq_ref[:, _head_cols(h)] = carry[h][0]
            drow_ref[h] = carry[h][1]

    qspec, kspec, vspec = _head_specs(s, (2 * N_HEADS, 3 * N_HEADS, 4 * N_HEADS))
    ccol, crow = _fox_specs(s)
    dospec = pl.BlockSpec((t, GROUP_W), lambda g, i: (i, N_GROUPS + g))
    blk = pl.BlockSpec((t, GROUP_W), lambda g, i: (i, g))
    whole = pl.BlockSpec((s, GROUP_W), lambda g, i: (0, g))
    shape = jax.ShapeDtypeStruct((s, HALF), F32)
    return _pcall(body, name=name, grid=(N_GROUPS, s // t),
                  in_specs=[qspec, kspec, vspec, ccol, crow, ccol, dospec, blk],
                  out_specs=[blk, whole, whole, whole, ccol],
                  out_shape=[shape, shape, shape, shape, jax.ShapeDtypeStruct((N_HEADS, s, 1), F32)],
                  compiler_params=_params("parallel", "arbitrary", vmem_limit=ATT_VMEM_LIMIT_BYTES))(p, p, p, c_col, c_row, lse, d_cd, d_out)


def _row_tile(rows, cap):
    for t in (256, 128, 64, 32, 16, 8):
        if t <= cap and rows % t == 0:
            return t
    return rows


def _adamw_small(ws, gs, ms, vs, name):
    n = len(ws)
    c1 = 1.0 / (1.0 - ADAM_B1 ** ADAM_STEP)
    c2 = 1.0 / (1.0 - ADAM_B2 ** ADAM_STEP)

    def body(*refs):
        for k in range(n):
            w_ref, g_ref, m_ref, v_ref = (refs[j * n + k] for j in range(4))
            d_ref, nm_ref, nv_ref = (refs[(4 + j) * n + k] for j in range(3))
            gv = g_ref[...]
            nm = ADAM_B1 * m_ref[...] + (1.0 - ADAM_B1) * gv
            nv = ADAM_B2 * v_ref[...] + (1.0 - ADAM_B2) * (gv * gv)
            nm_ref[...] = nm
            nv_ref[...] = nv
            d_ref[...] = -ADAM_LR * ((nm * c1) / (jnp.sqrt(nv * c2) + ADAM_EPS) + ADAM_WD * w_ref[...])

    shapes = [jax.ShapeDtypeStruct(w.shape, F32) for w in ws] * 3
    outs = _pcall(body, name=name, in_specs=[VMEM_SPEC] * (4 * n), out_specs=[VMEM_SPEC] * (3 * n), out_shape=shapes,
                  compiler_params=pltpu.CompilerParams(vmem_limit_bytes=VMEM_LIMIT_BYTES))(*ws, *gs, *ms, *vs)
    return outs[:n], outs[n:2 * n], outs[2 * n:]


def _half_shape(whole_shape, kind):
    if kind == "col":
        return (whole_shape[0] // 2, whole_shape[1] // 4)
    if kind == "row":
        return (whole_shape[0] // 8, whole_shape[1])
    return (whole_shape[1] // 2, whole_shape[2])


def _own_half_spec(whole_shape, kind, tr):
    hr, hc = _half_shape(whole_shape, kind)
    nb = hr // tr
    if kind == "col":
        return pl.BlockSpec((tr, hc), lambda i, pos: (pos[1] * nb + i, pos[0]))
    if kind == "row":
        return pl.BlockSpec((tr, hc), lambda i, pos: ((2 * pos[0] + pos[1]) * nb + i, 0))
    return pl.BlockSpec((None, tr, hc), lambda i, pos: (pos[0], pos[1] * nb + i, 0))


def _sum_partials(pos, grad, landed, kind, name):
    hr, hc = _half_shape(grad.shape, kind)
    tr = _row_tile(hr, 128)

    def body(pos_ref, g_ref, p_ref, o_ref):
        acc = g_ref[...].astype(F32)
        for k in range(N_DEV - 1):
            acc = acc + p_ref[k].astype(F32)
        o_ref[...] = acc

    grid_spec = pltpu.PrefetchScalarGridSpec(
        num_scalar_prefetch=1, grid=(hr // tr,),
        in_specs=[_own_half_spec(grad.shape, kind, tr), pl.BlockSpec((N_DEV - 1, tr, hc), lambda i, pos: (0, i, 0))],
        out_specs=pl.BlockSpec((tr, hc), lambda i, pos: (i, 0)))
    return _pcall(body, name=name, grid_spec=grid_spec, out_shape=jax.ShapeDtypeStruct((hr, hc), F32),
                  compiler_params=_params("parallel"))(pos, grad, landed)


def _adamw_shard(pos, w, g_mine, g_sibling, m, v, name):
    hr, hc = g_mine.shape
    tr = _row_tile(hr, 128)
    nb = hr // tr
    c1 = 1.0 / (1.0 - ADAM_B1 ** ADAM_STEP)
    c2 = 1.0 / (1.0 - ADAM_B2 ** ADAM_STEP)

    def body(pos_ref, w_ref, gm_ref, gs_ref, m_ref, v_ref, g_ref, d_ref, nm_ref, nv_ref):
        mine = (pl.program_id(0) // nb) == pos_ref[1]
        gv = jnp.where(mine, gm_ref[...], gs_ref[...])
        nm = ADAM_B1 * m_ref[...] + (1.0 - ADAM_B1) * gv
        nv = ADAM_B2 * v_ref[...] + (1.0 - ADAM_B2) * (gv * gv)
        g_ref[...] = gv
        nm_ref[...] = nm
        nv_ref[...] = nv
        d_ref[...] = -ADAM_LR * ((nm * c1) / (jnp.sqrt(nv * c2) + ADAM_EPS) + ADAM_WD * w_ref[...])

    full = pl.BlockSpec((tr, hc), lambda i, pos: (i, 0))
    mine_spec = pl.BlockSpec((tr, hc), lambda i, pos: (jnp.clip(i - pos[1] * nb, 0, nb - 1), 0))
    sib_spec = pl.BlockSpec((tr, hc), lambda i, pos: (jnp.clip(i - (1 - pos[1]) * nb, 0, nb - 1), 0))
    grid_spec = pltpu.PrefetchScalarGridSpec(
        num_scalar_prefetch=1, grid=(2 * nb,), in_specs=[full, mine_spec, sib_spec, full, full], out_specs=[full] * 4)
    shape = jax.ShapeDtypeStruct((2 * hr, hc), F32)
    return _pcall(body, name=name, grid_spec=grid_spec, out_shape=[shape] * 4,
                  compiler_params=_params("parallel"))(pos, w, g_mine, g_sibling, m, v)


def _place_shard(pos, shard, kind, name, after=()):
    rows, cols = shard.shape
    tr = _row_tile(rows, 256)
    nb = rows // tr
    if kind == "col":
        out_spec = pl.BlockSpec((tr, cols), lambda i, pos: (i, pos[0]))
    elif kind == "row":
        out_spec = pl.BlockSpec((tr, cols), lambda i, pos: (pos[0] * nb + i, 0))
    else:
        out_spec = pl.BlockSpec((None, tr, cols), lambda i, pos: (pos[0], i, 0))

    def body(pos_ref, s_ref, *rest):
        rest[-1][...] = s_ref[...].astype(BF16)

    grid_spec = pltpu.PrefetchScalarGridSpec(
        num_scalar_prefetch=1, grid=(nb,),
        in_specs=[pl.BlockSpec((tr, cols), lambda i, pos: (i, 0))] + [pl.BlockSpec(memory_space=pl.ANY)] * len(after),
        out_specs=out_spec)
    return _pcall(body, name=name, grid_spec=grid_spec,
                  out_shape=jax.ShapeDtypeStruct(_whole_shape(shard.shape, kind), BF16),
                  compiler_params=_params("parallel"))(pos, shard, *after)


N_DEV = 8
RELATIONS = [(r >> 2 & 1, r >> 1 & 1, r & 1) for r in range(1, N_DEV)]


def _position():
    return lax.axis_index("x"), lax.axis_index("y"), lax.axis_index("c")


def _related(pos, rel):
    return tuple(1 - p if f else p for p, f in zip(pos, rel))


def _index(pos):
    return 4 * pos[0] + 2 * pos[1] + pos[2]


def _window(ref, kind, pos):
    px, py, pc = pos
    j = 2 * px + py
    if kind == "col":
        r, c = ref.shape
        return ref.at[pl.ds(pc * (r // 2), r // 2), pl.ds(pl.multiple_of(j * (c // 4), 128), c // 4)]
    if kind == "row":
        rj = ref.shape[0] // 4
        return ref.at[pl.ds(j * rj + pc * (rj // 2), rj // 2), :]
    r = ref.shape[1]
    return ref.at[j, pl.ds(pc * (r // 2), r // 2), :]


def _whole_shape(shard_shape, kind):
    r, c = shard_shape
    return {"col": (r, 4 * c), "row": (4 * r, c), "maj": (4, r, c)}[kind]


SEM_SPEC = pl.BlockSpec(memory_space=pltpu.SEMAPHORE)
ANY_SPEC = pl.BlockSpec(memory_space=pl.ANY)
DATAFLOW = pltpu.SideEffectType.DATAFLOW_SIDE_EFFECTING
TOKEN = jax.ShapeDtypeStruct((8, 128), F32)


def _hbm(a):
    return pltpu.with_memory_space_constraint(a, pltpu.HBM)


def _chips(x, y):
    return [(1 - x, y), (x, 1 - y), (1 - x, 1 - y)]


def _split_start(body, name, buffers, n_sems, after=()):
    n = len(buffers)

    def wrapped(*refs):
        body(refs[:n], refs[n], refs[n + 1])
        refs[-1][...] = jnp.zeros_like(refs[-1])

    outs = _pcall(
        wrapped, after=after, name=name, in_specs=[HBM_SPEC] * n,
        out_specs=[SEM_SPEC, SEM_SPEC] + [HBM_SPEC] * n + [VMEM_SPEC],
        out_shape=[pltpu.SemaphoreType.DMA(n_sems), pltpu.SemaphoreType.DMA(n_sems)]
        + [pltpu.HBM(b.shape, b.dtype) for b in buffers] + [TOKEN],
        input_output_aliases={i: 2 + i for i in range(n)},
        compiler_params=pltpu.CompilerParams(has_side_effects=DATAFLOW))(*[_hbm(b) for b in buffers])
    return outs[0], outs[1], list(outs[2:2 + n]), outs[2 + n]


def _split_wait(body, name, buffers, send_sems, recv_sems, after):
    n = len(buffers)
    after = list(after) if isinstance(after, (list, tuple)) else [after]

    def wrapped(*refs):
        body(refs[:n], refs[n], refs[n + 1])

    outs = _pcall(
        wrapped, name=name, in_specs=[HBM_SPEC] * n + [SEM_SPEC, SEM_SPEC] + [ANY_SPEC] * len(after),
        out_specs=[HBM_SPEC] * n, out_shape=[pltpu.HBM(b.shape, b.dtype) for b in buffers],
        input_output_aliases={i: i for i in range(n)},
        compiler_params=pltpu.CompilerParams(has_side_effects=DATAFLOW))(*buffers, send_sems, recv_sems, *after)
    return list(outs)


def _gather_start(wholes, kinds, name, after=()):
    def body(w_refs, send_sems, recv_sems):
        x, y, c = _position()
        for w, ref in enumerate(w_refs):
            mine = _window(ref, kinds[w], (x, y, c))
            for k, chip in enumerate(_chips(x, y)):
                pltpu.make_async_remote_copy(src_ref=mine, dst_ref=mine, send_sem=send_sems.at[3 * w + k],
                                             recv_sem=recv_sems.at[3 * w + k], device_id=(*chip, c),
                                             device_id_type=MESH).start()

    return _split_start(body, name, wholes, (3 * len(wholes),), after)


def _gather_forward(wholes, kinds, send1, recv1, after, name):
    n = len(wholes)

    def wrapped(*refs):
        w_refs, s1, r1, s2, r2 = refs[:n], refs[n], refs[n + 1], refs[n + 3], refs[n + 4]
        x, y, c = _position()
        for k, chip in enumerate(_chips(x, y)):
            for w, ref in enumerate(w_refs):
                theirs = _window(ref, kinds[w], (*chip, c))
                pltpu.make_async_remote_copy(src_ref=theirs, dst_ref=theirs, send_sem=s1.at[3 * w + k],
                                             recv_sem=r1.at[3 * w + k], device_id=(*chip, c),
                                             device_id_type=MESH).wait_recv()
                pltpu.make_async_remote_copy(src_ref=theirs, dst_ref=theirs, send_sem=s2.at[3 * w + k],
                                             recv_sem=r2.at[3 * w + k], device_id=(x, y, 1 - c),
                                             device_id_type=MESH).start()
        for w, ref in enumerate(w_refs):
            mine = _window(ref, kinds[w], (x, y, c))
            for k, chip in enumerate(_chips(x, y)):
                pltpu.make_async_remote_copy(src_ref=mine, dst_ref=mine, send_sem=s1.at[3 * w + k],
                                             recv_sem=r1.at[3 * w + k], device_id=(*chip, c),
                                             device_id_type=MESH).wait_send()
        refs[-1][...] = jnp.zeros_like(refs[-1])

    outs = _pcall(
        wrapped, name=name, in_specs=[HBM_SPEC] * n + [SEM_SPEC, SEM_SPEC, ANY_SPEC],
        out_specs=[SEM_SPEC, SEM_SPEC] + [HBM_SPEC] * n + [VMEM_SPEC],
        out_shape=[pltpu.SemaphoreType.DMA((3 * n,)), pltpu.SemaphoreType.DMA((3 * n,))]
        + [pltpu.HBM(b.shape, b.dtype) for b in wholes] + [TOKEN],
        input_output_aliases={i: 2 + i for i in range(n)},
        compiler_params=pltpu.CompilerParams(has_side_effects=DATAFLOW))(*wholes, send1, recv1, after)
    return outs[0], outs[1], list(outs[2:2 + n]), outs[2 + n]


def _gather_finish(wholes, kinds, send2, recv2, after, name):
    def body(w_refs, s2, r2):
        x, y, c = _position()
        for k, chip in enumerate(_chips(x, y)):
            for w, ref in enumerate(w_refs):
                sent = _window(ref, kinds[w], (*chip, c))
                got = _window(ref, kinds[w], (*chip, 1 - c))
                pltpu.make_async_remote_copy(src_ref=sent, dst_ref=got, send_sem=s2.at[3 * w + k],
                                             recv_sem=r2.at[3 * w + k], device_id=(x, y, 1 - c),
                                             device_id_type=MESH).wait()

    return _split_wait(body, name, wholes, send2, recv2, after)


def _gather_small_start(small, after):
    def body(refs, send_sems, recv_sems):
        s_ref, land_ref = refs
        x, y, c = _position()
        for k, chip in enumerate(_chips(x, y)):
            pltpu.make_async_remote_copy(src_ref=s_ref, dst_ref=land_ref.at[2 * x + y], send_sem=send_sems.at[k],
                                         recv_sem=recv_sems.at[k], device_id=(*chip, c), device_id_type=MESH).start()

    return _split_start(body, "gather_small_start", [small, lax.empty((4,) + small.shape, small.dtype)], (3,), after)


def _gather_small_finish(send, recv, thru, after):
    def body(refs, send_sems, recv_sems):
        s_ref, land_ref = refs
        x, y, c = _position()
        for k, chip in enumerate(_chips(x, y)):
            pltpu.make_async_remote_copy(src_ref=s_ref, dst_ref=land_ref.at[2 * chip[0] + chip[1]],
                                         send_sem=send_sems.at[k], recv_sem=recv_sems.at[k], device_id=(*chip, c),
                                         device_id_type=MESH).wait()

    return _split_wait(body, "gather_small_finish", thru, send, recv, after)[1]


def _scatter_copies(g_refs, land_refs, kinds, send_sems, recv_sems):
    me = _position()
    copies = []
    for k, rel in enumerate(RELATIONS):
        peer = _related(me, rel)
        for w, (g_ref, land_ref) in enumerate(zip(g_refs, land_refs)):
            copies.append(pltpu.make_async_remote_copy(
                src_ref=_window(g_ref, kinds[w], peer), dst_ref=land_ref.at[k],
                send_sem=send_sems.at[7 * w + k], recv_sem=recv_sems.at[7 * w + k], device_id=peer,
                device_id_type=MESH))
    return copies


def _scatter_start(grads, kinds, name):
    n = len(grads)
    lands = [lax.empty((N_DEV - 1,) + _half_shape(g.shape, kd), g.dtype) for g, kd in zip(grads, kinds)]

    def body(refs, send_sems, recv_sems):
        for cp in _scatter_copies(refs[:n], refs[n:], kinds, send_sems, recv_sems):
            cp.start()

    send, recv, thru, token = _split_start(body, name, list(grads) + lands, ((N_DEV - 1) * n,))
    return send, recv, thru[:n], thru[n:], token


def _scatter_wait(grads, lands, kinds, send, recv, after, name):
    n = len(grads)

    def body(refs, send_sems, recv_sems):
        for cp in _scatter_copies(refs[:n], refs[n:], kinds, send_sems, recv_sems):
            cp.wait()

    out = _split_wait(body, name, list(grads) + list(lands), send, recv, after)
    return out[:n], out[n:]


def _swap_start(halves, name):
    n = len(halves)
    lands = [lax.empty(h.shape, h.dtype) for h in halves]

    def body(refs, send_sems, recv_sems):
        x, y, c = _position()
        for w in range(n):
            pltpu.make_async_remote_copy(src_ref=refs[w], dst_ref=refs[n + w], send_sem=send_sems.at[w],
                                         recv_sem=recv_sems.at[w], device_id=(x, y, 1 - c), device_id_type=MESH).start()

    send, recv, thru, token = _split_start(body, name, list(halves) + lands, (n,))
    return send, recv, thru[:n], thru[n:], token


def _swap_wait(halves, lands, send, recv, after, name):
    n = len(halves)

    def body(refs, send_sems, recv_sems):
        x, y, c = _position()
        for w in range(n):
            pltpu.make_async_remote_copy(src_ref=refs[w], dst_ref=refs[n + w], send_sem=send_sems.at[w],
                                         recv_sem=recv_sems.at[w], device_id=(x, y, 1 - c), device_id_type=MESH).wait()

    out = _split_wait(body, name, list(halves) + list(lands), send, recv, after)
    return out[:n], out[n:]


def _allreduce_small(v, after=()):
    rows = v.shape[0]

    def body(v_ref, o_ref, recv_ref, send_sems, recv_sems):
        me = _position()
        recv_ref[_index(me)] = v_ref[...]
        sends = []
        for k, rel in enumerate(RELATIONS):
            peer = _related(me, rel)
            cp = pltpu.make_async_remote_copy(
                src_ref=v_ref, dst_ref=recv_ref.at[_index(me)],
                send_sem=send_sems.at[k], recv_sem=recv_sems.at[k], device_id=peer, device_id_type=MESH)
            cp.start()
            sends.append(cp)
        for k, rel in enumerate(RELATIONS):
            peer = _related(me, rel)
            pltpu.make_async_remote_copy(
                src_ref=v_ref, dst_ref=recv_ref.at[_index(peer)],
                send_sem=send_sems.at[k], recv_sem=recv_sems.at[k], device_id=peer, device_id_type=MESH).wait_recv()
        for cp in sends:
            cp.wait_send()
        acc = recv_ref[0]
        for k in range(1, N_DEV):
            acc = acc + recv_ref[k]
        o_ref[...] = acc

    return _pcall(body, after=after, name="allreduce_small", in_specs=[VMEM_SPEC], out_specs=VMEM_SPEC,
                  out_shape=jax.ShapeDtypeStruct((rows, 128), F32),
                  scratch_shapes=[pltpu.VMEM((N_DEV, rows, 128), F32), pltpu.SemaphoreType.DMA((7,)),
                                  pltpu.SemaphoreType.DMA((7,))],
                  compiler_params=pltpu.CompilerParams(vmem_limit_bytes=VMEM_LIMIT_BYTES))(v)


def _small_copies(refs, send_sems, recv_sems):
    v_ref, land_ref = refs
    me = _position()
    return [pltpu.make_async_remote_copy(src_ref=v_ref, dst_ref=land_ref.at[_index(me)], send_sem=send_sems.at[k],
                                         recv_sem=recv_sems.at[k], device_id=_related(me, rel), device_id_type=MESH)
            for k, rel in enumerate(RELATIONS)]


def _small_wait_copies(refs, send_sems, recv_sems):
    v_ref, land_ref = refs
    me = _position()
    return [pltpu.make_async_remote_copy(src_ref=v_ref, dst_ref=land_ref.at[_index(_related(me, rel))],
                                         send_sem=send_sems.at[k], recv_sem=recv_sems.at[k],
                                         device_id=_related(me, rel), device_id_type=MESH)
            for k, rel in enumerate(RELATIONS)]


def _small_reduce_start(v, name, after):
    def body(refs, send_sems, recv_sems):
        for cp in _small_copies(refs, send_sems, recv_sems):
            cp.start()

    send, recv, thru, token = _split_start(body, name, [v, lax.empty((N_DEV,) + v.shape, v.dtype)], (N_DEV - 1,), after)
    return send, recv, thru, token


def _small_reduce_finish(me, send, recv, thru, after, name):
    def body(refs, send_sems, recv_sems):
        for cp in _small_wait_copies(refs, send_sems, recv_sems):
            cp.wait()

    v, landed = _split_wait(body, name + "_wait", thru, send, recv, after)

    def add(me_ref, v_ref, land_ref, o_ref):
        acc = jnp.where(me_ref[0] == 0, v_ref[...], land_ref[0])
        for d in range(1, N_DEV):
            acc = acc + jnp.where(me_ref[0] == d, v_ref[...], land_ref[d])
        o_ref[...] = acc

    return _pcall(add, name=name + "_sum",
                  in_specs=[pl.BlockSpec(memory_space=pltpu.SMEM), VMEM_SPEC, VMEM_SPEC], out_specs=VMEM_SPEC,
                  out_shape=jax.ShapeDtypeStruct(v.shape, F32),
                  compiler_params=pltpu.CompilerParams(vmem_limit_bytes=VMEM_LIMIT_BYTES))(me, v, landed)


def _pack(arrays):
    flat = []
    for a in arrays:
        a = a.reshape(-1)
        flat.append(jnp.pad(a, (0, -a.shape[0] % 128)))
    flat = jnp.concatenate(flat)
    flat = jnp.pad(flat, (0, -flat.shape[0] % 1024))
    return flat.reshape(-1, 128)


def _unpack(packed, shapes):
    flat = packed.reshape(-1)
    out, at = [], 0
    for shp in shapes:
        size = 1
        for d in shp:
            size *= d
        out.append(flat[at:at + size].reshape(shp))
        at += size + (-size % 128)
    return out


WEIGHTS = ['l0_mix_norm_g', 'l0_w_in', 'l0_sc_conv_w', 'l0_w_out', 'l0_ffn_norm_g', 'l0_ffn_up', 'l0_ffn_conv_w',
           'l0_ffn_down', 'l1_mix_norm_g', 'l1_w_in', 'l1_fox_b_f', 'l1_sg_w', 'l1_sg_b', 'l1_sg_norm_g', 'l1_w_out',
           'l1_ffn_norm_g', 'l1_ffn_up', 'l1_ffn_conv_w', 'l1_ffn_down', 'final_norm_g']
BIG = {'l0_w_in': 'col', 'l0_w_out': 'row', 'l0_ffn_up': 'col', 'l0_ffn_down': 'row',
       'l1_w_in': 'maj', 'l1_w_out': 'row', 'l1_ffn_up': 'col', 'l1_ffn_down': 'row'}
GATHER_GROUPS = [['l0_w_in'], ['l0_w_out'], ['l0_ffn_up'], ['l0_ffn_down'], ['l1_w_in'], ['l1_w_out'],
                 ['l1_ffn_up'], ['l1_ffn_down']]
CONV = ['l0_sc_conv_w', 'l0_ffn_conv_w', 'l1_ffn_conv_w']
SMALL = [n for n in WEIGHTS if n not in BIG]
LATE_SMALL = ['l0_sc_conv_w', 'l0_mix_norm_g']
IN_CD = 5 * HALF + N_HEADS


def _ffn_forward(x, g, get_up, behind_act, get_down, conv_w, tag):
    h = _rmsnorm_fwd(x, g, tag + "_norm")
    u = _matmul(h, get_up(h), "nn", F32, tag + "_up")
    f = _ffn_act_fwd(u, conv_w, tag + "_act", after=behind_act(u))
    w_down, tokens = get_down(f)
    return _matmul(f, w_down, "nn", F32, tag + "_down", res=x, after=tokens), (h, u, f)


def _ffn_backward(x, g, w_up, conv_w, w_down, saved, d_out, send_up, send_down, tag):
    h, u, f = saved
    dw_down = _matmul(f, d_out, "tn", BF16, tag + "_dwdown")
    d_f = _matmul(d_out, w_down, "nt", F32, tag + "_df", after=[send_down(dw_down)])
    du, dcw_gate, dcw_up = _ffn_act_bwd(u, conv_w, d_f, tag + "_dact")
    dw_up = _matmul(h, du, "tn", BF16, tag + "_dwup")
    dh = _matmul(du, w_up, "nt", F32, tag + "_dh", after=[send_up(dw_up)])
    dx, dg = _rmsnorm_bwd(x, g, dh, d_out, tag + "_dnorm")
    return dx, dg, jnp.concatenate([dcw_gate, dcw_up], axis=1)


def kernel(x, l0_mix_norm_g, l0_w_in, l0_sc_conv_w, l0_w_out, l0_ffn_norm_g, l0_ffn_up, l0_ffn_conv_w, l0_ffn_down, l1_mix_norm_g, l1_w_in, l1_fox_b_f, l1_sg_w, l1_sg_b, l1_sg_norm_g, l1_w_out, l1_ffn_norm_g, l1_ffn_up, l1_ffn_conv_w, l1_ffn_down, final_norm_g, loss_target, m_l0_mix_norm_g, m_l0_w_in, m_l0_sc_conv_w, m_l0_w_out, m_l0_ffn_norm_g, m_l0_ffn_up, m_l0_ffn_conv_w, m_l0_ffn_down, m_l1_mix_norm_g, m_l1_w_in, m_l1_fox_b_f, m_l1_sg_w, m_l1_sg_b, m_l1_sg_norm_g, m_l1_w_out, m_l1_ffn_norm_g, m_l1_ffn_up, m_l1_ffn_conv_w, m_l1_ffn_down, m_final_norm_g, v_l0_mix_norm_g, v_l0_w_in, v_l0_sc_conv_w, v_l0_w_out, v_l0_ffn_norm_g, v_l0_ffn_up, v_l0_ffn_conv_w, v_l0_ffn_down, v_l1_mix_norm_g, v_l1_w_in, v_l1_fox_b_f, v_l1_sg_w, v_l1_sg_b, v_l1_sg_norm_g, v_l1_w_out, v_l1_ffn_norm_g, v_l1_ffn_up, v_l1_ffn_conv_w, v_l1_ffn_down, v_final_norm_g):
    given = (l0_mix_norm_g, l0_w_in, l0_sc_conv_w, l0_w_out, l0_ffn_norm_g, l0_ffn_up, l0_ffn_conv_w, l0_ffn_down, l1_mix_norm_g, l1_w_in, l1_fox_b_f, l1_sg_w, l1_sg_b, l1_sg_norm_g, l1_w_out, l1_ffn_norm_g, l1_ffn_up, l1_ffn_conv_w, l1_ffn_down, final_norm_g)
    given_m = (m_l0_mix_norm_g, m_l0_w_in, m_l0_sc_conv_w, m_l0_w_out, m_l0_ffn_norm_g, m_l0_ffn_up, m_l0_ffn_conv_w, m_l0_ffn_down, m_l1_mix_norm_g, m_l1_w_in, m_l1_fox_b_f, m_l1_sg_w, m_l1_sg_b, m_l1_sg_norm_g, m_l1_w_out, m_l1_ffn_norm_g, m_l1_ffn_up, m_l1_ffn_conv_w, m_l1_ffn_down, m_final_norm_g)
    given_v = (v_l0_mix_norm_g, v_l0_w_in, v_l0_sc_conv_w, v_l0_w_out, v_l0_ffn_norm_g, v_l0_ffn_up, v_l0_ffn_conv_w, v_l0_ffn_down, v_l1_mix_norm_g, v_l1_w_in, v_l1_fox_b_f, v_l1_sg_w, v_l1_sg_b, v_l1_sg_norm_g, v_l1_w_out, v_l1_ffn_norm_g, v_l1_ffn_up, v_l1_ffn_conv_w, v_l1_ffn_down, v_final_norm_g)
    wt = dict(zip(WEIGHTS, given))
    mom = dict(zip(WEIGHTS, given_m))
    var = dict(zip(WEIGHTS, given_v))
    s = x.shape[1]
    t = ATT_BLOCK
    x0, target = x[0], loss_target[0]
    chip = 2 * lax.axis_index("x") + lax.axis_index("y")

    pos = jnp.stack([chip, lax.axis_index("c")]).astype(jnp.int32)

    conv_shard = jnp.concatenate([wt[n] for n in CONV], axis=1)
    gathers, token = [], ()
    for gi, names in enumerate(GATHER_GROUPS):
        placed = [_place_shard(pos, wt[n], BIG[n], "place_" + n, token) for n in names]
        send, recv, thru, tok = _gather_start(placed, [BIG[n] for n in names], "gather_start_%d" % gi, token)
        gathers.append((send, recv, thru))
        if gi == 0:
            conv_started = _gather_small_start(conv_shard, [tok])
            tok = conv_started[3]
        token = [tok]
    token = token[0]
    full = {}
    conv_full = {}

    def finish_conv(after):
        landed = _gather_small_finish(*conv_started[:3], after)
        conv_all = lax.dynamic_update_slice(landed, conv_shard[None], (chip, 0, 0))
        at = 0
        for n in CONV:
            cw = wt[n].shape[1]
            conv_full[n] = jnp.transpose(conv_all[:, :, at:at + cw], (1, 0, 2)).reshape(3, 4 * cw)
            at += cw

    def forward_gather(gi, after):
        send, recv, thru = gathers[gi]
        kinds = [BIG[n] for n in GATHER_GROUPS[gi]]
        gathers[gi] = _gather_forward(thru, kinds, send, recv, after, "gather_forward_%d" % gi)
        return gathers[gi][3]

    def finish_gather(gi, after):
        send, recv, thru, tok = gathers[gi]
        names = GATHER_GROUPS[gi]
        wholes = _gather_finish(thru, [BIG[n] for n in names], send, recv, tok if after is None else after,
                                "gather_finish_%d" % gi)
        full.update(zip(names, wholes))

    def vec(name):
        return wt[name].reshape(1, -1)

    h0 = _rmsnorm_fwd(x0, vec('l0_mix_norm_g'), "l0_mix_norm", after=[token])
    forward_gather(0, h0)
    finish_gather(0, None)
    p0 = _matmul(h0, full['l0_w_in'], "nn", F32, "l0_in")
    a_out, sb_carries = _sb_fwd(p0, "l0_sb", after=[forward_gather(1, p0)])
    finish_gather(1, a_out)
    finish_conv(p0)
    b_out = _sc_fwd(p0, conv_full['l0_sc_conv_w'], "l0_sc")
    ab0 = jnp.concatenate([a_out, b_out], axis=1)
    x1 = _matmul(ab0, full['l0_w_out'], "nn", F32, "l0_out", res=x0, after=[forward_gather(2, b_out)])

    def ffn_weights(up_group, next_group):
        def get_up(h):
            finish_gather(up_group, h)
            return full[GATHER_GROUPS[up_group][0]]

        def behind_act(u):
            return [forward_gather(up_group + 1, u)]

        def get_down(f):
            finish_gather(up_group + 1, f)
            return full[GATHER_GROUPS[up_group + 1][0]], ([forward_gather(next_group, f)] if next_group else ())

        return get_up, behind_act, get_down

    x2, ffn0_saved = _ffn_forward(x1, vec('l0_ffn_norm_g'), *ffn_weights(2, 4), conv_full['l0_ffn_conv_w'], "l0_ffn")
    h2 = _rmsnorm_fwd(x2, vec('l1_mix_norm_g'), "l1_mix_norm")
    finish_gather(4, h2)
    w_in1 = jnp.transpose(full['l1_w_in'], (1, 0, 2)).reshape(D_MODEL, IN_CD)
    w_in1_main = w_in1[:, :5 * HALF]
    w_in1_f = jnp.pad(w_in1[:, 5 * HALF:], ((0, 0), (0, 128 - N_HEADS)))
    p1 = _matmul(h2, w_in1_main, "nn", F32, "l1_in")
    f_logit = _matmul(h2, w_in1_f, "nn", F32, "l1_in_f", after=[forward_gather(5, p1)])
    b_f = jnp.pad(wt['l1_fox_b_f'], (0, 128 - N_HEADS)).reshape(1, 128)
    c_heads = _fox_prep(f_logit, b_f, "l1_fox_prep")[:, :N_HEADS].T
    c_col = c_heads[:, :, None]
    c_row = c_heads.reshape(N_HEADS, s // t, 1, t)
    sg_bias = jnp.repeat(wt['l1_sg_b'].T, HEAD, axis=1)
    sg_gain = vec('l1_sg_norm_g')
    c_out = _sg_fwd(p1, wt['l1_sg_w'], sg_bias, sg_gain, "l1_sg")
    d_out, lse = _fox_fwd(p1, c_col, c_row, "l1_fox", after=[forward_gather(6, c_out)])
    finish_gather(5, d_out)
    cd1 = jnp.concatenate([c_out, d_out.astype(BF16)], axis=1)
    x3 = _matmul(cd1, full['l1_w_out'], "nn", F32, "l1_out", res=x2)
    x4, ffn1_saved = _ffn_forward(x3, vec('l1_ffn_norm_g'), *ffn_weights(6, None), conv_full['l1_ffn_conv_w'], "l1_ffn")
    dx4, dg_final, loss_part = _loss_head(x4, vec('final_norm_g'), target, "loss_head")

    grads = {'final_norm_g': dg_final}
    scatters = []

    def send_grads(names):
        def start(*group):
            send, recv, thru, lands, tok = _scatter_start(list(group), [BIG[n] for n in names],
                                                          "scatter_start_%d" % len(scatters))
            scatters.append((names, send, recv, thru, lands))
            return tok
        return start

    dx3, grads['l1_ffn_norm_g'], grads['l1_ffn_conv_w'] = _ffn_backward(
        x3, vec('l1_ffn_norm_g'), full['l1_ffn_up'], conv_full['l1_ffn_conv_w'], full['l1_ffn_down'], ffn1_saved, dx4,
        send_grads(['l1_ffn_up']), send_grads(['l1_ffn_down']), "l1_ffn")
    dw_out1 = _matmul(cd1, dx3, "tn", BF16, "l1_dwout")
    d_cd = _matmul(dx3, full['l1_w_out'], "nt", F32, "l1_dcd")
    du, dv, grads['l1_sg_w'], db_sg, grads['l1_sg_norm_g'] = _sg_bwd(p1, wt['l1_sg_w'], sg_bias, sg_gain, d_cd, "l1_dsg")
    grads['l1_sg_b'] = db_sg[:, :N_HEADS].T
    dq, dk, dvv, dcol, drow = _fox_bwd(p1, c_col, c_row, lse, d_cd, d_out, "l1_dfox")
    d_f_logit, d_b_f = _fox_post(drow, dcol, f_logit, b_f, "l1_fox_post")
    grads['l1_fox_b_f'] = d_b_f[0, :N_HEADS]
    dp1 = jnp.concatenate([a.astype(BF16) for a in (du, dv, dq, dk, dvv)], axis=1)
    dw_main = _matmul(h2, dp1, "tn", BF16, "l1_dwin")
    dw_f = _matmul(h2, d_f_logit, "tn", BF16, "l1_dwin_f")
    dw_in1 = jnp.concatenate([dw_main, dw_f[:, :N_HEADS]], axis=1)
    dw_in1 = jnp.transpose(dw_in1.reshape(D_MODEL, 4, IN_CD // 4), (1, 0, 2))
    dh2 = _matmul(dp1, w_in1_main, "nt", F32, "l1_dh", after=[send_grads(['l1_w_out', 'l1_w_in'])(dw_out1, dw_in1)])
    dh2 = _matmul(d_f_logit, w_in1_f, "nt", F32, "l1_dh_f", res=dh2)
    dx2, grads['l1_mix_norm_g'] = _rmsnorm_bwd(x2, vec('l1_mix_norm_g'), dh2, dx3, "l1_dmix_norm")
    dx1, grads['l0_ffn_norm_g'], grads['l0_ffn_conv_w'] = _ffn_backward(
        x1, vec('l0_ffn_norm_g'), full['l0_ffn_up'], conv_full['l0_ffn_conv_w'], full['l0_ffn_down'], ffn0_saved, dx2,
        send_grads(['l0_ffn_up']), send_grads(['l0_ffn_down']), "l0_ffn")
    early_names = [n for n in SMALL if n not in LATE_SMALL]
    early = _small_reduce_start(_pack([grads[n] for n in early_names] + [loss_part]), "small_start", [dx1])
    dw_out0 = _matmul(ab0, dx1, "tn", BF16, "l0_dwout", after=[early[3]])
    d_ab = _matmul(dx1, full['l0_w_out'], "nt", F32, "l0_dab", after=[send_grads(['l0_w_out'])(dw_out0)])
    dq0, dk0, dv0 = _sb_bwd(p0, d_ab, sb_carries, "l0_dsb")
    dgb, dgc, dhin, grads['l0_sc_conv_w'] = _sc_bwd(p0, conv_full['l0_sc_conv_w'], d_ab, "l0_dsc")
    dp0 = jnp.concatenate([a.astype(BF16) for a in (dq0, dk0, dv0, dgb, dgc, dhin)], axis=1)
    dw_in0 = _matmul(h0, dp0, "tn", BF16, "l0_dwin")
    dh0 = _matmul(dp0, full['l0_w_in'], "nt", F32, "l0_dh", after=[send_grads(['l0_w_in'])(dw_in0)])
    dx0, grads['l0_mix_norm_g'] = _rmsnorm_bwd(x0, vec('l0_mix_norm_g'), dh0, dx1, "l0_dmix_norm")

    shard_grads, delta, new_m, new_v, swaps = {}, {}, {}, {}, {}

    def reduce_group(gi, after):
        names, send, recv, thru, lands = scatters[gi]
        kinds = [BIG[n] for n in names]
        g_thru, landed = _scatter_wait(thru, lands, kinds, send, recv, after, "scatter_wait_%d" % gi)
        halves = [_sum_partials(pos, g, ld, kd, "sum_" + n) for n, g, ld, kd in zip(names, g_thru, landed, kinds)]
        s_send, s_recv, h_thru, s_lands, tok = _swap_start(halves, "swap_start_%d" % gi)
        swaps[gi] = (names, s_send, s_recv, h_thru, s_lands)
        return tok

    def update_group(gi, after):
        names, s_send, s_recv, h_thru, s_lands = swaps[gi]
        mine, theirs = _swap_wait(h_thru, s_lands, s_send, s_recv, after, "swap_wait_%d" % gi)
        for n, gm, gs in zip(names, mine, theirs):
            shard_grads[n], delta[n], new_m[n], new_v[n] = _adamw_shard(pos, wt[n], gm, gs, mom[n], var[n], "adamw_" + n)
        return [delta[n] for n in names]

    after = reduce_group(2, reduce_group(1, reduce_group(0, dx0)))
    after = update_group(2, update_group(1, update_group(0, after)))
    after = reduce_group(5, reduce_group(4, reduce_group(3, after)))
    after = update_group(5, update_group(4, update_group(3, after)))
    after = reduce_group(6, after)
    def small_shapes(names):
        return [conv_full[n].shape if n in CONV else wt[n].shape for n in names]

    me = (2 * chip + lax.axis_index("c")).astype(jnp.int32).reshape(1)
    early_all = _small_reduce_finish(me, early[0], early[1], early[2], after, "small_early")
    early_sums = _unpack(early_all, small_shapes(early_names) + [loss_part.shape])
    loss = early_sums[-1][0, 0]
    late_all = _allreduce_small(_pack([grads[n] for n in LATE_SMALL]), [early_all])
    small_sums = dict(zip(early_names + LATE_SMALL, early_sums[:-1] + _unpack(late_all, small_shapes(LATE_SMALL))))
    for n in SMALL:
        g = small_sums[n]
        shard_grads[n] = lax.dynamic_slice_in_dim(g, chip * wt[n].shape[1], wt[n].shape[1], axis=1) if n in CONV else g
    update_group(6, late_all)
    def flat2d(a):
        return a.reshape(-1, a.shape[-1])

    small_out = _adamw_small(*[[flat2d(src[n]) for n in SMALL] for src in (wt, shard_grads, mom, var)], "adamw_small")
    for out, arrays in zip((delta, new_m, new_v), small_out):
        out.update((n, a.reshape(wt[n].shape)) for n, a in zip(SMALL, arrays))

    return (loss, dx0[None], *[shard_grads[n] for n in WEIGHTS], *[delta[n] for n in WEIGHTS],
            *[new_m[n] for n in WEIGHTS], *[new_v[n] for n in WEIGHTS])
```

```python
import jax
import jax.numpy as jnp
from jax import lax
from jax.experimental import pallas as pl
from jax.experimental.pallas import tpu as pltpu

F32 = jnp.float32
BF16 = jnp.bfloat16

D_MODEL = 2048
HEAD = 128
N_HEADS = 8
HALF = N_HEADS * HEAD
D_FF = 5632
EPS = 1e-6
ATT_SCALE = HEAD ** -0.5
ATT_BLOCK = 512
NEG = -1e30

ADAM_LR = 0.001
ADAM_B1 = 0.9
ADAM_B2 = 0.999
ADAM_EPS = 1e-08
ADAM_WD = 0.01
ADAM_STEP = 10

VMEM_LIMIT_BYTES = 48 * 1024 * 1024
MM_VMEM_LIMIT_BYTES = 56 * 1024 * 1024
ATT_VMEM_LIMIT_BYTES = 48 * 1024 * 1024
MESH = pl.DeviceIdType.MESH
HBM_SPEC = pl.BlockSpec(memory_space=pltpu.HBM)
VMEM_SPEC = pl.BlockSpec(memory_space=pltpu.VMEM)


def _pcall(body, after=(), **kw):
    if not after:
        return pl.pallas_call(body, **kw)
    n_in, n_after, inner = len(kw["in_specs"]), len(after), body
    kw["in_specs"] = list(kw["in_specs"]) + [pl.BlockSpec(memory_space=pl.ANY)] * n_after

    def body(*refs):
        inner(*refs[:n_in], *refs[n_in + n_after:])

    call = pl.pallas_call(body, **kw)
    return lambda *args: call(*args, *after)


def _params(*semantics, vmem_limit=VMEM_LIMIT_BYTES):
    return pltpu.CompilerParams(dimension_semantics=semantics, vmem_limit_bytes=vmem_limit)


def _pick(n, cap):
    best = None
    for t in range(128, min(n, cap) + 1, 128):
        if n % t == 0:
            best = t
    return n if best is None else best


def _dot(a, b, dims):
    return lax.dot_general(a, b, (dims, ((), ())), preferred_element_type=F32)


def _dot_nn(a, b):
    return _dot(a, b, ((1,), (0,)))


def _dot_nt(a, b):
    return _dot(a, b, ((1,), (1,)))


def _dot_tn(a, b):
    return _dot(a, b, ((0,), (0,)))


def _split3(x):
    hi = x.astype(BF16)
    r1 = x - hi.astype(F32)
    mid = r1.astype(BF16)
    lo = (r1 - mid.astype(F32)).astype(BF16)
    return hi, mid, lo


def _log_sigmoid(z):
    return jnp.minimum(z, 0.0) - jnp.log1p(jnp.exp(-jnp.abs(z)))


_GELU_K = 0.7978845608028654


def _gelu(x):
    return 0.5 * x * (1.0 + jnp.tanh(_GELU_K * (x + 0.044715 * x * x * x)))


def _gelu_grad(x):
    t = jnp.tanh(_GELU_K * (x + 0.044715 * x * x * x))
    return 0.5 * (1.0 + t) + 0.5 * x * (1.0 - t * t) * _GELU_K * (1.0 + 3.0 * 0.044715 * x * x)


SUBLANES = 8


def _shift_down(x, k):
    rolled = pltpu.roll(x, k, axis=0)
    head = rolled[:SUBLANES]
    head = jnp.where(lax.broadcasted_iota(jnp.int32, head.shape, 0) >= k, head, 0.0)
    return jnp.concatenate([head, rolled[SUBLANES:]], axis=0)


def _shift_up(x, k):
    n = x.shape[0]
    rolled = pltpu.roll(x, n - k, axis=0)
    tail = rolled[n - SUBLANES:]
    tail = jnp.where(lax.broadcasted_iota(jnp.int32, tail.shape, 0) < SUBLANES - k, tail, 0.0)
    return jnp.concatenate([rolled[:n - SUBLANES], tail], axis=0)


def _conv3(s, w, shifted=None):
    s1, s2 = shifted if shifted else (_shift_down(s, 1), _shift_down(s, 2))
    return w[0:1, :] * s2 + w[1:2, :] * s1 + w[2:3, :] * s


def _conv3_transpose(d, w):
    return w[2:3, :] * d + w[1:2, :] * _shift_up(d, 1) + w[0:1, :] * _shift_up(d, 2)


def _conv3_wgrad(d, s, shifted, dw_ref):
    s1, s2 = shifted
    dw_ref[0:1, :] = jnp.sum(d * s2, axis=0, keepdims=True)
    dw_ref[1:2, :] = jnp.sum(d * s1, axis=0, keepdims=True)
    dw_ref[2:3, :] = jnp.sum(d * s, axis=0, keepdims=True)


MM_TILE_M, MM_TILE_N, MM_TILE_K = 1408, 512, 5632


def _matmul(a, b, mode, out_dtype, name, res=None, after=()):
    a_parts = a.shape[0] if a.ndim == 3 else 1
    b_parts = b.shape[0] if b.ndim == 3 else 1
    a_shape = (a.shape[1], a_parts * a.shape[2]) if a.ndim == 3 else a.shape
    b_shape = (b.shape[1], b_parts * b.shape[2]) if b.ndim == 3 else b.shape
    assert (a_parts == 1 or mode != "tn") and (b_parts == 1 or mode == "tn")
    if mode == "nn":
        (m, k), (k2, n) = a_shape, b_shape
    elif mode == "nt":
        (m, k), (n, k2) = a_shape, b_shape
    else:
        (k, m), (k2, n) = a_shape, b_shape
    assert k == k2, (a.shape, b.shape, mode)
    tm, tn, tk = _pick(m, MM_TILE_M), _pick(n // b_parts, MM_TILE_N), _pick(k // a_parts, MM_TILE_K)
    nk = k // tk
    if mode == "tn":
        a_spec = pl.BlockSpec((tk, tm), lambda i, j, kk: (kk, i))
    elif a_parts > 1:
        per = nk // a_parts
        a_spec = pl.BlockSpec((None, tm, tk), lambda i, j, kk: (kk // per, i, kk % per))
    else:
        a_spec = pl.BlockSpec((tm, tk), lambda i, j, kk: (i, kk))
    if mode == "nt":
        b_spec = pl.BlockSpec((tn, tk), lambda i, j, kk: (j, kk))
    elif b_parts > 1:
        per = n // b_parts // tn
        b_spec = pl.BlockSpec((None, tk, tn), lambda i, j, kk: (j // per, kk, j % per))
    else:
        b_spec = pl.BlockSpec((tk, tn), lambda i, j, kk: (kk, j))
    o_spec = pl.BlockSpec((tm, tn), lambda i, j, kk: (i, j))
    dims = {"nn": ((1,), (0,)), "nt": ((1,), (1,)), "tn": ((0,), (0,))}[mode]
    has_res = res is not None

    def body(*refs):
        a_ref, b_ref = refs[0], refs[1]
        r_ref = refs[2] if has_res else None
        o_ref = refs[3] if has_res else refs[2]
        part = _dot(a_ref[...].astype(BF16), b_ref[...].astype(BF16), dims)

        def finish(total):
            if has_res:
                total = total + r_ref[...]
            o_ref[...] = total.astype(out_dtype)

        if nk == 1:
            finish(part)
        else:
            acc_ref = refs[-1]
            kk = pl.program_id(2)

            @pl.when(kk == 0)
            def _():
                acc_ref[...] = part

            @pl.when(kk > 0)
            def _():
                acc_ref[...] += part

            @pl.when(kk == nk - 1)
            def _():
                finish(acc_ref[...])

    in_specs = [a_spec, b_spec] + ([o_spec] if has_res else [])
    args = (a, b) + ((res,) if has_res else ())
    return _pcall(
        body, after=after, name=name, grid=(m // tm, n // tn, nk),
        in_specs=in_specs, out_specs=o_spec,
        out_shape=jax.ShapeDtypeStruct((m, n), out_dtype),
        scratch_shapes=[pltpu.VMEM((tm, tn), F32)] if nk > 1 else [],
        compiler_params=_params("parallel", "parallel", "arbitrary", vmem_limit=MM_VMEM_LIMIT_BYTES),
    )(*args)


ROW_TILE = 256


def _rmsnorm_fwd(x, g, name, after=()):
    s, d = x.shape

    def body(x_ref, g_ref, o_ref):
        xf = x_ref[...]
        r = lax.rsqrt(jnp.mean(xf * xf, axis=-1, keepdims=True) + EPS)
        o_ref[...] = (xf * r * g_ref[...]).astype(BF16)

    row = pl.BlockSpec((ROW_TILE, d), lambda i: (i, 0))
    vec = pl.BlockSpec((1, d), lambda i: (0, 0))
    return _pcall(body, after=after, name=name, grid=(s // ROW_TILE,), in_specs=[row, vec], out_specs=row,
                  out_shape=jax.ShapeDtypeStruct((s, d), BF16), compiler_params=_params("parallel"))(x, g)


def _rmsnorm_bwd(x, g, dh, dres, name):
    s, d = x.shape

    def body(x_ref, g_ref, dh_ref, dres_ref, dx_ref, dg_ref):
        xf = x_ref[...]
        r = lax.rsqrt(jnp.mean(xf * xf, axis=-1, keepdims=True) + EPS)
        xhat = xf * r
        dh_v = dh_ref[...]
        dxh = dh_v * g_ref[...]
        proj = jnp.mean(dxh * xhat, axis=-1, keepdims=True)
        dx_ref[...] = dres_ref[...] + r * (dxh - xhat * proj)
        part = jnp.sum(dh_v * xhat, axis=0, keepdims=True)

        @pl.when(pl.program_id(0) == 0)
        def _():
            dg_ref[...] = part

        @pl.when(pl.program_id(0) > 0)
        def _():
            dg_ref[...] += part

    row = pl.BlockSpec((ROW_TILE, d), lambda i: (i, 0))
    vec = pl.BlockSpec((1, d), lambda i: (0, 0))
    return _pcall(body, name=name, grid=(s // ROW_TILE,), in_specs=[row, vec, row, row], out_specs=[row, vec],
                  out_shape=[jax.ShapeDtypeStruct((s, d), F32), jax.ShapeDtypeStruct((1, d), F32)],
                  compiler_params=_params("arbitrary"))(x, g, dh, dres)


def _loss_head(x, g, target, name):
    s, d = x.shape

    def body(x_ref, g_ref, t_ref, dx_ref, dg_ref, loss_ref):
        xf = x_ref[...]
        r = lax.rsqrt(jnp.mean(xf * xf, axis=-1, keepdims=True) + EPS)
        xhat = xf * r
        gv = g_ref[...]
        err = xhat * gv - t_ref[...]
        dy = err * (1.0 / d)
        dxh = dy * gv
        proj = jnp.mean(dxh * xhat, axis=-1, keepdims=True)
        dx_ref[...] = r * (dxh - xhat * proj)
        dg_part = jnp.sum(dy * xhat, axis=0, keepdims=True)
        row_loss = jnp.sum(err * err, axis=-1, keepdims=True) * (0.5 / d)
        loss_part = jnp.broadcast_to(jnp.sum(row_loss, axis=0, keepdims=True), (1, 128))

        @pl.when(pl.program_id(0) == 0)
        def _():
            dg_ref[...] = dg_part
            loss_ref[...] = loss_part

        @pl.when(pl.program_id(0) > 0)
        def _():
            dg_ref[...] += dg_part
            loss_ref[...] += loss_part

    row = pl.BlockSpec((ROW_TILE, d), lambda i: (i, 0))
    vec = pl.BlockSpec((1, d), lambda i: (0, 0))
    one = pl.BlockSpec((1, 128), lambda i: (0, 0))
    return _pcall(body, name=name, grid=(s // ROW_TILE,), in_specs=[row, vec, row], out_specs=[row, vec, one],
                  out_shape=[jax.ShapeDtypeStruct((s, d), F32), jax.ShapeDtypeStruct((1, d), F32),
                             jax.ShapeDtypeStruct((1, 128), F32)],
                  compiler_params=_params("arbitrary"))(x, g, target)


HEADS_PER_STEP = 2
GROUP_W = HEADS_PER_STEP * HEAD
N_GROUPS = N_HEADS // HEADS_PER_STEP


def _head_cols(h):
    return slice(h * HEAD, (h + 1) * HEAD)


def _head_specs(s, col0):
    t = ATT_BLOCK
    g0 = [c // HEADS_PER_STEP for c in col0]
    qspec = pl.BlockSpec((t, GROUP_W), lambda g, i: (i, g0[0] + g))
    kspec = pl.BlockSpec((s, GROUP_W), lambda g, i: (0, g0[1] + g))
    vspec = pl.BlockSpec((s, GROUP_W), lambda g, i: (0, g0[2] + g))
    return qspec, kspec, vspec


TRI = 256


def _order_matrix(later):
    r, c = lax.broadcasted_iota(jnp.int32, (TRI, TRI), 0), lax.broadcasted_iota(jnp.int32, (TRI, TRI), 1)
    return (r > c if later else r < c).astype(BF16)


def _exact_dot(x, m, later):
    parts = [x[:, c:c + TRI] for c in range(0, x.shape[1], TRI)]
    totals = [jnp.sum(p, axis=1, keepdims=True) for p in parts] if len(parts) > 1 else None
    out = []
    for j, p in enumerate(parts):
        hi = p.astype(BF16)
        lo = (p - hi.astype(F32)).astype(BF16)
        acc = _dot_nn(hi, m) + _dot_nn(lo, m)
        for other in (range(j + 1, len(parts)) if later else range(j)):
            acc = acc + totals[other]
        out.append(acc)
    return out[0] if len(out) == 1 else jnp.concatenate(out, axis=1)


def _sb_block(q, kblk, carry_l, u, diagonal):
    t = ATT_BLOCK
    z = _dot_nt(q, kblk) * ATT_SCALE
    sp = jnp.maximum(z, 0.0) + jnp.log(1.0 + jnp.exp(-jnp.abs(z)))
    if not diagonal:
        l = -sp
        return z, None, l, jnp.exp(z + l + _exact_dot(l, u, True) + carry_l)
    mask = lax.broadcasted_iota(jnp.int32, (t, t), 1) < lax.broadcasted_iota(jnp.int32, (t, t), 0)
    l = jnp.where(mask, -sp, 0.0)
    a = jnp.where(mask, jnp.exp(z - sp + _exact_dot(l, u, True) + carry_l), 0.0)
    return z, mask, l, a


def _sb_carry_spec(s):
    t = ATT_BLOCK
    return pl.BlockSpec((HEADS_PER_STEP, None, s // t, t, 1), lambda g, i: (g, i, 0, 0, 0))


def _sb_fwd(p, name, after=()):
    s = p.shape[0]
    t = ATT_BLOCK
    nb = s // t

    def body(q_ref, k_ref, v_ref, o_ref, cl_ref):
        i = pl.program_id(1)
        heads = range(HEADS_PER_STEP)
        q = [q_ref[:, _head_cols(h)].astype(BF16) for h in heads]
        u = _order_matrix(True)
        cl_ref[...] = jnp.zeros_like(cl_ref)

        def tile(kb, carry, diagonal):
            ks = pl.multiple_of(kb * t, t)
            out = []
            for h in heads:
                acc, carry_l = carry[h]
                kblk = k_ref[pl.ds(ks, t), _head_cols(h)].astype(BF16)
                vblk = v_ref[pl.ds(ks, t), _head_cols(h)].astype(BF16)
                cl_ref[h, kb] = carry_l
                _, _, l, a = _sb_block(q[h], kblk, carry_l, u, diagonal)
                out.append((acc + _dot_nn(a.astype(BF16), vblk), carry_l + jnp.sum(l, axis=1, keepdims=True)))
            return tuple(out)

        carry = tile(i, tuple((jnp.zeros((t, HEAD), F32), jnp.zeros((t, 1), F32)) for _ in heads), True)
        carry = lax.fori_loop(0, i, lambda n, c: tile(i - 1 - n, c, False), carry)
        for h in heads:
            o_ref[:, _head_cols(h)] = carry[h][0].astype(BF16)

    qspec, kspec, vspec = _head_specs(s, (0, N_HEADS, 2 * N_HEADS))
    ospec = pl.BlockSpec((t, GROUP_W), lambda g, i: (i, g))
    return _pcall(body, after=after, name=name, grid=(N_GROUPS, nb), in_specs=[qspec, kspec, vspec],
                  out_specs=[ospec, _sb_carry_spec(s)],
                  out_shape=[jax.ShapeDtypeStruct((s, 2 * HALF), BF16), jax.ShapeDtypeStruct((N_HEADS, nb, nb, t, 1), F32)],
                  compiler_params=_params("parallel", "parallel", vmem_limit=ATT_VMEM_LIMIT_BYTES))(p, p, p)


def _sb_bwd(p, d_ab, carries, name):
    s = p.shape[0]
    t = ATT_BLOCK

    def body(q_ref, k_ref, v_ref, do_ref, cl_ref, dq_ref, dk_ref, dv_ref):
        i = pl.program_id(1)

        @pl.when(i == 0)
        def _():
            dk_ref[...] = jnp.zeros_like(dk_ref)
            dv_ref[...] = jnp.zeros_like(dv_ref)

        heads = range(HEADS_PER_STEP)
        q = [q_ref[:, _head_cols(h)].astype(BF16) for h in heads]
        do = [do_ref[:, _head_cols(h)].astype(BF16) for h in heads]
        u = _order_matrix(True)
        lower = _order_matrix(False)

        def tile(kb, carry, diagonal):
            ks = pl.multiple_of(kb * t, t)
            out = []
            for h in heads:
                dq, carry_g = carry[h]
                kblk = k_ref[pl.ds(ks, t), _head_cols(h)].astype(BF16)
                vblk = v_ref[pl.ds(ks, t), _head_cols(h)].astype(BF16)
                z, mask, _, a = _sb_block(q[h], kblk, cl_ref[h, kb], u, diagonal)
                g = a * _dot_nt(do[h], vblk)
                earlier_g = _exact_dot(g, lower, False) + carry_g
                sig = jax.nn.sigmoid(z)
                dz = g * (1.0 - sig) - sig * earlier_g
                if diagonal:
                    dz = jnp.where(mask, dz, 0.0)
                dz = dz.astype(BF16)
                dv_ref[pl.ds(ks, t), _head_cols(h)] += _dot_tn(a.astype(BF16), do[h])
                dk_ref[pl.ds(ks, t), _head_cols(h)] += _dot_tn(dz, q[h]) * ATT_SCALE
                out.append((dq + _dot_nn(dz, kblk) * ATT_SCALE, carry_g + jnp.sum(g, axis=1, keepdims=True)))
            return tuple(out)

        init = tuple((jnp.zeros((t, HEAD), F32), jnp.zeros((t, 1), F32)) for _ in heads)
        carry = tile(i, lax.fori_loop(0, i, lambda kb, c: tile(kb, c, False), init), True)
        for h in heads:
            dq_ref[:, _head_cols(h)] = carry[h][0]

    qspec, kspec, vspec = _head_specs(s, (0, N_HEADS, 2 * N_HEADS))
    blk = pl.BlockSpec((t, GROUP_W), lambda g, i: (i, g))
    whole = pl.BlockSpec((s, GROUP_W), lambda g, i: (0, g))
    shape = jax.ShapeDtypeStruct((s, HALF), F32)
    return _pcall(body, name=name, grid=(N_GROUPS, s // t), in_specs=[qspec, kspec, vspec, blk, _sb_carry_spec(s)],
                  out_specs=[blk, whole, whole], out_shape=[shape, shape, shape],
                  compiler_params=_params("parallel", "arbitrary", vmem_limit=ATT_VMEM_LIMIT_BYTES))(p, p, p, d_ab, carries)


COL_TILE = 256


def _sc_fwd(p, w, mixed, name):
    s = p.shape[0]
    nb = HALF // COL_TILE

    def body(gb_ref, gc_ref, h_ref, w_ref, mixed_ref, o_ref):
        conv = _conv3(gc_ref[...] * h_ref[...], w_ref[...])
        o_ref[...] = (gb_ref[...] * conv).astype(BF16)

    def col(k):
        return pl.BlockSpec((s, COL_TILE), lambda j: (0, k * nb + j))

    wspec = pl.BlockSpec((3, COL_TILE), lambda j: (0, j))
    return _pcall(body, name=name, grid=(nb,), in_specs=[col(3), col(4), col(5), wspec, pl.BlockSpec(memory_space=pl.ANY)],
                  out_specs=col(1), out_shape=jax.ShapeDtypeStruct(mixed.shape, BF16), input_output_aliases={4: 0},
                  compiler_params=_params("parallel"))(p, p, p, w, mixed)


def _sc_bwd(p, w, d_ab, name):
    s = p.shape[0]
    nb = HALF // COL_TILE

    def body(gb_ref, gc_ref, h_ref, w_ref, d_ref, dgb_ref, dgc_ref, dh_ref, dw_ref):
        gc, hin, wv, d = gc_ref[...], h_ref[...], w_ref[...], d_ref[...]
        sig = gc * hin
        shifted = (_shift_down(sig, 1), _shift_down(sig, 2))
        dgb_ref[...] = d * _conv3(sig, wv, shifted)
        dconv = d * gb_ref[...]
        _conv3_wgrad(dconv, sig, shifted, dw_ref)
        dsig = _conv3_transpose(dconv, wv)
        dgc_ref[...] = dsig * hin
        dh_ref[...] = dsig * gc

    def col(k):
        return pl.BlockSpec((s, COL_TILE), lambda j: (0, k * nb + j))

    wspec = pl.BlockSpec((3, COL_TILE), lambda j: (0, j))
    act = jax.ShapeDtypeStruct((s, HALF), F32)
    return _pcall(body, name=name, grid=(nb,), in_specs=[col(3), col(4), col(5), wspec, col(1)],
                  out_specs=[col(0), col(0), col(0), wspec],
                  out_shape=[act, act, act, jax.ShapeDtypeStruct((3, HALF), F32)],
                  compiler_params=_params("parallel"))(p, p, p, w, d_ab)


def _ffn_act_fwd(u, w, name, after=()):
    s = u.shape[0]
    nb = D_FF // COL_TILE

    def body(ug_ref, uu_ref, wg_ref, wu_ref, o_ref):
        gate = _conv3(ug_ref[...], wg_ref[...])
        up = _conv3(uu_ref[...], wu_ref[...])
        o_ref[...] = (gate * jax.nn.sigmoid(gate) * up).astype(BF16)

    def col(k):
        return pl.BlockSpec((s, COL_TILE), lambda j: (0, k * nb + j))

    def wcol(k):
        return pl.BlockSpec((3, COL_TILE), lambda j: (0, k * nb + j))

    return _pcall(body, after=after, name=name, grid=(nb,), in_specs=[col(0), col(1), wcol(0), wcol(1)], out_specs=col(0),
                  out_shape=jax.ShapeDtypeStruct((s, D_FF), BF16),
                  compiler_params=_params("parallel"))(u, u, w, w)


def _ffn_act_bwd(u, w, d_f, name):
    s = u.shape[0]
    nb = D_FF // COL_TILE

    def body(ug_ref, uu_ref, wg_ref, wu_ref, d_ref, du_ref, dwg_ref, dwu_ref):
        ug, uu, wg, wu, d = ug_ref[...], uu_ref[...], wg_ref[...], wu_ref[...], d_ref[...]
        ug_shifted = (_shift_down(ug, 1), _shift_down(ug, 2))
        uu_shifted = (_shift_down(uu, 1), _shift_down(uu, 2))
        gate = _conv3(ug, wg, ug_shifted)
        up = _conv3(uu, wu, uu_shifted)
        sig = jax.nn.sigmoid(gate)
        d_up = d * gate * sig
        d_gate = d * up * sig * (1.0 + gate * (1.0 - sig))
        _conv3_wgrad(d_gate, ug, ug_shifted, dwg_ref)
        _conv3_wgrad(d_up, uu, uu_shifted, dwu_ref)
        du_ref[0] = _conv3_transpose(d_gate, wg).astype(BF16)
        du_ref[1] = _conv3_transpose(d_up, wu).astype(BF16)

    def col(k):
        return pl.BlockSpec((s, COL_TILE), lambda j: (0, k * nb + j))

    def wcol(k):
        return pl.BlockSpec((3, COL_TILE), lambda j: (0, k * nb + j))

    both = pl.BlockSpec((2, s, COL_TILE), lambda j: (0, 0, j))
    wsh = jax.ShapeDtypeStruct((3, D_FF), F32)
    return _pcall(body, name=name, grid=(nb,), in_specs=[col(0), col(1), wcol(0), wcol(1), col(0)],
                  out_specs=[both, wcol(0), wcol(0)], out_shape=[jax.ShapeDtypeStruct((2, s, D_FF), BF16), wsh, wsh],
                  compiler_params=_params("parallel"))(u, u, w, w, d_f)


def _sg_common(u, v, g, w_ref, bias, mixed_ref):
    rows = u.shape[0]
    gu = _gelu(u)
    gv = _gelu(v)
    xc = gv - jnp.mean(gv, axis=-1, keepdims=True)
    rstd = lax.rsqrt(jnp.mean(xc * xc, axis=-1, keepdims=True) + EPS)
    xhat = xc * rstd
    vn = xhat * g
    tril = lax.broadcasted_iota(jnp.int32, (HEAD, HEAD), 0) >= lax.broadcasted_iota(jnp.int32, (HEAD, HEAD), 1)
    wts = [jnp.where(tril, w_ref[grp], 0.0).astype(BF16) for grp in range(N_HEADS)]
    for n in range(rows // HEAD):
        for grp in range(N_HEADS):
            blk = vn[n * HEAD:(n + 1) * HEAD, grp * HEAD:(grp + 1) * HEAD].astype(BF16)
            mixed_ref[n * HEAD:(n + 1) * HEAD, grp * HEAD:(grp + 1) * HEAD] = _dot_nn(wts[grp], blk)
    mixed = mixed_ref[...] + jnp.concatenate([bias] * (rows // HEAD), axis=0)
    return gu, xhat, rstd, vn, mixed, wts, tril


def _sg_fwd(p, sg_w, bias, g, name):
    s = p.shape[0]

    def body(u_ref, v_ref, w_ref, b_ref, g_ref, o_ref, mixed_ref):
        gu, _, _, _, mixed, _, _ = _sg_common(u_ref[...], v_ref[...], g_ref[...], w_ref, b_ref[...], mixed_ref)
        o_ref[...] = (gu * mixed).astype(BF16)

    def half(k):
        return pl.BlockSpec((ROW_TILE, HALF), lambda i: (i, k))

    wspec = pl.BlockSpec((N_HEADS, HEAD, HEAD), lambda i: (0, 0, 0))
    bspec = pl.BlockSpec((HEAD, HALF), lambda i: (0, 0))
    gspec = pl.BlockSpec((1, HALF), lambda i: (0, 0))
    return _pcall(body, name=name, grid=(s // ROW_TILE,), in_specs=[half(0), half(1), wspec, bspec, gspec],
                  out_specs=half(0), out_shape=jax.ShapeDtypeStruct((s, 2 * HALF), BF16),
                  scratch_shapes=[pltpu.VMEM((ROW_TILE, HALF), F32)],
                  compiler_params=_params("parallel"))(p, p, sg_w, bias, g)


def _sg_bwd(p, sg_w, bias, g, d_cd, name):
    s = p.shape[0]
    nsteps = s // ROW_TILE

    def body(u_ref, v_ref, w_ref, b_ref, g_ref, d_ref, du_ref, dv_ref, dw_ref, db_ref, dg_ref,
             mixed_ref, dvn_ref, dbias_ref):
        i = pl.program_id(0)
        u, v, gain, d = u_ref[...], v_ref[...], g_ref[...], d_ref[...]
        gu, xhat, rstd, vn, mixed, wts, tril = _sg_common(u, v, gain, w_ref, b_ref[...], mixed_ref)

        @pl.when(i == 0)
        def _():
            dw_ref[...] = jnp.zeros_like(dw_ref)
            dg_ref[...] = jnp.zeros_like(dg_ref)
            dbias_ref[...] = jnp.zeros_like(dbias_ref)

        du_ref[...] = d * mixed * _gelu_grad(u)
        dm = d * gu
        for n in range(ROW_TILE // HEAD):
            rs = slice(n * HEAD, (n + 1) * HEAD)
            dbias_ref[...] += dm[rs, :]
            for grp in range(N_HEADS):
                cs = slice(grp * HEAD, (grp + 1) * HEAD)
                dm_blk = dm[rs, cs].astype(BF16)
                dw_ref[grp] += jnp.where(tril, _dot_nt(dm_blk, vn[rs, cs].astype(BF16)), 0.0)
                dvn_ref[rs, cs] = _dot_tn(wts[grp], dm_blk)
        dvn = dvn_ref[...]
        dg_ref[...] += jnp.sum(dvn * xhat, axis=0, keepdims=True)
        dxh = dvn * gain
        d_gv = rstd * (dxh - jnp.mean(dxh, axis=-1, keepdims=True) - xhat * jnp.mean(dxh * xhat, axis=-1, keepdims=True))
        dv_ref[...] = d_gv * _gelu_grad(v)

        @pl.when(i == nsteps - 1)
        def _():
            lane = lax.broadcasted_iota(jnp.int32, (HEAD, HEAD), 1)
            out = jnp.zeros((HEAD, HEAD), F32)
            for grp in range(N_HEADS):
                tot = jnp.sum(dbias_ref[:, grp * HEAD:(grp + 1) * HEAD], axis=1, keepdims=True)
                out = out + jnp.where(lane == grp, tot, 0.0)
            db_ref[...] = out

    def half(k):
        return pl.BlockSpec((ROW_TILE, HALF), lambda i: (i, k))

    wspec = pl.BlockSpec((N_HEADS, HEAD, HEAD), lambda i: (0, 0, 0))
    bspec = pl.BlockSpec((HEAD, HALF), lambda i: (0, 0))
    gspec = pl.BlockSpec((1, HALF), lambda i: (0, 0))
    dbspec = pl.BlockSpec((HEAD, HEAD), lambda i: (0, 0))
    act = jax.ShapeDtypeStruct((s, HALF), F32)
    return _pcall(body, name=name, grid=(nsteps,), in_specs=[half(0), half(1), wspec, bspec, gspec, half(0)],
                  out_specs=[half(0), half(0), wspec, dbspec, gspec],
                  out_shape=[act, act, jax.ShapeDtypeStruct((N_HEADS, HEAD, HEAD), F32),
                             jax.ShapeDtypeStruct((HEAD, HEAD), F32), jax.ShapeDtypeStruct((1, HALF), F32)],
                  scratch_shapes=[pltpu.VMEM((ROW_TILE, HALF), F32), pltpu.VMEM((ROW_TILE, HALF), F32),
                                  pltpu.VMEM((HEAD, HALF), F32)],
                  compiler_params=_params("arbitrary"))(p, p, sg_w, bias, g, d_cd)


def _fox_prep(f, b, name):
    s = f.shape[0]
    t = ATT_BLOCK

    def body(f_ref, b_ref, c_ref):
        tri = (lax.broadcasted_iota(jnp.int32, (t, t), 0) >= lax.broadcasted_iota(jnp.int32, (t, t), 1)).astype(BF16)
        carry = jnp.zeros((1, 128), F32)
        for n in range(s // t):
            lf = _log_sigmoid(f_ref[n * t:(n + 1) * t, :] + b_ref[...])
            hi, mid, lo = _split3(lf)
            c_ref[n * t:(n + 1) * t, :] = _dot_nn(tri, hi) + _dot_nn(tri, mid) + _dot_nn(tri, lo) + carry
            carry = carry + jnp.sum(lf, axis=0, keepdims=True)

    return _pcall(body, name=name, in_specs=[VMEM_SPEC, VMEM_SPEC], out_specs=VMEM_SPEC,
                  out_shape=jax.ShapeDtypeStruct((s, 128), F32))(f, b)


def _fox_post(drow, dcol, f, b, name):
    s = f.shape[0]
    t = ATT_BLOCK

    def body(drow_ref, dcol_ref, f_ref, b_ref, df_ref, db_ref):
        tri = (lax.broadcasted_iota(jnp.int32, (t, t), 1) >= lax.broadcasted_iota(jnp.int32, (t, t), 0)).astype(BF16)
        lane = lax.broadcasted_iota(jnp.int32, (t, 128), 1)
        carry = jnp.zeros((1, 128), F32)
        db = jnp.zeros((1, 128), F32)
        for n in reversed(range(s // t)):
            rs = slice(n * t, (n + 1) * t)
            dc = jnp.zeros((t, 128), F32)
            for h in range(N_HEADS):
                dc = jnp.where(lane == h, drow_ref[h, rs, :] - dcol_ref[rs, _head_cols(h)], dc)
            hi, mid, lo = _split3(dc)
            dlogf = _dot_nn(tri, hi) + _dot_nn(tri, mid) + _dot_nn(tri, lo) + carry
            carry = carry + jnp.sum(dc, axis=0, keepdims=True)
            df = dlogf * jax.nn.sigmoid(-(f_ref[rs, :] + b_ref[...]))
            df_ref[rs, :] = df
            db = db + jnp.sum(df, axis=0, keepdims=True)
        db_ref[...] = db

    return _pcall(body, name=name, in_specs=[VMEM_SPEC] * 4, out_specs=[VMEM_SPEC, VMEM_SPEC],
                  out_shape=[jax.ShapeDtypeStruct((s, 128), F32), jax.ShapeDtypeStruct((1, 128), F32)],
                  compiler_params=pltpu.CompilerParams(vmem_limit_bytes=VMEM_LIMIT_BYTES))(drow, dcol, f, b)


def _fox_specs(s):
    t = ATT_BLOCK
    ccol = pl.BlockSpec((HEADS_PER_STEP, t, 1), lambda g, i: (g, i, 0))
    crow = pl.BlockSpec((HEADS_PER_STEP, s // t, 1, t), lambda g, i: (g, 0, 0, 0))
    return ccol, crow


def _fox_fwd(p, c_col, c_row, mixed, name, after=()):
    s = p.shape[0]
    t = ATT_BLOCK

    def body(q_ref, k_ref, v_ref, cc_ref, cr_ref, mixed_in_ref, o_ref, lse_ref, mixed_ref):
        i = pl.program_id(1)
        heads = range(HEADS_PER_STEP)
        q = [q_ref[:, _head_cols(h)].astype(BF16) for h in heads]
        ct = [cc_ref[h] for h in heads]

        def tile(n, carry, diagonal):
            ks = pl.multiple_of(n * t, t)
            out = []
            for h in heads:
                acc, m, l = carry[h]
                kblk = k_ref[pl.ds(ks, t), _head_cols(h)].astype(BF16)
                vblk = v_ref[pl.ds(ks, t), _head_cols(h)].astype(BF16)
                logit = _dot_nt(q[h], kblk) * ATT_SCALE + ct[h] - cr_ref[h, n]
                if diagonal:
                    causal = lax.broadcasted_iota(jnp.int32, (t, t), 1) <= lax.broadcasted_iota(jnp.int32, (t, t), 0)
                    logit = jnp.where(causal, logit, NEG)
                m_new = jnp.maximum(m, jnp.max(logit, axis=1, keepdims=True))
                alpha = jnp.exp(m - m_new)
                pr = jnp.exp(logit - m_new)
                l = alpha * l + jnp.sum(pr, axis=1, keepdims=True)
                out.append((alpha * acc + _dot_nn(pr.astype(BF16), vblk), m_new, l))
            return tuple(out)

        init = tuple((jnp.zeros((t, HEAD), F32), jnp.full((t, 1), NEG, F32), jnp.zeros((t, 1), F32)) for _ in heads)
        carry = tile(i, lax.fori_loop(0, i, lambda n, c: tile(n, c, False), init), True)
        for h in heads:
            acc, m, l = carry[h]
            out = acc / l
            o_ref[:, _head_cols(h)] = out
            mixed_ref[:, _head_cols(h)] = out.astype(BF16)
            lse_ref[h] = m + jnp.log(l)

    qspec, kspec, vspec = _head_specs(s, (2 * N_HEADS, 3 * N_HEADS, 4 * N_HEADS))
    ccol, crow = _fox_specs(s)
    ospec = pl.BlockSpec((t, GROUP_W), lambda g, i: (i, g))
    mspec = pl.BlockSpec((t, GROUP_W), lambda g, i: (i, N_GROUPS + g))
    return _pcall(body, after=after, name=name, grid=(N_GROUPS, s // t),
                  in_specs=[qspec, kspec, vspec, ccol, crow, pl.BlockSpec(memory_space=pl.ANY)],
                  out_specs=[ospec, ccol, mspec],
                  out_shape=[jax.ShapeDtypeStruct((s, HALF), F32), jax.ShapeDtypeStruct((N_HEADS, s, 1), F32),
                             jax.ShapeDtypeStruct(mixed.shape, BF16)],
                  input_output_aliases={5: 2},
                  compiler_params=_params("parallel", "parallel", vmem_limit=ATT_VMEM_LIMIT_BYTES))(p, p, p, c_col, c_row, mixed)


def _fox_bwd(p, c_col, c_row, lse, d_cd, d_out, name):
    s = p.shape[0]
    t = ATT_BLOCK

    def body(q_ref, k_ref, v_ref, cc_ref, cr_ref, lse_ref, do_ref, o_ref, dq_ref, dk_ref, dv_ref, dcol_ref, drow_ref):
        i = pl.program_id(1)

        @pl.when(i == 0)
        def _():
            dk_ref[...] = jnp.zeros_like(dk_ref)
            dv_ref[...] = jnp.zeros_like(dv_ref)
            dcol_ref[...] = jnp.zeros_like(dcol_ref)

        heads = range(HEADS_PER_STEP)
        q = [q_ref[:, _head_cols(h)].astype(BF16) for h in heads]
        do = [do_ref[:, _head_cols(h)].astype(BF16) for h in heads]
        delta = [jnp.sum(do_ref[:, _head_cols(h)] * o_ref[:, _head_cols(h)], axis=1, keepdims=True) for h in heads]
        ct = [cc_ref[h] for h in heads]
        lse_v = [lse_ref[h] for h in heads]
        ones = jnp.ones((t, HEAD), BF16)

        def tile(n, carry, diagonal):
            ks = pl.multiple_of(n * t, t)
            out = []
            for h in heads:
                dq, drow = carry[h]
                kblk = k_ref[pl.ds(ks, t), _head_cols(h)].astype(BF16)
                vblk = v_ref[pl.ds(ks, t), _head_cols(h)].astype(BF16)
                logit = _dot_nt(q[h], kblk) * ATT_SCALE + ct[h] - cr_ref[h, n]
                pr = jnp.exp(logit - lse_v[h])
                if diagonal:
                    causal = lax.broadcasted_iota(jnp.int32, (t, t), 1) <= lax.broadcasted_iota(jnp.int32, (t, t), 0)
                    pr = jnp.where(causal, pr, 0.0)
                ds = pr * (_dot_nt(do[h], vblk) - delta[h])
                dsb = ds.astype(BF16)
                dv_ref[pl.ds(ks, t), _head_cols(h)] += _dot_tn(pr.astype(BF16), do[h])
                dk_ref[pl.ds(ks, t), _head_cols(h)] += _dot_tn(dsb, q[h]) * ATT_SCALE
                dcol_ref[pl.ds(ks, t), _head_cols(h)] += _dot_tn(dsb, ones)
                out.append((dq + _dot_nn(dsb, kblk) * ATT_SCALE,
                            drow + jnp.sum(dsb.astype(F32), axis=1, keepdims=True)))
            return tuple(out)

        init = tuple((jnp.zeros((t, HEAD), F32), jnp.zeros((t, 1), F32)) for _ in heads)
        carry = tile(i, lax.fori_loop(0, i, lambda n, c: tile(n, c, False), init), True)
        for h in heads:
            dq_ref[:, _head_cols(h)] = carry[h][0]
            drow_ref[h] = carry[h][1]

    qspec, kspec, vspec = _head_specs(s, (2 * N_HEADS, 3 * N_HEADS, 4 * N_HEADS))
    ccol, crow = _fox_specs(s)
    dospec = pl.BlockSpec((t, GROUP_W), lambda g, i: (i, N_GROUPS + g))
    blk = pl.BlockSpec((t, GROUP_W), lambda g, i: (i, g))
    whole = pl.BlockSpec((s, GROUP_W), lambda g, i: (0, g))
    shape = jax.ShapeDtypeStruct((s, HALF), F32)
    return _pcall(body, name=name, grid=(N_GROUPS, s // t),
                  in_specs=[qspec, kspec, vspec, ccol, crow, ccol, dospec, blk],
                  out_specs=[blk, whole, whole, whole, ccol],
                  out_shape=[shape, shape, shape, shape, jax.ShapeDtypeStruct((N_HEADS, s, 1), F32)],
                  compiler_params=_params("parallel", "arbitrary", vmem_limit=ATT_VMEM_LIMIT_BYTES))(p, p, p, c_col, c_row, lse, d_cd, d_out)


def _row_tile(rows, cap):
    for t in (256, 128, 64, 32, 16, 8):
        if t <= cap and rows % t == 0:
            return t
    return rows


def _adamw_small(ws, gs, ms, vs, name):
    n = len(ws)
    c1 = 1.0 / (1.0 - ADAM_B1 ** ADAM_STEP)
    c2 = 1.0 / (1.0 - ADAM_B2 ** ADAM_STEP)

    def body(*refs):
        for k in range(n):
            w_ref, g_ref, m_ref, v_ref = (refs[j * n + k] for j in range(4))
            d_ref, nm_ref, nv_ref = (refs[(4 + j) * n + k] for j in range(3))
            gv = g_ref[...]
            nm = ADAM_B1 * m_ref[...] + (1.0 - ADAM_B1) * gv
            nv = ADAM_B2 * v_ref[...] + (1.0 - ADAM_B2) * (gv * gv)
            nm_ref[...] = nm
            nv_ref[...] = nv
            d_ref[...] = -ADAM_LR * ((nm * c1) / (jnp.sqrt(nv * c2) + ADAM_EPS) + ADAM_WD * w_ref[...])

    shapes = [jax.ShapeDtypeStruct(w.shape, F32) for w in ws] * 3
    outs = _pcall(body, name=name, in_specs=[VMEM_SPEC] * (4 * n), out_specs=[VMEM_SPEC] * (3 * n), out_shape=shapes,
                  compiler_params=pltpu.CompilerParams(vmem_limit_bytes=VMEM_LIMIT_BYTES))(*ws, *gs, *ms, *vs)
    return outs[:n], outs[n:2 * n], outs[2 * n:]


def _half_shape(whole_shape, kind):
    if kind == "col":
        return (whole_shape[0] // 2, whole_shape[1] // 4)
    if kind == "row":
        return (whole_shape[0] // 8, whole_shape[1])
    return (whole_shape[1] // 2, whole_shape[2])


def _own_half_spec(whole_shape, kind, tr):
    hr, hc = _half_shape(whole_shape, kind)
    nb = hr // tr
    if kind == "col":
        return pl.BlockSpec((tr, hc), lambda i, pos: (pos[1] * nb + i, pos[0]))
    if kind == "row":
        return pl.BlockSpec((tr, hc), lambda i, pos: ((2 * pos[0] + pos[1]) * nb + i, 0))
    return pl.BlockSpec((None, tr, hc), lambda i, pos: (pos[0], pos[1] * nb + i, 0))


def _sum_partials(pos, grad, landed, kind, name):
    hr, hc = _half_shape(grad.shape, kind)
    tr = _row_tile(hr, 128)

    def body(pos_ref, g_ref, p_ref, o_ref):
        acc = g_ref[...].astype(F32)
        for k in range(N_DEV - 1):
            acc = acc + p_ref[k].astype(F32)
        o_ref[...] = acc

    grid_spec = pltpu.PrefetchScalarGridSpec(
        num_scalar_prefetch=1, grid=(hr // tr,),
        in_specs=[_own_half_spec(grad.shape, kind, tr), pl.BlockSpec((N_DEV - 1, tr, hc), lambda i, pos: (0, i, 0))],
        out_specs=pl.BlockSpec((tr, hc), lambda i, pos: (i, 0)))
    return _pcall(body, name=name, grid_spec=grid_spec, out_shape=jax.ShapeDtypeStruct((hr, hc), F32),
                  compiler_params=_params("parallel"))(pos, grad, landed)


def _adamw_shard(pos, w, g_mine, g_sibling, m, v, name):
    hr, hc = g_mine.shape
    tr = _row_tile(hr, 128)
    nb = hr // tr
    c1 = 1.0 / (1.0 - ADAM_B1 ** ADAM_STEP)
    c2 = 1.0 / (1.0 - ADAM_B2 ** ADAM_STEP)

    def body(pos_ref, w_ref, gm_ref, gs_ref, m_ref, v_ref, g_ref, d_ref, nm_ref, nv_ref):
        mine = (pl.program_id(0) // nb) == pos_ref[1]
        gv = jnp.where(mine, gm_ref[...], gs_ref[...])
        nm = ADAM_B1 * m_ref[...] + (1.0 - ADAM_B1) * gv
        nv = ADAM_B2 * v_ref[...] + (1.0 - ADAM_B2) * (gv * gv)
        g_ref[...] = gv
        nm_ref[...] = nm
        nv_ref[...] = nv
        d_ref[...] = -ADAM_LR * ((nm * c1) / (jnp.sqrt(nv * c2) + ADAM_EPS) + ADAM_WD * w_ref[...])

    full = pl.BlockSpec((tr, hc), lambda i, pos: (i, 0))
    mine_spec = pl.BlockSpec((tr, hc), lambda i, pos: (jnp.clip(i - pos[1] * nb, 0, nb - 1), 0))
    sib_spec = pl.BlockSpec((tr, hc), lambda i, pos: (jnp.clip(i - (1 - pos[1]) * nb, 0, nb - 1), 0))
    grid_spec = pltpu.PrefetchScalarGridSpec(
        num_scalar_prefetch=1, grid=(2 * nb,), in_specs=[full, mine_spec, sib_spec, full, full], out_specs=[full] * 4)
    shape = jax.ShapeDtypeStruct((2 * hr, hc), F32)
    return _pcall(body, name=name, grid_spec=grid_spec, out_shape=[shape] * 4,
                  compiler_params=_params("parallel"))(pos, w, g_mine, g_sibling, m, v)


def _place_shard(pos, shard, kind, name, after=()):
    rows, cols = shard.shape
    tr = _row_tile(rows, 256)
    nb = rows // tr
    if kind == "col":
        out_spec = pl.BlockSpec((tr, cols), lambda i, pos: (i, pos[0]))
    elif kind == "row":
        out_spec = pl.BlockSpec((tr, cols), lambda i, pos: (pos[0] * nb + i, 0))
    else:
        out_spec = pl.BlockSpec((None, tr, cols), lambda i, pos: (pos[0], i, 0))

    def body(pos_ref, s_ref, *rest):
        rest[-1][...] = s_ref[...].astype(BF16)

    grid_spec = pltpu.PrefetchScalarGridSpec(
        num_scalar_prefetch=1, grid=(nb,),
        in_specs=[pl.BlockSpec((tr, cols), lambda i, pos: (i, 0))] + [pl.BlockSpec(memory_space=pl.ANY)] * len(after),
        out_specs=out_spec)
    return _pcall(body, name=name, grid_spec=grid_spec,
                  out_shape=jax.ShapeDtypeStruct(_whole_shape(shard.shape, kind), BF16),
                  compiler_params=_params("parallel"))(pos, shard, *after)


N_DEV = 8
RELATIONS = [(r >> 2 & 1, r >> 1 & 1, r & 1) for r in range(1, N_DEV)]


def _position():
    return lax.axis_index("x"), lax.axis_index("y"), lax.axis_index("c")


def _related(pos, rel):
    return tuple(1 - p if f else p for p, f in zip(pos, rel))


def _index(pos):
    return 4 * pos[0] + 2 * pos[1] + pos[2]


def _window(ref, kind, pos):
    px, py, pc = pos
    j = 2 * px + py
    if kind == "col":
        r, c = ref.shape
        return ref.at[pl.ds(pc * (r // 2), r // 2), pl.ds(pl.multiple_of(j * (c // 4), 128), c // 4)]
    if kind == "row":
        rj = ref.shape[0] // 4
        return ref.at[pl.ds(j * rj + pc * (rj // 2), rj // 2), :]
    r = ref.shape[1]
    return ref.at[j, pl.ds(pc * (r // 2), r // 2), :]


def _whole_shape(shard_shape, kind):
    r, c = shard_shape
    return {"col": (r, 4 * c), "row": (4 * r, c), "maj": (4, r, c)}[kind]


SEM_SPEC = pl.BlockSpec(memory_space=pltpu.SEMAPHORE)
ANY_SPEC = pl.BlockSpec(memory_space=pl.ANY)
DATAFLOW = pltpu.SideEffectType.DATAFLOW_SIDE_EFFECTING
TOKEN = jax.ShapeDtypeStruct((8, 128), F32)


def _hbm(a):
    return pltpu.with_memory_space_constraint(a, pltpu.HBM)


def _chips(x, y):
    return [(1 - x, y), (x, 1 - y), (1 - x, 1 - y)]


def _split_start(body, name, buffers, n_sems, after=()):
    n = len(buffers)

    def wrapped(*refs):
        body(refs[:n], refs[n], refs[n + 1])
        refs[-1][...] = jnp.zeros_like(refs[-1])

    outs = _pcall(
        wrapped, after=after, name=name, in_specs=[HBM_SPEC] * n,
        out_specs=[SEM_SPEC, SEM_SPEC] + [HBM_SPEC] * n + [VMEM_SPEC],
        out_shape=[pltpu.SemaphoreType.DMA(n_sems), pltpu.SemaphoreType.DMA(n_sems)]
        + [pltpu.HBM(b.shape, b.dtype) for b in buffers] + [TOKEN],
        input_output_aliases={i: 2 + i for i in range(n)},
        compiler_params=pltpu.CompilerParams(has_side_effects=DATAFLOW))(*[_hbm(b) for b in buffers])
    return outs[0], outs[1], list(outs[2:2 + n]), outs[2 + n]


def _split_wait(body, name, buffers, send_sems, recv_sems, after):
    n = len(buffers)
    after = list(after) if isinstance(after, (list, tuple)) else [after]

    def wrapped(*refs):
        body(refs[:n], refs[n], refs[n + 1])

    outs = _pcall(
        wrapped, name=name, in_specs=[HBM_SPEC] * n + [SEM_SPEC, SEM_SPEC] + [ANY_SPEC] * len(after),
        out_specs=[HBM_SPEC] * n, out_shape=[pltpu.HBM(b.shape, b.dtype) for b in buffers],
        input_output_aliases={i: i for i in range(n)},
        compiler_params=pltpu.CompilerParams(has_side_effects=DATAFLOW))(*buffers, send_sems, recv_sems, *after)
    return list(outs)


def _gather_start(wholes, kinds, name, after=()):
    def body(w_refs, send_sems, recv_sems):
        x, y, c = _position()
        for w, ref in enumerate(w_refs):
            mine = _window(ref, kinds[w], (x, y, c))
            for k, chip in enumerate(_chips(x, y)):
                pltpu.make_async_remote_copy(src_ref=mine, dst_ref=mine, send_sem=send_sems.at[3 * w + k],
                                             recv_sem=recv_sems.at[3 * w + k], device_id=(*chip, c),
                                             device_id_type=MESH).start()

    return _split_start(body, name, wholes, (3 * len(wholes),), after)


def _gather_forward(wholes, kinds, send1, recv1, after, name):
    n = len(wholes)

    def wrapped(*refs):
        w_refs, s1, r1, s2, r2 = refs[:n], refs[n], refs[n + 1], refs[n + 3], refs[n + 4]
        x, y, c = _position()
        for k, chip in enumerate(_chips(x, y)):
            for w, ref in enumerate(w_refs):
                theirs = _window(ref, kinds[w], (*chip, c))
                pltpu.make_async_remote_copy(src_ref=theirs, dst_ref=theirs, send_sem=s1.at[3 * w + k],
                                             recv_sem=r1.at[3 * w + k], device_id=(*chip, c),
                                             device_id_type=MESH).wait_recv()
                pltpu.make_async_remote_copy(src_ref=theirs, dst_ref=theirs, send_sem=s2.at[3 * w + k],
                                             recv_sem=r2.at[3 * w + k], device_id=(x, y, 1 - c),
                                             device_id_type=MESH).start()
        for w, ref in enumerate(w_refs):
            mine = _window(ref, kinds[w], (x, y, c))
            for k, chip in enumerate(_chips(x, y)):
                pltpu.make_async_remote_copy(src_ref=mine, dst_ref=mine, send_sem=s1.at[3 * w + k],
                                             recv_sem=r1.at[3 * w + k], device_id=(*chip, c),
                                             device_id_type=MESH).wait_send()
        refs[-1][...] = jnp.zeros_like(refs[-1])

    outs = _pcall(
        wrapped, name=name, in_specs=[HBM_SPEC] * n + [SEM_SPEC, SEM_SPEC, ANY_SPEC],
        out_specs=[SEM_SPEC, SEM_SPEC] + [HBM_SPEC] * n + [VMEM_SPEC],
        out_shape=[pltpu.SemaphoreType.DMA((3 * n,)), pltpu.SemaphoreType.DMA((3 * n,))]
        + [pltpu.HBM(b.shape, b.dtype) for b in wholes] + [TOKEN],
        input_output_aliases={i: 2 + i for i in range(n)},
        compiler_params=pltpu.CompilerParams(has_side_effects=DATAFLOW))(*wholes, send1, recv1, after)
    return outs[0], outs[1], list(outs[2:2 + n]), outs[2 + n]


def _gather_finish(wholes, kinds, send2, recv2, after, name):
    def body(w_refs, s2, r2):
        x, y, c = _position()
        for k, chip in enumerate(_chips(x, y)):
            for w, ref in enumerate(w_refs):
                sent = _window(ref, kinds[w], (*chip, c))
                got = _window(ref, kinds[w], (*chip, 1 - c))
                pltpu.make_async_remote_copy(src_ref=sent, dst_ref=got, send_sem=s2.at[3 * w + k],
                                             recv_sem=r2.at[3 * w + k], device_id=(x, y, 1 - c),
                                             device_id_type=MESH).wait()

    return _split_wait(body, name, wholes, send2, recv2, after)


def _gather_small_start(small, after):
    def body(refs, send_sems, recv_sems):
        s_ref, land_ref = refs
        x, y, c = _position()
        for k, chip in enumerate(_chips(x, y)):
            pltpu.make_async_remote_copy(src_ref=s_ref, dst_ref=land_ref.at[2 * x + y], send_sem=send_sems.at[k],
                                         recv_sem=recv_sems.at[k], device_id=(*chip, c), device_id_type=MESH).start()

    return _split_start(body, "gather_small_start", [small, lax.empty((4,) + small.shape, small.dtype)], (3,), after)


def _gather_small_finish(send, recv, thru, after):
    def body(refs, send_sems, recv_sems):
        s_ref, land_ref = refs
        x, y, c = _position()
        for k, chip in enumerate(_chips(x, y)):
            pltpu.make_async_remote_copy(src_ref=s_ref, dst_ref=land_ref.at[2 * chip[0] + chip[1]],
                                         send_sem=send_sems.at[k], recv_sem=recv_sems.at[k], device_id=(*chip, c),
                                         device_id_type=MESH).wait()

    return _split_wait(body, "gather_small_finish", thru, send, recv, after)[1]


def _scatter_copies(g_refs, land_refs, kinds, send_sems, recv_sems):
    me = _position()
    copies = []
    for k, rel in enumerate(RELATIONS):
        peer = _related(me, rel)
        for w, (g_ref, land_ref) in enumerate(zip(g_refs, land_refs)):
            copies.append(pltpu.make_async_remote_copy(
                src_ref=_window(g_ref, kinds[w], peer), dst_ref=land_ref.at[k],
                send_sem=send_sems.at[7 * w + k], recv_sem=recv_sems.at[7 * w + k], device_id=peer,
                device_id_type=MESH))
    return copies


def _scatter_start(grads, kinds, name):
    n = len(grads)
    lands = [lax.empty((N_DEV - 1,) + _half_shape(g.shape, kd), g.dtype) for g, kd in zip(grads, kinds)]

    def body(refs, send_sems, recv_sems):
        for cp in _scatter_copies(refs[:n], refs[n:], kinds, send_sems, recv_sems):
            cp.start()

    send, recv, thru, token = _split_start(body, name, list(grads) + lands, ((N_DEV - 1) * n,))
    return send, recv, thru[:n], thru[n:], token


def _scatter_wait(grads, lands, kinds, send, recv, after, name):
    n = len(grads)

    def body(refs, send_sems, recv_sems):
        for cp in _scatter_copies(refs[:n], refs[n:], kinds, send_sems, recv_sems):
            cp.wait()

    out = _split_wait(body, name, list(grads) + list(lands), send, recv, after)
    return out[:n], out[n:]


def _swap_start(halves, name):
    n = len(halves)
    lands = [lax.empty(h.shape, h.dtype) for h in halves]

    def body(refs, send_sems, recv_sems):
        x, y, c = _position()
        for w in range(n):
            pltpu.make_async_remote_copy(src_ref=refs[w], dst_ref=refs[n + w], send_sem=send_sems.at[w],
                                         recv_sem=recv_sems.at[w], device_id=(x, y, 1 - c), device_id_type=MESH).start()

    send, recv, thru, token = _split_start(body, name, list(halves) + lands, (n,))
    return send, recv, thru[:n], thru[n:], token


def _swap_wait(halves, lands, send, recv, after, name):
    n = len(halves)

    def body(refs, send_sems, recv_sems):
        x, y, c = _position()
        for w in range(n):
            pltpu.make_async_remote_copy(src_ref=refs[w], dst_ref=refs[n + w], send_sem=send_sems.at[w],
                                         recv_sem=recv_sems.at[w], device_id=(x, y, 1 - c), device_id_type=MESH).wait()

    out = _split_wait(body, name, list(halves) + list(lands), send, recv, after)
    return out[:n], out[n:]


def _allreduce_small(v, after=()):
    rows = v.shape[0]

    def body(v_ref, o_ref, recv_ref, send_sems, recv_sems):
        me = _position()
        recv_ref[_index(me)] = v_ref[...]
        sends = []
        for k, rel in enumerate(RELATIONS):
            peer = _related(me, rel)
            cp = pltpu.make_async_remote_copy(
                src_ref=v_ref, dst_ref=recv_ref.at[_index(me)],
                send_sem=send_sems.at[k], recv_sem=recv_sems.at[k], device_id=peer, device_id_type=MESH)
            cp.start()
            sends.append(cp)
        for k, rel in enumerate(RELATIONS):
            peer = _related(me, rel)
            pltpu.make_async_remote_copy(
                src_ref=v_ref, dst_ref=recv_ref.at[_index(peer)],
                send_sem=send_sems.at[k], recv_sem=recv_sems.at[k], device_id=peer, device_id_type=MESH).wait_recv()
        for cp in sends:
            cp.wait_send()
        acc = recv_ref[0]
        for k in range(1, N_DEV):
            acc = acc + recv_ref[k]
        o_ref[...] = acc

    return _pcall(body, after=after, name="allreduce_small", in_specs=[VMEM_SPEC], out_specs=VMEM_SPEC,
                  out_shape=jax.ShapeDtypeStruct((rows, 128), F32),
                  scratch_shapes=[pltpu.VMEM((N_DEV, rows, 128), F32), pltpu.SemaphoreType.DMA((7,)),
                                  pltpu.SemaphoreType.DMA((7,))],
                  compiler_params=pltpu.CompilerParams(vmem_limit_bytes=VMEM_LIMIT_BYTES))(v)


def _small_copies(refs, send_sems, recv_sems):
    v_ref, land_ref = refs
    me = _position()
    return [pltpu.make_async_remote_copy(src_ref=v_ref, dst_ref=land_ref.at[_index(me)], send_sem=send_sems.at[k],
                                         recv_sem=recv_sems.at[k], device_id=_related(me, rel), device_id_type=MESH)
            for k, rel in enumerate(RELATIONS)]


def _small_wait_copies(refs, send_sems, recv_sems):
    v_ref, land_ref = refs
    me = _position()
    return [pltpu.make_async_remote_copy(src_ref=v_ref, dst_ref=land_ref.at[_index(_related(me, rel))],
                                         send_sem=send_sems.at[k], recv_sem=recv_sems.at[k],
                                         device_id=_related(me, rel), device_id_type=MESH)
            for k, rel in enumerate(RELATIONS)]


def _small_reduce_start(v, name, after):
    def body(refs, send_sems, recv_sems):
        for cp in _small_copies(refs, send_sems, recv_sems):
            cp.start()

    send, recv, thru, token = _split_start(body, name, [v, lax.empty((N_DEV,) + v.shape, v.dtype)], (N_DEV - 1,), after)
    return send, recv, thru, token


def _small_reduce_finish(me, send, recv, thru, after, name):
    def body(refs, send_sems, recv_sems):
        for cp in _small_wait_copies(refs, send_sems, recv_sems):
            cp.wait()

    v, landed = _split_wait(body, name + "_wait", thru, send, recv, after)

    def add(me_ref, v_ref, land_ref, o_ref):
        acc = jnp.where(me_ref[0] == 0, v_ref[...], land_ref[0])
        for d in range(1, N_DEV):
            acc = acc + jnp.where(me_ref[0] == d, v_ref[...], land_ref[d])
        o_ref[...] = acc

    return _pcall(add, name=name + "_sum",
                  in_specs=[pl.BlockSpec(memory_space=pltpu.SMEM), VMEM_SPEC, VMEM_SPEC], out_specs=VMEM_SPEC,
                  out_shape=jax.ShapeDtypeStruct(v.shape, F32),
                  compiler_params=pltpu.CompilerParams(vmem_limit_bytes=VMEM_LIMIT_BYTES))(me, v, landed)


def _pack(arrays):
    flat = []
    for a in arrays:
        a = a.reshape(-1)
        flat.append(jnp.pad(a, (0, -a.shape[0] % 128)))
    flat = jnp.concatenate(flat)
    flat = jnp.pad(flat, (0, -flat.shape[0] % 1024))
    return flat.reshape(-1, 128)


def _unpack(packed, shapes):
    flat = packed.reshape(-1)
    out, at = [], 0
    for shp in shapes:
        size = 1
        for d in shp:
            size *= d
        out.append(flat[at:at + size].reshape(shp))
        at += size + (-size % 128)
    return out


WEIGHTS = ['l0_mix_norm_g', 'l0_w_in', 'l0_sc_conv_w', 'l0_w_out', 'l0_ffn_norm_g', 'l0_ffn_up', 'l0_ffn_conv_w',
           'l0_ffn_down', 'l1_mix_norm_g', 'l1_w_in', 'l1_fox_b_f', 'l1_sg_w', 'l1_sg_b', 'l1_sg_norm_g', 'l1_w_out',
           'l1_ffn_norm_g', 'l1_ffn_up', 'l1_ffn_conv_w', 'l1_ffn_down', 'final_norm_g']
BIG = {'l0_w_in': 'col', 'l0_w_out': 'row', 'l0_ffn_up': 'col', 'l0_ffn_down': 'row',
       'l1_w_in': 'maj', 'l1_w_out': 'row', 'l1_ffn_up': 'col', 'l1_ffn_down': 'row'}
GATHER_GROUPS = [['l0_w_in'], ['l0_w_out'], ['l0_ffn_up'], ['l0_ffn_down'], ['l1_w_in'], ['l1_w_out'],
                 ['l1_ffn_up'], ['l1_ffn_down']]
CONV = ['l0_sc_conv_w', 'l0_ffn_conv_w', 'l1_ffn_conv_w']
SMALL = [n for n in WEIGHTS if n not in BIG]
LATE_SMALL = ['l0_sc_conv_w', 'l0_mix_norm_g']
IN_CD = 5 * HALF + N_HEADS


def _ffn_forward(x, g, get_up, behind_act, get_down, conv_w, tag):
    h = _rmsnorm_fwd(x, g, tag + "_norm")
    u = _matmul(h, get_up(h), "nn", F32, tag + "_up")
    f = _ffn_act_fwd(u, conv_w, tag + "_act", after=behind_act(u))
    w_down, tokens = get_down(f)
    return _matmul(f, w_down, "nn", F32, tag + "_down", res=x, after=tokens), (h, u, f)


def _ffn_backward(x, g, w_up, conv_w, w_down, saved, d_out, send_up, send_down, tag):
    h, u, f = saved
    dw_down = _matmul(f, d_out, "tn", BF16, tag + "_dwdown")
    d_f = _matmul(d_out, w_down, "nt", F32, tag + "_df", after=[send_down(dw_down)])
    du, dcw_gate, dcw_up = _ffn_act_bwd(u, conv_w, d_f, tag + "_dact")
    dw_up = _matmul(h, du, "tn", BF16, tag + "_dwup")
    dh = _matmul(du, w_up, "nt", F32, tag + "_dh", after=[send_up(dw_up)])
    dx, dg = _rmsnorm_bwd(x, g, dh, d_out, tag + "_dnorm")
    return dx, dg, jnp.concatenate([dcw_gate, dcw_up], axis=1)


def kernel(x, l0_mix_norm_g, l0_w_in, l0_sc_conv_w, l0_w_out, l0_ffn_norm_g, l0_ffn_up, l0_ffn_conv_w, l0_ffn_down, l1_mix_norm_g, l1_w_in, l1_fox_b_f, l1_sg_w, l1_sg_b, l1_sg_norm_g, l1_w_out, l1_ffn_norm_g, l1_ffn_up, l1_ffn_conv_w, l1_ffn_down, final_norm_g, loss_target, m_l0_mix_norm_g, m_l0_w_in, m_l0_sc_conv_w, m_l0_w_out, m_l0_ffn_norm_g, m_l0_ffn_up, m_l0_ffn_conv_w, m_l0_ffn_down, m_l1_mix_norm_g, m_l1_w_in, m_l1_fox_b_f, m_l1_sg_w, m_l1_sg_b, m_l1_sg_norm_g, m_l1_w_out, m_l1_ffn_norm_g, m_l1_ffn_up, m_l1_ffn_conv_w, m_l1_ffn_down, m_final_norm_g, v_l0_mix_norm_g, v_l0_w_in, v_l0_sc_conv_w, v_l0_w_out, v_l0_ffn_norm_g, v_l0_ffn_up, v_l0_ffn_conv_w, v_l0_ffn_down, v_l1_mix_norm_g, v_l1_w_in, v_l1_fox_b_f, v_l1_sg_w, v_l1_sg_b, v_l1_sg_norm_g, v_l1_w_out, v_l1_ffn_norm_g, v_l1_ffn_up, v_l1_ffn_conv_w, v_l1_ffn_down, v_final_norm_g):
    given = (l0_mix_norm_g, l0_w_in, l0_sc_conv_w, l0_w_out, l0_ffn_norm_g, l0_ffn_up, l0_ffn_conv_w, l0_ffn_down, l1_mix_norm_g, l1_w_in, l1_fox_b_f, l1_sg_w, l1_sg_b, l1_sg_norm_g, l1_w_out, l1_ffn_norm_g, l1_ffn_up, l1_ffn_conv_w, l1_ffn_down, final_norm_g)
    given_m = (m_l0_mix_norm_g, m_l0_w_in, m_l0_sc_conv_w, m_l0_w_out, m_l0_ffn_norm_g, m_l0_ffn_up, m_l0_ffn_conv_w, m_l0_ffn_down, m_l1_mix_norm_g, m_l1_w_in, m_l1_fox_b_f, m_l1_sg_w, m_l1_sg_b, m_l1_sg_norm_g, m_l1_w_out, m_l1_ffn_norm_g, m_l1_ffn_up, m_l1_ffn_conv_w, m_l1_ffn_down, m_final_norm_g)
    given_v = (v_l0_mix_norm_g, v_l0_w_in, v_l0_sc_conv_w, v_l0_w_out, v_l0_ffn_norm_g, v_l0_ffn_up, v_l0_ffn_conv_w, v_l0_ffn_down, v_l1_mix_norm_g, v_l1_w_in, v_l1_fox_b_f, v_l1_sg_w, v_l1_sg_b, v_l1_sg_norm_g, v_l1_w_out, v_l1_ffn_norm_g, v_l1_ffn_up, v_l1_ffn_conv_w, v_l1_ffn_down, v_final_norm_g)
    wt = dict(zip(WEIGHTS, given))
    mom = dict(zip(WEIGHTS, given_m))
    var = dict(zip(WEIGHTS, given_v))
    s = x.shape[1]
    t = ATT_BLOCK
    x0, target = x[0], loss_target[0]
    chip = 2 * lax.axis_index("x") + lax.axis_index("y")

    pos = jnp.stack([chip, lax.axis_index("c")]).astype(jnp.int32)

    conv_shard = jnp.concatenate([wt[n] for n in CONV], axis=1)
    gathers, token = [], ()
    for gi, names in enumerate(GATHER_GROUPS):
        placed = [_place_shard(pos, wt[n], BIG[n], "place_" + n, token) for n in names]
        send, recv, thru, tok = _gather_start(placed, [BIG[n] for n in names], "gather_start_%d" % gi, token)
        gathers.append((send, recv, thru))
        if gi == 0:
            conv_started = _gather_small_start(conv_shard, [tok])
            tok = conv_started[3]
        token = [tok]
    token = token[0]
    full = {}
    conv_full = {}

    def finish_conv(after):
        landed = _gather_small_finish(*conv_started[:3], after)
        conv_all = lax.dynamic_update_slice(landed, conv_shard[None], (chip, 0, 0))
        at = 0
        for n in CONV:
            cw = wt[n].shape[1]
            conv_full[n] = jnp.transpose(conv_all[:, :, at:at + cw], (1, 0, 2)).reshape(3, 4 * cw)
            at += cw

    def forward_gather(gi, after):
        send, recv, thru = gathers[gi]
        kinds = [BIG[n] for n in GATHER_GROUPS[gi]]
        gathers[gi] = _gather_forward(thru, kinds, send, recv, after, "gather_forward_%d" % gi)
        return gathers[gi][3]

    def finish_gather(gi, after):
        send, recv, thru, tok = gathers[gi]
        names = GATHER_GROUPS[gi]
        wholes = _gather_finish(thru, [BIG[n] for n in names], send, recv, tok if after is None else after,
                                "gather_finish_%d" % gi)
        full.update(zip(names, wholes))

    def vec(name):
        return wt[name].reshape(1, -1)

    h0 = _rmsnorm_fwd(x0, vec('l0_mix_norm_g'), "l0_mix_norm", after=[token])
    forward_gather(0, h0)
    finish_gather(0, None)
    p0 = _matmul(h0, full['l0_w_in'], "nn", F32, "l0_in")
    a_out, sb_carries = _sb_fwd(p0, "l0_sb", after=[forward_gather(1, p0)])
    finish_gather(1, a_out)
    finish_conv(p0)
    ab0 = _sc_fwd(p0, conv_full['l0_sc_conv_w'], a_out, "l0_sc")
    x1 = _matmul(ab0, full['l0_w_out'], "nn", F32, "l0_out", res=x0, after=[forward_gather(2, ab0)])

    def ffn_weights(up_group, next_group):
        def get_up(h):
            finish_gather(up_group, h)
            return full[GATHER_GROUPS[up_group][0]]

        def behind_act(u):
            return [forward_gather(up_group + 1, u)]

        def get_down(f):
            finish_gather(up_group + 1, f)
            return full[GATHER_GROUPS[up_group + 1][0]], ([forward_gather(next_group, f)] if next_group else ())

        return get_up, behind_act, get_down

    x2, ffn0_saved = _ffn_forward(x1, vec('l0_ffn_norm_g'), *ffn_weights(2, 4), conv_full['l0_ffn_conv_w'], "l0_ffn")
    h2 = _rmsnorm_fwd(x2, vec('l1_mix_norm_g'), "l1_mix_norm")
    finish_gather(4, h2)
    w_in1 = jnp.transpose(full['l1_w_in'], (1, 0, 2)).reshape(D_MODEL, IN_CD)
    w_in1_main = w_in1[:, :5 * HALF]
    w_in1_f = jnp.pad(w_in1[:, 5 * HALF:], ((0, 0), (0, 128 - N_HEADS)))
    p1 = _matmul(h2, w_in1_main, "nn", F32, "l1_in")
    f_logit = _matmul(h2, w_in1_f, "nn", F32, "l1_in_f", after=[forward_gather(5, p1)])
    b_f = jnp.pad(wt['l1_fox_b_f'], (0, 128 - N_HEADS)).reshape(1, 128)
    c_heads = _fox_prep(f_logit, b_f, "l1_fox_prep")[:, :N_HEADS].T
    c_col = c_heads[:, :, None]
    c_row = c_heads.reshape(N_HEADS, s // t, 1, t)
    sg_bias = jnp.repeat(wt['l1_sg_b'].T, HEAD, axis=1)
    sg_gain = vec('l1_sg_norm_g')
    c_out = _sg_fwd(p1, wt['l1_sg_w'], sg_bias, sg_gain, "l1_sg")
    d_out, lse, cd1 = _fox_fwd(p1, c_col, c_row, c_out, "l1_fox", after=[forward_gather(6, c_out)])
    finish_gather(5, d_out)
    x3 = _matmul(cd1, full['l1_w_out'], "nn", F32, "l1_out", res=x2)
    x4, ffn1_saved = _ffn_forward(x3, vec('l1_ffn_norm_g'), *ffn_weights(6, None), conv_full['l1_ffn_conv_w'], "l1_ffn")
    dx4, dg_final, loss_part = _loss_head(x4, vec('final_norm_g'), target, "loss_head")

    grads = {'final_norm_g': dg_final}
    scatters = []

    def send_grads(names):
        def start(*group):
            send, recv, thru, lands, tok = _scatter_start(list(group), [BIG[n] for n in names],
                                                          "scatter_start_%d" % len(scatters))
            scatters.append((names, send, recv, thru, lands))
            return tok
        return start

    dx3, grads['l1_ffn_norm_g'], grads['l1_ffn_conv_w'] = _ffn_backward(
        x3, vec('l1_ffn_norm_g'), full['l1_ffn_up'], conv_full['l1_ffn_conv_w'], full['l1_ffn_down'], ffn1_saved, dx4,
        send_grads(['l1_ffn_up']), send_grads(['l1_ffn_down']), "l1_ffn")
    dw_out1 = _matmul(cd1, dx3, "tn", BF16, "l1_dwout")
    d_cd = _matmul(dx3, full['l1_w_out'], "nt", F32, "l1_dcd")
    du, dv, grads['l1_sg_w'], db_sg, grads['l1_sg_norm_g'] = _sg_bwd(p1, wt['l1_sg_w'], sg_bias, sg_gain, d_cd, "l1_dsg")
    grads['l1_sg_b'] = db_sg[:, :N_HEADS].T
    dq, dk, dvv, dcol, drow = _fox_bwd(p1, c_col, c_row, lse, d_cd, d_out, "l1_dfox")
    d_f_logit, d_b_f = _fox_post(drow, dcol, f_logit, b_f, "l1_fox_post")
    grads['l1_fox_b_f'] = d_b_f[0, :N_HEADS]
    dp1 = jnp.concatenate([a.astype(BF16) for a in (du, dv, dq, dk, dvv)], axis=1)
    dw_main = _matmul(h2, dp1, "tn", BF16, "l1_dwin")
    dw_f = _matmul(h2, d_f_logit, "tn", BF16, "l1_dwin_f")
    dw_in1 = jnp.concatenate([dw_main, dw_f[:, :N_HEADS]], axis=1)
    dw_in1 = jnp.transpose(dw_in1.reshape(D_MODEL, 4, IN_CD // 4), (1, 0, 2))
    dh2 = _matmul(dp1, w_in1_main, "nt", F32, "l1_dh", after=[send_grads(['l1_w_out', 'l1_w_in'])(dw_out1, dw_in1)])
    dh2 = _matmul(d_f_logit, w_in1_f, "nt", F32, "l1_dh_f", res=dh2)
    dx2, grads['l1_mix_norm_g'] = _rmsnorm_bwd(x2, vec('l1_mix_norm_g'), dh2, dx3, "l1_dmix_norm")
    dx1, grads['l0_ffn_norm_g'], grads['l0_ffn_conv_w'] = _ffn_backward(
        x1, vec('l0_ffn_norm_g'), full['l0_ffn_up'], conv_full['l0_ffn_conv_w'], full['l0_ffn_down'], ffn0_saved, dx2,
        send_grads(['l0_ffn_up']), send_grads(['l0_ffn_down']), "l0_ffn")
    early_names = [n for n in SMALL if n not in LATE_SMALL]
    early = _small_reduce_start(_pack([grads[n] for n in early_names] + [loss_part]), "small_start", [dx1])
    dw_out0 = _matmul(ab0, dx1, "tn", BF16, "l0_dwout", after=[early[3]])
    d_ab = _matmul(dx1, full['l0_w_out'], "nt", F32, "l0_dab", after=[send_grads(['l0_w_out'])(dw_out0)])
    dq0, dk0, dv0 = _sb_bwd(p0, d_ab, sb_carries, "l0_dsb")
    dgb, dgc, dhin, grads['l0_sc_conv_w'] = _sc_bwd(p0, conv_full['l0_sc_conv_w'], d_ab, "l0_dsc")
    dp0 = jnp.concatenate([a.astype(BF16) for a in (dq0, dk0, dv0, dgb, dgc, dhin)], axis=1)
    dw_in0 = _matmul(h0, dp0, "tn", BF16, "l0_dwin")
    dh0 = _matmul(dp0, full['l0_w_in'], "nt", F32, "l0_dh", after=[send_grads(['l0_w_in'])(dw_in0)])
    dx0, grads['l0_mix_norm_g'] = _rmsnorm_bwd(x0, vec('l0_mix_norm_g'), dh0, dx1, "l0_dmix_norm")

    shard_grads, delta, new_m, new_v, swaps = {}, {}, {}, {}, {}

    def reduce_group(gi, after):
        names, send, recv, thru, lands = scatters[gi]
        kinds = [BIG[n] for n in names]
        g_thru, landed = _scatter_wait(thru, lands, kinds, send, recv, after, "scatter_wait_%d" % gi)
        halves = [_sum_partials(pos, g, ld, kd, "sum_" + n) for n, g, ld, kd in zip(names, g_thru, landed, kinds)]
        s_send, s_recv, h_thru, s_lands, tok = _swap_start(halves, "swap_start_%d" % gi)
        swaps[gi] = (names, s_send, s_recv, h_thru, s_lands)
        return tok

    def update_group(gi, after):
        names, s_send, s_recv, h_thru, s_lands = swaps[gi]
        mine, theirs = _swap_wait(h_thru, s_lands, s_send, s_recv, after, "swap_wait_%d" % gi)
        for n, gm, gs in zip(names, mine, theirs):
            shard_grads[n], delta[n], new_m[n], new_v[n] = _adamw_shard(pos, wt[n], gm, gs, mom[n], var[n], "adamw_" + n)
        return [delta[n] for n in names]

    after = reduce_group(2, reduce_group(1, reduce_group(0, dx0)))
    after = update_group(2, update_group(1, update_group(0, after)))
    after = reduce_group(5, reduce_group(4, reduce_group(3, after)))
    after = update_group(5, update_group(4, update_group(3, after)))
    after = reduce_group(6, after)
    def small_shapes(names):
        return [conv_full[n].shape if n in CONV else wt[n].shape for n in names]

    me = (2 * chip + lax.axis_index("c")).astype(jnp.int32).reshape(1)
    early_all = _small_reduce_finish(me, early[0], early[1], early[2], after, "small_early")
    early_sums = _unpack(early_all, small_shapes(early_names) + [loss_part.shape])
    loss = early_sums[-1][0, 0]
    late_all = _allreduce_small(_pack([grads[n] for n in LATE_SMALL]), [early_all])
    small_sums = dict(zip(early_names + LATE_SMALL, early_sums[:-1] + _unpack(late_all, small_shapes(LATE_SMALL))))
    for n in SMALL:
        g = small_sums[n]
        shard_grads[n] = lax.dynamic_slice_in_dim(g, chip * wt[n].shape[1], wt[n].shape[1], axis=1) if n in CONV else g
    update_group(6, late_all)
    def flat2d(a):
        return a.reshape(-1, a.shape[-1])

    small_out = _adamw_small(*[[flat2d(src[n]) for n in SMALL] for src in (wt, shard_grads, mom, var)], "adamw_small")
    for out, arrays in zip((delta, new_m, new_v), small_out):
        out.update((n, a.reshape(wt[n].shape)) for n, a in zip(SMALL, arrays))

    return (loss, dx0[None], *[shard_grads[n] for n in WEIGHTS], *[delta[n] for n in WEIGHTS],
            *[new_m[n] for n in WEIGHTS], *[new_v[n] for n in WEIGHTS])
```

```python
import jax
import jax.numpy as jnp
from jax import lax
from jax.experimental import pallas as pl
from jax.experimental.pallas import tpu as pltpu
from jax.experimental.pallas import tpu_sc as plsc

F32 = jnp.float32
BF16 = jnp.bfloat16

D_MODEL = 2048
HEAD = 128
N_HEADS = 8
HALF = N_HEADS * HEAD
D_FF = 5632
EPS = 1e-6
ATT_SCALE = HEAD ** -0.5
ATT_BLOCK = 512
NEG = -1e30

ADAM_LR = 0.001
ADAM_B1 = 0.9
ADAM_B2 = 0.999
ADAM_EPS = 1e-08
ADAM_WD = 0.01
ADAM_STEP = 10

VMEM_LIMIT_BYTES = 48 * 1024 * 1024
MM_VMEM_LIMIT_BYTES = 56 * 1024 * 1024
ATT_VMEM_LIMIT_BYTES = 48 * 1024 * 1024
MESH = pl.DeviceIdType.MESH
HBM_SPEC = pl.BlockSpec(memory_space=pltpu.HBM)
VMEM_SPEC = pl.BlockSpec(memory_space=pltpu.VMEM)


def _pcall(body, after=(), **kw):
    if not after:
        return pl.pallas_call(body, **kw)
    n_in, n_after, inner = len(kw["in_specs"]), len(after), body
    kw["in_specs"] = list(kw["in_specs"]) + [pl.BlockSpec(memory_space=pl.ANY)] * n_after

    def body(*refs):
        inner(*refs[:n_in], *refs[n_in + n_after:])

    call = pl.pallas_call(body, **kw)
    return lambda *args: call(*args, *after)


def _params(*semantics, vmem_limit=VMEM_LIMIT_BYTES):
    return pltpu.CompilerParams(dimension_semantics=semantics, vmem_limit_bytes=vmem_limit)


def _pick(n, cap):
    best = None
    for t in range(128, min(n, cap) + 1, 128):
        if n % t == 0:
            best = t
    return n if best is None else best


def _dot(a, b, dims):
    return lax.dot_general(a, b, (dims, ((), ())), preferred_element_type=F32)


def _dot_nn(a, b):
    return _dot(a, b, ((1,), (0,)))


def _dot_nt(a, b):
    return _dot(a, b, ((1,), (1,)))


def _dot_tn(a, b):
    return _dot(a, b, ((0,), (0,)))


def _split3(x):
    hi = x.astype(BF16)
    r1 = x - hi.astype(F32)
    mid = r1.astype(BF16)
    lo = (r1 - mid.astype(F32)).astype(BF16)
    return hi, mid, lo


def _log_sigmoid(z):
    return jnp.minimum(z, 0.0) - jnp.log1p(jnp.exp(-jnp.abs(z)))


_GELU_K = 0.7978845608028654


def _gelu(x):
    return 0.5 * x * (1.0 + jnp.tanh(_GELU_K * (x + 0.044715 * x * x * x)))


def _gelu_grad(x):
    t = jnp.tanh(_GELU_K * (x + 0.044715 * x * x * x))
    return 0.5 * (1.0 + t) + 0.5 * x * (1.0 - t * t) * _GELU_K * (1.0 + 3.0 * 0.044715 * x * x)


SUBLANES = 8


def _shift_down(x, k):
    rolled = pltpu.roll(x, k, axis=0)
    head = rolled[:SUBLANES]
    head = jnp.where(lax.broadcasted_iota(jnp.int32, head.shape, 0) >= k, head, 0.0)
    return jnp.concatenate([head, rolled[SUBLANES:]], axis=0)


def _shift_up(x, k):
    n = x.shape[0]
    rolled = pltpu.roll(x, n - k, axis=0)
    tail = rolled[n - SUBLANES:]
    tail = jnp.where(lax.broadcasted_iota(jnp.int32, tail.shape, 0) < SUBLANES - k, tail, 0.0)
    return jnp.concatenate([rolled[:n - SUBLANES], tail], axis=0)


def _conv3(s, w, shifted=None):
    s1, s2 = shifted if shifted else (_shift_down(s, 1), _shift_down(s, 2))
    return w[0:1, :] * s2 + w[1:2, :] * s1 + w[2:3, :] * s


def _conv3_transpose(d, w):
    return w[2:3, :] * d + w[1:2, :] * _shift_up(d, 1) + w[0:1, :] * _shift_up(d, 2)


def _conv3_wgrad(d, s, shifted, dw_ref):
    s1, s2 = shifted
    dw_ref[0:1, :] = jnp.sum(d * s2, axis=0, keepdims=True)
    dw_ref[1:2, :] = jnp.sum(d * s1, axis=0, keepdims=True)
    dw_ref[2:3, :] = jnp.sum(d * s, axis=0, keepdims=True)


MM_TILE_M, MM_TILE_N, MM_TILE_K = 1408, 512, 5632


def _matmul(a, b, mode, out_dtype, name, res=None, after=()):
    a_parts = a.shape[0] if a.ndim == 3 else 1
    b_parts = b.shape[0] if b.ndim == 3 else 1
    a_shape = (a.shape[1], a_parts * a.shape[2]) if a.ndim == 3 else a.shape
    b_shape = (b.shape[1], b_parts * b.shape[2]) if b.ndim == 3 else b.shape
    assert (a_parts == 1 or mode != "tn") and (b_parts == 1 or mode == "tn")
    if mode == "nn":
        (m, k), (k2, n) = a_shape, b_shape
    elif mode == "nt":
        (m, k), (n, k2) = a_shape, b_shape
    else:
        (k, m), (k2, n) = a_shape, b_shape
    assert k == k2, (a.shape, b.shape, mode)
    tm, tn, tk = _pick(m, MM_TILE_M), _pick(n // b_parts, MM_TILE_N), _pick(k // a_parts, MM_TILE_K)
    nk = k // tk
    if mode == "tn":
        a_spec = pl.BlockSpec((tk, tm), lambda i, j, kk: (kk, i))
    elif a_parts > 1:
        per = nk // a_parts
        a_spec = pl.BlockSpec((None, tm, tk), lambda i, j, kk: (kk // per, i, kk % per))
    else:
        a_spec = pl.BlockSpec((tm, tk), lambda i, j, kk: (i, kk))
    if mode == "nt":
        b_spec = pl.BlockSpec((tn, tk), lambda i, j, kk: (j, kk))
    elif b_parts > 1:
        per = n // b_parts // tn
        b_spec = pl.BlockSpec((None, tk, tn), lambda i, j, kk: (j // per, kk, j % per))
    else:
        b_spec = pl.BlockSpec((tk, tn), lambda i, j, kk: (kk, j))
    o_spec = pl.BlockSpec((tm, tn), lambda i, j, kk: (i, j))
    dims = {"nn": ((1,), (0,)), "nt": ((1,), (1,)), "tn": ((0,), (0,))}[mode]
    has_res = res is not None

    def body(*refs):
        a_ref, b_ref = refs[0], refs[1]
        r_ref = refs[2] if has_res else None
        o_ref = refs[3] if has_res else refs[2]
        part = _dot(a_ref[...].astype(BF16), b_ref[...].astype(BF16), dims)

        def finish(total):
            if has_res:
                total = total + r_ref[...]
            o_ref[...] = total.astype(out_dtype)

        if nk == 1:
            finish(part)
        else:
            acc_ref = refs[-1]
            kk = pl.program_id(2)

            @pl.when(kk == 0)
            def _():
                acc_ref[...] = part

            @pl.when(kk > 0)
            def _():
                acc_ref[...] += part

            @pl.when(kk == nk - 1)
            def _():
                finish(acc_ref[...])

    in_specs = [a_spec, b_spec] + ([o_spec] if has_res else [])
    args = (a, b) + ((res,) if has_res else ())
    return _pcall(
        body, after=after, name=name, grid=(m // tm, n // tn, nk),
        in_specs=in_specs, out_specs=o_spec,
        out_shape=jax.ShapeDtypeStruct((m, n), out_dtype),
        scratch_shapes=[pltpu.VMEM((tm, tn), F32)] if nk > 1 else [],
        compiler_params=_params("parallel", "parallel", "arbitrary", vmem_limit=MM_VMEM_LIMIT_BYTES),
    )(*args)


ROW_TILE = 256


def _rmsnorm_fwd(x, g, name, after=()):
    s, d = x.shape

    def body(x_ref, g_ref, o_ref):
        xf = x_ref[...]
        r = lax.rsqrt(jnp.mean(xf * xf, axis=-1, keepdims=True) + EPS)
        o_ref[...] = (xf * r * g_ref[...]).astype(BF16)

    row = pl.BlockSpec((ROW_TILE, d), lambda i: (i, 0))
    vec = pl.BlockSpec((1, d), lambda i: (0, 0))
    return _pcall(body, after=after, name=name, grid=(s // ROW_TILE,), in_specs=[row, vec], out_specs=row,
                  out_shape=jax.ShapeDtypeStruct((s, d), BF16), compiler_params=_params("parallel"))(x, g)


def _rmsnorm_bwd(x, g, dh, dres, name):
    s, d = x.shape

    def body(x_ref, g_ref, dh_ref, dres_ref, dx_ref, dg_ref):
        xf = x_ref[...]
        r = lax.rsqrt(jnp.mean(xf * xf, axis=-1, keepdims=True) + EPS)
        xhat = xf * r
        dh_v = dh_ref[...]
        dxh = dh_v * g_ref[...]
        proj = jnp.mean(dxh * xhat, axis=-1, keepdims=True)
        dx_ref[...] = dres_ref[...] + r * (dxh - xhat * proj)
        part = jnp.sum(dh_v * xhat, axis=0, keepdims=True)

        @pl.when(pl.program_id(0) == 0)
        def _():
            dg_ref[...] = part

        @pl.when(pl.program_id(0) > 0)
        def _():
            dg_ref[...] += part

    row = pl.BlockSpec((ROW_TILE, d), lambda i: (i, 0))
    vec = pl.BlockSpec((1, d), lambda i: (0, 0))
    return _pcall(body, name=name, grid=(s // ROW_TILE,), in_specs=[row, vec, row, row], out_specs=[row, vec],
                  out_shape=[jax.ShapeDtypeStruct((s, d), F32), jax.ShapeDtypeStruct((1, d), F32)],
                  compiler_params=_params("arbitrary"))(x, g, dh, dres)


def _loss_head(x, g, target, name):
    s, d = x.shape

    def body(x_ref, g_ref, t_ref, dx_ref, dg_ref, loss_ref):
        xf = x_ref[...]
        r = lax.rsqrt(jnp.mean(xf * xf, axis=-1, keepdims=True) + EPS)
        xhat = xf * r
        gv = g_ref[...]
        err = xhat * gv - t_ref[...]
        dy = err * (1.0 / d)
        dxh = dy * gv
        proj = jnp.mean(dxh * xhat, axis=-1, keepdims=True)
        dx_ref[...] = r * (dxh - xhat * proj)
        dg_part = jnp.sum(dy * xhat, axis=0, keepdims=True)
        row_loss = jnp.sum(err * err, axis=-1, keepdims=True) * (0.5 / d)
        loss_part = jnp.broadcast_to(jnp.sum(row_loss, axis=0, keepdims=True), (1, 128))

        @pl.when(pl.program_id(0) == 0)
        def _():
            dg_ref[...] = dg_part
            loss_ref[...] = loss_part

        @pl.when(pl.program_id(0) > 0)
        def _():
            dg_ref[...] += dg_part
            loss_ref[...] += loss_part

    row = pl.BlockSpec((ROW_TILE, d), lambda i: (i, 0))
    vec = pl.BlockSpec((1, d), lambda i: (0, 0))
    one = pl.BlockSpec((1, 128), lambda i: (0, 0))
    return _pcall(body, name=name, grid=(s // ROW_TILE,), in_specs=[row, vec, row], out_specs=[row, vec, one],
                  out_shape=[jax.ShapeDtypeStruct((s, d), F32), jax.ShapeDtypeStruct((1, d), F32),
                             jax.ShapeDtypeStruct((1, 128), F32)],
                  compiler_params=_params("arbitrary"))(x, g, target)


HEADS_PER_STEP = 2
GROUP_W = HEADS_PER_STEP * HEAD
N_GROUPS = N_HEADS // HEADS_PER_STEP


def _head_cols(h):
    return slice(h * HEAD, (h + 1) * HEAD)


def _head_specs(s, col0):
    t = ATT_BLOCK
    g0 = [c // HEADS_PER_STEP for c in col0]
    qspec = pl.BlockSpec((t, GROUP_W), lambda g, i: (i, g0[0] + g))
    kspec = pl.BlockSpec((s, GROUP_W), lambda g, i: (0, g0[1] + g))
    vspec = pl.BlockSpec((s, GROUP_W), lambda g, i: (0, g0[2] + g))
    return qspec, kspec, vspec


TRI = 256


def _order_matrix(later):
    r, c = lax.broadcasted_iota(jnp.int32, (TRI, TRI), 0), lax.broadcasted_iota(jnp.int32, (TRI, TRI), 1)
    return (r > c if later else r < c).astype(BF16)


def _exact_dot(x, m, later):
    parts = [x[:, c:c + TRI] for c in range(0, x.shape[1], TRI)]
    totals = [jnp.sum(p, axis=1, keepdims=True) for p in parts] if len(parts) > 1 else None
    out = []
    for j, p in enumerate(parts):
        hi = p.astype(BF16)
        lo = (p - hi.astype(F32)).astype(BF16)
        acc = _dot_nn(hi, m) + _dot_nn(lo, m)
        for other in (range(j + 1, len(parts)) if later else range(j)):
            acc = acc + totals[other]
        out.append(acc)
    return out[0] if len(out) == 1 else jnp.concatenate(out, axis=1)


def _sb_block(q, kblk, carry_l, u, diagonal):
    t = ATT_BLOCK
    z = _dot_nt(q, kblk) * ATT_SCALE
    sp = jnp.maximum(z, 0.0) + jnp.log(1.0 + jnp.exp(-jnp.abs(z)))
    if not diagonal:
        l = -sp
        return z, None, l, jnp.exp(z + l + _exact_dot(l, u, True) + carry_l)
    mask = lax.broadcasted_iota(jnp.int32, (t, t), 1) < lax.broadcasted_iota(jnp.int32, (t, t), 0)
    l = jnp.where(mask, -sp, 0.0)
    a = jnp.where(mask, jnp.exp(z - sp + _exact_dot(l, u, True) + carry_l), 0.0)
    return z, mask, l, a


def _sb_carry_spec(s):
    t = ATT_BLOCK
    return pl.BlockSpec((HEADS_PER_STEP, None, s // t, t, 1), lambda g, i: (g, i, 0, 0, 0))


def _sb_fwd(p, name, after=()):
    s = p.shape[0]
    t = ATT_BLOCK
    nb = s // t

    def body(q_ref, k_ref, v_ref, o_ref, cl_ref):
        i = pl.program_id(1)
        heads = range(HEADS_PER_STEP)
        q = [q_ref[:, _head_cols(h)].astype(BF16) for h in heads]
        u = _order_matrix(True)
        cl_ref[...] = jnp.zeros_like(cl_ref)

        def tile(kb, carry, diagonal):
            ks = pl.multiple_of(kb * t, t)
            out = []
            for h in heads:
                acc, carry_l = carry[h]
                kblk = k_ref[pl.ds(ks, t), _head_cols(h)].astype(BF16)
                vblk = v_ref[pl.ds(ks, t), _head_cols(h)].astype(BF16)
                cl_ref[h, kb] = carry_l
                _, _, l, a = _sb_block(q[h], kblk, carry_l, u, diagonal)
                out.append((acc + _dot_nn(a.astype(BF16), vblk), carry_l + jnp.sum(l, axis=1, keepdims=True)))
            return tuple(out)

        carry = tile(i, tuple((jnp.zeros((t, HEAD), F32), jnp.zeros((t, 1), F32)) for _ in heads), True)
        carry = lax.fori_loop(0, i, lambda n, c: tile(i - 1 - n, c, False), carry)
        for h in heads:
            o_ref[:, _head_cols(h)] = carry[h][0].astype(BF16)

    qspec, kspec, vspec = _head_specs(s, (0, N_HEADS, 2 * N_HEADS))
    ospec = pl.BlockSpec((t, GROUP_W), lambda g, i: (i, g))
    return _pcall(body, after=after, name=name, grid=(N_GROUPS, nb), in_specs=[qspec, kspec, vspec],
                  out_specs=[ospec, _sb_carry_spec(s)],
                  out_shape=[jax.ShapeDtypeStruct((s, 2 * HALF), BF16), jax.ShapeDtypeStruct((N_HEADS, nb, nb, t, 1), F32)],
                  compiler_params=_params("parallel", "parallel", vmem_limit=ATT_VMEM_LIMIT_BYTES))(p, p, p)


def _sb_bwd(p, d_ab, carries, name):
    s = p.shape[0]
    t = ATT_BLOCK

    def body(q_ref, k_ref, v_ref, do_ref, cl_ref, dq_ref, dk_ref, dv_ref):
        i = pl.program_id(1)

        @pl.when(i == 0)
        def _():
            dk_ref[...] = jnp.zeros_like(dk_ref)
            dv_ref[...] = jnp.zeros_like(dv_ref)

        heads = range(HEADS_PER_STEP)
        q = [q_ref[:, _head_cols(h)].astype(BF16) for h in heads]
        do = [do_ref[:, _head_cols(h)].astype(BF16) for h in heads]
        u = _order_matrix(True)
        lower = _order_matrix(False)

        def tile(kb, carry, diagonal):
            ks = pl.multiple_of(kb * t, t)
            out = []
            for h in heads:
                dq, carry_g = carry[h]
                kblk = k_ref[pl.ds(ks, t), _head_cols(h)].astype(BF16)
                vblk = v_ref[pl.ds(ks, t), _head_cols(h)].astype(BF16)
                z, mask, _, a = _sb_block(q[h], kblk, cl_ref[h, kb], u, diagonal)
                g = a * _dot_nt(do[h], vblk)
                earlier_g = _exact_dot(g, lower, False) + carry_g
                sig = jax.nn.sigmoid(z)
                dz = g * (1.0 - sig) - sig * earlier_g
                if diagonal:
                    dz = jnp.where(mask, dz, 0.0)
                dz = dz.astype(BF16)
                dv_ref[pl.ds(ks, t), _head_cols(h)] += _dot_tn(a.astype(BF16), do[h])
                dk_ref[pl.ds(ks, t), _head_cols(h)] += _dot_tn(dz, q[h]) * ATT_SCALE
                out.append((dq + _dot_nn(dz, kblk) * ATT_SCALE, carry_g + jnp.sum(g, axis=1, keepdims=True)))
            return tuple(out)

        init = tuple((jnp.zeros((t, HEAD), F32), jnp.zeros((t, 1), F32)) for _ in heads)
        carry = tile(i, lax.fori_loop(0, i, lambda kb, c: tile(kb, c, False), init), True)
        for h in heads:
            dq_ref[:, _head_cols(h)] = carry[h][0]

    qspec, kspec, vspec = _head_specs(s, (0, N_HEADS, 2 * N_HEADS))
    blk = pl.BlockSpec((t, GROUP_W), lambda g, i: (i, g))
    whole = pl.BlockSpec((s, GROUP_W), lambda g, i: (0, g))
    shape = jax.ShapeDtypeStruct((s, HALF), F32)
    return _pcall(body, name=name, grid=(N_GROUPS, s // t), in_specs=[qspec, kspec, vspec, blk, _sb_carry_spec(s)],
                  out_specs=[blk, whole, whole], out_shape=[shape, shape, shape],
                  compiler_params=_params("parallel", "arbitrary", vmem_limit=ATT_VMEM_LIMIT_BYTES))(p, p, p, d_ab, carries)


COL_TILE = 256


def _sc_fwd(p, w, mixed, name):
    s = p.shape[0]
    nb = HALF // COL_TILE

    def body(gb_ref, gc_ref, h_ref, w_ref, mixed_ref, o_ref):
        conv = _conv3(gc_ref[...] * h_ref[...], w_ref[...])
        o_ref[...] = (gb_ref[...] * conv).astype(BF16)

    def col(k):
        return pl.BlockSpec((s, COL_TILE), lambda j: (0, k * nb + j))

    wspec = pl.BlockSpec((3, COL_TILE), lambda j: (0, j))
    return _pcall(body, name=name, grid=(nb,), in_specs=[col(3), col(4), col(5), wspec, pl.BlockSpec(memory_space=pl.ANY)],
                  out_specs=col(1), out_shape=jax.ShapeDtypeStruct(mixed.shape, BF16), input_output_aliases={4: 0},
                  compiler_params=_params("parallel"))(p, p, p, w, mixed)


def _sc_bwd(p, w, d_ab, name):
    s = p.shape[0]
    nb = HALF // COL_TILE

    def body(gb_ref, gc_ref, h_ref, w_ref, d_ref, dgb_ref, dgc_ref, dh_ref, dw_ref):
        gc, hin, wv, d = gc_ref[...], h_ref[...], w_ref[...], d_ref[...]
        sig = gc * hin
        shifted = (_shift_down(sig, 1), _shift_down(sig, 2))
        dgb_ref[...] = d * _conv3(sig, wv, shifted)
        dconv = d * gb_ref[...]
        _conv3_wgrad(dconv, sig, shifted, dw_ref)
        dsig = _conv3_transpose(dconv, wv)
        dgc_ref[...] = dsig * hin
        dh_ref[...] = dsig * gc

    def col(k):
        return pl.BlockSpec((s, COL_TILE), lambda j: (0, k * nb + j))

    wspec = pl.BlockSpec((3, COL_TILE), lambda j: (0, j))
    act = jax.ShapeDtypeStruct((s, HALF), F32)
    return _pcall(body, name=name, grid=(nb,), in_specs=[col(3), col(4), col(5), wspec, col(1)],
                  out_specs=[col(0), col(0), col(0), wspec],
                  out_shape=[act, act, act, jax.ShapeDtypeStruct((3, HALF), F32)],
                  compiler_params=_params("parallel"))(p, p, p, w, d_ab)


def _ffn_act_fwd(u, w, name, after=()):
    s = u.shape[0]
    nb = D_FF // COL_TILE

    def body(ug_ref, uu_ref, wg_ref, wu_ref, o_ref):
        gate = _conv3(ug_ref[...], wg_ref[...])
        up = _conv3(uu_ref[...], wu_ref[...])
        o_ref[...] = (gate * jax.nn.sigmoid(gate) * up).astype(BF16)

    def col(k):
        return pl.BlockSpec((s, COL_TILE), lambda j: (0, k * nb + j))

    def wcol(k):
        return pl.BlockSpec((3, COL_TILE), lambda j: (0, k * nb + j))

    return _pcall(body, after=after, name=name, grid=(nb,), in_specs=[col(0), col(1), wcol(0), wcol(1)], out_specs=col(0),
                  out_shape=jax.ShapeDtypeStruct((s, D_FF), BF16),
                  compiler_params=_params("parallel"))(u, u, w, w)


def _ffn_act_bwd(u, w, d_f, name):
    s = u.shape[0]
    nb = D_FF // COL_TILE

    def body(ug_ref, uu_ref, wg_ref, wu_ref, d_ref, du_ref, dwg_ref, dwu_ref):
        ug, uu, wg, wu, d = ug_ref[...], uu_ref[...], wg_ref[...], wu_ref[...], d_ref[...]
        ug_shifted = (_shift_down(ug, 1), _shift_down(ug, 2))
        uu_shifted = (_shift_down(uu, 1), _shift_down(uu, 2))
        gate = _conv3(ug, wg, ug_shifted)
        up = _conv3(uu, wu, uu_shifted)
        sig = jax.nn.sigmoid(gate)
        d_up = d * gate * sig
        d_gate = d * up * sig * (1.0 + gate * (1.0 - sig))
        _conv3_wgrad(d_gate, ug, ug_shifted, dwg_ref)
        _conv3_wgrad(d_up, uu, uu_shifted, dwu_ref)
        du_ref[0] = _conv3_transpose(d_gate, wg).astype(BF16)
        du_ref[1] = _conv3_transpose(d_up, wu).astype(BF16)

    def col(k):
        return pl.BlockSpec((s, COL_TILE), lambda j: (0, k * nb + j))

    def wcol(k):
        return pl.BlockSpec((3, COL_TILE), lambda j: (0, k * nb + j))

    both = pl.BlockSpec((2, s, COL_TILE), lambda j: (0, 0, j))
    wsh = jax.ShapeDtypeStruct((3, D_FF), F32)
    return _pcall(body, name=name, grid=(nb,), in_specs=[col(0), col(1), wcol(0), wcol(1), col(0)],
                  out_specs=[both, wcol(0), wcol(0)], out_shape=[jax.ShapeDtypeStruct((2, s, D_FF), BF16), wsh, wsh],
                  compiler_params=_params("parallel"))(u, u, w, w, d_f)


def _sg_common(u, v, g, w_ref, bias, mixed_ref):
    rows = u.shape[0]
    gu = _gelu(u)
    gv = _gelu(v)
    xc = gv - jnp.mean(gv, axis=-1, keepdims=True)
    rstd = lax.rsqrt(jnp.mean(xc * xc, axis=-1, keepdims=True) + EPS)
    xhat = xc * rstd
    vn = xhat * g
    tril = lax.broadcasted_iota(jnp.int32, (HEAD, HEAD), 0) >= lax.broadcasted_iota(jnp.int32, (HEAD, HEAD), 1)
    wts = [jnp.where(tril, w_ref[grp], 0.0).astype(BF16) for grp in range(N_HEADS)]
    for n in range(rows // HEAD):
        for grp in range(N_HEADS):
            blk = vn[n * HEAD:(n + 1) * HEAD, grp * HEAD:(grp + 1) * HEAD].astype(BF16)
            mixed_ref[n * HEAD:(n + 1) * HEAD, grp * HEAD:(grp + 1) * HEAD] = _dot_nn(wts[grp], blk)
    mixed = mixed_ref[...] + jnp.concatenate([bias] * (rows // HEAD), axis=0)
    return gu, xhat, rstd, vn, mixed, wts, tril


def _sg_fwd(p, sg_w, bias, g, name):
    s = p.shape[0]

    def body(u_ref, v_ref, w_ref, b_ref, g_ref, o_ref, mixed_ref):
        gu, _, _, _, mixed, _, _ = _sg_common(u_ref[...], v_ref[...], g_ref[...], w_ref, b_ref[...], mixed_ref)
        o_ref[...] = (gu * mixed).astype(BF16)

    def half(k):
        return pl.BlockSpec((ROW_TILE, HALF), lambda i: (i, k))

    wspec = pl.BlockSpec((N_HEADS, HEAD, HEAD), lambda i: (0, 0, 0))
    bspec = pl.BlockSpec((HEAD, HALF), lambda i: (0, 0))
    gspec = pl.BlockSpec((1, HALF), lambda i: (0, 0))
    return _pcall(body, name=name, grid=(s // ROW_TILE,), in_specs=[half(0), half(1), wspec, bspec, gspec],
                  out_specs=half(0), out_shape=jax.ShapeDtypeStruct((s, 2 * HALF), BF16),
                  scratch_shapes=[pltpu.VMEM((ROW_TILE, HALF), F32)],
                  compiler_params=_params("parallel"))(p, p, sg_w, bias, g)


def _sg_bwd(p, sg_w, bias, g, d_cd, name):
    s = p.shape[0]
    nsteps = s // ROW_TILE

    def body(u_ref, v_ref, w_ref, b_ref, g_ref, d_ref, du_ref, dv_ref, dw_ref, db_ref, dg_ref,
             mixed_ref, dvn_ref, dbias_ref):
        i = pl.program_id(0)
        u, v, gain, d = u_ref[...], v_ref[...], g_ref[...], d_ref[...]
        gu, xhat, rstd, vn, mixed, wts, tril = _sg_common(u, v, gain, w_ref, b_ref[...], mixed_ref)

        @pl.when(i == 0)
        def _():
            dw_ref[...] = jnp.zeros_like(dw_ref)
            dg_ref[...] = jnp.zeros_like(dg_ref)
            dbias_ref[...] = jnp.zeros_like(dbias_ref)

        du_ref[...] = d * mixed * _gelu_grad(u)
        dm = d * gu
        for n in range(ROW_TILE // HEAD):
            rs = slice(n * HEAD, (n + 1) * HEAD)
            dbias_ref[...] += dm[rs, :]
            for grp in range(N_HEADS):
                cs = slice(grp * HEAD, (grp + 1) * HEAD)
                dm_blk = dm[rs, cs].astype(BF16)
                dw_ref[grp] += jnp.where(tril, _dot_nt(dm_blk, vn[rs, cs].astype(BF16)), 0.0)
                dvn_ref[rs, cs] = _dot_tn(wts[grp], dm_blk)
        dvn = dvn_ref[...]
        dg_ref[...] += jnp.sum(dvn * xhat, axis=0, keepdims=True)
        dxh = dvn * gain
        d_gv = rstd * (dxh - jnp.mean(dxh, axis=-1, keepdims=True) - xhat * jnp.mean(dxh * xhat, axis=-1, keepdims=True))
        dv_ref[...] = d_gv * _gelu_grad(v)

        @pl.when(i == nsteps - 1)
        def _():
            lane = lax.broadcasted_iota(jnp.int32, (HEAD, HEAD), 1)
            out = jnp.zeros((HEAD, HEAD), F32)
            for grp in range(N_HEADS):
                tot = jnp.sum(dbias_ref[:, grp * HEAD:(grp + 1) * HEAD], axis=1, keepdims=True)
                out = out + jnp.where(lane == grp, tot, 0.0)
            db_ref[...] = out

    def half(k):
        return pl.BlockSpec((ROW_TILE, HALF), lambda i: (i, k))

    wspec = pl.BlockSpec((N_HEADS, HEAD, HEAD), lambda i: (0, 0, 0))
    bspec = pl.BlockSpec((HEAD, HALF), lambda i: (0, 0))
    gspec = pl.BlockSpec((1, HALF), lambda i: (0, 0))
    dbspec = pl.BlockSpec((HEAD, HEAD), lambda i: (0, 0))
    act = jax.ShapeDtypeStruct((s, HALF), F32)
    return _pcall(body, name=name, grid=(nsteps,), in_specs=[half(0), half(1), wspec, bspec, gspec, half(0)],
                  out_specs=[half(0), half(0), wspec, dbspec, gspec],
                  out_shape=[act, act, jax.ShapeDtypeStruct((N_HEADS, HEAD, HEAD), F32),
                             jax.ShapeDtypeStruct((HEAD, HEAD), F32), jax.ShapeDtypeStruct((1, HALF), F32)],
                  scratch_shapes=[pltpu.VMEM((ROW_TILE, HALF), F32), pltpu.VMEM((ROW_TILE, HALF), F32),
                                  pltpu.VMEM((HEAD, HALF), F32)],
                  compiler_params=_params("arbitrary"))(p, p, sg_w, bias, g, d_cd)


def _fox_prep(f, b, name):
    s = f.shape[0]
    t = ATT_BLOCK

    def body(f_ref, b_ref, c_ref):
        tri = (lax.broadcasted_iota(jnp.int32, (t, t), 0) >= lax.broadcasted_iota(jnp.int32, (t, t), 1)).astype(BF16)
        carry = jnp.zeros((1, 128), F32)
        for n in range(s // t):
            lf = _log_sigmoid(f_ref[n * t:(n + 1) * t, :] + b_ref[...])
            hi, mid, lo = _split3(lf)
            c_ref[n * t:(n + 1) * t, :] = _dot_nn(tri, hi) + _dot_nn(tri, mid) + _dot_nn(tri, lo) + carry
            carry = carry + jnp.sum(lf, axis=0, keepdims=True)

    return _pcall(body, name=name, in_specs=[VMEM_SPEC, VMEM_SPEC], out_specs=VMEM_SPEC,
                  out_shape=jax.ShapeDtypeStruct((s, 128), F32))(f, b)


def _fox_post(drow, dcol, f, b, name):
    s = f.shape[0]
    t = ATT_BLOCK

    def body(drow_ref, dcol_ref, f_ref, b_ref, df_ref, db_ref):
        tri = (lax.broadcasted_iota(jnp.int32, (t, t), 1) >= lax.broadcasted_iota(jnp.int32, (t, t), 0)).astype(BF16)
        lane = lax.broadcasted_iota(jnp.int32, (t, 128), 1)
        carry = jnp.zeros((1, 128), F32)
        db = jnp.zeros((1, 128), F32)
        for n in reversed(range(s // t)):
            rs = slice(n * t, (n + 1) * t)
            dc = jnp.zeros((t, 128), F32)
            for h in range(N_HEADS):
                dc = jnp.where(lane == h, drow_ref[h, rs, :] - dcol_ref[rs, _head_cols(h)], dc)
            hi, mid, lo = _split3(dc)
            dlogf = _dot_nn(tri, hi) + _dot_nn(tri, mid) + _dot_nn(tri, lo) + carry
            carry = carry + jnp.sum(dc, axis=0, keepdims=True)
            df = dlogf * jax.nn.sigmoid(-(f_ref[rs, :] + b_ref[...]))
            df_ref[rs, :] = df
            db = db + jnp.sum(df, axis=0, keepdims=True)
        db_ref[...] = db

    return _pcall(body, name=name, in_specs=[VMEM_SPEC] * 4, out_specs=[VMEM_SPEC, VMEM_SPEC],
                  out_shape=[jax.ShapeDtypeStruct((s, 128), F32), jax.ShapeDtypeStruct((1, 128), F32)],
                  compiler_params=pltpu.CompilerParams(vmem_limit_bytes=VMEM_LIMIT_BYTES))(drow, dcol, f, b)


def _fox_specs(s):
    t = ATT_BLOCK
    ccol = pl.BlockSpec((HEADS_PER_STEP, t, 1), lambda g, i: (g, i, 0))
    crow = pl.BlockSpec((HEADS_PER_STEP, s // t, 1, t), lambda g, i: (g, 0, 0, 0))
    return ccol, crow


def _fox_fwd(p, c_col, c_row, mixed, name, after=()):
    s = p.shape[0]
    t = ATT_BLOCK

    def body(q_ref, k_ref, v_ref, cc_ref, cr_ref, mixed_in_ref, o_ref, lse_ref, mixed_ref):
        i = pl.program_id(1)
        heads = range(HEADS_PER_STEP)
        q = [q_ref[:, _head_cols(h)].astype(BF16) for h in heads]
        ct = [cc_ref[h] for h in heads]

        def tile(n, carry, diagonal):
            ks = pl.multiple_of(n * t, t)
            out = []
            for h in heads:
                acc, m, l = carry[h]
                kblk = k_ref[pl.ds(ks, t), _head_cols(h)].astype(BF16)
                vblk = v_ref[pl.ds(ks, t), _head_cols(h)].astype(BF16)
                logit = _dot_nt(q[h], kblk) * ATT_SCALE + ct[h] - cr_ref[h, n]
                if diagonal:
                    causal = lax.broadcasted_iota(jnp.int32, (t, t), 1) <= lax.broadcasted_iota(jnp.int32, (t, t), 0)
                    logit = jnp.where(causal, logit, NEG)
                m_new = jnp.maximum(m, jnp.max(logit, axis=1, keepdims=True))
                alpha = jnp.exp(m - m_new)
                pr = jnp.exp(logit - m_new)
                l = alpha * l + jnp.sum(pr, axis=1, keepdims=True)
                out.append((alpha * acc + _dot_nn(pr.astype(BF16), vblk), m_new, l))
            return tuple(out)

        init = tuple((jnp.zeros((t, HEAD), F32), jnp.full((t, 1), NEG, F32), jnp.zeros((t, 1), F32)) for _ in heads)
        carry = tile(i, lax.fori_loop(0, i, lambda n, c: tile(n, c, False), init), True)
        for h in heads:
            acc, m, l = carry[h]
            out = acc / l
            o_ref[:, _head_cols(h)] = out
            mixed_ref[:, _head_cols(h)] = out.astype(BF16)
            lse_ref[h] = m + jnp.log(l)

    qspec, kspec, vspec = _head_specs(s, (2 * N_HEADS, 3 * N_HEADS, 4 * N_HEADS))
    ccol, crow = _fox_specs(s)
    ospec = pl.BlockSpec((t, GROUP_W), lambda g, i: (i, g))
    mspec = pl.BlockSpec((t, GROUP_W), lambda g, i: (i, N_GROUPS + g))
    return _pcall(body, after=after, name=name, grid=(N_GROUPS, s // t),
                  in_specs=[qspec, kspec, vspec, ccol, crow, pl.BlockSpec(memory_space=pl.ANY)],
                  out_specs=[ospec, ccol, mspec],
                  out_shape=[jax.ShapeDtypeStruct((s, HALF), F32), jax.ShapeDtypeStruct((N_HEADS, s, 1), F32),
                             jax.ShapeDtypeStruct(mixed.shape, BF16)],
                  input_output_aliases={5: 2},
                  compiler_params=_params("parallel", "parallel", vmem_limit=ATT_VMEM_LIMIT_BYTES))(p, p, p, c_col, c_row, mixed)


def _fox_bwd(p, c_col, c_row, lse, d_cd, d_out, name):
    s = p.shape[0]
    t = ATT_BLOCK

    def body(q_ref, k_ref, v_ref, cc_ref, cr_ref, lse_ref, do_ref, o_ref, dq_ref, dk_ref, dv_ref, dcol_ref, drow_ref):
        i = pl.program_id(1)

        @pl.when(i == 0)
        def _():
            dk_ref[...] = jnp.zeros_like(dk_ref)
            dv_ref[...] = jnp.zeros_like(dv_ref)
            dcol_ref[...] = jnp.zeros_like(dcol_ref)

        heads = range(HEADS_PER_STEP)
        q = [q_ref[:, _head_cols(h)].astype(BF16) for h in heads]
        do = [do_ref[:, _head_cols(h)].astype(BF16) for h in heads]
        delta = [jnp.sum(do_ref[:, _head_cols(h)] * o_ref[:, _head_cols(h)], axis=1, keepdims=True) for h in heads]
        ct = [cc_ref[h] for h in heads]
        lse_v = [lse_ref[h] for h in heads]
        ones = jnp.ones((t, HEAD), BF16)

        def tile(n, carry, diagonal):
            ks = pl.multiple_of(n * t, t)
            out = []
            for h in heads:
                dq, drow = carry[h]
                kblk = k_ref[pl.ds(ks, t), _head_cols(h)].astype(BF16)
                vblk = v_ref[pl.ds(ks, t), _head_cols(h)].astype(BF16)
                logit = _dot_nt(q[h], kblk) * ATT_SCALE + ct[h] - cr_ref[h, n]
                pr = jnp.exp(logit - lse_v[h])
                if diagonal:
                    causal = lax.broadcasted_iota(jnp.int32, (t, t), 1) <= lax.broadcasted_iota(jnp.int32, (t, t), 0)
                    pr = jnp.where(causal, pr, 0.0)
                ds = pr * (_dot_nt(do[h], vblk) - delta[h])
                dsb = ds.astype(BF16)
                dv_ref[pl.ds(ks, t), _head_cols(h)] += _dot_tn(pr.astype(BF16), do[h])
                dk_ref[pl.ds(ks, t), _head_cols(h)] += _dot_tn(dsb, q[h]) * ATT_SCALE
                dcol_ref[pl.ds(ks, t), _head_cols(h)] += _dot_tn(dsb, ones)
                out.append((dq + _dot_nn(dsb, kblk) * ATT_SCALE,
                            drow + jnp.sum(dsb.astype(F32), axis=1, keepdims=True)))
            return tuple(out)

        init = tuple((jnp.zeros((t, HEAD), F32), jnp.zeros((t, 1), F32)) for _ in heads)
        carry = tile(i, lax.fori_loop(0, i, lambda n, c: tile(n, c, False), init), True)
        for h in heads:
            dq_ref[:, _head_cols(h)] = carry[h][0]
            drow_ref[h] = carry[h][1]

    qspec, kspec, vspec = _head_specs(s, (2 * N_HEADS, 3 * N_HEADS, 4 * N_HEADS))
    ccol, crow = _fox_specs(s)
    dospec = pl.BlockSpec((t, GROUP_W), lambda g, i: (i, N_GROUPS + g))
    blk = pl.BlockSpec((t, GROUP_W), lambda g, i: (i, g))
    whole = pl.BlockSpec((s, GROUP_W), lambda g, i: (0, g))
    shape = jax.ShapeDtypeStruct((s, HALF), F32)
    return _pcall(body, name=name, grid=(N_GROUPS, s // t),
                  in_specs=[qspec, kspec, vspec, ccol, crow, ccol, dospec, blk],
                  out_specs=[blk, whole, whole, whole, ccol],
                  out_shape=[shape, shape, shape, shape, jax.ShapeDtypeStruct((N_HEADS, s, 1), F32)],
                  compiler_params=_params("parallel", "arbitrary", vmem_limit=ATT_VMEM_LIMIT_BYTES))(p, p, p, c_col, c_row, lse, d_cd, d_out)


def _row_tile(rows, cap):
    for t in (256, 128, 64, 32, 16, 8):
        if t <= cap and rows % t == 0:
            return t
    return rows


def _adamw_small(ws, gs, ms, vs, name):
    n = len(ws)
    c1 = 1.0 / (1.0 - ADAM_B1 ** ADAM_STEP)
    c2 = 1.0 / (1.0 - ADAM_B2 ** ADAM_STEP)

    def body(*refs):
        for k in range(n):
            w_ref, g_ref, m_ref, v_ref = (refs[j * n + k] for j in range(4))
            d_ref, nm_ref, nv_ref = (refs[(4 + j) * n + k] for j in range(3))
            gv = g_ref[...]
            nm = ADAM_B1 * m_ref[...] + (1.0 - ADAM_B1) * gv
            nv = ADAM_B2 * v_ref[...] + (1.0 - ADAM_B2) * (gv * gv)
            nm_ref[...] = nm
            nv_ref[...] = nv
            d_ref[...] = -ADAM_LR * ((nm * c1) / (jnp.sqrt(nv * c2) + ADAM_EPS) + ADAM_WD * w_ref[...])

    shapes = [jax.ShapeDtypeStruct(w.shape, F32) for w in ws] * 3
    outs = _pcall(body, name=name, in_specs=[VMEM_SPEC] * (4 * n), out_specs=[VMEM_SPEC] * (3 * n), out_shape=shapes,
                  compiler_params=pltpu.CompilerParams(vmem_limit_bytes=VMEM_LIMIT_BYTES))(*ws, *gs, *ms, *vs)
    return outs[:n], outs[n:2 * n], outs[2 * n:]


def _half_shape(whole_shape, kind):
    if kind == "col":
        return (whole_shape[0] // 2, whole_shape[1] // 4)
    if kind == "row":
        return (whole_shape[0] // 8, whole_shape[1])
    return (whole_shape[1] // 2, whole_shape[2])


def _own_half_spec(whole_shape, kind, tr):
    hr, hc = _half_shape(whole_shape, kind)
    nb = hr // tr
    if kind == "col":
        return pl.BlockSpec((tr, hc), lambda i, pos: (pos[1] * nb + i, pos[0]))
    if kind == "row":
        return pl.BlockSpec((tr, hc), lambda i, pos: ((2 * pos[0] + pos[1]) * nb + i, 0))
    return pl.BlockSpec((None, tr, hc), lambda i, pos: (pos[0], pos[1] * nb + i, 0))


def _sum_partials(pos, grad, landed, kind, name):
    hr, hc = _half_shape(grad.shape, kind)
    tr = _row_tile(hr, 128)

    def body(pos_ref, g_ref, p_ref, o_ref):
        acc = g_ref[...].astype(F32)
        for k in range(N_DEV - 1):
            acc = acc + p_ref[k].astype(F32)
        o_ref[...] = acc

    grid_spec = pltpu.PrefetchScalarGridSpec(
        num_scalar_prefetch=1, grid=(hr // tr,),
        in_specs=[_own_half_spec(grad.shape, kind, tr), pl.BlockSpec((N_DEV - 1, tr, hc), lambda i, pos: (0, i, 0))],
        out_specs=pl.BlockSpec((tr, hc), lambda i, pos: (i, 0)))
    return _pcall(body, name=name, grid_spec=grid_spec, out_shape=jax.ShapeDtypeStruct((hr, hc), F32),
                  compiler_params=_params("parallel"))(pos, grad, landed)


def _adamw_shard(pos, w, g_mine, g_sibling, m, v, name):
    hr, hc = g_mine.shape
    tr = _row_tile(hr, 128)
    nb = hr // tr
    c1 = 1.0 / (1.0 - ADAM_B1 ** ADAM_STEP)
    c2 = 1.0 / (1.0 - ADAM_B2 ** ADAM_STEP)

    def body(pos_ref, w_ref, gm_ref, gs_ref, m_ref, v_ref, g_ref, d_ref, nm_ref, nv_ref):
        mine = (pl.program_id(0) // nb) == pos_ref[1]
        gv = jnp.where(mine, gm_ref[...], gs_ref[...])
        nm = ADAM_B1 * m_ref[...] + (1.0 - ADAM_B1) * gv
        nv = ADAM_B2 * v_ref[...] + (1.0 - ADAM_B2) * (gv * gv)
        g_ref[...] = gv
        nm_ref[...] = nm
        nv_ref[...] = nv
        d_ref[...] = -ADAM_LR * ((nm * c1) / (jnp.sqrt(nv * c2) + ADAM_EPS) + ADAM_WD * w_ref[...])

    full = pl.BlockSpec((tr, hc), lambda i, pos: (i, 0))
    mine_spec = pl.BlockSpec((tr, hc), lambda i, pos: (jnp.clip(i - pos[1] * nb, 0, nb - 1), 0))
    sib_spec = pl.BlockSpec((tr, hc), lambda i, pos: (jnp.clip(i - (1 - pos[1]) * nb, 0, nb - 1), 0))
    grid_spec = pltpu.PrefetchScalarGridSpec(
        num_scalar_prefetch=1, grid=(2 * nb,), in_specs=[full, mine_spec, sib_spec, full, full], out_specs=[full] * 4)
    shape = jax.ShapeDtypeStruct((2 * hr, hc), F32)
    return _pcall(body, name=name, grid_spec=grid_spec, out_shape=[shape] * 4,
                  compiler_params=_params("parallel"))(pos, w, g_mine, g_sibling, m, v)


SC_TILES = 32
SC_LANES = 16
SC_CHUNK_ROWS = 4


def _adamw_sparsecore(w, g_mine, g_sibling, m, v, name):
    hr, cols = g_mine.shape
    rows = 2 * hr
    per_tile = rows // SC_TILES
    assert rows % SC_TILES == 0 and per_tile % SC_CHUNK_ROWS == 0 and hr % SC_CHUNK_ROWS == 0 and cols % SC_LANES == 0
    c1 = 1.0 / (1.0 - ADAM_B1 ** ADAM_STEP)
    c2 = 1.0 / (1.0 - ADAM_B2 ** ADAM_STEP)

    def body(w_hbm, gm_hbm, gs_hbm, m_hbm, v_hbm, g_out, d_out, nm_out, nv_out, w_buf, g_buf, m_buf, v_buf):
        tile = lax.axis_index("sc_tile") * 2 + lax.axis_index("sc_core")
        core = lax.axis_index("c")

        @pl.loop(0, per_tile, step=SC_CHUNK_ROWS)
        def _(r):
            row0 = tile * per_tile + r
            band = pl.ds(row0, SC_CHUNK_ROWS)
            pltpu.sync_copy(w_hbm.at[band, :], w_buf)
            pltpu.sync_copy(m_hbm.at[band, :], m_buf)
            pltpu.sync_copy(v_hbm.at[band, :], v_buf)
            half = row0 // hr
            local = pl.ds(row0 - half * hr, SC_CHUNK_ROWS)

            @pl.when(half == core)
            def _():
                pltpu.sync_copy(gm_hbm.at[local, :], g_buf)

            @pl.when(half != core)
            def _():
                pltpu.sync_copy(gs_hbm.at[local, :], g_buf)

            @pl.loop(0, SC_CHUNK_ROWS)
            def _(i):
                @pl.loop(0, cols, step=SC_LANES)
                def _(j):
                    at = (i, pl.ds(j, SC_LANES))
                    gv = g_buf[at]
                    nm = ADAM_B1 * m_buf[at] + (1.0 - ADAM_B1) * gv
                    nv = ADAM_B2 * v_buf[at] + (1.0 - ADAM_B2) * (gv * gv)
                    m_buf[at] = nm
                    v_buf[at] = nv
                    w_buf[at] = -ADAM_LR * ((nm * c1) / (jnp.sqrt(nv * c2) + ADAM_EPS) + ADAM_WD * w_buf[at])

            pltpu.sync_copy(g_buf, g_out.at[band, :])
            pltpu.sync_copy(w_buf, d_out.at[band, :])
            pltpu.sync_copy(m_buf, nm_out.at[band, :])
            pltpu.sync_copy(v_buf, nv_out.at[band, :])

    shape = jax.ShapeDtypeStruct((rows, cols), F32)
    return pl.kernel(body, name=name, out_type=[shape] * 4,
                     mesh=plsc.VectorSubcoreMesh(core_axis_name="sc_core", subcore_axis_name="sc_tile"),
                     scratch_types=[pltpu.VMEM((SC_CHUNK_ROWS, cols), F32)] * 4)(w, g_mine, g_sibling, m, v)


def _place_shard(pos, shard, kind, name, after=()):
    rows, cols = shard.shape
    tr = _row_tile(rows, 256)
    nb = rows // tr
    if kind == "col":
        out_spec = pl.BlockSpec((tr, cols), lambda i, pos: (i, pos[0]))
    elif kind == "row":
        out_spec = pl.BlockSpec((tr, cols), lambda i, pos: (pos[0] * nb + i, 0))
    else:
        out_spec = pl.BlockSpec((None, tr, cols), lambda i, pos: (pos[0], i, 0))

    def body(pos_ref, s_ref, *rest):
        rest[-1][...] = s_ref[...].astype(BF16)

    grid_spec = pltpu.PrefetchScalarGridSpec(
        num_scalar_prefetch=1, grid=(nb,),
        in_specs=[pl.BlockSpec((tr, cols), lambda i, pos: (i, 0))] + [pl.BlockSpec(memory_space=pl.ANY)] * len(after),
        out_specs=out_spec)
    return _pcall(body, name=name, grid_spec=grid_spec,
                  out_shape=jax.ShapeDtypeStruct(_whole_shape(shard.shape, kind), BF16),
                  compiler_params=_params("parallel"))(pos, shard, *after)


N_DEV = 8
RELATIONS = [(r >> 2 & 1, r >> 1 & 1, r & 1) for r in range(1, N_DEV)]


def _position():
    return lax.axis_index("x"), lax.axis_index("y"), lax.axis_index("c")


def _related(pos, rel):
    return tuple(1 - p if f else p for p, f in zip(pos, rel))


def _index(pos):
    return 4 * pos[0] + 2 * pos[1] + pos[2]


def _window(ref, kind, pos):
    px, py, pc = pos
    j = 2 * px + py
    if kind == "col":
        r, c = ref.shape
        return ref.at[pl.ds(pc * (r // 2), r // 2), pl.ds(pl.multiple_of(j * (c // 4), 128), c // 4)]
    if kind == "row":
        rj = ref.shape[0] // 4
        return ref.at[pl.ds(j * rj + pc * (rj // 2), rj // 2), :]
    r = ref.shape[1]
    return ref.at[j, pl.ds(pc * (r // 2), r // 2), :]


def _whole_shape(shard_shape, kind):
    r, c = shard_shape
    return {"col": (r, 4 * c), "row": (4 * r, c), "maj": (4, r, c)}[kind]


SEM_SPEC = pl.BlockSpec(memory_space=pltpu.SEMAPHORE)
ANY_SPEC = pl.BlockSpec(memory_space=pl.ANY)
DATAFLOW = pltpu.SideEffectType.DATAFLOW_SIDE_EFFECTING
TOKEN = jax.ShapeDtypeStruct((8, 128), F32)


def _hbm(a):
    return pltpu.with_memory_space_constraint(a, pltpu.HBM)


def _chips(x, y):
    return [(1 - x, y), (x, 1 - y), (1 - x, 1 - y)]


def _split_start(body, name, buffers, n_sems, after=()):
    n = len(buffers)

    def wrapped(*refs):
        body(refs[:n], refs[n], refs[n + 1])
        refs[-1][...] = jnp.zeros_like(refs[-1])

    outs = _pcall(
        wrapped, after=after, name=name, in_specs=[HBM_SPEC] * n,
        out_specs=[SEM_SPEC, SEM_SPEC] + [HBM_SPEC] * n + [VMEM_SPEC],
        out_shape=[pltpu.SemaphoreType.DMA(n_sems), pltpu.SemaphoreType.DMA(n_sems)]
        + [pltpu.HBM(b.shape, b.dtype) for b in buffers] + [TOKEN],
        input_output_aliases={i: 2 + i for i in range(n)},
        compiler_params=pltpu.CompilerParams(has_side_effects=DATAFLOW))(*[_hbm(b) for b in buffers])
    return outs[0], outs[1], list(outs[2:2 + n]), outs[2 + n]


def _split_wait(body, name, buffers, send_sems, recv_sems, after):
    n = len(buffers)
    after = list(after) if isinstance(after, (list, tuple)) else [after]

    def wrapped(*refs):
        body(refs[:n], refs[n], refs[n + 1])

    outs = _pcall(
        wrapped, name=name, in_specs=[HBM_SPEC] * n + [SEM_SPEC, SEM_SPEC] + [ANY_SPEC] * len(after),
        out_specs=[HBM_SPEC] * n, out_shape=[pltpu.HBM(b.shape, b.dtype) for b in buffers],
        input_output_aliases={i: i for i in range(n)},
        compiler_params=pltpu.CompilerParams(has_side_effects=DATAFLOW))(*buffers, send_sems, recv_sems, *after)
    return list(outs)


def _gather_start(wholes, kinds, name, after=()):
    def body(w_refs, send_sems, recv_sems):
        x, y, c = _position()
        for w, ref in enumerate(w_refs):
            mine = _window(ref, kinds[w], (x, y, c))
            for k, chip in enumerate(_chips(x, y)):
                pltpu.make_async_remote_copy(src_ref=mine, dst_ref=mine, send_sem=send_sems.at[3 * w + k],
                                             recv_sem=recv_sems.at[3 * w + k], device_id=(*chip, c),
                                             device_id_type=MESH).start()

    return _split_start(body, name, wholes, (3 * len(wholes),), after)


def _gather_forward(wholes, kinds, send1, recv1, after, name):
    n = len(wholes)

    def wrapped(*refs):
        w_refs, s1, r1, s2, r2 = refs[:n], refs[n], refs[n + 1], refs[n + 3], refs[n + 4]
        x, y, c = _position()
        for k, chip in enumerate(_chips(x, y)):
            for w, ref in enumerate(w_refs):
                theirs = _window(ref, kinds[w], (*chip, c))
                pltpu.make_async_remote_copy(src_ref=theirs, dst_ref=theirs, send_sem=s1.at[3 * w + k],
                                             recv_sem=r1.at[3 * w + k], device_id=(*chip, c),
                                             device_id_type=MESH).wait_recv()
                pltpu.make_async_remote_copy(src_ref=theirs, dst_ref=theirs, send_sem=s2.at[3 * w + k],
                                             recv_sem=r2.at[3 * w + k], device_id=(x, y, 1 - c),
                                             device_id_type=MESH).start()
        for w, ref in enumerate(w_refs):
            mine = _window(ref, kinds[w], (x, y, c))
            for k, chip in enumerate(_chips(x, y)):
                pltpu.make_async_remote_copy(src_ref=mine, dst_ref=mine, send_sem=s1.at[3 * w + k],
                                             recv_sem=r1.at[3 * w + k], device_id=(*chip, c),
                                             device_id_type=MESH).wait_send()
        refs[-1][...] = jnp.zeros_like(refs[-1])

    outs = _pcall(
        wrapped, name=name, in_specs=[HBM_SPEC] * n + [SEM_SPEC, SEM_SPEC, ANY_SPEC],
        out_specs=[SEM_SPEC, SEM_SPEC] + [HBM_SPEC] * n + [VMEM_SPEC],
        out_shape=[pltpu.SemaphoreType.DMA((3 * n,)), pltpu.SemaphoreType.DMA((3 * n,))]
        + [pltpu.HBM(b.shape, b.dtype) for b in wholes] + [TOKEN],
        input_output_aliases={i: 2 + i for i in range(n)},
        compiler_params=pltpu.CompilerParams(has_side_effects=DATAFLOW))(*wholes, send1, recv1, after)
    return outs[0], outs[1], list(outs[2:2 + n]), outs[2 + n]


def _gather_finish(wholes, kinds, send2, recv2, after, name):
    def body(w_refs, s2, r2):
        x, y, c = _position()
        for k, chip in enumerate(_chips(x, y)):
            for w, ref in enumerate(w_refs):
                sent = _window(ref, kinds[w], (*chip, c))
                got = _window(ref, kinds[w], (*chip, 1 - c))
                pltpu.make_async_remote_copy(src_ref=sent, dst_ref=got, send_sem=s2.at[3 * w + k],
                                             recv_sem=r2.at[3 * w + k], device_id=(x, y, 1 - c),
                                             device_id_type=MESH).wait()

    return _split_wait(body, name, wholes, send2, recv2, after)


def _gather_small_start(small, after):
    def body(refs, send_sems, recv_sems):
        s_ref, land_ref = refs
        x, y, c = _position()
        for k, chip in enumerate(_chips(x, y)):
            pltpu.make_async_remote_copy(src_ref=s_ref, dst_ref=land_ref.at[2 * x + y], send_sem=send_sems.at[k],
                                         recv_sem=recv_sems.at[k], device_id=(*chip, c), device_id_type=MESH).start()

    return _split_start(body, "gather_small_start", [small, lax.empty((4,) + small.shape, small.dtype)], (3,), after)


def _gather_small_finish(send, recv, thru, after):
    def body(refs, send_sems, recv_sems):
        s_ref, land_ref = refs
        x, y, c = _position()
        for k, chip in enumerate(_chips(x, y)):
            pltpu.make_async_remote_copy(src_ref=s_ref, dst_ref=land_ref.at[2 * chip[0] + chip[1]],
                                         send_sem=send_sems.at[k], recv_sem=recv_sems.at[k], device_id=(*chip, c),
                                         device_id_type=MESH).wait()

    return _split_wait(body, "gather_small_finish", thru, send, recv, after)[1]


def _scatter_copies(g_refs, land_refs, kinds, send_sems, recv_sems):
    me = _position()
    copies = []
    for k, rel in enumerate(RELATIONS):
        peer = _related(me, rel)
        for w, (g_ref, land_ref) in enumerate(zip(g_refs, land_refs)):
            copies.append(pltpu.make_async_remote_copy(
                src_ref=_window(g_ref, kinds[w], peer), dst_ref=land_ref.at[k],
                send_sem=send_sems.at[7 * w + k], recv_sem=recv_sems.at[7 * w + k], device_id=peer,
                device_id_type=MESH))
    return copies


def _scatter_start(grads, kinds, name):
    n = len(grads)
    lands = [lax.empty((N_DEV - 1,) + _half_shape(g.shape, kd), g.dtype) for g, kd in zip(grads, kinds)]

    def body(refs, send_sems, recv_sems):
        for cp in _scatter_copies(refs[:n], refs[n:], kinds, send_sems, recv_sems):
            cp.start()

    send, recv, thru, token = _split_start(body, name, list(grads) + lands, ((N_DEV - 1) * n,))
    return send, recv, thru[:n], thru[n:], token


def _scatter_wait(grads, lands, kinds, send, recv, after, name):
    n = len(grads)

    def body(refs, send_sems, recv_sems):
        for cp in _scatter_copies(refs[:n], refs[n:], kinds, send_sems, recv_sems):
            cp.wait()

    out = _split_wait(body, name, list(grads) + list(lands), send, recv, after)
    return out[:n], out[n:]


def _swap_start(halves, name):
    n = len(halves)
    lands = [lax.empty(h.shape, h.dtype) for h in halves]

    def body(refs, send_sems, recv_sems):
        x, y, c = _position()
        for w in range(n):
            pltpu.make_async_remote_copy(src_ref=refs[w], dst_ref=refs[n + w], send_sem=send_sems.at[w],
                                         recv_sem=recv_sems.at[w], device_id=(x, y, 1 - c), device_id_type=MESH).start()

    send, recv, thru, token = _split_start(body, name, list(halves) + lands, (n,))
    return send, recv, thru[:n], thru[n:], token


def _swap_wait(halves, lands, send, recv, after, name):
    n = len(halves)

    def body(refs, send_sems, recv_sems):
        x, y, c = _position()
        for w in range(n):
            pltpu.make_async_remote_copy(src_ref=refs[w], dst_ref=refs[n + w], send_sem=send_sems.at[w],
                                         recv_sem=recv_sems.at[w], device_id=(x, y, 1 - c), device_id_type=MESH).wait()

    out = _split_wait(body, name, list(halves) + list(lands), send, recv, after)
    return out[:n], out[n:]


def _allreduce_small(v, after=()):
    rows = v.shape[0]

    def body(v_ref, o_ref, recv_ref, send_sems, recv_sems):
        me = _position()
        recv_ref[_index(me)] = v_ref[...]
        sends = []
        for k, rel in enumerate(RELATIONS):
            peer = _related(me, rel)
            cp = pltpu.make_async_remote_copy(
                src_ref=v_ref, dst_ref=recv_ref.at[_index(me)],
                send_sem=send_sems.at[k], recv_sem=recv_sems.at[k], device_id=peer, device_id_type=MESH)
            cp.start()
            sends.append(cp)
        for k, rel in enumerate(RELATIONS):
            peer = _related(me, rel)
            pltpu.make_async_remote_copy(
                src_ref=v_ref, dst_ref=recv_ref.at[_index(peer)],
                send_sem=send_sems.at[k], recv_sem=recv_sems.at[k], device_id=peer, device_id_type=MESH).wait_recv()
        for cp in sends:
            cp.wait_send()
        acc = recv_ref[0]
        for k in range(1, N_DEV):
            acc = acc + recv_ref[k]
        o_ref[...] = acc

    return _pcall(body, after=after, name="allreduce_small", in_specs=[VMEM_SPEC], out_specs=VMEM_SPEC,
                  out_shape=jax.ShapeDtypeStruct((rows, 128), F32),
                  scratch_shapes=[pltpu.VMEM((N_DEV, rows, 128), F32), pltpu.SemaphoreType.DMA((7,)),
                                  pltpu.SemaphoreType.DMA((7,))],
                  compiler_params=pltpu.CompilerParams(vmem_limit_bytes=VMEM_LIMIT_BYTES))(v)


def _small_copies(refs, send_sems, recv_sems):
    v_ref, land_ref = refs
    me = _position()
    return [pltpu.make_async_remote_copy(src_ref=v_ref, dst_ref=land_ref.at[_index(me)], send_sem=send_sems.at[k],
                                         recv_sem=recv_sems.at[k], device_id=_related(me, rel), device_id_type=MESH)
            for k, rel in enumerate(RELATIONS)]


def _small_wait_copies(refs, send_sems, recv_sems):
    v_ref, land_ref = refs
    me = _position()
    return [pltpu.make_async_remote_copy(src_ref=v_ref, dst_ref=land_ref.at[_index(_related(me, rel))],
                                         send_sem=send_sems.at[k], recv_sem=recv_sems.at[k],
                                         device_id=_related(me, rel), device_id_type=MESH)
            for k, rel in enumerate(RELATIONS)]


def _small_reduce_start(v, name, after):
    def body(refs, send_sems, recv_sems):
        for cp in _small_copies(refs, send_sems, recv_sems):
            cp.start()

    send, recv, thru, token = _split_start(body, name, [v, lax.empty((N_DEV,) + v.shape, v.dtype)], (N_DEV - 1,), after)
    return send, recv, thru, token


def _small_reduce_finish(me, send, recv, thru, after, name):
    def body(refs, send_sems, recv_sems):
        for cp in _small_wait_copies(refs, send_sems, recv_sems):
            cp.wait()

    v, landed = _split_wait(body, name + "_wait", thru, send, recv, after)

    def add(me_ref, v_ref, land_ref, o_ref):
        acc = jnp.where(me_ref[0] == 0, v_ref[...], land_ref[0])
        for d in range(1, N_DEV):
            acc = acc + jnp.where(me_ref[0] == d, v_ref[...], land_ref[d])
        o_ref[...] = acc

    return _pcall(add, name=name + "_sum",
                  in_specs=[pl.BlockSpec(memory_space=pltpu.SMEM), VMEM_SPEC, VMEM_SPEC], out_specs=VMEM_SPEC,
                  out_shape=jax.ShapeDtypeStruct(v.shape, F32),
                  compiler_params=pltpu.CompilerParams(vmem_limit_bytes=VMEM_LIMIT_BYTES))(me, v, landed)


def _pack(arrays):
    flat = []
    for a in arrays:
        a = a.reshape(-1)
        flat.append(jnp.pad(a, (0, -a.shape[0] % 128)))
    flat = jnp.concatenate(flat)
    flat = jnp.pad(flat, (0, -flat.shape[0] % 1024))
    return flat.reshape(-1, 128)


def _unpack(packed, shapes):
    flat = packed.reshape(-1)
    out, at = [], 0
    for shp in shapes:
        size = 1
        for d in shp:
            size *= d
        out.append(flat[at:at + size].reshape(shp))
        at += size + (-size % 128)
    return out


WEIGHTS = ['l0_mix_norm_g', 'l0_w_in', 'l0_sc_conv_w', 'l0_w_out', 'l0_ffn_norm_g', 'l0_ffn_up', 'l0_ffn_conv_w',
           'l0_ffn_down', 'l1_mix_norm_g', 'l1_w_in', 'l1_fox_b_f', 'l1_sg_w', 'l1_sg_b', 'l1_sg_norm_g', 'l1_w_out',
           'l1_ffn_norm_g', 'l1_ffn_up', 'l1_ffn_conv_w', 'l1_ffn_down', 'final_norm_g']
BIG = {'l0_w_in': 'col', 'l0_w_out': 'row', 'l0_ffn_up': 'col', 'l0_ffn_down': 'row',
       'l1_w_in': 'maj', 'l1_w_out': 'row', 'l1_ffn_up': 'col', 'l1_ffn_down': 'row'}
GATHER_GROUPS = [['l0_w_in'], ['l0_w_out'], ['l0_ffn_up'], ['l0_ffn_down'], ['l1_w_in'], ['l1_w_out'],
                 ['l1_ffn_up'], ['l1_ffn_down']]
CONV = ['l0_sc_conv_w', 'l0_ffn_conv_w', 'l1_ffn_conv_w']
SMALL = [n for n in WEIGHTS if n not in BIG]
SC_ADAMW = 'l1_ffn_down'
LATE_SMALL = ['l0_sc_conv_w', 'l0_mix_norm_g']
IN_CD = 5 * HALF + N_HEADS


def _ffn_forward(x, g, get_up, behind_act, get_down, conv_w, tag):
    h = _rmsnorm_fwd(x, g, tag + "_norm")
    u = _matmul(h, get_up(h), "nn", F32, tag + "_up")
    f = _ffn_act_fwd(u, conv_w, tag + "_act", after=behind_act(u))
    w_down, tokens = get_down(f)
    return _matmul(f, w_down, "nn", F32, tag + "_down", res=x, after=tokens), (h, u, f)


def _ffn_backward(x, g, w_up, conv_w, w_down, saved, d_out, send_up, send_down, tag):
    h, u, f = saved
    dw_down = _matmul(f, d_out, "tn", BF16, tag + "_dwdown")
    d_f = _matmul(d_out, w_down, "nt", F32, tag + "_df", after=[send_down(dw_down)])
    du, dcw_gate, dcw_up = _ffn_act_bwd(u, conv_w, d_f, tag + "_dact")
    dw_up = _matmul(h, du, "tn", BF16, tag + "_dwup")
    dh = _matmul(du, w_up, "nt", F32, tag + "_dh", after=[send_up(dw_up)])
    dx, dg = _rmsnorm_bwd(x, g, dh, d_out, tag + "_dnorm")
    return dx, dg, jnp.concatenate([dcw_gate, dcw_up], axis=1)


def kernel(x, l0_mix_norm_g, l0_w_in, l0_sc_conv_w, l0_w_out, l0_ffn_norm_g, l0_ffn_up, l0_ffn_conv_w, l0_ffn_down, l1_mix_norm_g, l1_w_in, l1_fox_b_f, l1_sg_w, l1_sg_b, l1_sg_norm_g, l1_w_out, l1_ffn_norm_g, l1_ffn_up, l1_ffn_conv_w, l1_ffn_down, final_norm_g, loss_target, m_l0_mix_norm_g, m_l0_w_in, m_l0_sc_conv_w, m_l0_w_out, m_l0_ffn_norm_g, m_l0_ffn_up, m_l0_ffn_conv_w, m_l0_ffn_down, m_l1_mix_norm_g, m_l1_w_in, m_l1_fox_b_f, m_l1_sg_w, m_l1_sg_b, m_l1_sg_norm_g, m_l1_w_out, m_l1_ffn_norm_g, m_l1_ffn_up, m_l1_ffn_conv_w, m_l1_ffn_down, m_final_norm_g, v_l0_mix_norm_g, v_l0_w_in, v_l0_sc_conv_w, v_l0_w_out, v_l0_ffn_norm_g, v_l0_ffn_up, v_l0_ffn_conv_w, v_l0_ffn_down, v_l1_mix_norm_g, v_l1_w_in, v_l1_fox_b_f, v_l1_sg_w, v_l1_sg_b, v_l1_sg_norm_g, v_l1_w_out, v_l1_ffn_norm_g, v_l1_ffn_up, v_l1_ffn_conv_w, v_l1_ffn_down, v_final_norm_g):
    given = (l0_mix_norm_g, l0_w_in, l0_sc_conv_w, l0_w_out, l0_ffn_norm_g, l0_ffn_up, l0_ffn_conv_w, l0_ffn_down, l1_mix_norm_g, l1_w_in, l1_fox_b_f, l1_sg_w, l1_sg_b, l1_sg_norm_g, l1_w_out, l1_ffn_norm_g, l1_ffn_up, l1_ffn_conv_w, l1_ffn_down, final_norm_g)
    given_m = (m_l0_mix_norm_g, m_l0_w_in, m_l0_sc_conv_w, m_l0_w_out, m_l0_ffn_norm_g, m_l0_ffn_up, m_l0_ffn_conv_w, m_l0_ffn_down, m_l1_mix_norm_g, m_l1_w_in, m_l1_fox_b_f, m_l1_sg_w, m_l1_sg_b, m_l1_sg_norm_g, m_l1_w_out, m_l1_ffn_norm_g, m_l1_ffn_up, m_l1_ffn_conv_w, m_l1_ffn_down, m_final_norm_g)
    given_v = (v_l0_mix_norm_g, v_l0_w_in, v_l0_sc_conv_w, v_l0_w_out, v_l0_ffn_norm_g, v_l0_ffn_up, v_l0_ffn_conv_w, v_l0_ffn_down, v_l1_mix_norm_g, v_l1_w_in, v_l1_fox_b_f, v_l1_sg_w, v_l1_sg_b, v_l1_sg_norm_g, v_l1_w_out, v_l1_ffn_norm_g, v_l1_ffn_up, v_l1_ffn_conv_w, v_l1_ffn_down, v_final_norm_g)
    wt = dict(zip(WEIGHTS, given))
    mom = dict(zip(WEIGHTS, given_m))
    var = dict(zip(WEIGHTS, given_v))
    s = x.shape[1]
    t = ATT_BLOCK
    x0, target = x[0], loss_target[0]
    chip = 2 * lax.axis_index("x") + lax.axis_index("y")

    pos = jnp.stack([chip, lax.axis_index("c")]).astype(jnp.int32)

    conv_shard = jnp.concatenate([wt[n] for n in CONV], axis=1)
    gathers, token = [], ()
    for gi, names in enumerate(GATHER_GROUPS):
        placed = [_place_shard(pos, wt[n], BIG[n], "place_" + n, token) for n in names]
        send, recv, thru, tok = _gather_start(placed, [BIG[n] for n in names], "gather_start_%d" % gi, token)
        gathers.append((send, recv, thru))
        if gi == 0:
            conv_started = _gather_small_start(conv_shard, [tok])
            tok = conv_started[3]
        token = [tok]
    token = token[0]
    full = {}
    conv_full = {}

    def finish_conv(after):
        landed = _gather_small_finish(*conv_started[:3], after)
        conv_all = lax.dynamic_update_slice(landed, conv_shard[None], (chip, 0, 0))
        at = 0
        for n in CONV:
            cw = wt[n].shape[1]
            conv_full[n] = jnp.transpose(conv_all[:, :, at:at + cw], (1, 0, 2)).reshape(3, 4 * cw)
            at += cw

    def forward_gather(gi, after):
        send, recv, thru = gathers[gi]
        kinds = [BIG[n] for n in GATHER_GROUPS[gi]]
        gathers[gi] = _gather_forward(thru, kinds, send, recv, after, "gather_forward_%d" % gi)
        return gathers[gi][3]

    def finish_gather(gi, after):
        send, recv, thru, tok = gathers[gi]
        names = GATHER_GROUPS[gi]
        wholes = _gather_finish(thru, [BIG[n] for n in names], send, recv, tok if after is None else after,
                                "gather_finish_%d" % gi)
        full.update(zip(names, wholes))

    def vec(name):
        return wt[name].reshape(1, -1)

    h0 = _rmsnorm_fwd(x0, vec('l0_mix_norm_g'), "l0_mix_norm", after=[token])
    forward_gather(0, h0)
    finish_gather(0, None)
    p0 = _matmul(h0, full['l0_w_in'], "nn", F32, "l0_in")
    a_out, sb_carries = _sb_fwd(p0, "l0_sb", after=[forward_gather(1, p0)])
    finish_gather(1, a_out)
    finish_conv(p0)
    ab0 = _sc_fwd(p0, conv_full['l0_sc_conv_w'], a_out, "l0_sc")
    x1 = _matmul(ab0, full['l0_w_out'], "nn", F32, "l0_out", res=x0, after=[forward_gather(2, ab0)])

    def ffn_weights(up_group, next_group):
        def get_up(h):
            finish_gather(up_group, h)
            return full[GATHER_GROUPS[up_group][0]]

        def behind_act(u):
            return [forward_gather(up_group + 1, u)]

        def get_down(f):
            finish_gather(up_group + 1, f)
            return full[GATHER_GROUPS[up_group + 1][0]], ([forward_gather(next_group, f)] if next_group else ())

        return get_up, behind_act, get_down

    x2, ffn0_saved = _ffn_forward(x1, vec('l0_ffn_norm_g'), *ffn_weights(2, 4), conv_full['l0_ffn_conv_w'], "l0_ffn")
    h2 = _rmsnorm_fwd(x2, vec('l1_mix_norm_g'), "l1_mix_norm")
    finish_gather(4, h2)
    w_in1 = jnp.transpose(full['l1_w_in'], (1, 0, 2)).reshape(D_MODEL, IN_CD)
    w_in1_main = w_in1[:, :5 * HALF]
    w_in1_f = jnp.pad(w_in1[:, 5 * HALF:], ((0, 0), (0, 128 - N_HEADS)))
    p1 = _matmul(h2, w_in1_main, "nn", F32, "l1_in")
    f_logit = _matmul(h2, w_in1_f, "nn", F32, "l1_in_f", after=[forward_gather(5, p1)])
    b_f = jnp.pad(wt['l1_fox_b_f'], (0, 128 - N_HEADS)).reshape(1, 128)
    c_heads = _fox_prep(f_logit, b_f, "l1_fox_prep")[:, :N_HEADS].T
    c_col = c_heads[:, :, None]
    c_row = c_heads.reshape(N_HEADS, s // t, 1, t)
    sg_bias = jnp.repeat(wt['l1_sg_b'].T, HEAD, axis=1)
    sg_gain = vec('l1_sg_norm_g')
    c_out = _sg_fwd(p1, wt['l1_sg_w'], sg_bias, sg_gain, "l1_sg")
    d_out, lse, cd1 = _fox_fwd(p1, c_col, c_row, c_out, "l1_fox", after=[forward_gather(6, c_out)])
    finish_gather(5, d_out)
    x3 = _matmul(cd1, full['l1_w_out'], "nn", F32, "l1_out", res=x2)
    x4, ffn1_saved = _ffn_forward(x3, vec('l1_ffn_norm_g'), *ffn_weights(6, None), conv_full['l1_ffn_conv_w'], "l1_ffn")
    dx4, dg_final, loss_part = _loss_head(x4, vec('final_norm_g'), target, "loss_head")

    grads = {'final_norm_g': dg_final}
    scatters = []

    def send_grads(names):
        def start(*group):
            send, recv, thru, lands, tok = _scatter_start(list(group), [BIG[n] for n in names],
                                                          "scatter_start_%d" % len(scatters))
            scatters.append((names, send, recv, thru, lands))
            return tok
        return start

    dx3, grads['l1_ffn_norm_g'], grads['l1_ffn_conv_w'] = _ffn_backward(
        x3, vec('l1_ffn_norm_g'), full['l1_ffn_up'], conv_full['l1_ffn_conv_w'], full['l1_ffn_down'], ffn1_saved, dx4,
        send_grads(['l1_ffn_up']), send_grads(['l1_ffn_down']), "l1_ffn")
    dw_out1 = _matmul(cd1, dx3, "tn", BF16, "l1_dwout")
    d_cd = _matmul(dx3, full['l1_w_out'], "nt", F32, "l1_dcd")
    du, dv, grads['l1_sg_w'], db_sg, grads['l1_sg_norm_g'] = _sg_bwd(p1, wt['l1_sg_w'], sg_bias, sg_gain, d_cd, "l1_dsg")
    grads['l1_sg_b'] = db_sg[:, :N_HEADS].T
    dq, dk, dvv, dcol, drow = _fox_bwd(p1, c_col, c_row, lse, d_cd, d_out, "l1_dfox")
    d_f_logit, d_b_f = _fox_post(drow, dcol, f_logit, b_f, "l1_fox_post")
    grads['l1_fox_b_f'] = d_b_f[0, :N_HEADS]
    dp1 = jnp.concatenate([a.astype(BF16) for a in (du, dv, dq, dk, dvv)], axis=1)
    dw_main = _matmul(h2, dp1, "tn", BF16, "l1_dwin")
    dw_f = _matmul(h2, d_f_logit, "tn", BF16, "l1_dwin_f")
    dw_in1 = jnp.concatenate([dw_main, dw_f[:, :N_HEADS]], axis=1)
    dw_in1 = jnp.transpose(dw_in1.reshape(D_MODEL, 4, IN_CD // 4), (1, 0, 2))
    dh2 = _matmul(dp1, w_in1_main, "nt", F32, "l1_dh", after=[send_grads(['l1_w_out', 'l1_w_in'])(dw_out1, dw_in1)])
    dh2 = _matmul(d_f_logit, w_in1_f, "nt", F32, "l1_dh_f", res=dh2)
    dx2, grads['l1_mix_norm_g'] = _rmsnorm_bwd(x2, vec('l1_mix_norm_g'), dh2, dx3, "l1_dmix_norm")
    dx1, grads['l0_ffn_norm_g'], grads['l0_ffn_conv_w'] = _ffn_backward(
        x1, vec('l0_ffn_norm_g'), full['l0_ffn_up'], conv_full['l0_ffn_conv_w'], full['l0_ffn_down'], ffn0_saved, dx2,
        send_grads(['l0_ffn_up']), send_grads(['l0_ffn_down']), "l0_ffn")
    early_names = [n for n in SMALL if n not in LATE_SMALL]
    early = _small_reduce_start(_pack([grads[n] for n in early_names] + [loss_part]), "small_start", [dx1])
    dw_out0 = _matmul(ab0, dx1, "tn", BF16, "l0_dwout", after=[early[3]])
    d_ab = _matmul(dx1, full['l0_w_out'], "nt", F32, "l0_dab", after=[send_grads(['l0_w_out'])(dw_out0)])
    dq0, dk0, dv0 = _sb_bwd(p0, d_ab, sb_carries, "l0_dsb")
    dgb, dgc, dhin, grads['l0_sc_conv_w'] = _sc_bwd(p0, conv_full['l0_sc_conv_w'], d_ab, "l0_dsc")
    dp0 = jnp.concatenate([a.astype(BF16) for a in (dq0, dk0, dv0, dgb, dgc, dhin)], axis=1)
    dw_in0 = _matmul(h0, dp0, "tn", BF16, "l0_dwin")
    dh0 = _matmul(dp0, full['l0_w_in'], "nt", F32, "l0_dh", after=[send_grads(['l0_w_in'])(dw_in0)])
    dx0, grads['l0_mix_norm_g'] = _rmsnorm_bwd(x0, vec('l0_mix_norm_g'), dh0, dx1, "l0_dmix_norm")

    shard_grads, delta, new_m, new_v, swaps = {}, {}, {}, {}, {}

    def reduce_group(gi, after):
        names, send, recv, thru, lands = scatters[gi]
        kinds = [BIG[n] for n in names]
        g_thru, landed = _scatter_wait(thru, lands, kinds, send, recv, after, "scatter_wait_%d" % gi)
        halves = [_sum_partials(pos, g, ld, kd, "sum_" + n) for n, g, ld, kd in zip(names, g_thru, landed, kinds)]
        s_send, s_recv, h_thru, s_lands, tok = _swap_start(halves, "swap_start_%d" % gi)
        swaps[gi] = (names, s_send, s_recv, h_thru, s_lands)
        return tok

    def update_group(gi, after):
        names, s_send, s_recv, h_thru, s_lands = swaps[gi]
        mine, theirs = _swap_wait(h_thru, s_lands, s_send, s_recv, after, "swap_wait_%d" % gi)
        for n, gm, gs in zip(names, mine, theirs):
            if n == SC_ADAMW:
                shard_grads[n], delta[n], new_m[n], new_v[n] = _adamw_sparsecore(wt[n], gm, gs, mom[n], var[n],
                                                                                 "adamw_sc_" + n)
            else:
                shard_grads[n], delta[n], new_m[n], new_v[n] = _adamw_shard(pos, wt[n], gm, gs, mom[n], var[n],
                                                                            "adamw_" + n)
        return [delta[n] for n in names if n != SC_ADAMW] or [mine[0]]

    after = reduce_group(2, reduce_group(1, reduce_group(0, dx0)))
    after = update_group(2, update_group(1, update_group(0, after)))
    after = reduce_group(5, reduce_group(4, reduce_group(3, after)))
    after = update_group(5, update_group(4, update_group(3, after)))
    after = reduce_group(6, after)
    def small_shapes(names):
        return [conv_full[n].shape if n in CONV else wt[n].shape for n in names]

    me = (2 * chip + lax.axis_index("c")).astype(jnp.int32).reshape(1)
    early_all = _small_reduce_finish(me, early[0], early[1], early[2], after, "small_early")
    early_sums = _unpack(early_all, small_shapes(early_names) + [loss_part.shape])
    loss = early_sums[-1][0, 0]
    late_all = _allreduce_small(_pack([grads[n] for n in LATE_SMALL]), [early_all])
    small_sums = dict(zip(early_names + LATE_SMALL, early_sums[:-1] + _unpack(late_all, small_shapes(LATE_SMALL))))
    for n in SMALL:
        g = small_sums[n]
        shard_grads[n] = lax.dynamic_slice_in_dim(g, chip * wt[n].shape[1], wt[n].shape[1], axis=1) if n in CONV else g
    update_group(6, late_all)
    def flat2d(a):
        return a.reshape(-1, a.shape[-1])

    small_out = _adamw_small(*[[flat2d(src[n]) for n in SMALL] for src in (wt, shard_grads, mom, var)], "adamw_small")
    for out, arrays in zip((delta, new_m, new_v), small_out):
        out.update((n, a.reshape(wt[n].shape)) for n, a in zip(SMALL, arrays))

    return (loss, dx0[None], *[shard_grads[n] for n in WEIGHTS], *[delta[n] for n in WEIGHTS],
            *[new_m[n] for n in WEIGHTS], *[new_v[n] for n in WEIGHTS])
```

```python
import jax
import jax.numpy as jnp
from jax import lax
from jax.experimental import pallas as pl
from jax.experimental.pallas import tpu as pltpu
from jax.experimental.pallas import tpu_sc as plsc

F32 = jnp.float32
BF16 = jnp.bfloat16

D_MODEL = 2048
HEAD = 128
N_HEADS = 8
HALF = N_HEADS * HEAD
D_FF = 5632
EPS = 1e-6
ATT_SCALE = HEAD ** -0.5
ATT_BLOCK = 512
NEG = -1e30

ADAM_LR = 0.001
ADAM_B1 = 0.9
ADAM_B2 = 0.999
ADAM_EPS = 1e-08
ADAM_WD = 0.01
ADAM_STEP = 10

VMEM_LIMIT_BYTES = 48 * 1024 * 1024
MM_VMEM_LIMIT_BYTES = 56 * 1024 * 1024
ATT_VMEM_LIMIT_BYTES = 48 * 1024 * 1024
MESH = pl.DeviceIdType.MESH
HBM_SPEC = pl.BlockSpec(memory_space=pltpu.HBM)
VMEM_SPEC = pl.BlockSpec(memory_space=pltpu.VMEM)


def _pcall(body, after=(), **kw):
    if not after:
        return pl.pallas_call(body, **kw)
    n_in, n_after, inner = len(kw["in_specs"]), len(after), body
    kw["in_specs"] = list(kw["in_specs"]) + [pl.BlockSpec(memory_space=pl.ANY)] * n_after

    def body(*refs):
        inner(*refs[:n_in], *refs[n_in + n_after:])

    call = pl.pallas_call(body, **kw)
    return lambda *args: call(*args, *after)


def _params(*semantics, vmem_limit=VMEM_LIMIT_BYTES):
    return pltpu.CompilerParams(dimension_semantics=semantics, vmem_limit_bytes=vmem_limit)


def _pick(n, cap):
    best = None
    for t in range(128, min(n, cap) + 1, 128):
        if n % t == 0:
            best = t
    return n if best is None else best


def _dot(a, b, dims):
    return lax.dot_general(a, b, (dims, ((), ())), preferred_element_type=F32)


def _dot_nn(a, b):
    return _dot(a, b, ((1,), (0,)))


def _dot_nt(a, b):
    return _dot(a, b, ((1,), (1,)))


def _dot_tn(a, b):
    return _dot(a, b, ((0,), (0,)))


def _split3(x):
    hi = x.astype(BF16)
    r1 = x - hi.astype(F32)
    mid = r1.astype(BF16)
    lo = (r1 - mid.astype(F32)).astype(BF16)
    return hi, mid, lo


def _log_sigmoid(z):
    return jnp.minimum(z, 0.0) - jnp.log1p(jnp.exp(-jnp.abs(z)))


_GELU_K = 0.7978845608028654


def _gelu(x):
    return 0.5 * x * (1.0 + jnp.tanh(_GELU_K * (x + 0.044715 * x * x * x)))


def _gelu_grad(x):
    t = jnp.tanh(_GELU_K * (x + 0.044715 * x * x * x))
    return 0.5 * (1.0 + t) + 0.5 * x * (1.0 - t * t) * _GELU_K * (1.0 + 3.0 * 0.044715 * x * x)


SUBLANES = 8


def _shift_down(x, k):
    rolled = pltpu.roll(x, k, axis=0)
    head = rolled[:SUBLANES]
    head = jnp.where(lax.broadcasted_iota(jnp.int32, head.shape, 0) >= k, head, 0.0)
    return jnp.concatenate([head, rolled[SUBLANES:]], axis=0)


def _shift_up(x, k):
    n = x.shape[0]
    rolled = pltpu.roll(x, n - k, axis=0)
    tail = rolled[n - SUBLANES:]
    tail = jnp.where(lax.broadcasted_iota(jnp.int32, tail.shape, 0) < SUBLANES - k, tail, 0.0)
    return jnp.concatenate([rolled[:n - SUBLANES], tail], axis=0)


def _conv3(s, w, shifted=None):
    s1, s2 = shifted if shifted else (_shift_down(s, 1), _shift_down(s, 2))
    return w[0:1, :] * s2 + w[1:2, :] * s1 + w[2:3, :] * s


def _conv3_transpose(d, w):
    return w[2:3, :] * d + w[1:2, :] * _shift_up(d, 1) + w[0:1, :] * _shift_up(d, 2)


def _conv3_wgrad(d, s, shifted, dw_ref):
    s1, s2 = shifted
    dw_ref[0:1, :] = jnp.sum(d * s2, axis=0, keepdims=True)
    dw_ref[1:2, :] = jnp.sum(d * s1, axis=0, keepdims=True)
    dw_ref[2:3, :] = jnp.sum(d * s, axis=0, keepdims=True)


MM_TILE_M, MM_TILE_N, MM_TILE_K = 1408, 512, 5632


def _matmul(a, b, mode, out_dtype, name, res=None, after=()):
    a_parts = a.shape[0] if a.ndim == 3 else 1
    b_parts = b.shape[0] if b.ndim == 3 else 1
    a_shape = (a.shape[1], a_parts * a.shape[2]) if a.ndim == 3 else a.shape
    b_shape = (b.shape[1], b_parts * b.shape[2]) if b.ndim == 3 else b.shape
    assert (a_parts == 1 or mode != "tn") and (b_parts == 1 or mode == "tn")
    if mode == "nn":
        (m, k), (k2, n) = a_shape, b_shape
    elif mode == "nt":
        (m, k), (n, k2) = a_shape, b_shape
    else:
        (k, m), (k2, n) = a_shape, b_shape
    assert k == k2, (a.shape, b.shape, mode)
    tm, tn, tk = _pick(m, MM_TILE_M), _pick(n // b_parts, MM_TILE_N), _pick(k // a_parts, MM_TILE_K)
    nk = k // tk
    if mode == "tn":
        a_spec = pl.BlockSpec((tk, tm), lambda i, j, kk: (kk, i))
    elif a_parts > 1:
        per = nk // a_parts
        a_spec = pl.BlockSpec((None, tm, tk), lambda i, j, kk: (kk // per, i, kk % per))
    else:
        a_spec = pl.BlockSpec((tm, tk), lambda i, j, kk: (i, kk))
    if mode == "nt":
        b_spec = pl.BlockSpec((tn, tk), lambda i, j, kk: (j, kk))
    elif b_parts > 1:
        per = n // b_parts // tn
        b_spec = pl.BlockSpec((None, tk, tn), lambda i, j, kk: (j // per, kk, j % per))
    else:
        b_spec = pl.BlockSpec((tk, tn), lambda i, j, kk: (kk, j))
    o_spec = pl.BlockSpec((tm, tn), lambda i, j, kk: (i, j))
    dims = {"nn": ((1,), (0,)), "nt": ((1,), (1,)), "tn": ((0,), (0,))}[mode]
    has_res = res is not None

    def body(*refs):
        a_ref, b_ref = refs[0], refs[1]
        r_ref = refs[2] if has_res else None
        o_ref = refs[3] if has_res else refs[2]
        part = _dot(a_ref[...].astype(BF16), b_ref[...].astype(BF16), dims)

        def finish(total):
            if has_res:
                total = total + r_ref[...]
            o_ref[...] = total.astype(out_dtype)

        if nk == 1:
            finish(part)
        else:
            acc_ref = refs[-1]
            kk = pl.program_id(2)

            @pl.when(kk == 0)
            def _():
                acc_ref[...] = part

            @pl.when(kk > 0)
            def _():
                acc_ref[...] += part

            @pl.when(kk == nk - 1)
            def _():
                finish(acc_ref[...])

    in_specs = [a_spec, b_spec] + ([o_spec] if has_res else [])
    args = (a, b) + ((res,) if has_res else ())
    return _pcall(
        body, after=after, name=name, grid=(m // tm, n // tn, nk),
        in_specs=in_specs, out_specs=o_spec,
        out_shape=jax.ShapeDtypeStruct((m, n), out_dtype),
        scratch_shapes=[pltpu.VMEM((tm, tn), F32)] if nk > 1 else [],
        compiler_params=_params("parallel", "parallel", "arbitrary", vmem_limit=MM_VMEM_LIMIT_BYTES),
    )(*args)


ROW_TILE = 256


def _rmsnorm_fwd(x, g, name, after=()):
    s, d = x.shape

    def body(x_ref, g_ref, o_ref):
        xf = x_ref[...]
        r = lax.rsqrt(jnp.mean(xf * xf, axis=-1, keepdims=True) + EPS)
        o_ref[...] = (xf * r * g_ref[...]).astype(BF16)

    row = pl.BlockSpec((ROW_TILE, d), lambda i: (i, 0))
    vec = pl.BlockSpec((1, d), lambda i: (0, 0))
    return _pcall(body, after=after, name=name, grid=(s // ROW_TILE,), in_specs=[row, vec], out_specs=row,
                  out_shape=jax.ShapeDtypeStruct((s, d), BF16), compiler_params=_params("parallel"))(x, g)


def _rmsnorm_bwd(x, g, dh, dres, name):
    s, d = x.shape

    def body(x_ref, g_ref, dh_ref, dres_ref, dx_ref, dg_ref):
        xf = x_ref[...]
        r = lax.rsqrt(jnp.mean(xf * xf, axis=-1, keepdims=True) + EPS)
        xhat = xf * r
        dh_v = dh_ref[...]
        dxh = dh_v * g_ref[...]
        proj = jnp.mean(dxh * xhat, axis=-1, keepdims=True)
        dx_ref[...] = dres_ref[...] + r * (dxh - xhat * proj)
        part = jnp.sum(dh_v * xhat, axis=0, keepdims=True)

        @pl.when(pl.program_id(0) == 0)
        def _():
            dg_ref[...] = part

        @pl.when(pl.program_id(0) > 0)
        def _():
            dg_ref[...] += part

    row = pl.BlockSpec((ROW_TILE, d), lambda i: (i, 0))
    vec = pl.BlockSpec((1, d), lambda i: (0, 0))
    return _pcall(body, name=name, grid=(s // ROW_TILE,), in_specs=[row, vec, row, row], out_specs=[row, vec],
                  out_shape=[jax.ShapeDtypeStruct((s, d), F32), jax.ShapeDtypeStruct((1, d), F32)],
                  compiler_params=_params("arbitrary"))(x, g, dh, dres)


def _loss_head(x, g, target, name):
    s, d = x.shape

    def body(x_ref, g_ref, t_ref, dx_ref, dg_ref, loss_ref):
        xf = x_ref[...]
        r = lax.rsqrt(jnp.mean(xf * xf, axis=-1, keepdims=True) + EPS)
        xhat = xf * r
        gv = g_ref[...]
        err = xhat * gv - t_ref[...]
        dy = err * (1.0 / d)
        dxh = dy * gv
        proj = jnp.mean(dxh * xhat, axis=-1, keepdims=True)
        dx_ref[...] = r * (dxh - xhat * proj)
        dg_part = jnp.sum(dy * xhat, axis=0, keepdims=True)
        row_loss = jnp.sum(err * err, axis=-1, keepdims=True) * (0.5 / d)
        loss_part = jnp.broadcast_to(jnp.sum(row_loss, axis=0, keepdims=True), (1, 128))

        @pl.when(pl.program_id(0) == 0)
        def _():
            dg_ref[...] = dg_part
            loss_ref[...] = loss_part

        @pl.when(pl.program_id(0) > 0)
        def _():
            dg_ref[...] += dg_part
            loss_ref[...] += loss_part

    row = pl.BlockSpec((ROW_TILE, d), lambda i: (i, 0))
    vec = pl.BlockSpec((1, d), lambda i: (0, 0))
    one = pl.BlockSpec((1, 128), lambda i: (0, 0))
    return _pcall(body, name=name, grid=(s // ROW_TILE,), in_specs=[row, vec, row], out_specs=[row, vec, one],
                  out_shape=[jax.ShapeDtypeStruct((s, d), F32), jax.ShapeDtypeStruct((1, d), F32),
                             jax.ShapeDtypeStruct((1, 128), F32)],
                  compiler_params=_params("arbitrary"))(x, g, target)


HEADS_PER_STEP = 2
GROUP_W = HEADS_PER_STEP * HEAD
N_GROUPS = N_HEADS // HEADS_PER_STEP


def _head_cols(h):
    return slice(h * HEAD, (h + 1) * HEAD)


def _head_specs(s, col0):
    t = ATT_BLOCK
    g0 = [c // HEADS_PER_STEP for c in col0]
    qspec = pl.BlockSpec((t, GROUP_W), lambda g, i: (i, g0[0] + g))
    kspec = pl.BlockSpec((s, GROUP_W), lambda g, i: (0, g0[1] + g))
    vspec = pl.BlockSpec((s, GROUP_W), lambda g, i: (0, g0[2] + g))
    return qspec, kspec, vspec


TRI = 256


def _order_matrix(later):
    r, c = lax.broadcasted_iota(jnp.int32, (TRI, TRI), 0), lax.broadcasted_iota(jnp.int32, (TRI, TRI), 1)
    return (r > c if later else r < c).astype(BF16)


def _exact_dot(x, m, later):
    parts = [x[:, c:c + TRI] for c in range(0, x.shape[1], TRI)]
    totals = [jnp.sum(p, axis=1, keepdims=True) for p in parts] if len(parts) > 1 else None
    out = []
    for j, p in enumerate(parts):
        hi = p.astype(BF16)
        lo = (p - hi.astype(F32)).astype(BF16)
        acc = _dot_nn(hi, m) + _dot_nn(lo, m)
        for other in (range(j + 1, len(parts)) if later else range(j)):
            acc = acc + totals[other]
        out.append(acc)
    return out[0] if len(out) == 1 else jnp.concatenate(out, axis=1)


def _sb_block(q, kblk, carry_l, u, diagonal):
    t = ATT_BLOCK
    z = _dot_nt(q, kblk) * ATT_SCALE
    sp = jnp.maximum(z, 0.0) + jnp.log(1.0 + jnp.exp(-jnp.abs(z)))
    if not diagonal:
        l = -sp
        return z, None, l, jnp.exp(z + l + _exact_dot(l, u, True) + carry_l)
    mask = lax.broadcasted_iota(jnp.int32, (t, t), 1) < lax.broadcasted_iota(jnp.int32, (t, t), 0)
    l = jnp.where(mask, -sp, 0.0)
    a = jnp.where(mask, jnp.exp(z - sp + _exact_dot(l, u, True) + carry_l), 0.0)
    return z, mask, l, a


def _sb_carry_spec(s):
    t = ATT_BLOCK
    return pl.BlockSpec((HEADS_PER_STEP, None, s // t, t, 1), lambda g, i: (g, i, 0, 0, 0))


def _sb_fwd(p, name, after=()):
    s = p.shape[0]
    t = ATT_BLOCK
    nb = s // t

    def body(q_ref, k_ref, v_ref, o_ref, cl_ref):
        i = pl.program_id(1)
        heads = range(HEADS_PER_STEP)
        q = [q_ref[:, _head_cols(h)].astype(BF16) for h in heads]
        u = _order_matrix(True)
        cl_ref[...] = jnp.zeros_like(cl_ref)

        def tile(kb, carry, diagonal):
            ks = pl.multiple_of(kb * t, t)
            out = []
            for h in heads:
                acc, carry_l = carry[h]
                kblk = k_ref[pl.ds(ks, t), _head_cols(h)].astype(BF16)
                vblk = v_ref[pl.ds(ks, t), _head_cols(h)].astype(BF16)
                cl_ref[h, kb] = carry_l
                _, _, l, a = _sb_block(q[h], kblk, carry_l, u, diagonal)
                out.append((acc + _dot_nn(a.astype(BF16), vblk), carry_l + jnp.sum(l, axis=1, keepdims=True)))
            return tuple(out)

        carry = tile(i, tuple((jnp.zeros((t, HEAD), F32), jnp.zeros((t, 1), F32)) for _ in heads), True)
        carry = lax.fori_loop(0, i, lambda n, c: tile(i - 1 - n, c, False), carry)
        for h in heads:
            o_ref[:, _head_cols(h)] = carry[h][0].astype(BF16)

    qspec, kspec, vspec = _head_specs(s, (0, N_HEADS, 2 * N_HEADS))
    ospec = pl.BlockSpec((t, GROUP_W), lambda g, i: (i, g))
    return _pcall(body, after=after, name=name, grid=(N_GROUPS, nb), in_specs=[qspec, kspec, vspec],
                  out_specs=[ospec, _sb_carry_spec(s)],
                  out_shape=[jax.ShapeDtypeStruct((s, 2 * HALF), BF16), jax.ShapeDtypeStruct((N_HEADS, nb, nb, t, 1), F32)],
                  compiler_params=_params("parallel", "parallel", vmem_limit=ATT_VMEM_LIMIT_BYTES))(p, p, p)


def _sb_bwd(p, d_ab, carries, name):
    s = p.shape[0]
    t = ATT_BLOCK

    def body(q_ref, k_ref, v_ref, do_ref, cl_ref, dq_ref, dk_ref, dv_ref):
        i = pl.program_id(1)

        @pl.when(i == 0)
        def _():
            dk_ref[...] = jnp.zeros_like(dk_ref)
            dv_ref[...] = jnp.zeros_like(dv_ref)

        heads = range(HEADS_PER_STEP)
        q = [q_ref[:, _head_cols(h)].astype(BF16) for h in heads]
        do = [do_ref[:, _head_cols(h)].astype(BF16) for h in heads]
        u = _order_matrix(True)
        lower = _order_matrix(False)

        def tile(kb, carry, diagonal):
            ks = pl.multiple_of(kb * t, t)
            out = []
            for h in heads:
                dq, carry_g = carry[h]
                kblk = k_ref[pl.ds(ks, t), _head_cols(h)].astype(BF16)
                vblk = v_ref[pl.ds(ks, t), _head_cols(h)].astype(BF16)
                z, mask, _, a = _sb_block(q[h], kblk, cl_ref[h, kb], u, diagonal)
                g = a * _dot_nt(do[h], vblk)
                earlier_g = _exact_dot(g, lower, False) + carry_g
                sig = jax.nn.sigmoid(z)
                dz = g * (1.0 - sig) - sig * earlier_g
                if diagonal:
                    dz = jnp.where(mask, dz, 0.0)
                dz = dz.astype(BF16)
                dv_ref[pl.ds(ks, t), _head_cols(h)] += _dot_tn(a.astype(BF16), do[h])
                dk_ref[pl.ds(ks, t), _head_cols(h)] += _dot_tn(dz, q[h]) * ATT_SCALE
                out.append((dq + _dot_nn(dz, kblk) * ATT_SCALE, carry_g + jnp.sum(g, axis=1, keepdims=True)))
            return tuple(out)

        init = tuple((jnp.zeros((t, HEAD), F32), jnp.zeros((t, 1), F32)) for _ in heads)
        carry = tile(i, lax.fori_loop(0, i, lambda kb, c: tile(kb, c, False), init), True)
        for h in heads:
            dq_ref[:, _head_cols(h)] = carry[h][0]

    qspec, kspec, vspec = _head_specs(s, (0, N_HEADS, 2 * N_HEADS))
    blk = pl.BlockSpec((t, GROUP_W), lambda g, i: (i, g))
    whole = pl.BlockSpec((s, GROUP_W), lambda g, i: (0, g))
    shape = jax.ShapeDtypeStruct((s, HALF), F32)
    return _pcall(body, name=name, grid=(N_GROUPS, s // t), in_specs=[qspec, kspec, vspec, blk, _sb_carry_spec(s)],
                  out_specs=[blk, whole, whole], out_shape=[shape, shape, shape],
                  compiler_params=_params("parallel", "arbitrary", vmem_limit=ATT_VMEM_LIMIT_BYTES))(p, p, p, d_ab, carries)


COL_TILE = 256


def _sc_fwd(p, w, mixed, name):
    s = p.shape[0]
    nb = HALF // COL_TILE

    def body(gb_ref, gc_ref, h_ref, w_ref, mixed_ref, o_ref):
        conv = _conv3(gc_ref[...] * h_ref[...], w_ref[...])
        o_ref[...] = (gb_ref[...] * conv).astype(BF16)

    def col(k):
        return pl.BlockSpec((s, COL_TILE), lambda j: (0, k * nb + j))

    wspec = pl.BlockSpec((3, COL_TILE), lambda j: (0, j))
    return _pcall(body, name=name, grid=(nb,), in_specs=[col(3), col(4), col(5), wspec, pl.BlockSpec(memory_space=pl.ANY)],
                  out_specs=col(1), out_shape=jax.ShapeDtypeStruct(mixed.shape, BF16), input_output_aliases={4: 0},
                  compiler_params=_params("parallel"))(p, p, p, w, mixed)


def _sc_bwd(p, w, d_ab, name):
    s = p.shape[0]
    nb = HALF // COL_TILE

    def body(gb_ref, gc_ref, h_ref, w_ref, d_ref, dgb_ref, dgc_ref, dh_ref, dw_ref):
        gc, hin, wv, d = gc_ref[...], h_ref[...], w_ref[...], d_ref[...]
        sig = gc * hin
        shifted = (_shift_down(sig, 1), _shift_down(sig, 2))
        dgb_ref[...] = d * _conv3(sig, wv, shifted)
        dconv = d * gb_ref[...]
        _conv3_wgrad(dconv, sig, shifted, dw_ref)
        dsig = _conv3_transpose(dconv, wv)
        dgc_ref[...] = dsig * hin
        dh_ref[...] = dsig * gc

    def col(k):
        return pl.BlockSpec((s, COL_TILE), lambda j: (0, k * nb + j))

    wspec = pl.BlockSpec((3, COL_TILE), lambda j: (0, j))
    act = jax.ShapeDtypeStruct((s, HALF), F32)
    return _pcall(body, name=name, grid=(nb,), in_specs=[col(3), col(4), col(5), wspec, col(1)],
                  out_specs=[col(0), col(0), col(0), wspec],
                  out_shape=[act, act, act, jax.ShapeDtypeStruct((3, HALF), F32)],
                  compiler_params=_params("parallel"))(p, p, p, w, d_ab)


def _ffn_act_fwd(u, w, name, after=()):
    s = u.shape[0]
    nb = D_FF // COL_TILE

    def body(ug_ref, uu_ref, wg_ref, wu_ref, o_ref):
        gate = _conv3(ug_ref[...], wg_ref[...])
        up = _conv3(uu_ref[...], wu_ref[...])
        o_ref[...] = (gate * jax.nn.sigmoid(gate) * up).astype(BF16)

    def col(k):
        return pl.BlockSpec((s, COL_TILE), lambda j: (0, k * nb + j))

    def wcol(k):
        return pl.BlockSpec((3, COL_TILE), lambda j: (0, k * nb + j))

    return _pcall(body, after=after, name=name, grid=(nb,), in_specs=[col(0), col(1), wcol(0), wcol(1)], out_specs=col(0),
                  out_shape=jax.ShapeDtypeStruct((s, D_FF), BF16),
                  compiler_params=_params("parallel"))(u, u, w, w)


def _ffn_act_bwd(u, w, d_f, name):
    s = u.shape[0]
    nb = D_FF // COL_TILE

    def body(ug_ref, uu_ref, wg_ref, wu_ref, d_ref, du_ref, dwg_ref, dwu_ref):
        ug, uu, wg, wu, d = ug_ref[...], uu_ref[...], wg_ref[...], wu_ref[...], d_ref[...]
        ug_shifted = (_shift_down(ug, 1), _shift_down(ug, 2))
        uu_shifted = (_shift_down(uu, 1), _shift_down(uu, 2))
        gate = _conv3(ug, wg, ug_shifted)
        up = _conv3(uu, wu, uu_shifted)
        sig = jax.nn.sigmoid(gate)
        d_up = d * gate * sig
        d_gate = d * up * sig * (1.0 + gate * (1.0 - sig))
        _conv3_wgrad(d_gate, ug, ug_shifted, dwg_ref)
        _conv3_wgrad(d_up, uu, uu_shifted, dwu_ref)
        du_ref[0] = _conv3_transpose(d_gate, wg).astype(BF16)
        du_ref[1] = _conv3_transpose(d_up, wu).astype(BF16)

    def col(k):
        return pl.BlockSpec((s, COL_TILE), lambda j: (0, k * nb + j))

    def wcol(k):
        return pl.BlockSpec((3, COL_TILE), lambda j: (0, k * nb + j))

    both = pl.BlockSpec((2, s, COL_TILE), lambda j: (0, 0, j))
    wsh = jax.ShapeDtypeStruct((3, D_FF), F32)
    return _pcall(body, name=name, grid=(nb,), in_specs=[col(0), col(1), wcol(0), wcol(1), col(0)],
                  out_specs=[both, wcol(0), wcol(0)], out_shape=[jax.ShapeDtypeStruct((2, s, D_FF), BF16), wsh, wsh],
                  compiler_params=_params("parallel"))(u, u, w, w, d_f)


def _sg_common(u, v, g, w_ref, bias, mixed_ref):
    rows = u.shape[0]
    gu = _gelu(u)
    gv = _gelu(v)
    xc = gv - jnp.mean(gv, axis=-1, keepdims=True)
    rstd = lax.rsqrt(jnp.mean(xc * xc, axis=-1, keepdims=True) + EPS)
    xhat = xc * rstd
    vn = xhat * g
    tril = lax.broadcasted_iota(jnp.int32, (HEAD, HEAD), 0) >= lax.broadcasted_iota(jnp.int32, (HEAD, HEAD), 1)
    wts = [jnp.where(tril, w_ref[grp], 0.0).astype(BF16) for grp in range(N_HEADS)]
    for n in range(rows // HEAD):
        for grp in range(N_HEADS):
            blk = vn[n * HEAD:(n + 1) * HEAD, grp * HEAD:(grp + 1) * HEAD].astype(BF16)
            mixed_ref[n * HEAD:(n + 1) * HEAD, grp * HEAD:(grp + 1) * HEAD] = _dot_nn(wts[grp], blk)
    mixed = mixed_ref[...] + jnp.concatenate([bias] * (rows // HEAD), axis=0)
    return gu, xhat, rstd, vn, mixed, wts, tril


def _sg_fwd(p, sg_w, bias, g, name):
    s = p.shape[0]

    def body(u_ref, v_ref, w_ref, b_ref, g_ref, o_ref, mixed_ref):
        gu, _, _, _, mixed, _, _ = _sg_common(u_ref[...], v_ref[...], g_ref[...], w_ref, b_ref[...], mixed_ref)
        o_ref[...] = (gu * mixed).astype(BF16)

    def half(k):
        return pl.BlockSpec((ROW_TILE, HALF), lambda i: (i, k))

    wspec = pl.BlockSpec((N_HEADS, HEAD, HEAD), lambda i: (0, 0, 0))
    bspec = pl.BlockSpec((HEAD, HALF), lambda i: (0, 0))
    gspec = pl.BlockSpec((1, HALF), lambda i: (0, 0))
    return _pcall(body, name=name, grid=(s // ROW_TILE,), in_specs=[half(0), half(1), wspec, bspec, gspec],
                  out_specs=half(0), out_shape=jax.ShapeDtypeStruct((s, 2 * HALF), BF16),
                  scratch_shapes=[pltpu.VMEM((ROW_TILE, HALF), F32)],
                  compiler_params=_params("parallel"))(p, p, sg_w, bias, g)


def _sg_bwd(p, sg_w, bias, g, d_cd, name):
    s = p.shape[0]
    nsteps = s // ROW_TILE

    def body(u_ref, v_ref, w_ref, b_ref, g_ref, d_ref, du_ref, dv_ref, dw_ref, db_ref, dg_ref,
             mixed_ref, dvn_ref, dbias_ref):
        i = pl.program_id(0)
        u, v, gain, d = u_ref[...], v_ref[...], g_ref[...], d_ref[...]
        gu, xhat, rstd, vn, mixed, wts, tril = _sg_common(u, v, gain, w_ref, b_ref[...], mixed_ref)

        @pl.when(i == 0)
        def _():
            dw_ref[...] = jnp.zeros_like(dw_ref)
            dg_ref[...] = jnp.zeros_like(dg_ref)
            dbias_ref[...] = jnp.zeros_like(dbias_ref)

        du_ref[...] = d * mixed * _gelu_grad(u)
        dm = d * gu
        for n in range(ROW_TILE // HEAD):
            rs = slice(n * HEAD, (n + 1) * HEAD)
            dbias_ref[...] += dm[rs, :]
            for grp in range(N_HEADS):
                cs = slice(grp * HEAD, (grp + 1) * HEAD)
                dm_blk = dm[rs, cs].astype(BF16)
                dw_ref[grp] += jnp.where(tril, _dot_nt(dm_blk, vn[rs, cs].astype(BF16)), 0.0)
                dvn_ref[rs, cs] = _dot_tn(wts[grp], dm_blk)
        dvn = dvn_ref[...]
        dg_ref[...] += jnp.sum(dvn * xhat, axis=0, keepdims=True)
        dxh = dvn * gain
        d_gv = rstd * (dxh - jnp.mean(dxh, axis=-1, keepdims=True) - xhat * jnp.mean(dxh * xhat, axis=-1, keepdims=True))
        dv_ref[...] = d_gv * _gelu_grad(v)

        @pl.when(i == nsteps - 1)
        def _():
            lane = lax.broadcasted_iota(jnp.int32, (HEAD, HEAD), 1)
            out = jnp.zeros((HEAD, HEAD), F32)
            for grp in range(N_HEADS):
                tot = jnp.sum(dbias_ref[:, grp * HEAD:(grp + 1) * HEAD], axis=1, keepdims=True)
                out = out + jnp.where(lane == grp, tot, 0.0)
            db_ref[...] = out

    def half(k):
        return pl.BlockSpec((ROW_TILE, HALF), lambda i: (i, k))

    wspec = pl.BlockSpec((N_HEADS, HEAD, HEAD), lambda i: (0, 0, 0))
    bspec = pl.BlockSpec((HEAD, HALF), lambda i: (0, 0))
    gspec = pl.BlockSpec((1, HALF), lambda i: (0, 0))
    dbspec = pl.BlockSpec((HEAD, HEAD), lambda i: (0, 0))
    act = jax.ShapeDtypeStruct((s, HALF), F32)
    return _pcall(body, name=name, grid=(nsteps,), in_specs=[half(0), half(1), wspec, bspec, gspec, half(0)],
                  out_specs=[half(0), half(0), wspec, dbspec, gspec],
                  out_shape=[act, act, jax.ShapeDtypeStruct((N_HEADS, HEAD, HEAD), F32),
                             jax.ShapeDtypeStruct((HEAD, HEAD), F32), jax.ShapeDtypeStruct((1, HALF), F32)],
                  scratch_shapes=[pltpu.VMEM((ROW_TILE, HALF), F32), pltpu.VMEM((ROW_TILE, HALF), F32),
                                  pltpu.VMEM((HEAD, HALF), F32)],
                  compiler_params=_params("arbitrary"))(p, p, sg_w, bias, g, d_cd)


def _fox_prep(f, b, name):
    s = f.shape[0]
    t = ATT_BLOCK

    def body(f_ref, b_ref, c_ref):
        tri = (lax.broadcasted_iota(jnp.int32, (t, t), 0) >= lax.broadcasted_iota(jnp.int32, (t, t), 1)).astype(BF16)
        carry = jnp.zeros((1, 128), F32)
        for n in range(s // t):
            lf = _log_sigmoid(f_ref[n * t:(n + 1) * t, :] + b_ref[...])
            hi, mid, lo = _split3(lf)
            c_ref[n * t:(n + 1) * t, :] = _dot_nn(tri, hi) + _dot_nn(tri, mid) + _dot_nn(tri, lo) + carry
            carry = carry + jnp.sum(lf, axis=0, keepdims=True)

    return _pcall(body, name=name, in_specs=[VMEM_SPEC, VMEM_SPEC], out_specs=VMEM_SPEC,
                  out_shape=jax.ShapeDtypeStruct((s, 128), F32))(f, b)


def _fox_post(drow, dcol, f, b, name):
    s = f.shape[0]
    t = ATT_BLOCK

    def body(drow_ref, dcol_ref, f_ref, b_ref, df_ref, db_ref):
        tri = (lax.broadcasted_iota(jnp.int32, (t, t), 1) >= lax.broadcasted_iota(jnp.int32, (t, t), 0)).astype(BF16)
        lane = lax.broadcasted_iota(jnp.int32, (t, 128), 1)
        carry = jnp.zeros((1, 128), F32)
        db = jnp.zeros((1, 128), F32)
        for n in reversed(range(s // t)):
            rs = slice(n * t, (n + 1) * t)
            dc = jnp.zeros((t, 128), F32)
            for h in range(N_HEADS):
                dc = jnp.where(lane == h, drow_ref[h, rs, :] - dcol_ref[rs, _head_cols(h)], dc)
            hi, mid, lo = _split3(dc)
            dlogf = _dot_nn(tri, hi) + _dot_nn(tri, mid) + _dot_nn(tri, lo) + carry
            carry = carry + jnp.sum(dc, axis=0, keepdims=True)
            df = dlogf * jax.nn.sigmoid(-(f_ref[rs, :] + b_ref[...]))
            df_ref[rs, :] = df
            db = db + jnp.sum(df, axis=0, keepdims=True)
        db_ref[...] = db

    return _pcall(body, name=name, in_specs=[VMEM_SPEC] * 4, out_specs=[VMEM_SPEC, VMEM_SPEC],
                  out_shape=[jax.ShapeDtypeStruct((s, 128), F32), jax.ShapeDtypeStruct((1, 128), F32)],
                  compiler_params=pltpu.CompilerParams(vmem_limit_bytes=VMEM_LIMIT_BYTES))(drow, dcol, f, b)


def _fox_specs(s):
    t = ATT_BLOCK
    ccol = pl.BlockSpec((HEADS_PER_STEP, t, 1), lambda g, i: (g, i, 0))
    crow = pl.BlockSpec((HEADS_PER_STEP, s // t, 1, t), lambda g, i: (g, 0, 0, 0))
    return ccol, crow


def _fox_fwd(p, c_col, c_row, mixed, name, after=()):
    s = p.shape[0]
    t = ATT_BLOCK

    def body(q_ref, k_ref, v_ref, cc_ref, cr_ref, mixed_in_ref, o_ref, lse_ref, mixed_ref):
        i = pl.program_id(1)
        heads = range(HEADS_PER_STEP)
        q = [q_ref[:, _head_cols(h)].astype(BF16) for h in heads]
        ct = [cc_ref[h] for h in heads]

        def tile(n, carry, diagonal):
            ks = pl.multiple_of(n * t, t)
            out = []
            for h in heads:
                acc, m, l = carry[h]
                kblk = k_ref[pl.ds(ks, t), _head_cols(h)].astype(BF16)
                vblk = v_ref[pl.ds(ks, t), _head_cols(h)].astype(BF16)
                logit = _dot_nt(q[h], kblk) * ATT_SCALE + ct[h] - cr_ref[h, n]
                if diagonal:
                    causal = lax.broadcasted_iota(jnp.int32, (t, t), 1) <= lax.broadcasted_iota(jnp.int32, (t, t), 0)
                    logit = jnp.where(causal, logit, NEG)
                m_new = jnp.maximum(m, jnp.max(logit, axis=1, keepdims=True))
                alpha = jnp.exp(m - m_new)
                pr = jnp.exp(logit - m_new)
                l = alpha * l + jnp.sum(pr, axis=1, keepdims=True)
                out.append((alpha * acc + _dot_nn(pr.astype(BF16), vblk), m_new, l))
            return tuple(out)

        init = tuple((jnp.zeros((t, HEAD), F32), jnp.full((t, 1), NEG, F32), jnp.zeros((t, 1), F32)) for _ in heads)
        carry = tile(i, lax.fori_loop(0, i, lambda n, c: tile(n, c, False), init), True)
        for h in heads:
            acc, m, l = carry[h]
            out = acc / l
            o_ref[:, _head_cols(h)] = out
            mixed_ref[:, _head_cols(h)] = out.astype(BF16)
            lse_ref[h] = m + jnp.log(l)

    qspec, kspec, vspec = _head_specs(s, (2 * N_HEADS, 3 * N_HEADS, 4 * N_HEADS))
    ccol, crow = _fox_specs(s)
    ospec = pl.BlockSpec((t, GROUP_W), lambda g, i: (i, g))
    mspec = pl.BlockSpec((t, GROUP_W), lambda g, i: (i, N_GROUPS + g))
    return _pcall(body, after=after, name=name, grid=(N_GROUPS, s // t),
                  in_specs=[qspec, kspec, vspec, ccol, crow, pl.BlockSpec(memory_space=pl.ANY)],
                  out_specs=[ospec, ccol, mspec],
                  out_shape=[jax.ShapeDtypeStruct((s, HALF), F32), jax.ShapeDtypeStruct((N_HEADS, s, 1), F32),
                             jax.ShapeDtypeStruct(mixed.shape, BF16)],
                  input_output_aliases={5: 2},
                  compiler_params=_params("parallel", "parallel", vmem_limit=ATT_VMEM_LIMIT_BYTES))(p, p, p, c_col, c_row, mixed)


def _fox_bwd(p, c_col, c_row, lse, d_cd, d_out, name):
    s = p.shape[0]
    t = ATT_BLOCK

    def body(q_ref, k_ref, v_ref, cc_ref, cr_ref, lse_ref, do_ref, o_ref, dq_ref, dk_ref, dv_ref, dcol_ref, drow_ref):
        i = pl.program_id(1)

        @pl.when(i == 0)
        def _():
            dk_ref[...] = jnp.zeros_like(dk_ref)
            dv_ref[...] = jnp.zeros_like(dv_ref)
            dcol_ref[...] = jnp.zeros_like(dcol_ref)

        heads = range(HEADS_PER_STEP)
        q = [q_ref[:, _head_cols(h)].astype(BF16) for h in heads]
        do = [do_ref[:, _head_cols(h)].astype(BF16) for h in heads]
        delta = [jnp.sum(do_ref[:, _head_cols(h)] * o_ref[:, _head_cols(h)], axis=1, keepdims=True) for h in heads]
        ct = [cc_ref[h] for h in heads]
        lse_v = [lse_ref[h] for h in heads]
        ones = jnp.ones((t, HEAD), BF16)

        def tile(n, carry, diagonal):
            ks = pl.multiple_of(n * t, t)
            out = []
            for h in heads:
                dq, drow = carry[h]
                kblk = k_ref[pl.ds(ks, t), _head_cols(h)].astype(BF16)
                vblk = v_ref[pl.ds(ks, t), _head_cols(h)].astype(BF16)
                logit = _dot_nt(q[h], kblk) * ATT_SCALE + ct[h] - cr_ref[h, n]
                pr = jnp.exp(logit - lse_v[h])
                if diagonal:
                    causal = lax.broadcasted_iota(jnp.int32, (t, t), 1) <= lax.broadcasted_iota(jnp.int32, (t, t), 0)
                    pr = jnp.where(causal, pr, 0.0)
                ds = pr * (_dot_nt(do[h], vblk) - delta[h])
                dsb = ds.astype(BF16)
                dv_ref[pl.ds(ks, t), _head_cols(h)] += _dot_tn(pr.astype(BF16), do[h])
                dk_ref[pl.ds(ks, t), _head_cols(h)] += _dot_tn(dsb, q[h]) * ATT_SCALE
                dcol_ref[pl.ds(ks, t), _head_cols(h)] += _dot_tn(dsb, ones)
                out.append((dq + _dot_nn(dsb, kblk) * ATT_SCALE,
                            drow + jnp.sum(dsb.astype(F32), axis=1, keepdims=True)))
            return tuple(out)

        init = tuple((jnp.zeros((t, HEAD), F32), jnp.zeros((t, 1), F32)) for _ in heads)
        carry = tile(i, lax.fori_loop(0, i, lambda n, c: tile(n, c, False), init), True)
        for h in heads:
            dq_ref[:, _head_cols(h)] = carry[h][0]
            drow_ref[h] = carry[h][1]

    qspec, kspec, vspec = _head_specs(s, (2 * N_HEADS, 3 * N_HEADS, 4 * N_HEADS))
    ccol, crow = _fox_specs(s)
    dospec = pl.BlockSpec((t, GROUP_W), lambda g, i: (i, N_GROUPS + g))
    blk = pl.BlockSpec((t, GROUP_W), lambda g, i: (i, g))
    whole = pl.BlockSpec((s, GROUP_W), lambda g, i: (0, g))
    shape = jax.ShapeDtypeStruct((s, HALF), F32)
    return _pcall(body, name=name, grid=(N_GROUPS, s // t),
                  in_specs=[qspec, kspec, vspec, ccol, crow, ccol, dospec, blk],
                  out_specs=[blk, whole, whole, whole, ccol],
                  out_shape=[shape, shape, shape, shape, jax.ShapeDtypeStruct((N_HEADS, s, 1), F32)],
                  compiler_params=_params("parallel", "arbitrary", vmem_limit=ATT_VMEM_LIMIT_BYTES))(p, p, p, c_col, c_row, lse, d_cd, d_out)


def _row_tile(rows, cap):
    for t in (256, 128, 64, 32, 16, 8):
        if t <= cap and rows % t == 0:
            return t
    return rows


def _adamw_small(ws, gs, ms, vs, name):
    n = len(ws)
    c1 = 1.0 / (1.0 - ADAM_B1 ** ADAM_STEP)
    c2 = 1.0 / (1.0 - ADAM_B2 ** ADAM_STEP)

    def body(*refs):
        for k in range(n):
            w_ref, g_ref, m_ref, v_ref = (refs[j * n + k] for j in range(4))
            d_ref, nm_ref, nv_ref = (refs[(4 + j) * n + k] for j in range(3))
            gv = g_ref[...]
            nm = ADAM_B1 * m_ref[...] + (1.0 - ADAM_B1) * gv
            nv = ADAM_B2 * v_ref[...] + (1.0 - ADAM_B2) * (gv * gv)
            nm_ref[...] = nm
            nv_ref[...] = nv
            d_ref[...] = -ADAM_LR * ((nm * c1) / (jnp.sqrt(nv * c2) + ADAM_EPS) + ADAM_WD * w_ref[...])

    shapes = [jax.ShapeDtypeStruct(w.shape, F32) for w in ws] * 3
    outs = _pcall(body, name=name, in_specs=[VMEM_SPEC] * (4 * n), out_specs=[VMEM_SPEC] * (3 * n), out_shape=shapes,
                  compiler_params=pltpu.CompilerParams(vmem_limit_bytes=VMEM_LIMIT_BYTES))(*ws, *gs, *ms, *vs)
    return outs[:n], outs[n:2 * n], outs[2 * n:]


def _half_shape(whole_shape, kind):
    if kind == "col":
        return (whole_shape[0] // 2, whole_shape[1] // 4)
    if kind == "row":
        return (whole_shape[0] // 8, whole_shape[1])
    return (whole_shape[1] // 2, whole_shape[2])


def _own_half_spec(whole_shape, kind, tr):
    hr, hc = _half_shape(whole_shape, kind)
    nb = hr // tr
    if kind == "col":
        return pl.BlockSpec((tr, hc), lambda i, pos: (pos[1] * nb + i, pos[0]))
    if kind == "row":
        return pl.BlockSpec((tr, hc), lambda i, pos: ((2 * pos[0] + pos[1]) * nb + i, 0))
    return pl.BlockSpec((None, tr, hc), lambda i, pos: (pos[0], pos[1] * nb + i, 0))


def _sum_partials(pos, grad, landed, kind, name):
    hr, hc = _half_shape(grad.shape, kind)
    tr = _row_tile(hr, 128)

    def body(pos_ref, g_ref, p_ref, o_ref):
        acc = g_ref[...].astype(F32)
        for k in range(N_DEV - 1):
            acc = acc + p_ref[k].astype(F32)
        o_ref[...] = acc

    grid_spec = pltpu.PrefetchScalarGridSpec(
        num_scalar_prefetch=1, grid=(hr // tr,),
        in_specs=[_own_half_spec(grad.shape, kind, tr), pl.BlockSpec((N_DEV - 1, tr, hc), lambda i, pos: (0, i, 0))],
        out_specs=pl.BlockSpec((tr, hc), lambda i, pos: (i, 0)))
    return _pcall(body, name=name, grid_spec=grid_spec, out_shape=jax.ShapeDtypeStruct((hr, hc), F32),
                  compiler_params=_params("parallel"))(pos, grad, landed)


def _adamw_shard(pos, w, g_mine, g_sibling, m, v, name):
    hr, hc = g_mine.shape
    tr = _row_tile(hr, 128)
    nb = hr // tr
    c1 = 1.0 / (1.0 - ADAM_B1 ** ADAM_STEP)
    c2 = 1.0 / (1.0 - ADAM_B2 ** ADAM_STEP)

    def body(pos_ref, w_ref, gm_ref, gs_ref, m_ref, v_ref, g_ref, d_ref, nm_ref, nv_ref):
        mine = (pl.program_id(0) // nb) == pos_ref[1]
        gv = jnp.where(mine, gm_ref[...], gs_ref[...])
        nm = ADAM_B1 * m_ref[...] + (1.0 - ADAM_B1) * gv
        nv = ADAM_B2 * v_ref[...] + (1.0 - ADAM_B2) * (gv * gv)
        g_ref[...] = gv
        nm_ref[...] = nm
        nv_ref[...] = nv
        d_ref[...] = -ADAM_LR * ((nm * c1) / (jnp.sqrt(nv * c2) + ADAM_EPS) + ADAM_WD * w_ref[...])

    full = pl.BlockSpec((tr, hc), lambda i, pos: (i, 0))
    mine_spec = pl.BlockSpec((tr, hc), lambda i, pos: (jnp.clip(i - pos[1] * nb, 0, nb - 1), 0))
    sib_spec = pl.BlockSpec((tr, hc), lambda i, pos: (jnp.clip(i - (1 - pos[1]) * nb, 0, nb - 1), 0))
    grid_spec = pltpu.PrefetchScalarGridSpec(
        num_scalar_prefetch=1, grid=(2 * nb,), in_specs=[full, mine_spec, sib_spec, full, full], out_specs=[full] * 4)
    shape = jax.ShapeDtypeStruct((2 * hr, hc), F32)
    return _pcall(body, name=name, grid_spec=grid_spec, out_shape=[shape] * 4,
                  compiler_params=_params("parallel"))(pos, w, g_mine, g_sibling, m, v)


SC_TILES = 32
SC_LANES = 16
SC_CHUNK_ROWS = 4


def _adamw_sparsecore(w, g_mine, g_sibling, m, v, name):
    hr, cols = g_mine.shape
    rows = 2 * hr
    per_tile = rows // SC_TILES
    assert rows % SC_TILES == 0 and per_tile % SC_CHUNK_ROWS == 0 and hr % SC_CHUNK_ROWS == 0 and cols % SC_LANES == 0
    c1 = 1.0 / (1.0 - ADAM_B1 ** ADAM_STEP)
    c2 = 1.0 / (1.0 - ADAM_B2 ** ADAM_STEP)

    def body(w_hbm, gm_hbm, gs_hbm, m_hbm, v_hbm, g_out, d_out, nm_out, nv_out, w_buf, g_buf, m_buf, v_buf):
        tile = lax.axis_index("sc_tile") * 2 + lax.axis_index("sc_core")
        core = lax.axis_index("c")

        @pl.loop(0, per_tile, step=SC_CHUNK_ROWS)
        def _(r):
            row0 = tile * per_tile + r
            band = pl.ds(row0, SC_CHUNK_ROWS)
            pltpu.sync_copy(w_hbm.at[band, :], w_buf)
            pltpu.sync_copy(m_hbm.at[band, :], m_buf)
            pltpu.sync_copy(v_hbm.at[band, :], v_buf)
            half = row0 // hr
            local = pl.ds(row0 - half * hr, SC_CHUNK_ROWS)

            @pl.when(half == core)
            def _():
                pltpu.sync_copy(gm_hbm.at[local, :], g_buf)

            @pl.when(half != core)
            def _():
                pltpu.sync_copy(gs_hbm.at[local, :], g_buf)

            @pl.loop(0, SC_CHUNK_ROWS)
            def _(i):
                @pl.loop(0, cols, step=SC_LANES)
                def _(j):
                    at = (i, pl.ds(j, SC_LANES))
                    gv = g_buf[at]
                    nm = ADAM_B1 * m_buf[at] + (1.0 - ADAM_B1) * gv
                    nv = ADAM_B2 * v_buf[at] + (1.0 - ADAM_B2) * (gv * gv)
                    m_buf[at] = nm
                    v_buf[at] = nv
                    w_buf[at] = -ADAM_LR * ((nm * c1) / (jnp.sqrt(nv * c2) + ADAM_EPS) + ADAM_WD * w_buf[at])

            pltpu.sync_copy(g_buf, g_out.at[band, :])
            pltpu.sync_copy(w_buf, d_out.at[band, :])
            pltpu.sync_copy(m_buf, nm_out.at[band, :])
            pltpu.sync_copy(v_buf, nv_out.at[band, :])

    shape = jax.ShapeDtypeStruct((rows, cols), F32)
    return pl.kernel(body, name=name, out_type=[shape] * 4,
                     mesh=plsc.VectorSubcoreMesh(core_axis_name="sc_core", subcore_axis_name="sc_tile"),
                     scratch_types=[pltpu.VMEM((SC_CHUNK_ROWS, cols), F32)] * 4)(w, g_mine, g_sibling, m, v)


def _place_shard(pos, shard, kind, name, after=()):
    rows, cols = shard.shape
    tr = _row_tile(rows, 256)
    nb = rows // tr
    if kind == "col":
        out_spec = pl.BlockSpec((tr, cols), lambda i, pos: (i, pos[0]))
    elif kind == "row":
        out_spec = pl.BlockSpec((tr, cols), lambda i, pos: (pos[0] * nb + i, 0))
    else:
        out_spec = pl.BlockSpec((None, tr, cols), lambda i, pos: (pos[0], i, 0))

    def body(pos_ref, s_ref, *rest):
        rest[-1][...] = s_ref[...].astype(BF16)

    grid_spec = pltpu.PrefetchScalarGridSpec(
        num_scalar_prefetch=1, grid=(nb,),
        in_specs=[pl.BlockSpec((tr, cols), lambda i, pos: (i, 0))] + [pl.BlockSpec(memory_space=pl.ANY)] * len(after),
        out_specs=out_spec)
    return _pcall(body, name=name, grid_spec=grid_spec,
                  out_shape=jax.ShapeDtypeStruct(_whole_shape(shard.shape, kind), BF16),
                  compiler_params=_params("parallel"))(pos, shard, *after)


N_DEV = 8
RELATIONS = [(r >> 2 & 1, r >> 1 & 1, r & 1) for r in range(1, N_DEV)]


def _position():
    return lax.axis_index("x"), lax.axis_index("y"), lax.axis_index("c")


def _related(pos, rel):
    return tuple(1 - p if f else p for p, f in zip(pos, rel))


def _index(pos):
    return 4 * pos[0] + 2 * pos[1] + pos[2]


def _window(ref, kind, pos):
    px, py, pc = pos
    j = 2 * px + py
    if kind == "col":
        r, c = ref.shape
        return ref.at[pl.ds(pc * (r // 2), r // 2), pl.ds(pl.multiple_of(j * (c // 4), 128), c // 4)]
    if kind == "row":
        rj = ref.shape[0] // 4
        return ref.at[pl.ds(j * rj + pc * (rj // 2), rj // 2), :]
    r = ref.shape[1]
    return ref.at[j, pl.ds(pc * (r // 2), r // 2), :]


def _whole_shape(shard_shape, kind):
    r, c = shard_shape
    return {"col": (r, 4 * c), "row": (4 * r, c), "maj": (4, r, c)}[kind]


SEM_SPEC = pl.BlockSpec(memory_space=pltpu.SEMAPHORE)
ANY_SPEC = pl.BlockSpec(memory_space=pl.ANY)
DATAFLOW = pltpu.SideEffectType.DATAFLOW_SIDE_EFFECTING
TOKEN = jax.ShapeDtypeStruct((8, 128), F32)


def _hbm(a):
    return pltpu.with_memory_space_constraint(a, pltpu.HBM)


def _chips(x, y):
    return [(1 - x, y), (x, 1 - y), (1 - x, 1 - y)]


def _split_start(body, name, buffers, n_sems, after=()):
    n = len(buffers)

    def wrapped(*refs):
        body(refs[:n], refs[n], refs[n + 1])
        refs[-1][...] = jnp.zeros_like(refs[-1])

    outs = _pcall(
        wrapped, after=after, name=name, in_specs=[HBM_SPEC] * n,
        out_specs=[SEM_SPEC, SEM_SPEC] + [HBM_SPEC] * n + [VMEM_SPEC],
        out_shape=[pltpu.SemaphoreType.DMA(n_sems), pltpu.SemaphoreType.DMA(n_sems)]
        + [pltpu.HBM(b.shape, b.dtype) for b in buffers] + [TOKEN],
        input_output_aliases={i: 2 + i for i in range(n)},
        compiler_params=pltpu.CompilerParams(has_side_effects=DATAFLOW))(*[_hbm(b) for b in buffers])
    return outs[0], outs[1], list(outs[2:2 + n]), outs[2 + n]


def _split_wait(body, name, buffers, send_sems, recv_sems, after):
    n = len(buffers)
    after = list(after) if isinstance(after, (list, tuple)) else [after]

    def wrapped(*refs):
        body(refs[:n], refs[n], refs[n + 1])

    outs = _pcall(
        wrapped, name=name, in_specs=[HBM_SPEC] * n + [SEM_SPEC, SEM_SPEC] + [ANY_SPEC] * len(after),
        out_specs=[HBM_SPEC] * n, out_shape=[pltpu.HBM(b.shape, b.dtype) for b in buffers],
        input_output_aliases={i: i for i in range(n)},
        compiler_params=pltpu.CompilerParams(has_side_effects=DATAFLOW))(*buffers, send_sems, recv_sems, *after)
    return list(outs)


def _gather_start(wholes, kinds, name, after=()):
    def body(w_refs, send_sems, recv_sems):
        x, y, c = _position()
        for w, ref in enumerate(w_refs):
            mine = _window(ref, kinds[w], (x, y, c))
            for k, chip in enumerate(_chips(x, y)):
                pltpu.make_async_remote_copy(src_ref=mine, dst_ref=mine, send_sem=send_sems.at[3 * w + k],
                                             recv_sem=recv_sems.at[3 * w + k], device_id=(*chip, c),
                                             device_id_type=MESH).start()

    return _split_start(body, name, wholes, (3 * len(wholes),), after)


def _gather_forward(wholes, kinds, send1, recv1, after, name):
    n = len(wholes)

    def wrapped(*refs):
        w_refs, s1, r1, s2, r2 = refs[:n], refs[n], refs[n + 1], refs[n + 3], refs[n + 4]
        x, y, c = _position()
        for k, chip in enumerate(_chips(x, y)):
            for w, ref in enumerate(w_refs):
                theirs = _window(ref, kinds[w], (*chip, c))
                pltpu.make_async_remote_copy(src_ref=theirs, dst_ref=theirs, send_sem=s1.at[3 * w + k],
                                             recv_sem=r1.at[3 * w + k], device_id=(*chip, c),
                                             device_id_type=MESH).wait_recv()
                pltpu.make_async_remote_copy(src_ref=theirs, dst_ref=theirs, send_sem=s2.at[3 * w + k],
                                             recv_sem=r2.at[3 * w + k], device_id=(x, y, 1 - c),
                                             device_id_type=MESH).start()
        for w, ref in enumerate(w_refs):
            mine = _window(ref, kinds[w], (x, y, c))
            for k, chip in enumerate(_chips(x, y)):
                pltpu.make_async_remote_copy(src_ref=mine, dst_ref=mine, send_sem=s1.at[3 * w + k],
                                             recv_sem=r1.at[3 * w + k], device_id=(*chip, c),
                                             device_id_type=MESH).wait_send()
        refs[-1][...] = jnp.zeros_like(refs[-1])

    outs = _pcall(
        wrapped, name=name, in_specs=[HBM_SPEC] * n + [SEM_SPEC, SEM_SPEC, ANY_SPEC],
        out_specs=[SEM_SPEC, SEM_SPEC] + [HBM_SPEC] * n + [VMEM_SPEC],
        out_shape=[pltpu.SemaphoreType.DMA((3 * n,)), pltpu.SemaphoreType.DMA((3 * n,))]
        + [pltpu.HBM(b.shape, b.dtype) for b in wholes] + [TOKEN],
        input_output_aliases={i: 2 + i for i in range(n)},
        compiler_params=pltpu.CompilerParams(has_side_effects=DATAFLOW))(*wholes, send1, recv1, after)
    return outs[0], outs[1], list(outs[2:2 + n]), outs[2 + n]


def _gather_finish(wholes, kinds, send2, recv2, after, name):
    def body(w_refs, s2, r2):
        x, y, c = _position()
        for k, chip in enumerate(_chips(x, y)):
            for w, ref in enumerate(w_refs):
                sent = _window(ref, kinds[w], (*chip, c))
                got = _window(ref, kinds[w], (*chip, 1 - c))
                pltpu.make_async_remote_copy(src_ref=sent, dst_ref=got, send_sem=s2.at[3 * w + k],
                                             recv_sem=r2.at[3 * w + k], device_id=(x, y, 1 - c),
                                             device_id_type=MESH).wait()

    return _split_wait(body, name, wholes, send2, recv2, after)


def _gather_small_start(small, after):
    def body(refs, send_sems, recv_sems):
        s_ref, land_ref = refs
        x, y, c = _position()
        for k, chip in enumerate(_chips(x, y)):
            pltpu.make_async_remote_copy(src_ref=s_ref, dst_ref=land_ref.at[2 * x + y], send_sem=send_sems.at[k],
                                         recv_sem=recv_sems.at[k], device_id=(*chip, c), device_id_type=MESH).start()

    return _split_start(body, "gather_small_start", [small, lax.empty((4,) + small.shape, small.dtype)], (3,), after)


def _gather_small_finish(send, recv, thru, after):
    def body(refs, send_sems, recv_sems):
        s_ref, land_ref = refs
        x, y, c = _position()
        for k, chip in enumerate(_chips(x, y)):
            pltpu.make_async_remote_copy(src_ref=s_ref, dst_ref=land_ref.at[2 * chip[0] + chip[1]],
                                         send_sem=send_sems.at[k], recv_sem=recv_sems.at[k], device_id=(*chip, c),
                                         device_id_type=MESH).wait()

    return _split_wait(body, "gather_small_finish", thru, send, recv, after)[1]


def _scatter_copies(g_refs, land_refs, kinds, send_sems, recv_sems):
    me = _position()
    copies = []
    for k, rel in enumerate(RELATIONS):
        peer = _related(me, rel)
        for w, (g_ref, land_ref) in enumerate(zip(g_refs, land_refs)):
            copies.append(pltpu.make_async_remote_copy(
                src_ref=_window(g_ref, kinds[w], peer), dst_ref=land_ref.at[k],
                send_sem=send_sems.at[7 * w + k], recv_sem=recv_sems.at[7 * w + k], device_id=peer,
                device_id_type=MESH))
    return copies


def _scatter_start(grads, kinds, name):
    n = len(grads)
    lands = [lax.empty((N_DEV - 1,) + _half_shape(g.shape, kd), g.dtype) for g, kd in zip(grads, kinds)]

    def body(refs, send_sems, recv_sems):
        for cp in _scatter_copies(refs[:n], refs[n:], kinds, send_sems, recv_sems):
            cp.start()

    send, recv, thru, token = _split_start(body, name, list(grads) + lands, ((N_DEV - 1) * n,))
    return send, recv, thru[:n], thru[n:], token


def _scatter_wait(grads, lands, kinds, send, recv, after, name):
    n = len(grads)

    def body(refs, send_sems, recv_sems):
        for cp in _scatter_copies(refs[:n], refs[n:], kinds, send_sems, recv_sems):
            cp.wait()

    out = _split_wait(body, name, list(grads) + list(lands), send, recv, after)
    return out[:n], out[n:]


def _swap_start(halves, name):
    n = len(halves)
    lands = [lax.empty(h.shape, h.dtype) for h in halves]

    def body(refs, send_sems, recv_sems):
        x, y, c = _position()
        for w in range(n):
            pltpu.make_async_remote_copy(src_ref=refs[w], dst_ref=refs[n + w], send_sem=send_sems.at[w],
                                         recv_sem=recv_sems.at[w], device_id=(x, y, 1 - c), device_id_type=MESH).start()

    send, recv, thru, token = _split_start(body, name, list(halves) + lands, (n,))
    return send, recv, thru[:n], thru[n:], token


def _swap_wait(halves, lands, send, recv, after, name):
    n = len(halves)

    def body(refs, send_sems, recv_sems):
        x, y, c = _position()
        for w in range(n):
            pltpu.make_async_remote_copy(src_ref=refs[w], dst_ref=refs[n + w], send_sem=send_sems.at[w],
                                         recv_sem=recv_sems.at[w], device_id=(x, y, 1 - c), device_id_type=MESH).wait()

    out = _split_wait(body, name, list(halves) + list(lands), send, recv, after)
    return out[:n], out[n:]


def _allreduce_small(v, after=()):
    rows = v.shape[0]

    def body(v_ref, o_ref, recv_ref, send_sems, recv_sems):
        me = _position()
        recv_ref[_index(me)] = v_ref[...]
        sends = []
        for k, rel in enumerate(RELATIONS):
            peer = _related(me, rel)
            cp = pltpu.make_async_remote_copy(
                src_ref=v_ref, dst_ref=recv_ref.at[_index(me)],
                send_sem=send_sems.at[k], recv_sem=recv_sems.at[k], device_id=peer, device_id_type=MESH)
            cp.start()
            sends.append(cp)
        for k, rel in enumerate(RELATIONS):
            peer = _related(me, rel)
            pltpu.make_async_remote_copy(
                src_ref=v_ref, dst_ref=recv_ref.at[_index(peer)],
                send_sem=send_sems.at[k], recv_sem=recv_sems.at[k], device_id=peer, device_id_type=MESH).wait_recv()
        for cp in sends:
            cp.wait_send()
        acc = recv_ref[0]
        for k in range(1, N_DEV):
            acc = acc + recv_ref[k]
        o_ref[...] = acc

    return _pcall(body, after=after, name="allreduce_small", in_specs=[VMEM_SPEC], out_specs=VMEM_SPEC,
                  out_shape=jax.ShapeDtypeStruct((rows, 128), F32),
                  scratch_shapes=[pltpu.VMEM((N_DEV, rows, 128), F32), pltpu.SemaphoreType.DMA((7,)),
                                  pltpu.SemaphoreType.DMA((7,))],
                  compiler_params=pltpu.CompilerParams(vmem_limit_bytes=VMEM_LIMIT_BYTES))(v)


def _small_copies(refs, send_sems, recv_sems):
    v_ref, land_ref = refs
    me = _position()
    return [pltpu.make_async_remote_copy(src_ref=v_ref, dst_ref=land_ref.at[_index(me)], send_sem=send_sems.at[k],
                                         recv_sem=recv_sems.at[k], device_id=_related(me, rel), device_id_type=MESH)
            for k, rel in enumerate(RELATIONS)]


def _small_wait_copies(refs, send_sems, recv_sems):
    v_ref, land_ref = refs
    me = _position()
    return [pltpu.make_async_remote_copy(src_ref=v_ref, dst_ref=land_ref.at[_index(_related(me, rel))],
                                         send_sem=send_sems.at[k], recv_sem=recv_sems.at[k],
                                         device_id=_related(me, rel), device_id_type=MESH)
            for k, rel in enumerate(RELATIONS)]


def _small_reduce_start(v, name, after):
    def body(refs, send_sems, recv_sems):
        for cp in _small_copies(refs, send_sems, recv_sems):
            cp.start()

    send, recv, thru, token = _split_start(body, name, [v, lax.empty((N_DEV,) + v.shape, v.dtype)], (N_DEV - 1,), after)
    return send, recv, thru, token


def _small_reduce_finish(me, send, recv, thru, after, name):
    def body(refs, send_sems, recv_sems):
        for cp in _small_wait_copies(refs, send_sems, recv_sems):
            cp.wait()

    v, landed = _split_wait(body, name + "_wait", thru, send, recv, after)

    def add(me_ref, v_ref, land_ref, o_ref):
        acc = jnp.where(me_ref[0] == 0, v_ref[...], land_ref[0])
        for d in range(1, N_DEV):
            acc = acc + jnp.where(me_ref[0] == d, v_ref[...], land_ref[d])
        o_ref[...] = acc

    return _pcall(add, name=name + "_sum",
                  in_specs=[pl.BlockSpec(memory_space=pltpu.SMEM), VMEM_SPEC, VMEM_SPEC], out_specs=VMEM_SPEC,
                  out_shape=jax.ShapeDtypeStruct(v.shape, F32),
                  compiler_params=pltpu.CompilerParams(vmem_limit_bytes=VMEM_LIMIT_BYTES))(me, v, landed)


def _pack(arrays):
    flat = []
    for a in arrays:
        a = a.reshape(-1)
        flat.append(jnp.pad(a, (0, -a.shape[0] % 128)))
    flat = jnp.concatenate(flat)
    flat = jnp.pad(flat, (0, -flat.shape[0] % 1024))
    return flat.reshape(-1, 128)


def _unpack(packed, shapes):
    flat = packed.reshape(-1)
    out, at = [], 0
    for shp in shapes:
        size = 1
        for d in shp:
            size *= d
        out.append(flat[at:at + size].reshape(shp))
        at += size + (-size % 128)
    return out


WEIGHTS = ['l0_mix_norm_g', 'l0_w_in', 'l0_sc_conv_w', 'l0_w_out', 'l0_ffn_norm_g', 'l0_ffn_up', 'l0_ffn_conv_w',
           'l0_ffn_down', 'l1_mix_norm_g', 'l1_w_in', 'l1_fox_b_f', 'l1_sg_w', 'l1_sg_b', 'l1_sg_norm_g', 'l1_w_out',
           'l1_ffn_norm_g', 'l1_ffn_up', 'l1_ffn_conv_w', 'l1_ffn_down', 'final_norm_g']
BIG = {'l0_w_in': 'col', 'l0_w_out': 'row', 'l0_ffn_up': 'col', 'l0_ffn_down': 'row',
       'l1_w_in': 'maj', 'l1_w_out': 'row', 'l1_ffn_up': 'col', 'l1_ffn_down': 'row'}
GATHER_GROUPS = [['l0_w_in'], ['l0_w_out'], ['l0_ffn_up'], ['l0_ffn_down'], ['l1_w_in'], ['l1_w_out'],
                 ['l1_ffn_up'], ['l1_ffn_down']]
CONV = ['l0_sc_conv_w', 'l0_ffn_conv_w', 'l1_ffn_conv_w']
SMALL = [n for n in WEIGHTS if n not in BIG]
SC_ADAMW = ('l1_ffn_down', 'l1_ffn_up', 'l1_w_out')
LATE_SMALL = ['l0_sc_conv_w', 'l0_mix_norm_g']
IN_CD = 5 * HALF + N_HEADS


def _ffn_forward(x, g, get_up, behind_act, get_down, conv_w, tag):
    h = _rmsnorm_fwd(x, g, tag + "_norm")
    u = _matmul(h, get_up(h), "nn", F32, tag + "_up")
    f = _ffn_act_fwd(u, conv_w, tag + "_act", after=behind_act(u))
    w_down, tokens = get_down(f)
    return _matmul(f, w_down, "nn", F32, tag + "_down", res=x, after=tokens), (h, u, f)


def _ffn_backward(x, g, w_up, conv_w, w_down, saved, d_out, send_up, send_down, tag):
    h, u, f = saved
    dw_down = _matmul(f, d_out, "tn", BF16, tag + "_dwdown")
    d_f = _matmul(d_out, w_down, "nt", F32, tag + "_df", after=[send_down(dw_down)])
    du, dcw_gate, dcw_up = _ffn_act_bwd(u, conv_w, d_f, tag + "_dact")
    dw_up = _matmul(h, du, "tn", BF16, tag + "_dwup")
    dh = _matmul(du, w_up, "nt", F32, tag + "_dh", after=[send_up(dw_up)])
    dx, dg = _rmsnorm_bwd(x, g, dh, d_out, tag + "_dnorm")
    return dx, dg, jnp.concatenate([dcw_gate, dcw_up], axis=1)


def kernel(x, l0_mix_norm_g, l0_w_in, l0_sc_conv_w, l0_w_out, l0_ffn_norm_g, l0_ffn_up, l0_ffn_conv_w, l0_ffn_down, l1_mix_norm_g, l1_w_in, l1_fox_b_f, l1_sg_w, l1_sg_b, l1_sg_norm_g, l1_w_out, l1_ffn_norm_g, l1_ffn_up, l1_ffn_conv_w, l1_ffn_down, final_norm_g, loss_target, m_l0_mix_norm_g, m_l0_w_in, m_l0_sc_conv_w, m_l0_w_out, m_l0_ffn_norm_g, m_l0_ffn_up, m_l0_ffn_conv_w, m_l0_ffn_down, m_l1_mix_norm_g, m_l1_w_in, m_l1_fox_b_f, m_l1_sg_w, m_l1_sg_b, m_l1_sg_norm_g, m_l1_w_out, m_l1_ffn_norm_g, m_l1_ffn_up, m_l1_ffn_conv_w, m_l1_ffn_down, m_final_norm_g, v_l0_mix_norm_g, v_l0_w_in, v_l0_sc_conv_w, v_l0_w_out, v_l0_ffn_norm_g, v_l0_ffn_up, v_l0_ffn_conv_w, v_l0_ffn_down, v_l1_mix_norm_g, v_l1_w_in, v_l1_fox_b_f, v_l1_sg_w, v_l1_sg_b, v_l1_sg_norm_g, v_l1_w_out, v_l1_ffn_norm_g, v_l1_ffn_up, v_l1_ffn_conv_w, v_l1_ffn_down, v_final_norm_g):
    given = (l0_mix_norm_g, l0_w_in, l0_sc_conv_w, l0_w_out, l0_ffn_norm_g, l0_ffn_up, l0_ffn_conv_w, l0_ffn_down, l1_mix_norm_g, l1_w_in, l1_fox_b_f, l1_sg_w, l1_sg_b, l1_sg_norm_g, l1_w_out, l1_ffn_norm_g, l1_ffn_up, l1_ffn_conv_w, l1_ffn_down, final_norm_g)
    given_m = (m_l0_mix_norm_g, m_l0_w_in, m_l0_sc_conv_w, m_l0_w_out, m_l0_ffn_norm_g, m_l0_ffn_up, m_l0_ffn_conv_w, m_l0_ffn_down, m_l1_mix_norm_g, m_l1_w_in, m_l1_fox_b_f, m_l1_sg_w, m_l1_sg_b, m_l1_sg_norm_g, m_l1_w_out, m_l1_ffn_norm_g, m_l1_ffn_up, m_l1_ffn_conv_w, m_l1_ffn_down, m_final_norm_g)
    given_v = (v_l0_mix_norm_g, v_l0_w_in, v_l0_sc_conv_w, v_l0_w_out, v_l0_ffn_norm_g, v_l0_ffn_up, v_l0_ffn_conv_w, v_l0_ffn_down, v_l1_mix_norm_g, v_l1_w_in, v_l1_fox_b_f, v_l1_sg_w, v_l1_sg_b, v_l1_sg_norm_g, v_l1_w_out, v_l1_ffn_norm_g, v_l1_ffn_up, v_l1_ffn_conv_w, v_l1_ffn_down, v_final_norm_g)
    wt = dict(zip(WEIGHTS, given))
    mom = dict(zip(WEIGHTS, given_m))
    var = dict(zip(WEIGHTS, given_v))
    s = x.shape[1]
    t = ATT_BLOCK
    x0, target = x[0], loss_target[0]
    chip = 2 * lax.axis_index("x") + lax.axis_index("y")

    pos = jnp.stack([chip, lax.axis_index("c")]).astype(jnp.int32)

    conv_shard = jnp.concatenate([wt[n] for n in CONV], axis=1)
    gathers, token = [], ()
    for gi, names in enumerate(GATHER_GROUPS):
        placed = [_place_shard(pos, wt[n], BIG[n], "place_" + n, token) for n in names]
        send, recv, thru, tok = _gather_start(placed, [BIG[n] for n in names], "gather_start_%d" % gi, token)
        gathers.append((send, recv, thru))
        if gi == 0:
            conv_started = _gather_small_start(conv_shard, [tok])
            tok = conv_started[3]
        token = [tok]
    token = token[0]
    full = {}
    conv_full = {}

    def finish_conv(after):
        landed = _gather_small_finish(*conv_started[:3], after)
        conv_all = lax.dynamic_update_slice(landed, conv_shard[None], (chip, 0, 0))
        at = 0
        for n in CONV:
            cw = wt[n].shape[1]
            conv_full[n] = jnp.transpose(conv_all[:, :, at:at + cw], (1, 0, 2)).reshape(3, 4 * cw)
            at += cw

    def forward_gather(gi, after):
        send, recv, thru = gathers[gi]
        kinds = [BIG[n] for n in GATHER_GROUPS[gi]]
        gathers[gi] = _gather_forward(thru, kinds, send, recv, after, "gather_forward_%d" % gi)
        return gathers[gi][3]

    def finish_gather(gi, after):
        send, recv, thru, tok = gathers[gi]
        names = GATHER_GROUPS[gi]
        wholes = _gather_finish(thru, [BIG[n] for n in names], send, recv, tok if after is None else after,
                                "gather_finish_%d" % gi)
        full.update(zip(names, wholes))

    def vec(name):
        return wt[name].reshape(1, -1)

    h0 = _rmsnorm_fwd(x0, vec('l0_mix_norm_g'), "l0_mix_norm", after=[token])
    forward_gather(0, h0)
    finish_gather(0, None)
    p0 = _matmul(h0, full['l0_w_in'], "nn", F32, "l0_in")
    a_out, sb_carries = _sb_fwd(p0, "l0_sb", after=[forward_gather(1, p0)])
    finish_gather(1, a_out)
    finish_conv(p0)
    ab0 = _sc_fwd(p0, conv_full['l0_sc_conv_w'], a_out, "l0_sc")
    x1 = _matmul(ab0, full['l0_w_out'], "nn", F32, "l0_out", res=x0, after=[forward_gather(2, ab0)])

    def ffn_weights(up_group, next_group):
        def get_up(h):
            finish_gather(up_group, h)
            return full[GATHER_GROUPS[up_group][0]]

        def behind_act(u):
            return [forward_gather(up_group + 1, u)]

        def get_down(f):
            finish_gather(up_group + 1, f)
            return full[GATHER_GROUPS[up_group + 1][0]], ([forward_gather(next_group, f)] if next_group else ())

        return get_up, behind_act, get_down

    x2, ffn0_saved = _ffn_forward(x1, vec('l0_ffn_norm_g'), *ffn_weights(2, 4), conv_full['l0_ffn_conv_w'], "l0_ffn")
    h2 = _rmsnorm_fwd(x2, vec('l1_mix_norm_g'), "l1_mix_norm")
    finish_gather(4, h2)
    w_in1 = jnp.transpose(full['l1_w_in'], (1, 0, 2)).reshape(D_MODEL, IN_CD)
    w_in1_main = w_in1[:, :5 * HALF]
    w_in1_f = jnp.pad(w_in1[:, 5 * HALF:], ((0, 0), (0, 128 - N_HEADS)))
    p1 = _matmul(h2, w_in1_main, "nn", F32, "l1_in")
    f_logit = _matmul(h2, w_in1_f, "nn", F32, "l1_in_f", after=[forward_gather(5, p1)])
    b_f = jnp.pad(wt['l1_fox_b_f'], (0, 128 - N_HEADS)).reshape(1, 128)
    c_heads = _fox_prep(f_logit, b_f, "l1_fox_prep")[:, :N_HEADS].T
    c_col = c_heads[:, :, None]
    c_row = c_heads.reshape(N_HEADS, s // t, 1, t)
    sg_bias = jnp.repeat(wt['l1_sg_b'].T, HEAD, axis=1)
    sg_gain = vec('l1_sg_norm_g')
    c_out = _sg_fwd(p1, wt['l1_sg_w'], sg_bias, sg_gain, "l1_sg")
    d_out, lse, cd1 = _fox_fwd(p1, c_col, c_row, c_out, "l1_fox", after=[forward_gather(6, c_out)])
    finish_gather(5, d_out)
    x3 = _matmul(cd1, full['l1_w_out'], "nn", F32, "l1_out", res=x2)
    x4, ffn1_saved = _ffn_forward(x3, vec('l1_ffn_norm_g'), *ffn_weights(6, None), conv_full['l1_ffn_conv_w'], "l1_ffn")
    dx4, dg_final, loss_part = _loss_head(x4, vec('final_norm_g'), target, "loss_head")

    grads = {'final_norm_g': dg_final}
    scatters = []
    shard_grads, delta, new_m, new_v, swaps = {}, {}, {}, {}, {}

    def reduce_group(gi, after):
        names, send, recv, thru, lands = scatters[gi]
        kinds = [BIG[n] for n in names]
        g_thru, landed = _scatter_wait(thru, lands, kinds, send, recv, after, "scatter_wait_%d" % gi)
        halves = [_sum_partials(pos, g, ld, kd, "sum_" + n) for n, g, ld, kd in zip(names, g_thru, landed, kinds)]
        s_send, s_recv, h_thru, s_lands, tok = _swap_start(halves, "swap_start_%d" % gi)
        swaps[gi] = (names, s_send, s_recv, h_thru, s_lands)
        return tok

    def update_group(gi, after):
        names, s_send, s_recv, h_thru, s_lands = swaps[gi]
        mine, theirs = _swap_wait(h_thru, s_lands, s_send, s_recv, after, "swap_wait_%d" % gi)
        for n, gm, gs in zip(names, mine, theirs):
            if n in SC_ADAMW:
                shard_grads[n], delta[n], new_m[n], new_v[n] = _adamw_sparsecore(wt[n], gm, gs, mom[n], var[n],
                                                                                 "adamw_sc_" + n)
            else:
                shard_grads[n], delta[n], new_m[n], new_v[n] = _adamw_shard(pos, wt[n], gm, gs, mom[n], var[n],
                                                                            "adamw_" + n)
        return [delta[n] for n in names if n not in SC_ADAMW] or [mine[0]]

    def send_grads(names):
        def start(*group):
            send, recv, thru, lands, tok = _scatter_start(list(group), [BIG[n] for n in names],
                                                          "scatter_start_%d" % len(scatters))
            scatters.append((names, send, recv, thru, lands))
            return tok
        return start

    dx3, grads['l1_ffn_norm_g'], grads['l1_ffn_conv_w'] = _ffn_backward(
        x3, vec('l1_ffn_norm_g'), full['l1_ffn_up'], conv_full['l1_ffn_conv_w'], full['l1_ffn_down'], ffn1_saved, dx4,
        send_grads(['l1_ffn_up']), send_grads(['l1_ffn_down']), "l1_ffn")
    dw_out1 = _matmul(cd1, dx3, "tn", BF16, "l1_dwout")
    d_cd = _matmul(dx3, full['l1_w_out'], "nt", F32, "l1_dcd")
    du, dv, grads['l1_sg_w'], db_sg, grads['l1_sg_norm_g'] = _sg_bwd(p1, wt['l1_sg_w'], sg_bias, sg_gain, d_cd, "l1_dsg")
    grads['l1_sg_b'] = db_sg[:, :N_HEADS].T
    dq, dk, dvv, dcol, drow = _fox_bwd(p1, c_col, c_row, lse, d_cd, d_out, "l1_dfox")
    d_f_logit, d_b_f = _fox_post(drow, dcol, f_logit, b_f, "l1_fox_post")
    grads['l1_fox_b_f'] = d_b_f[0, :N_HEADS]
    dp1 = jnp.concatenate([a.astype(BF16) for a in (du, dv, dq, dk, dvv)], axis=1)
    dw_main = _matmul(h2, dp1, "tn", BF16, "l1_dwin")
    dw_f = _matmul(h2, d_f_logit, "tn", BF16, "l1_dwin_f")
    dw_in1 = jnp.concatenate([dw_main, dw_f[:, :N_HEADS]], axis=1)
    dw_in1 = jnp.transpose(dw_in1.reshape(D_MODEL, 4, IN_CD // 4), (1, 0, 2))
    dh2 = _matmul(dp1, w_in1_main, "nt", F32, "l1_dh", after=[send_grads(['l1_w_out', 'l1_w_in'])(dw_out1, dw_in1)])
    dh2 = _matmul(d_f_logit, w_in1_f, "nt", F32, "l1_dh_f", res=dh2)
    dx2, grads['l1_mix_norm_g'] = _rmsnorm_bwd(x2, vec('l1_mix_norm_g'), dh2, dx3, "l1_dmix_norm")
    dx1, grads['l0_ffn_norm_g'], grads['l0_ffn_conv_w'] = _ffn_backward(
        x1, vec('l0_ffn_norm_g'), full['l0_ffn_up'], conv_full['l0_ffn_conv_w'], full['l0_ffn_down'], ffn0_saved, dx2,
        send_grads(['l0_ffn_up']), send_grads(['l0_ffn_down']), "l0_ffn")
    early_names = [n for n in SMALL if n not in LATE_SMALL]
    early = _small_reduce_start(_pack([grads[n] for n in early_names] + [loss_part]), "small_start", [dx1])
    swapped = reduce_group(2, reduce_group(1, reduce_group(0, early[3])))
    dw_out0 = _matmul(ab0, dx1, "tn", BF16, "l0_dwout", after=[swapped])
    d_ab = _matmul(dx1, full['l0_w_out'], "nt", F32, "l0_dab", after=[send_grads(['l0_w_out'])(dw_out0)])
    layer1_updates = update_group(2, update_group(1, update_group(0, d_ab)))
    dq0, dk0, dv0 = _sb_bwd(p0, d_ab, sb_carries, "l0_dsb")
    dgb, dgc, dhin, grads['l0_sc_conv_w'] = _sc_bwd(p0, conv_full['l0_sc_conv_w'], d_ab, "l0_dsc")
    dp0 = jnp.concatenate([a.astype(BF16) for a in (dq0, dk0, dv0, dgb, dgc, dhin)], axis=1)
    dw_in0 = _matmul(h0, dp0, "tn", BF16, "l0_dwin")
    dh0 = _matmul(dp0, full['l0_w_in'], "nt", F32, "l0_dh", after=[send_grads(['l0_w_in'])(dw_in0)])
    dx0, grads['l0_mix_norm_g'] = _rmsnorm_bwd(x0, vec('l0_mix_norm_g'), dh0, dx1, "l0_dmix_norm")

    after = reduce_group(5, reduce_group(4, reduce_group(3, [dx0] + layer1_updates)))
    after = update_group(5, update_group(4, update_group(3, after)))
    after = reduce_group(6, after)
    def small_shapes(names):
        return [conv_full[n].shape if n in CONV else wt[n].shape for n in names]

    me = (2 * chip + lax.axis_index("c")).astype(jnp.int32).reshape(1)
    early_all = _small_reduce_finish(me, early[0], early[1], early[2], after, "small_early")
    early_sums = _unpack(early_all, small_shapes(early_names) + [loss_part.shape])
    loss = early_sums[-1][0, 0]
    late_all = _allreduce_small(_pack([grads[n] for n in LATE_SMALL]), [early_all])
    small_sums = dict(zip(early_names + LATE_SMALL, early_sums[:-1] + _unpack(late_all, small_shapes(LATE_SMALL))))
    for n in SMALL:
        g = small_sums[n]
        shard_grads[n] = lax.dynamic_slice_in_dim(g, chip * wt[n].shape[1], wt[n].shape[1], axis=1) if n in CONV else g
    update_group(6, late_all)
    def flat2d(a):
        return a.reshape(-1, a.shape[-1])

    small_out = _adamw_small(*[[flat2d(src[n]) for n in SMALL] for src in (wt, shard_grads, mom, var)], "adamw_small")
    for out, arrays in zip((delta, new_m, new_v), small_out):
        out.update((n, a.reshape(wt[n].shape)) for n, a in zip(SMALL, arrays))

    return (loss, dx0[None], *[shard_grads[n] for n in WEIGHTS], *[delta[n] for n in WEIGHTS],
            *[new_m[n] for n in WEIGHTS], *[new_v[n] for n in WEIGHTS])
```
